```python
import jax
import jax.numpy as jnp
from jax import lax
import numpy as np

D_MODEL = 1024
BATCH = 4
SEQ = 4096
DEPTH = 2

GRID_W = 64
CTX_LEN = 256

NA_HEADS = 8
NA_HEAD_DIM = 64
NA_WIDTH = NA_HEADS * NA_HEAD_DIM
NA_WIN_ROWS = 8
NA_WIN_COLS = 16
ROPE_BASE = 10000.0
SC_WIDTH = 512
SC_CONV = 3
LRU_WIDTH = 512
LRU_BLOCKS = 8
LRU_BLOCK_DIM = LRU_WIDTH // LRU_BLOCKS
LRU_CONV = 4
LRU_C = 8.0
N_BRANCHES = 3
SPLIT_SIZES = (NA_WIDTH, NA_WIDTH, NA_WIDTH, SC_WIDTH, SC_WIDTH, SC_WIDTH,
               LRU_WIDTH, LRU_WIDTH, N_BRANCHES * D_MODEL)
P_TOTAL = sum(SPLIT_SIZES)
N_EXPERTS = 32
TOP_K = 4
D_EXPERT = D_MODEL
SWIGLU_LIMIT = 7.0
SWIGLU_ALPHA = 1.702
MOE_BLOCK = 128
LN_EPS = 1e-5
MOD_SCALE = 0.5
DEEPNORM_ALPHA = (2 * DEPTH) ** 0.25
DEEPNORM_BETA = (8 * DEPTH) ** -0.25

kernel_name = "hybrid_na_shortconv_rglru_moe_diffusion"


def layer_norm(x, gain=None, bias=None):
    xf = x.astype(jnp.float32)
    mu = jnp.mean(xf, axis=-1, keepdims=True)
    var = jnp.mean(jnp.square(xf - mu), axis=-1, keepdims=True)
    y = (xf - mu) * lax.rsqrt(var + LN_EPS)
    if gain is not None:
        y = y * gain.astype(jnp.float32) + bias.astype(jnp.float32)
    return y.astype(x.dtype)


def modulate(x, shift, scale):
    return x * (1.0 + scale) + shift


def depthwise_conv(u, w, pad_left, pad_right):
    return lax.conv_general_dilated(
        u, w.astype(u.dtype)[:, None, :], window_strides=(1,),
        padding=[(pad_left, pad_right)], dimension_numbers=("NWC", "WIO", "NWC"),
        feature_group_count=u.shape[-1])


def rope_1d(x, pos):
    m = x.shape[-1] // 2
    inv_freq = ROPE_BASE ** (-jnp.arange(m, dtype=jnp.float32) / m)
    ang = pos.astype(jnp.float32)[:, None] * inv_freq[None, :]
    cos = jnp.cos(ang)[None, :, None, :]
    sin = jnp.sin(ang)[None, :, None, :]
    xf = x.astype(jnp.float32)
    x1, x2 = xf[..., :m], xf[..., m:]
    return jnp.concatenate([x1 * cos - x2 * sin, x1 * sin + x2 * cos], axis=-1).astype(x.dtype)


def axial_rope(x, row_pos, col_pos):
    half = x.shape[-1] // 2
    return jnp.concatenate([rope_1d(x[..., :half], row_pos), rope_1d(x[..., half:], col_pos)], axis=-1)


def neighbourhood_attention(q, k, v, k_ctx, v_ctx, rpb):
    bsz, seq, heads, dh = q.shape
    rows = seq // GRID_W
    kr = min(NA_WIN_ROWS, rows)
    kc = NA_WIN_COLS
    n_loc = kr * kc
    scale = dh ** -0.5
    q5 = q.reshape(bsz, rows, GRID_W, heads, dh)
    k5 = k.reshape(bsz, rows, GRID_W, heads, dh)
    v5 = v.reshape(bsz, rows, GRID_W, heads, dh)
    cols = jnp.arange(GRID_W)
    col_start = jnp.clip(cols - kc // 2, 0, GRID_W - kc)
    col_idx = col_start[:, None] + jnp.arange(kc)[None, :]
    dc_idx = col_idx - cols[:, None] + (NA_WIN_COLS - 1)

    def row_block(r):
        r0 = jnp.clip(r - kr // 2, 0, rows - kr)
        dr_idx = r0 + jnp.arange(kr) - r + (NA_WIN_ROWS - 1)
        bias = rpb[:, dr_idx[None, :, None], dc_idx[:, None, :]].reshape(heads, GRID_W, n_loc)
        q_r = lax.dynamic_index_in_dim(q5, r, axis=1, keepdims=False)
        k_band = lax.dynamic_slice_in_dim(k5, r0, kr, axis=1)
        v_band = lax.dynamic_slice_in_dim(v5, r0, kr, axis=1)
        k_loc = k_band[:, :, col_idx].transpose(0, 2, 1, 3, 4, 5).reshape(bsz, GRID_W, n_loc, heads, dh)
        v_loc = v_band[:, :, col_idx].transpose(0, 2, 1, 3, 4, 5).reshape(bsz, GRID_W, n_loc, heads, dh)
        s_loc = jnp.einsum("bwhd,bwnhd->bhwn", q_r, k_loc).astype(jnp.float32) * scale \
            + bias.astype(jnp.float32)[None]
        s_ctx = jnp.einsum("bwhd,bchd->bhwc", q_r, k_ctx).astype(jnp.float32) * scale
        p = jax.nn.softmax(jnp.concatenate([s_loc, s_ctx], axis=-1), axis=-1).astype(v.dtype)
        return (jnp.einsum("bhwn,bwnhd->bwhd", p[..., :n_loc], v_loc)
                + jnp.einsum("bhwc,bchd->bwhd", p[..., n_loc:], v_ctx))

    o = lax.map(row_block, jnp.arange(rows))
    return o.transpose(1, 0, 2, 3, 4).reshape(bsz, seq, heads * dh)


def context_attention(q, k, v):
    bsz, n, heads, dh = q.shape
    s = jnp.einsum("bqhd,bkhd->bhqk", q, k).astype(jnp.float32) * dh ** -0.5
    p = jax.nn.softmax(s, axis=-1).astype(v.dtype)
    return jnp.einsum("bhqk,bkhd->bqhd", p, v).reshape(bsz, n, heads * dh)


def short_conv(b_gate, c_gate, xs, w):
    return b_gate * depthwise_conv(c_gate * xs, w, 1, 1)


def rglru_coeffs(xm, lam, w_r, b_r, w_i, b_i):
    bsz, seq, ch = xm.shape
    xb = xm.reshape(bsz, seq, LRU_BLOCKS, LRU_BLOCK_DIM)
    r = jax.nn.sigmoid(jnp.einsum("blnd,nde->blne", xb, w_r).reshape(bsz, seq, ch) + b_r).astype(jnp.float32)
    i = jax.nn.sigmoid(jnp.einsum("blnd,nde->blne", xb, w_i).reshape(bsz, seq, ch) + b_i)
    log_a = -LRU_C * jax.nn.softplus(-lam.astype(jnp.float32)) * r
    a = jnp.exp(log_a)
    b = jnp.sqrt(-jnp.expm1(2.0 * log_a)) * (i * xm).astype(jnp.float32)
    return a, b


def linear_scan(a, b, h0, reverse):
    edge = -1 if reverse else 0
    b = b.at[:, edge].add(a[:, edge] * h0)

    def combine(earlier, later):
        a_e, b_e = earlier
        a_l, b_l = later
        return a_e * a_l, a_l * b_e + b_l

    return lax.associative_scan(combine, (a, b), reverse=reverse, axis=1)[1]


def rglru_bidirectional(xm, xm_ctx, lam, w_r, b_r, w_i, b_i):
    h_lat, h_ctx = [], []
    for d, reverse in ((0, False), (1, True)):
        a_c, b_c = rglru_coeffs(xm_ctx, lam[d], w_r[d], b_r[d], w_i[d], b_i[d])
        hc = linear_scan(a_c, b_c, jnp.zeros_like(b_c[:, 0]), reverse)
        state = hc[:, 0] if reverse else hc[:, -1]
        a_l, b_l = rglru_coeffs(xm, lam[d], w_r[d], b_r[d], w_i[d], b_i[d])
        h_lat.append(linear_scan(a_l, b_l, state, reverse))
        h_ctx.append(hc)
    return (h_lat[0] + h_lat[1]).astype(xm.dtype), (h_ctx[0] + h_ctx[1]).astype(xm.dtype)


def merge_branches(gate_logits, y_a, y_b, y_c):
    g = jax.nn.sigmoid(gate_logits.reshape(gate_logits.shape[:-1] + (N_BRANCHES, D_MODEL)))
    return g[..., 0, :] * y_a + g[..., 1, :] * y_b + g[..., 2, :] * y_c


def mixer_sublayer(u, u_ctx, with_ctx_out, w_in, b_in, rpb, w_pa, w_pc, w_pl, sc_w,
                   lru_cw, lru_cb, lam, w_r, b_r, w_i, b_i, w_o, b_o):
    bsz, seq, _ = u.shape
    points = [int(p) for p in np.cumsum(SPLIT_SIZES)[:-1]]
    q, k, v, sb, scg, sx, lx, lg, gl = jnp.split(u @ w_in + b_in, points, axis=-1)
    qc, kc, vc, sbc, scgc, sxc, lxc, lgc, glc = jnp.split(u_ctx @ w_in + b_in, points, axis=-1)

    def heads(t):
        return t.reshape(t.shape[0], t.shape[1], NA_HEADS, NA_HEAD_DIM)

    t = jnp.arange(seq)
    row_pos, col_pos = t // GRID_W, t % GRID_W
    y_a = neighbourhood_attention(axial_rope(heads(q), row_pos, col_pos),
                                  axial_rope(heads(k), row_pos, col_pos),
                                  heads(v), heads(kc), heads(vc), rpb) @ w_pa
    y_b = short_conv(sb, scg, sx, sc_w) @ w_pc
    xm = depthwise_conv(lx, lru_cw, LRU_CONV // 2, LRU_CONV - 1 - LRU_CONV // 2) + lru_cb
    xm_c = depthwise_conv(lxc, lru_cw, LRU_CONV // 2, LRU_CONV - 1 - LRU_CONV // 2) + lru_cb
    h_lat, h_ctx = rglru_bidirectional(xm, xm_c, lam, w_r, b_r, w_i, b_i)
    y_c = (jax.nn.gelu(lg) * h_lat) @ w_pl
    out = merge_branches(gl, y_a, y_b, y_c) @ w_o + b_o
    if not with_ctx_out:
        return out, None
    y_a_c = context_attention(heads(qc), heads(kc), heads(vc)) @ w_pa
    y_b_c = short_conv(sbc, scgc, sxc, sc_w) @ w_pc
    y_c_c = (jax.nn.gelu(lgc) * h_ctx) @ w_pl
    out_c = merge_branches(glc, y_a_c, y_b_c, y_c_c) @ w_o + b_o
    return out, out_c


def routed_experts(tok, router_w, router_b, w_gu, b_gu, w_dn, b_dn):
    n_tok, d = tok.shape
    logits = (tok @ router_w + router_b).astype(jnp.float32)
    top_logits, top_idx = lax.top_k(logits, TOP_K)
    top_p = jax.nn.softmax(top_logits, axis=-1)
    n_assign = n_tok * TOP_K
    flat_e = top_idx.reshape(-1)
    flat_tok = jnp.repeat(jnp.arange(n_tok, dtype=jnp.int32), TOP_K)
    flat_p = top_p.reshape(-1)
    order = jnp.argsort(flat_e)
    s_e, s_tok, s_p = flat_e[order], flat_tok[order], flat_p[order]
    counts = jnp.bincount(flat_e, length=N_EXPERTS)
    padded = (counts + MOE_BLOCK - 1) // MOE_BLOCK * MOE_BLOCK
    start = jnp.cumsum(counts) - counts
    pad_end = jnp.cumsum(padded)
    pad_start = pad_end - padded
    dest = pad_start[s_e] + jnp.arange(n_assign) - start[s_e]
    n_blocks = -(-n_assign // MOE_BLOCK) + N_EXPERTS
    n_slots = n_blocks * MOE_BLOCK
    slot_tok = jnp.zeros((n_slots,), jnp.int32).at[dest].set(s_tok)
    slot_p = jnp.zeros((n_slots,), tok.dtype).at[dest].set(s_p.astype(tok.dtype))
    block_e = jnp.minimum(jnp.searchsorted(pad_end, jnp.arange(n_blocks) * MOE_BLOCK, side="right"),
                          N_EXPERTS - 1)
    x_blocks = tok[slot_tok].reshape(n_blocks, MOE_BLOCK, d)

    def expert_block(args):
        xb, e = args
        gate, up = jnp.split(xb @ w_gu[e] + b_gu[e], 2, axis=-1)
        gate = jnp.minimum(gate, SWIGLU_LIMIT)
        up = jnp.clip(up, -SWIGLU_LIMIT, SWIGLU_LIMIT)
        hid = (up + 1.0) * gate * jax.nn.sigmoid(SWIGLU_ALPHA * gate)
        return hid @ w_dn[e] + b_dn[e]

    y_blocks = lax.map(expert_block, (x_blocks, block_e))
    return jnp.zeros_like(tok).at[slot_tok].add(y_blocks.reshape(n_slots, d) * slot_p[:, None])


def setup_inputs(seed: int = 0) -> dict:
    key = jax.random.key(seed)
    ks = iter(jax.random.split(key, 40))

    def nrm(shape, scale):
        return jax.random.normal(next(ks), shape, jnp.float32) * scale

    d, f = D_MODEL, D_EXPERT
    a_c = jax.random.uniform(next(ks), (DEPTH, 2, LRU_WIDTH), jnp.float32, minval=0.9, maxval=0.999)
    a_base = a_c ** (1.0 / LRU_C)
    return {
        "x": nrm((BATCH, SEQ, d), 1.0),
        "c": nrm((BATCH, d), 1.0),
        "ctx": nrm((BATCH, CTX_LEN, d), 1.0),
        "c_ctx": nrm((d,), 1.0),
        "w_mod": nrm((DEPTH, d, 6 * d), MOD_SCALE * d ** -0.5),
        "b_mod": nrm((DEPTH, 6 * d), 0.02),
        "w_in": nrm((DEPTH, d, P_TOTAL), d ** -0.5),
        "b_in": nrm((DEPTH, P_TOTAL), 0.02),
        "na_rpb": nrm((DEPTH, NA_HEADS, 2 * NA_WIN_ROWS - 1, 2 * NA_WIN_COLS - 1), 0.1),
        "w_proj_attn": nrm((DEPTH, NA_WIDTH, d), DEEPNORM_BETA * NA_WIDTH ** -0.5),
        "w_proj_conv": nrm((DEPTH, SC_WIDTH, d), DEEPNORM_BETA * SC_WIDTH ** -0.5),
        "w_proj_lru": nrm((DEPTH, LRU_WIDTH, d), DEEPNORM_BETA * LRU_WIDTH ** -0.5),
        "sc_conv_w": nrm((DEPTH, SC_CONV, SC_WIDTH), SC_CONV ** -0.5),
        "lru_conv_w": nrm((DEPTH, LRU_CONV, LRU_WIDTH), LRU_CONV ** -0.5),
        "lru_conv_b": nrm((DEPTH, LRU_WIDTH), 0.02),
        "lru_lambda": jnp.log(a_base) - jnp.log1p(-a_base),
        "lru_w_r": nrm((DEPTH, 2, LRU_BLOCKS, LRU_BLOCK_DIM, LRU_BLOCK_DIM), LRU_BLOCK_DIM ** -0.5),
        "lru_b_r": nrm((DEPTH, 2, LRU_WIDTH), 0.02),
        "lru_w_i": nrm((DEPTH, 2, LRU_BLOCKS, LRU_BLOCK_DIM, LRU_BLOCK_DIM), LRU_BLOCK_DIM ** -0.5),
        "lru_b_i": nrm((DEPTH, 2, LRU_WIDTH), 0.02),
        "w_o": nrm((DEPTH, d, d), DEEPNORM_BETA * d ** -0.5),
        "b_o": nrm((DEPTH, d), 0.02),
        "ln1_g": 1.0 + nrm((DEPTH, d), 0.02),
        "ln1_b": nrm((DEPTH, d), 0.02),
        "router_w": nrm((DEPTH, d, N_EXPERTS), d ** -0.5),
        "router_b": nrm((DEPTH, N_EXPERTS), 0.01),
        "exp_w_gu": nrm((DEPTH, N_EXPERTS, d, 2 * f), d ** -0.5),
        "exp_b_gu": nrm((DEPTH, N_EXPERTS, 2 * f), 0.02),
        "exp_w_dn": nrm((DEPTH, N_EXPERTS, f, d), DEEPNORM_BETA * f ** -0.5),
        "exp_b_dn": nrm((DEPTH, N_EXPERTS, d), 0.02),
        "ln2_g": 1.0 + nrm((DEPTH, d), 0.02),
        "ln2_b": nrm((DEPTH, d), 0.02),
    }


def reference(x, c, ctx, c_ctx, w_mod, b_mod, w_in, b_in, na_rpb, w_proj_attn, w_proj_conv,
              w_proj_lru, sc_conv_w, lru_conv_w, lru_conv_b, lru_lambda, lru_w_r, lru_b_r,
              lru_w_i, lru_b_i, w_o, b_o, ln1_g, ln1_b, router_w, router_b, exp_w_gu,
              exp_b_gu, exp_w_dn, exp_b_dn, ln2_g, ln2_b):
    bsz, seq, d = x.shape
    n_ctx = ctx.shape[1]
    h, h_c = x, ctx
    for layer in range(DEPTH):
        last = layer == DEPTH - 1
        mod = jax.nn.silu(c) @ w_mod[layer] + b_mod[layer]
        mod_c = jax.nn.silu(c_ctx) @ w_mod[layer] + b_mod[layer]
        sh1, sc1, g1, sh2, sc2, g2 = jnp.split(mod[:, None, :], 6, axis=-1)
        csh1, csc1, cg1, csh2, csc2, cg2 = jnp.split(mod_c, 6, axis=-1)
        y, y_c = mixer_sublayer(
            modulate(layer_norm(h), sh1, sc1), modulate(layer_norm(h_c), csh1, csc1), not last,
            w_in[layer], b_in[layer], na_rpb[layer], w_proj_attn[layer], w_proj_conv[layer],
            w_proj_lru[layer], sc_conv_w[layer], lru_conv_w[layer], lru_conv_b[layer],
            lru_lambda[layer], lru_w_r[layer], lru_b_r[layer], lru_w_i[layer], lru_b_i[layer],
            w_o[layer], b_o[layer])
        h = layer_norm(DEEPNORM_ALPHA * h + g1 * y, ln1_g[layer], ln1_b[layer])
        u2 = modulate(layer_norm(h), sh2, sc2).reshape(bsz * seq, d)
        if last:
            y2 = routed_experts(u2, router_w[layer], router_b[layer], exp_w_gu[layer],
                                exp_b_gu[layer], exp_w_dn[layer], exp_b_dn[layer]).reshape(bsz, seq, d)
        else:
            h_c = layer_norm(DEEPNORM_ALPHA * h_c + cg1 * y_c, ln1_g[layer], ln1_b[layer])
            u2c = modulate(layer_norm(h_c), csh2, csc2).reshape(bsz * n_ctx, d)
            y2_all = routed_experts(jnp.concatenate([u2, u2c], axis=0), router_w[layer], router_b[layer],
                                    exp_w_gu[layer], exp_b_gu[layer], exp_w_dn[layer], exp_b_dn[layer])
            y2 = y2_all[:bsz * seq].reshape(bsz, seq, d)
            h_c = layer_norm(DEEPNORM_ALPHA * h_c + cg2 * y2_all[bsz * seq:].reshape(bsz, n_ctx, d),
                             ln2_g[layer], ln2_b[layer])
        h = layer_norm(DEEPNORM_ALPHA * h + g2 * y2, ln2_g[layer], ln2_b[layer])
    return h
```

```python
import functools

import numpy as np
import jax
import jax.numpy as jnp
from jax import lax
from jax.experimental import pallas as pl
from jax.experimental.pallas import tpu as pltpu

D_MODEL = 1024
DEPTH = 2
GRID_W = 64
NA_HEADS = 8
NA_HEAD_DIM = 64
NA_WIDTH = NA_HEADS * NA_HEAD_DIM
NA_WIN_ROWS = 8
NA_WIN_COLS = 16
ROPE_BASE = 10000.0
BRANCH_WIDTH = 512
LRU_BLOCKS = 8
LRU_C = 8.0
P_TOTAL = 7168
N_EXPERTS = 32
TOP_K = 4
SWIGLU_LIMIT = 7.0
SWIGLU_ALPHA = 1.702
LN_EPS = 1e-5
DEEPNORM_ALPHA = (2 * DEPTH) ** 0.25
NEG_BIG = -1e30

LANES = 128
SUBLANES = 8
VMEM_LIMIT_BYTES = 56 * 1024 * 1024

TM_INPROJ = 1024
TN_INPROJ = 1024
SCAN_CHUNK = 256
ATT_QROWS = GRID_W
TM_POST = 256
TM_EXPERT = 256
TM_DISPATCH = 512
TM_COMBINE = 256

F32 = jnp.float32
BF16 = jnp.bfloat16


def _params(n_axes):
    return pltpu.CompilerParams(dimension_semantics=("arbitrary",) * n_axes,
                                vmem_limit_bytes=VMEM_LIMIT_BYTES)


def _layer_norm(x):
    mu = jnp.mean(x, axis=-1, keepdims=True)
    xc = x - mu
    var = jnp.mean(xc * xc, axis=-1, keepdims=True)
    return xc * lax.rsqrt(var + LN_EPS)


def _mod_kernel(c_ref, w_ref, b_ref, o_ref):
    c = c_ref[...]
    s = (c * jax.nn.sigmoid(c)).astype(BF16)
    o_ref[0] = jnp.dot(s, w_ref[0].astype(BF16), preferred_element_type=F32) + b_ref[0]


def _modulation(cc, w_mod, b_mod):
    n_out = w_mod.shape[-1]
    return pl.pallas_call(
        _mod_kernel,
        grid=(DEPTH, n_out // D_MODEL),
        in_specs=[pl.BlockSpec((SUBLANES, D_MODEL), lambda l, j: (0, 0)),
                  pl.BlockSpec((1, D_MODEL, D_MODEL), lambda l, j: (l, 0, j)),
                  pl.BlockSpec((1, 1, D_MODEL), lambda l, j: (l, 0, j))],
        out_specs=pl.BlockSpec((1, SUBLANES, D_MODEL), lambda l, j: (l, 0, j)),
        out_shape=jax.ShapeDtypeStruct((DEPTH, SUBLANES, n_out), F32),
        compiler_params=_params(2),
        name="modulation",
    )(cc, w_mod, b_mod.reshape(DEPTH, 1, n_out))


def _rope_half(x, cos, sin_signed):
    lane = lax.broadcasted_iota(jnp.int32, (x.shape[0], LANES), 1)
    first = (lane % 32) < 16
    outs = []
    for cidx in range(x.shape[1] // LANES):
        xc = x[:, cidx * LANES:(cidx + 1) * LANES]
        partner = jnp.where(first, pltpu.roll(xc, LANES - 16, 1), pltpu.roll(xc, 16, 1))
        outs.append(xc * cos + partner * sin_signed)
    return jnp.concatenate(outs, axis=1)


def _inproj_kernel(h_ref, sh_ref, sc_ref, w_ref, b_ref, cos_ref, sin_ref,
                   q_ref, k_ref, v_ref, sb_ref, rest_ref, xn_ref):
    j = pl.program_id(1)

    @pl.when(j == 0)
    def _():
        y = _layer_norm(h_ref[...])
        xn_ref[...] = (y * (1.0 + sc_ref[0]) + sh_ref[0]).astype(BF16)

    acc = jnp.dot(xn_ref[...], w_ref[...], preferred_element_type=F32) + b_ref[...]
    half = BRANCH_WIDTH

    @pl.when(j == 0)
    def _():
        cos = cos_ref[...]
        sin = sin_ref[...]
        q_ref[...] = _rope_half(acc[:, :half], cos, sin).astype(BF16)
        k_ref[...] = _rope_half(acc[:, half:], cos, sin).astype(BF16)

    @pl.when(j == 1)
    def _():
        v_ref[...] = acc[:, :half].astype(BF16)
        sb_ref[...] = acc[:, half:]

    @pl.when(j >= 2)
    def _():
        rest_ref[...] = acc


def _input_projection(h, mod3, mod_base, w_in_bf, b_in, cos_t, sin_t, n_lat, seq):
    m = h.shape[0]
    tm, tn = TM_INPROJ, TN_INPROJ
    n_lat_tiles = n_lat // tm
    tiles_per_seq = seq // tm
    n_groups_lat = n_lat // seq

    def group(i):
        return jnp.where(i < n_lat_tiles, i // tiles_per_seq, n_groups_lat)

    def rope_blk(i):
        return jnp.where(i < n_lat_tiles, i % tiles_per_seq, tiles_per_seq)

    half = BRANCH_WIDTH
    return pl.pallas_call(
        _inproj_kernel,
        grid=(m // tm, P_TOTAL // tn),
        in_specs=[pl.BlockSpec((tm, D_MODEL), lambda i, j: (i, 0)),
                  pl.BlockSpec((1, 1, D_MODEL), lambda i, j: (mod_base + group(i) * 6 + 0, 0, 0)),
                  pl.BlockSpec((1, 1, D_MODEL), lambda i, j: (mod_base + group(i) * 6 + 1, 0, 0)),
                  pl.BlockSpec((D_MODEL, tn), lambda i, j: (0, j)),
                  pl.BlockSpec((1, tn), lambda i, j: (0, j)),
                  pl.BlockSpec((tm, LANES), lambda i, j: (rope_blk(i), 0)),
                  pl.BlockSpec((tm, LANES), lambda i, j: (rope_blk(i), 0))],
        out_specs=[pl.BlockSpec((tm, half), lambda i, j: (i, 0)),
                   pl.BlockSpec((tm, half), lambda i, j: (i, 0)),
                   pl.BlockSpec((tm, half), lambda i, j: (i, 0)),
                   pl.BlockSpec((tm, half), lambda i, j: (i, 0)),
                   pl.BlockSpec((tm, tn), lambda i, j: (i, jnp.maximum(j - 2, 0)))],
        out_shape=[jax.ShapeDtypeStruct((m, half), BF16),
                   jax.ShapeDtypeStruct((m, half), BF16),
                   jax.ShapeDtypeStruct((m, half), BF16),
                   jax.ShapeDtypeStruct((m, half), F32),
                   jax.ShapeDtypeStruct((m, P_TOTAL - 2 * tn), F32)],
        scratch_shapes=[pltpu.VMEM((tm, D_MODEL), BF16)],
        compiler_params=_params(2),
        name="input_projection",
    )(h, mod3, mod3, w_in_bf, b_in.reshape(1, P_TOTAL), cos_t, sin_t)


def _make_rope(seq, tm):
    t = np.arange(seq)
    row_pos, col_pos = t // GRID_W, t % GRID_W
    d = np.arange(LANES) % NA_HEAD_DIM
    m = NA_HEAD_DIM // 4
    inv_freq = (ROPE_BASE ** (-jnp.arange(m, dtype=F32) / m))[d % m]
    pos = np.where((d < 2 * m)[None, :], row_pos[:, None], col_pos[:, None])
    ang = jnp.asarray(pos).astype(F32) * inv_freq[None, :]
    cos = jnp.cos(ang)
    sin = jnp.sin(ang)
    sin_signed = jnp.where(jnp.asarray((d % (2 * m)) < m)[None, :], -sin, sin)
    cos = jnp.concatenate([cos, jnp.ones((tm, LANES), F32)], axis=0)
    sin_signed = jnp.concatenate([sin_signed, jnp.zeros((tm, LANES), F32)], axis=0)
    return cos, sin_signed


def _scan_kernel(fblk, bblk, first, last, seqb,
                 sb_ref, scg_ref, sx_ref, lxf_ref,
                 scgp_ref, sxp_ref, lxfp_ref, scgn_ref, sxn_ref, lxfn_ref,
                 lxb_ref, lxbp_ref, lxbn_ref,
                 scw_ref, cw_ref, cb_ref, sp_ref, wr_ref, wi_ref, br_ref, bi_ref,
                 zb_ref, hf_ref, hb_ref,
                 a_s, b_s, hc_s, st_s, *, ch, n_ctx_items):
    it = pl.program_id(0)
    is_first = first[it] == 1
    is_last = last[it] == 1
    is_ctx = it < n_ctx_items
    b = seqb[it]
    width = BRANCH_WIDTH
    row = lax.broadcasted_iota(jnp.int32, (ch, width), 0)
    not_first = jnp.where(is_first, 0.0, 1.0).astype(F32)
    not_last = jnp.where(is_last, 0.0, 1.0).astype(F32)

    def back1(u, prev_row):
        return jnp.where(row == 0, prev_row, pltpu.roll(u, 1, 0))

    def back2(u, prev2, prev1):
        return jnp.where(row == 0, prev2, jnp.where(row == 1, prev1, pltpu.roll(u, 2, 0)))

    def fwd1(u, next_row):
        return jnp.where(row == ch - 1, next_row, pltpu.roll(u, ch - 1, 0))

    def lru_input(lx_ref, lxp_ref, lxn_ref, prev_ok, next_ok):
        x = lx_ref[...]
        p = lxp_ref[...] * prev_ok
        n = lxn_ref[...] * next_ok
        return (cw_ref[0:1] * back2(x, p[6:7], p[7:8]) + cw_ref[1:2] * back1(x, p[7:8])
                + cw_ref[2:3] * x + cw_ref[3:4] * fwd1(x, n[0:1]) + cb_ref[...])

    def coeffs(d, xm):
        xb = xm.astype(BF16)
        r = jax.nn.sigmoid(jnp.dot(xb, wr_ref[d], preferred_element_type=F32) + br_ref[d:d + 1])
        g = jax.nn.sigmoid(jnp.dot(xb, wi_ref[d], preferred_element_type=F32) + bi_ref[d:d + 1])
        log_a = (-LRU_C * sp_ref[d:d + 1]) * r
        a = jnp.exp(log_a)
        a_s[d] = a
        b_s[d] = jnp.sqrt(-jnp.tanh(log_a) * (a * a + 1.0)) * (g * xm)

    u = scg_ref[...] * sx_ref[...]
    u_prev = scgp_ref[7:8] * sxp_ref[7:8] * not_first
    u_next = scgn_ref[0:1] * sxn_ref[0:1] * not_last
    conv = scw_ref[0:1] * back1(u, u_prev) + scw_ref[1:2] * u + scw_ref[2:3] * fwd1(u, u_next)
    zb_ref[...] = (sb_ref[...] * conv).astype(BF16)

    coeffs(0, lru_input(lxf_ref, lxfp_ref, lxfn_ref, not_first, not_last))
    coeffs(1, lru_input(lxb_ref, lxbp_ref, lxbn_ref, not_last, not_first))

    @pl.when(jnp.logical_and(is_first, is_ctx))
    def _():
        hc_s[...] = jnp.zeros_like(hc_s)

    @pl.when(jnp.logical_and(is_first, jnp.logical_not(is_ctx)))
    def _():
        hc_s[0:1] = st_s[pl.ds(2 * b, 1), :]
        hc_s[1:2] = st_s[pl.ds(2 * b + 1, 1), :]

    def body(i, carry):
        hf, hb = carry
        base = i * SUBLANES
        for r in range(SUBLANES):
            t = base + r
            hf = a_s[0, pl.ds(t, 1), :] * hf + b_s[0, pl.ds(t, 1), :]
            hf_ref[pl.ds(t, 1), :] = hf
            tb = ch - 1 - t
            hb = a_s[1, pl.ds(tb, 1), :] * hb + b_s[1, pl.ds(tb, 1), :]
            hb_ref[pl.ds(tb, 1), :] = hb
        return hf, hb

    hf, hb = lax.fori_loop(0, ch // SUBLANES, body, (hc_s[0:1], hc_s[1:2]))
    hc_s[0:1] = hf
    hc_s[1:2] = hb

    @pl.when(is_ctx)
    def _():
        st_s[pl.ds(2 * b, 1), :] = hf
        st_s[pl.ds(2 * b + 1, 1), :] = hb


def _scan_tables(n_batch, seq, n_ctx, ch):
    assert n_ctx == ch
    nc = seq // ch
    ctx0 = n_batch * seq // ch
    fblk, bblk, first, last, seqb = [], [], [], [], []
    for b in range(n_batch):
        fblk.append(ctx0 + b); bblk.append(ctx0 + b); first.append(1); last.append(1); seqb.append(b)
    for b in range(n_batch):
        for c in range(nc):
            fblk.append(b * nc + c); bblk.append(b * nc + nc - 1 - c)
            first.append(int(c == 0)); last.append(int(c == nc - 1)); seqb.append(b)
    return [np.asarray(a, np.int32) for a in (fblk, bblk, first, last, seqb)]


def _conv_scan(sb, rest, sc_w, lru_cw, lru_cb, sp, wr_bd, wi_bd, b_r, b_i, n_batch, seq, n_ctx):
    m = sb.shape[0]
    ch = SCAN_CHUNK
    width = BRANCH_WIDTH
    tables = _scan_tables(n_batch, seq, n_ctx, ch)
    n_items = len(tables[0])
    halo_per_chunk = ch // SUBLANES
    last_halo = m // SUBLANES - 1

    def cur(col, which):
        return pl.BlockSpec((ch, width), lambda i, f, bk, *_: ((f, bk)[which][i], col))

    def prev(col, which):
        return pl.BlockSpec((SUBLANES, width),
                            lambda i, f, bk, *_: (jnp.maximum((f, bk)[which][i] * halo_per_chunk - 1, 0), col))

    def nxt(col, which):
        return pl.BlockSpec((SUBLANES, width),
                            lambda i, f, bk, *_: (jnp.minimum(((f, bk)[which][i] + 1) * halo_per_chunk, last_halo), col))

    def full(shape):
        return pl.BlockSpec(shape, lambda i, *_: (0,) * len(shape))

    in_specs = [cur(0, 0), cur(0, 0), cur(1, 0), cur(2, 0),
                prev(0, 0), prev(1, 0), prev(2, 0), nxt(0, 0), nxt(1, 0), nxt(2, 0),
                cur(2, 1), prev(2, 1), nxt(2, 1),
                full(sc_w.shape), full(lru_cw.shape), full((1, width)), full(sp.shape),
                full(wr_bd.shape), full(wi_bd.shape), full(b_r.shape), full(b_i.shape)]
    out_specs = [cur(0, 0), cur(0, 0), cur(0, 1)]
    grid_spec = pltpu.PrefetchScalarGridSpec(
        num_scalar_prefetch=5, grid=(n_items,), in_specs=in_specs, out_specs=out_specs,
        scratch_shapes=[pltpu.VMEM((2, ch, width), F32), pltpu.VMEM((2, ch, width), F32),
                        pltpu.VMEM((SUBLANES, width), F32), pltpu.VMEM((2 * n_batch, width), F32)])
    return pl.pallas_call(
        functools.partial(_scan_kernel, ch=ch, n_ctx_items=n_batch),
        grid_spec=grid_spec,
        out_shape=[jax.ShapeDtypeStruct((m, width), BF16),
                   jax.ShapeDtypeStruct((m, width), F32),
                   jax.ShapeDtypeStruct((m, width), F32)],
        compiler_params=_params(1),
        name="conv_scan",
    )(*[jnp.asarray(t) for t in tables],
      sb, rest, rest, rest, rest, rest, rest, rest, rest, rest, rest, rest, rest,
      sc_w, lru_cw, lru_cb.reshape(1, width), sp, wr_bd, wi_bd, b_r, b_i)


def _attention_kernel(qblk, bat, r0t, cls,
                      q_ref, k_ref, v_ref, kc_ref, vc_ref, bias_ref, o_ref, *, band):
    it = pl.program_id(0)
    start = pl.multiple_of(r0t[it] * GRID_W, GRID_W)
    nq = q_ref.shape[0]
    lane = lax.broadcasted_iota(jnp.int32, (nq, LANES), 1)
    low = lane < NA_HEAD_DIM
    scale = NA_HEAD_DIM ** -0.5
    nt = (((1,), (1,)), ((), ()))
    for hp in range(NA_HEADS // 2):
        cols = slice(hp * LANES, (hp + 1) * LANES)
        qp = q_ref[:, cols].astype(F32)
        qs = jnp.concatenate([jnp.where(low, qp, 0.0), jnp.where(low, 0.0, qp)], axis=0).astype(BF16)
        kb = k_ref[pl.ds(start, band), cols]
        vb = v_ref[pl.ds(start, band), cols]
        s_loc = lax.dot_general(qs, kb, nt, preferred_element_type=F32)
        s_ctx = lax.dot_general(qs, kc_ref[:, cols], nt, preferred_element_type=F32)
        bias = jnp.concatenate([bias_ref[0, 2 * hp], bias_ref[0, 2 * hp + 1]], axis=0)
        s_loc = s_loc * scale + bias
        s_ctx = s_ctx * scale
        mx = jnp.maximum(jnp.max(s_loc, axis=-1, keepdims=True), jnp.max(s_ctx, axis=-1, keepdims=True))
        e_loc = jnp.exp(s_loc - mx)
        e_ctx = jnp.exp(s_ctx - mx)
        den = jnp.sum(e_loc, axis=-1, keepdims=True) + jnp.sum(e_ctx, axis=-1, keepdims=True)
        o = (jnp.dot(e_loc.astype(BF16), vb, preferred_element_type=F32)
             + jnp.dot(e_ctx.astype(BF16), vc_ref[:, cols], preferred_element_type=F32)) / den
        o_ref[:, cols] = jnp.where(low, o[:nq], o[nq:]).astype(BF16)


def _attention_tables(n_batch, seq, n_ctx, with_ctx_queries):
    rows = seq // GRID_W
    kr = min(NA_WIN_ROWS, rows)
    qblk, bat, r0t, cls = [], [], [], []
    ctx_q0 = n_batch * seq // ATT_QROWS
    for b in range(n_batch):
        for r in range(rows):
            r0 = min(max(r - kr // 2, 0), rows - kr)
            qblk.append(b * rows + r); bat.append(b); r0t.append(r0); cls.append(r - r0)
        if with_ctx_queries:
            for c in range(n_ctx // ATT_QROWS):
                qblk.append(ctx_q0 + b * (n_ctx // ATT_QROWS) + c); bat.append(b); r0t.append(0); cls.append(kr)
    return [np.asarray(a, np.int32) for a in (qblk, bat, r0t, cls)]


def _attention_bias(rpb, seq):
    rows = seq // GRID_W
    kr = min(NA_WIN_ROWS, rows)
    kc = NA_WIN_COLS
    cq = np.arange(GRID_W)
    c0 = np.clip(cq - kc // 2, 0, GRID_W - kc)
    ck = np.arange(GRID_W)
    inside = (ck[None, :] >= c0[:, None]) & (ck[None, :] < c0[:, None] + kc)
    dc = np.clip(ck[None, :] - cq[:, None] + (NA_WIN_COLS - 1), 0, 2 * NA_WIN_COLS - 2)
    classes = []
    for cl in range(kr):
        dr = np.arange(kr) - cl + (NA_WIN_ROWS - 1)
        g = rpb[:, dr][:, :, dc]
        g = jnp.where(jnp.asarray(inside)[None, None], g, NEG_BIG)
        classes.append(g.transpose(0, 2, 1, 3).reshape(NA_HEADS, GRID_W, kr * GRID_W))
    classes.append(jnp.full((NA_HEADS, GRID_W, kr * GRID_W), NEG_BIG, F32))
    return jnp.stack(classes, axis=0)


def _attention(q, k, v, bias, n_batch, seq, n_ctx, with_ctx_queries):
    m = q.shape[0] if with_ctx_queries else n_batch * seq
    rows = seq // GRID_W
    kr = min(NA_WIN_ROWS, rows)
    band = kr * GRID_W
    tables = _attention_tables(n_batch, seq, n_ctx, with_ctx_queries)
    n_items = len(tables[0])
    ctx_blk0 = n_batch * seq // n_ctx
    width = NA_WIDTH
    in_specs = [pl.BlockSpec((ATT_QROWS, width), lambda i, qb, bt, r0, cl: (qb[i], 0)),
                pl.BlockSpec((seq, width), lambda i, qb, bt, r0, cl: (bt[i], 0)),
                pl.BlockSpec((seq, width), lambda i, qb, bt, r0, cl: (bt[i], 0)),
                pl.BlockSpec((n_ctx, width), lambda i, qb, bt, r0, cl: (ctx_blk0 + bt[i], 0)),
                pl.BlockSpec((n_ctx, width), lambda i, qb, bt, r0, cl: (ctx_blk0 + bt[i], 0)),
                pl.BlockSpec((1, NA_HEADS, GRID_W, band), lambda i, qb, bt, r0, cl: (cl[i], 0, 0, 0))]
    out_specs = pl.BlockSpec((ATT_QROWS, width), lambda i, qb, bt, r0, cl: (qb[i], 0))
    grid_spec = pltpu.PrefetchScalarGridSpec(num_scalar_prefetch=4, grid=(n_items,),
                                             in_specs=in_specs, out_specs=out_specs)
    return pl.pallas_call(
        functools.partial(_attention_kernel, band=band),
        grid_spec=grid_spec,
        out_shape=jax.ShapeDtypeStruct((m, width), BF16),
        compiler_params=_params(1),
        name="attention",
    )(*[jnp.asarray(t) for t in tables], q, k, v, k, v, bias)


def _post_kernel(h_ref, att_ref, zb_ref, hf_ref, hb_ref, lg_ref, gl0_ref, gl1_ref, gl2_ref,
                 g1_ref, sh2_ref, sc2_ref,
                 wpa_ref, wpc_ref, wpl_ref, wo_ref, bo_ref, l1g_ref, l1b_ref, rw_ref, rb_ref, tri_ref,
                 h1_ref, u2_ref, eidx_ref, pw_ref, rank_ref, cnt_ref, base_s):
    i = pl.program_id(0)

    @pl.when(i == 0)
    def _():
        base_s[...] = jnp.zeros_like(base_s)

    y_a = jnp.dot(att_ref[...], wpa_ref[...], preferred_element_type=F32)
    y_b = jnp.dot(zb_ref[...], wpc_ref[...], preferred_element_type=F32)
    zc = jax.nn.gelu(lg_ref[...]) * (hf_ref[...] + hb_ref[...])
    y_c = jnp.dot(zc.astype(BF16), wpl_ref[...], preferred_element_type=F32)
    merged = (jax.nn.sigmoid(gl0_ref[...]) * y_a + jax.nn.sigmoid(gl1_ref[...]) * y_b
              + jax.nn.sigmoid(gl2_ref[...]) * y_c)
    y = jnp.dot(merged.astype(BF16), wo_ref[...], preferred_element_type=F32) + bo_ref[...]
    h1 = _layer_norm(DEEPNORM_ALPHA * h_ref[...] + g1_ref[0] * y) * l1g_ref[...] + l1b_ref[...]
    h1_ref[...] = h1
    u2 = _layer_norm(h1) * (1.0 + sc2_ref[0]) + sh2_ref[0]
    u2_ref[...] = u2

    logits = jnp.dot(u2, rw_ref[...], precision=lax.Precision.HIGHEST, preferred_element_type=F32) + rb_ref[...]
    tm = logits.shape[0]
    lane = lax.broadcasted_iota(jnp.int32, (tm, LANES), 1)
    lane_f = lane.astype(F32)
    work = logits
    tops, idxs, hots = [], [], []
    for _ in range(TOP_K):
        mx = jnp.max(work, axis=-1, keepdims=True)
        idx = jnp.min(jnp.where(work == mx, lane_f, float(LANES)), axis=-1, keepdims=True)
        hot = lane_f == idx
        work = jnp.where(hot, -3e38, work)
        tops.append(mx); idxs.append(idx); hots.append(hot)
    exps = [jnp.exp(t - tops[0]) for t in tops]
    den = exps[0] + exps[1] + exps[2] + exps[3]
    hot_all = jnp.zeros((tm, LANES), F32)
    for hot in hots:
        hot_all = hot_all + hot.astype(F32)
    before = base_s[0:1] + jnp.dot(tri_ref[...], hot_all.astype(BF16), preferred_element_type=F32)
    eidx = jnp.zeros((tm, LANES), F32)
    pw = jnp.zeros((tm, LANES), F32)
    rank = jnp.zeros((tm, LANES), F32)
    for kk in range(TOP_K):
        sel = lane == kk
        rk = jnp.sum(jnp.where(hots[kk], before, 0.0), axis=-1, keepdims=True)
        eidx = jnp.where(sel, idxs[kk], eidx)
        pw = jnp.where(sel, exps[kk] / den, pw)
        rank = jnp.where(sel, rk, rank)
    eidx_ref[...] = eidx.astype(jnp.int32)
    pw_ref[...] = pw
    rank_ref[...] = rank.astype(jnp.int32)
    new_base = base_s[0:1] + jnp.sum(hot_all, axis=0, keepdims=True)
    base_s[0:1] = new_base
    cnt_ref[...] = jnp.broadcast_to(new_base, cnt_ref.shape)


def _post_mixer(h, att, zb, hf, hb, rest, mod3, mod_base, wpa, wpc, wpl, wo, b_o, l1g, l1b, rw_pad, rb_pad,
                n_rows, n_lat, seq):
    tm = TM_POST
    width = BRANCH_WIDTH
    tiles_per_seq = seq // tm
    n_lat_tiles = n_lat // tm
    n_groups_lat = n_lat // seq
    tri = jnp.asarray(np.tril(np.ones((tm, tm), np.float32), -1), BF16)

    def group(i):
        return jnp.where(i < n_lat_tiles, i // tiles_per_seq, n_groups_lat)

    def rows(wd, col=0):
        return pl.BlockSpec((tm, wd), lambda i: (i, col))

    def full(shape):
        return pl.BlockSpec(shape, lambda i: (0,) * len(shape))

    def mod(which):
        return pl.BlockSpec((1, 1, D_MODEL), lambda i: (mod_base + group(i) * 6 + which, 0, 0))

    in_specs = [rows(D_MODEL), rows(width), rows(width), rows(width), rows(width), rows(width, 3),
                rows(D_MODEL, 2), rows(D_MODEL, 3), rows(D_MODEL, 4),
                mod(2), mod(3), mod(4),
                full(wpa.shape), full(wpc.shape), full(wpl.shape), full(wo.shape), full((1, D_MODEL)),
                full((1, D_MODEL)), full((1, D_MODEL)), full(rw_pad.shape), full(rb_pad.shape), full(tri.shape)]
    out_specs = [rows(D_MODEL), rows(D_MODEL), rows(LANES), rows(LANES), rows(LANES),
                 pl.BlockSpec((SUBLANES, LANES), lambda i: (0, 0))]
    return pl.pallas_call(
        _post_kernel,
        grid=(n_rows // tm,),
        in_specs=in_specs, out_specs=out_specs,
        out_shape=[jax.ShapeDtypeStruct((n_rows, D_MODEL), F32), jax.ShapeDtypeStruct((n_rows, D_MODEL), F32),
                   jax.ShapeDtypeStruct((n_rows, LANES), jnp.int32), jax.ShapeDtypeStruct((n_rows, LANES), F32),
                   jax.ShapeDtypeStruct((n_rows, LANES), jnp.int32), jax.ShapeDtypeStruct((SUBLANES, LANES), F32)],
        scratch_shapes=[pltpu.VMEM((SUBLANES, LANES), F32)],
        compiler_params=_params(1),
        name="post_mixer",
    )(h, att, zb, hf, hb, rest, rest, rest, rest, mod3, mod3, mod3,
      wpa, wpc, wpl, wo, b_o.reshape(1, D_MODEL), l1g.reshape(1, D_MODEL), l1b.reshape(1, D_MODEL),
      rw_pad, rb_pad, tri)


def _row_copy(src, src_row, dst, dst_row, sem):
    return pltpu.make_async_copy(src.at[pl.ds(src_row, 1)], dst.at[pl.ds(dst_row, 1)], sem)


def _dispatch_kernel(cend, pend, n_act, dest_ref, u_hbm, xs_hbm, zrow, sem, *, tm, n_tiles):
    i = pl.program_id(0)

    @pl.when(i == 0)
    def _():
        zrow[...] = jnp.zeros_like(zrow)

        def tile_copy(j):
            return pltpu.make_async_copy(zrow, xs_hbm.at[pl.ds(j * TM_EXPERT, TM_EXPERT)], sem)

        def start_tile(j, c):
            tile_copy(j).start()
            return c

        def wait_tile(j, c):
            tile_copy(j).wait()
            return c

        lax.fori_loop(n_act[0], n_tiles, start_tile, 0)
        lax.fori_loop(n_act[0], n_tiles, wait_tile, 0)

        def per_expert(e, carry):
            lo, hi = cend[e], pend[e]

            def start(s, c):
                _row_copy(zrow, 0, xs_hbm, s, sem).start()
                return c

            def wait(s, c):
                _row_copy(zrow, 0, xs_hbm, s, sem).wait()
                return c

            lax.fori_loop(lo, hi, start, 0)
            lax.fori_loop(lo, hi, wait, 0)
            return carry

        lax.fori_loop(0, N_EXPERTS, per_expert, 0)

    base = i * tm

    def start(r, c):
        for kk in range(TOP_K):
            _row_copy(u_hbm, base + r, xs_hbm, dest_ref[TOP_K * r + kk], sem).start()
        return c

    def wait(r, c):
        for kk in range(TOP_K):
            _row_copy(u_hbm, base + r, xs_hbm, dest_ref[TOP_K * r + kk], sem).wait()
        return c

    lax.fori_loop(0, tm, start, 0)
    lax.fori_loop(0, tm, wait, 0)


def _dispatch(u2, dest_flat, cend, pend, n_act, n_slots):
    n_rows = u2.shape[0]
    tm = TM_DISPATCH
    grid_spec = pltpu.PrefetchScalarGridSpec(
        num_scalar_prefetch=3, grid=(n_rows // tm,),
        in_specs=[pl.BlockSpec((tm * TOP_K,), lambda i, *_: (i,), memory_space=pltpu.SMEM),
                  pl.BlockSpec(memory_space=pl.ANY)],
        out_specs=pl.BlockSpec(memory_space=pl.ANY),
        scratch_shapes=[pltpu.VMEM((TM_EXPERT, D_MODEL), F32), pltpu.SemaphoreType.DMA(())])
    return pl.pallas_call(
        functools.partial(_dispatch_kernel, tm=tm, n_tiles=n_slots // TM_EXPERT),
        grid_spec=grid_spec,
        out_shape=jax.ShapeDtypeStruct((n_slots, D_MODEL), F32),
        compiler_params=_params(1),
        name="moe_dispatch",
    )(cend, pend, n_act, dest_flat, u2)


def _expert_kernel(tile_e, n_act, xs_ref, wgu_ref, bgu_ref, wdn_ref, bdn_ref, ys_ref, wgu_s, wdn_s):
    j = pl.program_id(0)

    @pl.when(j < n_act[0])
    def _():
        e = tile_e[j]
        e_prev = tile_e[jnp.maximum(j - 1, 0)]

        @pl.when(jnp.logical_or(j == 0, e != e_prev))
        def _():
            wgu_s[...] = wgu_ref[0].astype(BF16)
            wdn_s[...] = wdn_ref[0].astype(BF16)

        x = xs_ref[...].astype(BF16)
        gu = jnp.dot(x, wgu_s[...], preferred_element_type=F32) + bgu_ref[0]
        f = gu.shape[1] // 2
        gate = jnp.minimum(gu[:, :f], SWIGLU_LIMIT)
        up = jnp.clip(gu[:, f:], -SWIGLU_LIMIT, SWIGLU_LIMIT)
        hid = (up + 1.0) * gate * jax.nn.sigmoid(SWIGLU_ALPHA * gate)
        ys_ref[...] = jnp.dot(hid.astype(BF16), wdn_s[...], preferred_element_type=F32) + bdn_ref[0]

    @pl.when(j >= n_act[0])
    def _():
        ys_ref[...] = jnp.zeros_like(ys_ref)


def _experts(xs, tile_e, n_act, w_gu, b_gu, w_dn, b_dn):
    n_slots = xs.shape[0]
    tm = TM_EXPERT
    n_tiles = n_slots // tm
    f2 = w_gu.shape[-1]

    def tile(j, te, na):
        return jnp.minimum(j, na[0] - 1)

    in_specs = [pl.BlockSpec((tm, D_MODEL), lambda j, te, na: (tile(j, te, na), 0)),
                pl.BlockSpec((1, D_MODEL, f2), lambda j, te, na: (te[tile(j, te, na)], 0, 0)),
                pl.BlockSpec((1, 1, f2), lambda j, te, na: (te[tile(j, te, na)], 0, 0)),
                pl.BlockSpec((1, f2 // 2, D_MODEL), lambda j, te, na: (te[tile(j, te, na)], 0, 0)),
                pl.BlockSpec((1, 1, D_MODEL), lambda j, te, na: (te[tile(j, te, na)], 0, 0))]
    out_specs = pl.BlockSpec((tm, D_MODEL), lambda j, te, na: (j, 0))
    grid_spec = pltpu.PrefetchScalarGridSpec(
        num_scalar_prefetch=2, grid=(n_tiles,), in_specs=in_specs, out_specs=out_specs,
        scratch_shapes=[pltpu.VMEM((D_MODEL, f2), BF16), pltpu.VMEM((f2 // 2, D_MODEL), BF16)])
    return pl.pallas_call(
        _expert_kernel,
        grid_spec=grid_spec,
        out_shape=jax.ShapeDtypeStruct((n_slots, D_MODEL), F32),
        compiler_params=_params(1),
        name="moe_experts",
    )(tile_e, n_act, xs, w_gu, b_gu.reshape(N_EXPERTS, 1, f2), w_dn, b_dn.reshape(N_EXPERTS, 1, D_MODEL))


def _combine_kernel(dest_ref, ys_hbm, pw_ref, h1_ref, g2_ref, l2g_ref, l2b_ref, o_ref, buf, sem, *, tm):
    def start(r, c):
        for kk in range(TOP_K):
            _row_copy(ys_hbm, dest_ref[TOP_K * r + kk], buf.at[kk], r, sem).start()
        return c

    def wait(r, c):
        for kk in range(TOP_K):
            _row_copy(ys_hbm, dest_ref[TOP_K * r + kk], buf.at[kk], r, sem).wait()
        return c

    lax.fori_loop(0, tm, start, 0)
    lax.fori_loop(0, tm, wait, 0)
    pw = pw_ref[...]
    y2 = pw[:, 0:1] * buf[0]
    for kk in range(1, TOP_K):
        y2 = y2 + pw[:, kk:kk + 1] * buf[kk]
    o_ref[...] = _layer_norm(DEEPNORM_ALPHA * h1_ref[...] + g2_ref[0] * y2) * l2g_ref[...] + l2b_ref[...]


def _combine(ys, dest_flat, pw, h1, mod3, mod_base, l2g, l2b, n_lat, seq):
    n_rows = h1.shape[0]
    tm = TM_COMBINE
    tiles_per_seq = seq // tm
    n_lat_tiles = n_lat // tm
    n_groups_lat = n_lat // seq

    def group(i):
        return jnp.where(i < n_lat_tiles, i // tiles_per_seq, n_groups_lat)

    in_specs = [pl.BlockSpec((tm * TOP_K,), lambda i: (i,), memory_space=pltpu.SMEM),
                pl.BlockSpec(memory_space=pl.ANY),
                pl.BlockSpec((tm, LANES), lambda i: (i, 0)),
                pl.BlockSpec((tm, D_MODEL), lambda i: (i, 0)),
                pl.BlockSpec((1, 1, D_MODEL), lambda i: (mod_base + group(i) * 6 + 5, 0, 0)),
                pl.BlockSpec((1, D_MODEL), lambda i: (0, 0)),
                pl.BlockSpec((1, D_MODEL), lambda i: (0, 0))]
    return pl.pallas_call(
        functools.partial(_combine_kernel, tm=tm),
        grid=(n_rows // tm,),
        in_specs=in_specs,
        out_specs=pl.BlockSpec((tm, D_MODEL), lambda i: (i, 0)),
        out_shape=jax.ShapeDtypeStruct((n_rows, D_MODEL), F32),
        scratch_shapes=[pltpu.VMEM((TOP_K, tm, D_MODEL), F32), pltpu.SemaphoreType.DMA(())],
        compiler_params=_params(1),
        name="moe_combine",
    )(dest_flat, ys, pw, h1, mod3, l2g.reshape(1, D_MODEL), l2b.reshape(1, D_MODEL))


def _routing_plan(eidx, rank, counts_f, n_rows):
    tm = TM_EXPERT
    counts = counts_f[0, :N_EXPERTS].astype(jnp.int32)
    padded = (counts + tm - 1) // tm * tm
    pad_end = jnp.cumsum(padded)
    pad_start = pad_end - padded
    dest = pad_start[eidx[:, :TOP_K]] + rank[:, :TOP_K]
    n_tiles = n_rows * TOP_K // tm + N_EXPERTS
    tile_e = jnp.minimum(jnp.searchsorted(pad_end, jnp.arange(n_tiles, dtype=jnp.int32) * tm, side="right"),
                         N_EXPERTS - 1).astype(jnp.int32)
    n_act = (pad_end[-1:] // tm).astype(jnp.int32)
    return dest.reshape(-1).astype(jnp.int32), (pad_start + counts).astype(jnp.int32), pad_end.astype(jnp.int32), \
        tile_e, n_act, n_tiles * tm


def _block_diag(w):
    two, n, d, e = w.shape
    eye = jnp.eye(n, dtype=w.dtype)
    return (w[:, :, :, None, :] * eye[None, :, None, :, None]).reshape(two, n * d, n * e)


def kernel(x, c, ctx, c_ctx, w_mod, b_mod, w_in, b_in, na_rpb, w_proj_attn, w_proj_conv, w_proj_lru, sc_conv_w, lru_conv_w, lru_conv_b, lru_lambda, lru_w_r, lru_b_r, lru_w_i, lru_b_i, w_o, b_o, ln1_g, ln1_b, router_w, router_b, exp_w_gu, exp_b_gu, exp_w_dn, exp_b_dn, ln2_g, ln2_b):
    n_batch, seq, d = x.shape
    n_ctx = ctx.shape[1]
    n_lat = n_batch * seq
    n_all = n_lat + n_batch * n_ctx
    assert d == D_MODEL and n_batch + 1 <= SUBLANES

    cc = jnp.concatenate([c, c_ctx[None], jnp.zeros((SUBLANES - n_batch - 1, d), F32)], axis=0)
    mod = _modulation(cc, w_mod, b_mod)
    groups = n_batch + 1
    mod3 = mod.reshape(DEPTH, SUBLANES, 6, d)[:, :groups].reshape(DEPTH * groups * 6, 1, d)

    cos_t, sin_t = _make_rope(seq, TM_INPROJ)
    h = jnp.concatenate([x.reshape(n_lat, d), ctx.reshape(n_batch * n_ctx, d)], axis=0)

    for layer in range(DEPTH):
        last = layer == DEPTH - 1
        mod_base = layer * groups * 6
        q, k, v, sb, rest = _input_projection(h, mod3, mod_base, w_in[layer].astype(BF16), b_in[layer],
                                              cos_t, sin_t, n_lat, seq)
        sp = jax.nn.softplus(-lru_lambda[layer])
        zb, hf, hb = _conv_scan(sb, rest, sc_conv_w[layer], lru_conv_w[layer], lru_conv_b[layer], sp,
                                _block_diag(lru_w_r[layer]).astype(BF16), _block_diag(lru_w_i[layer]).astype(BF16),
                                lru_b_r[layer], lru_b_i[layer], n_batch, seq, n_ctx)
        att = _attention(q, k, v, _attention_bias(na_rpb[layer], seq), n_batch, seq, n_ctx, not last)
        n_rows = n_lat if last else n_all
        rw_pad = jnp.pad(router_w[layer], ((0, 0), (0, LANES - N_EXPERTS)))
        rb_pad = jnp.concatenate([router_b[layer], jnp.full((LANES - N_EXPERTS,), NEG_BIG, F32)]).reshape(1, LANES)
        h1, u2, eidx, pw, rank, counts = _post_mixer(
            h, att, zb, hf, hb, rest, mod3, mod_base,
            w_proj_attn[layer].astype(BF16), w_proj_conv[layer].astype(BF16), w_proj_lru[layer].astype(BF16),
            w_o[layer].astype(BF16), b_o[layer], ln1_g[layer], ln1_b[layer], rw_pad, rb_pad, n_rows, n_lat, seq)
        dest, cend, pend, tile_e, n_act, n_slots = _routing_plan(eidx, rank, counts, n_rows)
        xs = _dispatch(u2, dest, cend, pend, n_act, n_slots)
        ys = _experts(xs, tile_e, n_act, exp_w_gu[layer], exp_b_gu[layer], exp_w_dn[layer], exp_b_dn[layer])
        h = _combine(ys, dest, pw, h1, mod3, mod_base, ln2_g[layer], ln2_b[layer], n_lat, seq)
    return h.reshape(n_batch, seq, d)
```

```python
import functools

import numpy as np
import jax
import jax.numpy as jnp
from jax import lax
from jax.experimental import pallas as pl
from jax.experimental.pallas import tpu as pltpu

D_MODEL = 1024
DEPTH = 2
GRID_W = 64
NA_HEADS = 8
NA_HEAD_DIM = 64
NA_WIDTH = NA_HEADS * NA_HEAD_DIM
NA_WIN_ROWS = 8
NA_WIN_COLS = 16
ROPE_BASE = 10000.0
BRANCH_WIDTH = 512
LRU_BLOCKS = 8
LRU_C = 8.0
P_TOTAL = 7168
N_EXPERTS = 32
TOP_K = 4
SWIGLU_LIMIT = 7.0
SWIGLU_ALPHA = 1.702
LN_EPS = 1e-5
DEEPNORM_ALPHA = (2 * DEPTH) ** 0.25
NEG_BIG = -1e30

LANES = 128
SUBLANES = 8
VMEM_LIMIT_BYTES = 56 * 1024 * 1024

TM_INPROJ = 1024
TN_INPROJ = 1024
SCAN_CHUNK = 256
ATT_QROWS = GRID_W
TM_POST = 256
TM_EXPERT = 256
TM_DISPATCH = 512
TM_COMBINE = 256

F32 = jnp.float32
BF16 = jnp.bfloat16


def _params(n_axes):
    return pltpu.CompilerParams(dimension_semantics=("arbitrary",) * n_axes,
                                vmem_limit_bytes=VMEM_LIMIT_BYTES)


def _layer_norm(x):
    mu = jnp.mean(x, axis=-1, keepdims=True)
    xc = x - mu
    var = jnp.mean(xc * xc, axis=-1, keepdims=True)
    return xc * lax.rsqrt(var + LN_EPS)


def _mod_kernel(c_ref, w_ref, b_ref, o_ref):
    c = c_ref[...]
    s = (c * jax.nn.sigmoid(c)).astype(BF16)
    o_ref[0] = jnp.dot(s, w_ref[0].astype(BF16), preferred_element_type=F32) + b_ref[0]


def _modulation(cc, w_mod, b_mod):
    n_out = w_mod.shape[-1]
    return pl.pallas_call(
        _mod_kernel,
        grid=(DEPTH, n_out // D_MODEL),
        in_specs=[pl.BlockSpec((SUBLANES, D_MODEL), lambda l, j: (0, 0)),
                  pl.BlockSpec((1, D_MODEL, D_MODEL), lambda l, j: (l, 0, j)),
                  pl.BlockSpec((1, 1, D_MODEL), lambda l, j: (l, 0, j))],
        out_specs=pl.BlockSpec((1, SUBLANES, D_MODEL), lambda l, j: (l, 0, j)),
        out_shape=jax.ShapeDtypeStruct((DEPTH, SUBLANES, n_out), F32),
        compiler_params=_params(2),
        name="modulation",
    )(cc, w_mod, b_mod.reshape(DEPTH, 1, n_out))


def _rope_half(x, cos, sin_signed):
    lane = lax.broadcasted_iota(jnp.int32, (x.shape[0], LANES), 1)
    first = (lane % 32) < 16
    outs = []
    for cidx in range(x.shape[1] // LANES):
        xc = x[:, cidx * LANES:(cidx + 1) * LANES]
        partner = jnp.where(first, pltpu.roll(xc, LANES - 16, 1), pltpu.roll(xc, 16, 1))
        outs.append(xc * cos + partner * sin_signed)
    return jnp.concatenate(outs, axis=1)


def _inproj_kernel(h_ref, sh_ref, sc_ref, w_ref, b_ref, cos_ref, sin_ref,
                   q_ref, k_ref, v_ref, sb_ref, rest_ref, xn_ref):
    j = pl.program_id(1)

    @pl.when(j == 0)
    def _():
        y = _layer_norm(h_ref[...])
        xn_ref[...] = (y * (1.0 + sc_ref[0]) + sh_ref[0]).astype(BF16)

    acc = jnp.dot(xn_ref[...], w_ref[...], preferred_element_type=F32) + b_ref[...]
    half = BRANCH_WIDTH

    @pl.when(j == 0)
    def _():
        cos = cos_ref[...]
        sin = sin_ref[...]
        q_ref[...] = _rope_half(acc[:, :half], cos, sin).astype(BF16)
        k_ref[...] = _rope_half(acc[:, half:], cos, sin).astype(BF16)

    @pl.when(j == 1)
    def _():
        v_ref[...] = acc[:, :half].astype(BF16)
        sb_ref[...] = acc[:, half:]

    @pl.when(j >= 2)
    def _():
        rest_ref[...] = acc


def _input_projection(h, mod3, mod_base, w_in_bf, b_in, cos_t, sin_t, n_lat, seq):
    m = h.shape[0]
    tm, tn = TM_INPROJ, TN_INPROJ
    n_lat_tiles = n_lat // tm
    tiles_per_seq = seq // tm
    n_groups_lat = n_lat // seq

    def group(i):
        return jnp.where(i < n_lat_tiles, i // tiles_per_seq, n_groups_lat)

    def rope_blk(i):
        return jnp.where(i < n_lat_tiles, i % tiles_per_seq, tiles_per_seq)

    half = BRANCH_WIDTH
    return pl.pallas_call(
        _inproj_kernel,
        grid=(m // tm, P_TOTAL // tn),
        in_specs=[pl.BlockSpec((tm, D_MODEL), lambda i, j: (i, 0)),
                  pl.BlockSpec((1, 1, D_MODEL), lambda i, j: (mod_base + group(i) * 6 + 0, 0, 0)),
                  pl.BlockSpec((1, 1, D_MODEL), lambda i, j: (mod_base + group(i) * 6 + 1, 0, 0)),
                  pl.BlockSpec((D_MODEL, tn), lambda i, j: (0, j)),
                  pl.BlockSpec((1, tn), lambda i, j: (0, j)),
                  pl.BlockSpec((tm, LANES), lambda i, j: (rope_blk(i), 0)),
                  pl.BlockSpec((tm, LANES), lambda i, j: (rope_blk(i), 0))],
        out_specs=[pl.BlockSpec((tm, half), lambda i, j: (i, 0)),
                   pl.BlockSpec((tm, half), lambda i, j: (i, 0)),
                   pl.BlockSpec((tm, half), lambda i, j: (i, 0)),
                   pl.BlockSpec((tm, half), lambda i, j: (i, 0)),
                   pl.BlockSpec((tm, tn), lambda i, j: (i, jnp.maximum(j - 2, 0)))],
        out_shape=[jax.ShapeDtypeStruct((m, half), BF16),
                   jax.ShapeDtypeStruct((m, half), BF16),
                   jax.ShapeDtypeStruct((m, half), BF16),
                   jax.ShapeDtypeStruct((m, half), F32),
                   jax.ShapeDtypeStruct((m, P_TOTAL - 2 * tn), F32)],
        scratch_shapes=[pltpu.VMEM((tm, D_MODEL), BF16)],
        compiler_params=_params(2),
        name="input_projection",
    )(h, mod3, mod3, w_in_bf, b_in.reshape(1, P_TOTAL), cos_t, sin_t)


def _make_rope(seq, tm):
    t = np.arange(seq)
    row_pos, col_pos = t // GRID_W, t % GRID_W
    d = np.arange(LANES) % NA_HEAD_DIM
    m = NA_HEAD_DIM // 4
    inv_freq = (ROPE_BASE ** (-jnp.arange(m, dtype=F32) / m))[d % m]
    pos = np.where((d < 2 * m)[None, :], row_pos[:, None], col_pos[:, None])
    ang = jnp.asarray(pos).astype(F32) * inv_freq[None, :]
    cos = jnp.cos(ang)
    sin = jnp.sin(ang)
    sin_signed = jnp.where(jnp.asarray((d % (2 * m)) < m)[None, :], -sin, sin)
    cos = jnp.concatenate([cos, jnp.ones((tm, LANES), F32)], axis=0)
    sin_signed = jnp.concatenate([sin_signed, jnp.zeros((tm, LANES), F32)], axis=0)
    return cos, sin_signed


def _scan_kernel(fblk, bblk, first, last, seqb,
                 sb_ref, scg_ref, sx_ref, lxf_ref,
                 scgp_ref, sxp_ref, lxfp_ref, scgn_ref, sxn_ref, lxfn_ref,
                 lxb_ref, lxbp_ref, lxbn_ref,
                 scw_ref, cw_ref, cb_ref, sp_ref, wr_ref, wi_ref, br_ref, bi_ref,
                 zb_ref, hf_ref, hb_ref,
                 a_s, b_s, hc_s, st_s, *, ch, n_ctx_items):
    it = pl.program_id(0)
    is_first = first[it] == 1
    is_last = last[it] == 1
    is_ctx = it < n_ctx_items
    b = seqb[it]
    width = BRANCH_WIDTH
    row = lax.broadcasted_iota(jnp.int32, (ch, width), 0)
    not_first = jnp.where(is_first, 0.0, 1.0).astype(F32)
    not_last = jnp.where(is_last, 0.0, 1.0).astype(F32)

    def back1(u, prev_row):
        return jnp.where(row == 0, prev_row, pltpu.roll(u, 1, 0))

    def back2(u, prev2, prev1):
        return jnp.where(row == 0, prev2, jnp.where(row == 1, prev1, pltpu.roll(u, 2, 0)))

    def fwd1(u, next_row):
        return jnp.where(row == ch - 1, next_row, pltpu.roll(u, ch - 1, 0))

    def lru_input(lx_ref, lxp_ref, lxn_ref, prev_ok, next_ok):
        x = lx_ref[...]
        p = lxp_ref[...] * prev_ok
        n = lxn_ref[...] * next_ok
        return (cw_ref[0:1] * back2(x, p[6:7], p[7:8]) + cw_ref[1:2] * back1(x, p[7:8])
                + cw_ref[2:3] * x + cw_ref[3:4] * fwd1(x, n[0:1]) + cb_ref[...])

    def coeffs(d, xm):
        xb = xm.astype(BF16)
        r = jax.nn.sigmoid(jnp.dot(xb, wr_ref[d], preferred_element_type=F32) + br_ref[d:d + 1])
        g = jax.nn.sigmoid(jnp.dot(xb, wi_ref[d], preferred_element_type=F32) + bi_ref[d:d + 1])
        log_a = (-LRU_C * sp_ref[d:d + 1]) * r
        a = jnp.exp(log_a)
        a_s[d] = a
        b_s[d] = jnp.sqrt(-jnp.tanh(log_a) * (a * a + 1.0)) * (g * xm)

    u = scg_ref[...] * sx_ref[...]
    u_prev = scgp_ref[7:8] * sxp_ref[7:8] * not_first
    u_next = scgn_ref[0:1] * sxn_ref[0:1] * not_last
    conv = scw_ref[0:1] * back1(u, u_prev) + scw_ref[1:2] * u + scw_ref[2:3] * fwd1(u, u_next)
    zb_ref[...] = (sb_ref[...] * conv).astype(BF16)

    coeffs(0, lru_input(lxf_ref, lxfp_ref, lxfn_ref, not_first, not_last))
    coeffs(1, lru_input(lxb_ref, lxbp_ref, lxbn_ref, not_last, not_first))

    @pl.when(jnp.logical_and(is_first, is_ctx))
    def _():
        hc_s[...] = jnp.zeros_like(hc_s)

    @pl.when(jnp.logical_and(is_first, jnp.logical_not(is_ctx)))
    def _():
        hc_s[0:1] = st_s[pl.ds(2 * b, 1), :]
        hc_s[1:2] = st_s[pl.ds(2 * b + 1, 1), :]

    def body(i, carry):
        hf, hb = carry
        base = i * SUBLANES
        for r in range(SUBLANES):
            t = base + r
            hf = a_s[0, pl.ds(t, 1), :] * hf + b_s[0, pl.ds(t, 1), :]
            hf_ref[pl.ds(t, 1), :] = hf
            tb = ch - 1 - t
            hb = a_s[1, pl.ds(tb, 1), :] * hb + b_s[1, pl.ds(tb, 1), :]
            hb_ref[pl.ds(tb, 1), :] = hb
        return hf, hb

    hf, hb = lax.fori_loop(0, ch // SUBLANES, body, (hc_s[0:1], hc_s[1:2]))
    hc_s[0:1] = hf
    hc_s[1:2] = hb

    @pl.when(is_ctx)
    def _():
        st_s[pl.ds(2 * b, 1), :] = hf
        st_s[pl.ds(2 * b + 1, 1), :] = hb


def _scan_tables(n_batch, seq, n_ctx, ch):
    assert n_ctx == ch
    nc = seq // ch
    ctx0 = n_batch * seq // ch
    fblk, bblk, first, last, seqb = [], [], [], [], []
    for b in range(n_batch):
        fblk.append(ctx0 + b); bblk.append(ctx0 + b); first.append(1); last.append(1); seqb.append(b)
    for b in range(n_batch):
        for c in range(nc):
            fblk.append(b * nc + c); bblk.append(b * nc + nc - 1 - c)
            first.append(int(c == 0)); last.append(int(c == nc - 1)); seqb.append(b)
    return [np.asarray(a, np.int32) for a in (fblk, bblk, first, last, seqb)]


def _conv_scan(sb, rest, sc_w, lru_cw, lru_cb, sp, wr_bd, wi_bd, b_r, b_i, n_batch, seq, n_ctx):
    m = sb.shape[0]
    ch = SCAN_CHUNK
    width = BRANCH_WIDTH
    tables = _scan_tables(n_batch, seq, n_ctx, ch)
    n_items = len(tables[0])
    halo_per_chunk = ch // SUBLANES
    last_halo = m // SUBLANES - 1

    def cur(col, which):
        return pl.BlockSpec((ch, width), lambda i, f, bk, *_: ((f, bk)[which][i], col))

    def prev(col, which):
        return pl.BlockSpec((SUBLANES, width),
                            lambda i, f, bk, *_: (jnp.maximum((f, bk)[which][i] * halo_per_chunk - 1, 0), col))

    def nxt(col, which):
        return pl.BlockSpec((SUBLANES, width),
                            lambda i, f, bk, *_: (jnp.minimum(((f, bk)[which][i] + 1) * halo_per_chunk, last_halo), col))

    def full(shape):
        return pl.BlockSpec(shape, lambda i, *_: (0,) * len(shape))

    in_specs = [cur(0, 0), cur(0, 0), cur(1, 0), cur(2, 0),
                prev(0, 0), prev(1, 0), prev(2, 0), nxt(0, 0), nxt(1, 0), nxt(2, 0),
                cur(2, 1), prev(2, 1), nxt(2, 1),
                full(sc_w.shape), full(lru_cw.shape), full((1, width)), full(sp.shape),
                full(wr_bd.shape), full(wi_bd.shape), full(b_r.shape), full(b_i.shape)]
    out_specs = [cur(0, 0), cur(0, 0), cur(0, 1)]
    grid_spec = pltpu.PrefetchScalarGridSpec(
        num_scalar_prefetch=5, grid=(n_items,), in_specs=in_specs, out_specs=out_specs,
        scratch_shapes=[pltpu.VMEM((2, ch, width), F32), pltpu.VMEM((2, ch, width), F32),
                        pltpu.VMEM((SUBLANES, width), F32), pltpu.VMEM((2 * n_batch, width), F32)])
    return pl.pallas_call(
        functools.partial(_scan_kernel, ch=ch, n_ctx_items=n_batch),
        grid_spec=grid_spec,
        out_shape=[jax.ShapeDtypeStruct((m, width), BF16),
                   jax.ShapeDtypeStruct((m, width), F32),
                   jax.ShapeDtypeStruct((m, width), F32)],
        compiler_params=_params(1),
        name="conv_scan",
    )(*[jnp.asarray(t) for t in tables],
      sb, rest, rest, rest, rest, rest, rest, rest, rest, rest, rest, rest, rest,
      sc_w, lru_cw, lru_cb.reshape(1, width), sp, wr_bd, wi_bd, b_r, b_i)


def _attention_kernel(qblk, bat, r0t, cls,
                      q_ref, k_ref, v_ref, kc_ref, vc_ref, bias_ref, o_ref, *, band):
    it = pl.program_id(0)
    start = pl.multiple_of(r0t[it] * GRID_W, GRID_W)
    nq = q_ref.shape[0]
    lane = lax.broadcasted_iota(jnp.int32, (nq, LANES), 1)
    low = lane < NA_HEAD_DIM
    scale = NA_HEAD_DIM ** -0.5
    nt = (((1,), (1,)), ((), ()))
    for hp in range(NA_HEADS // 2):
        cols = slice(hp * LANES, (hp + 1) * LANES)
        qp = q_ref[:, cols].astype(F32)
        qs = jnp.concatenate([jnp.where(low, qp, 0.0), jnp.where(low, 0.0, qp)], axis=0).astype(BF16)
        kb = k_ref[pl.ds(start, band), cols]
        vb = v_ref[pl.ds(start, band), cols]
        s_loc = lax.dot_general(qs, kb, nt, preferred_element_type=F32)
        s_ctx = lax.dot_general(qs, kc_ref[:, cols], nt, preferred_element_type=F32)
        bias = jnp.concatenate([bias_ref[0, 2 * hp], bias_ref[0, 2 * hp + 1]], axis=0)
        s_loc = s_loc * scale + bias
        s_ctx = s_ctx * scale
        mx = jnp.maximum(jnp.max(s_loc, axis=-1, keepdims=True), jnp.max(s_ctx, axis=-1, keepdims=True))
        e_loc = jnp.exp(s_loc - mx)
        e_ctx = jnp.exp(s_ctx - mx)
        den = jnp.sum(e_loc, axis=-1, keepdims=True) + jnp.sum(e_ctx, axis=-1, keepdims=True)
        o = (jnp.dot(e_loc.astype(BF16), vb, preferred_element_type=F32)
             + jnp.dot(e_ctx.astype(BF16), vc_ref[:, cols], preferred_element_type=F32)) / den
        o_ref[:, cols] = jnp.where(low, o[:nq], o[nq:]).astype(BF16)


def _attention_tables(n_batch, seq, n_ctx, with_ctx_queries):
    rows = seq // GRID_W
    kr = min(NA_WIN_ROWS, rows)
    qblk, bat, r0t, cls = [], [], [], []
    ctx_q0 = n_batch * seq // ATT_QROWS
    for b in range(n_batch):
        for r in range(rows):
            r0 = min(max(r - kr // 2, 0), rows - kr)
            qblk.append(b * rows + r); bat.append(b); r0t.append(r0); cls.append(r - r0)
        if with_ctx_queries:
            for c in range(n_ctx // ATT_QROWS):
                qblk.append(ctx_q0 + b * (n_ctx // ATT_QROWS) + c); bat.append(b); r0t.append(0); cls.append(kr)
    return [np.asarray(a, np.int32) for a in (qblk, bat, r0t, cls)]


def _attention_bias(rpb, seq):
    rows = seq // GRID_W
    kr = min(NA_WIN_ROWS, rows)
    kc = NA_WIN_COLS
    cq = np.arange(GRID_W)
    c0 = np.clip(cq - kc // 2, 0, GRID_W - kc)
    ck = np.arange(GRID_W)
    inside = (ck[None, :] >= c0[:, None]) & (ck[None, :] < c0[:, None] + kc)
    dc = np.clip(ck[None, :] - cq[:, None] + (NA_WIN_COLS - 1), 0, 2 * NA_WIN_COLS - 2)
    n_dr = 2 * NA_WIN_ROWS - 1
    table = jnp.where(jnp.asarray(inside)[None, None], rpb[:, :, dc], NEG_BIG)
    table = table.transpose(0, 2, 1, 3).reshape(NA_HEADS, GRID_W, n_dr * GRID_W)
    classes = []
    for cl in range(kr):
        lo = (NA_WIN_ROWS - 1 - cl) * GRID_W
        classes.append(table[:, :, lo:lo + kr * GRID_W])
    classes.append(jnp.full((NA_HEADS, GRID_W, kr * GRID_W), NEG_BIG, F32))
    return jnp.stack(classes, axis=0)


def _attention(q, k, v, bias, n_batch, seq, n_ctx, with_ctx_queries):
    m = q.shape[0] if with_ctx_queries else n_batch * seq
    rows = seq // GRID_W
    kr = min(NA_WIN_ROWS, rows)
    band = kr * GRID_W
    tables = _attention_tables(n_batch, seq, n_ctx, with_ctx_queries)
    n_items = len(tables[0])
    ctx_blk0 = n_batch * seq // n_ctx
    width = NA_WIDTH
    in_specs = [pl.BlockSpec((ATT_QROWS, width), lambda i, qb, bt, r0, cl: (qb[i], 0)),
                pl.BlockSpec((seq, width), lambda i, qb, bt, r0, cl: (bt[i], 0)),
                pl.BlockSpec((seq, width), lambda i, qb, bt, r0, cl: (bt[i], 0)),
                pl.BlockSpec((n_ctx, width), lambda i, qb, bt, r0, cl: (ctx_blk0 + bt[i], 0)),
                pl.BlockSpec((n_ctx, width), lambda i, qb, bt, r0, cl: (ctx_blk0 + bt[i], 0)),
                pl.BlockSpec((1, NA_HEADS, GRID_W, band), lambda i, qb, bt, r0, cl: (cl[i], 0, 0, 0))]
    out_specs = pl.BlockSpec((ATT_QROWS, width), lambda i, qb, bt, r0, cl: (qb[i], 0))
    grid_spec = pltpu.PrefetchScalarGridSpec(num_scalar_prefetch=4, grid=(n_items,),
                                             in_specs=in_specs, out_specs=out_specs)
    return pl.pallas_call(
        functools.partial(_attention_kernel, band=band),
        grid_spec=grid_spec,
        out_shape=jax.ShapeDtypeStruct((m, width), BF16),
        compiler_params=_params(1),
        name="attention",
    )(*[jnp.asarray(t) for t in tables], q, k, v, k, v, bias)


def _post_kernel(h_ref, att_ref, zb_ref, hf_ref, hb_ref, lg_ref, gl0_ref, gl1_ref, gl2_ref,
                 g1_ref, sh2_ref, sc2_ref,
                 wpa_ref, wpc_ref, wpl_ref, wo_ref, bo_ref, l1g_ref, l1b_ref, rw_ref, rb_ref, tri_ref,
                 h1_ref, u2_ref, eidx_ref, pw_ref, rank_ref, cnt_ref, base_s):
    i = pl.program_id(0)

    @pl.when(i == 0)
    def _():
        base_s[...] = jnp.zeros_like(base_s)

    y_a = jnp.dot(att_ref[...], wpa_ref[...], preferred_element_type=F32)
    y_b = jnp.dot(zb_ref[...], wpc_ref[...], preferred_element_type=F32)
    zc = jax.nn.gelu(lg_ref[...]) * (hf_ref[...] + hb_ref[...])
    y_c = jnp.dot(zc.astype(BF16), wpl_ref[...], preferred_element_type=F32)
    merged = (jax.nn.sigmoid(gl0_ref[...]) * y_a + jax.nn.sigmoid(gl1_ref[...]) * y_b
              + jax.nn.sigmoid(gl2_ref[...]) * y_c)
    y = jnp.dot(merged.astype(BF16), wo_ref[...], preferred_element_type=F32) + bo_ref[...]
    h1 = _layer_norm(DEEPNORM_ALPHA * h_ref[...] + g1_ref[0] * y) * l1g_ref[...] + l1b_ref[...]
    h1_ref[...] = h1
    u2 = _layer_norm(h1) * (1.0 + sc2_ref[0]) + sh2_ref[0]
    u2_ref[...] = u2

    logits = jnp.dot(u2, rw_ref[...], precision=lax.Precision.HIGHEST, preferred_element_type=F32) + rb_ref[...]
    tm = logits.shape[0]
    lane = lax.broadcasted_iota(jnp.int32, (tm, LANES), 1)
    lane_f = lane.astype(F32)
    work = logits
    tops, idxs, hots = [], [], []
    for _ in range(TOP_K):
        mx = jnp.max(work, axis=-1, keepdims=True)
        idx = jnp.min(jnp.where(work == mx, lane_f, float(LANES)), axis=-1, keepdims=True)
        hot = lane_f == idx
        work = jnp.where(hot, -3e38, work)
        tops.append(mx); idxs.append(idx); hots.append(hot)
    exps = [jnp.exp(t - tops[0]) for t in tops]
    den = exps[0] + exps[1] + exps[2] + exps[3]
    hot_all = jnp.zeros((tm, LANES), F32)
    for hot in hots:
        hot_all = hot_all + hot.astype(F32)
    before = base_s[0:1] + jnp.dot(tri_ref[...], hot_all.astype(BF16), preferred_element_type=F32)
    eidx = jnp.zeros((tm, LANES), F32)
    pw = jnp.zeros((tm, LANES), F32)
    rank = jnp.zeros((tm, LANES), F32)
    for kk in range(TOP_K):
        sel = lane == kk
        rk = jnp.sum(jnp.where(hots[kk], before, 0.0), axis=-1, keepdims=True)
        eidx = jnp.where(sel, idxs[kk], eidx)
        pw = jnp.where(sel, exps[kk] / den, pw)
        rank = jnp.where(sel, rk, rank)
    eidx_ref[...] = eidx.astype(jnp.int32)
    pw_ref[...] = pw
    rank_ref[...] = rank.astype(jnp.int32)
    new_base = base_s[0:1] + jnp.sum(hot_all, axis=0, keepdims=True)
    base_s[0:1] = new_base
    cnt_ref[...] = jnp.broadcast_to(new_base, cnt_ref.shape)


def _post_mixer(h, att, zb, hf, hb, rest, mod3, mod_base, wpa, wpc, wpl, wo, b_o, l1g, l1b, rw_pad, rb_pad,
                n_rows, n_lat, seq):
    tm = TM_POST
    width = BRANCH_WIDTH
    tiles_per_seq = seq // tm
    n_lat_tiles = n_lat // tm
    n_groups_lat = n_lat // seq
    tri = jnp.asarray(np.tril(np.ones((tm, tm), np.float32), -1), BF16)

    def group(i):
        return jnp.where(i < n_lat_tiles, i // tiles_per_seq, n_groups_lat)

    def rows(wd, col=0):
        return pl.BlockSpec((tm, wd), lambda i: (i, col))

    def full(shape):
        return pl.BlockSpec(shape, lambda i: (0,) * len(shape))

    def mod(which):
        return pl.BlockSpec((1, 1, D_MODEL), lambda i: (mod_base + group(i) * 6 + which, 0, 0))

    in_specs = [rows(D_MODEL), rows(width), rows(width), rows(width), rows(width), rows(width, 3),
                rows(D_MODEL, 2), rows(D_MODEL, 3), rows(D_MODEL, 4),
                mod(2), mod(3), mod(4),
                full(wpa.shape), full(wpc.shape), full(wpl.shape), full(wo.shape), full((1, D_MODEL)),
                full((1, D_MODEL)), full((1, D_MODEL)), full(rw_pad.shape), full(rb_pad.shape), full(tri.shape)]
    out_specs = [rows(D_MODEL), rows(D_MODEL), rows(LANES), rows(LANES), rows(LANES),
                 pl.BlockSpec((SUBLANES, LANES), lambda i: (0, 0))]
    return pl.pallas_call(
        _post_kernel,
        grid=(n_rows // tm,),
        in_specs=in_specs, out_specs=out_specs,
        out_shape=[jax.ShapeDtypeStruct((n_rows, D_MODEL), F32), jax.ShapeDtypeStruct((n_rows, D_MODEL), F32),
                   jax.ShapeDtypeStruct((n_rows, LANES), jnp.int32), jax.ShapeDtypeStruct((n_rows, LANES), F32),
                   jax.ShapeDtypeStruct((n_rows, LANES), jnp.int32), jax.ShapeDtypeStruct((SUBLANES, LANES), F32)],
        scratch_shapes=[pltpu.VMEM((SUBLANES, LANES), F32)],
        compiler_params=_params(1),
        name="post_mixer",
    )(h, att, zb, hf, hb, rest, rest, rest, rest, mod3, mod3, mod3,
      wpa, wpc, wpl, wo, b_o.reshape(1, D_MODEL), l1g.reshape(1, D_MODEL), l1b.reshape(1, D_MODEL),
      rw_pad, rb_pad, tri)


def _row_copy(src, src_row, dst, dst_row, sem):
    return pltpu.make_async_copy(src.at[pl.ds(src_row, 1)], dst.at[pl.ds(dst_row, 1)], sem)


def _dispatch_kernel(cend, pend, n_act, dest_ref, u_ref, xs_hbm, zrow, sem, *, tm, n_tiles):
    i = pl.program_id(0)

    @pl.when(i == 0)
    def _():
        zrow[...] = jnp.zeros_like(zrow)

        def tile_copy(j):
            return pltpu.make_async_copy(zrow, xs_hbm.at[pl.ds(j * TM_EXPERT, TM_EXPERT)], sem)

        def start_tile(j, c):
            tile_copy(j).start()
            return c

        def wait_tile(j, c):
            tile_copy(j).wait()
            return c

        lax.fori_loop(n_act[0], n_tiles, start_tile, 0)
        lax.fori_loop(n_act[0], n_tiles, wait_tile, 0)

        def per_expert(e, carry):
            lo, hi = cend[e], pend[e]

            def start(s, c):
                _row_copy(zrow, 0, xs_hbm, s, sem).start()
                return c

            def wait(s, c):
                _row_copy(zrow, 0, xs_hbm, s, sem).wait()
                return c

            lax.fori_loop(lo, hi, start, 0)
            lax.fori_loop(lo, hi, wait, 0)
            return carry

        lax.fori_loop(0, N_EXPERTS, per_expert, 0)

    def start(r, c):
        for kk in range(TOP_K):
            _row_copy(u_ref, r, xs_hbm, dest_ref[TOP_K * r + kk], sem).start()
        return c

    def wait(r, c):
        for kk in range(TOP_K):
            _row_copy(u_ref, r, xs_hbm, dest_ref[TOP_K * r + kk], sem).wait()
        return c

    lax.fori_loop(0, tm, start, 0, unroll=8)
    lax.fori_loop(0, tm, wait, 0, unroll=8)


def _dispatch(u2, dest_flat, cend, pend, n_act, n_slots):
    n_rows = u2.shape[0]
    tm = TM_DISPATCH
    grid_spec = pltpu.PrefetchScalarGridSpec(
        num_scalar_prefetch=3, grid=(n_rows // tm,),
        in_specs=[pl.BlockSpec((tm * TOP_K,), lambda i, *_: (i,), memory_space=pltpu.SMEM),
                  pl.BlockSpec((tm, D_MODEL), lambda i, *_: (i, 0))],
        out_specs=pl.BlockSpec(memory_space=pl.ANY),
        scratch_shapes=[pltpu.VMEM((TM_EXPERT, D_MODEL), F32), pltpu.SemaphoreType.DMA(())])
    return pl.pallas_call(
        functools.partial(_dispatch_kernel, tm=tm, n_tiles=n_slots // TM_EXPERT),
        grid_spec=grid_spec,
        out_shape=jax.ShapeDtypeStruct((n_slots, D_MODEL), F32),
        compiler_params=_params(1),
        name="moe_dispatch",
    )(cend, pend, n_act, dest_flat, u2)


def _expert_kernel(tile_e, n_act, xs_ref, wgu_ref, bgu_ref, wdn_ref, bdn_ref, ys_ref, wgu_s, wdn_s):
    j = pl.program_id(0)

    @pl.when(j < n_act[0])
    def _():
        e = tile_e[j]
        e_prev = tile_e[jnp.maximum(j - 1, 0)]

        @pl.when(jnp.logical_or(j == 0, e != e_prev))
        def _():
            wgu_s[...] = wgu_ref[0, 0].astype(BF16)
            wdn_s[...] = wdn_ref[0, 0].astype(BF16)

        x = xs_ref[...].astype(BF16)
        gu = jnp.dot(x, wgu_s[...], preferred_element_type=F32) + bgu_ref[0, 0]
        f = gu.shape[1] // 2
        gate = jnp.minimum(gu[:, :f], SWIGLU_LIMIT)
        up = jnp.clip(gu[:, f:], -SWIGLU_LIMIT, SWIGLU_LIMIT)
        hid = (up + 1.0) * gate * jax.nn.sigmoid(SWIGLU_ALPHA * gate)
        ys_ref[...] = jnp.dot(hid.astype(BF16), wdn_s[...], preferred_element_type=F32) + bdn_ref[0, 0]

    @pl.when(j >= n_act[0])
    def _():
        ys_ref[...] = jnp.zeros_like(ys_ref)


def _experts(xs, tile_e, n_act, layer, w_gu, b_gu, w_dn, b_dn):
    n_slots = xs.shape[0]
    tm = TM_EXPERT
    n_tiles = n_slots // tm
    f2 = w_gu.shape[-1]

    def tile(j, te, na):
        return jnp.minimum(j, na[0] - 1)

    def expert(j, te, na):
        return (layer, te[tile(j, te, na)], 0, 0)

    in_specs = [pl.BlockSpec((tm, D_MODEL), lambda j, te, na: (tile(j, te, na), 0)),
                pl.BlockSpec((1, 1, D_MODEL, f2), expert),
                pl.BlockSpec((1, 1, 1, f2), expert),
                pl.BlockSpec((1, 1, f2 // 2, D_MODEL), expert),
                pl.BlockSpec((1, 1, 1, D_MODEL), expert)]
    out_specs = pl.BlockSpec((tm, D_MODEL), lambda j, te, na: (j, 0))
    grid_spec = pltpu.PrefetchScalarGridSpec(
        num_scalar_prefetch=2, grid=(n_tiles,), in_specs=in_specs, out_specs=out_specs,
        scratch_shapes=[pltpu.VMEM((D_MODEL, f2), BF16), pltpu.VMEM((f2 // 2, D_MODEL), BF16)])
    return pl.pallas_call(
        _expert_kernel,
        grid_spec=grid_spec,
        out_shape=jax.ShapeDtypeStruct((n_slots, D_MODEL), F32),
        compiler_params=_params(1),
        name="moe_experts",
    )(tile_e, n_act, xs, w_gu, b_gu.reshape(DEPTH, N_EXPERTS, 1, f2), w_dn,
      b_dn.reshape(DEPTH, N_EXPERTS, 1, D_MODEL))


def _combine_kernel(dest_ref, ys_hbm, pw_ref, h1_ref, g2_ref, l2g_ref, l2b_ref, o_ref, buf, sem, *, tm):
    def start(r, c):
        for kk in range(TOP_K):
            _row_copy(ys_hbm, dest_ref[TOP_K * r + kk], buf.at[kk], r, sem).start()
        return c

    def wait(r, c):
        for kk in range(TOP_K):
            _row_copy(ys_hbm, dest_ref[TOP_K * r + kk], buf.at[kk], r, sem).wait()
        return c

    lax.fori_loop(0, tm, start, 0)
    lax.fori_loop(0, tm, wait, 0)
    pw = pw_ref[...]
    y2 = pw[:, 0:1] * buf[0]
    for kk in range(1, TOP_K):
        y2 = y2 + pw[:, kk:kk + 1] * buf[kk]
    o_ref[...] = _layer_norm(DEEPNORM_ALPHA * h1_ref[...] + g2_ref[0] * y2) * l2g_ref[...] + l2b_ref[...]


def _combine(ys, dest_flat, pw, h1, mod3, mod_base, l2g, l2b, n_lat, seq):
    n_rows = h1.shape[0]
    tm = TM_COMBINE
    tiles_per_seq = seq // tm
    n_lat_tiles = n_lat // tm
    n_groups_lat = n_lat // seq

    def group(i):
        return jnp.where(i < n_lat_tiles, i // tiles_per_seq, n_groups_lat)

    in_specs = [pl.BlockSpec((tm * TOP_K,), lambda i: (i,), memory_space=pltpu.SMEM),
                pl.BlockSpec(memory_space=pl.ANY),
                pl.BlockSpec((tm, LANES), lambda i: (i, 0)),
                pl.BlockSpec((tm, D_MODEL), lambda i: (i, 0)),
                pl.BlockSpec((1, 1, D_MODEL), lambda i: (mod_base + group(i) * 6 + 5, 0, 0)),
                pl.BlockSpec((1, D_MODEL), lambda i: (0, 0)),
                pl.BlockSpec((1, D_MODEL), lambda i: (0, 0))]
    return pl.pallas_call(
        functools.partial(_combine_kernel, tm=tm),
        grid=(n_rows // tm,),
        in_specs=in_specs,
        out_specs=pl.BlockSpec((tm, D_MODEL), lambda i: (i, 0)),
        out_shape=jax.ShapeDtypeStruct((n_rows, D_MODEL), F32),
        scratch_shapes=[pltpu.VMEM((TOP_K, tm, D_MODEL), F32), pltpu.SemaphoreType.DMA(())],
        compiler_params=_params(1),
        name="moe_combine",
    )(dest_flat, ys, pw, h1, mod3, l2g.reshape(1, D_MODEL), l2b.reshape(1, D_MODEL))


def _routing_plan(eidx, rank, counts_f, n_rows):
    tm = TM_EXPERT
    counts = counts_f[0, :N_EXPERTS].astype(jnp.int32)
    padded = (counts + tm - 1) // tm * tm
    pad_end = jnp.cumsum(padded)
    pad_start = pad_end - padded
    dest = pad_start[eidx[:, :TOP_K]] + rank[:, :TOP_K]
    n_tiles = n_rows * TOP_K // tm + N_EXPERTS
    tile_row0 = jnp.arange(n_tiles, dtype=jnp.int32) * tm
    tile_e = jnp.minimum(jnp.sum((pad_end[None, :] <= tile_row0[:, None]).astype(jnp.int32), axis=1),
                         N_EXPERTS - 1)
    n_act = (pad_end[-1:] // tm).astype(jnp.int32)
    return dest.reshape(-1).astype(jnp.int32), (pad_start + counts).astype(jnp.int32), pad_end.astype(jnp.int32), \
        tile_e, n_act, n_tiles * tm


def _block_diag(w):
    two, n, d, e = w.shape
    eye = jnp.eye(n, dtype=w.dtype)
    return (w[:, :, :, None, :] * eye[None, :, None, :, None]).reshape(two, n * d, n * e)


def kernel(x, c, ctx, c_ctx, w_mod, b_mod, w_in, b_in, na_rpb, w_proj_attn, w_proj_conv, w_proj_lru, sc_conv_w, lru_conv_w, lru_conv_b, lru_lambda, lru_w_r, lru_b_r, lru_w_i, lru_b_i, w_o, b_o, ln1_g, ln1_b, router_w, router_b, exp_w_gu, exp_b_gu, exp_w_dn, exp_b_dn, ln2_g, ln2_b):
    n_batch, seq, d = x.shape
    n_ctx = ctx.shape[1]
    n_lat = n_batch * seq
    n_all = n_lat + n_batch * n_ctx
    assert d == D_MODEL and n_batch + 1 <= SUBLANES

    cc = jnp.concatenate([c, c_ctx[None], jnp.zeros((SUBLANES - n_batch - 1, d), F32)], axis=0)
    mod = _modulation(cc, w_mod, b_mod)
    groups = n_batch + 1
    mod3 = mod.reshape(DEPTH, SUBLANES, 6, d)[:, :groups].reshape(DEPTH * groups * 6, 1, d)

    cos_t, sin_t = _make_rope(seq, TM_INPROJ)
    h = jnp.concatenate([x.reshape(n_lat, d), ctx.reshape(n_batch * n_ctx, d)], axis=0)

    for layer in range(DEPTH):
        last = layer == DEPTH - 1
        mod_base = layer * groups * 6
        q, k, v, sb, rest = _input_projection(h, mod3, mod_base, w_in[layer].astype(BF16), b_in[layer],
                                              cos_t, sin_t, n_lat, seq)
        sp = jax.nn.softplus(-lru_lambda[layer])
        zb, hf, hb = _conv_scan(sb, rest, sc_conv_w[layer], lru_conv_w[layer], lru_conv_b[layer], sp,
                                _block_diag(lru_w_r[layer]).astype(BF16), _block_diag(lru_w_i[layer]).astype(BF16),
                                lru_b_r[layer], lru_b_i[layer], n_batch, seq, n_ctx)
        att = _attention(q, k, v, _attention_bias(na_rpb[layer], seq), n_batch, seq, n_ctx, not last)
        n_rows = n_lat if last else n_all
        rw_pad = jnp.pad(router_w[layer], ((0, 0), (0, LANES - N_EXPERTS)))
        rb_pad = jnp.concatenate([router_b[layer], jnp.full((LANES - N_EXPERTS,), NEG_BIG, F32)]).reshape(1, LANES)
        h1, u2, eidx, pw, rank, counts = _post_mixer(
            h, att, zb, hf, hb, rest, mod3, mod_base,
            w_proj_attn[layer].astype(BF16), w_proj_conv[layer].astype(BF16), w_proj_lru[layer].astype(BF16),
            w_o[layer].astype(BF16), b_o[layer], ln1_g[layer], ln1_b[layer], rw_pad, rb_pad, n_rows, n_lat, seq)
        dest, cend, pend, tile_e, n_act, n_slots = _routing_plan(eidx, rank, counts, n_rows)
        xs = _dispatch(u2, dest, cend, pend, n_act, n_slots)
        ys = _experts(xs, tile_e, n_act, layer, exp_w_gu, exp_b_gu, exp_w_dn, exp_b_dn)
        h = _combine(ys, dest, pw, h1, mod3, mod_base, ln2_g[layer], ln2_b[layer], n_lat, seq)
    return h.reshape(n_batch, seq, d)
```

```python
import functools

import numpy as np
import jax
import jax.numpy as jnp
from jax import lax
from jax.experimental import pallas as pl
from jax.experimental.pallas import tpu as pltpu

D_MODEL = 1024
DEPTH = 2
GRID_W = 64
NA_HEADS = 8
NA_HEAD_DIM = 64
NA_WIDTH = NA_HEADS * NA_HEAD_DIM
NA_WIN_ROWS = 8
NA_WIN_COLS = 16
ROPE_BASE = 10000.0
BRANCH_WIDTH = 512
LRU_BLOCKS = 8
LRU_C = 8.0
P_TOTAL = 7168
N_EXPERTS = 32
TOP_K = 4
SWIGLU_LIMIT = 7.0
SWIGLU_ALPHA = 1.702
LN_EPS = 1e-5
DEEPNORM_ALPHA = (2 * DEPTH) ** 0.25
NEG_BIG = -1e30

LANES = 128
SUBLANES = 8
VMEM_LIMIT_BYTES = 56 * 1024 * 1024

TM_INPROJ = 1024
TN_INPROJ = 1024
SCAN_CHUNK = 256
ATT_QROWS = GRID_W
TM_POST = 256
TM_EXPERT = 512
XS_WIDTH = D_MODEL + LANES
TOK_BLOCK = TM_POST * TOP_K + N_EXPERTS * SUBLANES
ZERO_ROWS = 256

F32 = jnp.float32
BF16 = jnp.bfloat16


def _params(n_axes):
    return pltpu.CompilerParams(dimension_semantics=("arbitrary",) * n_axes,
                                vmem_limit_bytes=VMEM_LIMIT_BYTES)


def _layer_norm(x):
    mu = jnp.mean(x, axis=-1, keepdims=True)
    xc = x - mu
    var = jnp.mean(xc * xc, axis=-1, keepdims=True)
    return xc * lax.rsqrt(var + LN_EPS)


def _mod_kernel(c_ref, w_ref, b_ref, o_ref):
    c = c_ref[...]
    s = (c * jax.nn.sigmoid(c)).astype(BF16)
    o_ref[0] = jnp.dot(s, w_ref[0].astype(BF16), preferred_element_type=F32) + b_ref[0]


def _modulation(cc, w_mod, b_mod):
    n_out = w_mod.shape[-1]
    return pl.pallas_call(
        _mod_kernel,
        grid=(DEPTH, n_out // D_MODEL),
        in_specs=[pl.BlockSpec((SUBLANES, D_MODEL), lambda l, j: (0, 0)),
                  pl.BlockSpec((1, D_MODEL, D_MODEL), lambda l, j: (l, 0, j)),
                  pl.BlockSpec((1, 1, D_MODEL), lambda l, j: (l, 0, j))],
        out_specs=pl.BlockSpec((1, SUBLANES, D_MODEL), lambda l, j: (l, 0, j)),
        out_shape=jax.ShapeDtypeStruct((DEPTH, SUBLANES, n_out), F32),
        compiler_params=_params(2),
        name="modulation",
    )(cc, w_mod, b_mod.reshape(DEPTH, 1, n_out))


def _rope_half(x, cos, sin_signed):
    lane = lax.broadcasted_iota(jnp.int32, (x.shape[0], LANES), 1)
    first = (lane % 32) < 16
    outs = []
    for cidx in range(x.shape[1] // LANES):
        xc = x[:, cidx * LANES:(cidx + 1) * LANES]
        partner = jnp.where(first, pltpu.roll(xc, LANES - 16, 1), pltpu.roll(xc, 16, 1))
        outs.append(xc * cos + partner * sin_signed)
    return jnp.concatenate(outs, axis=1)


def _inproj_kernel(h_ref, sh_ref, sc_ref, w_ref, b_ref, cos_ref, sin_ref,
                   q_ref, k_ref, v_ref, sb_ref, rest_ref, xn_ref):
    j = pl.program_id(1)

    @pl.when(j == 0)
    def _():
        y = _layer_norm(h_ref[...])
        xn_ref[...] = (y * (1.0 + sc_ref[0]) + sh_ref[0]).astype(BF16)

    acc = jnp.dot(xn_ref[...], w_ref[...], preferred_element_type=F32) + b_ref[...]
    half = BRANCH_WIDTH

    @pl.when(j == 0)
    def _():
        cos = cos_ref[...]
        sin = sin_ref[...]
        q_ref[...] = _rope_half(acc[:, :half], cos, sin).astype(BF16)
        k_ref[...] = _rope_half(acc[:, half:], cos, sin).astype(BF16)

    @pl.when(j == 1)
    def _():
        v_ref[...] = acc[:, :half].astype(BF16)
        sb_ref[...] = acc[:, half:]

    @pl.when(j >= 2)
    def _():
        rest_ref[...] = acc


def _input_projection(h, mod3, mod_base, w_in_bf, b_in, cos_t, sin_t, n_lat, seq):
    m = h.shape[0]
    tm, tn = TM_INPROJ, TN_INPROJ
    n_lat_tiles = n_lat // tm
    tiles_per_seq = seq // tm
    n_groups_lat = n_lat // seq

    def group(i):
        return jnp.where(i < n_lat_tiles, i // tiles_per_seq, n_groups_lat)

    def rope_blk(i):
        return jnp.where(i < n_lat_tiles, i % tiles_per_seq, tiles_per_seq)

    half = BRANCH_WIDTH
    return pl.pallas_call(
        _inproj_kernel,
        grid=(m // tm, P_TOTAL // tn),
        in_specs=[pl.BlockSpec((tm, D_MODEL), lambda i, j: (i, 0)),
                  pl.BlockSpec((1, 1, D_MODEL), lambda i, j: (mod_base + group(i) * 6 + 0, 0, 0)),
                  pl.BlockSpec((1, 1, D_MODEL), lambda i, j: (mod_base + group(i) * 6 + 1, 0, 0)),
                  pl.BlockSpec((D_MODEL, tn), lambda i, j: (0, j)),
                  pl.BlockSpec((1, tn), lambda i, j: (0, j)),
                  pl.BlockSpec((tm, LANES), lambda i, j: (rope_blk(i), 0)),
                  pl.BlockSpec((tm, LANES), lambda i, j: (rope_blk(i), 0))],
        out_specs=[pl.BlockSpec((tm, half), lambda i, j: (i, 0)),
                   pl.BlockSpec((tm, half), lambda i, j: (i, 0)),
                   pl.BlockSpec((tm, half), lambda i, j: (i, 0)),
                   pl.BlockSpec((tm, half), lambda i, j: (i, 0)),
                   pl.BlockSpec((tm, tn), lambda i, j: (i, jnp.maximum(j - 2, 0)))],
        out_shape=[jax.ShapeDtypeStruct((m, half), BF16),
                   jax.ShapeDtypeStruct((m, half), BF16),
                   jax.ShapeDtypeStruct((m, half), BF16),
                   jax.ShapeDtypeStruct((m, half), F32),
                   jax.ShapeDtypeStruct((m, P_TOTAL - 2 * tn), F32)],
        scratch_shapes=[pltpu.VMEM((tm, D_MODEL), BF16)],
        compiler_params=_params(2),
        name="input_projection",
    )(h, mod3, mod3, w_in_bf, b_in.reshape(1, P_TOTAL), cos_t, sin_t)


def _make_rope(seq, tm):
    t = np.arange(seq)
    row_pos, col_pos = t // GRID_W, t % GRID_W
    d = np.arange(LANES) % NA_HEAD_DIM
    m = NA_HEAD_DIM // 4
    inv_freq = (ROPE_BASE ** (-jnp.arange(m, dtype=F32) / m))[d % m]
    pos = np.where((d < 2 * m)[None, :], row_pos[:, None], col_pos[:, None])
    ang = jnp.asarray(pos).astype(F32) * inv_freq[None, :]
    cos = jnp.cos(ang)
    sin = jnp.sin(ang)
    sin_signed = jnp.where(jnp.asarray((d % (2 * m)) < m)[None, :], -sin, sin)
    cos = jnp.concatenate([cos, jnp.ones((tm, LANES), F32)], axis=0)
    sin_signed = jnp.concatenate([sin_signed, jnp.zeros((tm, LANES), F32)], axis=0)
    return cos, sin_signed


def _scan_kernel(fblk, bblk, first, last, seqb,
                 sb_ref, scg_ref, sx_ref, lxf_ref,
                 scgp_ref, sxp_ref, lxfp_ref, scgn_ref, sxn_ref, lxfn_ref,
                 lxb_ref, lxbp_ref, lxbn_ref,
                 scw_ref, cw_ref, cb_ref, sp_ref, wr_ref, wi_ref, br_ref, bi_ref,
                 zb_ref, hf_ref, hb_ref,
                 a_s, b_s, hc_s, st_s, *, ch, n_ctx_items):
    it = pl.program_id(0)
    is_first = first[it] == 1
    is_last = last[it] == 1
    is_ctx = it < n_ctx_items
    b = seqb[it]
    width = BRANCH_WIDTH
    row = lax.broadcasted_iota(jnp.int32, (ch, width), 0)
    not_first = jnp.where(is_first, 0.0, 1.0).astype(F32)
    not_last = jnp.where(is_last, 0.0, 1.0).astype(F32)

    def back1(u, prev_row):
        return jnp.where(row == 0, prev_row, pltpu.roll(u, 1, 0))

    def back2(u, prev2, prev1):
        return jnp.where(row == 0, prev2, jnp.where(row == 1, prev1, pltpu.roll(u, 2, 0)))

    def fwd1(u, next_row):
        return jnp.where(row == ch - 1, next_row, pltpu.roll(u, ch - 1, 0))

    def lru_input(lx_ref, lxp_ref, lxn_ref, prev_ok, next_ok):
        x = lx_ref[...]
        p = lxp_ref[...] * prev_ok
        n = lxn_ref[...] * next_ok
        return (cw_ref[0:1] * back2(x, p[6:7], p[7:8]) + cw_ref[1:2] * back1(x, p[7:8])
                + cw_ref[2:3] * x + cw_ref[3:4] * fwd1(x, n[0:1]) + cb_ref[...])

    def coeffs(d, xm):
        xb = xm.astype(BF16)
        r = jax.nn.sigmoid(jnp.dot(xb, wr_ref[d], preferred_element_type=F32) + br_ref[d:d + 1])
        g = jax.nn.sigmoid(jnp.dot(xb, wi_ref[d], preferred_element_type=F32) + bi_ref[d:d + 1])
        log_a = (-LRU_C * sp_ref[d:d + 1]) * r
        a = jnp.exp(log_a)
        a_s[d] = a
        b_s[d] = jnp.sqrt(-jnp.tanh(log_a) * (a * a + 1.0)) * (g * xm)

    u = scg_ref[...] * sx_ref[...]
    u_prev = scgp_ref[7:8] * sxp_ref[7:8] * not_first
    u_next = scgn_ref[0:1] * sxn_ref[0:1] * not_last
    conv = scw_ref[0:1] * back1(u, u_prev) + scw_ref[1:2] * u + scw_ref[2:3] * fwd1(u, u_next)
    zb_ref[...] = (sb_ref[...] * conv).astype(BF16)

    coeffs(0, lru_input(lxf_ref, lxfp_ref, lxfn_ref, not_first, not_last))
    coeffs(1, lru_input(lxb_ref, lxbp_ref, lxbn_ref, not_last, not_first))

    @pl.when(jnp.logical_and(is_first, is_ctx))
    def _():
        hc_s[...] = jnp.zeros_like(hc_s)

    @pl.when(jnp.logical_and(is_first, jnp.logical_not(is_ctx)))
    def _():
        hc_s[0:1] = st_s[pl.ds(2 * b, 1), :]
        hc_s[1:2] = st_s[pl.ds(2 * b + 1, 1), :]

    def body(i, carry):
        hf, hb = carry
        base = i * SUBLANES
        for r in range(SUBLANES):
            t = base + r
            hf = a_s[0, pl.ds(t, 1), :] * hf + b_s[0, pl.ds(t, 1), :]
            hf_ref[pl.ds(t, 1), :] = hf
            tb = ch - 1 - t
            hb = a_s[1, pl.ds(tb, 1), :] * hb + b_s[1, pl.ds(tb, 1), :]
            hb_ref[pl.ds(tb, 1), :] = hb
        return hf, hb

    hf, hb = lax.fori_loop(0, ch // SUBLANES, body, (hc_s[0:1], hc_s[1:2]))
    hc_s[0:1] = hf
    hc_s[1:2] = hb

    @pl.when(is_ctx)
    def _():
        st_s[pl.ds(2 * b, 1), :] = hf
        st_s[pl.ds(2 * b + 1, 1), :] = hb


def _scan_tables(n_batch, seq, n_ctx, ch):
    assert n_ctx == ch
    nc = seq // ch
    ctx0 = n_batch * seq // ch
    fblk, bblk, first, last, seqb = [], [], [], [], []
    for b in range(n_batch):
        fblk.append(ctx0 + b); bblk.append(ctx0 + b); first.append(1); last.append(1); seqb.append(b)
    for b in range(n_batch):
        for c in range(nc):
            fblk.append(b * nc + c); bblk.append(b * nc + nc - 1 - c)
            first.append(int(c == 0)); last.append(int(c == nc - 1)); seqb.append(b)
    return [np.asarray(a, np.int32) for a in (fblk, bblk, first, last, seqb)]


def _conv_scan(sb, rest, sc_w, lru_cw, lru_cb, sp, wr_bd, wi_bd, b_r, b_i, n_batch, seq, n_ctx):
    m = sb.shape[0]
    ch = SCAN_CHUNK
    width = BRANCH_WIDTH
    tables = _scan_tables(n_batch, seq, n_ctx, ch)
    n_items = len(tables[0])
    halo_per_chunk = ch // SUBLANES
    last_halo = m // SUBLANES - 1

    def cur(col, which):
        return pl.BlockSpec((ch, width), lambda i, f, bk, *_: ((f, bk)[which][i], col))

    def prev(col, which):
        return pl.BlockSpec((SUBLANES, width),
                            lambda i, f, bk, *_: (jnp.maximum((f, bk)[which][i] * halo_per_chunk - 1, 0), col))

    def nxt(col, which):
        return pl.BlockSpec((SUBLANES, width),
                            lambda i, f, bk, *_: (jnp.minimum(((f, bk)[which][i] + 1) * halo_per_chunk, last_halo), col))

    def full(shape):
        return pl.BlockSpec(shape, lambda i, *_: (0,) * len(shape))

    in_specs = [cur(0, 0), cur(0, 0), cur(1, 0), cur(2, 0),
                prev(0, 0), prev(1, 0), prev(2, 0), nxt(0, 0), nxt(1, 0), nxt(2, 0),
                cur(2, 1), prev(2, 1), nxt(2, 1),
                full(sc_w.shape), full(lru_cw.shape), full((1, width)), full(sp.shape),
                full(wr_bd.shape), full(wi_bd.shape), full(b_r.shape), full(b_i.shape)]
    out_specs = [cur(0, 0), cur(0, 0), cur(0, 1)]
    grid_spec = pltpu.PrefetchScalarGridSpec(
        num_scalar_prefetch=5, grid=(n_items,), in_specs=in_specs, out_specs=out_specs,
        scratch_shapes=[pltpu.VMEM((2, ch, width), F32), pltpu.VMEM((2, ch, width), F32),
                        pltpu.VMEM((SUBLANES, width), F32), pltpu.VMEM((2 * n_batch, width), F32)])
    return pl.pallas_call(
        functools.partial(_scan_kernel, ch=ch, n_ctx_items=n_batch),
        grid_spec=grid_spec,
        out_shape=[jax.ShapeDtypeStruct((m, width), BF16),
                   jax.ShapeDtypeStruct((m, width), F32),
                   jax.ShapeDtypeStruct((m, width), F32)],
        compiler_params=_params(1),
        name="conv_scan",
    )(*[jnp.asarray(t) for t in tables],
      sb, rest, rest, rest, rest, rest, rest, rest, rest, rest, rest, rest, rest,
      sc_w, lru_cw, lru_cb.reshape(1, width), sp, wr_bd, wi_bd, b_r, b_i)


def _attention_kernel(qblk, bat, r0t, cls,
                      q_ref, k_ref, v_ref, kc_ref, vc_ref, bias_ref, o_ref, *, band):
    it = pl.program_id(0)
    start = pl.multiple_of(r0t[it] * GRID_W, GRID_W)
    nq = q_ref.shape[0]
    lane = lax.broadcasted_iota(jnp.int32, (nq, LANES), 1)
    low = lane < NA_HEAD_DIM
    scale = NA_HEAD_DIM ** -0.5
    nt = (((1,), (1,)), ((), ()))
    for hp in range(NA_HEADS // 2):
        cols = slice(hp * LANES, (hp + 1) * LANES)
        qp = q_ref[:, cols].astype(F32)
        qs = jnp.concatenate([jnp.where(low, qp, 0.0), jnp.where(low, 0.0, qp)], axis=0).astype(BF16)
        kb = k_ref[pl.ds(start, band), cols]
        vb = v_ref[pl.ds(start, band), cols]
        s_loc = lax.dot_general(qs, kb, nt, preferred_element_type=F32)
        s_ctx = lax.dot_general(qs, kc_ref[:, cols], nt, preferred_element_type=F32)
        bias = jnp.concatenate([bias_ref[0, 2 * hp], bias_ref[0, 2 * hp + 1]], axis=0)
        s_loc = s_loc * scale + bias
        s_ctx = s_ctx * scale
        mx = jnp.maximum(jnp.max(s_loc, axis=-1, keepdims=True), jnp.max(s_ctx, axis=-1, keepdims=True))
        e_loc = jnp.exp(s_loc - mx)
        e_ctx = jnp.exp(s_ctx - mx)
        den = jnp.sum(e_loc, axis=-1, keepdims=True) + jnp.sum(e_ctx, axis=-1, keepdims=True)
        o = (jnp.dot(e_loc.astype(BF16), vb, preferred_element_type=F32)
             + jnp.dot(e_ctx.astype(BF16), vc_ref[:, cols], preferred_element_type=F32)) / den
        o_ref[:, cols] = jnp.where(low, o[:nq], o[nq:]).astype(BF16)


def _attention_tables(n_batch, seq, n_ctx, with_ctx_queries):
    rows = seq // GRID_W
    kr = min(NA_WIN_ROWS, rows)
    qblk, bat, r0t, cls = [], [], [], []
    ctx_q0 = n_batch * seq // ATT_QROWS
    for b in range(n_batch):
        for r in range(rows):
            r0 = min(max(r - kr // 2, 0), rows - kr)
            qblk.append(b * rows + r); bat.append(b); r0t.append(r0); cls.append(r - r0)
        if with_ctx_queries:
            for c in range(n_ctx // ATT_QROWS):
                qblk.append(ctx_q0 + b * (n_ctx // ATT_QROWS) + c); bat.append(b); r0t.append(0); cls.append(kr)
    return [np.asarray(a, np.int32) for a in (qblk, bat, r0t, cls)]


def _attention_bias(rpb, seq):
    rows = seq // GRID_W
    kr = min(NA_WIN_ROWS, rows)
    kc = NA_WIN_COLS
    cq = np.arange(GRID_W)
    c0 = np.clip(cq - kc // 2, 0, GRID_W - kc)
    ck = np.arange(GRID_W)
    inside = (ck[None, :] >= c0[:, None]) & (ck[None, :] < c0[:, None] + kc)
    dc = np.clip(ck[None, :] - cq[:, None] + (NA_WIN_COLS - 1), 0, 2 * NA_WIN_COLS - 2)
    n_dr = 2 * NA_WIN_ROWS - 1
    table = jnp.where(jnp.asarray(inside)[None, None], rpb[:, :, dc], NEG_BIG)
    table = table.transpose(0, 2, 1, 3).reshape(NA_HEADS, GRID_W, n_dr * GRID_W)
    classes = []
    for cl in range(kr):
        lo = (NA_WIN_ROWS - 1 - cl) * GRID_W
        classes.append(table[:, :, lo:lo + kr * GRID_W])
    classes.append(jnp.full((NA_HEADS, GRID_W, kr * GRID_W), NEG_BIG, F32))
    return jnp.stack(classes, axis=0)


def _attention(q, k, v, bias, n_batch, seq, n_ctx, with_ctx_queries):
    m = q.shape[0] if with_ctx_queries else n_batch * seq
    rows = seq // GRID_W
    kr = min(NA_WIN_ROWS, rows)
    band = kr * GRID_W
    tables = _attention_tables(n_batch, seq, n_ctx, with_ctx_queries)
    n_items = len(tables[0])
    ctx_blk0 = n_batch * seq // n_ctx
    width = NA_WIDTH
    in_specs = [pl.BlockSpec((ATT_QROWS, width), lambda i, qb, bt, r0, cl: (qb[i], 0)),
                pl.BlockSpec((seq, width), lambda i, qb, bt, r0, cl: (bt[i], 0)),
                pl.BlockSpec((seq, width), lambda i, qb, bt, r0, cl: (bt[i], 0)),
                pl.BlockSpec((n_ctx, width), lambda i, qb, bt, r0, cl: (ctx_blk0 + bt[i], 0)),
                pl.BlockSpec((n_ctx, width), lambda i, qb, bt, r0, cl: (ctx_blk0 + bt[i], 0)),
                pl.BlockSpec((1, NA_HEADS, GRID_W, band), lambda i, qb, bt, r0, cl: (cl[i], 0, 0, 0))]
    out_specs = pl.BlockSpec((ATT_QROWS, width), lambda i, qb, bt, r0, cl: (qb[i], 0))
    grid_spec = pltpu.PrefetchScalarGridSpec(num_scalar_prefetch=4, grid=(n_items,),
                                             in_specs=in_specs, out_specs=out_specs)
    return pl.pallas_call(
        functools.partial(_attention_kernel, band=band),
        grid_spec=grid_spec,
        out_shape=jax.ShapeDtypeStruct((m, width), BF16),
        compiler_params=_params(1),
        name="attention",
    )(*[jnp.asarray(t) for t in tables], q, k, v, k, v, bias)


def _post_kernel(h_ref, att_ref, zb_ref, hf_ref, hb_ref, lg_ref, gl0_ref, gl1_ref, gl2_ref,
                 g1_ref, sh2_ref, sc2_ref,
                 wpa_ref, wpc_ref, wpl_ref, wo_ref, bo_ref, l1g_ref, l1b_ref, rw_ref, rb_ref, tri_ref, upper_ref,
                 h1_ref, xs_ref, pos_ref, cnt_ref):
    y_a = jnp.dot(att_ref[...], wpa_ref[...], preferred_element_type=F32)
    y_b = jnp.dot(zb_ref[...], wpc_ref[...], preferred_element_type=F32)
    zc = jax.nn.gelu(lg_ref[...]) * (hf_ref[...] + hb_ref[...])
    y_c = jnp.dot(zc.astype(BF16), wpl_ref[...], preferred_element_type=F32)
    merged = (jax.nn.sigmoid(gl0_ref[...]) * y_a + jax.nn.sigmoid(gl1_ref[...]) * y_b
              + jax.nn.sigmoid(gl2_ref[...]) * y_c)
    y = jnp.dot(merged.astype(BF16), wo_ref[...], preferred_element_type=F32) + bo_ref[...]
    h1 = _layer_norm(DEEPNORM_ALPHA * h_ref[...] + g1_ref[0] * y) * l1g_ref[...] + l1b_ref[...]
    h1_ref[...] = h1
    u2 = _layer_norm(h1) * (1.0 + sc2_ref[0]) + sh2_ref[0]

    logits =jnp.dot(u2, rw_ref[...], precision=lax.Precision.HIGHEST, preferred_element_type=F32) + rb_ref[...]
    tm = logits.shape[0]
    lane = lax.broadcasted_iota(jnp.int32, (tm, LANES), 1)
    lane_f = lane.astype(F32)
    work = logits
    tops, idxs, hots = [], [], []
    for _ in range(TOP_K):
        mx = jnp.max(work, axis=-1, keepdims=True)
        idx = jnp.min(jnp.where(work == mx, lane_f, float(LANES)), axis=-1, keepdims=True)
        hot = lane_f == idx
        work = jnp.where(hot, -3e38, work)
        tops.append(mx); idxs.append(idx); hots.append(hot)
    exps = [jnp.exp(t - tops[0]) for t in tops]
    den = exps[0] + exps[1] + exps[2] + exps[3]
    hot_all = jnp.zeros((tm, LANES), F32)
    for hot in hots:
        hot_all = hot_all + hot.astype(F32)
    cnt = jnp.sum(hot_all, axis=0, keepdims=True)
    cnt_pad = jnp.floor((cnt + (SUBLANES - 1.0)) * (1.0 / SUBLANES)) * SUBLANES
    off = jnp.dot(jnp.broadcast_to(cnt_pad, (SUBLANES, LANES)), upper_ref[...],
                  precision=lax.Precision.HIGHEST, preferred_element_type=F32)[0:1]
    slot = off + jnp.dot(tri_ref[...], hot_all.astype(BF16), preferred_element_type=F32)
    pos4 = jnp.zeros((tm, LANES), F32)
    weights = []
    for kk in range(TOP_K):
        pos_k = jnp.sum(jnp.where(hots[kk], slot, 0.0), axis=-1, keepdims=True)
        pos4 = jnp.where(lane == kk, pos_k, pos4)
        p = exps[kk] / den
        p_hi = p.astype(BF16).astype(F32)
        p_mid = (p - p_hi).astype(BF16).astype(F32)
        p_lo = p - p_hi - p_mid
        weights.append(jnp.where(lane == 0, p_hi, jnp.where(lane == 1, p_mid, jnp.where(lane == 2, p_lo, 0.0))))
    pos_ref[...] = pos4
    sub = lax.broadcasted_iota(jnp.int32, (SUBLANES, LANES), 0)
    cnt_ref[...] = jnp.where(sub == 0, cnt, jnp.where(sub == 1, off, 0.0))

    n_sorted = xs_ref.shape[0]
    pos_t = pos4.T
    r_iota = lax.broadcasted_iota(jnp.int32, (n_sorted, tm), 0).astype(F32)
    perm = jnp.zeros((n_sorted, tm), F32)
    w_sorted = jnp.zeros((n_sorted, LANES), F32)
    for kk in range(TOP_K):
        sel_k = (r_iota == pos_t[kk:kk + 1, :]).astype(F32)
        perm = perm + sel_k
        w_sorted = w_sorted + jnp.dot(sel_k.astype(BF16), weights[kk].astype(BF16), preferred_element_type=F32)
    xs_ref[:, :D_MODEL] = jnp.dot(perm.astype(BF16), u2.astype(BF16), preferred_element_type=F32)
    xs_ref[:, D_MODEL:] = w_sorted


def _post_mixer(h, att, zb, hf, hb, rest, mod3, mod_base, wpa, wpc, wpl, wo, b_o, l1g, l1b, rw_pad, rb_pad,
                n_rows, n_lat, seq):
    tm = TM_POST
    width = BRANCH_WIDTH
    tiles_per_seq = seq // tm
    n_lat_tiles = n_lat // tm
    n_groups_lat = n_lat // seq
    tri = jnp.asarray(np.tril(np.ones((tm, tm), np.float32), -1), BF16)
    upper = jnp.asarray(np.triu(np.ones((LANES, LANES), np.float32), 1))
    n_tiles = n_rows // tm

    def group(i):
        return jnp.where(i < n_lat_tiles, i // tiles_per_seq, n_groups_lat)

    def rows(wd, col=0):
        return pl.BlockSpec((tm, wd), lambda i: (i, col))

    def full(shape):
        return pl.BlockSpec(shape, lambda i: (0,) * len(shape))

    def mod(which):
        return pl.BlockSpec((1, 1, D_MODEL), lambda i: (mod_base + group(i) * 6 + which, 0, 0))

    in_specs = [rows(D_MODEL), rows(width), rows(width), rows(width), rows(width), rows(width, 3),
                rows(D_MODEL, 2), rows(D_MODEL, 3), rows(D_MODEL, 4),
                mod(2), mod(3), mod(4),
                full(wpa.shape), full(wpc.shape), full(wpl.shape), full(wo.shape), full((1, D_MODEL)),
                full((1, D_MODEL)), full((1, D_MODEL)), full(rw_pad.shape), full(rb_pad.shape), full(tri.shape),
                full(upper.shape)]
    out_specs = [rows(D_MODEL), pl.BlockSpec((TOK_BLOCK, XS_WIDTH), lambda i: (i, 0)), rows(LANES),
                 pl.BlockSpec((SUBLANES, LANES), lambda i: (i, 0))]
    return pl.pallas_call(
        _post_kernel,
        grid=(n_tiles,),
        in_specs=in_specs, out_specs=out_specs,
        out_shape=[jax.ShapeDtypeStruct((n_rows, D_MODEL), F32),
                   jax.ShapeDtypeStruct((n_tiles * TOK_BLOCK, XS_WIDTH), F32),
                   jax.ShapeDtypeStruct((n_rows, LANES), F32),
                   jax.ShapeDtypeStruct((n_tiles * SUBLANES, LANES), F32)],
        compiler_params=_params(1),
        name="post_mixer",
    )(h, att, zb, hf, hb, rest, rest, rest, rest, mod3, mod3, mod3,
      wpa, wpc, wpl, wo, b_o.reshape(1, D_MODEL), l1g.reshape(1, D_MODEL), l1b.reshape(1, D_MODEL),
      rw_pad, rb_pad, tri, upper)


def _expert_plan(cnt_out, n_tok_tiles, n_rows):
    tm = TM_EXPERT
    co = cnt_out.reshape(n_tok_tiles, SUBLANES, LANES)
    cnt = (co[:, 0, :N_EXPERTS].astype(jnp.int32) + SUBLANES - 1) // SUBLANES * SUBLANES
    off = co[:, 1, :N_EXPERTS].astype(jnp.int32)
    cum_end = jnp.cumsum(cnt, axis=0)
    cum = cum_end - cnt
    total = cum_end[-1]
    n_et = (total + tm - 1) // tm
    et_end = jnp.cumsum(n_et)
    n_act = et_end[-1:].astype(jnp.int32)
    n_tiles = -(-n_tok_tiles * TOK_BLOCK // tm) + N_EXPERTS
    j = jnp.arange(n_tiles, dtype=jnp.int32)
    tile_e = jnp.minimum(jnp.sum((et_end[None, :] <= j[:, None]).astype(jnp.int32), axis=1), N_EXPERTS - 1)
    row0 = (j - (et_end - n_et)[tile_e]) * tm
    cum_e = cum.T[tile_e]
    cum_end_e = cum_end.T[tile_e]
    i_lo = jnp.sum((cum_end_e <= row0[:, None]).astype(jnp.int32), axis=1)
    i_hi = jnp.sum((cum_e < (row0 + tm)[:, None]).astype(jnp.int32), axis=1)
    return (tile_e.astype(jnp.int32), row0.astype(jnp.int32), i_lo, i_hi, n_act,
            cum.reshape(-1), cnt.reshape(-1), off.reshape(-1), n_tiles)


def _expert_kernel(tile_e, row0_t, ilo_t, ihi_t, n_act, cum_t, cnt_t, off_t,
                   xs_hbm, wgu_ref, bgu_ref, wdn_ref, bdn_ref, ys_hbm,
                   xin, yout, wgu_s, wdn_s, zeros, sem_in, sem_out, sem_zero, *, tm, tok_block, n_tok_tiles):
    j = pl.program_id(0)
    na = n_act[0]

    def runs(jj, slot, copy_fn):
        e = tile_e[jj]
        r0 = row0_t[jj]

        def per_tok_tile(i, c):
            idx = i * N_EXPERTS + e
            c0 = cum_t[idx]
            lo = jnp.maximum(c0, r0)
            hi = jnp.minimum(c0 + cnt_t[idx], r0 + tm)
            length = jnp.maximum(hi - lo, 0)
            src = i * tok_block + off_t[idx] + (lo - c0)
            dst = lo - r0
            src = pl.multiple_of(src, SUBLANES)
            dst = pl.multiple_of(dst, SUBLANES)
            n_big = lax.shift_right_logical(length, 5)

            def big(b, cc):
                copy_fn(pl.multiple_of(src + b * 32, SUBLANES), pl.multiple_of(dst + b * 32, SUBLANES), 32, slot)
                return cc

            lax.fori_loop(0, n_big, big, 0)
            done = n_big * 32
            rem = length - done
            for size, mask in ((16, 0), (8, 16)):
                @pl.when((rem & size) != 0)
                def _():
                    at = done + (rem & mask)
                    copy_fn(pl.multiple_of(src + at, SUBLANES), pl.multiple_of(dst + at, SUBLANES), size, slot)
            return c

        lax.fori_loop(ilo_t[jj], ihi_t[jj], per_tok_tile, 0)

    def gather(src, dst, size, slot):
        return pltpu.make_async_copy(xs_hbm.at[pl.ds(src, size)], xin.at[slot, pl.ds(dst, size)], sem_in.at[slot])

    def scatter(src, dst, size, slot):
        return pltpu.make_async_copy(yout.at[slot, pl.ds(dst, size)], ys_hbm.at[pl.ds(src, size)], sem_out.at[slot])

    def start_gather(*a):
        gather(*a).start()

    def wait_gather(*a):
        gather(*a).wait()

    def start_scatter(*a):
        scatter(*a).start()

    def wait_scatter(*a):
        scatter(*a).wait()

    slot = j % 2

    @pl.when(j == 0)
    def _():
        xin[...] = jnp.zeros_like(xin)
        runs(0, 0, start_gather)
        zeros[...] = jnp.zeros_like(zeros)
        last = N_EXPERTS - 1

        def clear_tail(i, copy_op):
            idx = i * N_EXPERTS + last
            used = off_t[idx] + cnt_t[idx]
            tail = tok_block - used
            for size in (256, 128, 64, 32, 16, 8):
                @pl.when((tail & size) != 0)
                def _():
                    at = pl.multiple_of(i * tok_block + used + (tail & ~(2 * size - 1)), SUBLANES)
                    copy_op(pltpu.make_async_copy(zeros.at[pl.ds(0, size)], ys_hbm.at[pl.ds(at, size)], sem_zero))

        def start_clear(i, c):
            clear_tail(i, lambda cp: cp.start())
            return c

        def wait_clear(i, c):
            clear_tail(i, lambda cp: cp.wait())
            return c

        lax.fori_loop(0, n_tok_tiles, start_clear, 0)
        lax.fori_loop(0, n_tok_tiles, wait_clear, 0)

    @pl.when(j + 1 < na)
    def _():
        runs(j + 1, 1 - slot, start_gather)

    @pl.when(j < na)
    def _():
        e = tile_e[j]
        e_prev = tile_e[jnp.maximum(j - 1, 0)]

        @pl.when(jnp.logical_or(j == 0, e != e_prev))
        def _():
            wgu_s[...] = wgu_ref[0, 0].astype(BF16)
            wdn_s[...] = wdn_ref[0, 0].astype(BF16)

        runs(j, slot, wait_gather)
        x = xin[slot]
        p = x[:, D_MODEL:D_MODEL + 1] + x[:, D_MODEL + 1:D_MODEL + 2] + x[:, D_MODEL + 2:D_MODEL + 3]
        gu = jnp.dot(x[:, :D_MODEL].astype(BF16), wgu_s[...], preferred_element_type=F32) + bgu_ref[0, 0]
        f = gu.shape[1] // 2
        gate = jnp.minimum(gu[:, :f], SWIGLU_LIMIT)
        up = jnp.clip(gu[:, f:], -SWIGLU_LIMIT, SWIGLU_LIMIT)
        hid = (up + 1.0) * gate * jax.nn.sigmoid(SWIGLU_ALPHA * gate)
        y = jnp.dot(hid.astype(BF16), wdn_s[...], preferred_element_type=F32) + bdn_ref[0, 0]
        yout[slot] = y * p
        runs(j, slot, start_scatter)

        @pl.when(j >= 1)
        def _():
            runs(j - 1, 1 - slot, wait_scatter)

        @pl.when(j == na - 1)
        def _():
            runs(j, slot, wait_scatter)


def _experts(xs, plan, layer, w_gu, b_gu, w_dn, b_dn):
    tile_e, row0, i_lo, i_hi, n_act, cum, cnt, off, n_tiles = plan
    tm = TM_EXPERT
    f2 = w_gu.shape[-1]

    def expert(j, te, r0, lo, hi, na, *_):
        return (layer, te[jnp.minimum(j, na[0] - 1)], 0, 0)

    in_specs = [pl.BlockSpec(memory_space=pl.ANY),
                pl.BlockSpec((1, 1, D_MODEL, f2), expert),
                pl.BlockSpec((1, 1, 1, f2), expert),
                pl.BlockSpec((1, 1, f2 // 2, D_MODEL), expert),
                pl.BlockSpec((1, 1, 1, D_MODEL), expert)]
    grid_spec = pltpu.PrefetchScalarGridSpec(
        num_scalar_prefetch=8, grid=(n_tiles,), in_specs=in_specs,
        out_specs=pl.BlockSpec(memory_space=pl.ANY),
        scratch_shapes=[pltpu.VMEM((2, tm, XS_WIDTH), F32), pltpu.VMEM((2, tm, D_MODEL), F32),
                        pltpu.VMEM((D_MODEL, f2), BF16), pltpu.VMEM((f2 // 2, D_MODEL), BF16),
                        pltpu.VMEM((ZERO_ROWS, D_MODEL), F32),
                        pltpu.SemaphoreType.DMA((2,)), pltpu.SemaphoreType.DMA((2,)), pltpu.SemaphoreType.DMA(())])
    return pl.pallas_call(
        functools.partial(_expert_kernel, tm=tm, tok_block=TOK_BLOCK, n_tok_tiles=xs.shape[0] // TOK_BLOCK),
        grid_spec=grid_spec,
        out_shape=jax.ShapeDtypeStruct((xs.shape[0], D_MODEL), F32),
        compiler_params=_params(1),
        name="moe_experts",
    )(tile_e, row0, i_lo, i_hi, n_act, cum, cnt, off, xs, w_gu, b_gu.reshape(DEPTH, N_EXPERTS, 1, f2), w_dn,
      b_dn.reshape(DEPTH, N_EXPERTS, 1, D_MODEL))


def _combine_kernel(ys_ref, pos_ref, h1_ref, g2_ref, l2g_ref, l2b_ref, o_ref):
    tm = pos_ref.shape[0]
    n_sorted = ys_ref.shape[0]
    pos = pos_ref[...]
    col = lax.broadcasted_iota(jnp.int32, (tm, n_sorted), 1).astype(F32)
    sel = jnp.zeros((tm, n_sorted), F32)
    for kk in range(TOP_K):
        sel = sel + (col == pos[:, kk:kk + 1]).astype(F32)
    sel = sel.astype(BF16)
    ys = ys_ref[...]
    hi = ys.astype(BF16)
    rest = ys - hi.astype(F32)
    mid = rest.astype(BF16)
    lo = (rest - mid.astype(F32)).astype(BF16)
    y2 = (jnp.dot(sel, hi, preferred_element_type=F32) + jnp.dot(sel, mid, preferred_element_type=F32)
          + jnp.dot(sel, lo, preferred_element_type=F32))
    o_ref[...] = _layer_norm(DEEPNORM_ALPHA * h1_ref[...] + g2_ref[0] * y2) * l2g_ref[...] + l2b_ref[...]


def _combine(ys, pos4, h1, mod3, mod_base, l2g, l2b, n_lat, seq):
    n_rows = h1.shape[0]
    tm = TM_POST
    tiles_per_seq = seq // tm
    n_lat_tiles = n_lat // tm
    n_groups_lat = n_lat // seq

    def group(i):
        return jnp.where(i < n_lat_tiles, i // tiles_per_seq, n_groups_lat)

    in_specs = [pl.BlockSpec((TOK_BLOCK, D_MODEL), lambda i: (i, 0)),
                pl.BlockSpec((tm, LANES), lambda i: (i, 0)),
                pl.BlockSpec((tm, D_MODEL), lambda i: (i, 0)),
                pl.BlockSpec((1, 1, D_MODEL), lambda i: (mod_base + group(i) * 6 + 5, 0, 0)),
                pl.BlockSpec((1, D_MODEL), lambda i: (0, 0)),
                pl.BlockSpec((1, D_MODEL), lambda i: (0, 0))]
    return pl.pallas_call(
        _combine_kernel,
        grid=(n_rows // tm,),
        in_specs=in_specs,
        out_specs=pl.BlockSpec((tm, D_MODEL), lambda i: (i, 0)),
        out_shape=jax.ShapeDtypeStruct((n_rows, D_MODEL), F32),
        compiler_params=_params(1),
        name="moe_combine",
    )(ys, pos4, h1, mod3, l2g.reshape(1, D_MODEL), l2b.reshape(1, D_MODEL))


def _block_diag(w):
    two, n, d, e = w.shape
    eye = jnp.eye(n, dtype=w.dtype)
    return (w[:, :, :, None, :] * eye[None, :, None, :, None]).reshape(two, n * d, n * e)


def kernel(x, c, ctx, c_ctx, w_mod, b_mod, w_in, b_in, na_rpb, w_proj_attn, w_proj_conv, w_proj_lru, sc_conv_w, lru_conv_w, lru_conv_b, lru_lambda, lru_w_r, lru_b_r, lru_w_i, lru_b_i, w_o, b_o, ln1_g, ln1_b, router_w, router_b, exp_w_gu, exp_b_gu, exp_w_dn, exp_b_dn, ln2_g, ln2_b):
    n_batch, seq, d = x.shape
    n_ctx = ctx.shape[1]
    n_lat = n_batch * seq
    n_all = n_lat + n_batch * n_ctx
    assert d == D_MODEL and n_batch + 1 <= SUBLANES

    cc = jnp.concatenate([c, c_ctx[None], jnp.zeros((SUBLANES - n_batch - 1, d), F32)], axis=0)
    mod = _modulation(cc, w_mod, b_mod)
    groups = n_batch + 1
    mod3 = mod.reshape(DEPTH, SUBLANES, 6, d)[:, :groups].reshape(DEPTH * groups * 6, 1, d)

    cos_t, sin_t = _make_rope(seq, TM_INPROJ)
    h = jnp.concatenate([x.reshape(n_lat, d), ctx.reshape(n_batch * n_ctx, d)], axis=0)

    for layer in range(DEPTH):
        last = layer == DEPTH - 1
        mod_base = layer * groups * 6
        q, k, v, sb, rest = _input_projection(h, mod3, mod_base, w_in[layer].astype(BF16), b_in[layer],
                                              cos_t, sin_t, n_lat, seq)
        sp = jax.nn.softplus(-lru_lambda[layer])
        zb, hf, hb = _conv_scan(sb, rest, sc_conv_w[layer], lru_conv_w[layer], lru_conv_b[layer], sp,
                                _block_diag(lru_w_r[layer]).astype(BF16), _block_diag(lru_w_i[layer]).astype(BF16),
                                lru_b_r[layer], lru_b_i[layer], n_batch, seq, n_ctx)
        att = _attention(q, k, v, _attention_bias(na_rpb[layer], seq), n_batch, seq, n_ctx, not last)
        n_rows = n_lat if last else n_all
        rw_pad = jnp.pad(router_w[layer], ((0, 0), (0, LANES - N_EXPERTS)))
        rb_pad = jnp.concatenate([router_b[layer], jnp.full((LANES - N_EXPERTS,), NEG_BIG, F32)]).reshape(1, LANES)
        h1, xs, pos4, cnt_out = _post_mixer(
            h, att, zb, hf, hb, rest, mod3, mod_base,
            w_proj_attn[layer].astype(BF16), w_proj_conv[layer].astype(BF16), w_proj_lru[layer].astype(BF16),
            w_o[layer].astype(BF16), b_o[layer], ln1_g[layer], ln1_b[layer], rw_pad, rb_pad, n_rows, n_lat, seq)
        plan = _expert_plan(cnt_out, n_rows // TM_POST, n_rows)
        ys = _experts(xs, plan, layer, exp_w_gu, exp_b_gu, exp_w_dn, exp_b_dn)
        h = _combine(ys, pos4, h1, mod3, mod_base, ln2_g[layer], ln2_b[layer], n_lat, seq)
    return h.reshape(n_batch, seq, d)
```

```python
import functools

import numpy as np
import jax
import jax.numpy as jnp
from jax import lax
from jax.experimental import pallas as pl
from jax.experimental.pallas import tpu as pltpu

D_MODEL = 1024
DEPTH = 2
GRID_W = 64
NA_HEADS = 8
NA_HEAD_DIM = 64
NA_WIDTH = NA_HEADS * NA_HEAD_DIM
NA_WIN_ROWS = 8
NA_WIN_COLS = 16
ROPE_BASE = 10000.0
BRANCH_WIDTH = 512
LRU_BLOCKS = 8
LRU_C = 8.0
P_TOTAL = 7168
N_EXPERTS = 32
TOP_K = 4
SWIGLU_LIMIT = 7.0
SWIGLU_ALPHA = 1.702
LN_EPS = 1e-5
DEEPNORM_ALPHA = (2 * DEPTH) ** 0.25
NEG_BIG = -1e30

LANES = 128
SUBLANES = 8
VMEM_LIMIT_BYTES = 56 * 1024 * 1024

TM_INPROJ = 1024
TN_INPROJ = 1024
SCAN_CHUNK = 256
ATT_QROWS = GRID_W
TM_POST = 256
TM_EXPERT = 512
XS_WIDTH = D_MODEL + LANES
TOK_BLOCK = TM_POST * TOP_K + N_EXPERTS * SUBLANES
ZERO_ROWS = 256

F32 = jnp.float32
BF16 = jnp.bfloat16


def _params(n_axes):
    return pltpu.CompilerParams(dimension_semantics=("arbitrary",) * n_axes,
                                vmem_limit_bytes=VMEM_LIMIT_BYTES)


def _layer_norm(x):
    mu = jnp.mean(x, axis=-1, keepdims=True)
    xc = x - mu
    var = jnp.mean(xc * xc, axis=-1, keepdims=True)
    return xc * lax.rsqrt(var + LN_EPS)


def _mod_kernel(c_ref, w_ref, b_ref, o_ref):
    c = c_ref[...]
    s = (c * jax.nn.sigmoid(c)).astype(BF16)
    o_ref[0] = jnp.dot(s, w_ref[0].astype(BF16), preferred_element_type=F32) + b_ref[0]


def _modulation(cc, w_mod, b_mod):
    n_out = w_mod.shape[-1]
    return pl.pallas_call(
        _mod_kernel,
        grid=(DEPTH, n_out // D_MODEL),
        in_specs=[pl.BlockSpec((SUBLANES, D_MODEL), lambda l, j: (0, 0)),
                  pl.BlockSpec((1, D_MODEL, D_MODEL), lambda l, j: (l, 0, j)),
                  pl.BlockSpec((1, 1, D_MODEL), lambda l, j: (l, 0, j))],
        out_specs=pl.BlockSpec((1, SUBLANES, D_MODEL), lambda l, j: (l, 0, j)),
        out_shape=jax.ShapeDtypeStruct((DEPTH, SUBLANES, n_out), F32),
        compiler_params=_params(2),
        name="modulation",
    )(cc, w_mod, b_mod.reshape(DEPTH, 1, n_out))


def _rope_half(x, cos, sin_signed):
    lane = lax.broadcasted_iota(jnp.int32, (x.shape[0], LANES), 1)
    first = (lane % 32) < 16
    outs = []
    for cidx in range(x.shape[1] // LANES):
        xc = x[:, cidx * LANES:(cidx + 1) * LANES]
        partner = jnp.where(first, pltpu.roll(xc, LANES - 16, 1), pltpu.roll(xc, 16, 1))
        outs.append(xc * cos + partner * sin_signed)
    return jnp.concatenate(outs, axis=1)


def _inproj_kernel(h_ref, sh_ref, sc_ref, w_ref, b_ref, cos_ref, sin_ref,
                   q_ref, k_ref, v_ref, sb_ref, rest_ref, xn_ref):
    j = pl.program_id(1)

    @pl.when(j == 0)
    def _():
        y = _layer_norm(h_ref[...])
        xn_ref[...] = (y * (1.0 + sc_ref[0]) + sh_ref[0]).astype(BF16)

    acc = jnp.dot(xn_ref[...], w_ref[...], preferred_element_type=F32) + b_ref[...]
    half = BRANCH_WIDTH

    @pl.when(j == 0)
    def _():
        cos = cos_ref[...]
        sin = sin_ref[...]
        q_ref[...] = _rope_half(acc[:, :half], cos, sin).astype(BF16)
        k_ref[...] = _rope_half(acc[:, half:], cos, sin).astype(BF16)

    @pl.when(j == 1)
    def _():
        v_ref[...] = acc[:, :half].astype(BF16)
        sb_ref[...] = acc[:, half:]

    @pl.when(j >= 2)
    def _():
        rest_ref[...] = acc


def _input_projection(h, mod3, mod_base, w_in_bf, b_in, cos_t, sin_t, n_lat, seq):
    m = h.shape[0]
    tm, tn = TM_INPROJ, TN_INPROJ
    n_lat_tiles = n_lat // tm
    tiles_per_seq = seq // tm
    n_groups_lat = n_lat // seq

    def group(i):
        return jnp.where(i < n_lat_tiles, i // tiles_per_seq, n_groups_lat)

    def rope_blk(i):
        return jnp.where(i < n_lat_tiles, i % tiles_per_seq, tiles_per_seq)

    half = BRANCH_WIDTH
    return pl.pallas_call(
        _inproj_kernel,
        grid=(m // tm, P_TOTAL // tn),
        in_specs=[pl.BlockSpec((tm, D_MODEL), lambda i, j: (i, 0)),
                  pl.BlockSpec((1, 1, D_MODEL), lambda i, j: (mod_base + group(i) * 6 + 0, 0, 0)),
                  pl.BlockSpec((1, 1, D_MODEL), lambda i, j: (mod_base + group(i) * 6 + 1, 0, 0)),
                  pl.BlockSpec((D_MODEL, tn), lambda i, j: (0, j)),
                  pl.BlockSpec((1, tn), lambda i, j: (0, j)),
                  pl.BlockSpec((tm, LANES), lambda i, j: (rope_blk(i), 0)),
                  pl.BlockSpec((tm, LANES), lambda i, j: (rope_blk(i), 0))],
        out_specs=[pl.BlockSpec((tm, half), lambda i, j: (i, 0)),
                   pl.BlockSpec((tm, half), lambda i, j: (i, 0)),
                   pl.BlockSpec((tm, half), lambda i, j: (i, 0)),
                   pl.BlockSpec((tm, half), lambda i, j: (i, 0)),
                   pl.BlockSpec((tm, tn), lambda i, j: (i, jnp.maximum(j - 2, 0)))],
        out_shape=[jax.ShapeDtypeStruct((m, half), BF16),
                   jax.ShapeDtypeStruct((m, half), BF16),
                   jax.ShapeDtypeStruct((m, half), BF16),
                   jax.ShapeDtypeStruct((m, half), F32),
                   jax.ShapeDtypeStruct((m, P_TOTAL - 2 * tn), F32)],
        scratch_shapes=[pltpu.VMEM((tm, D_MODEL), BF16)],
        compiler_params=_params(2),
        name="input_projection",
    )(h, mod3, mod3, w_in_bf, b_in.reshape(1, P_TOTAL), cos_t, sin_t)


def _make_rope(seq, tm):
    t = np.arange(seq)
    row_pos, col_pos = t // GRID_W, t % GRID_W
    d = np.arange(LANES) % NA_HEAD_DIM
    m = NA_HEAD_DIM // 4
    inv_freq = (ROPE_BASE ** (-jnp.arange(m, dtype=F32) / m))[d % m]
    pos = np.where((d < 2 * m)[None, :], row_pos[:, None], col_pos[:, None])
    ang = jnp.asarray(pos).astype(F32) * inv_freq[None, :]
    cos = jnp.cos(ang)
    sin = jnp.sin(ang)
    sin_signed = jnp.where(jnp.asarray((d % (2 * m)) < m)[None, :], -sin, sin)
    cos = jnp.concatenate([cos, jnp.ones((tm, LANES), F32)], axis=0)
    sin_signed = jnp.concatenate([sin_signed, jnp.zeros((tm, LANES), F32)], axis=0)
    return cos, sin_signed


def _scan_kernel(fblk, bblk, first, last, seqb,
                 sb_ref, scg_ref, sx_ref, lxf_ref,
                 scgp_ref, sxp_ref, lxfp_ref, scgn_ref, sxn_ref, lxfn_ref,
                 lxb_ref, lxbp_ref, lxbn_ref,
                 scw_ref, cw_ref, cb_ref, sp_ref, wr_ref, wi_ref, br_ref, bi_ref,
                 zb_ref, hf_ref, hb_ref,
                 a_s, b_s, hc_s, st_s, *, ch, n_ctx_items):
    it = pl.program_id(0)
    is_first = first[it] == 1
    is_last = last[it] == 1
    is_ctx = it < n_ctx_items
    b = seqb[it]
    width = BRANCH_WIDTH
    row = lax.broadcasted_iota(jnp.int32, (ch, width), 0)
    not_first = jnp.where(is_first, 0.0, 1.0).astype(F32)
    not_last = jnp.where(is_last, 0.0, 1.0).astype(F32)

    def back1(u, prev_row):
        return jnp.where(row == 0, prev_row, pltpu.roll(u, 1, 0))

    def back2(u, prev2, prev1):
        return jnp.where(row == 0, prev2, jnp.where(row == 1, prev1, pltpu.roll(u, 2, 0)))

    def fwd1(u, next_row):
        return jnp.where(row == ch - 1, next_row, pltpu.roll(u, ch - 1, 0))

    def lru_input(lx_ref, lxp_ref, lxn_ref, prev_ok, next_ok):
        x = lx_ref[...]
        p = lxp_ref[...] * prev_ok
        n = lxn_ref[...] * next_ok
        return (cw_ref[0:1] * back2(x, p[6:7], p[7:8]) + cw_ref[1:2] * back1(x, p[7:8])
                + cw_ref[2:3] * x + cw_ref[3:4] * fwd1(x, n[0:1]) + cb_ref[...])

    def coeffs(d, xm):
        xb = xm.astype(BF16)
        r = jax.nn.sigmoid(jnp.dot(xb, wr_ref[d], preferred_element_type=F32) + br_ref[d:d + 1])
        g = jax.nn.sigmoid(jnp.dot(xb, wi_ref[d], preferred_element_type=F32) + bi_ref[d:d + 1])
        log_a = (-LRU_C * sp_ref[d:d + 1]) * r
        a = jnp.exp(log_a)
        a_s[d] = a
        b_s[d] = jnp.sqrt(-jnp.tanh(log_a) * (a * a + 1.0)) * (g * xm)

    u = scg_ref[...] * sx_ref[...]
    u_prev = scgp_ref[7:8] * sxp_ref[7:8] * not_first
    u_next = scgn_ref[0:1] * sxn_ref[0:1] * not_last
    conv = scw_ref[0:1] * back1(u, u_prev) + scw_ref[1:2] * u + scw_ref[2:3] * fwd1(u, u_next)
    zb_ref[...] = (sb_ref[...] * conv).astype(BF16)

    coeffs(0, lru_input(lxf_ref, lxfp_ref, lxfn_ref, not_first, not_last))
    coeffs(1, lru_input(lxb_ref, lxbp_ref, lxbn_ref, not_last, not_first))

    @pl.when(jnp.logical_and(is_first, is_ctx))
    def _():
        hc_s[...] = jnp.zeros_like(hc_s)

    @pl.when(jnp.logical_and(is_first, jnp.logical_not(is_ctx)))
    def _():
        hc_s[0:1] = st_s[pl.ds(2 * b, 1), :]
        hc_s[1:2] = st_s[pl.ds(2 * b + 1, 1), :]

    def body(i, carry):
        hf, hb = carry
        base = i * SUBLANES
        for r in range(SUBLANES):
            t = base + r
            hf = a_s[0, pl.ds(t, 1), :] * hf + b_s[0, pl.ds(t, 1), :]
            hf_ref[pl.ds(t, 1), :] = hf
            tb = ch - 1 - t
            hb = a_s[1, pl.ds(tb, 1), :] * hb + b_s[1, pl.ds(tb, 1), :]
            hb_ref[pl.ds(tb, 1), :] = hb
        return hf, hb

    hf, hb = lax.fori_loop(0, ch // SUBLANES, body, (hc_s[0:1], hc_s[1:2]))
    hc_s[0:1] = hf
    hc_s[1:2] = hb

    @pl.when(is_ctx)
    def _():
        st_s[pl.ds(2 * b, 1), :] = hf
        st_s[pl.ds(2 * b + 1, 1), :] = hb


def _scan_tables(n_batch, seq, n_ctx, ch):
    assert n_ctx == ch
    nc = seq // ch
    ctx0 = n_batch * seq // ch
    fblk, bblk, first, last, seqb = [], [], [], [], []
    for b in range(n_batch):
        fblk.append(ctx0 + b); bblk.append(ctx0 + b); first.append(1); last.append(1); seqb.append(b)
    for b in range(n_batch):
        for c in range(nc):
            fblk.append(b * nc + c); bblk.append(b * nc + nc - 1 - c)
            first.append(int(c == 0)); last.append(int(c == nc - 1)); seqb.append(b)
    return [np.asarray(a, np.int32) for a in (fblk, bblk, first, last, seqb)]


def _conv_scan(sb, rest, sc_w, lru_cw, lru_cb, sp, wr_bd, wi_bd, b_r, b_i, n_batch, seq, n_ctx):
    m = sb.shape[0]
    ch = SCAN_CHUNK
    width = BRANCH_WIDTH
    tables = _scan_tables(n_batch, seq, n_ctx, ch)
    n_items = len(tables[0])
    halo_per_chunk = ch // SUBLANES
    last_halo = m // SUBLANES - 1

    def cur(col, which):
        return pl.BlockSpec((ch, width), lambda i, f, bk, *_: ((f, bk)[which][i], col))

    def prev(col, which):
        return pl.BlockSpec((SUBLANES, width),
                            lambda i, f, bk, *_: (jnp.maximum((f, bk)[which][i] * halo_per_chunk - 1, 0), col))

    def nxt(col, which):
        return pl.BlockSpec((SUBLANES, width),
                            lambda i, f, bk, *_: (jnp.minimum(((f, bk)[which][i] + 1) * halo_per_chunk, last_halo), col))

    def full(shape):
        return pl.BlockSpec(shape, lambda i, *_: (0,) * len(shape))

    in_specs = [cur(0, 0), cur(0, 0), cur(1, 0), cur(2, 0),
                prev(0, 0), prev(1, 0), prev(2, 0), nxt(0, 0), nxt(1, 0), nxt(2, 0),
                cur(2, 1), prev(2, 1), nxt(2, 1),
                full(sc_w.shape), full(lru_cw.shape), full((1, width)), full(sp.shape),
                full(wr_bd.shape), full(wi_bd.shape), full(b_r.shape), full(b_i.shape)]
    out_specs = [cur(0, 0), cur(0, 0), cur(0, 1)]
    grid_spec = pltpu.PrefetchScalarGridSpec(
        num_scalar_prefetch=5, grid=(n_items,), in_specs=in_specs, out_specs=out_specs,
        scratch_shapes=[pltpu.VMEM((2, ch, width), F32), pltpu.VMEM((2, ch, width), F32),
                        pltpu.VMEM((SUBLANES, width), F32), pltpu.VMEM((2 * n_batch, width), F32)])
    return pl.pallas_call(
        functools.partial(_scan_kernel, ch=ch, n_ctx_items=n_batch),
        grid_spec=grid_spec,
        out_shape=[jax.ShapeDtypeStruct((m, width), BF16),
                   jax.ShapeDtypeStruct((m, width), F32),
                   jax.ShapeDtypeStruct((m, width), F32)],
        compiler_params=_params(1),
        name="conv_scan",
    )(*[jnp.asarray(t) for t in tables],
      sb, rest, rest, rest, rest, rest, rest, rest, rest, rest, rest, rest, rest,
      sc_w, lru_cw, lru_cb.reshape(1, width), sp, wr_bd, wi_bd, b_r, b_i)


def _attention_kernel(qblk, bat, r0t, cls,
                      q_ref, k_ref, v_ref, kc_ref, vc_ref, bias_ref, o_ref, *, band):
    it = pl.program_id(0)
    start = pl.multiple_of(r0t[it] * GRID_W, GRID_W)
    nq = q_ref.shape[0]
    lane = lax.broadcasted_iota(jnp.int32, (nq, LANES), 1)
    low = lane < NA_HEAD_DIM
    scale = NA_HEAD_DIM ** -0.5
    nt = (((1,), (1,)), ((), ()))
    for hp in range(NA_HEADS // 2):
        cols = slice(hp * LANES, (hp + 1) * LANES)
        qp = q_ref[:, cols].astype(F32)
        qs = jnp.concatenate([jnp.where(low, qp, 0.0), jnp.where(low, 0.0, qp)], axis=0).astype(BF16)
        kb = k_ref[pl.ds(start, band), cols]
        vb = v_ref[pl.ds(start, band), cols]
        s_loc = lax.dot_general(qs, kb, nt, preferred_element_type=F32)
        s_ctx = lax.dot_general(qs, kc_ref[:, cols], nt, preferred_element_type=F32)
        bias = jnp.concatenate([bias_ref[0, 2 * hp], bias_ref[0, 2 * hp + 1]], axis=0)
        s_loc = s_loc * scale + bias
        s_ctx = s_ctx * scale
        mx = jnp.maximum(jnp.max(s_loc, axis=-1, keepdims=True), jnp.max(s_ctx, axis=-1, keepdims=True))
        e_loc = jnp.exp(s_loc - mx)
        e_ctx = jnp.exp(s_ctx - mx)
        den = jnp.sum(e_loc, axis=-1, keepdims=True) + jnp.sum(e_ctx, axis=-1, keepdims=True)
        o = (jnp.dot(e_loc.astype(BF16), vb, preferred_element_type=F32)
             + jnp.dot(e_ctx.astype(BF16), vc_ref[:, cols], preferred_element_type=F32)) / den
        o_ref[:, cols] = jnp.where(low, o[:nq], o[nq:]).astype(BF16)


def _attention_tables(n_batch, seq, n_ctx, with_ctx_queries):
    rows = seq // GRID_W
    kr = min(NA_WIN_ROWS, rows)
    qblk, bat, r0t, cls = [], [], [], []
    ctx_q0 = n_batch * seq // ATT_QROWS
    for b in range(n_batch):
        for r in range(rows):
            r0 = min(max(r - kr // 2, 0), rows - kr)
            qblk.append(b * rows + r); bat.append(b); r0t.append(r0); cls.append(r - r0)
        if with_ctx_queries:
            for c in range(n_ctx // ATT_QROWS):
                qblk.append(ctx_q0 + b * (n_ctx // ATT_QROWS) + c); bat.append(b); r0t.append(0); cls.append(kr)
    return [np.asarray(a, np.int32) for a in (qblk, bat, r0t, cls)]


def _attention_bias(rpb, seq):
    rows = seq // GRID_W
    kr = min(NA_WIN_ROWS, rows)
    kc = NA_WIN_COLS
    cq = np.arange(GRID_W)
    c0 = np.clip(cq - kc // 2, 0, GRID_W - kc)
    ck = np.arange(GRID_W)
    inside = (ck[None, :] >= c0[:, None]) & (ck[None, :] < c0[:, None] + kc)
    dc = np.clip(ck[None, :] - cq[:, None] + (NA_WIN_COLS - 1), 0, 2 * NA_WIN_COLS - 2)
    n_dr = 2 * NA_WIN_ROWS - 1
    table = jnp.where(jnp.asarray(inside)[None, None], rpb[:, :, dc], NEG_BIG)
    table = table.transpose(0, 2, 1, 3).reshape(NA_HEADS, GRID_W, n_dr * GRID_W)
    classes = []
    for cl in range(kr):
        lo = (NA_WIN_ROWS - 1 - cl) * GRID_W
        classes.append(table[:, :, lo:lo + kr * GRID_W])
    classes.append(jnp.full((NA_HEADS, GRID_W, kr * GRID_W), NEG_BIG, F32))
    return jnp.stack(classes, axis=0)


def _attention(q, k, v, bias, n_batch, seq, n_ctx, with_ctx_queries):
    m = q.shape[0] if with_ctx_queries else n_batch * seq
    rows = seq // GRID_W
    kr = min(NA_WIN_ROWS, rows)
    band = kr * GRID_W
    tables = _attention_tables(n_batch, seq, n_ctx, with_ctx_queries)
    n_items = len(tables[0])
    ctx_blk0 = n_batch * seq // n_ctx
    width = NA_WIDTH
    in_specs = [pl.BlockSpec((ATT_QROWS, width), lambda i, qb, bt, r0, cl: (qb[i], 0)),
                pl.BlockSpec((seq, width), lambda i, qb, bt, r0, cl: (bt[i], 0)),
                pl.BlockSpec((seq, width), lambda i, qb, bt, r0, cl: (bt[i], 0)),
                pl.BlockSpec((n_ctx, width), lambda i, qb, bt, r0, cl: (ctx_blk0 + bt[i], 0)),
                pl.BlockSpec((n_ctx, width), lambda i, qb, bt, r0, cl: (ctx_blk0 + bt[i], 0)),
                pl.BlockSpec((1, NA_HEADS, GRID_W, band), lambda i, qb, bt, r0, cl: (cl[i], 0, 0, 0))]
    out_specs = pl.BlockSpec((ATT_QROWS, width), lambda i, qb, bt, r0, cl: (qb[i], 0))
    grid_spec = pltpu.PrefetchScalarGridSpec(num_scalar_prefetch=4, grid=(n_items,),
                                             in_specs=in_specs, out_specs=out_specs)
    return pl.pallas_call(
        functools.partial(_attention_kernel, band=band),
        grid_spec=grid_spec,
        out_shape=jax.ShapeDtypeStruct((m, width), BF16),
        compiler_params=_params(1),
        name="attention",
    )(*[jnp.asarray(t) for t in tables], q, k, v, k, v, bias)


def _post_kernel(h_ref, att_ref, zb_ref, hf_ref, hb_ref, lg_ref, gl0_ref, gl1_ref, gl2_ref,
                 g1_ref, sh2_ref, sc2_ref,
                 wpa_ref, wpc_ref, wpl_ref, wo_ref, bo_ref, l1g_ref, l1b_ref, rw_ref, rb_ref, tri_ref, upper_ref,
                 h1_ref, xs_ref, pos_ref, cnt_ref):
    y_a = jnp.dot(att_ref[...], wpa_ref[...], preferred_element_type=F32)
    y_b = jnp.dot(zb_ref[...], wpc_ref[...], preferred_element_type=F32)
    zc = jax.nn.gelu(lg_ref[...]) * (hf_ref[...] + hb_ref[...])
    y_c = jnp.dot(zc.astype(BF16), wpl_ref[...], preferred_element_type=F32)
    merged = (jax.nn.sigmoid(gl0_ref[...]) * y_a + jax.nn.sigmoid(gl1_ref[...]) * y_b
              + jax.nn.sigmoid(gl2_ref[...]) * y_c)
    y = jnp.dot(merged.astype(BF16), wo_ref[...], preferred_element_type=F32) + bo_ref[...]
    h1 = _layer_norm(DEEPNORM_ALPHA * h_ref[...] + g1_ref[0] * y) * l1g_ref[...] + l1b_ref[...]
    h1_ref[...] = h1
    u2 = _layer_norm(h1) * (1.0 + sc2_ref[0]) + sh2_ref[0]

    logits =jnp.dot(u2, rw_ref[...], precision=lax.Precision.HIGHEST, preferred_element_type=F32) + rb_ref[...]
    tm = logits.shape[0]
    lane = lax.broadcasted_iota(jnp.int32, (tm, LANES), 1)
    lane_f = lane.astype(F32)
    work = logits
    tops, idxs, hots = [], [], []
    for _ in range(TOP_K):
        mx = jnp.max(work, axis=-1, keepdims=True)
        idx = jnp.min(jnp.where(work == mx, lane_f, float(LANES)), axis=-1, keepdims=True)
        hot = lane_f == idx
        work = jnp.where(hot, -3e38, work)
        tops.append(mx); idxs.append(idx); hots.append(hot)
    exps = [jnp.exp(t - tops[0]) for t in tops]
    den = exps[0] + exps[1] + exps[2] + exps[3]
    hot_all = jnp.zeros((tm, LANES), F32)
    for hot in hots:
        hot_all = hot_all + hot.astype(F32)
    cnt = jnp.sum(hot_all, axis=0, keepdims=True)
    cnt_pad = jnp.floor((cnt + (SUBLANES - 1.0)) * (1.0 / SUBLANES)) * SUBLANES
    off = jnp.dot(jnp.broadcast_to(cnt_pad, (SUBLANES, LANES)), upper_ref[...],
                  precision=lax.Precision.HIGHEST, preferred_element_type=F32)[0:1]
    slot = off + jnp.dot(tri_ref[...], hot_all.astype(BF16), preferred_element_type=F32)
    pos4 = jnp.zeros((tm, LANES), F32)
    weights = []
    for kk in range(TOP_K):
        pos_k = jnp.sum(jnp.where(hots[kk], slot, 0.0), axis=-1, keepdims=True)
        pos4 = jnp.where(lane == kk, pos_k, pos4)
        p = exps[kk] / den
        p_hi = p.astype(BF16).astype(F32)
        p_mid = (p - p_hi).astype(BF16).astype(F32)
        p_lo = p - p_hi - p_mid
        weights.append(jnp.where(lane == 0, p_hi, jnp.where(lane == 1, p_mid, jnp.where(lane == 2, p_lo, 0.0))))
    pos_ref[...] = pos4
    sub = lax.broadcasted_iota(jnp.int32, (SUBLANES, LANES), 0)
    cnt_ref[...] = jnp.where(sub == 0, cnt, jnp.where(sub == 1, off, 0.0))

    n_sorted = xs_ref.shape[0]
    pos_t = pos4.T
    r_iota = lax.broadcasted_iota(jnp.int32, (n_sorted, tm), 0).astype(F32)
    perm = jnp.zeros((n_sorted, tm), F32)
    w_sorted = jnp.zeros((n_sorted, LANES), F32)
    for kk in range(TOP_K):
        sel_k = (r_iota == pos_t[kk:kk + 1, :]).astype(F32)
        perm = perm + sel_k
        w_sorted = w_sorted + jnp.dot(sel_k.astype(BF16), weights[kk].astype(BF16), preferred_element_type=F32)
    xs_ref[:, :D_MODEL] = jnp.dot(perm.astype(BF16), u2.astype(BF16), preferred_element_type=F32)
    xs_ref[:, D_MODEL:] = w_sorted


def _post_mixer(h, att, zb, hf, hb, rest, mod3, mod_base, wpa, wpc, wpl, wo, b_o, l1g, l1b, rw_pad, rb_pad,
                n_rows, n_lat, seq):
    tm = TM_POST
    width = BRANCH_WIDTH
    tiles_per_seq = seq // tm
    n_lat_tiles = n_lat // tm
    n_groups_lat = n_lat // seq
    tri = jnp.asarray(np.tril(np.ones((tm, tm), np.float32), -1), BF16)
    upper = jnp.asarray(np.triu(np.ones((LANES, LANES), np.float32), 1))
    n_tiles = n_rows // tm

    def group(i):
        return jnp.where(i < n_lat_tiles, i // tiles_per_seq, n_groups_lat)

    def rows(wd, col=0):
        return pl.BlockSpec((tm, wd), lambda i: (i, col))

    def full(shape):
        return pl.BlockSpec(shape, lambda i: (0,) * len(shape))

    def mod(which):
        return pl.BlockSpec((1, 1, D_MODEL), lambda i: (mod_base + group(i) * 6 + which, 0, 0))

    in_specs = [rows(D_MODEL), rows(width), rows(width), rows(width), rows(width), rows(width, 3),
                rows(D_MODEL, 2), rows(D_MODEL, 3), rows(D_MODEL, 4),
                mod(2), mod(3), mod(4),
                full(wpa.shape), full(wpc.shape), full(wpl.shape), full(wo.shape), full((1, D_MODEL)),
                full((1, D_MODEL)), full((1, D_MODEL)), full(rw_pad.shape), full(rb_pad.shape), full(tri.shape),
                full(upper.shape)]
    out_specs = [rows(D_MODEL), pl.BlockSpec((TOK_BLOCK, XS_WIDTH), lambda i: (i, 0)), rows(LANES),
                 pl.BlockSpec((SUBLANES, LANES), lambda i: (i, 0))]
    return pl.pallas_call(
        _post_kernel,
        grid=(n_tiles,),
        in_specs=in_specs, out_specs=out_specs,
        out_shape=[jax.ShapeDtypeStruct((n_rows, D_MODEL), F32),
                   jax.ShapeDtypeStruct((n_tiles * TOK_BLOCK, XS_WIDTH), F32),
                   jax.ShapeDtypeStruct((n_rows, LANES), F32),
                   jax.ShapeDtypeStruct((n_tiles * SUBLANES, LANES), F32)],
        compiler_params=_params(1),
        name="post_mixer",
    )(h, att, zb, hf, hb, rest, rest, rest, rest, mod3, mod3, mod3,
      wpa, wpc, wpl, wo, b_o.reshape(1, D_MODEL), l1g.reshape(1, D_MODEL), l1b.reshape(1, D_MODEL),
      rw_pad, rb_pad, tri, upper)


def _expert_plan(cnt_out, n_tok_tiles, n_rows):
    tm = TM_EXPERT
    co = cnt_out.reshape(n_tok_tiles, SUBLANES, LANES)
    cnt = (co[:, 0, :N_EXPERTS].astype(jnp.int32) + SUBLANES - 1) // SUBLANES * SUBLANES
    off = co[:, 1, :N_EXPERTS].astype(jnp.int32)
    cum_end = jnp.cumsum(cnt, axis=0)
    cum = cum_end - cnt
    total = cum_end[-1]
    n_et = (total + tm - 1) // tm
    et_end = jnp.cumsum(n_et)
    n_act = et_end[-1:].astype(jnp.int32)
    n_tiles = -(-n_tok_tiles * TOK_BLOCK // tm) + N_EXPERTS
    j = jnp.arange(n_tiles, dtype=jnp.int32)
    tile_e = jnp.minimum(jnp.sum((et_end[None, :] <= j[:, None]).astype(jnp.int32), axis=1), N_EXPERTS - 1)
    row0 = (j - (et_end - n_et)[tile_e]) * tm
    n_rows_tile = jnp.clip(total[tile_e] - row0, 0, tm)
    cum_e = cum.T[tile_e]
    cum_end_e = cum_end.T[tile_e]
    off_e = off.T[tile_e]
    q = row0[:, None] + SUBLANES * jnp.arange(tm // SUBLANES, dtype=jnp.int32)[None, :]
    tok_tile = jnp.minimum(jnp.sum((cum_end_e[:, None, :] <= q[:, :, None]).astype(jnp.int32), axis=-1),
                           n_tok_tiles - 1)
    src = (tok_tile * TOK_BLOCK + jnp.take_along_axis(off_e, tok_tile, axis=1)
           + q - jnp.take_along_axis(cum_e, tok_tile, axis=1))
    used = off[:, N_EXPERTS - 1] + cnt[:, N_EXPERTS - 1]
    return (tile_e.astype(jnp.int32), n_rows_tile.astype(jnp.int32), n_act, src.reshape(-1).astype(jnp.int32),
            used.astype(jnp.int32), n_tiles)


def _expert_kernel(tile_e, n_rows_t, n_act, src_t, used_t,
                   xs_hbm, wgu_ref, bgu_ref, wdn_ref, bdn_ref, ys_hbm,
                   xin, yout, wgu_s, wdn_s, zeros, sem_in, sem_out, sem_zero, *, tm, tok_block, n_tok_tiles):
    j = pl.program_id(0)
    na = n_act[0]
    chunks = tm // SUBLANES

    def gather(src, dst, size, slot):
        return pltpu.make_async_copy(xs_hbm.at[pl.ds(src, size)], xin.at[slot, pl.ds(dst, size)], sem_in.at[slot])

    def scatter(src, dst, size, slot):
        return pltpu.make_async_copy(yout.at[slot, pl.ds(dst, size)], ys_hbm.at[pl.ds(src, size)], sem_out.at[slot])

    def start_chunks(jj, slot, copy):
        def body(c, carry):
            src = pl.multiple_of(src_t[jj * chunks + c], SUBLANES)
            copy(src, pl.multiple_of(c * SUBLANES, SUBLANES), SUBLANES, slot).start()
            return carry

        lax.fori_loop(0, lax.shift_right_logical(n_rows_t[jj], 3), body, 0)

    def wait_rows(jj, slot, copy):
        n = n_rows_t[jj]
        size = tm
        while size >= SUBLANES:
            @pl.when((n & size) != 0)
            def _(size=size):
                copy(0, 0, size, slot).wait()
            size //= 2

    slot = j % 2

    @pl.when(j == 0)
    def _():
        xin[...] = jnp.zeros_like(xin)
        start_chunks(0, 0, gather)
        zeros[...] = jnp.zeros_like(zeros)

        def clear_tail(i, copy_op):
            used = used_t[i]
            tail = tok_block - used
            for size in (256, 128, 64, 32, 16, 8):
                @pl.when((tail & size) != 0)
                def _():
                    at = pl.multiple_of(i * tok_block + used + (tail & ~(2 * size - 1)), SUBLANES)
                    copy_op(pltpu.make_async_copy(zeros.at[pl.ds(0, size)], ys_hbm.at[pl.ds(at, size)], sem_zero))

        def start_clear(i, c):
            clear_tail(i, lambda cp: cp.start())
            return c

        def wait_clear(i, c):
            clear_tail(i, lambda cp: cp.wait())
            return c

        lax.fori_loop(0, n_tok_tiles, start_clear, 0)
        lax.fori_loop(0, n_tok_tiles, wait_clear, 0)

    @pl.when(j + 1 < na)
    def _():
        start_chunks(j + 1, 1 - slot, gather)

    @pl.when(j < na)
    def _():
        e = tile_e[j]
        e_prev = tile_e[jnp.maximum(j - 1, 0)]

        @pl.when(jnp.logical_or(j == 0, e != e_prev))
        def _():
            wgu_s[...] = wgu_ref[0, 0].astype(BF16)
            wdn_s[...] = wdn_ref[0, 0].astype(BF16)

        wait_rows(j, slot, gather)
        x = xin[slot]
        p = x[:, D_MODEL:D_MODEL + 1] + x[:, D_MODEL + 1:D_MODEL + 2] + x[:, D_MODEL + 2:D_MODEL + 3]
        gu = jnp.dot(x[:, :D_MODEL].astype(BF16), wgu_s[...], preferred_element_type=F32) + bgu_ref[0, 0]
        f = gu.shape[1] // 2
        gate = jnp.minimum(gu[:, :f], SWIGLU_LIMIT)
        up = jnp.clip(gu[:, f:], -SWIGLU_LIMIT, SWIGLU_LIMIT)
        hid = (up + 1.0) * gate * jax.nn.sigmoid(SWIGLU_ALPHA * gate)
        y = jnp.dot(hid.astype(BF16), wdn_s[...], preferred_element_type=F32) + bdn_ref[0, 0]
        yout[slot] = y * p
        start_chunks(j, slot, scatter)

        @pl.when(j >= 1)
        def _():
            wait_rows(j - 1, 1 - slot, scatter)

        @pl.when(j == na - 1)
        def _():
            wait_rows(j, slot, scatter)


def _experts(xs, plan, layer, w_gu, b_gu, w_dn, b_dn):
    tile_e, n_rows_tile, n_act, src, used, n_tiles = plan
    tm = TM_EXPERT
    f2 = w_gu.shape[-1]

    def expert(j, te, nr, na, *_):
        return (layer, te[jnp.minimum(j, na[0] - 1)], 0, 0)

    in_specs = [pl.BlockSpec(memory_space=pl.ANY),
                pl.BlockSpec((1, 1, D_MODEL, f2), expert),
                pl.BlockSpec((1, 1, 1, f2), expert),
                pl.BlockSpec((1, 1, f2 // 2, D_MODEL), expert),
                pl.BlockSpec((1, 1, 1, D_MODEL), expert)]
    grid_spec = pltpu.PrefetchScalarGridSpec(
        num_scalar_prefetch=5, grid=(n_tiles,), in_specs=in_specs,
        out_specs=pl.BlockSpec(memory_space=pl.ANY),
        scratch_shapes=[pltpu.VMEM((2, tm, XS_WIDTH), F32), pltpu.VMEM((2, tm, D_MODEL), F32),
                        pltpu.VMEM((D_MODEL, f2), BF16), pltpu.VMEM((f2 // 2, D_MODEL), BF16),
                        pltpu.VMEM((ZERO_ROWS, D_MODEL), F32),
                        pltpu.SemaphoreType.DMA((2,)), pltpu.SemaphoreType.DMA((2,)), pltpu.SemaphoreType.DMA(())])
    return pl.pallas_call(
        functools.partial(_expert_kernel, tm=tm, tok_block=TOK_BLOCK, n_tok_tiles=xs.shape[0] // TOK_BLOCK),
        grid_spec=grid_spec,
        out_shape=jax.ShapeDtypeStruct((xs.shape[0], D_MODEL), F32),
        compiler_params=_params(1),
        name="moe_experts",
    )(tile_e, n_rows_tile, n_act, src, used, xs, w_gu, b_gu.reshape(DEPTH, N_EXPERTS, 1, f2), w_dn,
      b_dn.reshape(DEPTH, N_EXPERTS, 1, D_MODEL))


def _combine_kernel(ys_ref, pos_ref, h1_ref, g2_ref, l2g_ref, l2b_ref, o_ref):
    tm = pos_ref.shape[0]
    n_sorted = ys_ref.shape[0]
    pos = pos_ref[...]
    col = lax.broadcasted_iota(jnp.int32, (tm, n_sorted), 1).astype(F32)
    sel = jnp.zeros((tm, n_sorted), F32)
    for kk in range(TOP_K):
        sel = sel + (col == pos[:, kk:kk + 1]).astype(F32)
    sel = sel.astype(BF16)
    ys = ys_ref[...]
    hi = ys.astype(BF16)
    rest = ys - hi.astype(F32)
    mid = rest.astype(BF16)
    lo = (rest - mid.astype(F32)).astype(BF16)
    y2 = (jnp.dot(sel, hi, preferred_element_type=F32) + jnp.dot(sel, mid, preferred_element_type=F32)
          + jnp.dot(sel, lo, preferred_element_type=F32))
    o_ref[...] = _layer_norm(DEEPNORM_ALPHA * h1_ref[...] + g2_ref[0] * y2) * l2g_ref[...] + l2b_ref[...]


def _combine(ys, pos4, h1, mod3, mod_base, l2g, l2b, n_lat, seq):
    n_rows = h1.shape[0]
    tm = TM_POST
    tiles_per_seq = seq // tm
    n_lat_tiles = n_lat // tm
    n_groups_lat = n_lat // seq

    def group(i):
        return jnp.where(i < n_lat_tiles, i // tiles_per_seq, n_groups_lat)

    in_specs = [pl.BlockSpec((TOK_BLOCK, D_MODEL), lambda i: (i, 0)),
                pl.BlockSpec((tm, LANES), lambda i: (i, 0)),
                pl.BlockSpec((tm, D_MODEL), lambda i: (i, 0)),
                pl.BlockSpec((1, 1, D_MODEL), lambda i: (mod_base + group(i) * 6 + 5, 0, 0)),
                pl.BlockSpec((1, D_MODEL), lambda i: (0, 0)),
                pl.BlockSpec((1, D_MODEL), lambda i: (0, 0))]
    return pl.pallas_call(
        _combine_kernel,
        grid=(n_rows // tm,),
        in_specs=in_specs,
        out_specs=pl.BlockSpec((tm, D_MODEL), lambda i: (i, 0)),
        out_shape=jax.ShapeDtypeStruct((n_rows, D_MODEL), F32),
        compiler_params=_params(1),
        name="moe_combine",
    )(ys, pos4, h1, mod3, l2g.reshape(1, D_MODEL), l2b.reshape(1, D_MODEL))


def _block_diag(w):
    two, n, d, e = w.shape
    eye = jnp.eye(n, dtype=w.dtype)
    return (w[:, :, :, None, :] * eye[None, :, None, :, None]).reshape(two, n * d, n * e)


def kernel(x, c, ctx, c_ctx, w_mod, b_mod, w_in, b_in, na_rpb, w_proj_attn, w_proj_conv, w_proj_lru, sc_conv_w, lru_conv_w, lru_conv_b, lru_lambda, lru_w_r, lru_b_r, lru_w_i, lru_b_i, w_o, b_o, ln1_g, ln1_b, router_w, router_b, exp_w_gu, exp_b_gu, exp_w_dn, exp_b_dn, ln2_g, ln2_b):
    n_batch, seq, d = x.shape
    n_ctx = ctx.shape[1]
    n_lat = n_batch * seq
    n_all = n_lat + n_batch * n_ctx
    assert d == D_MODEL and n_batch + 1 <= SUBLANES

    cc = jnp.concatenate([c, c_ctx[None], jnp.zeros((SUBLANES - n_batch - 1, d), F32)], axis=0)
    mod = _modulation(cc, w_mod, b_mod)
    groups = n_batch + 1
    mod3 = mod.reshape(DEPTH, SUBLANES, 6, d)[:, :groups].reshape(DEPTH * groups * 6, 1, d)

    cos_t, sin_t = _make_rope(seq, TM_INPROJ)
    h = jnp.concatenate([x.reshape(n_lat, d), ctx.reshape(n_batch * n_ctx, d)], axis=0)

    for layer in range(DEPTH):
        last = layer == DEPTH - 1
        mod_base = layer * groups * 6
        q, k, v, sb, rest = _input_projection(h, mod3, mod_base, w_in[layer].astype(BF16), b_in[layer],
                                              cos_t, sin_t, n_lat, seq)
        sp = jax.nn.softplus(-lru_lambda[layer])
        zb, hf, hb = _conv_scan(sb, rest, sc_conv_w[layer], lru_conv_w[layer], lru_conv_b[layer], sp,
                                _block_diag(lru_w_r[layer]).astype(BF16), _block_diag(lru_w_i[layer]).astype(BF16),
                                lru_b_r[layer], lru_b_i[layer], n_batch, seq, n_ctx)
        att = _attention(q, k, v, _attention_bias(na_rpb[layer], seq), n_batch, seq, n_ctx, not last)
        n_rows = n_lat if last else n_all
        rw_pad = jnp.pad(router_w[layer], ((0, 0), (0, LANES - N_EXPERTS)))
        rb_pad = jnp.concatenate([router_b[layer], jnp.full((LANES - N_EXPERTS,), NEG_BIG, F32)]).reshape(1, LANES)
        h1, xs, pos4, cnt_out = _post_mixer(
            h, att, zb, hf, hb, rest, mod3, mod_base,
            w_proj_attn[layer].astype(BF16), w_proj_conv[layer].astype(BF16), w_proj_lru[layer].astype(BF16),
            w_o[layer].astype(BF16), b_o[layer], ln1_g[layer], ln1_b[layer], rw_pad, rb_pad, n_rows, n_lat, seq)
        plan = _expert_plan(cnt_out, n_rows // TM_POST, n_rows)
        ys = _experts(xs, plan, layer, exp_w_gu, exp_b_gu, exp_w_dn, exp_b_dn)
        h = _combine(ys, pos4, h1, mod3, mod_base, ln2_g[layer], ln2_b[layer], n_lat, seq)
    return h.reshape(n_batch, seq, d)
```

```python
import functools

import numpy as np
import jax
import jax.numpy as jnp
from jax import lax
from jax.experimental import pallas as pl
from jax.experimental.pallas import tpu as pltpu

D_MODEL = 1024
DEPTH = 2
GRID_W = 64
NA_HEADS = 8
NA_HEAD_DIM = 64
NA_WIDTH = NA_HEADS * NA_HEAD_DIM
NA_WIN_ROWS = 8
NA_WIN_COLS = 16
ROPE_BASE = 10000.0
BRANCH_WIDTH = 512
LRU_BLOCKS = 8
LRU_C = 8.0
P_TOTAL = 7168
N_EARLY = 8 * BRANCH_WIDTH
N_EXPERTS = 32
TOP_K = 4
SWIGLU_LIMIT = 7.0
SWIGLU_ALPHA = 1.702
LN_EPS = 1e-5
DEEPNORM_ALPHA = (2 * DEPTH) ** 0.25
NEG_BIG = -1e30

LANES = 128
SUBLANES = 8
VMEM_LIMIT_BYTES = 56 * 1024 * 1024

TM_INPROJ = 1024
TN_INPROJ = 1024
SCAN_CHUNK = 256
ATT_QROWS = GRID_W
TM_POST = 256
TM_EXPERT = 512
XS_WIDTH = D_MODEL + LANES
TOK_BLOCK = TM_POST * TOP_K + N_EXPERTS * SUBLANES
ZERO_ROWS = 256

F32 = jnp.float32
BF16 = jnp.bfloat16


def _params(n_axes):
    return pltpu.CompilerParams(dimension_semantics=("arbitrary",) * n_axes,
                                vmem_limit_bytes=VMEM_LIMIT_BYTES)


def _layer_norm(x):
    mu = jnp.mean(x, axis=-1, keepdims=True)
    xc = x - mu
    var = jnp.mean(xc * xc, axis=-1, keepdims=True)
    return xc * lax.rsqrt(var + LN_EPS)


def _mod_kernel(c_ref, w_ref, b_ref, o_ref):
    c = c_ref[...]
    s = (c * jax.nn.sigmoid(c)).astype(BF16)
    o_ref[0] = jnp.dot(s, w_ref[0].astype(BF16), preferred_element_type=F32) + b_ref[0]


def _modulation(cc, w_mod, b_mod):
    n_out = w_mod.shape[-1]
    return pl.pallas_call(
        _mod_kernel,
        grid=(DEPTH, n_out // D_MODEL),
        in_specs=[pl.BlockSpec((SUBLANES, D_MODEL), lambda l, j: (0, 0)),
                  pl.BlockSpec((1, D_MODEL, D_MODEL), lambda l, j: (l, 0, j)),
                  pl.BlockSpec((1, 1, D_MODEL), lambda l, j: (l, 0, j))],
        out_specs=pl.BlockSpec((1, SUBLANES, D_MODEL), lambda l, j: (l, 0, j)),
        out_shape=jax.ShapeDtypeStruct((DEPTH, SUBLANES, n_out), F32),
        compiler_params=_params(2),
        name="modulation",
    )(cc, w_mod, b_mod.reshape(DEPTH, 1, n_out))


def _rope_half(x, cos, sin_signed):
    lane = lax.broadcasted_iota(jnp.int32, (x.shape[0], LANES), 1)
    first = (lane % 32) < 16
    outs = []
    for cidx in range(x.shape[1] // LANES):
        xc = x[:, cidx * LANES:(cidx + 1) * LANES]
        partner = jnp.where(first, pltpu.roll(xc, LANES - 16, 1), pltpu.roll(xc, 16, 1))
        outs.append(xc * cos + partner * sin_signed)
    return jnp.concatenate(outs, axis=1)


def _inproj_kernel(h_ref, sh_ref, sc_ref, w_ref, b_ref, cos_ref, sin_ref,
                   q_ref, k_ref, v_ref, sb_ref, rest_ref, xn_ref):
    j = pl.program_id(1)

    @pl.when(j == 0)
    def _():
        y = _layer_norm(h_ref[...])
        xn_ref[...] = (y * (1.0 + sc_ref[0]) + sh_ref[0]).astype(BF16)

    acc = jnp.dot(xn_ref[...], w_ref[...], preferred_element_type=F32) + b_ref[...]
    half = BRANCH_WIDTH

    @pl.when(j == 0)
    def _():
        cos = cos_ref[...]
        sin = sin_ref[...]
        q_ref[...] = _rope_half(acc[:, :half], cos, sin).astype(BF16)
        k_ref[...] = _rope_half(acc[:, half:], cos, sin).astype(BF16)

    @pl.when(j == 1)
    def _():
        v_ref[...] = acc[:, :half].astype(BF16)
        sb_ref[...] = acc[:, half:]

    @pl.when(j >= 2)
    def _():
        rest_ref[...] = acc


def _input_projection(h, mod3, mod_base, w_in_bf, b_in, cos_t, sin_t, n_lat, seq):
    m = h.shape[0]
    tm, tn = TM_INPROJ, TN_INPROJ
    n_lat_tiles = n_lat // tm
    tiles_per_seq = seq // tm
    n_groups_lat = n_lat // seq

    def group(i):
        return jnp.where(i < n_lat_tiles, i // tiles_per_seq, n_groups_lat)

    def rope_blk(i):
        return jnp.where(i < n_lat_tiles, i % tiles_per_seq, tiles_per_seq)

    half = BRANCH_WIDTH
    n_cols = w_in_bf.shape[1]
    return pl.pallas_call(
        _inproj_kernel,
        grid=(m // tm, n_cols // tn),
        in_specs=[pl.BlockSpec((tm, D_MODEL), lambda i, j: (i, 0)),
                  pl.BlockSpec((1, 1, D_MODEL), lambda i, j: (mod_base + group(i) * 6 + 0, 0, 0)),
                  pl.BlockSpec((1, 1, D_MODEL), lambda i, j: (mod_base + group(i) * 6 + 1, 0, 0)),
                  pl.BlockSpec((D_MODEL, tn), lambda i, j: (0, j)),
                  pl.BlockSpec((1, tn), lambda i, j: (0, j)),
                  pl.BlockSpec((tm, LANES), lambda i, j: (rope_blk(i), 0)),
                  pl.BlockSpec((tm, LANES), lambda i, j: (rope_blk(i), 0))],
        out_specs=[pl.BlockSpec((tm, half), lambda i, j: (i, 0)),
                   pl.BlockSpec((tm, half), lambda i, j: (i, 0)),
                   pl.BlockSpec((tm, half), lambda i, j: (i, 0)),
                   pl.BlockSpec((tm, half), lambda i, j: (i, 0)),
                   pl.BlockSpec((tm, tn), lambda i, j: (i, jnp.maximum(j - 2, 0)))],
        out_shape=[jax.ShapeDtypeStruct((m, half), BF16),
                   jax.ShapeDtypeStruct((m, half), BF16),
                   jax.ShapeDtypeStruct((m, half), BF16),
                   jax.ShapeDtypeStruct((m, half), F32),
                   jax.ShapeDtypeStruct((m, n_cols - 2 * tn), F32)],
        scratch_shapes=[pltpu.VMEM((tm, D_MODEL), BF16)],
        compiler_params=_params(2),
        name="input_projection",
    )(h, mod3, mod3, w_in_bf, b_in.reshape(1, n_cols), cos_t, sin_t)


def _make_rope(seq, tm):
    t = np.arange(seq)
    row_pos, col_pos = t // GRID_W, t % GRID_W
    d = np.arange(LANES) % NA_HEAD_DIM
    m = NA_HEAD_DIM // 4
    inv_freq = (ROPE_BASE ** (-jnp.arange(m, dtype=F32) / m))[d % m]
    pos = np.where((d < 2 * m)[None, :], row_pos[:, None], col_pos[:, None])
    ang = jnp.asarray(pos).astype(F32) * inv_freq[None, :]
    cos = jnp.cos(ang)
    sin = jnp.sin(ang)
    sin_signed = jnp.where(jnp.asarray((d % (2 * m)) < m)[None, :], -sin, sin)
    cos = jnp.concatenate([cos, jnp.ones((tm, LANES), F32)], axis=0)
    sin_signed = jnp.concatenate([sin_signed, jnp.zeros((tm, LANES), F32)], axis=0)
    return cos, sin_signed


def _scan_kernel(fblk, bblk, first, last, seqb,
                 sb_ref, scg_ref, sx_ref, lxf_ref,
                 scgp_ref, sxp_ref, lxfp_ref, scgn_ref, sxn_ref, lxfn_ref,
                 lxb_ref, lxbp_ref, lxbn_ref,
                 scw_ref, cw_ref, cb_ref, sp_ref, wr_ref, wi_ref, br_ref, bi_ref,
                 zb_ref, hf_ref, hb_ref,
                 a_s, b_s, hc_s, st_s, *, ch, n_ctx_items):
    it = pl.program_id(0)
    is_first = first[it] == 1
    is_last = last[it] == 1
    is_ctx = it < n_ctx_items
    b = seqb[it]
    width = BRANCH_WIDTH
    row = lax.broadcasted_iota(jnp.int32, (ch, width), 0)
    not_first = jnp.where(is_first, 0.0, 1.0).astype(F32)
    not_last = jnp.where(is_last, 0.0, 1.0).astype(F32)

    def back1(u, prev_row):
        return jnp.where(row == 0, prev_row, pltpu.roll(u, 1, 0))

    def back2(u, prev2, prev1):
        return jnp.where(row == 0, prev2, jnp.where(row == 1, prev1, pltpu.roll(u, 2, 0)))

    def fwd1(u, next_row):
        return jnp.where(row == ch - 1, next_row, pltpu.roll(u, ch - 1, 0))

    def lru_input(lx_ref, lxp_ref, lxn_ref, prev_ok, next_ok):
        x = lx_ref[...]
        p = lxp_ref[...] * prev_ok
        n = lxn_ref[...] * next_ok
        return (cw_ref[0:1] * back2(x, p[6:7], p[7:8]) + cw_ref[1:2] * back1(x, p[7:8])
                + cw_ref[2:3] * x + cw_ref[3:4] * fwd1(x, n[0:1]) + cb_ref[...])

    def coeffs(d, xm):
        xb = xm.astype(BF16)
        r = jax.nn.sigmoid(jnp.dot(xb, wr_ref[d], preferred_element_type=F32) + br_ref[d:d + 1])
        g = jax.nn.sigmoid(jnp.dot(xb, wi_ref[d], preferred_element_type=F32) + bi_ref[d:d + 1])
        log_a = (-LRU_C * sp_ref[d:d + 1]) * r
        a = jnp.exp(log_a)
        a_s[d] = a
        b_s[d] = jnp.sqrt(-jnp.tanh(log_a) * (a * a + 1.0)) * (g * xm)

    u = scg_ref[...] * sx_ref[...]
    u_prev = scgp_ref[7:8] * sxp_ref[7:8] * not_first
    u_next = scgn_ref[0:1] * sxn_ref[0:1] * not_last
    conv = scw_ref[0:1] * back1(u, u_prev) + scw_ref[1:2] * u + scw_ref[2:3] * fwd1(u, u_next)
    zb_ref[...] = (sb_ref[...] * conv).astype(BF16)

    coeffs(0, lru_input(lxf_ref, lxfp_ref, lxfn_ref, not_first, not_last))
    coeffs(1, lru_input(lxb_ref, lxbp_ref, lxbn_ref, not_last, not_first))

    @pl.when(jnp.logical_and(is_first, is_ctx))
    def _():
        hc_s[...] = jnp.zeros_like(hc_s)

    @pl.when(jnp.logical_and(is_first, jnp.logical_not(is_ctx)))
    def _():
        hc_s[0:1] = st_s[pl.ds(2 * b, 1), :]
        hc_s[1:2] = st_s[pl.ds(2 * b + 1, 1), :]

    def body(i, carry):
        hf, hb = carry
        base = i * SUBLANES
        for r in range(SUBLANES):
            t = base + r
            hf = a_s[0, pl.ds(t, 1), :] * hf + b_s[0, pl.ds(t, 1), :]
            hf_ref[pl.ds(t, 1), :] = hf
            tb = ch - 1 - t
            hb = a_s[1, pl.ds(tb, 1), :] * hb + b_s[1, pl.ds(tb, 1), :]
            hb_ref[pl.ds(tb, 1), :] = hb
        return hf, hb

    hf, hb = lax.fori_loop(0, ch // SUBLANES, body, (hc_s[0:1], hc_s[1:2]))
    hc_s[0:1] = hf
    hc_s[1:2] = hb

    @pl.when(is_ctx)
    def _():
        st_s[pl.ds(2 * b, 1), :] = hf
        st_s[pl.ds(2 * b + 1, 1), :] = hb


def _scan_tables(n_batch, seq, n_ctx, ch):
    assert n_ctx == ch
    nc = seq // ch
    ctx0 = n_batch * seq // ch
    fblk, bblk, first, last, seqb = [], [], [], [], []
    for b in range(n_batch):
        fblk.append(ctx0 + b); bblk.append(ctx0 + b); first.append(1); last.append(1); seqb.append(b)
    for b in range(n_batch):
        for c in range(nc):
            fblk.append(b * nc + c); bblk.append(b * nc + nc - 1 - c)
            first.append(int(c == 0)); last.append(int(c == nc - 1)); seqb.append(b)
    return [np.asarray(a, np.int32) for a in (fblk, bblk, first, last, seqb)]


def _conv_scan(sb, rest, sc_w, lru_cw, lru_cb, sp, wr_bd, wi_bd, b_r, b_i, n_batch, seq, n_ctx):
    m = sb.shape[0]
    ch = SCAN_CHUNK
    width = BRANCH_WIDTH
    tables = _scan_tables(n_batch, seq, n_ctx, ch)
    n_items = len(tables[0])
    halo_per_chunk = ch // SUBLANES
    last_halo = m // SUBLANES - 1

    def cur(col, which):
        return pl.BlockSpec((ch, width), lambda i, f, bk, *_: ((f, bk)[which][i], col))

    def prev(col, which):
        return pl.BlockSpec((SUBLANES, width),
                            lambda i, f, bk, *_: (jnp.maximum((f, bk)[which][i] * halo_per_chunk - 1, 0), col))

    def nxt(col, which):
        return pl.BlockSpec((SUBLANES, width),
                            lambda i, f, bk, *_: (jnp.minimum(((f, bk)[which][i] + 1) * halo_per_chunk, last_halo), col))

    def full(shape):
        return pl.BlockSpec(shape, lambda i, *_: (0,) * len(shape))

    in_specs = [cur(0, 0), cur(0, 0), cur(1, 0), cur(2, 0),
                prev(0, 0), prev(1, 0), prev(2, 0), nxt(0, 0), nxt(1, 0), nxt(2, 0),
                cur(2, 1), prev(2, 1), nxt(2, 1),
                full(sc_w.shape), full(lru_cw.shape), full((1, width)), full(sp.shape),
                full(wr_bd.shape), full(wi_bd.shape), full(b_r.shape), full(b_i.shape)]
    out_specs = [cur(0, 0), cur(0, 0), cur(0, 1)]
    grid_spec = pltpu.PrefetchScalarGridSpec(
        num_scalar_prefetch=5, grid=(n_items,), in_specs=in_specs, out_specs=out_specs,
        scratch_shapes=[pltpu.VMEM((2, ch, width), F32), pltpu.VMEM((2, ch, width), F32),
                        pltpu.VMEM((SUBLANES, width), F32), pltpu.VMEM((2 * n_batch, width), F32)])
    return pl.pallas_call(
        functools.partial(_scan_kernel, ch=ch, n_ctx_items=n_batch),
        grid_spec=grid_spec,
        out_shape=[jax.ShapeDtypeStruct((m, width), BF16),
                   jax.ShapeDtypeStruct((m, width), F32),
                   jax.ShapeDtypeStruct((m, width), F32)],
        compiler_params=_params(1),
        name="conv_scan",
    )(*[jnp.asarray(t) for t in tables],
      sb, rest, rest, rest, rest, rest, rest, rest, rest, rest, rest, rest, rest,
      sc_w, lru_cw, lru_cb.reshape(1, width), sp, wr_bd, wi_bd, b_r, b_i)


def _attention_kernel(qblk, bat, r0t, cls,
                      q_ref, k_ref, v_ref, kc_ref, vc_ref, bias_ref, o_ref, *, band):
    it = pl.program_id(0)
    start = pl.multiple_of(r0t[it] * GRID_W, GRID_W)
    nq = q_ref.shape[0]
    lane = lax.broadcasted_iota(jnp.int32, (nq, LANES), 1)
    low = lane < NA_HEAD_DIM
    scale = NA_HEAD_DIM ** -0.5
    nt = (((1,), (1,)), ((), ()))
    for hp in range(NA_HEADS // 2):
        cols = slice(hp * LANES, (hp + 1) * LANES)
        qp = q_ref[:, cols].astype(F32)
        qs = jnp.concatenate([jnp.where(low, qp, 0.0), jnp.where(low, 0.0, qp)], axis=0).astype(BF16)
        kb = k_ref[pl.ds(start, band), cols]
        vb = v_ref[pl.ds(start, band), cols]
        s_loc = lax.dot_general(qs, kb, nt, preferred_element_type=F32)
        s_ctx = lax.dot_general(qs, kc_ref[:, cols], nt, preferred_element_type=F32)
        bias = jnp.concatenate([bias_ref[0, 2 * hp], bias_ref[0, 2 * hp + 1]], axis=0)
        s_loc = s_loc * scale + bias
        s_ctx = s_ctx * scale
        mx = jnp.maximum(jnp.max(s_loc, axis=-1, keepdims=True), jnp.max(s_ctx, axis=-1, keepdims=True))
        e_loc = jnp.exp(s_loc - mx)
        e_ctx = jnp.exp(s_ctx - mx)
        den = jnp.sum(e_loc, axis=-1, keepdims=True) + jnp.sum(e_ctx, axis=-1, keepdims=True)
        o = (jnp.dot(e_loc.astype(BF16), vb, preferred_element_type=F32)
             + jnp.dot(e_ctx.astype(BF16), vc_ref[:, cols], preferred_element_type=F32)) / den
        o_ref[:, cols] = jnp.where(low, o[:nq], o[nq:]).astype(BF16)


def _attention_tables(n_batch, seq, n_ctx, with_ctx_queries):
    rows = seq // GRID_W
    kr = min(NA_WIN_ROWS, rows)
    qblk, bat, r0t, cls = [], [], [], []
    ctx_q0 = n_batch * seq // ATT_QROWS
    for b in range(n_batch):
        for r in range(rows):
            r0 = min(max(r - kr // 2, 0), rows - kr)
            qblk.append(b * rows + r); bat.append(b); r0t.append(r0); cls.append(r - r0)
        if with_ctx_queries:
            for c in range(n_ctx // ATT_QROWS):
                qblk.append(ctx_q0 + b * (n_ctx // ATT_QROWS) + c); bat.append(b); r0t.append(0); cls.append(kr)
    return [np.asarray(a, np.int32) for a in (qblk, bat, r0t, cls)]


def _attention_bias(rpb, seq):
    rows = seq // GRID_W
    kr = min(NA_WIN_ROWS, rows)
    kc = NA_WIN_COLS
    cq = np.arange(GRID_W)
    c0 = np.clip(cq - kc // 2, 0, GRID_W - kc)
    ck = np.arange(GRID_W)
    inside = (ck[None, :] >= c0[:, None]) & (ck[None, :] < c0[:, None] + kc)
    dc = np.clip(ck[None, :] - cq[:, None] + (NA_WIN_COLS - 1), 0, 2 * NA_WIN_COLS - 2)
    n_dr = 2 * NA_WIN_ROWS - 1
    table = jnp.where(jnp.asarray(inside)[None, None], rpb[:, :, dc], NEG_BIG)
    table = table.transpose(0, 2, 1, 3).reshape(NA_HEADS, GRID_W, n_dr * GRID_W)
    classes = []
    for cl in range(kr):
        lo = (NA_WIN_ROWS - 1 - cl) * GRID_W
        classes.append(table[:, :, lo:lo + kr * GRID_W])
    classes.append(jnp.full((NA_HEADS, GRID_W, kr * GRID_W), NEG_BIG, F32))
    return jnp.stack(classes, axis=0)


def _attention(q, k, v, bias, n_batch, seq, n_ctx, with_ctx_queries):
    m = q.shape[0] if with_ctx_queries else n_batch * seq
    rows = seq // GRID_W
    kr = min(NA_WIN_ROWS, rows)
    band = kr * GRID_W
    tables = _attention_tables(n_batch, seq, n_ctx, with_ctx_queries)
    n_items = len(tables[0])
    ctx_blk0 = n_batch * seq // n_ctx
    width = NA_WIDTH
    in_specs = [pl.BlockSpec((ATT_QROWS, width), lambda i, qb, bt, r0, cl: (qb[i], 0)),
                pl.BlockSpec((seq, width), lambda i, qb, bt, r0, cl: (bt[i], 0)),
                pl.BlockSpec((seq, width), lambda i, qb, bt, r0, cl: (bt[i], 0)),
                pl.BlockSpec((n_ctx, width), lambda i, qb, bt, r0, cl: (ctx_blk0 + bt[i], 0)),
                pl.BlockSpec((n_ctx, width), lambda i, qb, bt, r0, cl: (ctx_blk0 + bt[i], 0)),
                pl.BlockSpec((1, NA_HEADS, GRID_W, band), lambda i, qb, bt, r0, cl: (cl[i], 0, 0, 0))]
    out_specs = pl.BlockSpec((ATT_QROWS, width), lambda i, qb, bt, r0, cl: (qb[i], 0))
    grid_spec = pltpu.PrefetchScalarGridSpec(num_scalar_prefetch=4, grid=(n_items,),
                                             in_specs=in_specs, out_specs=out_specs)
    return pl.pallas_call(
        functools.partial(_attention_kernel, band=band),
        grid_spec=grid_spec,
        out_shape=jax.ShapeDtypeStruct((m, width), BF16),
        compiler_params=_params(1),
        name="attention",
    )(*[jnp.asarray(t) for t in tables], q, k, v, k, v, bias)


def _post_kernel(h_ref, att_ref, zb_ref, hf_ref, hb_ref, lg_ref,
                 sh1_ref, sc1_ref, g1_ref, sh2_ref, sc2_ref,
                 wgl_ref, bgl_ref, wpa_ref, wpc_ref, wpl_ref, wo_ref, bo_ref, l1g_ref, l1b_ref,
                 rw_ref, rb_ref, tri_ref, upper_ref,
                 h1_ref, xs_ref, pos_ref, cnt_ref):
    u1 = (_layer_norm(h_ref[...]) * (1.0 + sc1_ref[0]) + sh1_ref[0]).astype(BF16)
    gl = jnp.dot(u1, wgl_ref[...], preferred_element_type=F32) + bgl_ref[...]
    y_a = jnp.dot(att_ref[...], wpa_ref[...], preferred_element_type=F32)
    y_b = jnp.dot(zb_ref[...], wpc_ref[...], preferred_element_type=F32)
    zc = jax.nn.gelu(lg_ref[...]) * (hf_ref[...] + hb_ref[...])
    y_c = jnp.dot(zc.astype(BF16), wpl_ref[...], preferred_element_type=F32)
    merged = (jax.nn.sigmoid(gl[:, :D_MODEL]) * y_a + jax.nn.sigmoid(gl[:, D_MODEL:2 * D_MODEL]) * y_b
              + jax.nn.sigmoid(gl[:, 2 * D_MODEL:]) * y_c)
    y = jnp.dot(merged.astype(BF16), wo_ref[...], preferred_element_type=F32) + bo_ref[...]
    h1 = _layer_norm(DEEPNORM_ALPHA * h_ref[...] + g1_ref[0] * y) * l1g_ref[...] + l1b_ref[...]
    h1_ref[...] = h1
    u2 = _layer_norm(h1) * (1.0 + sc2_ref[0]) + sh2_ref[0]

    u_hi = u2.astype(BF16)
    u_lo = (u2 - u_hi.astype(F32)).astype(BF16)
    by_hi = jnp.dot(u_hi, rw_ref[...], preferred_element_type=F32)
    logits = (by_hi[:, :LANES] + by_hi[:, LANES:]
              + jnp.dot(u_lo, rw_ref[:, :LANES], preferred_element_type=F32) + rb_ref[...])
    tm = logits.shape[0]
    lane = lax.broadcasted_iota(jnp.int32, (tm, LANES), 1)
    lane_f = lane.astype(F32)
    work = logits
    tops, idxs, hots = [], [], []
    for _ in range(TOP_K):
        mx = jnp.max(work, axis=-1, keepdims=True)
        idx = jnp.min(jnp.where(work == mx, lane_f, float(LANES)), axis=-1, keepdims=True)
        hot = lane_f == idx
        work = jnp.where(hot, -3e38, work)
        tops.append(mx); idxs.append(idx); hots.append(hot)
    exps = [jnp.exp(t - tops[0]) for t in tops]
    den = exps[0] + exps[1] + exps[2] + exps[3]
    hot_all = jnp.zeros((tm, LANES), F32)
    for hot in hots:
        hot_all = hot_all + hot.astype(F32)
    cnt = jnp.sum(hot_all, axis=0, keepdims=True)
    cnt_pad = jnp.floor((cnt + (SUBLANES - 1.0)) * (1.0 / SUBLANES)) * SUBLANES
    off = jnp.dot(jnp.broadcast_to(cnt_pad, (SUBLANES, LANES)), upper_ref[...],
                  precision=lax.Precision.HIGHEST, preferred_element_type=F32)[0:1]
    slot = off + jnp.dot(tri_ref[...], hot_all.astype(BF16), preferred_element_type=F32)
    pos4 = jnp.zeros((tm, LANES), F32)
    w_tile = jnp.zeros((tm, LANES), F32)
    for kk in range(TOP_K):
        pos_k = jnp.sum(jnp.where(hots[kk], slot, 0.0), axis=-1, keepdims=True)
        pos4 = jnp.where(lane == kk, pos_k, pos4)
        p = exps[kk] / den
        p_hi = p.astype(BF16).astype(F32)
        p_mid = (p - p_hi).astype(BF16).astype(F32)
        p_lo = p - p_hi - p_mid
        w_tile = jnp.where(hots[kk], p_hi, w_tile)
        w_tile = jnp.where(lane_f == idxs[kk] + float(N_EXPERTS), p_mid, w_tile)
        w_tile = jnp.where(lane_f == idxs[kk] + float(2 * N_EXPERTS), p_lo, w_tile)
    pos_ref[...] = pos4
    sub = lax.broadcasted_iota(jnp.int32, (SUBLANES, LANES), 0)
    cnt_ref[...] = jnp.where(sub == 0, cnt, jnp.where(sub == 1, off, 0.0))

    n_sorted = xs_ref.shape[0]
    pos_t = pos4.T
    r_iota = lax.broadcasted_iota(jnp.int32, (n_sorted, tm), 0).astype(F32)
    hit = r_iota == pos_t[0:1, :]
    for kk in range(1, TOP_K):
        hit = jnp.logical_or(hit, r_iota == pos_t[kk:kk + 1, :])
    perm = jnp.where(hit, 1.0, 0.0).astype(BF16)
    feats = jnp.concatenate([u_hi, w_tile.astype(BF16)], axis=1)
    xs_ref[...] = jnp.dot(perm, feats, preferred_element_type=F32)


def _post_mixer(h, att, zb, hf, hb, rest, mod3, mod_base, w_gl, b_gl, wpa, wpc, wpl, wo, b_o, l1g, l1b,
                rw_pad, rb_pad, n_rows, n_lat, seq):
    tm = TM_POST
    width = BRANCH_WIDTH
    tiles_per_seq = seq // tm
    n_lat_tiles = n_lat // tm
    n_groups_lat = n_lat // seq
    tri = jnp.asarray(np.tril(np.ones((tm, tm), np.float32), -1), BF16)
    upper = jnp.asarray(np.triu(np.ones((LANES, LANES), np.float32), 1))
    n_tiles = n_rows // tm

    def group(i):
        return jnp.where(i < n_lat_tiles, i // tiles_per_seq, n_groups_lat)

    def rows(wd, col=0):
        return pl.BlockSpec((tm, wd), lambda i: (i, col))

    def full(shape):
        return pl.BlockSpec(shape, lambda i: (0,) * len(shape))

    def mod(which):
        return pl.BlockSpec((1, 1, D_MODEL), lambda i: (mod_base + group(i) * 6 + which, 0, 0))

    in_specs = [rows(D_MODEL), rows(width), rows(width), rows(width), rows(width), rows(width, 3),
                mod(0), mod(1), mod(2), mod(3), mod(4),
                full(w_gl.shape), full((1, w_gl.shape[1])),
                full(wpa.shape), full(wpc.shape), full(wpl.shape), full(wo.shape), full((1, D_MODEL)),
                full((1, D_MODEL)), full((1, D_MODEL)), full(rw_pad.shape), full(rb_pad.shape), full(tri.shape),
                full(upper.shape)]
    out_specs = [rows(D_MODEL), pl.BlockSpec((TOK_BLOCK, XS_WIDTH), lambda i: (i, 0)), rows(LANES),
                 pl.BlockSpec((SUBLANES, LANES), lambda i: (i, 0))]
    return pl.pallas_call(
        _post_kernel,
        grid=(n_tiles,),
        in_specs=in_specs, out_specs=out_specs,
        out_shape=[jax.ShapeDtypeStruct((n_rows, D_MODEL), F32),
                   jax.ShapeDtypeStruct((n_tiles * TOK_BLOCK, XS_WIDTH), F32),
                   jax.ShapeDtypeStruct((n_rows, LANES), F32),
                   jax.ShapeDtypeStruct((n_tiles * SUBLANES, LANES), F32)],
        compiler_params=_params(1),
        name="post_mixer",
    )(h, att, zb, hf, hb, rest, mod3, mod3, mod3, mod3, mod3,
      w_gl, b_gl.reshape(1, w_gl.shape[1]),
      wpa, wpc, wpl, wo, b_o.reshape(1, D_MODEL), l1g.reshape(1, D_MODEL), l1b.reshape(1, D_MODEL),
      rw_pad, rb_pad, tri, upper)


def _expert_plan(cnt_out, n_tok_tiles, n_rows):
    tm = TM_EXPERT
    co = cnt_out.reshape(n_tok_tiles, SUBLANES, LANES)
    cnt = (co[:, 0, :N_EXPERTS].astype(jnp.int32) + SUBLANES - 1) // SUBLANES * SUBLANES
    off = co[:, 1, :N_EXPERTS].astype(jnp.int32)
    cum_end = jnp.cumsum(cnt, axis=0)
    cum = cum_end - cnt
    total = cum_end[-1]
    n_et = (total + tm - 1) // tm
    et_end = jnp.cumsum(n_et)
    n_act = et_end[-1:].astype(jnp.int32)
    n_tiles = -(-n_tok_tiles * TOK_BLOCK // tm) + N_EXPERTS
    j = jnp.arange(n_tiles, dtype=jnp.int32)
    tile_e = jnp.minimum(jnp.sum((et_end[None, :] <= j[:, None]).astype(jnp.int32), axis=1), N_EXPERTS - 1)
    row0 = (j - (et_end - n_et)[tile_e]) * tm
    n_rows_tile = jnp.clip(total[tile_e] - row0, 0, tm)
    cum_e = cum.T[tile_e]
    cum_end_e = cum_end.T[tile_e]
    off_e = off.T[tile_e]
    q = row0[:, None] + SUBLANES * jnp.arange(tm // SUBLANES, dtype=jnp.int32)[None, :]
    tok_tile = jnp.minimum(jnp.sum((cum_end_e[:, None, :] <= q[:, :, None]).astype(jnp.int32), axis=-1),
                           n_tok_tiles - 1)
    src = (tok_tile * TOK_BLOCK + jnp.take_along_axis(off_e, tok_tile, axis=1)
           + q - jnp.take_along_axis(cum_e, tok_tile, axis=1))
    used = off[:, N_EXPERTS - 1] + cnt[:, N_EXPERTS - 1]
    return (tile_e.astype(jnp.int32), n_rows_tile.astype(jnp.int32), n_act, src.reshape(-1).astype(jnp.int32),
            used.astype(jnp.int32), n_tiles)


def _expert_kernel(tile_e, n_rows_t, n_act, src_t, used_t,
                   xs_hbm, wgu_ref, bgu_ref, wdn_ref, bdn_ref, ys_hbm,
                   xin, yout, wgu_s, wdn_s, zeros, sem_in, sem_out, sem_zero, *, tm, tok_block, n_tok_tiles):
    j = pl.program_id(0)
    na = n_act[0]
    chunks = tm // SUBLANES

    def gather(src, dst, size, slot):
        return pltpu.make_async_copy(xs_hbm.at[pl.ds(src, size)], xin.at[slot, pl.ds(dst, size)], sem_in.at[slot])

    def scatter(src, dst, size, slot):
        return pltpu.make_async_copy(yout.at[slot, pl.ds(dst, size)], ys_hbm.at[pl.ds(src, size)], sem_out.at[slot])

    def start_chunks(jj, slot, copy):
        def one(c):
            src = pl.multiple_of(src_t[jj * chunks + c], SUBLANES)
            copy(src, pl.multiple_of(c * SUBLANES, SUBLANES), SUBLANES, slot).start()

        def body(c, carry):
            one(c)
            return carry

        def body_unrolled(g, carry):
            for u in range(SUBLANES):
                one(g * SUBLANES + u)
            return carry

        n = lax.shift_right_logical(n_rows_t[jj], 3)

        @pl.when(n == chunks)
        def _():
            lax.fori_loop(0, chunks // SUBLANES, body_unrolled, 0)

        @pl.when(n != chunks)
        def _():
            lax.fori_loop(0, n, body, 0)

    def wait_rows(jj, slot, copy):
        n = n_rows_t[jj]
        size = tm
        while size >= SUBLANES:
            @pl.when((n & size) != 0)
            def _(size=size):
                copy(0, 0, size, slot).wait()
            size //= 2

    slot = j % 2

    @pl.when(j == 0)
    def _():
        xin[...] = jnp.zeros_like(xin)
        start_chunks(0, 0, gather)
        zeros[...] = jnp.zeros_like(zeros)

        def clear_tail(i, copy_op):
            used = used_t[i]
            tail = tok_block - used
            for size in (256, 128, 64, 32, 16, 8):
                @pl.when((tail & size) != 0)
                def _():
                    at = pl.multiple_of(i * tok_block + used + (tail & ~(2 * size - 1)), SUBLANES)
                    copy_op(pltpu.make_async_copy(zeros.at[pl.ds(0, size)], ys_hbm.at[pl.ds(at, size)], sem_zero))

        def start_clear(i, c):
            clear_tail(i, lambda cp: cp.start())
            return c

        def wait_clear(i, c):
            clear_tail(i, lambda cp: cp.wait())
            return c

        lax.fori_loop(0, n_tok_tiles, start_clear, 0)
        lax.fori_loop(0, n_tok_tiles, wait_clear, 0)

    @pl.when(j + 1 < na)
    def _():
        start_chunks(j + 1, 1 - slot, gather)

    @pl.when(j < na)
    def _():
        e = tile_e[j]
        e_prev = tile_e[jnp.maximum(j - 1, 0)]

        @pl.when(jnp.logical_or(j == 0, e != e_prev))
        def _():
            wgu_s[...] = wgu_ref[0, 0].astype(BF16)
            wdn_s[...] = wdn_ref[0, 0].astype(BF16)

        wait_rows(j, slot, gather)
        x = xin[slot]
        lane = lax.broadcasted_iota(jnp.int32, (tm, LANES), 1)
        p = jnp.sum(jnp.where(lane % N_EXPERTS == e, x[:, D_MODEL:], 0.0), axis=-1, keepdims=True)
        gu = jnp.dot(x[:, :D_MODEL].astype(BF16), wgu_s[...], preferred_element_type=F32) + bgu_ref[0, 0]
        f = gu.shape[1] // 2
        gate = jnp.minimum(gu[:, :f], SWIGLU_LIMIT)
        up = jnp.clip(gu[:, f:], -SWIGLU_LIMIT, SWIGLU_LIMIT)
        hid = (up + 1.0) * gate * jax.nn.sigmoid(SWIGLU_ALPHA * gate)
        y = jnp.dot(hid.astype(BF16), wdn_s[...], preferred_element_type=F32) + bdn_ref[0, 0]
        yout[slot] = y * p
        start_chunks(j, slot, scatter)

        @pl.when(j >= 1)
        def _():
            wait_rows(j - 1, 1 - slot, scatter)

        @pl.when(j == na - 1)
        def _():
            wait_rows(j, slot, scatter)


def _experts(xs, plan, layer, w_gu, b_gu, w_dn, b_dn):
    tile_e, n_rows_tile, n_act, src, used, n_tiles = plan
    tm = TM_EXPERT
    f2 = w_gu.shape[-1]

    def expert(j, te, nr, na, *_):
        return (layer, te[jnp.minimum(j, na[0] - 1)], 0, 0)

    in_specs = [pl.BlockSpec(memory_space=pl.ANY),
                pl.BlockSpec((1, 1, D_MODEL, f2), expert),
                pl.BlockSpec((1, 1, 1, f2), expert),
                pl.BlockSpec((1, 1, f2 // 2, D_MODEL), expert),
                pl.BlockSpec((1, 1, 1, D_MODEL), expert)]
    grid_spec = pltpu.PrefetchScalarGridSpec(
        num_scalar_prefetch=5, grid=(n_tiles,), in_specs=in_specs,
        out_specs=pl.BlockSpec(memory_space=pl.ANY),
        scratch_shapes=[pltpu.VMEM((2, tm, XS_WIDTH), F32), pltpu.VMEM((2, tm, D_MODEL), F32),
                        pltpu.VMEM((D_MODEL, f2), BF16), pltpu.VMEM((f2 // 2, D_MODEL), BF16),
                        pltpu.VMEM((ZERO_ROWS, D_MODEL), F32),
                        pltpu.SemaphoreType.DMA((2,)), pltpu.SemaphoreType.DMA((2,)), pltpu.SemaphoreType.DMA(())])
    return pl.pallas_call(
        functools.partial(_expert_kernel, tm=tm, tok_block=TOK_BLOCK, n_tok_tiles=xs.shape[0] // TOK_BLOCK),
        grid_spec=grid_spec,
        out_shape=jax.ShapeDtypeStruct((xs.shape[0], D_MODEL), F32),
        compiler_params=_params(1),
        name="moe_experts",
    )(tile_e, n_rows_tile, n_act, src, used, xs, w_gu, b_gu.reshape(DEPTH, N_EXPERTS, 1, f2), w_dn,
      b_dn.reshape(DEPTH, N_EXPERTS, 1, D_MODEL))


def _combine_kernel(ys_ref, pos_ref, h1_ref, g2_ref, l2g_ref, l2b_ref, o_ref):
    tm = pos_ref.shape[0]
    n_sorted = ys_ref.shape[0]
    pos = pos_ref[...]
    col = lax.broadcasted_iota(jnp.int32, (tm, n_sorted), 1).astype(F32)
    sel = jnp.zeros((tm, n_sorted), F32)
    for kk in range(TOP_K):
        sel = sel + (col == pos[:, kk:kk + 1]).astype(F32)
    sel = sel.astype(BF16)
    ys = ys_ref[...]
    hi = ys.astype(BF16)
    rest = ys - hi.astype(F32)
    mid = rest.astype(BF16)
    lo = (rest - mid.astype(F32)).astype(BF16)
    y2 = (jnp.dot(sel, hi, preferred_element_type=F32) + jnp.dot(sel, mid, preferred_element_type=F32)
          + jnp.dot(sel, lo, preferred_element_type=F32))
    o_ref[...] = _layer_norm(DEEPNORM_ALPHA * h1_ref[...] + g2_ref[0] * y2) * l2g_ref[...] + l2b_ref[...]


def _combine(ys, pos4, h1, mod3, mod_base, l2g, l2b, n_lat, seq):
    n_rows = h1.shape[0]
    tm = TM_POST
    tiles_per_seq = seq // tm
    n_lat_tiles = n_lat // tm
    n_groups_lat = n_lat // seq

    def group(i):
        return jnp.where(i < n_lat_tiles, i // tiles_per_seq, n_groups_lat)

    in_specs = [pl.BlockSpec((TOK_BLOCK, D_MODEL), lambda i: (i, 0)),
                pl.BlockSpec((tm, LANES), lambda i: (i, 0)),
                pl.BlockSpec((tm, D_MODEL), lambda i: (i, 0)),
                pl.BlockSpec((1, 1, D_MODEL), lambda i: (mod_base + group(i) * 6 + 5, 0, 0)),
                pl.BlockSpec((1, D_MODEL), lambda i: (0, 0)),
                pl.BlockSpec((1, D_MODEL), lambda i: (0, 0))]
    return pl.pallas_call(
        _combine_kernel,
        grid=(n_rows // tm,),
        in_specs=in_specs,
        out_specs=pl.BlockSpec((tm, D_MODEL), lambda i: (i, 0)),
        out_shape=jax.ShapeDtypeStruct((n_rows, D_MODEL), F32),
        compiler_params=_params(1),
        name="moe_combine",
    )(ys, pos4, h1, mod3, l2g.reshape(1, D_MODEL), l2b.reshape(1, D_MODEL))


def _block_diag(w):
    two, n, d, e = w.shape
    eye = jnp.eye(n, dtype=w.dtype)
    return (w[:, :, :, None, :] * eye[None, :, None, :, None]).reshape(two, n * d, n * e)


def kernel(x, c, ctx, c_ctx, w_mod, b_mod, w_in, b_in, na_rpb, w_proj_attn, w_proj_conv, w_proj_lru, sc_conv_w, lru_conv_w, lru_conv_b, lru_lambda, lru_w_r, lru_b_r, lru_w_i, lru_b_i, w_o, b_o, ln1_g, ln1_b, router_w, router_b, exp_w_gu, exp_b_gu, exp_w_dn, exp_b_dn, ln2_g, ln2_b):
    n_batch, seq, d = x.shape
    n_ctx = ctx.shape[1]
    n_lat = n_batch * seq
    n_all = n_lat + n_batch * n_ctx
    assert d == D_MODEL and n_batch + 1 <= SUBLANES

    cc = jnp.concatenate([c, c_ctx[None], jnp.zeros((SUBLANES - n_batch - 1, d), F32)], axis=0)
    mod = _modulation(cc, w_mod, b_mod)
    groups = n_batch + 1
    mod3 = mod.reshape(DEPTH, SUBLANES, 6, d)[:, :groups].reshape(DEPTH * groups * 6, 1, d)

    cos_t, sin_t = _make_rope(seq, TM_INPROJ)
    h = jnp.concatenate([x.reshape(n_lat, d), ctx.reshape(n_batch * n_ctx, d)], axis=0)

    for layer in range(DEPTH):
        last = layer == DEPTH - 1
        mod_base = layer * groups * 6
        w_in_bf = w_in[layer].astype(BF16)
        q, k, v, sb, rest = _input_projection(h, mod3, mod_base, w_in_bf[:, :N_EARLY], b_in[layer, :N_EARLY],
                                              cos_t, sin_t, n_lat, seq)
        sp = jax.nn.softplus(-lru_lambda[layer])
        zb, hf, hb = _conv_scan(sb, rest, sc_conv_w[layer], lru_conv_w[layer], lru_conv_b[layer], sp,
                                _block_diag(lru_w_r[layer]).astype(BF16), _block_diag(lru_w_i[layer]).astype(BF16),
                                lru_b_r[layer], lru_b_i[layer], n_batch, seq, n_ctx)
        att = _attention(q, k, v, _attention_bias(na_rpb[layer], seq), n_batch, seq, n_ctx, not last)
        n_rows = n_lat if last else n_all
        rw_full = jnp.pad(router_w[layer], ((0, 0), (0, LANES - N_EXPERTS)))
        rw_hi = rw_full.astype(BF16)
        rw_pad = jnp.concatenate([rw_hi, (rw_full - rw_hi.astype(F32)).astype(BF16)], axis=1)
        rb_pad = jnp.concatenate([router_b[layer], jnp.full((LANES - N_EXPERTS,), NEG_BIG, F32)]).reshape(1, LANES)
        h1, xs, pos4, cnt_out = _post_mixer(
            h, att, zb, hf, hb, rest, mod3, mod_base, w_in_bf[:, N_EARLY:], b_in[layer, N_EARLY:],
            w_proj_attn[layer].astype(BF16), w_proj_conv[layer].astype(BF16), w_proj_lru[layer].astype(BF16),
            w_o[layer].astype(BF16), b_o[layer], ln1_g[layer], ln1_b[layer], rw_pad, rb_pad, n_rows, n_lat, seq)
        plan = _expert_plan(cnt_out, n_rows // TM_POST, n_rows)
        ys = _experts(xs, plan, layer, exp_w_gu, exp_b_gu, exp_w_dn, exp_b_dn)
        h = _combine(ys, pos4, h1, mod3, mod_base, ln2_g[layer], ln2_b[layer], n_lat, seq)
    return h.reshape(n_batch, seq, d)
```

```python
import functools

import numpy as np
import jax
import jax.numpy as jnp
from jax import lax
from jax.experimental import pallas as pl
from jax.experimental.pallas import tpu as pltpu

D_MODEL = 1024
DEPTH = 2
GRID_W = 64
NA_HEADS = 8
NA_HEAD_DIM = 64
NA_WIDTH = NA_HEADS * NA_HEAD_DIM
NA_WIN_ROWS = 8
NA_WIN_COLS = 16
ROPE_BASE = 10000.0
BRANCH_WIDTH = 512
LRU_BLOCKS = 8
LRU_C = 8.0
P_TOTAL = 7168
N_EARLY = 8 * BRANCH_WIDTH
N_EXPERTS = 32
TOP_K = 4
SWIGLU_LIMIT = 7.0
SWIGLU_ALPHA = 1.702
LN_EPS = 1e-5
DEEPNORM_ALPHA = (2 * DEPTH) ** 0.25
NEG_BIG = -1e30

LANES = 128
SUBLANES = 8
VMEM_LIMIT_BYTES = 56 * 1024 * 1024

TM_INPROJ = 1024
TN_INPROJ = 1024
SCAN_CHUNK = 256
ATT_ROWS = 4
ATT_QROWS = ATT_ROWS * GRID_W
TM_POST = 256
TM_EXPERT = 512
XS_WIDTH = D_MODEL + LANES
TOK_BLOCK = TM_POST * TOP_K + N_EXPERTS * SUBLANES
ZERO_ROWS = 256

F32 = jnp.float32
BF16 = jnp.bfloat16


def _params(n_axes):
    return pltpu.CompilerParams(dimension_semantics=("arbitrary",) * n_axes,
                                vmem_limit_bytes=VMEM_LIMIT_BYTES)


def _layer_norm(x):
    mu = jnp.mean(x, axis=-1, keepdims=True)
    xc = x - mu
    var = jnp.mean(xc * xc, axis=-1, keepdims=True)
    return xc * lax.rsqrt(var + LN_EPS)


def _mod_kernel(c_ref, w_ref, b_ref, o_ref):
    c = c_ref[...]
    s = (c * jax.nn.sigmoid(c)).astype(BF16)
    o_ref[0] = jnp.dot(s, w_ref[0].astype(BF16), preferred_element_type=F32) + b_ref[0]


def _modulation(cc, w_mod, b_mod):
    n_out = w_mod.shape[-1]
    return pl.pallas_call(
        _mod_kernel,
        grid=(DEPTH, n_out // D_MODEL),
        in_specs=[pl.BlockSpec((SUBLANES, D_MODEL), lambda l, j: (0, 0)),
                  pl.BlockSpec((1, D_MODEL, D_MODEL), lambda l, j: (l, 0, j)),
                  pl.BlockSpec((1, 1, D_MODEL), lambda l, j: (l, 0, j))],
        out_specs=pl.BlockSpec((1, SUBLANES, D_MODEL), lambda l, j: (l, 0, j)),
        out_shape=jax.ShapeDtypeStruct((DEPTH, SUBLANES, n_out), F32),
        compiler_params=_params(2),
        name="modulation",
    )(cc, w_mod, b_mod.reshape(DEPTH, 1, n_out))


def _rope_half(x, cos, sin_signed):
    lane = lax.broadcasted_iota(jnp.int32, (x.shape[0], LANES), 1)
    first = (lane % 32) < 16
    outs = []
    for cidx in range(x.shape[1] // LANES):
        xc = x[:, cidx * LANES:(cidx + 1) * LANES]
        partner = jnp.where(first, pltpu.roll(xc, LANES - 16, 1), pltpu.roll(xc, 16, 1))
        outs.append(xc * cos + partner * sin_signed)
    return jnp.concatenate(outs, axis=1)


def _inproj_kernel(h_ref, sh_ref, sc_ref, w_ref, b_ref, cos_ref, sin_ref,
                   q_ref, k_ref, v_ref, sb_ref, rest_ref, xn_ref):
    j = pl.program_id(1)

    @pl.when(j == 0)
    def _():
        y = _layer_norm(h_ref[...])
        xn_ref[...] = (y * (1.0 + sc_ref[0]) + sh_ref[0]).astype(BF16)

    acc = jnp.dot(xn_ref[...], w_ref[...], preferred_element_type=F32) + b_ref[...]
    half = BRANCH_WIDTH

    @pl.when(j == 0)
    def _():
        cos = cos_ref[...]
        sin = sin_ref[...]
        q_ref[...] = _rope_half(acc[:, :half], cos, sin).astype(BF16)
        k_ref[...] = _rope_half(acc[:, half:], cos, sin).astype(BF16)

    @pl.when(j == 1)
    def _():
        v_ref[...] = acc[:, :half].astype(BF16)
        sb_ref[...] = acc[:, half:]

    @pl.when(j >= 2)
    def _():
        rest_ref[...] = acc


def _input_projection(h, mod3, mod_base, w_in_bf, b_in, cos_t, sin_t, n_lat, seq):
    m = h.shape[0]
    tm, tn = TM_INPROJ, TN_INPROJ
    n_lat_tiles = n_lat // tm
    tiles_per_seq = seq // tm
    n_groups_lat = n_lat // seq

    def group(i):
        return jnp.where(i < n_lat_tiles, i // tiles_per_seq, n_groups_lat)

    def rope_blk(i):
        return jnp.where(i < n_lat_tiles, i % tiles_per_seq, tiles_per_seq)

    half = BRANCH_WIDTH
    n_cols = w_in_bf.shape[1]
    return pl.pallas_call(
        _inproj_kernel,
        grid=(m // tm, n_cols // tn),
        in_specs=[pl.BlockSpec((tm, D_MODEL), lambda i, j: (i, 0)),
                  pl.BlockSpec((1, 1, D_MODEL), lambda i, j: (mod_base + group(i) * 6 + 0, 0, 0)),
                  pl.BlockSpec((1, 1, D_MODEL), lambda i, j: (mod_base + group(i) * 6 + 1, 0, 0)),
                  pl.BlockSpec((D_MODEL, tn), lambda i, j: (0, j)),
                  pl.BlockSpec((1, tn), lambda i, j: (0, j)),
                  pl.BlockSpec((tm, LANES), lambda i, j: (rope_blk(i), 0)),
                  pl.BlockSpec((tm, LANES), lambda i, j: (rope_blk(i), 0))],
        out_specs=[pl.BlockSpec((tm, half), lambda i, j: (i, 0)),
                   pl.BlockSpec((tm, half), lambda i, j: (i, 0)),
                   pl.BlockSpec((tm, half), lambda i, j: (i, 0)),
                   pl.BlockSpec((tm, half), lambda i, j: (i, 0)),
                   pl.BlockSpec((tm, tn), lambda i, j: (i, jnp.maximum(j - 2, 0)))],
        out_shape=[jax.ShapeDtypeStruct((m, half), BF16),
                   jax.ShapeDtypeStruct((m, half), BF16),
                   jax.ShapeDtypeStruct((m, half), BF16),
                   jax.ShapeDtypeStruct((m, half), F32),
                   jax.ShapeDtypeStruct((m, n_cols - 2 * tn), F32)],
        scratch_shapes=[pltpu.VMEM((tm, D_MODEL), BF16)],
        compiler_params=_params(2),
        name="input_projection",
    )(h, mod3, mod3, w_in_bf, b_in.reshape(1, n_cols), cos_t, sin_t)


def _make_rope(seq, tm):
    t = np.arange(seq)
    row_pos, col_pos = t // GRID_W, t % GRID_W
    d = np.arange(LANES) % NA_HEAD_DIM
    m = NA_HEAD_DIM // 4
    inv_freq = (ROPE_BASE ** (-jnp.arange(m, dtype=F32) / m))[d % m]
    pos = np.where((d < 2 * m)[None, :], row_pos[:, None], col_pos[:, None])
    ang = jnp.asarray(pos).astype(F32) * inv_freq[None, :]
    cos = jnp.cos(ang)
    sin = jnp.sin(ang)
    sin_signed = jnp.where(jnp.asarray((d % (2 * m)) < m)[None, :], -sin, sin)
    cos = jnp.concatenate([cos, jnp.ones((tm, LANES), F32)], axis=0)
    sin_signed = jnp.concatenate([sin_signed, jnp.zeros((tm, LANES), F32)], axis=0)
    return cos, sin_signed


def _scan_kernel(fblk, bblk, first, last, seqb,
                 sb_ref, scg_ref, sx_ref, lxf_ref,
                 scgp_ref, sxp_ref, lxfp_ref, scgn_ref, sxn_ref, lxfn_ref,
                 lxb_ref, lxbp_ref, lxbn_ref,
                 scw_ref, cw_ref, cb_ref, sp_ref, wr_ref, wi_ref, br_ref, bi_ref,
                 zb_ref, hf_ref, hb_ref,
                 a_s, b_s, hc_s, st_s, *, ch, n_ctx_items):
    it = pl.program_id(0)
    is_first = first[it] == 1
    is_last = last[it] == 1
    is_ctx = it < n_ctx_items
    b = seqb[it]
    width = BRANCH_WIDTH
    row = lax.broadcasted_iota(jnp.int32, (ch, width), 0)
    not_first = jnp.where(is_first, 0.0, 1.0).astype(F32)
    not_last = jnp.where(is_last, 0.0, 1.0).astype(F32)

    def back1(u, prev_row):
        return jnp.where(row == 0, prev_row, pltpu.roll(u, 1, 0))

    def back2(u, prev2, prev1):
        return jnp.where(row == 0, prev2, jnp.where(row == 1, prev1, pltpu.roll(u, 2, 0)))

    def fwd1(u, next_row):
        return jnp.where(row == ch - 1, next_row, pltpu.roll(u, ch - 1, 0))

    def lru_input(lx_ref, lxp_ref, lxn_ref, prev_ok, next_ok):
        x = lx_ref[...]
        p = lxp_ref[...] * prev_ok
        n = lxn_ref[...] * next_ok
        return (cw_ref[0:1] * back2(x, p[6:7], p[7:8]) + cw_ref[1:2] * back1(x, p[7:8])
                + cw_ref[2:3] * x + cw_ref[3:4] * fwd1(x, n[0:1]) + cb_ref[...])

    def coeffs(d, xm):
        xb = xm.astype(BF16)
        r = jax.nn.sigmoid(jnp.dot(xb, wr_ref[d], preferred_element_type=F32) + br_ref[d:d + 1])
        g = jax.nn.sigmoid(jnp.dot(xb, wi_ref[d], preferred_element_type=F32) + bi_ref[d:d + 1])
        log_a = (-LRU_C * sp_ref[d:d + 1]) * r
        a = jnp.exp(log_a)
        a_s[d] = a
        b_s[d] = jnp.sqrt(-jnp.tanh(log_a) * (a * a + 1.0)) * (g * xm)

    u = scg_ref[...] * sx_ref[...]
    u_prev = scgp_ref[7:8] * sxp_ref[7:8] * not_first
    u_next = scgn_ref[0:1] * sxn_ref[0:1] * not_last
    conv = scw_ref[0:1] * back1(u, u_prev) + scw_ref[1:2] * u + scw_ref[2:3] * fwd1(u, u_next)
    zb_ref[...] = (sb_ref[...] * conv).astype(BF16)

    coeffs(0, lru_input(lxf_ref, lxfp_ref, lxfn_ref, not_first, not_last))
    coeffs(1, lru_input(lxb_ref, lxbp_ref, lxbn_ref, not_last, not_first))

    @pl.when(jnp.logical_and(is_first, is_ctx))
    def _():
        hc_s[...] = jnp.zeros_like(hc_s)

    @pl.when(jnp.logical_and(is_first, jnp.logical_not(is_ctx)))
    def _():
        hc_s[0:1] = st_s[pl.ds(2 * b, 1), :]
        hc_s[1:2] = st_s[pl.ds(2 * b + 1, 1), :]

    def body(i, carry):
        hf, hb = carry
        base = i * SUBLANES
        for r in range(SUBLANES):
            t = base + r
            hf = a_s[0, pl.ds(t, 1), :] * hf + b_s[0, pl.ds(t, 1), :]
            hf_ref[pl.ds(t, 1), :] = hf
            tb = ch - 1 - t
            hb = a_s[1, pl.ds(tb, 1), :] * hb + b_s[1, pl.ds(tb, 1), :]
            hb_ref[pl.ds(tb, 1), :] = hb
        return hf, hb

    hf, hb = lax.fori_loop(0, ch // SUBLANES, body, (hc_s[0:1], hc_s[1:2]))
    hc_s[0:1] = hf
    hc_s[1:2] = hb

    @pl.when(is_ctx)
    def _():
        st_s[pl.ds(2 * b, 1), :] = hf
        st_s[pl.ds(2 * b + 1, 1), :] = hb


def _scan_tables(n_batch, seq, n_ctx, ch):
    assert n_ctx == ch
    nc = seq // ch
    ctx0 = n_batch * seq // ch
    fblk, bblk, first, last, seqb = [], [], [], [], []
    for b in range(n_batch):
        fblk.append(ctx0 + b); bblk.append(ctx0 + b); first.append(1); last.append(1); seqb.append(b)
    for b in range(n_batch):
        for c in range(nc):
            fblk.append(b * nc + c); bblk.append(b * nc + nc - 1 - c)
            first.append(int(c == 0)); last.append(int(c == nc - 1)); seqb.append(b)
    return [np.asarray(a, np.int32) for a in (fblk, bblk, first, last, seqb)]


def _conv_scan(sb, rest, sc_w, lru_cw, lru_cb, sp, wr_bd, wi_bd, b_r, b_i, n_batch, seq, n_ctx):
    m = sb.shape[0]
    ch = SCAN_CHUNK
    width = BRANCH_WIDTH
    tables = _scan_tables(n_batch, seq, n_ctx, ch)
    n_items = len(tables[0])
    halo_per_chunk = ch // SUBLANES
    last_halo = m // SUBLANES - 1

    def cur(col, which):
        return pl.BlockSpec((ch, width), lambda i, f, bk, *_: ((f, bk)[which][i], col))

    def prev(col, which):
        return pl.BlockSpec((SUBLANES, width),
                            lambda i, f, bk, *_: (jnp.maximum((f, bk)[which][i] * halo_per_chunk - 1, 0), col))

    def nxt(col, which):
        return pl.BlockSpec((SUBLANES, width),
                            lambda i, f, bk, *_: (jnp.minimum(((f, bk)[which][i] + 1) * halo_per_chunk, last_halo), col))

    def full(shape):
        return pl.BlockSpec(shape, lambda i, *_: (0,) * len(shape))

    in_specs = [cur(0, 0), cur(0, 0), cur(1, 0), cur(2, 0),
                prev(0, 0), prev(1, 0), prev(2, 0), nxt(0, 0), nxt(1, 0), nxt(2, 0),
                cur(2, 1), prev(2, 1), nxt(2, 1),
                full(sc_w.shape), full(lru_cw.shape), full((1, width)), full(sp.shape),
                full(wr_bd.shape), full(wi_bd.shape), full(b_r.shape), full(b_i.shape)]
    out_specs = [cur(0, 0), cur(0, 0), cur(0, 1)]
    grid_spec = pltpu.PrefetchScalarGridSpec(
        num_scalar_prefetch=5, grid=(n_items,), in_specs=in_specs, out_specs=out_specs,
        scratch_shapes=[pltpu.VMEM((2, ch, width), F32), pltpu.VMEM((2, ch, width), F32),
                        pltpu.VMEM((SUBLANES, width), F32), pltpu.VMEM((2 * n_batch, width), F32)])
    return pl.pallas_call(
        functools.partial(_scan_kernel, ch=ch, n_ctx_items=n_batch),
        grid_spec=grid_spec,
        out_shape=[jax.ShapeDtypeStruct((m, width), BF16),
                   jax.ShapeDtypeStruct((m, width), F32),
                   jax.ShapeDtypeStruct((m, width), F32)],
        compiler_params=_params(1),
        name="conv_scan",
    )(*[jnp.asarray(t) for t in tables],
      sb, rest, rest, rest, rest, rest, rest, rest, rest, rest, rest, rest, rest,
      sc_w, lru_cw, lru_cb.reshape(1, width), sp, wr_bd, wi_bd, b_r, b_i)


def _attention_kernel(qblk, bat, r0t, cls, q_ref, k_ref, v_ref, kc_ref, vc_ref, *rest, band):
    bias_refs, o_ref = rest[:ATT_ROWS], rest[ATT_ROWS]
    it = pl.program_id(0)
    nq = GRID_W
    lane = lax.broadcasted_iota(jnp.int32, (nq, LANES), 1)
    low = lane < NA_HEAD_DIM
    scale = NA_HEAD_DIM ** -0.5
    nt = (((1,), (1,)), ((), ()))
    for row in range(ATT_ROWS):
        start = pl.multiple_of(r0t[it * ATT_ROWS + row] * GRID_W, GRID_W)
        bias_ref = bias_refs[row]
        qrows = slice(row * nq, (row + 1) * nq)
        for hp in range(NA_HEADS // 2):
            cols = slice(hp * LANES, (hp + 1) * LANES)
            qp = q_ref[qrows, cols].astype(F32)
            qs = jnp.concatenate([jnp.where(low, qp, 0.0), jnp.where(low, 0.0, qp)], axis=0).astype(BF16)
            kb = k_ref[pl.ds(start, band), cols]
            vb = v_ref[pl.ds(start, band), cols]
            s_loc = lax.dot_general(qs, kb, nt, preferred_element_type=F32)
            s_ctx = lax.dot_general(qs, kc_ref[:, cols], nt, preferred_element_type=F32)
            bias = jnp.concatenate([bias_ref[0, 2 * hp], bias_ref[0, 2 * hp + 1]], axis=0)
            s_loc = s_loc * scale + bias
            s_ctx = s_ctx * scale
            mx = jnp.maximum(jnp.max(s_loc, axis=-1, keepdims=True), jnp.max(s_ctx, axis=-1, keepdims=True))
            e_loc = jnp.exp(s_loc - mx)
            e_ctx = jnp.exp(s_ctx - mx)
            den = jnp.sum(e_loc, axis=-1, keepdims=True) + jnp.sum(e_ctx, axis=-1, keepdims=True)
            o = (jnp.dot(e_loc.astype(BF16), vb, preferred_element_type=F32)
                 + jnp.dot(e_ctx.astype(BF16), vc_ref[:, cols], preferred_element_type=F32)) / den
            o_ref[qrows, cols] = jnp.where(low, o[:nq], o[nq:]).astype(BF16)


def _attention_tables(n_batch, seq, n_ctx, with_ctx_queries):
    rows = seq // GRID_W
    kr = min(NA_WIN_ROWS, rows)
    assert rows % ATT_ROWS == 0 and n_ctx % ATT_QROWS == 0
    qblk, bat, r0t, cls = [], [], [], []
    ctx_q0 = n_batch * seq // ATT_QROWS
    for b in range(n_batch):
        for rg in range(rows // ATT_ROWS):
            qblk.append(b * (rows // ATT_ROWS) + rg); bat.append(b)
            for r in range(rg * ATT_ROWS, (rg + 1) * ATT_ROWS):
                r0 = min(max(r - kr // 2, 0), rows - kr)
                r0t.append(r0); cls.append(r - r0)
        if with_ctx_queries:
            for c in range(n_ctx // ATT_QROWS):
                qblk.append(ctx_q0 + b * (n_ctx // ATT_QROWS) + c); bat.append(b)
                r0t.extend([0] * ATT_ROWS); cls.extend([kr] * ATT_ROWS)
    return [np.asarray(a, np.int32) for a in (qblk, bat, r0t, cls)]


def _attention_bias(rpb, seq):
    rows = seq // GRID_W
    kr = min(NA_WIN_ROWS, rows)
    kc = NA_WIN_COLS
    cq = np.arange(GRID_W)
    c0 = np.clip(cq - kc // 2, 0, GRID_W - kc)
    ck = np.arange(GRID_W)
    inside = (ck[None, :] >= c0[:, None]) & (ck[None, :] < c0[:, None] + kc)
    dc = np.clip(ck[None, :] - cq[:, None] + (NA_WIN_COLS - 1), 0, 2 * NA_WIN_COLS - 2)
    n_dr = 2 * NA_WIN_ROWS - 1
    n_dc = 2 * NA_WIN_COLS - 1
    pick = jnp.asarray((np.arange(n_dc)[:, None] == dc.reshape(1, -1)).astype(np.float32))
    picked = jnp.dot(rpb.reshape(-1, n_dc), pick, precision=lax.Precision.HIGHEST)
    picked = picked.reshape(NA_HEADS, 2 * NA_WIN_ROWS - 1, GRID_W, GRID_W)
    table = jnp.where(jnp.asarray(inside)[None, None], picked, NEG_BIG)
    table = table.transpose(0, 2, 1, 3).reshape(NA_HEADS, GRID_W, n_dr * GRID_W)
    classes = []
    for cl in range(kr):
        lo = (NA_WIN_ROWS - 1 - cl) * GRID_W
        classes.append(table[:, :, lo:lo + kr * GRID_W])
    classes.append(jnp.full((NA_HEADS, GRID_W, kr * GRID_W), NEG_BIG, F32))
    return jnp.stack(classes, axis=0)


def _attention(q, k, v, bias, n_batch, seq, n_ctx, with_ctx_queries):
    m = q.shape[0] if with_ctx_queries else n_batch * seq
    rows = seq // GRID_W
    kr = min(NA_WIN_ROWS, rows)
    band = kr * GRID_W
    tables = _attention_tables(n_batch, seq, n_ctx, with_ctx_queries)
    n_items = len(tables[0])
    ctx_blk0 = n_batch * seq // n_ctx
    width = NA_WIDTH
    in_specs = [pl.BlockSpec((ATT_QROWS, width), lambda i, qb, bt, r0, cl: (qb[i], 0)),
                pl.BlockSpec((seq, width), lambda i, qb, bt, r0, cl: (bt[i], 0)),
                pl.BlockSpec((seq, width), lambda i, qb, bt, r0, cl: (bt[i], 0)),
                pl.BlockSpec((n_ctx, width), lambda i, qb, bt, r0, cl: (ctx_blk0 + bt[i], 0)),
                pl.BlockSpec((n_ctx, width), lambda i, qb, bt, r0, cl: (ctx_blk0 + bt[i], 0))]
    for row in range(ATT_ROWS):
        in_specs.append(pl.BlockSpec((1, NA_HEADS, GRID_W, band),
                                     lambda i, qb, bt, r0, cl, row=row: (cl[i * ATT_ROWS + row], 0, 0, 0)))
    out_specs = pl.BlockSpec((ATT_QROWS, width), lambda i, qb, bt, r0, cl: (qb[i], 0))
    grid_spec = pltpu.PrefetchScalarGridSpec(num_scalar_prefetch=4, grid=(n_items,),
                                             in_specs=in_specs, out_specs=out_specs)
    return pl.pallas_call(
        functools.partial(_attention_kernel, band=band),
        grid_spec=grid_spec,
        out_shape=jax.ShapeDtypeStruct((m, width), BF16),
        compiler_params=_params(1),
        name="attention",
    )(*[jnp.asarray(t) for t in tables], q, k, v, k, v, *([bias] * ATT_ROWS))


def _post_kernel(h_ref, att_ref, zb_ref, hf_ref, hb_ref, lg_ref,
                 sh1_ref, sc1_ref, g1_ref, sh2_ref, sc2_ref,
                 wgl_ref, bgl_ref, wpa_ref, wpc_ref, wpl_ref, wo_ref, bo_ref, l1g_ref, l1b_ref,
                 rw_ref, rb_ref, tri_ref, upper_ref,
                 h1_ref, xs_ref, pos_ref, cnt_ref):
    u1 = (_layer_norm(h_ref[...]) * (1.0 + sc1_ref[0]) + sh1_ref[0]).astype(BF16)
    gl = jnp.dot(u1, wgl_ref[...], preferred_element_type=F32) + bgl_ref[...]
    y_a = jnp.dot(att_ref[...], wpa_ref[...], preferred_element_type=F32)
    y_b = jnp.dot(zb_ref[...], wpc_ref[...], preferred_element_type=F32)
    zc = jax.nn.gelu(lg_ref[...]) * (hf_ref[...] + hb_ref[...])
    y_c = jnp.dot(zc.astype(BF16), wpl_ref[...], preferred_element_type=F32)
    merged = (jax.nn.sigmoid(gl[:, :D_MODEL]) * y_a + jax.nn.sigmoid(gl[:, D_MODEL:2 * D_MODEL]) * y_b
              + jax.nn.sigmoid(gl[:, 2 * D_MODEL:]) * y_c)
    y = jnp.dot(merged.astype(BF16), wo_ref[...], preferred_element_type=F32) + bo_ref[...]
    h1 = _layer_norm(DEEPNORM_ALPHA * h_ref[...] + g1_ref[0] * y) * l1g_ref[...] + l1b_ref[...]
    h1_ref[...] = h1
    u2 = _layer_norm(h1) * (1.0 + sc2_ref[0]) + sh2_ref[0]

    u_hi = u2.astype(BF16)
    u_lo = (u2 - u_hi.astype(F32)).astype(BF16)
    by_hi = jnp.dot(u_hi, rw_ref[...], preferred_element_type=F32)
    logits = (by_hi[:, :LANES] + by_hi[:, LANES:]
              + jnp.dot(u_lo, rw_ref[:, :LANES], preferred_element_type=F32) + rb_ref[...])
    tm = logits.shape[0]
    lane = lax.broadcasted_iota(jnp.int32, (tm, LANES), 1)
    lane_f = lane.astype(F32)
    work = logits
    tops, idxs, hots = [], [], []
    for _ in range(TOP_K):
        mx = jnp.max(work, axis=-1, keepdims=True)
        idx = jnp.min(jnp.where(work == mx, lane_f, float(LANES)), axis=-1, keepdims=True)
        hot = lane_f == idx
        work = jnp.where(hot, -3e38, work)
        tops.append(mx); idxs.append(idx); hots.append(hot)
    exps = [jnp.exp(t - tops[0]) for t in tops]
    den = exps[0] + exps[1] + exps[2] + exps[3]
    hot_all = jnp.zeros((tm, LANES), F32)
    for hot in hots:
        hot_all = hot_all + hot.astype(F32)
    cnt = jnp.sum(hot_all, axis=0, keepdims=True)
    cnt_pad = jnp.floor((cnt + (SUBLANES - 1.0)) * (1.0 / SUBLANES)) * SUBLANES
    off = jnp.dot(jnp.broadcast_to(cnt_pad, (SUBLANES, LANES)), upper_ref[...],
                  precision=lax.Precision.HIGHEST, preferred_element_type=F32)[0:1]
    slot = off + jnp.dot(tri_ref[...], hot_all.astype(BF16), preferred_element_type=F32)
    pos4 = jnp.zeros((tm, LANES), F32)
    w_tile = jnp.zeros((tm, LANES), F32)
    for kk in range(TOP_K):
        pos_k = jnp.sum(jnp.where(hots[kk], slot, 0.0), axis=-1, keepdims=True)
        pos4 = jnp.where(lane == kk, pos_k, pos4)
        p = exps[kk] / den
        p_hi = p.astype(BF16).astype(F32)
        p_mid = (p - p_hi).astype(BF16).astype(F32)
        p_lo = p - p_hi - p_mid
        w_tile = jnp.where(hots[kk], p_hi, w_tile)
        w_tile = jnp.where(lane_f == idxs[kk] + float(N_EXPERTS), p_mid, w_tile)
        w_tile = jnp.where(lane_f == idxs[kk] + float(2 * N_EXPERTS), p_lo, w_tile)
    pos_ref[...] = pos4
    sub = lax.broadcasted_iota(jnp.int32, (SUBLANES, LANES), 0)
    cnt_ref[...] = jnp.where(sub == 0, cnt, jnp.where(sub == 1, off, 0.0))

    n_sorted = xs_ref.shape[0]
    pos_t = pos4.T
    r_iota = lax.broadcasted_iota(jnp.int32, (n_sorted, tm), 0).astype(F32)
    hit = r_iota == pos_t[0:1, :]
    for kk in range(1, TOP_K):
        hit = jnp.logical_or(hit, r_iota == pos_t[kk:kk + 1, :])
    perm = jnp.where(hit, 1.0, 0.0).astype(BF16)
    feats = jnp.concatenate([u_hi, w_tile.astype(BF16)], axis=1)
    xs_ref[...] = jnp.dot(perm, feats, preferred_element_type=F32)


def _post_mixer(h, att, zb, hf, hb, rest, mod3, mod_base, w_gl, b_gl, wpa, wpc, wpl, wo, b_o, l1g, l1b,
                rw_pad, rb_pad, n_rows, n_lat, seq):
    tm = TM_POST
    width = BRANCH_WIDTH
    tiles_per_seq = seq // tm
    n_lat_tiles = n_lat // tm
    n_groups_lat = n_lat // seq
    tri = jnp.asarray(np.tril(np.ones((tm, tm), np.float32), -1), BF16)
    upper = jnp.asarray(np.triu(np.ones((LANES, LANES), np.float32), 1))
    n_tiles = n_rows // tm

    def group(i):
        return jnp.where(i < n_lat_tiles, i // tiles_per_seq, n_groups_lat)

    def rows(wd, col=0):
        return pl.BlockSpec((tm, wd), lambda i: (i, col))

    def full(shape):
        return pl.BlockSpec(shape, lambda i: (0,) * len(shape))

    def mod(which):
        return pl.BlockSpec((1, 1, D_MODEL), lambda i: (mod_base + group(i) * 6 + which, 0, 0))

    in_specs = [rows(D_MODEL), rows(width), rows(width), rows(width), rows(width), rows(width, 3),
                mod(0), mod(1), mod(2), mod(3), mod(4),
                full(w_gl.shape), full((1, w_gl.shape[1])),
                full(wpa.shape), full(wpc.shape), full(wpl.shape), full(wo.shape), full((1, D_MODEL)),
                full((1, D_MODEL)), full((1, D_MODEL)), full(rw_pad.shape), full(rb_pad.shape), full(tri.shape),
                full(upper.shape)]
    out_specs = [rows(D_MODEL), pl.BlockSpec((TOK_BLOCK, XS_WIDTH), lambda i: (i, 0)), rows(LANES),
                 pl.BlockSpec((SUBLANES, LANES), lambda i: (i, 0))]
    return pl.pallas_call(
        _post_kernel,
        grid=(n_tiles,),
        in_specs=in_specs, out_specs=out_specs,
        out_shape=[jax.ShapeDtypeStruct((n_rows, D_MODEL), F32),
                   jax.ShapeDtypeStruct((n_tiles * TOK_BLOCK, XS_WIDTH), F32),
                   jax.ShapeDtypeStruct((n_rows, LANES), F32),
                   jax.ShapeDtypeStruct((n_tiles * SUBLANES, LANES), F32)],
        compiler_params=_params(1),
        name="post_mixer",
    )(h, att, zb, hf, hb, rest, mod3, mod3, mod3, mod3, mod3,
      w_gl, b_gl.reshape(1, w_gl.shape[1]),
      wpa, wpc, wpl, wo, b_o.reshape(1, D_MODEL), l1g.reshape(1, D_MODEL), l1b.reshape(1, D_MODEL),
      rw_pad, rb_pad, tri, upper)


def _expert_plan(cnt_out, n_tok_tiles, n_rows):
    tm = TM_EXPERT
    co = cnt_out.reshape(n_tok_tiles, SUBLANES, LANES)
    cnt = (co[:, 0, :N_EXPERTS].astype(jnp.int32) + SUBLANES - 1) // SUBLANES * SUBLANES
    off = co[:, 1, :N_EXPERTS].astype(jnp.int32)
    cum_end = jnp.cumsum(cnt, axis=0)
    cum = cum_end - cnt
    total = cum_end[-1]
    n_et = (total + tm - 1) // tm
    et_end = jnp.cumsum(n_et)
    n_act = et_end[-1:].astype(jnp.int32)
    n_tiles = -(-n_tok_tiles * TOK_BLOCK // tm) + N_EXPERTS
    j = jnp.arange(n_tiles, dtype=jnp.int32)
    tile_e = jnp.minimum(jnp.sum((et_end[None, :] <= j[:, None]).astype(jnp.int32), axis=1), N_EXPERTS - 1)
    row0 = (j - (et_end - n_et)[tile_e]) * tm
    n_rows_tile = jnp.clip(total[tile_e] - row0, 0, tm)
    cum_e = cum.T[tile_e]
    cum_end_e = cum_end.T[tile_e]
    off_e = off.T[tile_e]
    q = row0[:, None] + SUBLANES * jnp.arange(tm // SUBLANES, dtype=jnp.int32)[None, :]
    tok_tile = jnp.minimum(jnp.sum((cum_end_e.T[:, :, None] <= q[None, :, :]).astype(jnp.int32), axis=0),
                           n_tok_tiles - 1)
    src = (tok_tile * TOK_BLOCK + jnp.take_along_axis(off_e, tok_tile, axis=1)
           + q - jnp.take_along_axis(cum_e, tok_tile, axis=1))
    used = off[:, N_EXPERTS - 1] + cnt[:, N_EXPERTS - 1]
    return (tile_e.astype(jnp.int32), n_rows_tile.astype(jnp.int32), n_act, src.reshape(-1).astype(jnp.int32),
            used.astype(jnp.int32), n_tiles)


def _expert_kernel(tile_e, n_rows_t, n_act, src_t, used_t,
                   xs_hbm, wgu_ref, bgu_ref, wdn_ref, bdn_ref, ys_hbm,
                   xin, yout, wgu_s, wdn_s, zeros, sem_in, sem_out, sem_zero, *, tm, tok_block, n_tok_tiles):
    j = pl.program_id(0)
    na = n_act[0]
    chunks = tm // SUBLANES

    def gather(src, dst, size, slot):
        return pltpu.make_async_copy(xs_hbm.at[pl.ds(src, size)], xin.at[slot, pl.ds(dst, size)], sem_in.at[slot])

    def scatter(src, dst, size, slot):
        return pltpu.make_async_copy(yout.at[slot, pl.ds(dst, size)], ys_hbm.at[pl.ds(src, size)], sem_out.at[slot])

    def start_chunks(jj, slot, copy):
        def one(c):
            src = pl.multiple_of(src_t[jj * chunks + c], SUBLANES)
            copy(src, pl.multiple_of(c * SUBLANES, SUBLANES), SUBLANES, slot).start()

        def body(c, carry):
            one(c)
            return carry

        def body_unrolled(g, carry):
            for u in range(SUBLANES):
                one(g * SUBLANES + u)
            return carry

        n = lax.shift_right_logical(n_rows_t[jj], 3)

        @pl.when(n == chunks)
        def _():
            lax.fori_loop(0, chunks // SUBLANES, body_unrolled, 0)

        @pl.when(n != chunks)
        def _():
            lax.fori_loop(0, n, body, 0)

    def wait_rows(jj, slot, copy):
        n = n_rows_t[jj]
        size = tm
        while size >= SUBLANES:
            @pl.when((n & size) != 0)
            def _(size=size):
                copy(0, 0, size, slot).wait()
            size //= 2

    slot = j % 2

    @pl.when(j == 0)
    def _():
        xin[...] = jnp.zeros_like(xin)
        start_chunks(0, 0, gather)
        zeros[...] = jnp.zeros_like(zeros)

        def clear_tail(i, copy_op):
            used = used_t[i]
            tail = tok_block - used
            for size in (256, 128, 64, 32, 16, 8):
                @pl.when((tail & size) != 0)
                def _():
                    at = pl.multiple_of(i * tok_block + used + (tail & ~(2 * size - 1)), SUBLANES)
                    copy_op(pltpu.make_async_copy(zeros.at[pl.ds(0, size)], ys_hbm.at[pl.ds(at, size)], sem_zero))

        def start_clear(i, c):
            clear_tail(i, lambda cp: cp.start())
            return c

        def wait_clear(i, c):
            clear_tail(i, lambda cp: cp.wait())
            return c

        lax.fori_loop(0, n_tok_tiles, start_clear, 0)
        lax.fori_loop(0, n_tok_tiles, wait_clear, 0)

    @pl.when(j + 1 < na)
    def _():
        start_chunks(j + 1, 1 - slot, gather)

    @pl.when(j < na)
    def _():
        e = tile_e[j]
        e_prev = tile_e[jnp.maximum(j - 1, 0)]

        @pl.when(jnp.logical_or(j == 0, e != e_prev))
        def _():
            wgu_s[...] = wgu_ref[0, 0].astype(BF16)
            wdn_s[...] = wdn_ref[0, 0].astype(BF16)

        wait_rows(j, slot, gather)
        x = xin[slot]
        lane = lax.broadcasted_iota(jnp.int32, (tm, LANES), 1)
        p = jnp.sum(jnp.where(lane % N_EXPERTS == e, x[:, D_MODEL:], 0.0), axis=-1, keepdims=True)
        gu = jnp.dot(x[:, :D_MODEL].astype(BF16), wgu_s[...], preferred_element_type=F32) + bgu_ref[0, 0]
        f = gu.shape[1] // 2
        gate = jnp.minimum(gu[:, :f], SWIGLU_LIMIT)
        up = jnp.clip(gu[:, f:], -SWIGLU_LIMIT, SWIGLU_LIMIT)
        hid = (up + 1.0) * gate * jax.nn.sigmoid(SWIGLU_ALPHA * gate)
        y = jnp.dot(hid.astype(BF16), wdn_s[...], preferred_element_type=F32) + bdn_ref[0, 0]
        yout[slot] = y * p
        start_chunks(j, slot, scatter)

        @pl.when(j >= 1)
        def _():
            wait_rows(j - 1, 1 - slot, scatter)

        @pl.when(j == na - 1)
        def _():
            wait_rows(j, slot, scatter)


def _experts(xs, plan, layer, w_gu, b_gu, w_dn, b_dn):
    tile_e, n_rows_tile, n_act, src, used, n_tiles = plan
    tm = TM_EXPERT
    f2 = w_gu.shape[-1]

    def expert(j, te, nr, na, *_):
        return (layer, te[jnp.minimum(j, na[0] - 1)], 0, 0)

    in_specs = [pl.BlockSpec(memory_space=pl.ANY),
                pl.BlockSpec((1, 1, D_MODEL, f2), expert),
                pl.BlockSpec((1, 1, 1, f2), expert),
                pl.BlockSpec((1, 1, f2 // 2, D_MODEL), expert),
                pl.BlockSpec((1, 1, 1, D_MODEL), expert)]
    grid_spec = pltpu.PrefetchScalarGridSpec(
        num_scalar_prefetch=5, grid=(n_tiles,), in_specs=in_specs,
        out_specs=pl.BlockSpec(memory_space=pl.ANY),
        scratch_shapes=[pltpu.VMEM((2, tm, XS_WIDTH), F32), pltpu.VMEM((2, tm, D_MODEL), F32),
                        pltpu.VMEM((D_MODEL, f2), BF16), pltpu.VMEM((f2 // 2, D_MODEL), BF16),
                        pltpu.VMEM((ZERO_ROWS, D_MODEL), F32),
                        pltpu.SemaphoreType.DMA((2,)), pltpu.SemaphoreType.DMA((2,)), pltpu.SemaphoreType.DMA(())])
    return pl.pallas_call(
        functools.partial(_expert_kernel, tm=tm, tok_block=TOK_BLOCK, n_tok_tiles=xs.shape[0] // TOK_BLOCK),
        grid_spec=grid_spec,
        out_shape=jax.ShapeDtypeStruct((xs.shape[0], D_MODEL), F32),
        compiler_params=_params(1),
        name="moe_experts",
    )(tile_e, n_rows_tile, n_act, src, used, xs, w_gu, b_gu.reshape(DEPTH, N_EXPERTS, 1, f2), w_dn,
      b_dn.reshape(DEPTH, N_EXPERTS, 1, D_MODEL))


def _combine_kernel(ys_ref, pos_ref, h1_ref, g2_ref, l2g_ref, l2b_ref, o_ref):
    tm = pos_ref.shape[0]
    n_sorted = ys_ref.shape[0]
    pos = pos_ref[...]
    col = lax.broadcasted_iota(jnp.int32, (tm, n_sorted), 1).astype(F32)
    sel = jnp.zeros((tm, n_sorted), F32)
    for kk in range(TOP_K):
        sel = sel + (col == pos[:, kk:kk + 1]).astype(F32)
    sel = sel.astype(BF16)
    ys = ys_ref[...]
    hi = ys.astype(BF16)
    rest = ys - hi.astype(F32)
    mid = rest.astype(BF16)
    lo = (rest - mid.astype(F32)).astype(BF16)
    y2 = (jnp.dot(sel, hi, preferred_element_type=F32) + jnp.dot(sel, mid, preferred_element_type=F32)
          + jnp.dot(sel, lo, preferred_element_type=F32))
    o_ref[...] = _layer_norm(DEEPNORM_ALPHA * h1_ref[...] + g2_ref[0] * y2) * l2g_ref[...] + l2b_ref[...]


def _combine(ys, pos4, h1, mod3, mod_base, l2g, l2b, n_lat, seq):
    n_rows = h1.shape[0]
    tm = TM_POST
    tiles_per_seq = seq // tm
    n_lat_tiles = n_lat // tm
    n_groups_lat = n_lat // seq

    def group(i):
        return jnp.where(i < n_lat_tiles, i // tiles_per_seq, n_groups_lat)

    in_specs = [pl.BlockSpec((TOK_BLOCK, D_MODEL), lambda i: (i, 0)),
                pl.BlockSpec((tm, LANES), lambda i: (i, 0)),
                pl.BlockSpec((tm, D_MODEL), lambda i: (i, 0)),
                pl.BlockSpec((1, 1, D_MODEL), lambda i: (mod_base + group(i) * 6 + 5, 0, 0)),
                pl.BlockSpec((1, D_MODEL), lambda i: (0, 0)),
                pl.BlockSpec((1, D_MODEL), lambda i: (0, 0))]
    return pl.pallas_call(
        _combine_kernel,
        grid=(n_rows // tm,),
        in_specs=in_specs,
        out_specs=pl.BlockSpec((tm, D_MODEL), lambda i: (i, 0)),
        out_shape=jax.ShapeDtypeStruct((n_rows, D_MODEL), F32),
        compiler_params=_params(1),
        name="moe_combine",
    )(ys, pos4, h1, mod3, l2g.reshape(1, D_MODEL), l2b.reshape(1, D_MODEL))


def _block_diag(w):
    two, n, d, e = w.shape
    eye = jnp.eye(n, dtype=w.dtype)
    return (w[:, :, :, None, :] * eye[None, :, None, :, None]).reshape(two, n * d, n * e)


def kernel(x, c, ctx, c_ctx, w_mod, b_mod, w_in, b_in, na_rpb, w_proj_attn, w_proj_conv, w_proj_lru, sc_conv_w, lru_conv_w, lru_conv_b, lru_lambda, lru_w_r, lru_b_r, lru_w_i, lru_b_i, w_o, b_o, ln1_g, ln1_b, router_w, router_b, exp_w_gu, exp_b_gu, exp_w_dn, exp_b_dn, ln2_g, ln2_b):
    n_batch, seq, d = x.shape
    n_ctx = ctx.shape[1]
    n_lat = n_batch * seq
    n_all = n_lat + n_batch * n_ctx
    assert d == D_MODEL and n_batch + 1 <= SUBLANES

    cc = jnp.concatenate([c, c_ctx[None], jnp.zeros((SUBLANES - n_batch - 1, d), F32)], axis=0)
    mod = _modulation(cc, w_mod, b_mod)
    groups = n_batch + 1
    mod3 = mod.reshape(DEPTH, SUBLANES, 6, d)[:, :groups].reshape(DEPTH * groups * 6, 1, d)

    cos_t, sin_t = _make_rope(seq, TM_INPROJ)
    h = jnp.concatenate([x.reshape(n_lat, d), ctx.reshape(n_batch * n_ctx, d)], axis=0)

    for layer in range(DEPTH):
        last = layer == DEPTH - 1
        mod_base = layer * groups * 6
        w_in_bf = w_in[layer].astype(BF16)
        q, k, v, sb, rest = _input_projection(h, mod3, mod_base, w_in_bf[:, :N_EARLY], b_in[layer, :N_EARLY],
                                              cos_t, sin_t, n_lat, seq)
        sp = jax.nn.softplus(-lru_lambda[layer])
        zb, hf, hb = _conv_scan(sb, rest, sc_conv_w[layer], lru_conv_w[layer], lru_conv_b[layer], sp,
                                _block_diag(lru_w_r[layer]).astype(BF16), _block_diag(lru_w_i[layer]).astype(BF16),
                                lru_b_r[layer], lru_b_i[layer], n_batch, seq, n_ctx)
        att = _attention(q, k, v, _attention_bias(na_rpb[layer], seq), n_batch, seq, n_ctx, not last)
        n_rows = n_lat if last else n_all
        rw_full = jnp.pad(router_w[layer], ((0, 0), (0, LANES - N_EXPERTS)))
        rw_hi = rw_full.astype(BF16)
        rw_pad = jnp.concatenate([rw_hi, (rw_full - rw_hi.astype(F32)).astype(BF16)], axis=1)
        rb_pad = jnp.concatenate([router_b[layer], jnp.full((LANES - N_EXPERTS,), NEG_BIG, F32)]).reshape(1, LANES)
        h1, xs, pos4, cnt_out = _post_mixer(
            h, att, zb, hf, hb, rest, mod3, mod_base, w_in_bf[:, N_EARLY:], b_in[layer, N_EARLY:],
            w_proj_attn[layer].astype(BF16), w_proj_conv[layer].astype(BF16), w_proj_lru[layer].astype(BF16),
            w_o[layer].astype(BF16), b_o[layer], ln1_g[layer], ln1_b[layer], rw_pad, rb_pad, n_rows, n_lat, seq)
        plan = _expert_plan(cnt_out, n_rows // TM_POST, n_rows)
        ys = _experts(xs, plan, layer, exp_w_gu, exp_b_gu, exp_w_dn, exp_b_dn)
        h = _combine(ys, pos4, h1, mod3, mod_base, ln2_g[layer], ln2_b[layer], n_lat, seq)
    return h.reshape(n_batch, seq, d)
```

```python
import functools

import numpy as np
import jax
import jax.numpy as jnp
from jax import lax
from jax.experimental import pallas as pl
from jax.experimental.pallas import tpu as pltpu

D_MODEL = 1024
DEPTH = 2
GRID_W = 64
NA_HEADS = 8
NA_HEAD_DIM = 64
NA_WIDTH = NA_HEADS * NA_HEAD_DIM
NA_WIN_ROWS = 8
NA_WIN_COLS = 16
ROPE_BASE = 10000.0
BRANCH_WIDTH = 512
LRU_BLOCKS = 8
LRU_C = 8.0
P_TOTAL = 7168
N_EARLY = 8 * BRANCH_WIDTH
N_EXPERTS = 32
TOP_K = 4
SWIGLU_LIMIT = 7.0
SWIGLU_ALPHA = 1.702
LN_EPS = 1e-5
DEEPNORM_ALPHA = (2 * DEPTH) ** 0.25
NEG_BIG = -1e30

LANES = 128
SUBLANES = 8
VMEM_LIMIT_BYTES = 56 * 1024 * 1024

TM_INPROJ = 1024
TN_INPROJ = 1024
SCAN_CHUNK = 256
ATT_ROWS = 4
ATT_QROWS = ATT_ROWS * GRID_W
TM_POST = 256
TM_EXPERT = 512
XS_WIDTH = D_MODEL + LANES
TOK_BLOCK = TM_POST * TOP_K + N_EXPERTS * SUBLANES
ZERO_ROWS = 256

F32 = jnp.float32
BF16 = jnp.bfloat16


def _params(n_axes):
    return pltpu.CompilerParams(dimension_semantics=("arbitrary",) * n_axes,
                                vmem_limit_bytes=VMEM_LIMIT_BYTES)


def _layer_norm(x):
    mu = jnp.mean(x, axis=-1, keepdims=True)
    xc = x - mu
    var = jnp.mean(xc * xc, axis=-1, keepdims=True)
    return xc * lax.rsqrt(var + LN_EPS)


def _mod_kernel(c_ref, w_ref, b_ref, o_ref):
    c = c_ref[...]
    s = (c * jax.nn.sigmoid(c)).astype(BF16)
    o_ref[0] = jnp.dot(s, w_ref[0].astype(BF16), preferred_element_type=F32) + b_ref[0]


def _modulation(cc, w_mod, b_mod):
    n_out = w_mod.shape[-1]
    return pl.pallas_call(
        _mod_kernel,
        grid=(DEPTH, n_out // D_MODEL),
        in_specs=[pl.BlockSpec((SUBLANES, D_MODEL), lambda l, j: (0, 0)),
                  pl.BlockSpec((1, D_MODEL, D_MODEL), lambda l, j: (l, 0, j)),
                  pl.BlockSpec((1, 1, D_MODEL), lambda l, j: (l, 0, j))],
        out_specs=pl.BlockSpec((1, SUBLANES, D_MODEL), lambda l, j: (l, 0, j)),
        out_shape=jax.ShapeDtypeStruct((DEPTH, SUBLANES, n_out), F32),
        compiler_params=_params(2),
        name="modulation",
    )(cc, w_mod, b_mod.reshape(DEPTH, 1, n_out))


def _rope_half(x, cos, sin_signed):
    lane = lax.broadcasted_iota(jnp.int32, (x.shape[0], LANES), 1)
    first = (lane % 32) < 16
    outs = []
    for cidx in range(x.shape[1] // LANES):
        xc = x[:, cidx * LANES:(cidx + 1) * LANES]
        partner = jnp.where(first, pltpu.roll(xc, LANES - 16, 1), pltpu.roll(xc, 16, 1))
        outs.append(xc * cos + partner * sin_signed)
    return jnp.concatenate(outs, axis=1)


def _inproj_kernel(h_ref, sh_ref, sc_ref, w_ref, b_ref, cos_ref, sin_ref,
                   q_ref, k_ref, v_ref, sb_ref, rest_ref, xn_ref):
    j = pl.program_id(1)

    @pl.when(j == 0)
    def _():
        y = _layer_norm(h_ref[...])
        xn_ref[...] = (y * (1.0 + sc_ref[0]) + sh_ref[0]).astype(BF16)

    acc = jnp.dot(xn_ref[...], w_ref[...], preferred_element_type=F32) + b_ref[...]
    half = BRANCH_WIDTH

    @pl.when(j == 0)
    def _():
        cos = cos_ref[...]
        sin = sin_ref[...]
        q_ref[...] = _rope_half(acc[:, :half], cos, sin).astype(BF16)
        k_ref[...] = _rope_half(acc[:, half:], cos, sin).astype(BF16)

    @pl.when(j == 1)
    def _():
        v_ref[...] = acc[:, :half].astype(BF16)
        sb_ref[...] = acc[:, half:]

    @pl.when(j >= 2)
    def _():
        rest_ref[...] = acc


def _input_projection(h, mod3, mod_base, w_in_bf, b_in, cos_t, sin_t, n_lat, seq):
    m = h.shape[0]
    tm, tn = TM_INPROJ, TN_INPROJ
    n_lat_tiles = n_lat // tm
    tiles_per_seq = seq // tm
    n_groups_lat = n_lat // seq

    def group(i):
        return jnp.where(i < n_lat_tiles, i // tiles_per_seq, n_groups_lat)

    def rope_blk(i):
        return jnp.where(i < n_lat_tiles, i % tiles_per_seq, tiles_per_seq)

    half = BRANCH_WIDTH
    n_cols = w_in_bf.shape[1]
    return pl.pallas_call(
        _inproj_kernel,
        grid=(m // tm, n_cols // tn),
        in_specs=[pl.BlockSpec((tm, D_MODEL), lambda i, j: (i, 0)),
                  pl.BlockSpec((1, 1, D_MODEL), lambda i, j: (mod_base + group(i) * 6 + 0, 0, 0)),
                  pl.BlockSpec((1, 1, D_MODEL), lambda i, j: (mod_base + group(i) * 6 + 1, 0, 0)),
                  pl.BlockSpec((D_MODEL, tn), lambda i, j: (0, j)),
                  pl.BlockSpec((1, tn), lambda i, j: (0, j)),
                  pl.BlockSpec((tm, LANES), lambda i, j: (rope_blk(i), 0)),
                  pl.BlockSpec((tm, LANES), lambda i, j: (rope_blk(i), 0))],
        out_specs=[pl.BlockSpec((tm, half), lambda i, j: (i, 0)),
                   pl.BlockSpec((tm, half), lambda i, j: (i, 0)),
                   pl.BlockSpec((tm, half), lambda i, j: (i, 0)),
                   pl.BlockSpec((tm, half), lambda i, j: (i, 0)),
                   pl.BlockSpec((tm, tn), lambda i, j: (i, jnp.maximum(j - 2, 0)))],
        out_shape=[jax.ShapeDtypeStruct((m, half), BF16),
                   jax.ShapeDtypeStruct((m, half), BF16),
                   jax.ShapeDtypeStruct((m, half), BF16),
                   jax.ShapeDtypeStruct((m, half), F32),
                   jax.ShapeDtypeStruct((m, n_cols - 2 * tn), F32)],
        scratch_shapes=[pltpu.VMEM((tm, D_MODEL), BF16)],
        compiler_params=_params(2),
        name="input_projection",
    )(h, mod3, mod3, w_in_bf, b_in.reshape(1, n_cols), cos_t, sin_t)


def _make_rope(seq, tm):
    t = np.arange(seq)
    row_pos, col_pos = t // GRID_W, t % GRID_W
    d = np.arange(LANES) % NA_HEAD_DIM
    m = NA_HEAD_DIM // 4
    inv_freq = (ROPE_BASE ** (-jnp.arange(m, dtype=F32) / m))[d % m]
    pos = np.where((d < 2 * m)[None, :], row_pos[:, None], col_pos[:, None])
    ang = jnp.asarray(pos).astype(F32) * inv_freq[None, :]
    cos = jnp.cos(ang)
    sin = jnp.sin(ang)
    sin_signed = jnp.where(jnp.asarray((d % (2 * m)) < m)[None, :], -sin, sin)
    cos = jnp.concatenate([cos, jnp.ones((tm, LANES), F32)], axis=0)
    sin_signed = jnp.concatenate([sin_signed, jnp.zeros((tm, LANES), F32)], axis=0)
    return cos, sin_signed


def _scan_kernel(fblk, bblk, first, last, seqb,
                 sb_ref, scg_ref, sx_ref, lxf_ref,
                 scgp_ref, sxp_ref, lxfp_ref, scgn_ref, sxn_ref, lxfn_ref,
                 lxb_ref, lxbp_ref, lxbn_ref,
                 scw_ref, cw_ref, cb_ref, sp_ref, wr_ref, wi_ref, br_ref, bi_ref,
                 zb_ref, hf_ref, hb_ref,
                 a_s, b_s, hc_s, st_s, *, ch, n_ctx_items):
    it = pl.program_id(0)
    is_first = first[it] == 1
    is_last = last[it] == 1
    is_ctx = it < n_ctx_items
    b = seqb[it]
    width = BRANCH_WIDTH
    row = lax.broadcasted_iota(jnp.int32, (ch, width), 0)
    not_first = jnp.where(is_first, 0.0, 1.0).astype(F32)
    not_last = jnp.where(is_last, 0.0, 1.0).astype(F32)

    def back1(u, prev_row):
        return jnp.where(row == 0, prev_row, pltpu.roll(u, 1, 0))

    def back2(u, prev2, prev1):
        return jnp.where(row == 0, prev2, jnp.where(row == 1, prev1, pltpu.roll(u, 2, 0)))

    def fwd1(u, next_row):
        return jnp.where(row == ch - 1, next_row, pltpu.roll(u, ch - 1, 0))

    def lru_input(lx_ref, lxp_ref, lxn_ref, prev_ok, next_ok):
        x = lx_ref[...]
        p = lxp_ref[...] * prev_ok
        n = lxn_ref[...] * next_ok
        return (cw_ref[0:1] * back2(x, p[6:7], p[7:8]) + cw_ref[1:2] * back1(x, p[7:8])
                + cw_ref[2:3] * x + cw_ref[3:4] * fwd1(x, n[0:1]) + cb_ref[...])

    def coeffs(d, xm):
        xb = xm.astype(BF16)
        r = jax.nn.sigmoid(jnp.dot(xb, wr_ref[d], preferred_element_type=F32) + br_ref[d:d + 1])
        g = jax.nn.sigmoid(jnp.dot(xb, wi_ref[d], preferred_element_type=F32) + bi_ref[d:d + 1])
        log_a = (-LRU_C * sp_ref[d:d + 1]) * r
        a = jnp.exp(log_a)
        a_s[d] = a
        b_s[d] = jnp.sqrt(-jnp.tanh(log_a) * (a * a + 1.0)) * (g * xm)

    u = scg_ref[...] * sx_ref[...]
    u_prev = scgp_ref[7:8] * sxp_ref[7:8] * not_first
    u_next = scgn_ref[0:1] * sxn_ref[0:1] * not_last
    conv = scw_ref[0:1] * back1(u, u_prev) + scw_ref[1:2] * u + scw_ref[2:3] * fwd1(u, u_next)
    zb_ref[...] = (sb_ref[...] * conv).astype(BF16)

    coeffs(0, lru_input(lxf_ref, lxfp_ref, lxfn_ref, not_first, not_last))
    coeffs(1, lru_input(lxb_ref, lxbp_ref, lxbn_ref, not_last, not_first))

    @pl.when(jnp.logical_and(is_first, is_ctx))
    def _():
        hc_s[...] = jnp.zeros_like(hc_s)

    @pl.when(jnp.logical_and(is_first, jnp.logical_not(is_ctx)))
    def _():
        hc_s[0:1] = st_s[pl.ds(2 * b, 1), :]
        hc_s[1:2] = st_s[pl.ds(2 * b + 1, 1), :]

    def body(i, carry):
        hf, hb = carry
        base = i * SUBLANES
        for r in range(SUBLANES):
            t = base + r
            hf = a_s[0, pl.ds(t, 1), :] * hf + b_s[0, pl.ds(t, 1), :]
            hf_ref[pl.ds(t, 1), :] = hf
            tb = ch - 1 - t
            hb = a_s[1, pl.ds(tb, 1), :] * hb + b_s[1, pl.ds(tb, 1), :]
            hb_ref[pl.ds(tb, 1), :] = hb
        return hf, hb

    hf, hb = lax.fori_loop(0, ch // SUBLANES, body, (hc_s[0:1], hc_s[1:2]))
    hc_s[0:1] = hf
    hc_s[1:2] = hb

    @pl.when(is_ctx)
    def _():
        st_s[pl.ds(2 * b, 1), :] = hf
        st_s[pl.ds(2 * b + 1, 1), :] = hb


def _scan_tables(n_batch, seq, n_ctx, ch):
    assert n_ctx == ch
    nc = seq // ch
    ctx0 = n_batch * seq // ch
    fblk, bblk, first, last, seqb = [], [], [], [], []
    for b in range(n_batch):
        fblk.append(ctx0 + b); bblk.append(ctx0 + b); first.append(1); last.append(1); seqb.append(b)
    for b in range(n_batch):
        for c in range(nc):
            fblk.append(b * nc + c); bblk.append(b * nc + nc - 1 - c)
            first.append(int(c == 0)); last.append(int(c == nc - 1)); seqb.append(b)
    return [np.asarray(a, np.int32) for a in (fblk, bblk, first, last, seqb)]


def _conv_scan(sb, rest, sc_w, lru_cw, lru_cb, sp, wr_bd, wi_bd, b_r, b_i, n_batch, seq, n_ctx):
    m = sb.shape[0]
    ch = SCAN_CHUNK
    width = BRANCH_WIDTH
    tables = _scan_tables(n_batch, seq, n_ctx, ch)
    n_items = len(tables[0])
    halo_per_chunk = ch // SUBLANES
    last_halo = m // SUBLANES - 1

    def cur(col, which):
        return pl.BlockSpec((ch, width), lambda i, f, bk, *_: ((f, bk)[which][i], col))

    def prev(col, which):
        return pl.BlockSpec((SUBLANES, width),
                            lambda i, f, bk, *_: (jnp.maximum((f, bk)[which][i] * halo_per_chunk - 1, 0), col))

    def nxt(col, which):
        return pl.BlockSpec((SUBLANES, width),
                            lambda i, f, bk, *_: (jnp.minimum(((f, bk)[which][i] + 1) * halo_per_chunk, last_halo), col))

    def full(shape):
        return pl.BlockSpec(shape, lambda i, *_: (0,) * len(shape))

    in_specs = [cur(0, 0), cur(0, 0), cur(1, 0), cur(2, 0),
                prev(0, 0), prev(1, 0), prev(2, 0), nxt(0, 0), nxt(1, 0), nxt(2, 0),
                cur(2, 1), prev(2, 1), nxt(2, 1),
                full(sc_w.shape), full(lru_cw.shape), full((1, width)), full(sp.shape),
                full(wr_bd.shape), full(wi_bd.shape), full(b_r.shape), full(b_i.shape)]
    out_specs = [cur(0, 0), cur(0, 0), cur(0, 1)]
    grid_spec = pltpu.PrefetchScalarGridSpec(
        num_scalar_prefetch=5, grid=(n_items,), in_specs=in_specs, out_specs=out_specs,
        scratch_shapes=[pltpu.VMEM((2, ch, width), F32), pltpu.VMEM((2, ch, width), F32),
                        pltpu.VMEM((SUBLANES, width), F32), pltpu.VMEM((2 * n_batch, width), F32)])
    return pl.pallas_call(
        functools.partial(_scan_kernel, ch=ch, n_ctx_items=n_batch),
        grid_spec=grid_spec,
        out_shape=[jax.ShapeDtypeStruct((m, width), BF16),
                   jax.ShapeDtypeStruct((m, width), F32),
                   jax.ShapeDtypeStruct((m, width), F32)],
        compiler_params=_params(1),
        name="conv_scan",
    )(*[jnp.asarray(t) for t in tables],
      sb, rest, rest, rest, rest, rest, rest, rest, rest, rest, rest, rest, rest,
      sc_w, lru_cw, lru_cb.reshape(1, width), sp, wr_bd, wi_bd, b_r, b_i)


def _attention_kernel(qblk, bat, r0t, cls, q_ref, k_ref, v_ref, kc_ref, vc_ref, *rest, band):
    bias_refs, o_ref = rest[:ATT_ROWS], rest[ATT_ROWS]
    it = pl.program_id(0)
    nq = GRID_W
    lane = lax.broadcasted_iota(jnp.int32, (nq, LANES), 1)
    low = lane < NA_HEAD_DIM
    scale = NA_HEAD_DIM ** -0.5
    nt = (((1,), (1,)), ((), ()))
    for row in range(ATT_ROWS):
        start = pl.multiple_of(r0t[it * ATT_ROWS + row] * GRID_W, GRID_W)
        bias_ref = bias_refs[row]
        qrows = slice(row * nq, (row + 1) * nq)
        for hp in range(NA_HEADS // 2):
            cols = slice(hp * LANES, (hp + 1) * LANES)
            qp = q_ref[qrows, cols].astype(F32)
            qs = jnp.concatenate([jnp.where(low, qp, 0.0), jnp.where(low, 0.0, qp)], axis=0).astype(BF16)
            kb = k_ref[pl.ds(start, band), cols]
            vb = v_ref[pl.ds(start, band), cols]
            s_loc = lax.dot_general(qs, kb, nt, preferred_element_type=F32)
            s_ctx = lax.dot_general(qs, kc_ref[:, cols], nt, preferred_element_type=F32)
            bias = jnp.concatenate([bias_ref[0, 2 * hp], bias_ref[0, 2 * hp + 1]], axis=0)
            s_loc = s_loc * scale + bias
            s_ctx = s_ctx * scale
            mx = jnp.maximum(jnp.max(s_loc, axis=-1, keepdims=True), jnp.max(s_ctx, axis=-1, keepdims=True))
            e_loc = jnp.exp(s_loc - mx)
            e_ctx = jnp.exp(s_ctx - mx)
            den = jnp.sum(e_loc, axis=-1, keepdims=True) + jnp.sum(e_ctx, axis=-1, keepdims=True)
            o = (jnp.dot(e_loc.astype(BF16), vb, preferred_element_type=F32)
                 + jnp.dot(e_ctx.astype(BF16), vc_ref[:, cols], preferred_element_type=F32)) / den
            o_ref[qrows, cols] = jnp.where(low, o[:nq], o[nq:]).astype(BF16)


def _attention_tables(n_batch, seq, n_ctx, with_ctx_queries):
    rows = seq // GRID_W
    kr = min(NA_WIN_ROWS, rows)
    assert rows % ATT_ROWS == 0 and n_ctx % ATT_QROWS == 0
    qblk, bat, r0t, cls = [], [], [], []
    ctx_q0 = n_batch * seq // ATT_QROWS
    for b in range(n_batch):
        for rg in range(rows // ATT_ROWS):
            qblk.append(b * (rows // ATT_ROWS) + rg); bat.append(b)
            for r in range(rg * ATT_ROWS, (rg + 1) * ATT_ROWS):
                r0 = min(max(r - kr // 2, 0), rows - kr)
                r0t.append(r0); cls.append(r - r0)
        if with_ctx_queries:
            for c in range(n_ctx // ATT_QROWS):
                qblk.append(ctx_q0 + b * (n_ctx // ATT_QROWS) + c); bat.append(b)
                r0t.extend([0] * ATT_ROWS); cls.extend([kr] * ATT_ROWS)
    return [np.asarray(a, np.int32) for a in (qblk, bat, r0t, cls)]


def _attention_bias(rpb, seq):
    rows = seq // GRID_W
    kr = min(NA_WIN_ROWS, rows)
    kc = NA_WIN_COLS
    cq = np.arange(GRID_W)
    c0 = np.clip(cq - kc // 2, 0, GRID_W - kc)
    ck = np.arange(GRID_W)
    inside = (ck[None, :] >= c0[:, None]) & (ck[None, :] < c0[:, None] + kc)
    dc = np.clip(ck[None, :] - cq[:, None] + (NA_WIN_COLS - 1), 0, 2 * NA_WIN_COLS - 2)
    n_dr = 2 * NA_WIN_ROWS - 1
    n_dc = 2 * NA_WIN_COLS - 1
    pick = jnp.asarray((np.arange(n_dc)[:, None] == dc.reshape(1, -1)).astype(np.float32))
    picked = jnp.dot(rpb.reshape(-1, n_dc), pick, precision=lax.Precision.HIGHEST)
    picked = picked.reshape(NA_HEADS, 2 * NA_WIN_ROWS - 1, GRID_W, GRID_W)
    table = jnp.where(jnp.asarray(inside)[None, None], picked, NEG_BIG)
    table = table.transpose(0, 2, 1, 3).reshape(NA_HEADS, GRID_W, n_dr * GRID_W)
    classes = []
    for cl in range(kr):
        lo = (NA_WIN_ROWS - 1 - cl) * GRID_W
        classes.append(table[:, :, lo:lo + kr * GRID_W])
    classes.append(jnp.full((NA_HEADS, GRID_W, kr * GRID_W), NEG_BIG, F32))
    return jnp.stack(classes, axis=0)


def _attention(q, k, v, bias, n_batch, seq, n_ctx, with_ctx_queries):
    m = q.shape[0] if with_ctx_queries else n_batch * seq
    rows = seq // GRID_W
    kr = min(NA_WIN_ROWS, rows)
    band = kr * GRID_W
    tables = _attention_tables(n_batch, seq, n_ctx, with_ctx_queries)
    n_items = len(tables[0])
    ctx_blk0 = n_batch * seq // n_ctx
    width = NA_WIDTH
    in_specs = [pl.BlockSpec((ATT_QROWS, width), lambda i, qb, bt, r0, cl: (qb[i], 0)),
                pl.BlockSpec((seq, width), lambda i, qb, bt, r0, cl: (bt[i], 0)),
                pl.BlockSpec((seq, width), lambda i, qb, bt, r0, cl: (bt[i], 0)),
                pl.BlockSpec((n_ctx, width), lambda i, qb, bt, r0, cl: (ctx_blk0 + bt[i], 0)),
                pl.BlockSpec((n_ctx, width), lambda i, qb, bt, r0, cl: (ctx_blk0 + bt[i], 0))]
    for row in range(ATT_ROWS):
        in_specs.append(pl.BlockSpec((1, NA_HEADS, GRID_W, band),
                                     lambda i, qb, bt, r0, cl, row=row: (cl[i * ATT_ROWS + row], 0, 0, 0)))
    out_specs = pl.BlockSpec((ATT_QROWS, width), lambda i, qb, bt, r0, cl: (qb[i], 0))
    grid_spec = pltpu.PrefetchScalarGridSpec(num_scalar_prefetch=4, grid=(n_items,),
                                             in_specs=in_specs, out_specs=out_specs)
    return pl.pallas_call(
        functools.partial(_attention_kernel, band=band),
        grid_spec=grid_spec,
        out_shape=jax.ShapeDtypeStruct((m, width), BF16),
        compiler_params=_params(1),
        name="attention",
    )(*[jnp.asarray(t) for t in tables], q, k, v, k, v, *([bias] * ATT_ROWS))


def _post_kernel(h_ref, att_ref, zb_ref, hf_ref, hb_ref, lg_ref,
                 sh1_ref, sc1_ref, g1_ref, sh2_ref, sc2_ref,
                 wgl_ref, bgl_ref, wpa_ref, wpc_ref, wpl_ref, wo_ref, bo_ref, l1g_ref, l1b_ref,
                 rw_ref, rb_ref, tri_ref, upper_ref,
                 h1_ref, xs_ref, pos_ref, cnt_ref):
    u1 = (_layer_norm(h_ref[...]) * (1.0 + sc1_ref[0]) + sh1_ref[0]).astype(BF16)
    gl = jnp.dot(u1, wgl_ref[...], preferred_element_type=F32) + bgl_ref[...]
    y_a = jnp.dot(att_ref[...], wpa_ref[...], preferred_element_type=F32)
    y_b = jnp.dot(zb_ref[...], wpc_ref[...], preferred_element_type=F32)
    zc = jax.nn.gelu(lg_ref[...]) * (hf_ref[...] + hb_ref[...])
    y_c = jnp.dot(zc.astype(BF16), wpl_ref[...], preferred_element_type=F32)
    merged = (jax.nn.sigmoid(gl[:, :D_MODEL]) * y_a + jax.nn.sigmoid(gl[:, D_MODEL:2 * D_MODEL]) * y_b
              + jax.nn.sigmoid(gl[:, 2 * D_MODEL:]) * y_c)
    y = jnp.dot(merged.astype(BF16), wo_ref[...], preferred_element_type=F32) + bo_ref[...]
    h1 = _layer_norm(DEEPNORM_ALPHA * h_ref[...] + g1_ref[0] * y) * l1g_ref[...] + l1b_ref[...]
    h1_ref[...] = h1
    u2 = _layer_norm(h1) * (1.0 + sc2_ref[0]) + sh2_ref[0]

    u_hi = u2.astype(BF16)
    u_lo = (u2 - u_hi.astype(F32)).astype(BF16)
    by_hi = jnp.dot(u_hi, rw_ref[...], preferred_element_type=F32)
    logits = (by_hi[:, :LANES] + by_hi[:, LANES:]
              + jnp.dot(u_lo, rw_ref[:, :LANES], preferred_element_type=F32) + rb_ref[...])
    tm = logits.shape[0]
    lane = lax.broadcasted_iota(jnp.int32, (tm, LANES), 1)
    lane_f = lane.astype(F32)
    work = logits
    tops, idxs, hots = [], [], []
    for _ in range(TOP_K):
        mx = jnp.max(work, axis=-1, keepdims=True)
        idx = jnp.min(jnp.where(work == mx, lane_f, float(LANES)), axis=-1, keepdims=True)
        hot = lane_f == idx
        work = jnp.where(hot, -3e38, work)
        tops.append(mx); idxs.append(idx); hots.append(hot)
    exps = [jnp.exp(t - tops[0]) for t in tops]
    den = exps[0] + exps[1] + exps[2] + exps[3]
    hot_all = jnp.zeros((tm, LANES), F32)
    for hot in hots:
        hot_all = hot_all + hot.astype(F32)
    cnt = jnp.sum(hot_all, axis=0, keepdims=True)
    cnt_pad = jnp.floor((cnt + (SUBLANES - 1.0)) * (1.0 / SUBLANES)) * SUBLANES
    off = jnp.dot(jnp.broadcast_to(cnt_pad, (SUBLANES, LANES)), upper_ref[...],
                  precision=lax.Precision.HIGHEST, preferred_element_type=F32)[0:1]
    slot = off + jnp.dot(tri_ref[...], hot_all.astype(BF16), preferred_element_type=F32)
    pos4 = jnp.zeros((tm, LANES), F32)
    w_tile = jnp.zeros((tm, LANES), F32)
    for kk in range(TOP_K):
        pos_k = jnp.sum(jnp.where(hots[kk], slot, 0.0), axis=-1, keepdims=True)
        pos4 = jnp.where(lane == kk, pos_k, pos4)
        p = exps[kk] / den
        p_hi = p.astype(BF16).astype(F32)
        p_mid = (p - p_hi).astype(BF16).astype(F32)
        p_lo = p - p_hi - p_mid
        w_tile = jnp.where(hots[kk], p_hi, w_tile)
        w_tile = jnp.where(lane_f == idxs[kk] + float(N_EXPERTS), p_mid, w_tile)
        w_tile = jnp.where(lane_f == idxs[kk] + float(2 * N_EXPERTS), p_lo, w_tile)
    pos_ref[...] = pos4
    sub = lax.broadcasted_iota(jnp.int32, (SUBLANES, LANES), 0)
    cnt_ref[...] = jnp.where(sub == 0, cnt, jnp.where(sub == 1, off, 0.0))

    n_sorted = xs_ref.shape[0]
    pos_t = pos4.T
    r_iota = lax.broadcasted_iota(jnp.int32, (n_sorted, tm), 0).astype(F32)
    hit = r_iota == pos_t[0:1, :]
    for kk in range(1, TOP_K):
        hit = jnp.logical_or(hit, r_iota == pos_t[kk:kk + 1, :])
    perm = jnp.where(hit, 1.0, 0.0).astype(BF16)
    feats = jnp.concatenate([u_hi, w_tile.astype(BF16)], axis=1)
    xs_ref[...] = jnp.dot(perm, feats, preferred_element_type=F32)


def _post_mixer(h, att, zb, hf, hb, rest, mod3, mod_base, w_gl, b_gl, wpa, wpc, wpl, wo, b_o, l1g, l1b,
                rw_pad, rb_pad, n_rows, n_lat, seq):
    tm = TM_POST
    width = BRANCH_WIDTH
    tiles_per_seq = seq // tm
    n_lat_tiles = n_lat // tm
    n_groups_lat = n_lat // seq
    tri = jnp.asarray(np.tril(np.ones((tm, tm), np.float32), -1), BF16)
    upper = jnp.asarray(np.triu(np.ones((LANES, LANES), np.float32), 1))
    n_tiles = n_rows // tm

    def group(i):
        return jnp.where(i < n_lat_tiles, i // tiles_per_seq, n_groups_lat)

    def rows(wd, col=0):
        return pl.BlockSpec((tm, wd), lambda i: (i, col))

    def full(shape):
        return pl.BlockSpec(shape, lambda i: (0,) * len(shape))

    def mod(which):
        return pl.BlockSpec((1, 1, D_MODEL), lambda i: (mod_base + group(i) * 6 + which, 0, 0))

    in_specs = [rows(D_MODEL), rows(width), rows(width), rows(width), rows(width), rows(width, 3),
                mod(0), mod(1), mod(2), mod(3), mod(4),
                full(w_gl.shape), full((1, w_gl.shape[1])),
                full(wpa.shape), full(wpc.shape), full(wpl.shape), full(wo.shape), full((1, D_MODEL)),
                full((1, D_MODEL)), full((1, D_MODEL)), full(rw_pad.shape), full(rb_pad.shape), full(tri.shape),
                full(upper.shape)]
    out_specs = [rows(D_MODEL), pl.BlockSpec((TOK_BLOCK, XS_WIDTH), lambda i: (i, 0)), rows(LANES),
                 pl.BlockSpec((SUBLANES, LANES), lambda i: (i, 0))]
    return pl.pallas_call(
        _post_kernel,
        grid=(n_tiles,),
        in_specs=in_specs, out_specs=out_specs,
        out_shape=[jax.ShapeDtypeStruct((n_rows, D_MODEL), F32),
                   jax.ShapeDtypeStruct((n_tiles * TOK_BLOCK, XS_WIDTH), F32),
                   jax.ShapeDtypeStruct((n_rows, LANES), F32),
                   jax.ShapeDtypeStruct((n_tiles * SUBLANES, LANES), F32)],
        compiler_params=_params(1),
        name="post_mixer",
    )(h, att, zb, hf, hb, rest, mod3, mod3, mod3, mod3, mod3,
      w_gl, b_gl.reshape(1, w_gl.shape[1]),
      wpa, wpc, wpl, wo, b_o.reshape(1, D_MODEL), l1g.reshape(1, D_MODEL), l1b.reshape(1, D_MODEL),
      rw_pad, rb_pad, tri, upper)


def _expert_plan(cnt_out, n_tok_tiles, n_rows):
    tm = TM_EXPERT
    co = cnt_out.reshape(n_tok_tiles, SUBLANES, LANES)
    cnt = (co[:, 0, :N_EXPERTS].astype(jnp.int32) + SUBLANES - 1) // SUBLANES * SUBLANES
    off = co[:, 1, :N_EXPERTS].astype(jnp.int32)
    cum_end = jnp.cumsum(cnt, axis=0)
    cum = cum_end - cnt
    total = cum_end[-1]
    n_et = (total + tm - 1) // tm
    et_end = jnp.cumsum(n_et)
    n_act = et_end[-1:].astype(jnp.int32)
    n_tiles = -(-n_tok_tiles * TOK_BLOCK // tm) + N_EXPERTS
    j = jnp.arange(n_tiles, dtype=jnp.int32)
    tile_e = jnp.minimum(jnp.sum((et_end[None, :] <= j[:, None]).astype(jnp.int32), axis=1), N_EXPERTS - 1)
    row0 = (j - (et_end - n_et)[tile_e]) * tm
    n_rows_tile = jnp.clip(total[tile_e] - row0, 0, tm)
    cum_e = cum.T[tile_e]
    cum_end_e = cum_end.T[tile_e]
    off_e = off.T[tile_e]
    q = row0[:, None] + SUBLANES * jnp.arange(tm // SUBLANES, dtype=jnp.int32)[None, :]
    tok_tile = jnp.minimum(jnp.sum((cum_end_e.T[:, :, None] <= q[None, :, :]).astype(jnp.int32), axis=0),
                           n_tok_tiles - 1)
    src = (tok_tile * TOK_BLOCK + jnp.take_along_axis(off_e, tok_tile, axis=1)
           + q - jnp.take_along_axis(cum_e, tok_tile, axis=1))
    used = off[:, N_EXPERTS - 1] + cnt[:, N_EXPERTS - 1]
    first = jnp.concatenate([jnp.ones((1,), jnp.int32), (tile_e[1:] != tile_e[:-1]).astype(jnp.int32)])
    group = jnp.cumsum(first) - 1
    after = et_end[tile_e]
    next_e = jnp.where(after < n_act[0], tile_e[jnp.minimum(after, n_tiles - 1)], -1)
    return (tile_e.astype(jnp.int32), n_rows_tile.astype(jnp.int32), n_act, src.reshape(-1).astype(jnp.int32),
            used.astype(jnp.int32), first, group.astype(jnp.int32), next_e.astype(jnp.int32), n_tiles)


def _expert_kernel(tile_e, n_rows_t, n_act, src_t, used_t, first_t, group_t, next_t,
                   xs_hbm, wgu_hbm, bgu_ref, wdn_hbm, bdn_ref, ys_hbm,
                   xin, yout, wgu_f, wdn_f, wgu_s, wdn_s, zeros, sem_in, sem_out, sem_zero, sem_wgu, sem_wdn,
                   *, tm, tok_block, n_tok_tiles, layer):
    j = pl.program_id(0)
    na = n_act[0]
    chunks = tm // SUBLANES

    def weight_copies(e, wslot):
        return (pltpu.make_async_copy(wgu_hbm.at[layer, e], wgu_f.at[wslot], sem_wgu.at[wslot]),
                pltpu.make_async_copy(wdn_hbm.at[layer, e], wdn_f.at[wslot], sem_wdn.at[wslot]))

    def gather(src, dst, size, slot):
        return pltpu.make_async_copy(xs_hbm.at[pl.ds(src, size)], xin.at[slot, pl.ds(dst, size)], sem_in.at[slot])

    def scatter(src, dst, size, slot):
        return pltpu.make_async_copy(yout.at[slot, pl.ds(dst, size)], ys_hbm.at[pl.ds(src, size)], sem_out.at[slot])

    def start_chunks(jj, slot, copy):
        def one(c):
            src = pl.multiple_of(src_t[jj * chunks + c], SUBLANES)
            copy(src, pl.multiple_of(c * SUBLANES, SUBLANES), SUBLANES, slot).start()

        def body(c, carry):
            one(c)
            return carry

        def body_unrolled(g, carry):
            for u in range(SUBLANES):
                one(g * SUBLANES + u)
            return carry

        n = lax.shift_right_logical(n_rows_t[jj], 3)

        @pl.when(n == chunks)
        def _():
            lax.fori_loop(0, chunks // SUBLANES, body_unrolled, 0)

        @pl.when(n != chunks)
        def _():
            lax.fori_loop(0, n, body, 0)

    def wait_rows(jj, slot, copy):
        n = n_rows_t[jj]
        size = tm
        while size >= SUBLANES:
            @pl.when((n & size) != 0)
            def _(size=size):
                copy(0, 0, size, slot).wait()
            size //= 2

    slot = j % 2

    @pl.when(j == 0)
    def _():
        xin[...] = jnp.zeros_like(xin)
        start_chunks(0, 0, gather)
        zeros[...] = jnp.zeros_like(zeros)

        def clear_tail(i, copy_op):
            used = used_t[i]
            tail = tok_block - used
            for size in (256, 128, 64, 32, 16, 8):
                @pl.when((tail & size) != 0)
                def _():
                    at = pl.multiple_of(i * tok_block + used + (tail & ~(2 * size - 1)), SUBLANES)
                    copy_op(pltpu.make_async_copy(zeros.at[pl.ds(0, size)], ys_hbm.at[pl.ds(at, size)], sem_zero))

        def start_clear(i, c):
            clear_tail(i, lambda cp: cp.start())
            return c

        def wait_clear(i, c):
            clear_tail(i, lambda cp: cp.wait())
            return c

        lax.fori_loop(0, n_tok_tiles, start_clear, 0)
        lax.fori_loop(0, n_tok_tiles, wait_clear, 0)

    @pl.when(j + 1 < na)
    def _():
        start_chunks(j + 1, 1 - slot, gather)

    @pl.when(j < na)
    def _():
        e = tile_e[j]

        @pl.when(first_t[j] == 1)
        def _():
            wslot = group_t[j] % 2

            @pl.when(j == 0)
            def _():
                for cp in weight_copies(e, wslot):
                    cp.start()

            for cp in weight_copies(e, wslot):
                cp.wait()
            wgu_s[...] = wgu_f[wslot].astype(BF16)
            wdn_s[...] = wdn_f[wslot].astype(BF16)

            @pl.when(next_t[j] >= 0)
            def _():
                for cp in weight_copies(next_t[j], 1 - wslot):
                    cp.start()

        wait_rows(j, slot, gather)
        x = xin[slot]
        lane = lax.broadcasted_iota(jnp.int32, (tm, LANES), 1)
        p = jnp.sum(jnp.where(lane % N_EXPERTS == e, x[:, D_MODEL:], 0.0), axis=-1, keepdims=True)
        gu = jnp.dot(x[:, :D_MODEL].astype(BF16), wgu_s[...], preferred_element_type=F32) + bgu_ref[0, 0]
        f = gu.shape[1] // 2
        gate = jnp.minimum(gu[:, :f], SWIGLU_LIMIT)
        up = jnp.clip(gu[:, f:], -SWIGLU_LIMIT, SWIGLU_LIMIT)
        hid = (up + 1.0) * gate * jax.nn.sigmoid(SWIGLU_ALPHA * gate)
        y = jnp.dot(hid.astype(BF16), wdn_s[...], preferred_element_type=F32) + bdn_ref[0, 0]
        yout[slot] = y * p
        start_chunks(j, slot, scatter)

        @pl.when(j >= 1)
        def _():
            wait_rows(j - 1, 1 - slot, scatter)

        @pl.when(j == na - 1)
        def _():
            wait_rows(j, slot, scatter)


def _experts(xs, plan, layer, w_gu, b_gu, w_dn, b_dn):
    tile_e, n_rows_tile, n_act, src, used, first, group, next_e, n_tiles = plan
    tm = TM_EXPERT
    f2 = w_gu.shape[-1]

    def expert(j, te, nr, na, *_):
        return (layer, te[jnp.minimum(j, na[0] - 1)], 0, 0)

    in_specs = [pl.BlockSpec(memory_space=pl.ANY),
                pl.BlockSpec(memory_space=pl.ANY),
                pl.BlockSpec((1, 1, 1, f2), expert),
                pl.BlockSpec(memory_space=pl.ANY),
                pl.BlockSpec((1, 1, 1, D_MODEL), expert)]
    grid_spec = pltpu.PrefetchScalarGridSpec(
        num_scalar_prefetch=8, grid=(n_tiles,), in_specs=in_specs,
        out_specs=pl.BlockSpec(memory_space=pl.ANY),
        scratch_shapes=[pltpu.VMEM((2, tm, XS_WIDTH), F32), pltpu.VMEM((2, tm, D_MODEL), F32),
                        pltpu.VMEM((2, D_MODEL, f2), F32), pltpu.VMEM((2, f2 // 2, D_MODEL), F32),
                        pltpu.VMEM((D_MODEL, f2), BF16), pltpu.VMEM((f2 // 2, D_MODEL), BF16),
                        pltpu.VMEM((ZERO_ROWS, D_MODEL), F32),
                        pltpu.SemaphoreType.DMA((2,)), pltpu.SemaphoreType.DMA((2,)), pltpu.SemaphoreType.DMA(()),
                        pltpu.SemaphoreType.DMA((2,)), pltpu.SemaphoreType.DMA((2,))])
    return pl.pallas_call(
        functools.partial(_expert_kernel, tm=tm, tok_block=TOK_BLOCK, n_tok_tiles=xs.shape[0] // TOK_BLOCK,
                          layer=layer),
        grid_spec=grid_spec,
        out_shape=jax.ShapeDtypeStruct((xs.shape[0], D_MODEL), F32),
        compiler_params=_params(1),
        name="moe_experts",
    )(tile_e, n_rows_tile, n_act, src, used, first, group, next_e,
      xs, w_gu, b_gu.reshape(DEPTH, N_EXPERTS, 1, f2), w_dn, b_dn.reshape(DEPTH, N_EXPERTS, 1, D_MODEL))


def _combine_kernel(ys_ref, pos_ref, h1_ref, g2_ref, l2g_ref, l2b_ref, o_ref):
    tm = pos_ref.shape[0]
    n_sorted = ys_ref.shape[0]
    pos = pos_ref[...]
    col = lax.broadcasted_iota(jnp.int32, (tm, n_sorted), 1).astype(F32)
    sel = jnp.zeros((tm, n_sorted), F32)
    for kk in range(TOP_K):
        sel = sel + (col == pos[:, kk:kk + 1]).astype(F32)
    sel = sel.astype(BF16)
    ys = ys_ref[...]
    hi = ys.astype(BF16)
    rest = ys - hi.astype(F32)
    mid = rest.astype(BF16)
    lo = (rest - mid.astype(F32)).astype(BF16)
    y2 = (jnp.dot(sel, hi, preferred_element_type=F32) + jnp.dot(sel, mid, preferred_element_type=F32)
          + jnp.dot(sel, lo, preferred_element_type=F32))
    o_ref[...] = _layer_norm(DEEPNORM_ALPHA * h1_ref[...] + g2_ref[0] * y2) * l2g_ref[...] + l2b_ref[...]


def _combine(ys, pos4, h1, mod3, mod_base, l2g, l2b, n_lat, seq):
    n_rows = h1.shape[0]
    tm = TM_POST
    tiles_per_seq = seq // tm
    n_lat_tiles = n_lat // tm
    n_groups_lat = n_lat // seq

    def group(i):
        return jnp.where(i < n_lat_tiles, i // tiles_per_seq, n_groups_lat)

    in_specs = [pl.BlockSpec((TOK_BLOCK, D_MODEL), lambda i: (i, 0)),
                pl.BlockSpec((tm, LANES), lambda i: (i, 0)),
                pl.BlockSpec((tm, D_MODEL), lambda i: (i, 0)),
                pl.BlockSpec((1, 1, D_MODEL), lambda i: (mod_base + group(i) * 6 + 5, 0, 0)),
                pl.BlockSpec((1, D_MODEL), lambda i: (0, 0)),
                pl.BlockSpec((1, D_MODEL), lambda i: (0, 0))]
    return pl.pallas_call(
        _combine_kernel,
        grid=(n_rows // tm,),
        in_specs=in_specs,
        out_specs=pl.BlockSpec((tm, D_MODEL), lambda i: (i, 0)),
        out_shape=jax.ShapeDtypeStruct((n_rows, D_MODEL), F32),
        compiler_params=_params(1),
        name="moe_combine",
    )(ys, pos4, h1, mod3, l2g.reshape(1, D_MODEL), l2b.reshape(1, D_MODEL))


def _block_diag(w):
    two, n, d, e = w.shape
    eye = jnp.eye(n, dtype=w.dtype)
    return (w[:, :, :, None, :] * eye[None, :, None, :, None]).reshape(two, n * d, n * e)


def kernel(x, c, ctx, c_ctx, w_mod, b_mod, w_in, b_in, na_rpb, w_proj_attn, w_proj_conv, w_proj_lru, sc_conv_w, lru_conv_w, lru_conv_b, lru_lambda, lru_w_r, lru_b_r, lru_w_i, lru_b_i, w_o, b_o, ln1_g, ln1_b, router_w, router_b, exp_w_gu, exp_b_gu, exp_w_dn, exp_b_dn, ln2_g, ln2_b):
    n_batch, seq, d = x.shape
    n_ctx = ctx.shape[1]
    n_lat = n_batch * seq
    n_all = n_lat + n_batch * n_ctx
    assert d == D_MODEL and n_batch + 1 <= SUBLANES

    cc = jnp.concatenate([c, c_ctx[None], jnp.zeros((SUBLANES - n_batch - 1, d), F32)], axis=0)
    mod = _modulation(cc, w_mod, b_mod)
    groups = n_batch + 1
    mod3 = mod.reshape(DEPTH, SUBLANES, 6, d)[:, :groups].reshape(DEPTH * groups * 6, 1, d)

    cos_t, sin_t = _make_rope(seq, TM_INPROJ)
    h = jnp.concatenate([x.reshape(n_lat, d), ctx.reshape(n_batch * n_ctx, d)], axis=0)

    for layer in range(DEPTH):
        last = layer == DEPTH - 1
        mod_base = layer * groups * 6
        w_in_bf = w_in[layer].astype(BF16)
        q, k, v, sb, rest = _input_projection(h, mod3, mod_base, w_in_bf[:, :N_EARLY], b_in[layer, :N_EARLY],
                                              cos_t, sin_t, n_lat, seq)
        sp = jax.nn.softplus(-lru_lambda[layer])
        zb, hf, hb = _conv_scan(sb, rest, sc_conv_w[layer], lru_conv_w[layer], lru_conv_b[layer], sp,
                                _block_diag(lru_w_r[layer]).astype(BF16), _block_diag(lru_w_i[layer]).astype(BF16),
                                lru_b_r[layer], lru_b_i[layer], n_batch, seq, n_ctx)
        att = _attention(q, k, v, _attention_bias(na_rpb[layer], seq), n_batch, seq, n_ctx, not last)
        n_rows = n_lat if last else n_all
        rw_full = jnp.pad(router_w[layer], ((0, 0), (0, LANES - N_EXPERTS)))
        rw_hi = rw_full.astype(BF16)
        rw_pad = jnp.concatenate([rw_hi, (rw_full - rw_hi.astype(F32)).astype(BF16)], axis=1)
        rb_pad = jnp.concatenate([router_b[layer], jnp.full((LANES - N_EXPERTS,), NEG_BIG, F32)]).reshape(1, LANES)
        h1, xs, pos4, cnt_out = _post_mixer(
            h, att, zb, hf, hb, rest, mod3, mod_base, w_in_bf[:, N_EARLY:], b_in[layer, N_EARLY:],
            w_proj_attn[layer].astype(BF16), w_proj_conv[layer].astype(BF16), w_proj_lru[layer].astype(BF16),
            w_o[layer].astype(BF16), b_o[layer], ln1_g[layer], ln1_b[layer], rw_pad, rb_pad, n_rows, n_lat, seq)
        plan = _expert_plan(cnt_out, n_rows // TM_POST, n_rows)
        ys = _experts(xs, plan, layer, exp_w_gu, exp_b_gu, exp_w_dn, exp_b_dn)
        h = _combine(ys, pos4, h1, mod3, mod_base, ln2_g[layer], ln2_b[layer], n_lat, seq)
    return h.reshape(n_batch, seq, d)
```

```python
import functools

import numpy as np
import jax
import jax.numpy as jnp
from jax import lax
from jax.experimental import pallas as pl
from jax.experimental.pallas import tpu as pltpu

D_MODEL = 1024
DEPTH = 2
GRID_W = 64
NA_HEADS = 8
NA_HEAD_DIM = 64
NA_WIDTH = NA_HEADS * NA_HEAD_DIM
NA_WIN_ROWS = 8
NA_WIN_COLS = 16
ROPE_BASE = 10000.0
BRANCH_WIDTH = 512
LRU_BLOCKS = 8
LRU_C = 8.0
P_TOTAL = 7168
N_EARLY = 8 * BRANCH_WIDTH
N_EXPERTS = 32
TOP_K = 4
SWIGLU_LIMIT = 7.0
SWIGLU_ALPHA = 1.702
LN_EPS = 1e-5
DEEPNORM_ALPHA = (2 * DEPTH) ** 0.25
NEG_BIG = -1e30

LANES = 128
SUBLANES = 8
VMEM_LIMIT_BYTES = 56 * 1024 * 1024

TM_INPROJ = 1024
TN_INPROJ = 1024
SCAN_CHUNK = 256
ATT_ROWS = 4
ATT_QROWS = ATT_ROWS * GRID_W
TM_POST = 256
TM_EXPERT = 512
XS_WIDTH = D_MODEL + LANES
TOK_BLOCK = TM_POST * TOP_K + N_EXPERTS * SUBLANES
ZERO_ROWS = 256

F32 = jnp.float32
BF16 = jnp.bfloat16


def _params(n_axes):
    return pltpu.CompilerParams(dimension_semantics=("arbitrary",) * n_axes,
                                vmem_limit_bytes=VMEM_LIMIT_BYTES)


def _layer_norm(x):
    mu = jnp.mean(x, axis=-1, keepdims=True)
    xc = x - mu
    var = jnp.mean(xc * xc, axis=-1, keepdims=True)
    return xc * lax.rsqrt(var + LN_EPS)


def _mod_kernel(c_ref, w_ref, b_ref, o_ref):
    c = c_ref[...]
    s = (c * jax.nn.sigmoid(c)).astype(BF16)
    o_ref[0] = jnp.dot(s, w_ref[0].astype(BF16), preferred_element_type=F32) + b_ref[0]


def _modulation(cc, w_mod, b_mod):
    n_out = w_mod.shape[-1]
    return pl.pallas_call(
        _mod_kernel,
        grid=(DEPTH, n_out // D_MODEL),
        in_specs=[pl.BlockSpec((SUBLANES, D_MODEL), lambda l, j: (0, 0)),
                  pl.BlockSpec((1, D_MODEL, D_MODEL), lambda l, j: (l, 0, j)),
                  pl.BlockSpec((1, 1, D_MODEL), lambda l, j: (l, 0, j))],
        out_specs=pl.BlockSpec((1, SUBLANES, D_MODEL), lambda l, j: (l, 0, j)),
        out_shape=jax.ShapeDtypeStruct((DEPTH, SUBLANES, n_out), F32),
        compiler_params=_params(2),
        name="modulation",
    )(cc, w_mod, b_mod.reshape(DEPTH, 1, n_out))


def _rope_half(x, cos, sin_signed):
    lane = lax.broadcasted_iota(jnp.int32, (x.shape[0], LANES), 1)
    first = (lane % 32) < 16
    outs = []
    for cidx in range(x.shape[1] // LANES):
        xc = x[:, cidx * LANES:(cidx + 1) * LANES]
        partner = jnp.where(first, pltpu.roll(xc, LANES - 16, 1), pltpu.roll(xc, 16, 1))
        outs.append(xc * cos + partner * sin_signed)
    return jnp.concatenate(outs, axis=1)


def _inproj_kernel(ha_ref, hb_ref, sh_ref, sc_ref, w_ref, b_ref, cos_ref, sin_ref,
                   q_ref, k_ref, v_ref, sb_ref, rest_ref, xn_ref, *, split):
    i = pl.program_id(0)
    j = pl.program_id(1)

    def normalise(h_ref):
        y = _layer_norm(h_ref[...])
        xn_ref[...] = (y * (1.0 + sc_ref[0]) + sh_ref[0]).astype(BF16)

    @pl.when(jnp.logical_and(j == 0, i < split))
    def _():
        normalise(ha_ref)

    @pl.when(jnp.logical_and(j == 0, i >= split))
    def _():
        normalise(hb_ref)

    acc =jnp.dot(xn_ref[...], w_ref[...], preferred_element_type=F32) + b_ref[...]
    half = BRANCH_WIDTH

    @pl.when(j == 0)
    def _():
        cos = cos_ref[...]
        sin = sin_ref[...]
        q_ref[...] = _rope_half(acc[:, :half], cos, sin).astype(BF16)
        k_ref[...] = _rope_half(acc[:, half:], cos, sin).astype(BF16)

    @pl.when(j == 1)
    def _():
        v_ref[...] = acc[:, :half].astype(BF16)
        sb_ref[...] = acc[:, half:]

    @pl.when(j >= 2)
    def _():
        rest_ref[...] = acc


def _input_projection(h_parts, mod3, mod_base, w_in_bf, b_in, n_cols, cos_t, sin_t, n_lat, seq):
    tm, tn = TM_INPROJ, TN_INPROJ
    ha, hb = h_parts[0], h_parts[-1]
    split = ha.shape[0] // tm
    m = ha.shape[0] + (hb.shape[0] if len(h_parts) == 2 else 0)
    n_lat_tiles = n_lat // tm
    tiles_per_seq = seq // tm
    n_groups_lat = n_lat // seq

    def group(i):
        return jnp.where(i < n_lat_tiles, i // tiles_per_seq, n_groups_lat)

    def rope_blk(i):
        return jnp.where(i < n_lat_tiles, i % tiles_per_seq, tiles_per_seq)

    half = BRANCH_WIDTH
    return pl.pallas_call(
        functools.partial(_inproj_kernel, split=split),
        grid=(m // tm, n_cols // tn),
        in_specs=[pl.BlockSpec((tm, D_MODEL), lambda i, j: (jnp.minimum(i, split - 1), 0)),
                  pl.BlockSpec((tm, D_MODEL), lambda i, j: (jnp.maximum(i - split, 0), 0)),
                  pl.BlockSpec((1, 1, D_MODEL), lambda i, j: (mod_base + group(i) * 6 + 0, 0, 0)),
                  pl.BlockSpec((1, 1, D_MODEL), lambda i, j: (mod_base + group(i) * 6 + 1, 0, 0)),
                  pl.BlockSpec((D_MODEL, tn), lambda i, j: (0, j)),
                  pl.BlockSpec((1, tn), lambda i, j: (0, j)),
                  pl.BlockSpec((tm, LANES), lambda i, j: (rope_blk(i), 0)),
                  pl.BlockSpec((tm, LANES), lambda i, j: (rope_blk(i), 0))],
        out_specs=[pl.BlockSpec((tm, half), lambda i, j: (i, 0)),
                   pl.BlockSpec((tm, half), lambda i, j: (i, 0)),
                   pl.BlockSpec((tm, half), lambda i, j: (i, 0)),
                   pl.BlockSpec((tm, half), lambda i, j: (i, 0)),
                   pl.BlockSpec((tm, tn), lambda i, j: (i, jnp.maximum(j - 2, 0)))],
        out_shape=[jax.ShapeDtypeStruct((m, half), BF16),
                   jax.ShapeDtypeStruct((m, half), BF16),
                   jax.ShapeDtypeStruct((m, half), BF16),
                   jax.ShapeDtypeStruct((m, half), F32),
                   jax.ShapeDtypeStruct((m, n_cols - 2 * tn), F32)],
        scratch_shapes=[pltpu.VMEM((tm, D_MODEL), BF16)],
        compiler_params=_params(2),
        name="input_projection",
    )(ha, hb, mod3, mod3, w_in_bf, b_in.reshape(1, -1), cos_t, sin_t)


def _make_rope(seq, tm):
    t = np.arange(seq)
    row_pos, col_pos = t // GRID_W, t % GRID_W
    d = np.arange(LANES) % NA_HEAD_DIM
    m = NA_HEAD_DIM // 4
    inv_freq = (ROPE_BASE ** (-jnp.arange(m, dtype=F32) / m))[d % m]
    pos = np.where((d < 2 * m)[None, :], row_pos[:, None], col_pos[:, None])
    ang = jnp.asarray(pos).astype(F32) * inv_freq[None, :]
    cos = jnp.cos(ang)
    sin = jnp.sin(ang)
    sin_signed = jnp.where(jnp.asarray((d % (2 * m)) < m)[None, :], -sin, sin)
    cos = jnp.concatenate([cos, jnp.ones((tm, LANES), F32)], axis=0)
    sin_signed = jnp.concatenate([sin_signed, jnp.zeros((tm, LANES), F32)], axis=0)
    return cos, sin_signed


def _scan_kernel(fblk, bblk, first, last, seqb,
                 sb_ref, scg_ref, sx_ref, lxf_ref,
                 scgp_ref, sxp_ref, lxfp_ref, scgn_ref, sxn_ref, lxfn_ref,
                 lxb_ref, lxbp_ref, lxbn_ref,
                 scw_ref, cw_ref, cb_ref, sp_ref, wr_ref, wi_ref, br_ref, bi_ref,
                 zb_ref, hf_ref, hb_ref,
                 a_s, b_s, hc_s, st_s, *, ch, n_ctx_items):
    it = pl.program_id(0)
    is_first = first[it] == 1
    is_last = last[it] == 1
    is_ctx = it < n_ctx_items
    b = seqb[it]
    width = BRANCH_WIDTH
    row = lax.broadcasted_iota(jnp.int32, (ch, width), 0)
    not_first = jnp.where(is_first, 0.0, 1.0).astype(F32)
    not_last = jnp.where(is_last, 0.0, 1.0).astype(F32)

    def back1(u, prev_row):
        return jnp.where(row == 0, prev_row, pltpu.roll(u, 1, 0))

    def back2(u, prev2, prev1):
        return jnp.where(row == 0, prev2, jnp.where(row == 1, prev1, pltpu.roll(u, 2, 0)))

    def fwd1(u, next_row):
        return jnp.where(row == ch - 1, next_row, pltpu.roll(u, ch - 1, 0))

    def lru_input(lx_ref, lxp_ref, lxn_ref, prev_ok, next_ok):
        x = lx_ref[...]
        p = lxp_ref[...] * prev_ok
        n = lxn_ref[...] * next_ok
        return (cw_ref[0:1] * back2(x, p[6:7], p[7:8]) + cw_ref[1:2] * back1(x, p[7:8])
                + cw_ref[2:3] * x + cw_ref[3:4] * fwd1(x, n[0:1]) + cb_ref[...])

    def coeffs(d, xm):
        xb = xm.astype(BF16)
        r = jax.nn.sigmoid(jnp.dot(xb, wr_ref[d], preferred_element_type=F32) + br_ref[d:d + 1])
        g = jax.nn.sigmoid(jnp.dot(xb, wi_ref[d], preferred_element_type=F32) + bi_ref[d:d + 1])
        log_a = (-LRU_C * sp_ref[d:d + 1]) * r
        a = jnp.exp(log_a)
        a_s[d] = a
        b_s[d] = jnp.sqrt(-jnp.tanh(log_a) * (a * a + 1.0)) * (g * xm)

    u = scg_ref[...] * sx_ref[...]
    u_prev = scgp_ref[7:8] * sxp_ref[7:8] * not_first
    u_next = scgn_ref[0:1] * sxn_ref[0:1] * not_last
    conv = scw_ref[0:1] * back1(u, u_prev) + scw_ref[1:2] * u + scw_ref[2:3] * fwd1(u, u_next)
    zb_ref[...] = (sb_ref[...] * conv).astype(BF16)

    coeffs(0, lru_input(lxf_ref, lxfp_ref, lxfn_ref, not_first, not_last))
    coeffs(1, lru_input(lxb_ref, lxbp_ref, lxbn_ref, not_last, not_first))

    @pl.when(jnp.logical_and(is_first, is_ctx))
    def _():
        hc_s[...] = jnp.zeros_like(hc_s)

    @pl.when(jnp.logical_and(is_first, jnp.logical_not(is_ctx)))
    def _():
        hc_s[0:1] = st_s[pl.ds(2 * b, 1), :]
        hc_s[1:2] = st_s[pl.ds(2 * b + 1, 1), :]

    def body(i, carry):
        hf, hb = carry
        base = i * SUBLANES
        for r in range(SUBLANES):
            t = base + r
            hf = a_s[0, pl.ds(t, 1), :] * hf + b_s[0, pl.ds(t, 1), :]
            hf_ref[pl.ds(t, 1), :] = hf
            tb = ch - 1 - t
            hb = a_s[1, pl.ds(tb, 1), :] * hb + b_s[1, pl.ds(tb, 1), :]
            hb_ref[pl.ds(tb, 1), :] = hb
        return hf, hb

    hf, hb = lax.fori_loop(0, ch // SUBLANES, body, (hc_s[0:1], hc_s[1:2]))
    hc_s[0:1] = hf
    hc_s[1:2] = hb

    @pl.when(is_ctx)
    def _():
        st_s[pl.ds(2 * b, 1), :] = hf
        st_s[pl.ds(2 * b + 1, 1), :] = hb


def _scan_tables(n_batch, seq, n_ctx, ch):
    assert n_ctx == ch
    nc = seq // ch
    ctx0 = n_batch * seq // ch
    fblk, bblk, first, last, seqb = [], [], [], [], []
    for b in range(n_batch):
        fblk.append(ctx0 + b); bblk.append(ctx0 + b); first.append(1); last.append(1); seqb.append(b)
    for b in range(n_batch):
        for c in range(nc):
            fblk.append(b * nc + c); bblk.append(b * nc + nc - 1 - c)
            first.append(int(c == 0)); last.append(int(c == nc - 1)); seqb.append(b)
    return [np.asarray(a, np.int32) for a in (fblk, bblk, first, last, seqb)]


def _conv_scan(sb, rest, sc_w, lru_cw, lru_cb, sp, wr_bd, wi_bd, b_r, b_i, n_batch, seq, n_ctx):
    m = sb.shape[0]
    ch = SCAN_CHUNK
    width = BRANCH_WIDTH
    tables = _scan_tables(n_batch, seq, n_ctx, ch)
    n_items = len(tables[0])
    halo_per_chunk = ch // SUBLANES
    last_halo = m // SUBLANES - 1

    def cur(col, which):
        return pl.BlockSpec((ch, width), lambda i, f, bk, *_: ((f, bk)[which][i], col))

    def prev(col, which):
        return pl.BlockSpec((SUBLANES, width),
                            lambda i, f, bk, *_: (jnp.maximum((f, bk)[which][i] * halo_per_chunk - 1, 0), col))

    def nxt(col, which):
        return pl.BlockSpec((SUBLANES, width),
                            lambda i, f, bk, *_: (jnp.minimum(((f, bk)[which][i] + 1) * halo_per_chunk, last_halo), col))

    def full(shape):
        return pl.BlockSpec(shape, lambda i, *_: (0,) * len(shape))

    in_specs = [cur(0, 0), cur(0, 0), cur(1, 0), cur(2, 0),
                prev(0, 0), prev(1, 0), prev(2, 0), nxt(0, 0), nxt(1, 0), nxt(2, 0),
                cur(2, 1), prev(2, 1), nxt(2, 1),
                full(sc_w.shape), full(lru_cw.shape), full((1, width)), full(sp.shape),
                full(wr_bd.shape), full(wi_bd.shape), full(b_r.shape), full(b_i.shape)]
    out_specs = [cur(0, 0), cur(0, 0), cur(0, 1)]
    grid_spec = pltpu.PrefetchScalarGridSpec(
        num_scalar_prefetch=5, grid=(n_items,), in_specs=in_specs, out_specs=out_specs,
        scratch_shapes=[pltpu.VMEM((2, ch, width), F32), pltpu.VMEM((2, ch, width), F32),
                        pltpu.VMEM((SUBLANES, width), F32), pltpu.VMEM((2 * n_batch, width), F32)])
    return pl.pallas_call(
        functools.partial(_scan_kernel, ch=ch, n_ctx_items=n_batch),
        grid_spec=grid_spec,
        out_shape=[jax.ShapeDtypeStruct((m, width), BF16),
                   jax.ShapeDtypeStruct((m, width), F32),
                   jax.ShapeDtypeStruct((m, width), F32)],
        compiler_params=_params(1),
        name="conv_scan",
    )(*[jnp.asarray(t) for t in tables],
      sb, rest, rest, rest, rest, rest, rest, rest, rest, rest, rest, rest, rest,
      sc_w, lru_cw, lru_cb.reshape(1, width), sp, wr_bd, wi_bd, b_r, b_i)


def _attention_kernel(qblk, bat, r0t, cls, q_ref, k_ref, v_ref, kc_ref, vc_ref, *rest, band):
    bias_refs, o_ref = rest[:ATT_ROWS], rest[ATT_ROWS]
    it = pl.program_id(0)
    nq = GRID_W
    lane = lax.broadcasted_iota(jnp.int32, (nq, LANES), 1)
    low = lane < NA_HEAD_DIM
    scale = NA_HEAD_DIM ** -0.5
    nt = (((1,), (1,)), ((), ()))
    for row in range(ATT_ROWS):
        start = pl.multiple_of(r0t[it * ATT_ROWS + row] * GRID_W, GRID_W)
        bias_ref = bias_refs[row]
        qrows = slice(row * nq, (row + 1) * nq)
        for hp in range(NA_HEADS // 2):
            cols = slice(hp * LANES, (hp + 1) * LANES)
            qp = q_ref[qrows, cols].astype(F32) * scale
            qs = jnp.concatenate([jnp.where(low, qp, 0.0), jnp.where(low, 0.0, qp)], axis=0).astype(BF16)
            kb = k_ref[pl.ds(start, band), cols]
            vb = v_ref[pl.ds(start, band), cols]
            s_loc = lax.dot_general(qs, kb, nt, preferred_element_type=F32)
            s_ctx = lax.dot_general(qs, kc_ref[:, cols], nt, preferred_element_type=F32)
            bias = jnp.concatenate([bias_ref[0, 2 * hp], bias_ref[0, 2 * hp + 1]], axis=0)
            s_loc = s_loc + bias
            mx =jnp.maximum(jnp.max(s_loc, axis=-1, keepdims=True), jnp.max(s_ctx, axis=-1, keepdims=True))
            e_loc = jnp.exp(s_loc - mx)
            e_ctx = jnp.exp(s_ctx - mx)
            den = jnp.sum(e_loc, axis=-1, keepdims=True) + jnp.sum(e_ctx, axis=-1, keepdims=True)
            o = (jnp.dot(e_loc.astype(BF16), vb, preferred_element_type=F32)
                 + jnp.dot(e_ctx.astype(BF16), vc_ref[:, cols], preferred_element_type=F32)) / den
            o_ref[qrows, cols] = jnp.where(low, o[:nq], o[nq:]).astype(BF16)


def _attention_tables(n_batch, seq, n_ctx, with_ctx_queries):
    rows = seq // GRID_W
    kr = min(NA_WIN_ROWS, rows)
    assert rows % ATT_ROWS == 0 and n_ctx % ATT_QROWS == 0
    qblk, bat, r0t, cls = [], [], [], []
    ctx_q0 = n_batch * seq // ATT_QROWS
    for b in range(n_batch):
        for rg in range(rows // ATT_ROWS):
            qblk.append(b * (rows // ATT_ROWS) + rg); bat.append(b)
            for r in range(rg * ATT_ROWS, (rg + 1) * ATT_ROWS):
                r0 = min(max(r - kr // 2, 0), rows - kr)
                r0t.append(r0); cls.append(r - r0)
        if with_ctx_queries:
            for c in range(n_ctx // ATT_QROWS):
                qblk.append(ctx_q0 + b * (n_ctx // ATT_QROWS) + c); bat.append(b)
                r0t.extend([0] * ATT_ROWS); cls.extend([kr] * ATT_ROWS)
    return [np.asarray(a, np.int32) for a in (qblk, bat, r0t, cls)]


def _attention_bias(rpb, seq):
    rows = seq // GRID_W
    kr = min(NA_WIN_ROWS, rows)
    kc = NA_WIN_COLS
    cq = np.arange(GRID_W)
    c0 = np.clip(cq - kc // 2, 0, GRID_W - kc)
    ck = np.arange(GRID_W)
    inside = (ck[None, :] >= c0[:, None]) & (ck[None, :] < c0[:, None] + kc)
    dc = np.clip(ck[None, :] - cq[:, None] + (NA_WIN_COLS - 1), 0, 2 * NA_WIN_COLS - 2)
    n_dr = 2 * NA_WIN_ROWS - 1
    n_dc = 2 * NA_WIN_COLS - 1
    pick = jnp.asarray((np.arange(n_dc)[:, None] == dc.reshape(1, -1)).astype(np.float32))
    picked = jnp.dot(rpb.reshape(-1, n_dc), pick, precision=lax.Precision.HIGHEST)
    picked = picked.reshape(NA_HEADS, 2 * NA_WIN_ROWS - 1, GRID_W, GRID_W)
    table = jnp.where(jnp.asarray(inside)[None, None], picked, NEG_BIG)
    table = table.transpose(0, 2, 1, 3).reshape(NA_HEADS, GRID_W, n_dr * GRID_W)
    classes = []
    for cl in range(kr):
        lo = (NA_WIN_ROWS - 1 - cl) * GRID_W
        classes.append(table[:, :, lo:lo + kr * GRID_W])
    classes.append(jnp.full((NA_HEADS, GRID_W, kr * GRID_W), NEG_BIG, F32))
    return jnp.stack(classes, axis=0)


def _attention(q, k, v, bias, n_batch, seq, n_ctx, with_ctx_queries):
    m = q.shape[0] if with_ctx_queries else n_batch * seq
    rows = seq // GRID_W
    kr = min(NA_WIN_ROWS, rows)
    band = kr * GRID_W
    tables = _attention_tables(n_batch, seq, n_ctx, with_ctx_queries)
    n_items = len(tables[0])
    ctx_blk0 = n_batch * seq // n_ctx
    width = NA_WIDTH
    in_specs = [pl.BlockSpec((ATT_QROWS, width), lambda i, qb, bt, r0, cl: (qb[i], 0)),
                pl.BlockSpec((seq, width), lambda i, qb, bt, r0, cl: (bt[i], 0)),
                pl.BlockSpec((seq, width), lambda i, qb, bt, r0, cl: (bt[i], 0)),
                pl.BlockSpec((n_ctx, width), lambda i, qb, bt, r0, cl: (ctx_blk0 + bt[i], 0)),
                pl.BlockSpec((n_ctx, width), lambda i, qb, bt, r0, cl: (ctx_blk0 + bt[i], 0))]
    for row in range(ATT_ROWS):
        in_specs.append(pl.BlockSpec((1, NA_HEADS, GRID_W, band),
                                     lambda i, qb, bt, r0, cl, row=row: (cl[i * ATT_ROWS + row], 0, 0, 0)))
    out_specs = pl.BlockSpec((ATT_QROWS, width), lambda i, qb, bt, r0, cl: (qb[i], 0))
    grid_spec = pltpu.PrefetchScalarGridSpec(num_scalar_prefetch=4, grid=(n_items,),
                                             in_specs=in_specs, out_specs=out_specs)
    return pl.pallas_call(
        functools.partial(_attention_kernel, band=band),
        grid_spec=grid_spec,
        out_shape=jax.ShapeDtypeStruct((m, width), BF16),
        compiler_params=_params(1),
        name="attention",
    )(*[jnp.asarray(t) for t in tables], q, k, v, k, v, *([bias] * ATT_ROWS))


def _post_kernel(ha_ref, hc_ref, att_ref, zb_ref, hf_ref, hb_ref, lg_ref,
                 sh1_ref, sc1_ref, g1_ref, sh2_ref, sc2_ref,
                 wgl0_ref, wgl1_ref, wgl2_ref, bgl0_ref, bgl1_ref, bgl2_ref,
                 wpa_ref, wpc_ref, wpl_ref, wo_ref, bo_ref, l1g_ref, l1b_ref,
                 rw_ref, rb_ref, tri_ref, upper_ref,
                 h1_ref, xs_ref, pos_ref, cnt_ref, *, split):
    h = jnp.where(pl.program_id(0) < split, ha_ref[...], hc_ref[...])
    u1 = (_layer_norm(h) * (1.0 + sc1_ref[0]) + sh1_ref[0]).astype(BF16)
    y_a = jnp.dot(att_ref[...], wpa_ref[...], preferred_element_type=F32)
    y_b = jnp.dot(zb_ref[...], wpc_ref[...], preferred_element_type=F32)
    zc = jax.nn.gelu(lg_ref[...]) * (hf_ref[...] + hb_ref[...])
    y_c = jnp.dot(zc.astype(BF16), wpl_ref[...], preferred_element_type=F32)
    merged = (jax.nn.sigmoid(jnp.dot(u1, wgl0_ref[...], preferred_element_type=F32) + bgl0_ref[...]) * y_a
              + jax.nn.sigmoid(jnp.dot(u1, wgl1_ref[...], preferred_element_type=F32) + bgl1_ref[...]) * y_b
              + jax.nn.sigmoid(jnp.dot(u1, wgl2_ref[...], preferred_element_type=F32) + bgl2_ref[...]) * y_c)
    y = jnp.dot(merged.astype(BF16), wo_ref[...], preferred_element_type=F32) + bo_ref[...]
    h1 = _layer_norm(DEEPNORM_ALPHA * h + g1_ref[0] * y) * l1g_ref[...] + l1b_ref[...]
    h1_ref[...] = h1
    u2 = _layer_norm(h1) * (1.0 + sc2_ref[0]) + sh2_ref[0]

    u_hi = u2.astype(BF16)
    u_lo = (u2 - u_hi.astype(F32)).astype(BF16)
    by_hi = jnp.dot(u_hi, rw_ref[...], preferred_element_type=F32)
    logits = (by_hi[:, :LANES] + by_hi[:, LANES:]
              + jnp.dot(u_lo, rw_ref[:, :LANES], preferred_element_type=F32) + rb_ref[...])
    tm = logits.shape[0]
    lane = lax.broadcasted_iota(jnp.int32, (tm, LANES), 1)
    lane_f = lane.astype(F32)
    work = logits
    tops, idxs, hots = [], [], []
    for _ in range(TOP_K):
        mx = jnp.max(work, axis=-1, keepdims=True)
        idx = jnp.min(jnp.where(work == mx, lane_f, float(LANES)), axis=-1, keepdims=True)
        hot = lane_f == idx
        work = jnp.where(hot, -3e38, work)
        tops.append(mx); idxs.append(idx); hots.append(hot)
    exps = [jnp.exp(t - tops[0]) for t in tops]
    den = exps[0] + exps[1] + exps[2] + exps[3]
    hot_all = jnp.zeros((tm, LANES), F32)
    for hot in hots:
        hot_all = hot_all + hot.astype(F32)
    cnt = jnp.sum(hot_all, axis=0, keepdims=True)
    cnt_pad = jnp.floor((cnt + (SUBLANES - 1.0)) * (1.0 / SUBLANES)) * SUBLANES
    off = jnp.dot(jnp.broadcast_to(cnt_pad, (SUBLANES, LANES)), upper_ref[...],
                  precision=lax.Precision.HIGHEST, preferred_element_type=F32)[0:1]
    slot = off + jnp.dot(tri_ref[...], hot_all.astype(BF16), preferred_element_type=F32)
    pos4 = jnp.zeros((tm, LANES), F32)
    w_tile = jnp.zeros((tm, LANES), F32)
    for kk in range(TOP_K):
        pos_k = jnp.sum(jnp.where(hots[kk], slot, 0.0), axis=-1, keepdims=True)
        pos4 = jnp.where(lane == kk, pos_k, pos4)
        p = exps[kk] / den
        p_hi = p.astype(BF16).astype(F32)
        p_mid = (p - p_hi).astype(BF16).astype(F32)
        p_lo = p - p_hi - p_mid
        w_tile = jnp.where(hots[kk], p_hi, w_tile)
        w_tile = jnp.where(lane_f == idxs[kk] + float(N_EXPERTS), p_mid, w_tile)
        w_tile = jnp.where(lane_f == idxs[kk] + float(2 * N_EXPERTS), p_lo, w_tile)
    pos_ref[...] = pos4
    sub = lax.broadcasted_iota(jnp.int32, (SUBLANES, LANES), 0)
    cnt_ref[...] = jnp.where(sub == 0, cnt, jnp.where(sub == 1, off, 0.0))

    n_sorted = xs_ref.shape[0]
    pos_t = pos4.T
    r_iota = lax.broadcasted_iota(jnp.int32, (n_sorted, tm), 0).astype(F32)
    hit = r_iota == pos_t[0:1, :]
    for kk in range(1, TOP_K):
        hit = jnp.logical_or(hit, r_iota == pos_t[kk:kk + 1, :])
    perm = jnp.where(hit, 1.0, 0.0).astype(BF16)
    feats = jnp.concatenate([u_hi, w_tile.astype(BF16)], axis=1)
    xs_ref[...] = jnp.dot(perm, feats, preferred_element_type=F32)


def _post_mixer(h_parts, att, zb, hf, hb, rest, mod3, mod_base, w_in_bf, b_in, wpa, wpc, wpl, wo, b_o, l1g, l1b,
                rw_pad, rb_pad, n_rows, n_lat, seq):
    tm = TM_POST
    width = BRANCH_WIDTH
    ha, hc = h_parts[0], h_parts[-1]
    split = min(ha.shape[0], n_rows) // tm
    gate_col0 = N_EARLY // D_MODEL
    tiles_per_seq = seq // tm
    n_lat_tiles = n_lat // tm
    n_groups_lat = n_lat // seq
    tri = jnp.asarray(np.tril(np.ones((tm, tm), np.float32), -1), BF16)
    upper = jnp.asarray(np.triu(np.ones((LANES, LANES), np.float32), 1))
    n_tiles = n_rows // tm

    def group(i):
        return jnp.where(i < n_lat_tiles, i // tiles_per_seq, n_groups_lat)

    def rows(wd, col=0):
        return pl.BlockSpec((tm, wd), lambda i: (i, col))

    def full(shape):
        return pl.BlockSpec(shape, lambda i: (0,) * len(shape))

    def mod(which):
        return pl.BlockSpec((1, 1, D_MODEL), lambda i: (mod_base + group(i) * 6 + which, 0, 0))

    in_specs = [pl.BlockSpec((tm, D_MODEL), lambda i: (jnp.minimum(i, split - 1), 0)),
                pl.BlockSpec((tm, D_MODEL), lambda i: (jnp.maximum(i - split, 0), 0)),
                rows(width), rows(width), rows(width), rows(width), rows(width, 3),
                mod(0), mod(1), mod(2), mod(3), mod(4)]
    in_specs += [pl.BlockSpec((D_MODEL, D_MODEL), lambda i, c=c: (0, gate_col0 + c)) for c in range(3)]
    in_specs += [pl.BlockSpec((1, D_MODEL), lambda i, c=c: (0, gate_col0 + c)) for c in range(3)]
    in_specs += [full(wpa.shape), full(wpc.shape), full(wpl.shape), full(wo.shape), full((1, D_MODEL)),
                full((1, D_MODEL)), full((1, D_MODEL)), full(rw_pad.shape), full(rb_pad.shape), full(tri.shape),
                full(upper.shape)]
    out_specs = [rows(D_MODEL), pl.BlockSpec((TOK_BLOCK, XS_WIDTH), lambda i: (i, 0)), rows(LANES),
                 pl.BlockSpec((SUBLANES, LANES), lambda i: (i, 0))]
    return pl.pallas_call(
        functools.partial(_post_kernel, split=split),
        grid=(n_tiles,),
        in_specs=in_specs, out_specs=out_specs,
        out_shape=[jax.ShapeDtypeStruct((n_rows, D_MODEL), F32),
                   jax.ShapeDtypeStruct((n_tiles * TOK_BLOCK, XS_WIDTH), F32),
                   jax.ShapeDtypeStruct((n_rows, LANES), F32),
                   jax.ShapeDtypeStruct((n_tiles * SUBLANES, LANES), F32)],
        compiler_params=_params(1),
        name="post_mixer",
    )(ha, hc, att, zb, hf, hb, rest, mod3, mod3, mod3, mod3, mod3,
      w_in_bf, w_in_bf, w_in_bf, b_in.reshape(1, -1), b_in.reshape(1, -1), b_in.reshape(1, -1),
      wpa, wpc, wpl, wo, b_o.reshape(1, D_MODEL), l1g.reshape(1, D_MODEL), l1b.reshape(1, D_MODEL),
      rw_pad, rb_pad, tri, upper)


def _expert_plan(cnt_out, n_tok_tiles, n_rows):
    tm = TM_EXPERT
    co = cnt_out.reshape(n_tok_tiles, SUBLANES, LANES)
    cnt = (co[:, 0, :N_EXPERTS].astype(jnp.int32) + SUBLANES - 1) // SUBLANES * SUBLANES
    off = co[:, 1, :N_EXPERTS].astype(jnp.int32)
    cum_end = jnp.cumsum(cnt, axis=0)
    cum = cum_end - cnt
    total = cum_end[-1]
    n_et = (total + tm - 1) // tm
    et_end = jnp.cumsum(n_et)
    n_act = et_end[-1:].astype(jnp.int32)
    n_tiles = -(-n_tok_tiles * TOK_BLOCK // tm) + N_EXPERTS
    j = jnp.arange(n_tiles, dtype=jnp.int32)
    tile_e = jnp.minimum(jnp.sum((et_end[None, :] <= j[:, None]).astype(jnp.int32), axis=1), N_EXPERTS - 1)
    pick_e = (tile_e[:, None] == jnp.arange(N_EXPERTS, dtype=jnp.int32)[None, :]).astype(F32)

    def per_tile(table):
        return jnp.dot(pick_e, table.astype(F32), precision=lax.Precision.HIGHEST).astype(jnp.int32)

    row0 = (j - per_tile(et_end - n_et)) * tm
    n_rows_tile = jnp.clip(per_tile(total) - row0, 0, tm)
    cum_e = per_tile(cum.T)
    cum_end_e = per_tile(cum_end.T)
    delta_e = per_tile((jnp.arange(n_tok_tiles, dtype=jnp.int32)[:, None] * TOK_BLOCK + off - cum).T)
    q = row0[:, None] + SUBLANES * jnp.arange(tm // SUBLANES, dtype=jnp.int32)[None, :]
    inside = jnp.logical_and(cum_e.T[:, :, None] <= q[None], q[None] < cum_end_e.T[:, :, None])
    src = q + jnp.sum(jnp.where(inside, delta_e.T[:, :, None], 0), axis=0)
    used = off[:, N_EXPERTS - 1] + cnt[:, N_EXPERTS - 1]
    first = jnp.concatenate([jnp.ones((1,), jnp.int32), (tile_e[1:] != tile_e[:-1]).astype(jnp.int32)])
    group = jnp.cumsum(first) - 1
    after = per_tile(et_end)
    next_e = jnp.where(after < n_act[0],
                       jnp.minimum(jnp.sum((et_end[None, :] <= after[:, None]).astype(jnp.int32), axis=1),
                                   N_EXPERTS - 1), -1)
    return (tile_e.astype(jnp.int32), n_rows_tile.astype(jnp.int32), n_act, src.reshape(-1).astype(jnp.int32),
            used.astype(jnp.int32), first, group.astype(jnp.int32), next_e.astype(jnp.int32), n_tiles)


def _expert_kernel(tile_e, n_rows_t, n_act, src_t, used_t, first_t, group_t, next_t,
                   xs_hbm, wgu_hbm, bgu_ref, wdn_hbm, bdn_ref, ys_hbm,
                   xin, yout, wgu_f, wdn_f, wgu_s, wdn_s, zeros, sem_in, sem_out, sem_zero, sem_wgu, sem_wdn,
                   *, tm, tok_block, n_tok_tiles, layer):
    j = pl.program_id(0)
    na = n_act[0]
    chunks = tm // SUBLANES

    def weight_copies(e, wslot):
        return (pltpu.make_async_copy(wgu_hbm.at[layer, e], wgu_f.at[wslot], sem_wgu.at[wslot]),
                pltpu.make_async_copy(wdn_hbm.at[layer, e], wdn_f.at[wslot], sem_wdn.at[wslot]))

    def gather(src, dst, size, slot):
        return pltpu.make_async_copy(xs_hbm.at[pl.ds(src, size)], xin.at[slot, pl.ds(dst, size)], sem_in.at[slot])

    def scatter(src, dst, size, slot):
        return pltpu.make_async_copy(yout.at[slot, pl.ds(dst, size)], ys_hbm.at[pl.ds(src, size)], sem_out.at[slot])

    def start_chunks(jj, slot, copy):
        def one(c):
            src = pl.multiple_of(src_t[jj * chunks + c], SUBLANES)
            copy(src, pl.multiple_of(c * SUBLANES, SUBLANES), SUBLANES, slot).start()

        def body(c, carry):
            one(c)
            return carry

        def body_unrolled(g, carry):
            for u in range(SUBLANES):
                one(g * SUBLANES + u)
            return carry

        n = lax.shift_right_logical(n_rows_t[jj], 3)

        @pl.when(n == chunks)
        def _():
            lax.fori_loop(0, chunks // SUBLANES, body_unrolled, 0)

        @pl.when(n != chunks)
        def _():
            lax.fori_loop(0, n, body, 0)

    def wait_rows(jj, slot, copy):
        n = n_rows_t[jj]
        size = tm
        while size >= SUBLANES:
            @pl.when((n & size) != 0)
            def _(size=size):
                copy(0, 0, size, slot).wait()
            size //= 2

    slot = j % 2

    @pl.when(j == 0)
    def _():
        xin[...] = jnp.zeros_like(xin)
        start_chunks(0, 0, gather)
        zeros[...] = jnp.zeros_like(zeros)

        def clear_tail(i, copy_op):
            used = used_t[i]
            tail = tok_block - used
            for size in (256, 128, 64, 32, 16, 8):
                @pl.when((tail & size) != 0)
                def _():
                    at = pl.multiple_of(i * tok_block + used + (tail & ~(2 * size - 1)), SUBLANES)
                    copy_op(pltpu.make_async_copy(zeros.at[pl.ds(0, size)], ys_hbm.at[pl.ds(at, size)], sem_zero))

        def start_clear(i, c):
            clear_tail(i, lambda cp: cp.start())
            return c

        def wait_clear(i, c):
            clear_tail(i, lambda cp: cp.wait())
            return c

        lax.fori_loop(0, n_tok_tiles, start_clear, 0)
        lax.fori_loop(0, n_tok_tiles, wait_clear, 0)

    @pl.when(j + 1 < na)
    def _():
        start_chunks(j + 1, 1 - slot, gather)

    @pl.when(j < na)
    def _():
        e = tile_e[j]

        @pl.when(first_t[j] == 1)
        def _():
            wslot = group_t[j] % 2

            @pl.when(j == 0)
            def _():
                for cp in weight_copies(e, wslot):
                    cp.start()

            for cp in weight_copies(e, wslot):
                cp.wait()
            wgu_s[...] = wgu_f[wslot].astype(BF16)
            wdn_s[...] = wdn_f[wslot].astype(BF16)

            @pl.when(next_t[j] >= 0)
            def _():
                for cp in weight_copies(next_t[j], 1 - wslot):
                    cp.start()

        wait_rows(j, slot, gather)
        x = xin[slot]
        lane = lax.broadcasted_iota(jnp.int32, (tm, LANES), 1)
        p = jnp.sum(jnp.where(lane % N_EXPERTS == e, x[:, D_MODEL:], 0.0), axis=-1, keepdims=True)
        gu = jnp.dot(x[:, :D_MODEL].astype(BF16), wgu_s[...], preferred_element_type=F32) + bgu_ref[0, 0]
        f = gu.shape[1] // 2
        gate = jnp.minimum(gu[:, :f], SWIGLU_LIMIT)
        up = jnp.clip(gu[:, f:], -SWIGLU_LIMIT, SWIGLU_LIMIT)
        hid = (up + 1.0) * gate * jax.nn.sigmoid(SWIGLU_ALPHA * gate)
        y = jnp.dot(hid.astype(BF16), wdn_s[...], preferred_element_type=F32) + bdn_ref[0, 0]
        yout[slot] = y * p
        start_chunks(j, slot, scatter)

        @pl.when(j >= 1)
        def _():
            wait_rows(j - 1, 1 - slot, scatter)

        @pl.when(j == na - 1)
        def _():
            wait_rows(j, slot, scatter)


def _experts(xs, plan, layer, w_gu, b_gu, w_dn, b_dn):
    tile_e, n_rows_tile, n_act, src, used, first, group, next_e, n_tiles = plan
    tm = TM_EXPERT
    f2 = w_gu.shape[-1]

    def expert(j, te, nr, na, *_):
        return (layer, te[jnp.minimum(j, na[0] - 1)], 0, 0)

    in_specs = [pl.BlockSpec(memory_space=pl.ANY),
                pl.BlockSpec(memory_space=pl.ANY),
                pl.BlockSpec((1, 1, 1, f2), expert),
                pl.BlockSpec(memory_space=pl.ANY),
                pl.BlockSpec((1, 1, 1, D_MODEL), expert)]
    grid_spec = pltpu.PrefetchScalarGridSpec(
        num_scalar_prefetch=8, grid=(n_tiles,), in_specs=in_specs,
        out_specs=pl.BlockSpec(memory_space=pl.ANY),
        scratch_shapes=[pltpu.VMEM((2, tm, XS_WIDTH), F32), pltpu.VMEM((2, tm, D_MODEL), F32),
                        pltpu.VMEM((2, D_MODEL, f2), F32), pltpu.VMEM((2, f2 // 2, D_MODEL), F32),
                        pltpu.VMEM((D_MODEL, f2), BF16), pltpu.VMEM((f2 // 2, D_MODEL), BF16),
                        pltpu.VMEM((ZERO_ROWS, D_MODEL), F32),
                        pltpu.SemaphoreType.DMA((2,)), pltpu.SemaphoreType.DMA((2,)), pltpu.SemaphoreType.DMA(()),
                        pltpu.SemaphoreType.DMA((2,)), pltpu.SemaphoreType.DMA((2,))])
    return pl.pallas_call(
        functools.partial(_expert_kernel, tm=tm, tok_block=TOK_BLOCK, n_tok_tiles=xs.shape[0] // TOK_BLOCK,
                          layer=layer),
        grid_spec=grid_spec,
        out_shape=jax.ShapeDtypeStruct((xs.shape[0], D_MODEL), F32),
        compiler_params=_params(1),
        name="moe_experts",
    )(tile_e, n_rows_tile, n_act, src, used, first, group, next_e,
      xs, w_gu, b_gu.reshape(DEPTH, N_EXPERTS, 1, f2), w_dn, b_dn.reshape(DEPTH, N_EXPERTS, 1, D_MODEL))


def _combine_kernel(ys_ref, pos_ref, h1_ref, g2_ref, l2g_ref, l2b_ref, o_ref):
    tm = pos_ref.shape[0]
    n_sorted = ys_ref.shape[0]
    pos = pos_ref[...]
    col = lax.broadcasted_iota(jnp.int32, (tm, n_sorted), 1).astype(F32)
    sel = jnp.zeros((tm, n_sorted), F32)
    for kk in range(TOP_K):
        sel = sel + (col == pos[:, kk:kk + 1]).astype(F32)
    sel = sel.astype(BF16)
    ys = ys_ref[...]
    hi = ys.astype(BF16)
    rest = ys - hi.astype(F32)
    mid = rest.astype(BF16)
    lo = (rest - mid.astype(F32)).astype(BF16)
    y2 = (jnp.dot(sel, hi, preferred_element_type=F32) + jnp.dot(sel, mid, preferred_element_type=F32)
          + jnp.dot(sel, lo, preferred_element_type=F32))
    o_ref[...] = _layer_norm(DEEPNORM_ALPHA * h1_ref[...] + g2_ref[0] * y2) * l2g_ref[...] + l2b_ref[...]


def _combine(ys, pos4, h1, mod3, mod_base, l2g, l2b, n_lat, seq):
    n_rows = h1.shape[0]
    tm = TM_POST
    tiles_per_seq = seq // tm
    n_lat_tiles = n_lat // tm
    n_groups_lat = n_lat // seq

    def group(i):
        return jnp.where(i < n_lat_tiles, i // tiles_per_seq, n_groups_lat)

    in_specs = [pl.BlockSpec((TOK_BLOCK, D_MODEL), lambda i: (i, 0)),
                pl.BlockSpec((tm, LANES), lambda i: (i, 0)),
                pl.BlockSpec((tm, D_MODEL), lambda i: (i, 0)),
                pl.BlockSpec((1, 1, D_MODEL), lambda i: (mod_base + group(i) * 6 + 5, 0, 0)),
                pl.BlockSpec((1, D_MODEL), lambda i: (0, 0)),
                pl.BlockSpec((1, D_MODEL), lambda i: (0, 0))]
    return pl.pallas_call(
        _combine_kernel,
        grid=(n_rows // tm,),
        in_specs=in_specs,
        out_specs=pl.BlockSpec((tm, D_MODEL), lambda i: (i, 0)),
        out_shape=jax.ShapeDtypeStruct((n_rows, D_MODEL), F32),
        compiler_params=_params(1),
        name="moe_combine",
    )(ys, pos4, h1, mod3, l2g.reshape(1, D_MODEL), l2b.reshape(1, D_MODEL))


def _block_diag(w):
    two, n, d, e = w.shape
    eye = jnp.eye(n, dtype=w.dtype)
    return (w[:, :, :, None, :] * eye[None, :, None, :, None]).reshape(two, n * d, n * e)


def kernel(x, c, ctx, c_ctx, w_mod, b_mod, w_in, b_in, na_rpb, w_proj_attn, w_proj_conv, w_proj_lru, sc_conv_w, lru_conv_w, lru_conv_b, lru_lambda, lru_w_r, lru_b_r, lru_w_i, lru_b_i, w_o, b_o, ln1_g, ln1_b, router_w, router_b, exp_w_gu, exp_b_gu, exp_w_dn, exp_b_dn, ln2_g, ln2_b):
    n_batch, seq, d = x.shape
    n_ctx = ctx.shape[1]
    n_lat = n_batch * seq
    n_all = n_lat + n_batch * n_ctx
    assert d == D_MODEL and n_batch + 1 <= SUBLANES

    cc = jnp.concatenate([c, c_ctx[None], jnp.zeros((SUBLANES - n_batch - 1, d), F32)], axis=0)
    mod = _modulation(cc, w_mod, b_mod)
    groups = n_batch + 1
    mod3 = mod.reshape(DEPTH, SUBLANES, 6, d)[:, :groups].reshape(DEPTH * groups * 6, 1, d)

    cos_t, sin_t = _make_rope(seq, TM_INPROJ)
    h = (x.reshape(n_lat, d), ctx.reshape(n_batch * n_ctx, d))

    for layer in range(DEPTH):
        last = layer == DEPTH - 1
        mod_base = layer * groups * 6
        w_in_bf = w_in[layer].astype(BF16)
        q, k, v, sb, rest = _input_projection(h, mod3, mod_base, w_in_bf, b_in[layer], N_EARLY,
                                              cos_t, sin_t, n_lat, seq)
        sp = jax.nn.softplus(-lru_lambda[layer])
        zb, hf, hb = _conv_scan(sb, rest, sc_conv_w[layer], lru_conv_w[layer], lru_conv_b[layer], sp,
                                _block_diag(lru_w_r[layer]).astype(BF16), _block_diag(lru_w_i[layer]).astype(BF16),
                                lru_b_r[layer], lru_b_i[layer], n_batch, seq, n_ctx)
        att = _attention(q, k, v, _attention_bias(na_rpb[layer], seq), n_batch, seq, n_ctx, not last)
        n_rows = n_lat if last else n_all
        rw_full = jnp.pad(router_w[layer], ((0, 0), (0, LANES - N_EXPERTS)))
        rw_hi = rw_full.astype(BF16)
        rw_pad = jnp.concatenate([rw_hi, (rw_full - rw_hi.astype(F32)).astype(BF16)], axis=1)
        rb_pad = jnp.concatenate([router_b[layer], jnp.full((LANES - N_EXPERTS,), NEG_BIG, F32)]).reshape(1, LANES)
        h1, xs, pos4, cnt_out = _post_mixer(
            h, att, zb, hf, hb, rest, mod3, mod_base, w_in_bf, b_in[layer],
            w_proj_attn[layer].astype(BF16), w_proj_conv[layer].astype(BF16), w_proj_lru[layer].astype(BF16),
            w_o[layer].astype(BF16), b_o[layer], ln1_g[layer], ln1_b[layer], rw_pad, rb_pad, n_rows, n_lat, seq)
        plan = _expert_plan(cnt_out, n_rows // TM_POST, n_rows)
        ys = _experts(xs, plan, layer, exp_w_gu, exp_b_gu, exp_w_dn, exp_b_dn)
        h = (_combine(ys, pos4, h1, mod3, mod_base, ln2_g[layer], ln2_b[layer], n_lat, seq),)
    return h[0].reshape(n_batch, seq, d)
```

```python
import functools

import numpy as np
import jax
import jax.numpy as jnp
from jax import lax
from jax.experimental import pallas as pl
from jax.experimental.pallas import tpu as pltpu

D_MODEL = 1024
DEPTH = 2
GRID_W = 64
NA_HEADS = 8
NA_HEAD_DIM = 64
NA_WIDTH = NA_HEADS * NA_HEAD_DIM
NA_WIN_ROWS = 8
NA_WIN_COLS = 16
ROPE_BASE = 10000.0
BRANCH_WIDTH = 512
LRU_BLOCKS = 8
LRU_C = 8.0
P_TOTAL = 7168
N_EARLY = 8 * BRANCH_WIDTH
N_EXPERTS = 32
TOP_K = 4
SWIGLU_LIMIT = 7.0
SWIGLU_ALPHA = 1.702
LN_EPS = 1e-5
DEEPNORM_ALPHA = (2 * DEPTH) ** 0.25
NEG_BIG = -1e30

LANES = 128
SUBLANES = 8
VMEM_LIMIT_BYTES = 56 * 1024 * 1024

TM_INPROJ = 1024
TN_INPROJ = 1024
SCAN_CHUNK = 256
ATT_ROWS = 4
ATT_QROWS = ATT_ROWS * GRID_W
TM_POST = 256
TM_EXPERT = 512
XS_WIDTH = D_MODEL + LANES
TOK_BLOCK = TM_POST * TOP_K + N_EXPERTS * SUBLANES
ZERO_ROWS = 256

F32 = jnp.float32
BF16 = jnp.bfloat16


def _params(n_axes):
    return pltpu.CompilerParams(dimension_semantics=("arbitrary",) * n_axes,
                                vmem_limit_bytes=VMEM_LIMIT_BYTES)


def _layer_norm(x):
    mu = jnp.mean(x, axis=-1, keepdims=True)
    xc = x - mu
    var = jnp.mean(xc * xc, axis=-1, keepdims=True)
    return xc * lax.rsqrt(var + LN_EPS)


def _mod_kernel(c_ref, w_ref, b_ref, o_ref):
    c = c_ref[...]
    s = (c * jax.nn.sigmoid(c)).astype(BF16)
    o_ref[0] = jnp.dot(s, w_ref[0].astype(BF16), preferred_element_type=F32) + b_ref[0]


def _modulation(cc, w_mod, b_mod):
    n_out = w_mod.shape[-1]
    return pl.pallas_call(
        _mod_kernel,
        grid=(DEPTH, n_out // D_MODEL),
        in_specs=[pl.BlockSpec((SUBLANES, D_MODEL), lambda l, j: (0, 0)),
                  pl.BlockSpec((1, D_MODEL, D_MODEL), lambda l, j: (l, 0, j)),
                  pl.BlockSpec((1, 1, D_MODEL), lambda l, j: (l, 0, j))],
        out_specs=pl.BlockSpec((1, SUBLANES, D_MODEL), lambda l, j: (l, 0, j)),
        out_shape=jax.ShapeDtypeStruct((DEPTH, SUBLANES, n_out), F32),
        compiler_params=_params(2),
        name="modulation",
    )(cc, w_mod, b_mod.reshape(DEPTH, 1, n_out))


def _rope_half(x, cos, sin_signed):
    lane = lax.broadcasted_iota(jnp.int32, (x.shape[0], LANES), 1)
    first = (lane % 32) < 16
    outs = []
    for cidx in range(x.shape[1] // LANES):
        xc = x[:, cidx * LANES:(cidx + 1) * LANES]
        partner = jnp.where(first, pltpu.roll(xc, LANES - 16, 1), pltpu.roll(xc, 16, 1))
        outs.append(xc * cos + partner * sin_signed)
    return jnp.concatenate(outs, axis=1)


def _inproj_kernel(ha_ref, hb_ref, sh_ref, sc_ref, w_ref, b_ref, cos_ref, sin_ref,
                   q_ref, k_ref, v_ref, sb_ref, rest_ref, xn_ref, *, split):
    i = pl.program_id(0)
    j = pl.program_id(1)

    def normalise(h_ref):
        y = _layer_norm(h_ref[...])
        xn_ref[...] = (y * (1.0 + sc_ref[0]) + sh_ref[0]).astype(BF16)

    @pl.when(jnp.logical_and(j == 0, i < split))
    def _():
        normalise(ha_ref)

    @pl.when(jnp.logical_and(j == 0, i >= split))
    def _():
        normalise(hb_ref)

    acc =jnp.dot(xn_ref[...], w_ref[...], preferred_element_type=F32) + b_ref[...]
    half = BRANCH_WIDTH

    @pl.when(j == 0)
    def _():
        cos = cos_ref[...]
        sin = sin_ref[...]
        q_ref[...] = _rope_half(acc[:, :half], cos, sin).astype(BF16)
        k_ref[...] = _rope_half(acc[:, half:], cos, sin).astype(BF16)

    @pl.when(j == 1)
    def _():
        v_ref[...] = acc[:, :half].astype(BF16)
        sb_ref[...] = acc[:, half:]

    @pl.when(j >= 2)
    def _():
        rest_ref[...] = acc


def _input_projection(h_parts, mod3, mod_base, w_in_bf, b_in, n_cols, cos_t, sin_t, n_lat, seq):
    tm, tn = TM_INPROJ, TN_INPROJ
    ha, hb = h_parts[0], h_parts[-1]
    split = ha.shape[0] // tm
    m = ha.shape[0] + (hb.shape[0] if len(h_parts) == 2 else 0)
    n_lat_tiles = n_lat // tm
    tiles_per_seq = seq // tm
    n_groups_lat = n_lat // seq

    def group(i):
        return jnp.where(i < n_lat_tiles, i // tiles_per_seq, n_groups_lat)

    def rope_blk(i):
        return jnp.where(i < n_lat_tiles, i % tiles_per_seq, tiles_per_seq)

    half = BRANCH_WIDTH
    return pl.pallas_call(
        functools.partial(_inproj_kernel, split=split),
        grid=(m // tm, n_cols // tn),
        in_specs=[pl.BlockSpec((tm, D_MODEL), lambda i, j: (jnp.minimum(i, split - 1), 0)),
                  pl.BlockSpec((tm, D_MODEL), lambda i, j: (jnp.maximum(i - split, 0), 0)),
                  pl.BlockSpec((1, 1, D_MODEL), lambda i, j: (mod_base + group(i) * 6 + 0, 0, 0)),
                  pl.BlockSpec((1, 1, D_MODEL), lambda i, j: (mod_base + group(i) * 6 + 1, 0, 0)),
                  pl.BlockSpec((D_MODEL, tn), lambda i, j: (0, j)),
                  pl.BlockSpec((1, tn), lambda i, j: (0, j)),
                  pl.BlockSpec((tm, LANES), lambda i, j: (rope_blk(i), 0)),
                  pl.BlockSpec((tm, LANES), lambda i, j: (rope_blk(i), 0))],
        out_specs=[pl.BlockSpec((tm, half), lambda i, j: (i, 0)),
                   pl.BlockSpec((tm, half), lambda i, j: (i, 0)),
                   pl.BlockSpec((tm, half), lambda i, j: (i, 0)),
                   pl.BlockSpec((tm, half), lambda i, j: (i, 0)),
                   pl.BlockSpec((tm, tn), lambda i, j: (i, jnp.maximum(j - 2, 0)))],
        out_shape=[jax.ShapeDtypeStruct((m, half), BF16),
                   jax.ShapeDtypeStruct((m, half), BF16),
                   jax.ShapeDtypeStruct((m, half), BF16),
                   jax.ShapeDtypeStruct((m, half), F32),
                   jax.ShapeDtypeStruct((m, n_cols - 2 * tn), F32)],
        scratch_shapes=[pltpu.VMEM((tm, D_MODEL), BF16)],
        compiler_params=_params(2),
        name="input_projection",
    )(ha, hb, mod3, mod3, w_in_bf, b_in.reshape(1, -1), cos_t, sin_t)


def _make_rope(seq, tm):
    t = np.arange(seq)
    row_pos, col_pos = t // GRID_W, t % GRID_W
    d = np.arange(LANES) % NA_HEAD_DIM
    m = NA_HEAD_DIM // 4
    inv_freq = (ROPE_BASE ** (-jnp.arange(m, dtype=F32) / m))[d % m]
    pos = np.where((d < 2 * m)[None, :], row_pos[:, None], col_pos[:, None])
    ang = jnp.asarray(pos).astype(F32) * inv_freq[None, :]
    cos = jnp.cos(ang)
    sin = jnp.sin(ang)
    sin_signed = jnp.where(jnp.asarray((d % (2 * m)) < m)[None, :], -sin, sin)
    cos = jnp.concatenate([cos, jnp.ones((tm, LANES), F32)], axis=0)
    sin_signed = jnp.concatenate([sin_signed, jnp.zeros((tm, LANES), F32)], axis=0)
    return cos, sin_signed


def _scan_kernel(fblk, bblk, first, last, seqb,
                 sb_ref, scg_ref, sx_ref, lxf_ref,
                 scgp_ref, sxp_ref, lxfp_ref, scgn_ref, sxn_ref, lxfn_ref,
                 lxb_ref, lxbp_ref, lxbn_ref,
                 scw_ref, cw_ref, cb_ref, sp_ref, wr_ref, wi_ref, br_ref, bi_ref,
                 zb_ref, hf_ref, hb_ref,
                 a_s, b_s, hc_s, st_s, *, ch, n_ctx_items):
    it = pl.program_id(0)
    is_first = first[it] == 1
    is_last = last[it] == 1
    is_ctx = it < n_ctx_items
    b = seqb[it]
    width = BRANCH_WIDTH
    row = lax.broadcasted_iota(jnp.int32, (ch, width), 0)
    not_first = jnp.where(is_first, 0.0, 1.0).astype(F32)
    not_last = jnp.where(is_last, 0.0, 1.0).astype(F32)

    def back1(u, prev_row):
        return jnp.where(row == 0, prev_row, pltpu.roll(u, 1, 0))

    def back2(u, prev2, prev1):
        return jnp.where(row == 0, prev2, jnp.where(row == 1, prev1, pltpu.roll(u, 2, 0)))

    def fwd1(u, next_row):
        return jnp.where(row == ch - 1, next_row, pltpu.roll(u, ch - 1, 0))

    def lru_input(lx_ref, lxp_ref, lxn_ref, prev_ok, next_ok):
        x = lx_ref[...]
        p = lxp_ref[...] * prev_ok
        n = lxn_ref[...] * next_ok
        return (cw_ref[0:1] * back2(x, p[6:7], p[7:8]) + cw_ref[1:2] * back1(x, p[7:8])
                + cw_ref[2:3] * x + cw_ref[3:4] * fwd1(x, n[0:1]) + cb_ref[...])

    def coeffs(d, xm):
        xb = xm.astype(BF16)
        r = jax.nn.sigmoid(jnp.dot(xb, wr_ref[d], preferred_element_type=F32) + br_ref[d:d + 1])
        g = jax.nn.sigmoid(jnp.dot(xb, wi_ref[d], preferred_element_type=F32) + bi_ref[d:d + 1])
        log_a = (-LRU_C * sp_ref[d:d + 1]) * r
        a = jnp.exp(log_a)
        a_s[d] = a
        b_s[d] = jnp.sqrt(-jnp.tanh(log_a) * (a * a + 1.0)) * (g * xm)

    u = scg_ref[...] * sx_ref[...]
    u_prev = scgp_ref[7:8] * sxp_ref[7:8] * not_first
    u_next = scgn_ref[0:1] * sxn_ref[0:1] * not_last
    conv = scw_ref[0:1] * back1(u, u_prev) + scw_ref[1:2] * u + scw_ref[2:3] * fwd1(u, u_next)
    zb_ref[...] = (sb_ref[...] * conv).astype(BF16)

    coeffs(0, lru_input(lxf_ref, lxfp_ref, lxfn_ref, not_first, not_last))
    coeffs(1, lru_input(lxb_ref, lxbp_ref, lxbn_ref, not_last, not_first))

    @pl.when(jnp.logical_and(is_first, is_ctx))
    def _():
        hc_s[...] = jnp.zeros_like(hc_s)

    @pl.when(jnp.logical_and(is_first, jnp.logical_not(is_ctx)))
    def _():
        hc_s[0:1] = st_s[pl.ds(2 * b, 1), :]
        hc_s[1:2] = st_s[pl.ds(2 * b + 1, 1), :]

    def body(i, carry):
        hf, hb = carry
        base = i * SUBLANES
        for r in range(SUBLANES):
            t = base + r
            hf = a_s[0, pl.ds(t, 1), :] * hf + b_s[0, pl.ds(t, 1), :]
            hf_ref[pl.ds(t, 1), :] = hf
            tb = ch - 1 - t
            hb = a_s[1, pl.ds(tb, 1), :] * hb + b_s[1, pl.ds(tb, 1), :]
            hb_ref[pl.ds(tb, 1), :] = hb
        return hf, hb

    hf, hb = lax.fori_loop(0, ch // SUBLANES, body, (hc_s[0:1], hc_s[1:2]))
    hc_s[0:1] = hf
    hc_s[1:2] = hb

    @pl.when(is_ctx)
    def _():
        st_s[pl.ds(2 * b, 1), :] = hf
        st_s[pl.ds(2 * b + 1, 1), :] = hb


def _scan_tables(n_batch, seq, n_ctx, ch):
    assert n_ctx == ch
    nc = seq // ch
    ctx0 = n_batch * seq // ch
    fblk, bblk, first, last, seqb = [], [], [], [], []
    for b in range(n_batch):
        fblk.append(ctx0 + b); bblk.append(ctx0 + b); first.append(1); last.append(1); seqb.append(b)
    for b in range(n_batch):
        for c in range(nc):
            fblk.append(b * nc + c); bblk.append(b * nc + nc - 1 - c)
            first.append(int(c == 0)); last.append(int(c == nc - 1)); seqb.append(b)
    return [np.asarray(a, np.int32) for a in (fblk, bblk, first, last, seqb)]


def _conv_scan(sb, rest, sc_w, lru_cw, lru_cb, sp, wr_bd, wi_bd, b_r, b_i, n_batch, seq, n_ctx):
    m = sb.shape[0]
    ch = SCAN_CHUNK
    width = BRANCH_WIDTH
    tables = _scan_tables(n_batch, seq, n_ctx, ch)
    n_items = len(tables[0])
    halo_per_chunk = ch // SUBLANES
    last_halo = m // SUBLANES - 1

    def cur(col, which):
        return pl.BlockSpec((ch, width), lambda i, f, bk, *_: ((f, bk)[which][i], col))

    def prev(col, which):
        return pl.BlockSpec((SUBLANES, width),
                            lambda i, f, bk, *_: (jnp.maximum((f, bk)[which][i] * halo_per_chunk - 1, 0), col))

    def nxt(col, which):
        return pl.BlockSpec((SUBLANES, width),
                            lambda i, f, bk, *_: (jnp.minimum(((f, bk)[which][i] + 1) * halo_per_chunk, last_halo), col))

    def full(shape):
        return pl.BlockSpec(shape, lambda i, *_: (0,) * len(shape))

    in_specs = [cur(0, 0), cur(0, 0), cur(1, 0), cur(2, 0),
                prev(0, 0), prev(1, 0), prev(2, 0), nxt(0, 0), nxt(1, 0), nxt(2, 0),
                cur(2, 1), prev(2, 1), nxt(2, 1),
                full(sc_w.shape), full(lru_cw.shape), full((1, width)), full(sp.shape),
                full(wr_bd.shape), full(wi_bd.shape), full(b_r.shape), full(b_i.shape)]
    out_specs = [cur(0, 0), cur(0, 0), cur(0, 1)]
    grid_spec = pltpu.PrefetchScalarGridSpec(
        num_scalar_prefetch=5, grid=(n_items,), in_specs=in_specs, out_specs=out_specs,
        scratch_shapes=[pltpu.VMEM((2, ch, width), F32), pltpu.VMEM((2, ch, width), F32),
                        pltpu.VMEM((SUBLANES, width), F32), pltpu.VMEM((2 * n_batch, width), F32)])
    return pl.pallas_call(
        functools.partial(_scan_kernel, ch=ch, n_ctx_items=n_batch),
        grid_spec=grid_spec,
        out_shape=[jax.ShapeDtypeStruct((m, width), BF16),
                   jax.ShapeDtypeStruct((m, width), F32),
                   jax.ShapeDtypeStruct((m, width), F32)],
        compiler_params=_params(1),
        name="conv_scan",
    )(*[jnp.asarray(t) for t in tables],
      sb, rest, rest, rest, rest, rest, rest, rest, rest, rest, rest, rest, rest,
      sc_w, lru_cw, lru_cb.reshape(1, width), sp, wr_bd, wi_bd, b_r, b_i)


def _attention_kernel(qblk, bat, r0t, cls, q_ref, k_ref, v_ref, kc_ref, vc_ref, *rest, band):
    bias_refs, o_ref = rest[:ATT_ROWS], rest[ATT_ROWS]
    it = pl.program_id(0)
    nq = GRID_W
    lane = lax.broadcasted_iota(jnp.int32, (nq, LANES), 1)
    low = lane < NA_HEAD_DIM
    scale = NA_HEAD_DIM ** -0.5
    nt = (((1,), (1,)), ((), ()))
    for row in range(ATT_ROWS):
        start = pl.multiple_of(r0t[it * ATT_ROWS + row] * GRID_W, GRID_W)
        bias_ref = bias_refs[row]
        qrows = slice(row * nq, (row + 1) * nq)
        for hp in range(NA_HEADS // 2):
            cols = slice(hp * LANES, (hp + 1) * LANES)
            qp = q_ref[qrows, cols].astype(F32) * scale
            qs = jnp.concatenate([jnp.where(low, qp, 0.0), jnp.where(low, 0.0, qp)], axis=0).astype(BF16)
            kb = k_ref[pl.ds(start, band), cols]
            vb = v_ref[pl.ds(start, band), cols]
            s_loc = lax.dot_general(qs, kb, nt, preferred_element_type=F32)
            s_ctx = lax.dot_general(qs, kc_ref[:, cols], nt, preferred_element_type=F32)
            bias = jnp.concatenate([bias_ref[0, 2 * hp], bias_ref[0, 2 * hp + 1]], axis=0)
            s_loc = s_loc + bias
            mx =jnp.maximum(jnp.max(s_loc, axis=-1, keepdims=True), jnp.max(s_ctx, axis=-1, keepdims=True))
            e_loc = jnp.exp(s_loc - mx)
            e_ctx = jnp.exp(s_ctx - mx)
            den = jnp.sum(e_loc, axis=-1, keepdims=True) + jnp.sum(e_ctx, axis=-1, keepdims=True)
            o = (jnp.dot(e_loc.astype(BF16), vb, preferred_element_type=F32)
                 + jnp.dot(e_ctx.astype(BF16), vc_ref[:, cols], preferred_element_type=F32)) / den
            o_ref[qrows, cols] = jnp.where(low, o[:nq], o[nq:]).astype(BF16)


def _attention_tables(n_batch, seq, n_ctx, with_ctx_queries):
    rows = seq // GRID_W
    kr = min(NA_WIN_ROWS, rows)
    assert rows % ATT_ROWS == 0 and n_ctx % ATT_QROWS == 0
    qblk, bat, r0t, cls = [], [], [], []
    ctx_q0 = n_batch * seq // ATT_QROWS
    for b in range(n_batch):
        for rg in range(rows // ATT_ROWS):
            qblk.append(b * (rows // ATT_ROWS) + rg); bat.append(b)
            for r in range(rg * ATT_ROWS, (rg + 1) * ATT_ROWS):
                r0 = min(max(r - kr // 2, 0), rows - kr)
                r0t.append(r0); cls.append(r - r0)
        if with_ctx_queries:
            for c in range(n_ctx // ATT_QROWS):
                qblk.append(ctx_q0 + b * (n_ctx // ATT_QROWS) + c); bat.append(b)
                r0t.extend([0] * ATT_ROWS); cls.extend([kr] * ATT_ROWS)
    return [np.asarray(a, np.int32) for a in (qblk, bat, r0t, cls)]


def _attention_bias(rpb, seq):
    rows = seq // GRID_W
    kr = min(NA_WIN_ROWS, rows)
    kc = NA_WIN_COLS
    cq = np.arange(GRID_W)
    c0 = np.clip(cq - kc // 2, 0, GRID_W - kc)
    ck = np.arange(GRID_W)
    inside = (ck[None, :] >= c0[:, None]) & (ck[None, :] < c0[:, None] + kc)
    dc = np.clip(ck[None, :] - cq[:, None] + (NA_WIN_COLS - 1), 0, 2 * NA_WIN_COLS - 2)
    n_dr = 2 * NA_WIN_ROWS - 1
    n_dc = 2 * NA_WIN_COLS - 1
    pick = jnp.asarray((np.arange(n_dc)[:, None] == dc.reshape(1, -1)).astype(np.float32))
    picked = jnp.dot(rpb.reshape(-1, n_dc), pick, precision=lax.Precision.HIGHEST)
    picked = picked.reshape(NA_HEADS, 2 * NA_WIN_ROWS - 1, GRID_W, GRID_W)
    table = jnp.where(jnp.asarray(inside)[None, None], picked, NEG_BIG)
    table = table.transpose(0, 2, 1, 3).reshape(NA_HEADS, GRID_W, n_dr * GRID_W)
    classes = []
    for cl in range(kr):
        lo = (NA_WIN_ROWS - 1 - cl) * GRID_W
        classes.append(table[:, :, lo:lo + kr * GRID_W])
    classes.append(jnp.full((NA_HEADS, GRID_W, kr * GRID_W), NEG_BIG, F32))
    return jnp.stack(classes, axis=0)


def _attention(q, k, v, bias, n_batch, seq, n_ctx, with_ctx_queries):
    m = q.shape[0] if with_ctx_queries else n_batch * seq
    rows = seq // GRID_W
    kr = min(NA_WIN_ROWS, rows)
    band = kr * GRID_W
    tables = _attention_tables(n_batch, seq, n_ctx, with_ctx_queries)
    n_items = len(tables[0])
    ctx_blk0 = n_batch * seq // n_ctx
    width = NA_WIDTH
    in_specs = [pl.BlockSpec((ATT_QROWS, width), lambda i, qb, bt, r0, cl: (qb[i], 0)),
                pl.BlockSpec((seq, width), lambda i, qb, bt, r0, cl: (bt[i], 0)),
                pl.BlockSpec((seq, width), lambda i, qb, bt, r0, cl: (bt[i], 0)),
                pl.BlockSpec((n_ctx, width), lambda i, qb, bt, r0, cl: (ctx_blk0 + bt[i], 0)),
                pl.BlockSpec((n_ctx, width), lambda i, qb, bt, r0, cl: (ctx_blk0 + bt[i], 0))]
    for row in range(ATT_ROWS):
        in_specs.append(pl.BlockSpec((1, NA_HEADS, GRID_W, band),
                                     lambda i, qb, bt, r0, cl, row=row: (cl[i * ATT_ROWS + row], 0, 0, 0)))
    out_specs = pl.BlockSpec((ATT_QROWS, width), lambda i, qb, bt, r0, cl: (qb[i], 0))
    grid_spec = pltpu.PrefetchScalarGridSpec(num_scalar_prefetch=4, grid=(n_items,),
                                             in_specs=in_specs, out_specs=out_specs)
    return pl.pallas_call(
        functools.partial(_attention_kernel, band=band),
        grid_spec=grid_spec,
        out_shape=jax.ShapeDtypeStruct((m, width), BF16),
        compiler_params=_params(1),
        name="attention",
    )(*[jnp.asarray(t) for t in tables], q, k, v, k, v, *([bias] * ATT_ROWS))


def _post_kernel(ha_ref, hc_ref, att_ref, zb_ref, hf_ref, hb_ref, lg_ref,
                 sh1_ref, sc1_ref, g1_ref, sh2_ref, sc2_ref,
                 wgl0_ref, wgl1_ref, wgl2_ref, bgl0_ref, bgl1_ref, bgl2_ref,
                 wpa_ref, wpc_ref, wpl_ref, wo_ref, bo_ref, l1g_ref, l1b_ref,
                 rw_ref, rb_ref, tri_ref, upper_ref,
                 h1_ref, xs_ref, pos_ref, cnt_ref, *, split):
    h = jnp.where(pl.program_id(0) < split, ha_ref[...], hc_ref[...])
    u1 = (_layer_norm(h) * (1.0 + sc1_ref[0]) + sh1_ref[0]).astype(BF16)
    y_a = jnp.dot(att_ref[...], wpa_ref[...], preferred_element_type=F32)
    y_b = jnp.dot(zb_ref[...], wpc_ref[...], preferred_element_type=F32)
    zc = jax.nn.gelu(lg_ref[...]) * (hf_ref[...] + hb_ref[...])
    y_c = jnp.dot(zc.astype(BF16), wpl_ref[...], preferred_element_type=F32)
    merged = (jax.nn.sigmoid(jnp.dot(u1, wgl0_ref[...], preferred_element_type=F32) + bgl0_ref[...]) * y_a
              + jax.nn.sigmoid(jnp.dot(u1, wgl1_ref[...], preferred_element_type=F32) + bgl1_ref[...]) * y_b
              + jax.nn.sigmoid(jnp.dot(u1, wgl2_ref[...], preferred_element_type=F32) + bgl2_ref[...]) * y_c)
    y = jnp.dot(merged.astype(BF16), wo_ref[...], preferred_element_type=F32) + bo_ref[...]
    h1 = _layer_norm(DEEPNORM_ALPHA * h + g1_ref[0] * y) * l1g_ref[...] + l1b_ref[...]
    h1_ref[...] = h1
    u2 = _layer_norm(h1) * (1.0 + sc2_ref[0]) + sh2_ref[0]

    u_hi = u2.astype(BF16)
    u_lo = (u2 - u_hi.astype(F32)).astype(BF16)
    by_hi = jnp.dot(u_hi, rw_ref[...], preferred_element_type=F32)
    logits = (by_hi[:, :LANES] + by_hi[:, LANES:]
              + jnp.dot(u_lo, rw_ref[:, :LANES], preferred_element_type=F32) + rb_ref[...])
    tm = logits.shape[0]
    lane = lax.broadcasted_iota(jnp.int32, (tm, LANES), 1)
    lane_f = lane.astype(F32)
    work = logits
    tops, idxs, hots = [], [], []
    for _ in range(TOP_K):
        mx = jnp.max(work, axis=-1, keepdims=True)
        idx = jnp.min(jnp.where(work == mx, lane_f, float(LANES)), axis=-1, keepdims=True)
        hot = lane_f == idx
        work = jnp.where(hot, -3e38, work)
        tops.append(mx); idxs.append(idx); hots.append(hot)
    exps = [jnp.exp(t - tops[0]) for t in tops]
    den = exps[0] + exps[1] + exps[2] + exps[3]
    hot_all = jnp.zeros((tm, LANES), F32)
    for hot in hots:
        hot_all = hot_all + hot.astype(F32)
    cnt = jnp.sum(hot_all, axis=0, keepdims=True)
    cnt_pad = jnp.floor((cnt + (SUBLANES - 1.0)) * (1.0 / SUBLANES)) * SUBLANES
    off = jnp.dot(jnp.broadcast_to(cnt_pad, (SUBLANES, LANES)), upper_ref[...],
                  precision=lax.Precision.HIGHEST, preferred_element_type=F32)[0:1]
    slot = off + jnp.dot(tri_ref[...], hot_all.astype(BF16), preferred_element_type=F32)
    pos4 = jnp.zeros((tm, LANES), F32)
    w_tile = jnp.zeros((tm, LANES), F32)
    for kk in range(TOP_K):
        pos_k = jnp.sum(jnp.where(hots[kk], slot, 0.0), axis=-1, keepdims=True)
        pos4 = jnp.where(lane == kk, pos_k, pos4)
        p = exps[kk] / den
        p_hi = p.astype(BF16).astype(F32)
        p_mid = (p - p_hi).astype(BF16).astype(F32)
        p_lo = p - p_hi - p_mid
        w_tile = jnp.where(hots[kk], p_hi, w_tile)
        w_tile = jnp.where(lane_f == idxs[kk] + float(N_EXPERTS), p_mid, w_tile)
        w_tile = jnp.where(lane_f == idxs[kk] + float(2 * N_EXPERTS), p_lo, w_tile)
    pos_ref[...] = pos4
    sub = lax.broadcasted_iota(jnp.int32, (SUBLANES, LANES), 0)
    cnt_ref[...] = jnp.where(sub == 0, cnt, jnp.where(sub == 1, off, 0.0))

    n_sorted = xs_ref.shape[0]
    pos_t = pos4.T
    r_iota = lax.broadcasted_iota(jnp.int32, (n_sorted, tm), 0).astype(F32)
    hit = r_iota == pos_t[0:1, :]
    for kk in range(1, TOP_K):
        hit = jnp.logical_or(hit, r_iota == pos_t[kk:kk + 1, :])
    perm = jnp.where(hit, 1.0, 0.0).astype(BF16)
    feats = jnp.concatenate([u_hi, w_tile.astype(BF16)], axis=1)
    xs_ref[...] = jnp.dot(perm, feats, preferred_element_type=F32)


def _post_mixer(h_parts, att, zb, hf, hb, rest, mod3, mod_base, w_in_bf, b_in, wpa, wpc, wpl, wo, b_o, l1g, l1b,
                rw_pad, rb_pad, n_rows, n_lat, seq):
    tm = TM_POST
    width = BRANCH_WIDTH
    ha, hc = h_parts[0], h_parts[-1]
    split = min(ha.shape[0], n_rows) // tm
    gate_col0 = N_EARLY // D_MODEL
    tiles_per_seq = seq // tm
    n_lat_tiles = n_lat // tm
    n_groups_lat = n_lat // seq
    tri = jnp.asarray(np.tril(np.ones((tm, tm), np.float32), -1), BF16)
    upper = jnp.asarray(np.triu(np.ones((LANES, LANES), np.float32), 1))
    n_tiles = n_rows // tm

    def group(i):
        return jnp.where(i < n_lat_tiles, i // tiles_per_seq, n_groups_lat)

    def rows(wd, col=0):
        return pl.BlockSpec((tm, wd), lambda i: (i, col))

    def full(shape):
        return pl.BlockSpec(shape, lambda i: (0,) * len(shape))

    def mod(which):
        return pl.BlockSpec((1, 1, D_MODEL), lambda i: (mod_base + group(i) * 6 + which, 0, 0))

    in_specs = [pl.BlockSpec((tm, D_MODEL), lambda i: (jnp.minimum(i, split - 1), 0)),
                pl.BlockSpec((tm, D_MODEL), lambda i: (jnp.maximum(i - split, 0), 0)),
                rows(width), rows(width), rows(width), rows(width), rows(width, 3),
                mod(0), mod(1), mod(2), mod(3), mod(4)]
    in_specs += [pl.BlockSpec((D_MODEL, D_MODEL), lambda i, c=c: (0, gate_col0 + c)) for c in range(3)]
    in_specs += [pl.BlockSpec((1, D_MODEL), lambda i, c=c: (0, gate_col0 + c)) for c in range(3)]
    in_specs += [full(wpa.shape), full(wpc.shape), full(wpl.shape), full(wo.shape), full((1, D_MODEL)),
                full((1, D_MODEL)), full((1, D_MODEL)), full(rw_pad.shape), full(rb_pad.shape), full(tri.shape),
                full(upper.shape)]
    out_specs = [rows(D_MODEL), pl.BlockSpec((TOK_BLOCK, XS_WIDTH), lambda i: (i, 0)), rows(LANES),
                 pl.BlockSpec((SUBLANES, LANES), lambda i: (i, 0))]
    return pl.pallas_call(
        functools.partial(_post_kernel, split=split),
        grid=(n_tiles,),
        in_specs=in_specs, out_specs=out_specs,
        out_shape=[jax.ShapeDtypeStruct((n_rows, D_MODEL), F32),
                   jax.ShapeDtypeStruct((n_tiles * TOK_BLOCK, XS_WIDTH), F32),
                   jax.ShapeDtypeStruct((n_rows, LANES), F32),
                   jax.ShapeDtypeStruct((n_tiles * SUBLANES, LANES), F32)],
        compiler_params=_params(1),
        name="post_mixer",
    )(ha, hc, att, zb, hf, hb, rest, mod3, mod3, mod3, mod3, mod3,
      w_in_bf, w_in_bf, w_in_bf, b_in.reshape(1, -1), b_in.reshape(1, -1), b_in.reshape(1, -1),
      wpa, wpc, wpl, wo, b_o.reshape(1, D_MODEL), l1g.reshape(1, D_MODEL), l1b.reshape(1, D_MODEL),
      rw_pad, rb_pad, tri, upper)


def _expert_plan(cnt_out, n_tok_tiles, n_rows):
    tm = TM_EXPERT
    co = cnt_out.reshape(n_tok_tiles, SUBLANES, LANES)
    cnt = (co[:, 0, :N_EXPERTS].astype(jnp.int32) + SUBLANES - 1) // SUBLANES * SUBLANES
    off = co[:, 1, :N_EXPERTS].astype(jnp.int32)
    cum_end = jnp.cumsum(cnt, axis=0)
    cum = cum_end - cnt
    total = cum_end[-1]
    n_et = (total + tm - 1) // tm
    et_end = jnp.cumsum(n_et)
    n_act = et_end[-1:].astype(jnp.int32)
    n_tiles = -(-n_tok_tiles * TOK_BLOCK // tm) + N_EXPERTS
    j = jnp.arange(n_tiles, dtype=jnp.int32)
    tile_e = jnp.minimum(jnp.sum((et_end[None, :] <= j[:, None]).astype(jnp.int32), axis=1), N_EXPERTS - 1)
    pick_e = (tile_e[:, None] == jnp.arange(N_EXPERTS, dtype=jnp.int32)[None, :]).astype(F32)

    def per_tile(table):
        return jnp.dot(pick_e, table.astype(F32), precision=lax.Precision.HIGHEST).astype(jnp.int32)

    row0 = (j - per_tile(et_end - n_et)) * tm
    cum_e = per_tile(cum.T)
    cum_end_e = per_tile(cum_end.T)
    delta_e = per_tile((jnp.arange(n_tok_tiles, dtype=jnp.int32)[:, None] * TOK_BLOCK + off - cum).T)
    q = row0[:, None] + SUBLANES * jnp.arange(tm // SUBLANES, dtype=jnp.int32)[None, :]
    inside = jnp.logical_and(cum_e.T[:, :, None] <= q[None], q[None] < cum_end_e.T[:, :, None])
    src = q + jnp.sum(jnp.where(inside, delta_e.T[:, :, None], 0), axis=0)
    valid = q < per_tile(total)[:, None]
    spare = n_tok_tiles * TOK_BLOCK + SUBLANES * jnp.arange(tm // SUBLANES, dtype=jnp.int32)[None, :]
    gather_src = jnp.concatenate([jnp.where(valid, src, 0), jnp.zeros((1, tm // SUBLANES), jnp.int32)], axis=0)
    scatter_dst = jnp.where(valid, src, spare)
    used = off[:, N_EXPERTS - 1] + cnt[:, N_EXPERTS - 1]
    first = jnp.concatenate([jnp.ones((1,), jnp.int32), (tile_e[1:] != tile_e[:-1]).astype(jnp.int32)])
    group = jnp.cumsum(first) - 1
    after = per_tile(et_end)
    next_e = jnp.where(after < n_act[0],
                       jnp.minimum(jnp.sum((et_end[None, :] <= after[:, None]).astype(jnp.int32), axis=1),
                                   N_EXPERTS - 1), -1)
    return (tile_e.astype(jnp.int32), n_act, gather_src.reshape(-1).astype(jnp.int32),
            scatter_dst.reshape(-1).astype(jnp.int32), used.astype(jnp.int32), first, group.astype(jnp.int32),
            next_e.astype(jnp.int32), n_tiles)


def _expert_kernel(tile_e, n_act, gsrc_t, sdst_t, used_t, first_t, group_t, next_t,
                   xs_hbm, wgu_hbm, bgu_ref, wdn_hbm, bdn_ref, ys_hbm,
                   xin, yout, wgu_f, wdn_f, wgu_s, wdn_s, zeros, sem_in, sem_out, sem_zero, sem_wgu, sem_wdn,
                   *, tm, tok_block, n_tok_tiles, layer):
    j = pl.program_id(0)
    na = n_act[0]
    chunks = tm // SUBLANES

    def weight_copies(e, wslot):
        return (pltpu.make_async_copy(wgu_hbm.at[layer, e], wgu_f.at[wslot], sem_wgu.at[wslot]),
                pltpu.make_async_copy(wdn_hbm.at[layer, e], wdn_f.at[wslot], sem_wdn.at[wslot]))

    def gather(src, dst, size, slot):
        return pltpu.make_async_copy(xs_hbm.at[pl.ds(src, size)], xin.at[slot, pl.ds(dst, size)], sem_in.at[slot])

    def scatter(src, dst, size, slot):
        return pltpu.make_async_copy(yout.at[slot, pl.ds(dst, size)], ys_hbm.at[pl.ds(src, size)], sem_out.at[slot])

    def start_chunks(table, jj, slot, copy):
        for c in range(chunks):
            row = pl.multiple_of(table[jj * chunks + c], SUBLANES)
            copy(row, c * SUBLANES, SUBLANES, slot).start()

    def wait_tile(slot, copy):
        copy(0, 0, tm, slot).wait()

    slot = j % 2

    @pl.when(j == 0)
    def _():
        start_chunks(gsrc_t, 0, 0, gather)
        zeros[...] = jnp.zeros_like(zeros)
        spare = [pltpu.make_async_copy(zeros, ys_hbm.at[pl.ds(n_tok_tiles * tok_block + r, ZERO_ROWS)], sem_zero)
                 for r in range(0, tm, ZERO_ROWS)]
        for cp in spare:
            cp.start()
        for cp in spare:
            cp.wait()

        def clear_tail(i, copy_op):
            used = used_t[i]
            tail = tok_block - used
            for size in (256, 128, 64, 32, 16, 8):
                @pl.when((tail & size) != 0)
                def _():
                    at = pl.multiple_of(i * tok_block + used + (tail & ~(2 * size - 1)), SUBLANES)
                    copy_op(pltpu.make_async_copy(zeros.at[pl.ds(0, size)], ys_hbm.at[pl.ds(at, size)], sem_zero))

        def start_clear(i, c):
            clear_tail(i, lambda cp: cp.start())
            return c

        def wait_clear(i, c):
            clear_tail(i, lambda cp: cp.wait())
            return c

        lax.fori_loop(0, n_tok_tiles, start_clear, 0)
        lax.fori_loop(0, n_tok_tiles, wait_clear, 0)

    @pl.when(j < na)
    def _():
        e = tile_e[j]

        @pl.when(first_t[j] == 1)
        def _():
            wslot = group_t[j] % 2

            @pl.when(j == 0)
            def _():
                for cp in weight_copies(e, wslot):
                    cp.start()

            for cp in weight_copies(e, wslot):
                cp.wait()
            wgu_s[...] = wgu_f[wslot].astype(BF16)
            wdn_s[...] = wdn_f[wslot].astype(BF16)

            @pl.when(next_t[j] >= 0)
            def _():
                for cp in weight_copies(next_t[j], 1 - wslot):
                    cp.start()

        wait_tile(slot, gather)
        start_chunks(gsrc_t, j + 1, 1 - slot, gather)
        x = xin[slot]
        lane = lax.broadcasted_iota(jnp.int32, (tm, LANES), 1)
        p = jnp.sum(jnp.where(lane % N_EXPERTS == e, x[:, D_MODEL:], 0.0), axis=-1, keepdims=True)
        gu = jnp.dot(x[:, :D_MODEL].astype(BF16), wgu_s[...], preferred_element_type=F32) + bgu_ref[0, 0]
        f = gu.shape[1] // 2
        gate = jnp.minimum(gu[:, :f], SWIGLU_LIMIT)
        up = jnp.clip(gu[:, f:], -SWIGLU_LIMIT, SWIGLU_LIMIT)
        hid = (up + 1.0) * gate * jax.nn.sigmoid(SWIGLU_ALPHA * gate)
        y = jnp.dot(hid.astype(BF16), wdn_s[...], preferred_element_type=F32) + bdn_ref[0, 0]
        yout[slot] = y * p
        start_chunks(sdst_t, j, slot, scatter)

        @pl.when(j >= 1)
        def _():
            wait_tile(1 - slot, scatter)

        @pl.when(j == na - 1)
        def _():
            wait_tile(slot, scatter)
            wait_tile(1 - slot, gather)


def _experts(xs, plan, layer, w_gu, b_gu, w_dn, b_dn):
    tile_e, n_act, gather_src, scatter_dst, used, first, group, next_e, n_tiles = plan
    tm = TM_EXPERT
    f2 = w_gu.shape[-1]

    def expert(j, te, na, *_):
        return (layer, te[jnp.minimum(j, na[0] - 1)], 0, 0)

    in_specs = [pl.BlockSpec(memory_space=pl.ANY),
                pl.BlockSpec(memory_space=pl.ANY),
                pl.BlockSpec((1, 1, 1, f2), expert),
                pl.BlockSpec(memory_space=pl.ANY),
                pl.BlockSpec((1, 1, 1, D_MODEL), expert)]
    grid_spec = pltpu.PrefetchScalarGridSpec(
        num_scalar_prefetch=8, grid=(n_tiles,), in_specs=in_specs,
        out_specs=pl.BlockSpec(memory_space=pl.ANY),
        scratch_shapes=[pltpu.VMEM((2, tm, XS_WIDTH), F32), pltpu.VMEM((2, tm, D_MODEL), F32),
                        pltpu.VMEM((2, D_MODEL, f2), F32), pltpu.VMEM((2, f2 // 2, D_MODEL), F32),
                        pltpu.VMEM((D_MODEL, f2), BF16), pltpu.VMEM((f2 // 2, D_MODEL), BF16),
                        pltpu.VMEM((ZERO_ROWS, D_MODEL), F32),
                        pltpu.SemaphoreType.DMA((2,)), pltpu.SemaphoreType.DMA((2,)), pltpu.SemaphoreType.DMA(()),
                        pltpu.SemaphoreType.DMA((2,)), pltpu.SemaphoreType.DMA((2,))])
    return pl.pallas_call(
        functools.partial(_expert_kernel, tm=tm, tok_block=TOK_BLOCK, n_tok_tiles=xs.shape[0] // TOK_BLOCK,
                          layer=layer),
        grid_spec=grid_spec,
        out_shape=jax.ShapeDtypeStruct((xs.shape[0] + tm, D_MODEL), F32),
        compiler_params=_params(1),
        name="moe_experts",
    )(tile_e, n_act, gather_src, scatter_dst, used, first, group, next_e,
      xs, w_gu, b_gu.reshape(DEPTH, N_EXPERTS, 1, f2), w_dn, b_dn.reshape(DEPTH, N_EXPERTS, 1, D_MODEL))


def _combine_kernel(ys_ref, pos_ref, h1_ref, g2_ref, l2g_ref, l2b_ref, o_ref):
    tm = pos_ref.shape[0]
    n_sorted = ys_ref.shape[0]
    pos = pos_ref[...]
    col = lax.broadcasted_iota(jnp.int32, (tm, n_sorted), 1).astype(F32)
    sel = jnp.zeros((tm, n_sorted), F32)
    for kk in range(TOP_K):
        sel = sel + (col == pos[:, kk:kk + 1]).astype(F32)
    sel = sel.astype(BF16)
    ys = ys_ref[...]
    hi = ys.astype(BF16)
    rest = ys - hi.astype(F32)
    mid = rest.astype(BF16)
    lo = (rest - mid.astype(F32)).astype(BF16)
    y2 = (jnp.dot(sel, hi, preferred_element_type=F32) + jnp.dot(sel, mid, preferred_element_type=F32)
          + jnp.dot(sel, lo, preferred_element_type=F32))
    o_ref[...] = _layer_norm(DEEPNORM_ALPHA * h1_ref[...] + g2_ref[0] * y2) * l2g_ref[...] + l2b_ref[...]


def _combine(ys, pos4, h1, mod3, mod_base, l2g, l2b, n_lat, seq):
    n_rows = h1.shape[0]
    tm = TM_POST
    tiles_per_seq = seq // tm
    n_lat_tiles = n_lat // tm
    n_groups_lat = n_lat // seq

    def group(i):
        return jnp.where(i < n_lat_tiles, i // tiles_per_seq, n_groups_lat)

    in_specs = [pl.BlockSpec((TOK_BLOCK, D_MODEL), lambda i: (i, 0)),
                pl.BlockSpec((tm, LANES), lambda i: (i, 0)),
                pl.BlockSpec((tm, D_MODEL), lambda i: (i, 0)),
                pl.BlockSpec((1, 1, D_MODEL), lambda i: (mod_base + group(i) * 6 + 5, 0, 0)),
                pl.BlockSpec((1, D_MODEL), lambda i: (0, 0)),
                pl.BlockSpec((1, D_MODEL), lambda i: (0, 0))]
    return pl.pallas_call(
        _combine_kernel,
        grid=(n_rows // tm,),
        in_specs=in_specs,
        out_specs=pl.BlockSpec((tm, D_MODEL), lambda i: (i, 0)),
        out_shape=jax.ShapeDtypeStruct((n_rows, D_MODEL), F32),
        compiler_params=_params(1),
        name="moe_combine",
    )(ys, pos4, h1, mod3, l2g.reshape(1, D_MODEL), l2b.reshape(1, D_MODEL))


def _block_diag(w):
    two, n, d, e = w.shape
    eye = jnp.eye(n, dtype=w.dtype)
    return (w[:, :, :, None, :] * eye[None, :, None, :, None]).reshape(two, n * d, n * e)


def kernel(x, c, ctx, c_ctx, w_mod, b_mod, w_in, b_in, na_rpb, w_proj_attn, w_proj_conv, w_proj_lru, sc_conv_w, lru_conv_w, lru_conv_b, lru_lambda, lru_w_r, lru_b_r, lru_w_i, lru_b_i, w_o, b_o, ln1_g, ln1_b, router_w, router_b, exp_w_gu, exp_b_gu, exp_w_dn, exp_b_dn, ln2_g, ln2_b):
    n_batch, seq, d = x.shape
    n_ctx = ctx.shape[1]
    n_lat = n_batch * seq
    n_all = n_lat + n_batch * n_ctx
    assert d == D_MODEL and n_batch + 1 <= SUBLANES

    cc = jnp.concatenate([c, c_ctx[None], jnp.zeros((SUBLANES - n_batch - 1, d), F32)], axis=0)
    mod = _modulation(cc, w_mod, b_mod)
    groups = n_batch + 1
    mod3 = mod.reshape(DEPTH, SUBLANES, 6, d)[:, :groups].reshape(DEPTH * groups * 6, 1, d)

    cos_t, sin_t = _make_rope(seq, TM_INPROJ)
    h = (x.reshape(n_lat, d), ctx.reshape(n_batch * n_ctx, d))

    for layer in range(DEPTH):
        last = layer == DEPTH - 1
        mod_base = layer * groups * 6
        w_in_bf = w_in[layer].astype(BF16)
        q, k, v, sb, rest = _input_projection(h, mod3, mod_base, w_in_bf, b_in[layer], N_EARLY,
                                              cos_t, sin_t, n_lat, seq)
        sp = jax.nn.softplus(-lru_lambda[layer])
        zb, hf, hb = _conv_scan(sb, rest, sc_conv_w[layer], lru_conv_w[layer], lru_conv_b[layer], sp,
                                _block_diag(lru_w_r[layer]).astype(BF16), _block_diag(lru_w_i[layer]).astype(BF16),
                                lru_b_r[layer], lru_b_i[layer], n_batch, seq, n_ctx)
        att = _attention(q, k, v, _attention_bias(na_rpb[layer], seq), n_batch, seq, n_ctx, not last)
        n_rows = n_lat if last else n_all
        rw_full = jnp.pad(router_w[layer], ((0, 0), (0, LANES - N_EXPERTS)))
        rw_hi = rw_full.astype(BF16)
        rw_pad = jnp.concatenate([rw_hi, (rw_full - rw_hi.astype(F32)).astype(BF16)], axis=1)
        rb_pad = jnp.concatenate([router_b[layer], jnp.full((LANES - N_EXPERTS,), NEG_BIG, F32)]).reshape(1, LANES)
        h1, xs, pos4, cnt_out = _post_mixer(
            h, att, zb, hf, hb, rest, mod3, mod_base, w_in_bf, b_in[layer],
            w_proj_attn[layer].astype(BF16), w_proj_conv[layer].astype(BF16), w_proj_lru[layer].astype(BF16),
            w_o[layer].astype(BF16), b_o[layer], ln1_g[layer], ln1_b[layer], rw_pad, rb_pad, n_rows, n_lat, seq)
        plan = _expert_plan(cnt_out, n_rows // TM_POST, n_rows)
        ys = _experts(xs, plan, layer, exp_w_gu, exp_b_gu, exp_w_dn, exp_b_dn)
        h = (_combine(ys, pos4, h1, mod3, mod_base, ln2_g[layer], ln2_b[layer], n_lat, seq),)
    return h[0].reshape(n_batch, seq, d)
```

```python
import functools

import numpy as np
import jax
import jax.numpy as jnp
from jax import lax
from jax.experimental import pallas as pl
from jax.experimental.pallas import tpu as pltpu

D_MODEL = 1024
DEPTH = 2
GRID_W = 64
NA_HEADS = 8
NA_HEAD_DIM = 64
NA_WIDTH = NA_HEADS * NA_HEAD_DIM
NA_WIN_ROWS = 8
NA_WIN_COLS = 16
ROPE_BASE = 10000.0
BRANCH_WIDTH = 512
LRU_BLOCKS = 8
LRU_C = 8.0
P_TOTAL = 7168
N_EARLY = 8 * BRANCH_WIDTH
N_EXPERTS = 32
TOP_K = 4
SWIGLU_LIMIT = 7.0
SWIGLU_ALPHA = 1.702
LN_EPS = 1e-5
DEEPNORM_ALPHA = (2 * DEPTH) ** 0.25
NEG_BIG = -1e30

LANES = 128
SUBLANES = 8
VMEM_LIMIT_BYTES = 56 * 1024 * 1024

TM_INPROJ = 1024
TN_INPROJ = 1024
SCAN_CHUNK = 256
ATT_ROWS = 4
ATT_QROWS = ATT_ROWS * GRID_W
TM_POST = 256
TM_EXPERT = 512
XS_WIDTH = D_MODEL + LANES
TOK_BLOCK = TM_POST * TOP_K + N_EXPERTS * SUBLANES
ZERO_ROWS = 256

F32 = jnp.float32
BF16 = jnp.bfloat16


def _params(n_axes):
    return pltpu.CompilerParams(dimension_semantics=("arbitrary",) * n_axes,
                                vmem_limit_bytes=VMEM_LIMIT_BYTES)


def _layer_norm(x):
    mu = jnp.mean(x, axis=-1, keepdims=True)
    xc = x - mu
    var = jnp.mean(xc * xc, axis=-1, keepdims=True)
    return xc * lax.rsqrt(var + LN_EPS)


def _mod_kernel(c_ref, w_ref, b_ref, o_ref):
    c = c_ref[...]
    s = (c * jax.nn.sigmoid(c)).astype(BF16)
    o_ref[0] = jnp.dot(s, w_ref[0].astype(BF16), preferred_element_type=F32) + b_ref[0]


def _modulation(cc, w_mod, b_mod):
    n_out = w_mod.shape[-1]
    return pl.pallas_call(
        _mod_kernel,
        grid=(DEPTH, n_out // D_MODEL),
        in_specs=[pl.BlockSpec((SUBLANES, D_MODEL), lambda l, j: (0, 0)),
                  pl.BlockSpec((1, D_MODEL, D_MODEL), lambda l, j: (l, 0, j)),
                  pl.BlockSpec((1, 1, D_MODEL), lambda l, j: (l, 0, j))],
        out_specs=pl.BlockSpec((1, SUBLANES, D_MODEL), lambda l, j: (l, 0, j)),
        out_shape=jax.ShapeDtypeStruct((DEPTH, SUBLANES, n_out), F32),
        compiler_params=_params(2),
        name="modulation",
    )(cc, w_mod, b_mod.reshape(DEPTH, 1, n_out))


def _rope_half(x, cos, sin_signed):
    lane = lax.broadcasted_iota(jnp.int32, (x.shape[0], LANES), 1)
    first = (lane % 32) < 16
    outs = []
    for cidx in range(x.shape[1] // LANES):
        xc = x[:, cidx * LANES:(cidx + 1) * LANES]
        partner = jnp.where(first, pltpu.roll(xc, LANES - 16, 1), pltpu.roll(xc, 16, 1))
        outs.append(xc * cos + partner * sin_signed)
    return jnp.concatenate(outs, axis=1)


def _inproj_kernel(ha_ref, hb_ref, sh_ref, sc_ref, w_ref, b_ref, cos_ref, sin_ref,
                   q_ref, k_ref, v_ref, sb_ref, rest_ref, xn_ref, *, split):
    i = pl.program_id(0)
    j = pl.program_id(1)

    def normalise(h_ref):
        y = _layer_norm(h_ref[...])
        xn_ref[...] = (y * (1.0 + sc_ref[0]) + sh_ref[0]).astype(BF16)

    @pl.when(jnp.logical_and(j == 0, i < split))
    def _():
        normalise(ha_ref)

    @pl.when(jnp.logical_and(j == 0, i >= split))
    def _():
        normalise(hb_ref)

    acc =jnp.dot(xn_ref[...], w_ref[...], preferred_element_type=F32) + b_ref[...]
    half = BRANCH_WIDTH

    @pl.when(j == 0)
    def _():
        cos = cos_ref[...]
        sin = sin_ref[...]
        q_ref[...] = _rope_half(acc[:, :half], cos, sin).astype(BF16)
        k_ref[...] = _rope_half(acc[:, half:], cos, sin).astype(BF16)

    @pl.when(j == 1)
    def _():
        v_ref[...] = acc[:, :half].astype(BF16)
        sb_ref[...] = acc[:, half:]

    @pl.when(j >= 2)
    def _():
        rest_ref[...] = acc


def _input_projection(h_parts, mod3, mod_base, w_in_bf, b_in, n_cols, cos_t, sin_t, n_lat, seq):
    tm, tn = TM_INPROJ, TN_INPROJ
    ha, hb = h_parts[0], h_parts[-1]
    split = ha.shape[0] // tm
    m = ha.shape[0] + (hb.shape[0] if len(h_parts) == 2 else 0)
    n_lat_tiles = n_lat // tm
    tiles_per_seq = seq // tm
    n_groups_lat = n_lat // seq

    def group(i):
        return jnp.where(i < n_lat_tiles, i // tiles_per_seq, n_groups_lat)

    def rope_blk(i):
        return jnp.where(i < n_lat_tiles, i % tiles_per_seq, tiles_per_seq)

    half = BRANCH_WIDTH
    return pl.pallas_call(
        functools.partial(_inproj_kernel, split=split),
        grid=(m // tm, n_cols // tn),
        in_specs=[pl.BlockSpec((tm, D_MODEL), lambda i, j: (jnp.minimum(i, split - 1), 0)),
                  pl.BlockSpec((tm, D_MODEL), lambda i, j: (jnp.maximum(i - split, 0), 0)),
                  pl.BlockSpec((1, 1, D_MODEL), lambda i, j: (mod_base + group(i) * 6 + 0, 0, 0)),
                  pl.BlockSpec((1, 1, D_MODEL), lambda i, j: (mod_base + group(i) * 6 + 1, 0, 0)),
                  pl.BlockSpec((D_MODEL, tn), lambda i, j: (0, j)),
                  pl.BlockSpec((1, tn), lambda i, j: (0, j)),
                  pl.BlockSpec((tm, LANES), lambda i, j: (rope_blk(i), 0)),
                  pl.BlockSpec((tm, LANES), lambda i, j: (rope_blk(i), 0))],
        out_specs=[pl.BlockSpec((tm, half), lambda i, j: (i, 0)),
                   pl.BlockSpec((tm, half), lambda i, j: (i, 0)),
                   pl.BlockSpec((tm, half), lambda i, j: (i, 0)),
                   pl.BlockSpec((tm, half), lambda i, j: (i, 0)),
                   pl.BlockSpec((tm, tn), lambda i, j: (i, jnp.maximum(j - 2, 0)))],
        out_shape=[jax.ShapeDtypeStruct((m, half), BF16),
                   jax.ShapeDtypeStruct((m, half), BF16),
                   jax.ShapeDtypeStruct((m, half), BF16),
                   jax.ShapeDtypeStruct((m, half), F32),
                   jax.ShapeDtypeStruct((m, n_cols - 2 * tn), F32)],
        scratch_shapes=[pltpu.VMEM((tm, D_MODEL), BF16)],
        compiler_params=_params(2),
        name="input_projection",
    )(ha, hb, mod3, mod3, w_in_bf, b_in.reshape(1, -1), cos_t, sin_t)


def _make_rope(seq, tm):
    t = np.arange(seq)
    row_pos, col_pos = t // GRID_W, t % GRID_W
    d = np.arange(LANES) % NA_HEAD_DIM
    m = NA_HEAD_DIM // 4
    inv_freq = (ROPE_BASE ** (-jnp.arange(m, dtype=F32) / m))[d % m]
    pos = np.where((d < 2 * m)[None, :], row_pos[:, None], col_pos[:, None])
    ang = jnp.asarray(pos).astype(F32) * inv_freq[None, :]
    cos = jnp.cos(ang)
    sin = jnp.sin(ang)
    sin_signed = jnp.where(jnp.asarray((d % (2 * m)) < m)[None, :], -sin, sin)
    cos = jnp.concatenate([cos, jnp.ones((tm, LANES), F32)], axis=0)
    sin_signed = jnp.concatenate([sin_signed, jnp.zeros((tm, LANES), F32)], axis=0)
    return cos, sin_signed


def _scan_kernel(fblk, bblk, first, last, seqb,
                 sb_ref, scg_ref, sx_ref, lxf_ref,
                 scgp_ref, sxp_ref, lxfp_ref, scgn_ref, sxn_ref, lxfn_ref,
                 lxb_ref, lxbp_ref, lxbn_ref,
                 scw_ref, cw_ref, cb_ref, sp_ref, wr_ref, wi_ref, br_ref, bi_ref,
                 zb_ref, hf_ref, hb_ref,
                 a_s, b_s, hc_s, st_s, *, ch, n_ctx_items):
    it = pl.program_id(0)
    is_first = first[it] == 1
    is_last = last[it] == 1
    is_ctx = it < n_ctx_items
    b = seqb[it]
    width = BRANCH_WIDTH
    row = lax.broadcasted_iota(jnp.int32, (ch, width), 0)
    not_first = jnp.where(is_first, 0.0, 1.0).astype(F32)
    not_last = jnp.where(is_last, 0.0, 1.0).astype(F32)

    def back1(u, prev_row):
        return jnp.where(row == 0, prev_row, pltpu.roll(u, 1, 0))

    def back2(u, prev2, prev1):
        return jnp.where(row == 0, prev2, jnp.where(row == 1, prev1, pltpu.roll(u, 2, 0)))

    def fwd1(u, next_row):
        return jnp.where(row == ch - 1, next_row, pltpu.roll(u, ch - 1, 0))

    def lru_input(lx_ref, lxp_ref, lxn_ref, prev_ok, next_ok):
        x = lx_ref[...]
        p = lxp_ref[...] * prev_ok
        n = lxn_ref[...] * next_ok
        return (cw_ref[0:1] * back2(x, p[6:7], p[7:8]) + cw_ref[1:2] * back1(x, p[7:8])
                + cw_ref[2:3] * x + cw_ref[3:4] * fwd1(x, n[0:1]) + cb_ref[...])

    def coeffs(d, xm):
        xb = xm.astype(BF16)
        r = jax.nn.sigmoid(jnp.dot(xb, wr_ref[d], preferred_element_type=F32) + br_ref[d:d + 1])
        g = jax.nn.sigmoid(jnp.dot(xb, wi_ref[d], preferred_element_type=F32) + bi_ref[d:d + 1])
        log_a = (-LRU_C * sp_ref[d:d + 1]) * r
        a = jnp.exp(log_a)
        a_s[d] = a
        b_s[d] = jnp.sqrt(-jnp.tanh(log_a) * (a * a + 1.0)) * (g * xm)

    u = scg_ref[...] * sx_ref[...]
    u_prev = scgp_ref[7:8] * sxp_ref[7:8] * not_first
    u_next = scgn_ref[0:1] * sxn_ref[0:1] * not_last
    conv = scw_ref[0:1] * back1(u, u_prev) + scw_ref[1:2] * u + scw_ref[2:3] * fwd1(u, u_next)
    zb_ref[...] = (sb_ref[...] * conv).astype(BF16)

    coeffs(0, lru_input(lxf_ref, lxfp_ref, lxfn_ref, not_first, not_last))
    coeffs(1, lru_input(lxb_ref, lxbp_ref, lxbn_ref, not_last, not_first))

    @pl.when(jnp.logical_and(is_first, is_ctx))
    def _():
        hc_s[...] = jnp.zeros_like(hc_s)

    @pl.when(jnp.logical_and(is_first, jnp.logical_not(is_ctx)))
    def _():
        hc_s[0:1] = st_s[pl.ds(2 * b, 1), :]
        hc_s[1:2] = st_s[pl.ds(2 * b + 1, 1), :]

    def body(i, carry):
        hf, hb = carry
        base = i * SUBLANES
        for r in range(SUBLANES):
            t = base + r
            hf = a_s[0, pl.ds(t, 1), :] * hf + b_s[0, pl.ds(t, 1), :]
            hf_ref[pl.ds(t, 1), :] = hf
            tb = ch - 1 - t
            hb = a_s[1, pl.ds(tb, 1), :] * hb + b_s[1, pl.ds(tb, 1), :]
            hb_ref[pl.ds(tb, 1), :] = hb
        return hf, hb

    hf, hb = lax.fori_loop(0, ch // SUBLANES, body, (hc_s[0:1], hc_s[1:2]))
    hc_s[0:1] = hf
    hc_s[1:2] = hb

    @pl.when(is_ctx)
    def _():
        st_s[pl.ds(2 * b, 1), :] = hf
        st_s[pl.ds(2 * b + 1, 1), :] = hb


def _scan_tables(n_batch, seq, n_ctx, ch):
    assert n_ctx == ch
    nc = seq // ch
    ctx0 = n_batch * seq // ch
    fblk, bblk, first, last, seqb = [], [], [], [], []
    for b in range(n_batch):
        fblk.append(ctx0 + b); bblk.append(ctx0 + b); first.append(1); last.append(1); seqb.append(b)
    for b in range(n_batch):
        for c in range(nc):
            fblk.append(b * nc + c); bblk.append(b * nc + nc - 1 - c)
            first.append(int(c == 0)); last.append(int(c == nc - 1)); seqb.append(b)
    return [np.asarray(a, np.int32) for a in (fblk, bblk, first, last, seqb)]


def _conv_scan(sb, rest, sc_w, lru_cw, lru_cb, sp, wr_bd, wi_bd, b_r, b_i, n_batch, seq, n_ctx):
    m = sb.shape[0]
    ch = SCAN_CHUNK
    width = BRANCH_WIDTH
    tables = _scan_tables(n_batch, seq, n_ctx, ch)
    n_items = len(tables[0])
    halo_per_chunk = ch // SUBLANES
    last_halo = m // SUBLANES - 1

    def cur(col, which):
        return pl.BlockSpec((ch, width), lambda i, f, bk, *_: ((f, bk)[which][i], col))

    def prev(col, which):
        return pl.BlockSpec((SUBLANES, width),
                            lambda i, f, bk, *_: (jnp.maximum((f, bk)[which][i] * halo_per_chunk - 1, 0), col))

    def nxt(col, which):
        return pl.BlockSpec((SUBLANES, width),
                            lambda i, f, bk, *_: (jnp.minimum(((f, bk)[which][i] + 1) * halo_per_chunk, last_halo), col))

    def full(shape):
        return pl.BlockSpec(shape, lambda i, *_: (0,) * len(shape))

    in_specs = [cur(0, 0), cur(0, 0), cur(1, 0), cur(2, 0),
                prev(0, 0), prev(1, 0), prev(2, 0), nxt(0, 0), nxt(1, 0), nxt(2, 0),
                cur(2, 1), prev(2, 1), nxt(2, 1),
                full(sc_w.shape), full(lru_cw.shape), full((1, width)), full(sp.shape),
                full(wr_bd.shape), full(wi_bd.shape), full(b_r.shape), full(b_i.shape)]
    out_specs = [cur(0, 0), cur(0, 0), cur(0, 1)]
    grid_spec = pltpu.PrefetchScalarGridSpec(
        num_scalar_prefetch=5, grid=(n_items,), in_specs=in_specs, out_specs=out_specs,
        scratch_shapes=[pltpu.VMEM((2, ch, width), F32), pltpu.VMEM((2, ch, width), F32),
                        pltpu.VMEM((SUBLANES, width), F32), pltpu.VMEM((2 * n_batch, width), F32)])
    return pl.pallas_call(
        functools.partial(_scan_kernel, ch=ch, n_ctx_items=n_batch),
        grid_spec=grid_spec,
        out_shape=[jax.ShapeDtypeStruct((m, width), BF16),
                   jax.ShapeDtypeStruct((m, width), F32),
                   jax.ShapeDtypeStruct((m, width), F32)],
        compiler_params=_params(1),
        name="conv_scan",
    )(*[jnp.asarray(t) for t in tables],
      sb, rest, rest, rest, rest, rest, rest, rest, rest, rest, rest, rest, rest,
      sc_w, lru_cw, lru_cb.reshape(1, width), sp, wr_bd, wi_bd, b_r, b_i)


def _attention_kernel(qblk, bat, r0t, cls, q_ref, k_ref, v_ref, kc_ref, vc_ref, *rest, band):
    bias_refs, o_ref = rest[:ATT_ROWS], rest[ATT_ROWS]
    it = pl.program_id(0)
    nq = GRID_W
    lane = lax.broadcasted_iota(jnp.int32, (nq, LANES), 1)
    low = lane < NA_HEAD_DIM
    scale = NA_HEAD_DIM ** -0.5
    nt = (((1,), (1,)), ((), ()))
    for row in range(ATT_ROWS):
        start = pl.multiple_of(r0t[it * ATT_ROWS + row] * GRID_W, GRID_W)
        bias_ref = bias_refs[row]
        qrows = slice(row * nq, (row + 1) * nq)
        for hp in range(NA_HEADS // 2):
            cols = slice(hp * LANES, (hp + 1) * LANES)
            qp = q_ref[qrows, cols].astype(F32) * scale
            qs = jnp.concatenate([jnp.where(low, qp, 0.0), jnp.where(low, 0.0, qp)], axis=0).astype(BF16)
            kb = k_ref[pl.ds(start, band), cols]
            vb = v_ref[pl.ds(start, band), cols]
            s_loc = lax.dot_general(qs, kb, nt, preferred_element_type=F32)
            s_ctx = lax.dot_general(qs, kc_ref[:, cols], nt, preferred_element_type=F32)
            bias = jnp.concatenate([bias_ref[0, 2 * hp], bias_ref[0, 2 * hp + 1]], axis=0)
            s_loc = s_loc + bias
            mx =jnp.maximum(jnp.max(s_loc, axis=-1, keepdims=True), jnp.max(s_ctx, axis=-1, keepdims=True))
            e_loc = jnp.exp(s_loc - mx)
            e_ctx = jnp.exp(s_ctx - mx)
            den = jnp.sum(e_loc, axis=-1, keepdims=True) + jnp.sum(e_ctx, axis=-1, keepdims=True)
            o = (jnp.dot(e_loc.astype(BF16), vb, preferred_element_type=F32)
                 + jnp.dot(e_ctx.astype(BF16), vc_ref[:, cols], preferred_element_type=F32)) / den
            o_ref[qrows, cols] = jnp.where(low, o[:nq], o[nq:]).astype(BF16)


def _attention_tables(n_batch, seq, n_ctx, with_ctx_queries):
    rows = seq // GRID_W
    kr = min(NA_WIN_ROWS, rows)
    assert rows % ATT_ROWS == 0 and n_ctx % ATT_QROWS == 0
    qblk, bat, r0t, cls = [], [], [], []
    ctx_q0 = n_batch * seq // ATT_QROWS
    for b in range(n_batch):
        for rg in range(rows // ATT_ROWS):
            qblk.append(b * (rows // ATT_ROWS) + rg); bat.append(b)
            for r in range(rg * ATT_ROWS, (rg + 1) * ATT_ROWS):
                r0 = min(max(r - kr // 2, 0), rows - kr)
                r0t.append(r0); cls.append(r - r0)
        if with_ctx_queries:
            for c in range(n_ctx // ATT_QROWS):
                qblk.append(ctx_q0 + b * (n_ctx // ATT_QROWS) + c); bat.append(b)
                r0t.extend([0] * ATT_ROWS); cls.extend([kr] * ATT_ROWS)
    return [np.asarray(a, np.int32) for a in (qblk, bat, r0t, cls)]


def _attention_bias(rpb, seq):
    rows = seq // GRID_W
    kr = min(NA_WIN_ROWS, rows)
    kc = NA_WIN_COLS
    cq = np.arange(GRID_W)
    c0 = np.clip(cq - kc // 2, 0, GRID_W - kc)
    ck = np.arange(GRID_W)
    inside = (ck[None, :] >= c0[:, None]) & (ck[None, :] < c0[:, None] + kc)
    dc = np.clip(ck[None, :] - cq[:, None] + (NA_WIN_COLS - 1), 0, 2 * NA_WIN_COLS - 2)
    n_dr = 2 * NA_WIN_ROWS - 1
    n_dc = 2 * NA_WIN_COLS - 1
    pick = jnp.asarray((np.arange(n_dc)[:, None] == dc.reshape(1, -1)).astype(np.float32))
    picked = jnp.dot(rpb.reshape(-1, n_dc), pick, precision=lax.Precision.HIGHEST)
    picked = picked.reshape(NA_HEADS, 2 * NA_WIN_ROWS - 1, GRID_W, GRID_W)
    table = jnp.where(jnp.asarray(inside)[None, None], picked, NEG_BIG)
    table = table.transpose(0, 2, 1, 3).reshape(NA_HEADS, GRID_W, n_dr * GRID_W)
    classes = []
    for cl in range(kr):
        lo = (NA_WIN_ROWS - 1 - cl) * GRID_W
        classes.append(table[:, :, lo:lo + kr * GRID_W])
    classes.append(jnp.full((NA_HEADS, GRID_W, kr * GRID_W), NEG_BIG, F32))
    return jnp.stack(classes, axis=0)


def _attention(q, k, v, bias, n_batch, seq, n_ctx, with_ctx_queries):
    m = q.shape[0] if with_ctx_queries else n_batch * seq
    rows = seq // GRID_W
    kr = min(NA_WIN_ROWS, rows)
    band = kr * GRID_W
    tables = _attention_tables(n_batch, seq, n_ctx, with_ctx_queries)
    n_items = len(tables[0])
    ctx_blk0 = n_batch * seq // n_ctx
    width = NA_WIDTH
    in_specs = [pl.BlockSpec((ATT_QROWS, width), lambda i, qb, bt, r0, cl: (qb[i], 0)),
                pl.BlockSpec((seq, width), lambda i, qb, bt, r0, cl: (bt[i], 0)),
                pl.BlockSpec((seq, width), lambda i, qb, bt, r0, cl: (bt[i], 0)),
                pl.BlockSpec((n_ctx, width), lambda i, qb, bt, r0, cl: (ctx_blk0 + bt[i], 0)),
                pl.BlockSpec((n_ctx, width), lambda i, qb, bt, r0, cl: (ctx_blk0 + bt[i], 0))]
    for row in range(ATT_ROWS):
        in_specs.append(pl.BlockSpec((1, NA_HEADS, GRID_W, band),
                                     lambda i, qb, bt, r0, cl, row=row: (cl[i * ATT_ROWS + row], 0, 0, 0)))
    out_specs = pl.BlockSpec((ATT_QROWS, width), lambda i, qb, bt, r0, cl: (qb[i], 0))
    grid_spec = pltpu.PrefetchScalarGridSpec(num_scalar_prefetch=4, grid=(n_items,),
                                             in_specs=in_specs, out_specs=out_specs)
    return pl.pallas_call(
        functools.partial(_attention_kernel, band=band),
        grid_spec=grid_spec,
        out_shape=jax.ShapeDtypeStruct((m, width), BF16),
        compiler_params=_params(1),
        name="attention",
    )(*[jnp.asarray(t) for t in tables], q, k, v, k, v, *([bias] * ATT_ROWS))


def _post_kernel(ha_ref, hc_ref, att_ref, zb_ref, hf_ref, hb_ref, lg_ref,
                 sh1_ref, sc1_ref, g1_ref, sh2_ref, sc2_ref,
                 wgl0_ref, wgl1_ref, wgl2_ref, bgl0_ref, bgl1_ref, bgl2_ref,
                 wpa_ref, wpc_ref, wpl_ref, wo_ref, bo_ref, l1g_ref, l1b_ref,
                 rw_ref, rb_ref, tri_ref, upper_ref,
                 h1_ref, xs_ref, pos_ref, cnt_ref, *, split):
    h = jnp.where(pl.program_id(0) < split, ha_ref[...], hc_ref[...])
    u1 = (_layer_norm(h) * (1.0 + sc1_ref[0]) + sh1_ref[0]).astype(BF16)
    y_a = jnp.dot(att_ref[...], wpa_ref[...], preferred_element_type=F32)
    y_b = jnp.dot(zb_ref[...], wpc_ref[...], preferred_element_type=F32)
    zc = jax.nn.gelu(lg_ref[...]) * (hf_ref[...] + hb_ref[...])
    y_c = jnp.dot(zc.astype(BF16), wpl_ref[...], preferred_element_type=F32)
    merged = (jax.nn.sigmoid(jnp.dot(u1, wgl0_ref[...], preferred_element_type=F32) + bgl0_ref[...]) * y_a
              + jax.nn.sigmoid(jnp.dot(u1, wgl1_ref[...], preferred_element_type=F32) + bgl1_ref[...]) * y_b
              + jax.nn.sigmoid(jnp.dot(u1, wgl2_ref[...], preferred_element_type=F32) + bgl2_ref[...]) * y_c)
    y = jnp.dot(merged.astype(BF16), wo_ref[...], preferred_element_type=F32) + bo_ref[...]
    h1 = _layer_norm(DEEPNORM_ALPHA * h + g1_ref[0] * y) * l1g_ref[...] + l1b_ref[...]
    h1_ref[...] = h1
    u2 = _layer_norm(h1) * (1.0 + sc2_ref[0]) + sh2_ref[0]

    u_hi = u2.astype(BF16)
    u_lo = (u2 - u_hi.astype(F32)).astype(BF16)
    by_hi = jnp.dot(u_hi, rw_ref[...], preferred_element_type=F32)
    logits = (by_hi[:, :LANES] + by_hi[:, LANES:]
              + jnp.dot(u_lo, rw_ref[:, :LANES], preferred_element_type=F32) + rb_ref[...])
    tm = logits.shape[0]
    lane = lax.broadcasted_iota(jnp.int32, (tm, LANES), 1)
    lane_f = lane.astype(F32)
    work = logits
    tops, idxs, hots = [], [], []
    for _ in range(TOP_K):
        mx = jnp.max(work, axis=-1, keepdims=True)
        idx = jnp.min(jnp.where(work == mx, lane_f, float(LANES)), axis=-1, keepdims=True)
        hot = lane_f == idx
        work = jnp.where(hot, -3e38, work)
        tops.append(mx); idxs.append(idx); hots.append(hot)
    exps = [jnp.exp(t - tops[0]) for t in tops]
    den = exps[0] + exps[1] + exps[2] + exps[3]
    hot_all = jnp.zeros((tm, LANES), F32)
    for hot in hots:
        hot_all = hot_all + hot.astype(F32)
    cnt = jnp.sum(hot_all, axis=0, keepdims=True)
    cnt_pad = jnp.floor((cnt + (SUBLANES - 1.0)) * (1.0 / SUBLANES)) * SUBLANES
    off = jnp.dot(jnp.broadcast_to(cnt_pad, (SUBLANES, LANES)), upper_ref[...],
                  precision=lax.Precision.HIGHEST, preferred_element_type=F32)[0:1]
    slot = off + jnp.dot(tri_ref[...], hot_all.astype(BF16), preferred_element_type=F32)
    pos4 = jnp.zeros((tm, LANES), F32)
    w_tile = jnp.zeros((tm, LANES), F32)
    for kk in range(TOP_K):
        pos_k = jnp.sum(jnp.where(hots[kk], slot, 0.0), axis=-1, keepdims=True)
        pos4 = jnp.where(lane == kk, pos_k, pos4)
        p = exps[kk] / den
        p_hi = p.astype(BF16).astype(F32)
        p_mid = (p - p_hi).astype(BF16).astype(F32)
        p_lo = p - p_hi - p_mid
        w_tile = jnp.where(hots[kk], p_hi, w_tile)
        w_tile = jnp.where(lane_f == idxs[kk] + float(N_EXPERTS), p_mid, w_tile)
        w_tile = jnp.where(lane_f == idxs[kk] + float(2 * N_EXPERTS), p_lo, w_tile)
    pos_ref[...] = pos4
    sub = lax.broadcasted_iota(jnp.int32, (SUBLANES, LANES), 0)
    cnt_ref[...] = jnp.where(sub == 0, cnt, jnp.where(sub == 1, off, 0.0))

    n_sorted = xs_ref.shape[0]
    pos_t = pos4.T
    r_iota = lax.broadcasted_iota(jnp.int32, (n_sorted, tm), 0).astype(F32)
    hit = r_iota == pos_t[0:1, :]
    for kk in range(1, TOP_K):
        hit = jnp.logical_or(hit, r_iota == pos_t[kk:kk + 1, :])
    perm = jnp.where(hit, 1.0, 0.0).astype(BF16)
    feats = jnp.concatenate([u_hi, w_tile.astype(BF16)], axis=1)
    xs_ref[...] = jnp.dot(perm, feats, preferred_element_type=F32)


def _post_mixer(h_parts, att, zb, hf, hb, rest, mod3, mod_base, w_in_bf, b_in, wpa, wpc, wpl, wo, b_o, l1g, l1b,
                rw_pad, rb_pad, n_rows, n_lat, seq):
    tm = TM_POST
    width = BRANCH_WIDTH
    ha, hc = h_parts[0], h_parts[-1]
    split = min(ha.shape[0], n_rows) // tm
    gate_col0 = N_EARLY // D_MODEL
    tiles_per_seq = seq // tm
    n_lat_tiles = n_lat // tm
    n_groups_lat = n_lat // seq
    tri = jnp.asarray(np.tril(np.ones((tm, tm), np.float32), -1), BF16)
    upper = jnp.asarray(np.triu(np.ones((LANES, LANES), np.float32), 1))
    n_tiles = n_rows // tm

    def group(i):
        return jnp.where(i < n_lat_tiles, i // tiles_per_seq, n_groups_lat)

    def rows(wd, col=0):
        return pl.BlockSpec((tm, wd), lambda i: (i, col))

    def full(shape):
        return pl.BlockSpec(shape, lambda i: (0,) * len(shape))

    def mod(which):
        return pl.BlockSpec((1, 1, D_MODEL), lambda i: (mod_base + group(i) * 6 + which, 0, 0))

    in_specs = [pl.BlockSpec((tm, D_MODEL), lambda i: (jnp.minimum(i, split - 1), 0)),
                pl.BlockSpec((tm, D_MODEL), lambda i: (jnp.maximum(i - split, 0), 0)),
                rows(width), rows(width), rows(width), rows(width), rows(width, 3),
                mod(0), mod(1), mod(2), mod(3), mod(4)]
    in_specs += [pl.BlockSpec((D_MODEL, D_MODEL), lambda i, c=c: (0, gate_col0 + c)) for c in range(3)]
    in_specs += [pl.BlockSpec((1, D_MODEL), lambda i, c=c: (0, gate_col0 + c)) for c in range(3)]
    in_specs += [full(wpa.shape), full(wpc.shape), full(wpl.shape), full(wo.shape), full((1, D_MODEL)),
                full((1, D_MODEL)), full((1, D_MODEL)), full(rw_pad.shape), full(rb_pad.shape), full(tri.shape),
                full(upper.shape)]
    out_specs = [rows(D_MODEL), pl.BlockSpec((TOK_BLOCK, XS_WIDTH), lambda i: (i, 0)), rows(LANES),
                 pl.BlockSpec((SUBLANES, LANES), lambda i: (i, 0))]
    return pl.pallas_call(
        functools.partial(_post_kernel, split=split),
        grid=(n_tiles,),
        in_specs=in_specs, out_specs=out_specs,
        out_shape=[jax.ShapeDtypeStruct((n_rows, D_MODEL), F32),
                   jax.ShapeDtypeStruct((n_tiles * TOK_BLOCK, XS_WIDTH), F32),
                   jax.ShapeDtypeStruct((n_rows, LANES), F32),
                   jax.ShapeDtypeStruct((n_tiles * SUBLANES, LANES), F32)],
        compiler_params=_params(1),
        name="post_mixer",
    )(ha, hc, att, zb, hf, hb, rest, mod3, mod3, mod3, mod3, mod3,
      w_in_bf, w_in_bf, w_in_bf, b_in.reshape(1, -1), b_in.reshape(1, -1), b_in.reshape(1, -1),
      wpa, wpc, wpl, wo, b_o.reshape(1, D_MODEL), l1g.reshape(1, D_MODEL), l1b.reshape(1, D_MODEL),
      rw_pad, rb_pad, tri, upper)


def _expert_plan(cnt_out, n_tok_tiles, n_rows):
    tm = TM_EXPERT
    co = cnt_out.reshape(n_tok_tiles, SUBLANES, LANES)
    cnt = (co[:, 0, :N_EXPERTS].astype(jnp.int32) + SUBLANES - 1) // SUBLANES * SUBLANES
    off = co[:, 1, :N_EXPERTS].astype(jnp.int32)
    cum_end = jnp.cumsum(cnt, axis=0)
    cum = cum_end - cnt
    total = cum_end[-1]
    n_et = (total + tm - 1) // tm
    et_end = jnp.cumsum(n_et)
    n_act = et_end[-1:].astype(jnp.int32)
    n_tiles = -(-n_tok_tiles * TOK_BLOCK // tm) + N_EXPERTS
    j = jnp.arange(n_tiles, dtype=jnp.int32)
    tile_e = jnp.minimum(jnp.sum((et_end[None, :] <= j[:, None]).astype(jnp.int32), axis=1), N_EXPERTS - 1)
    pick_e = (tile_e[:, None] == jnp.arange(N_EXPERTS, dtype=jnp.int32)[None, :]).astype(F32)

    def per_tile(table):
        return jnp.dot(pick_e, table.astype(F32), precision=lax.Precision.HIGHEST).astype(jnp.int32)

    row0 = (j - per_tile(et_end - n_et)) * tm
    n_rows_tile = jnp.clip(per_tile(total) - row0, 0, tm)
    cum_e = per_tile(cum.T)
    cum_end_e = per_tile(cum_end.T)
    delta_e = per_tile((jnp.arange(n_tok_tiles, dtype=jnp.int32)[:, None] * TOK_BLOCK + off - cum).T)
    q = row0[:, None] + SUBLANES * jnp.arange(tm // SUBLANES, dtype=jnp.int32)[None, :]
    inside = jnp.logical_and(cum_e.T[:, :, None] <= q[None], q[None] < cum_end_e.T[:, :, None])
    src = q + jnp.sum(jnp.where(inside, delta_e.T[:, :, None], 0), axis=0)
    used = off[:, N_EXPERTS - 1] + cnt[:, N_EXPERTS - 1]
    first = jnp.concatenate([jnp.ones((1,), jnp.int32), (tile_e[1:] != tile_e[:-1]).astype(jnp.int32)])
    group = jnp.cumsum(first) - 1
    after = per_tile(et_end)
    next_e = jnp.where(after < n_act[0],
                       jnp.minimum(jnp.sum((et_end[None, :] <= after[:, None]).astype(jnp.int32), axis=1),
                                   N_EXPERTS - 1), -1)
    return (tile_e.astype(jnp.int32), n_rows_tile.astype(jnp.int32), n_act, src.reshape(-1).astype(jnp.int32),
            used.astype(jnp.int32), first, group.astype(jnp.int32), next_e.astype(jnp.int32), n_tiles)


def _expert_kernel(tile_e, n_rows_t, n_act, src_t, used_t, first_t, group_t, next_t,
                   xs_hbm, wgu_hbm, bgu_ref, wdn_hbm, bdn_ref, ys_hbm,
                   xin, yout, wgu_f, wdn_f, wgu_s, wdn_s, zeros, sem_in, sem_out, sem_zero, sem_wgu, sem_wdn,
                   *, tm, tok_block, n_tok_tiles, layer):
    j = pl.program_id(0)
    na = n_act[0]
    chunks = tm // SUBLANES

    def weight_copies(e, wslot):
        return (pltpu.make_async_copy(wgu_hbm.at[layer, e], wgu_f.at[wslot], sem_wgu.at[wslot]),
                pltpu.make_async_copy(wdn_hbm.at[layer, e], wdn_f.at[wslot], sem_wdn.at[wslot]))

    def gather(src, dst, size, slot):
        return pltpu.make_async_copy(xs_hbm.at[pl.ds(src, size)], xin.at[slot, pl.ds(dst, size)], sem_in.at[slot])

    def scatter(src, dst, size, slot):
        return pltpu.make_async_copy(yout.at[slot, pl.ds(dst, size)], ys_hbm.at[pl.ds(src, size)], sem_out.at[slot])

    def start_chunks(jj, slot, copy):
        def one(c, priority):
            src = pl.multiple_of(src_t[jj * chunks + c], SUBLANES)
            copy(src, pl.multiple_of(c * SUBLANES, SUBLANES), SUBLANES, slot).start(priority=priority)

        def body(c, carry):
            one(c, 0)
            return carry

        def body_unrolled(g, carry):
            for u in range(SUBLANES):
                one(g * SUBLANES + u, u % 2)
            return carry

        n = lax.shift_right_logical(n_rows_t[jj], 3)

        @pl.when(n == chunks)
        def _():
            lax.fori_loop(0, chunks // SUBLANES, body_unrolled, 0)

        @pl.when(n != chunks)
        def _():
            lax.fori_loop(0, n, body, 0)

    def wait_rows(jj, slot, copy):
        n = n_rows_t[jj]
        size = tm
        while size >= SUBLANES:
            @pl.when((n & size) != 0)
            def _(size=size):
                copy(0, 0, size, slot).wait()
            size //= 2

    slot = j % 2

    @pl.when(j == 0)
    def _():
        xin[...] = jnp.zeros_like(xin)
        start_chunks(0, 0, gather)
        zeros[...] = jnp.zeros_like(zeros)

        def clear_tail(i, copy_op):
            used = used_t[i]
            tail = tok_block - used
            for size in (256, 128, 64, 32, 16, 8):
                @pl.when((tail & size) != 0)
                def _():
                    at = pl.multiple_of(i * tok_block + used + (tail & ~(2 * size - 1)), SUBLANES)
                    copy_op(pltpu.make_async_copy(zeros.at[pl.ds(0, size)], ys_hbm.at[pl.ds(at, size)], sem_zero))

        def start_clear(i, c):
            clear_tail(i, lambda cp: cp.start())
            return c

        def wait_clear(i, c):
            clear_tail(i, lambda cp: cp.wait())
            return c

        lax.fori_loop(0, n_tok_tiles, start_clear, 0)
        lax.fori_loop(0, n_tok_tiles, wait_clear, 0)

    @pl.when(j + 1 < na)
    def _():
        start_chunks(j + 1, 1 - slot, gather)

    @pl.when(j < na)
    def _():
        e = tile_e[j]

        @pl.when(first_t[j] == 1)
        def _():
            wslot = group_t[j] % 2

            @pl.when(j == 0)
            def _():
                for cp in weight_copies(e, wslot):
                    cp.start()

            for cp in weight_copies(e, wslot):
                cp.wait()
            wgu_s[...] = wgu_f[wslot].astype(BF16)
            wdn_s[...] = wdn_f[wslot].astype(BF16)

            @pl.when(next_t[j] >= 0)
            def _():
                for cp in weight_copies(next_t[j], 1 - wslot):
                    cp.start()

        wait_rows(j, slot, gather)

        def compute(rows):
            x = xin[slot, pl.ds(0, rows)]
            lane = lax.broadcasted_iota(jnp.int32, (rows, LANES), 1)
            p = jnp.sum(jnp.where(lane % N_EXPERTS == e, x[:, D_MODEL:], 0.0), axis=-1, keepdims=True)
            gu = jnp.dot(x[:, :D_MODEL].astype(BF16), wgu_s[...], preferred_element_type=F32) + bgu_ref[0, 0]
            f = gu.shape[1] // 2
            gate = jnp.minimum(gu[:, :f], SWIGLU_LIMIT)
            up = jnp.clip(gu[:, f:], -SWIGLU_LIMIT, SWIGLU_LIMIT)
            hid = (up + 1.0) * gate * jax.nn.sigmoid(SWIGLU_ALPHA * gate)
            y = jnp.dot(hid.astype(BF16), wdn_s[...], preferred_element_type=F32) + bdn_ref[0, 0]
            yout[slot, pl.ds(0, rows)] = y * p

        @pl.when(n_rows_t[j] > tm // 2)
        def _():
            compute(tm)

        @pl.when(n_rows_t[j] <= tm // 2)
        def _():
            compute(tm // 2)

        start_chunks(j, slot, scatter)

        @pl.when(j >= 1)
        def _():
            wait_rows(j - 1, 1 - slot, scatter)

        @pl.when(j == na - 1)
        def _():
            wait_rows(j, slot, scatter)


def _experts(xs, plan, layer, w_gu, b_gu, w_dn, b_dn):
    tile_e, n_rows_tile, n_act, src, used, first, group, next_e, n_tiles = plan
    tm = TM_EXPERT
    f2 = w_gu.shape[-1]

    def expert(j, te, nr, na, *_):
        return (layer, te[jnp.minimum(j, na[0] - 1)], 0, 0)

    in_specs = [pl.BlockSpec(memory_space=pl.ANY),
                pl.BlockSpec(memory_space=pl.ANY),
                pl.BlockSpec((1, 1, 1, f2), expert),
                pl.BlockSpec(memory_space=pl.ANY),
                pl.BlockSpec((1, 1, 1, D_MODEL), expert)]
    grid_spec = pltpu.PrefetchScalarGridSpec(
        num_scalar_prefetch=8, grid=(n_tiles,), in_specs=in_specs,
        out_specs=pl.BlockSpec(memory_space=pl.ANY),
        scratch_shapes=[pltpu.VMEM((2, tm, XS_WIDTH), F32), pltpu.VMEM((2, tm, D_MODEL), F32),
                        pltpu.VMEM((2, D_MODEL, f2), F32), pltpu.VMEM((2, f2 // 2, D_MODEL), F32),
                        pltpu.VMEM((D_MODEL, f2), BF16), pltpu.VMEM((f2 // 2, D_MODEL), BF16),
                        pltpu.VMEM((ZERO_ROWS, D_MODEL), F32),
                        pltpu.SemaphoreType.DMA((2,)), pltpu.SemaphoreType.DMA((2,)), pltpu.SemaphoreType.DMA(()),
                        pltpu.SemaphoreType.DMA((2,)), pltpu.SemaphoreType.DMA((2,))])
    return pl.pallas_call(
        functools.partial(_expert_kernel, tm=tm, tok_block=TOK_BLOCK, n_tok_tiles=xs.shape[0] // TOK_BLOCK,
                          layer=layer),
        grid_spec=grid_spec,
        out_shape=jax.ShapeDtypeStruct((xs.shape[0], D_MODEL), F32),
        compiler_params=_params(1),
        name="moe_experts",
    )(tile_e, n_rows_tile, n_act, src, used, first, group, next_e,
      xs, w_gu, b_gu.reshape(DEPTH, N_EXPERTS, 1, f2), w_dn, b_dn.reshape(DEPTH, N_EXPERTS, 1, D_MODEL))


def _combine_kernel(ys_ref, pos_ref, h1_ref, g2_ref, l2g_ref, l2b_ref, o_ref):
    tm = pos_ref.shape[0]
    n_sorted = ys_ref.shape[0]
    pos = pos_ref[...]
    col = lax.broadcasted_iota(jnp.int32, (tm, n_sorted), 1).astype(F32)
    sel = jnp.zeros((tm, n_sorted), F32)
    for kk in range(TOP_K):
        sel = sel + (col == pos[:, kk:kk + 1]).astype(F32)
    sel = sel.astype(BF16)
    ys = ys_ref[...]
    hi = ys.astype(BF16)
    rest = ys - hi.astype(F32)
    mid = rest.astype(BF16)
    lo = (rest - mid.astype(F32)).astype(BF16)
    y2 = (jnp.dot(sel, hi, preferred_element_type=F32) + jnp.dot(sel, mid, preferred_element_type=F32)
          + jnp.dot(sel, lo, preferred_element_type=F32))
    o_ref[...] = _layer_norm(DEEPNORM_ALPHA * h1_ref[...] + g2_ref[0] * y2) * l2g_ref[...] + l2b_ref[...]


def _combine(ys, pos4, h1, mod3, mod_base, l2g, l2b, n_lat, seq):
    n_rows = h1.shape[0]
    tm = TM_POST
    tiles_per_seq = seq // tm
    n_lat_tiles = n_lat // tm
    n_groups_lat = n_lat // seq

    def group(i):
        return jnp.where(i < n_lat_tiles, i // tiles_per_seq, n_groups_lat)

    in_specs = [pl.BlockSpec((TOK_BLOCK, D_MODEL), lambda i: (i, 0)),
                pl.BlockSpec((tm, LANES), lambda i: (i, 0)),
                pl.BlockSpec((tm, D_MODEL), lambda i: (i, 0)),
                pl.BlockSpec((1, 1, D_MODEL), lambda i: (mod_base + group(i) * 6 + 5, 0, 0)),
                pl.BlockSpec((1, D_MODEL), lambda i: (0, 0)),
                pl.BlockSpec((1, D_MODEL), lambda i: (0, 0))]
    return pl.pallas_call(
        _combine_kernel,
        grid=(n_rows // tm,),
        in_specs=in_specs,
        out_specs=pl.BlockSpec((tm, D_MODEL), lambda i: (i, 0)),
        out_shape=jax.ShapeDtypeStruct((n_rows, D_MODEL), F32),
        compiler_params=_params(1),
        name="moe_combine",
    )(ys, pos4, h1, mod3, l2g.reshape(1, D_MODEL), l2b.reshape(1, D_MODEL))


def _block_diag(w):
    two, n, d, e = w.shape
    eye = jnp.eye(n, dtype=w.dtype)
    return (w[:, :, :, None, :] * eye[None, :, None, :, None]).reshape(two, n * d, n * e)


def kernel(x, c, ctx, c_ctx, w_mod, b_mod, w_in, b_in, na_rpb, w_proj_attn, w_proj_conv, w_proj_lru, sc_conv_w, lru_conv_w, lru_conv_b, lru_lambda, lru_w_r, lru_b_r, lru_w_i, lru_b_i, w_o, b_o, ln1_g, ln1_b, router_w, router_b, exp_w_gu, exp_b_gu, exp_w_dn, exp_b_dn, ln2_g, ln2_b):
    n_batch, seq, d = x.shape
    n_ctx = ctx.shape[1]
    n_lat = n_batch * seq
    n_all = n_lat + n_batch * n_ctx
    assert d == D_MODEL and n_batch + 1 <= SUBLANES

    cc = jnp.concatenate([c, c_ctx[None], jnp.zeros((SUBLANES - n_batch - 1, d), F32)], axis=0)
    mod = _modulation(cc, w_mod, b_mod)
    groups = n_batch + 1
    mod3 = mod.reshape(DEPTH, SUBLANES, 6, d)[:, :groups].reshape(DEPTH * groups * 6, 1, d)

    cos_t, sin_t = _make_rope(seq, TM_INPROJ)
    h = (x.reshape(n_lat, d), ctx.reshape(n_batch * n_ctx, d))

    for layer in range(DEPTH):
        last = layer == DEPTH - 1
        mod_base = layer * groups * 6
        w_in_bf = w_in[layer].astype(BF16)
        q, k, v, sb, rest = _input_projection(h, mod3, mod_base, w_in_bf, b_in[layer], N_EARLY,
                                              cos_t, sin_t, n_lat, seq)
        sp = jax.nn.softplus(-lru_lambda[layer])
        zb, hf, hb = _conv_scan(sb, rest, sc_conv_w[layer], lru_conv_w[layer], lru_conv_b[layer], sp,
                                _block_diag(lru_w_r[layer]).astype(BF16), _block_diag(lru_w_i[layer]).astype(BF16),
                                lru_b_r[layer], lru_b_i[layer], n_batch, seq, n_ctx)
        att = _attention(q, k, v, _attention_bias(na_rpb[layer], seq), n_batch, seq, n_ctx, not last)
        n_rows = n_lat if last else n_all
        rw_full = jnp.pad(router_w[layer], ((0, 0), (0, LANES - N_EXPERTS)))
        rw_hi = rw_full.astype(BF16)
        rw_pad = jnp.concatenate([rw_hi, (rw_full - rw_hi.astype(F32)).astype(BF16)], axis=1)
        rb_pad = jnp.concatenate([router_b[layer], jnp.full((LANES - N_EXPERTS,), NEG_BIG, F32)]).reshape(1, LANES)
        h1, xs, pos4, cnt_out = _post_mixer(
            h, att, zb, hf, hb, rest, mod3, mod_base, w_in_bf, b_in[layer],
            w_proj_attn[layer].astype(BF16), w_proj_conv[layer].astype(BF16), w_proj_lru[layer].astype(BF16),
            w_o[layer].astype(BF16), b_o[layer], ln1_g[layer], ln1_b[layer], rw_pad, rb_pad, n_rows, n_lat, seq)
        plan = _expert_plan(cnt_out, n_rows // TM_POST, n_rows)
        ys = _experts(xs, plan, layer, exp_w_gu, exp_b_gu, exp_w_dn, exp_b_dn)
        h = (_combine(ys, pos4, h1, mod3, mod_base, ln2_g[layer], ln2_b[layer], n_lat, seq),)
    return h[0].reshape(n_batch, seq, d)
```

```python
import functools

import numpy as np
import jax
import jax.numpy as jnp
from jax import lax
from jax.experimental import pallas as pl
from jax.experimental.pallas import tpu as pltpu

D_MODEL = 1024
DEPTH = 2
GRID_W = 64
NA_HEADS = 8
NA_HEAD_DIM = 64
NA_WIDTH = NA_HEADS * NA_HEAD_DIM
NA_WIN_ROWS = 8
NA_WIN_COLS = 16
ROPE_BASE = 10000.0
BRANCH_WIDTH = 512
LRU_BLOCKS = 8
LRU_C = 8.0
P_TOTAL = 7168
N_EARLY = 8 * BRANCH_WIDTH
N_EXPERTS = 32
TOP_K = 4
SWIGLU_LIMIT = 7.0
SWIGLU_ALPHA = 1.702
LN_EPS = 1e-5
DEEPNORM_ALPHA = (2 * DEPTH) ** 0.25
NEG_BIG = -1e30

LANES = 128
SUBLANES = 8
VMEM_LIMIT_BYTES = 56 * 1024 * 1024

TM_INPROJ = 512
SCAN_CHUNK = 256
ATT_ROWS = 4
ATT_QROWS = ATT_ROWS * GRID_W
TM_POST = 256
TM_EXPERT = 512
XS_WIDTH = D_MODEL + LANES
TOK_BLOCK = TM_POST * TOP_K + N_EXPERTS * SUBLANES
ZERO_ROWS = 256

F32 = jnp.float32
BF16 = jnp.bfloat16


def _params(n_axes):
    return pltpu.CompilerParams(dimension_semantics=("arbitrary",) * n_axes,
                                vmem_limit_bytes=VMEM_LIMIT_BYTES)


def _layer_norm(x):
    mu = jnp.mean(x, axis=-1, keepdims=True)
    xc = x - mu
    var = jnp.mean(xc * xc, axis=-1, keepdims=True)
    return xc * lax.rsqrt(var + LN_EPS)


def _mod_kernel(c_ref, w_ref, b_ref, o_ref):
    c = c_ref[...]
    s = (c * jax.nn.sigmoid(c)).astype(BF16)
    o_ref[0] = jnp.dot(s, w_ref[0].astype(BF16), preferred_element_type=F32) + b_ref[0]


def _modulation(cc, w_mod, b_mod):
    n_out = w_mod.shape[-1]
    return pl.pallas_call(
        _mod_kernel,
        grid=(DEPTH, n_out // D_MODEL),
        in_specs=[pl.BlockSpec((SUBLANES, D_MODEL), lambda l, j: (0, 0)),
                  pl.BlockSpec((1, D_MODEL, D_MODEL), lambda l, j: (l, 0, j)),
                  pl.BlockSpec((1, 1, D_MODEL), lambda l, j: (l, 0, j))],
        out_specs=pl.BlockSpec((1, SUBLANES, D_MODEL), lambda l, j: (l, 0, j)),
        out_shape=jax.ShapeDtypeStruct((DEPTH, SUBLANES, n_out), F32),
        compiler_params=_params(2),
        name="modulation",
    )(cc, w_mod, b_mod.reshape(DEPTH, 1, n_out))


def _rope_half(x, cos, sin_signed):
    lane = lax.broadcasted_iota(jnp.int32, (x.shape[0], LANES), 1)
    first = (lane % 32) < 16
    outs = []
    for cidx in range(x.shape[1] // LANES):
        xc = x[:, cidx * LANES:(cidx + 1) * LANES]
        partner = jnp.where(first, pltpu.roll(xc, LANES - 16, 1), pltpu.roll(xc, 16, 1))
        outs.append(xc * cos + partner * sin_signed)
    return jnp.concatenate(outs, axis=1)


def _inproj_kernel(ha_ref, hb_ref, sh_ref, sc_ref, w_ref, b_ref, cos_ref, sin_ref,
                   q_ref, k_ref, v_ref, sb_ref, rest_ref, *, split):
    h = jnp.where(pl.program_id(0) < split, ha_ref[...], hb_ref[...])
    xn = (_layer_norm(h) * (1.0 + sc_ref[0]) + sh_ref[0]).astype(BF16)
    half = BRANCH_WIDTH

    def columns(lo, width):
        return jnp.dot(xn, w_ref[:, lo:lo + width], preferred_element_type=F32) + b_ref[:, lo:lo + width]

    cos = cos_ref[...]
    sin = sin_ref[...]
    q_ref[...] = _rope_half(columns(0, half), cos, sin).astype(BF16)
    k_ref[...] = _rope_half(columns(half, half), cos, sin).astype(BF16)
    v_ref[...] = columns(2 * half, half).astype(BF16)
    sb_ref[...] = columns(3 * half, half)
    n_rest = rest_ref.shape[1]
    for lo in range(0, n_rest, 2 * half):
        rest_ref[:, lo:lo + 2 * half] = columns(4 * half + lo, 2 * half)


def _input_projection(h_parts, mod3, mod_base, w_in_bf, b_in, n_cols, cos_t, sin_t, n_lat, seq):
    tm = TM_INPROJ
    ha, hb = h_parts[0], h_parts[-1]
    split = ha.shape[0] // tm
    m = ha.shape[0] + (hb.shape[0] if len(h_parts) == 2 else 0)
    n_lat_tiles = n_lat // tm
    tiles_per_seq = seq // tm
    n_groups_lat = n_lat // seq

    def group(i):
        return jnp.where(i < n_lat_tiles, i // tiles_per_seq, n_groups_lat)

    def rope_blk(i):
        return jnp.where(i < n_lat_tiles, i % tiles_per_seq, tiles_per_seq)

    half = BRANCH_WIDTH
    return pl.pallas_call(
        functools.partial(_inproj_kernel, split=split),
        grid=(m // tm,),
        in_specs=[pl.BlockSpec((tm, D_MODEL), lambda i: (jnp.minimum(i, split - 1), 0)),
                  pl.BlockSpec((tm, D_MODEL), lambda i: (jnp.maximum(i - split, 0), 0)),
                  pl.BlockSpec((1, 1, D_MODEL), lambda i: (mod_base + group(i) * 6 + 0, 0, 0)),
                  pl.BlockSpec((1, 1, D_MODEL), lambda i: (mod_base + group(i) * 6 + 1, 0, 0)),
                  pl.BlockSpec((D_MODEL, n_cols), lambda i: (0, 0)),
                  pl.BlockSpec((1, n_cols), lambda i: (0, 0)),
                  pl.BlockSpec((tm, LANES), lambda i: (rope_blk(i), 0)),
                  pl.BlockSpec((tm, LANES), lambda i: (rope_blk(i), 0))],
        out_specs=[pl.BlockSpec((tm, half), lambda i: (i, 0)),
                   pl.BlockSpec((tm, half), lambda i: (i, 0)),
                   pl.BlockSpec((tm, half), lambda i: (i, 0)),
                   pl.BlockSpec((tm, half), lambda i: (i, 0)),
                   pl.BlockSpec((tm, n_cols - 4 * half), lambda i: (i, 0))],
        out_shape=[jax.ShapeDtypeStruct((m, half), BF16),
                   jax.ShapeDtypeStruct((m, half), BF16),
                   jax.ShapeDtypeStruct((m, half), BF16),
                   jax.ShapeDtypeStruct((m, half), F32),
                   jax.ShapeDtypeStruct((m, n_cols - 4 * half), F32)],
        compiler_params=_params(1),
        name="input_projection",
    )(ha, hb, mod3, mod3, w_in_bf, b_in.reshape(1, -1), cos_t, sin_t)


def _make_rope(seq, tm):
    t = np.arange(seq)
    row_pos, col_pos = t // GRID_W, t % GRID_W
    d = np.arange(LANES) % NA_HEAD_DIM
    m = NA_HEAD_DIM // 4
    inv_freq = (ROPE_BASE ** (-jnp.arange(m, dtype=F32) / m))[d % m]
    pos = np.where((d < 2 * m)[None, :], row_pos[:, None], col_pos[:, None])
    ang = jnp.asarray(pos).astype(F32) * inv_freq[None, :]
    cos = jnp.cos(ang)
    sin = jnp.sin(ang)
    sin_signed = jnp.where(jnp.asarray((d % (2 * m)) < m)[None, :], -sin, sin)
    cos = jnp.concatenate([cos, jnp.ones((tm, LANES), F32)], axis=0)
    sin_signed = jnp.concatenate([sin_signed, jnp.zeros((tm, LANES), F32)], axis=0)
    return cos, sin_signed


def _scan_kernel(fblk, bblk, first, last, seqb,
                 sb_ref, scg_ref, sx_ref, lxf_ref,
                 scgp_ref, sxp_ref, lxfp_ref, scgn_ref, sxn_ref, lxfn_ref,
                 lxb_ref, lxbp_ref, lxbn_ref,
                 scw_ref, cw_ref, cb_ref, sp_ref, wr_ref, wi_ref, br_ref, bi_ref,
                 zb_ref, hf_ref, hb_ref,
                 a_s, b_s, hc_s, st_s, *, ch, n_ctx_items):
    it = pl.program_id(0)
    is_first = first[it] == 1
    is_last = last[it] == 1
    is_ctx = it < n_ctx_items
    b = seqb[it]
    width = BRANCH_WIDTH
    row = lax.broadcasted_iota(jnp.int32, (ch, width), 0)
    not_first = jnp.where(is_first, 0.0, 1.0).astype(F32)
    not_last = jnp.where(is_last, 0.0, 1.0).astype(F32)

    def back1(u, prev_row):
        return jnp.where(row == 0, prev_row, pltpu.roll(u, 1, 0))

    def back2(u, prev2, prev1):
        return jnp.where(row == 0, prev2, jnp.where(row == 1, prev1, pltpu.roll(u, 2, 0)))

    def fwd1(u, next_row):
        return jnp.where(row == ch - 1, next_row, pltpu.roll(u, ch - 1, 0))

    def lru_input(lx_ref, lxp_ref, lxn_ref, prev_ok, next_ok):
        x = lx_ref[...]
        p = lxp_ref[...] * prev_ok
        n = lxn_ref[...] * next_ok
        return (cw_ref[0:1] * back2(x, p[6:7], p[7:8]) + cw_ref[1:2] * back1(x, p[7:8])
                + cw_ref[2:3] * x + cw_ref[3:4] * fwd1(x, n[0:1]) + cb_ref[...])

    def coeffs(d, xm):
        xb = xm.astype(BF16)
        r = jax.nn.sigmoid(jnp.dot(xb, wr_ref[d], preferred_element_type=F32) + br_ref[d:d + 1])
        g = jax.nn.sigmoid(jnp.dot(xb, wi_ref[d], preferred_element_type=F32) + bi_ref[d:d + 1])
        log_a = (-LRU_C * sp_ref[d:d + 1]) * r
        a = jnp.exp(log_a)
        a_s[d] = a
        b_s[d] = jnp.sqrt(-jnp.tanh(log_a) * (a * a + 1.0)) * (g * xm)

    u = scg_ref[...] * sx_ref[...]
    u_prev = scgp_ref[7:8] * sxp_ref[7:8] * not_first
    u_next = scgn_ref[0:1] * sxn_ref[0:1] * not_last
    conv = scw_ref[0:1] * back1(u, u_prev) + scw_ref[1:2] * u + scw_ref[2:3] * fwd1(u, u_next)
    zb_ref[...] = (sb_ref[...] * conv).astype(BF16)

    coeffs(0, lru_input(lxf_ref, lxfp_ref, lxfn_ref, not_first, not_last))
    coeffs(1, lru_input(lxb_ref, lxbp_ref, lxbn_ref, not_last, not_first))

    @pl.when(jnp.logical_and(is_first, is_ctx))
    def _():
        hc_s[...] = jnp.zeros_like(hc_s)

    @pl.when(jnp.logical_and(is_first, jnp.logical_not(is_ctx)))
    def _():
        hc_s[0:1] = st_s[pl.ds(2 * b, 1), :]
        hc_s[1:2] = st_s[pl.ds(2 * b + 1, 1), :]

    def body(i, carry):
        hf, hb = carry
        base = i * SUBLANES
        for r in range(SUBLANES):
            t = base + r
            hf = a_s[0, pl.ds(t, 1), :] * hf + b_s[0, pl.ds(t, 1), :]
            hf_ref[pl.ds(t, 1), :] = hf
            tb = ch - 1 - t
            hb = a_s[1, pl.ds(tb, 1), :] * hb + b_s[1, pl.ds(tb, 1), :]
            hb_ref[pl.ds(tb, 1), :] = hb
        return hf, hb

    hf, hb = lax.fori_loop(0, ch // SUBLANES, body, (hc_s[0:1], hc_s[1:2]))
    hc_s[0:1] = hf
    hc_s[1:2] = hb

    @pl.when(is_ctx)
    def _():
        st_s[pl.ds(2 * b, 1), :] = hf
        st_s[pl.ds(2 * b + 1, 1), :] = hb


def _scan_tables(n_batch, seq, n_ctx, ch):
    assert n_ctx == ch
    nc = seq // ch
    ctx0 = n_batch * seq // ch
    fblk, bblk, first, last, seqb = [], [], [], [], []
    for b in range(n_batch):
        fblk.append(ctx0 + b); bblk.append(ctx0 + b); first.append(1); last.append(1); seqb.append(b)
    for b in range(n_batch):
        for c in range(nc):
            fblk.append(b * nc + c); bblk.append(b * nc + nc - 1 - c)
            first.append(int(c == 0)); last.append(int(c == nc - 1)); seqb.append(b)
    return [np.asarray(a, np.int32) for a in (fblk, bblk, first, last, seqb)]


def _conv_scan(sb, rest, sc_w, lru_cw, lru_cb, sp, wr_bd, wi_bd, b_r, b_i, n_batch, seq, n_ctx):
    m = sb.shape[0]
    ch = SCAN_CHUNK
    width = BRANCH_WIDTH
    tables = _scan_tables(n_batch, seq, n_ctx, ch)
    n_items = len(tables[0])
    halo_per_chunk = ch // SUBLANES
    last_halo = m // SUBLANES - 1

    def cur(col, which):
        return pl.BlockSpec((ch, width), lambda i, f, bk, *_: ((f, bk)[which][i], col))

    def prev(col, which):
        return pl.BlockSpec((SUBLANES, width),
                            lambda i, f, bk, *_: (jnp.maximum((f, bk)[which][i] * halo_per_chunk - 1, 0), col))

    def nxt(col, which):
        return pl.BlockSpec((SUBLANES, width),
                            lambda i, f, bk, *_: (jnp.minimum(((f, bk)[which][i] + 1) * halo_per_chunk, last_halo), col))

    def full(shape):
        return pl.BlockSpec(shape, lambda i, *_: (0,) * len(shape))

    in_specs = [cur(0, 0), cur(0, 0), cur(1, 0), cur(2, 0),
                prev(0, 0), prev(1, 0), prev(2, 0), nxt(0, 0), nxt(1, 0), nxt(2, 0),
                cur(2, 1), prev(2, 1), nxt(2, 1),
                full(sc_w.shape), full(lru_cw.shape), full((1, width)), full(sp.shape),
                full(wr_bd.shape), full(wi_bd.shape), full(b_r.shape), full(b_i.shape)]
    out_specs = [cur(0, 0), cur(0, 0), cur(0, 1)]
    grid_spec = pltpu.PrefetchScalarGridSpec(
        num_scalar_prefetch=5, grid=(n_items,), in_specs=in_specs, out_specs=out_specs,
        scratch_shapes=[pltpu.VMEM((2, ch, width), F32), pltpu.VMEM((2, ch, width), F32),
                        pltpu.VMEM((SUBLANES, width), F32), pltpu.VMEM((2 * n_batch, width), F32)])
    return pl.pallas_call(
        functools.partial(_scan_kernel, ch=ch, n_ctx_items=n_batch),
        grid_spec=grid_spec,
        out_shape=[jax.ShapeDtypeStruct((m, width), BF16),
                   jax.ShapeDtypeStruct((m, width), F32),
                   jax.ShapeDtypeStruct((m, width), F32)],
        compiler_params=_params(1),
        name="conv_scan",
    )(*[jnp.asarray(t) for t in tables],
      sb, rest, rest, rest, rest, rest, rest, rest, rest, rest, rest, rest, rest,
      sc_w, lru_cw, lru_cb.reshape(1, width), sp, wr_bd, wi_bd, b_r, b_i)


def _attention_kernel(qblk, bat, r0t, cls, q_ref, k_ref, v_ref, kc_ref, vc_ref, *rest, band):
    bias_refs, o_ref = rest[:ATT_ROWS], rest[ATT_ROWS]
    it = pl.program_id(0)
    nq = GRID_W
    lane = lax.broadcasted_iota(jnp.int32, (nq, LANES), 1)
    low = lane < NA_HEAD_DIM
    scale = NA_HEAD_DIM ** -0.5
    nt = (((1,), (1,)), ((), ()))
    for row in range(ATT_ROWS):
        start = pl.multiple_of(r0t[it * ATT_ROWS + row] * GRID_W, GRID_W)
        bias_ref = bias_refs[row]
        qrows = slice(row * nq, (row + 1) * nq)
        for hp in range(NA_HEADS // 2):
            cols = slice(hp * LANES, (hp + 1) * LANES)
            qp = q_ref[qrows, cols].astype(F32) * scale
            qs = jnp.concatenate([jnp.where(low, qp, 0.0), jnp.where(low, 0.0, qp)], axis=0).astype(BF16)
            kb = k_ref[pl.ds(start, band), cols]
            vb = v_ref[pl.ds(start, band), cols]
            s_loc = lax.dot_general(qs, kb, nt, preferred_element_type=F32)
            s_ctx = lax.dot_general(qs, kc_ref[:, cols], nt, preferred_element_type=F32)
            bias = jnp.concatenate([bias_ref[0, 2 * hp], bias_ref[0, 2 * hp + 1]], axis=0)
            s_loc = s_loc + bias
            mx =jnp.maximum(jnp.max(s_loc, axis=-1, keepdims=True), jnp.max(s_ctx, axis=-1, keepdims=True))
            e_loc = jnp.exp(s_loc - mx)
            e_ctx = jnp.exp(s_ctx - mx)
            den = jnp.sum(e_loc, axis=-1, keepdims=True) + jnp.sum(e_ctx, axis=-1, keepdims=True)
            o = (jnp.dot(e_loc.astype(BF16), vb, preferred_element_type=F32)
                 + jnp.dot(e_ctx.astype(BF16), vc_ref[:, cols], preferred_element_type=F32)) / den
            o_ref[qrows, cols] = jnp.where(low, o[:nq], o[nq:]).astype(BF16)


def _attention_tables(n_batch, seq, n_ctx, with_ctx_queries):
    rows = seq // GRID_W
    kr = min(NA_WIN_ROWS, rows)
    assert rows % ATT_ROWS == 0 and n_ctx % ATT_QROWS == 0
    qblk, bat, r0t, cls = [], [], [], []
    ctx_q0 = n_batch * seq // ATT_QROWS
    for b in range(n_batch):
        for rg in range(rows // ATT_ROWS):
            qblk.append(b * (rows // ATT_ROWS) + rg); bat.append(b)
            for r in range(rg * ATT_ROWS, (rg + 1) * ATT_ROWS):
                r0 = min(max(r - kr // 2, 0), rows - kr)
                r0t.append(r0); cls.append(r - r0)
        if with_ctx_queries:
            for c in range(n_ctx // ATT_QROWS):
                qblk.append(ctx_q0 + b * (n_ctx // ATT_QROWS) + c); bat.append(b)
                r0t.extend([0] * ATT_ROWS); cls.extend([kr] * ATT_ROWS)
    return [np.asarray(a, np.int32) for a in (qblk, bat, r0t, cls)]


def _attention_bias(rpb, seq):
    rows = seq // GRID_W
    kr = min(NA_WIN_ROWS, rows)
    kc = NA_WIN_COLS
    cq = np.arange(GRID_W)
    c0 = np.clip(cq - kc // 2, 0, GRID_W - kc)
    ck = np.arange(GRID_W)
    inside = (ck[None, :] >= c0[:, None]) & (ck[None, :] < c0[:, None] + kc)
    dc = np.clip(ck[None, :] - cq[:, None] + (NA_WIN_COLS - 1), 0, 2 * NA_WIN_COLS - 2)
    n_dr = 2 * NA_WIN_ROWS - 1
    n_dc = 2 * NA_WIN_COLS - 1
    pick = jnp.asarray((np.arange(n_dc)[:, None] == dc.reshape(1, -1)).astype(np.float32))
    picked = jnp.dot(rpb.reshape(-1, n_dc), pick, precision=lax.Precision.HIGHEST)
    picked = picked.reshape(NA_HEADS, 2 * NA_WIN_ROWS - 1, GRID_W, GRID_W)
    table = jnp.where(jnp.asarray(inside)[None, None], picked, NEG_BIG)
    table = table.transpose(0, 2, 1, 3).reshape(NA_HEADS, GRID_W, n_dr * GRID_W)
    classes = []
    for cl in range(kr):
        lo = (NA_WIN_ROWS - 1 - cl) * GRID_W
        classes.append(table[:, :, lo:lo + kr * GRID_W])
    classes.append(jnp.full((NA_HEADS, GRID_W, kr * GRID_W), NEG_BIG, F32))
    return jnp.stack(classes, axis=0)


def _attention(q, k, v, bias, n_batch, seq, n_ctx, with_ctx_queries):
    m = q.shape[0] if with_ctx_queries else n_batch * seq
    rows = seq // GRID_W
    kr = min(NA_WIN_ROWS, rows)
    band = kr * GRID_W
    tables = _attention_tables(n_batch, seq, n_ctx, with_ctx_queries)
    n_items = len(tables[0])
    ctx_blk0 = n_batch * seq // n_ctx
    width = NA_WIDTH
    in_specs = [pl.BlockSpec((ATT_QROWS, width), lambda i, qb, bt, r0, cl: (qb[i], 0)),
                pl.BlockSpec((seq, width), lambda i, qb, bt, r0, cl: (bt[i], 0)),
                pl.BlockSpec((seq, width), lambda i, qb, bt, r0, cl: (bt[i], 0)),
                pl.BlockSpec((n_ctx, width), lambda i, qb, bt, r0, cl: (ctx_blk0 + bt[i], 0)),
                pl.BlockSpec((n_ctx, width), lambda i, qb, bt, r0, cl: (ctx_blk0 + bt[i], 0))]
    for row in range(ATT_ROWS):
        in_specs.append(pl.BlockSpec((1, NA_HEADS, GRID_W, band),
                                     lambda i, qb, bt, r0, cl, row=row: (cl[i * ATT_ROWS + row], 0, 0, 0)))
    out_specs = pl.BlockSpec((ATT_QROWS, width), lambda i, qb, bt, r0, cl: (qb[i], 0))
    grid_spec = pltpu.PrefetchScalarGridSpec(num_scalar_prefetch=4, grid=(n_items,),
                                             in_specs=in_specs, out_specs=out_specs)
    return pl.pallas_call(
        functools.partial(_attention_kernel, band=band),
        grid_spec=grid_spec,
        out_shape=jax.ShapeDtypeStruct((m, width), BF16),
        compiler_params=_params(1),
        name="attention",
    )(*[jnp.asarray(t) for t in tables], q, k, v, k, v, *([bias] * ATT_ROWS))


def _post_kernel(ha_ref, hc_ref, att_ref, zb_ref, hf_ref, hb_ref, lg_ref,
                 sh1_ref, sc1_ref, g1_ref, sh2_ref, sc2_ref,
                 wgl0_ref, wgl1_ref, wgl2_ref, bgl0_ref, bgl1_ref, bgl2_ref,
                 wpa_ref, wpc_ref, wpl_ref, wo_ref, bo_ref, l1g_ref, l1b_ref,
                 rw_ref, rb_ref, tri_ref, upper_ref,
                 h1_ref, xs_ref, pos_ref, cnt_ref, *, split):
    h = jnp.where(pl.program_id(0) < split, ha_ref[...], hc_ref[...])
    u1 = (_layer_norm(h) * (1.0 + sc1_ref[0]) + sh1_ref[0]).astype(BF16)
    y_a = jnp.dot(att_ref[...], wpa_ref[...], preferred_element_type=F32)
    y_b = jnp.dot(zb_ref[...], wpc_ref[...], preferred_element_type=F32)
    zc = jax.nn.gelu(lg_ref[...]) * (hf_ref[...] + hb_ref[...])
    y_c = jnp.dot(zc.astype(BF16), wpl_ref[...], preferred_element_type=F32)
    merged = (jax.nn.sigmoid(jnp.dot(u1, wgl0_ref[...], preferred_element_type=F32) + bgl0_ref[...]) * y_a
              + jax.nn.sigmoid(jnp.dot(u1, wgl1_ref[...], preferred_element_type=F32) + bgl1_ref[...]) * y_b
              + jax.nn.sigmoid(jnp.dot(u1, wgl2_ref[...], preferred_element_type=F32) + bgl2_ref[...]) * y_c)
    y = jnp.dot(merged.astype(BF16), wo_ref[...], preferred_element_type=F32) + bo_ref[...]
    h1 = _layer_norm(DEEPNORM_ALPHA * h + g1_ref[0] * y) * l1g_ref[...] + l1b_ref[...]
    h1_ref[...] = h1
    u2 = _layer_norm(h1) * (1.0 + sc2_ref[0]) + sh2_ref[0]

    u_hi = u2.astype(BF16)
    u_lo = (u2 - u_hi.astype(F32)).astype(BF16)
    by_hi = jnp.dot(u_hi, rw_ref[...], preferred_element_type=F32)
    logits = (by_hi[:, :LANES] + by_hi[:, LANES:]
              + jnp.dot(u_lo, rw_ref[:, :LANES], preferred_element_type=F32) + rb_ref[...])
    tm = logits.shape[0]
    lane = lax.broadcasted_iota(jnp.int32, (tm, LANES), 1)
    lane_f = lane.astype(F32)
    work = logits
    tops, idxs, hots = [], [], []
    for _ in range(TOP_K):
        mx = jnp.max(work, axis=-1, keepdims=True)
        idx = jnp.min(jnp.where(work == mx, lane_f, float(LANES)), axis=-1, keepdims=True)
        hot = lane_f == idx
        work = jnp.where(hot, -3e38, work)
        tops.append(mx); idxs.append(idx); hots.append(hot)
    exps = [jnp.exp(t - tops[0]) for t in tops]
    den = exps[0] + exps[1] + exps[2] + exps[3]
    hot_all = jnp.zeros((tm, LANES), F32)
    for hot in hots:
        hot_all = hot_all + hot.astype(F32)
    cnt = jnp.sum(hot_all, axis=0, keepdims=True)
    cnt_pad = jnp.floor((cnt + (SUBLANES - 1.0)) * (1.0 / SUBLANES)) * SUBLANES
    off = jnp.dot(jnp.broadcast_to(cnt_pad, (SUBLANES, LANES)), upper_ref[...],
                  precision=lax.Precision.HIGHEST, preferred_element_type=F32)[0:1]
    slot = off + jnp.dot(tri_ref[...], hot_all.astype(BF16), preferred_element_type=F32)
    pos4 = jnp.zeros((tm, LANES), F32)
    w_tile = jnp.zeros((tm, LANES), F32)
    for kk in range(TOP_K):
        pos_k = jnp.sum(jnp.where(hots[kk], slot, 0.0), axis=-1, keepdims=True)
        pos4 = jnp.where(lane == kk, pos_k, pos4)
        p = exps[kk] / den
        p_hi = p.astype(BF16).astype(F32)
        p_mid = (p - p_hi).astype(BF16).astype(F32)
        p_lo = p - p_hi - p_mid
        w_tile = jnp.where(hots[kk], p_hi, w_tile)
        w_tile = jnp.where(lane_f == idxs[kk] + float(N_EXPERTS), p_mid, w_tile)
        w_tile = jnp.where(lane_f == idxs[kk] + float(2 * N_EXPERTS), p_lo, w_tile)
    pos_ref[...] = pos4
    sub = lax.broadcasted_iota(jnp.int32, (SUBLANES, LANES), 0)
    cnt_ref[...] = jnp.where(sub == 0, cnt, jnp.where(sub == 1, off, 0.0))

    n_sorted = xs_ref.shape[0]
    pos_t = pos4.T
    r_iota = lax.broadcasted_iota(jnp.int32, (n_sorted, tm), 0).astype(F32)
    hit = r_iota == pos_t[0:1, :]
    for kk in range(1, TOP_K):
        hit = jnp.logical_or(hit, r_iota == pos_t[kk:kk + 1, :])
    perm = jnp.where(hit, 1.0, 0.0).astype(BF16)
    feats = jnp.concatenate([u_hi, w_tile.astype(BF16)], axis=1)
    xs_ref[...] = jnp.dot(perm, feats, preferred_element_type=F32)


def _post_mixer(h_parts, att, zb, hf, hb, rest, mod3, mod_base, w_in_bf, b_in, wpa, wpc, wpl, wo, b_o, l1g, l1b,
                rw_pad, rb_pad, n_rows, n_lat, seq):
    tm = TM_POST
    width = BRANCH_WIDTH
    ha, hc = h_parts[0], h_parts[-1]
    split = min(ha.shape[0], n_rows) // tm
    gate_col0 = N_EARLY // D_MODEL
    tiles_per_seq = seq // tm
    n_lat_tiles = n_lat // tm
    n_groups_lat = n_lat // seq
    tri = jnp.asarray(np.tril(np.ones((tm, tm), np.float32), -1), BF16)
    upper = jnp.asarray(np.triu(np.ones((LANES, LANES), np.float32), 1))
    n_tiles = n_rows // tm

    def group(i):
        return jnp.where(i < n_lat_tiles, i // tiles_per_seq, n_groups_lat)

    def rows(wd, col=0):
        return pl.BlockSpec((tm, wd), lambda i: (i, col))

    def full(shape):
        return pl.BlockSpec(shape, lambda i: (0,) * len(shape))

    def mod(which):
        return pl.BlockSpec((1, 1, D_MODEL), lambda i: (mod_base + group(i) * 6 + which, 0, 0))

    in_specs = [pl.BlockSpec((tm, D_MODEL), lambda i: (jnp.minimum(i, split - 1), 0)),
                pl.BlockSpec((tm, D_MODEL), lambda i: (jnp.maximum(i - split, 0), 0)),
                rows(width), rows(width), rows(width), rows(width), rows(width, 3),
                mod(0), mod(1), mod(2), mod(3), mod(4)]
    in_specs += [pl.BlockSpec((D_MODEL, D_MODEL), lambda i, c=c: (0, gate_col0 + c)) for c in range(3)]
    in_specs += [pl.BlockSpec((1, D_MODEL), lambda i, c=c: (0, gate_col0 + c)) for c in range(3)]
    in_specs += [full(wpa.shape), full(wpc.shape), full(wpl.shape), full(wo.shape), full((1, D_MODEL)),
                full((1, D_MODEL)), full((1, D_MODEL)), full(rw_pad.shape), full(rb_pad.shape), full(tri.shape),
                full(upper.shape)]
    out_specs = [rows(D_MODEL), pl.BlockSpec((TOK_BLOCK, XS_WIDTH), lambda i: (i, 0)), rows(LANES),
                 pl.BlockSpec((SUBLANES, LANES), lambda i: (i, 0))]
    return pl.pallas_call(
        functools.partial(_post_kernel, split=split),
        grid=(n_tiles,),
        in_specs=in_specs, out_specs=out_specs,
        out_shape=[jax.ShapeDtypeStruct((n_rows, D_MODEL), F32),
                   jax.ShapeDtypeStruct((n_tiles * TOK_BLOCK, XS_WIDTH), F32),
                   jax.ShapeDtypeStruct((n_rows, LANES), F32),
                   jax.ShapeDtypeStruct((n_tiles * SUBLANES, LANES), F32)],
        compiler_params=_params(1),
        name="post_mixer",
    )(ha, hc, att, zb, hf, hb, rest, mod3, mod3, mod3, mod3, mod3,
      w_in_bf, w_in_bf, w_in_bf, b_in.reshape(1, -1), b_in.reshape(1, -1), b_in.reshape(1, -1),
      wpa, wpc, wpl, wo, b_o.reshape(1, D_MODEL), l1g.reshape(1, D_MODEL), l1b.reshape(1, D_MODEL),
      rw_pad, rb_pad, tri, upper)


def _expert_plan(cnt_out, n_tok_tiles, n_rows):
    tm = TM_EXPERT
    co = cnt_out.reshape(n_tok_tiles, SUBLANES, LANES)
    cnt = (co[:, 0, :N_EXPERTS].astype(jnp.int32) + SUBLANES - 1) // SUBLANES * SUBLANES
    off = co[:, 1, :N_EXPERTS].astype(jnp.int32)
    cum_end = jnp.cumsum(cnt, axis=0)
    cum = cum_end - cnt
    total = cum_end[-1]
    n_et = (total + tm - 1) // tm
    et_end = jnp.cumsum(n_et)
    n_act = et_end[-1:].astype(jnp.int32)
    n_tiles = -(-n_tok_tiles * TOK_BLOCK // tm) + N_EXPERTS
    j = jnp.arange(n_tiles, dtype=jnp.int32)
    tile_e = jnp.minimum(jnp.sum((et_end[None, :] <= j[:, None]).astype(jnp.int32), axis=1), N_EXPERTS - 1)
    pick_e = (tile_e[:, None] == jnp.arange(N_EXPERTS, dtype=jnp.int32)[None, :]).astype(F32)

    def per_tile(table):
        return jnp.dot(pick_e, table.astype(F32), precision=lax.Precision.HIGHEST).astype(jnp.int32)

    row0 = (j - per_tile(et_end - n_et)) * tm
    n_rows_tile = jnp.clip(per_tile(total) - row0, 0, tm)
    cum_e = per_tile(cum.T)
    cum_end_e = per_tile(cum_end.T)
    delta_e = per_tile((jnp.arange(n_tok_tiles, dtype=jnp.int32)[:, None] * TOK_BLOCK + off - cum).T)
    q = row0[:, None] + SUBLANES * jnp.arange(tm // SUBLANES, dtype=jnp.int32)[None, :]
    inside = jnp.logical_and(cum_e.T[:, :, None] <= q[None], q[None] < cum_end_e.T[:, :, None])
    src = q + jnp.sum(jnp.where(inside, delta_e.T[:, :, None], 0), axis=0)
    used = off[:, N_EXPERTS - 1] + cnt[:, N_EXPERTS - 1]
    first = jnp.concatenate([jnp.ones((1,), jnp.int32), (tile_e[1:] != tile_e[:-1]).astype(jnp.int32)])
    group = jnp.cumsum(first) - 1
    after = per_tile(et_end)
    next_e = jnp.where(after < n_act[0],
                       jnp.minimum(jnp.sum((et_end[None, :] <= after[:, None]).astype(jnp.int32), axis=1),
                                   N_EXPERTS - 1), -1)
    return (tile_e.astype(jnp.int32), n_rows_tile.astype(jnp.int32), n_act, src.reshape(-1).astype(jnp.int32),
            used.astype(jnp.int32), first, group.astype(jnp.int32), next_e.astype(jnp.int32), n_tiles)


def _expert_kernel(tile_e, n_rows_t, n_act, src_t, used_t, first_t, group_t, next_t,
                   xs_hbm, wgu_hbm, bgu_ref, wdn_hbm, bdn_ref, ys_hbm,
                   xin, yout, wgu_f, wdn_f, wgu_s, wdn_s, zeros, sem_in, sem_out, sem_zero, sem_wgu, sem_wdn,
                   *, tm, tok_block, n_tok_tiles, layer):
    j = pl.program_id(0)
    na = n_act[0]
    chunks = tm // SUBLANES

    def weight_copies(e, wslot):
        return (pltpu.make_async_copy(wgu_hbm.at[layer, e], wgu_f.at[wslot], sem_wgu.at[wslot]),
                pltpu.make_async_copy(wdn_hbm.at[layer, e], wdn_f.at[wslot], sem_wdn.at[wslot]))

    def gather(src, dst, size, slot):
        return pltpu.make_async_copy(xs_hbm.at[pl.ds(src, size)], xin.at[slot, pl.ds(dst, size)], sem_in.at[slot])

    def scatter(src, dst, size, slot):
        return pltpu.make_async_copy(yout.at[slot, pl.ds(dst, size)], ys_hbm.at[pl.ds(src, size)], sem_out.at[slot])

    def start_chunks(jj, slot, copy):
        def one(c, priority):
            src = pl.multiple_of(src_t[jj * chunks + c], SUBLANES)
            copy(src, pl.multiple_of(c * SUBLANES, SUBLANES), SUBLANES, slot).start(priority=priority)

        def body(c, carry):
            one(c, 0)
            return carry

        def body_unrolled(g, carry):
            for u in range(SUBLANES):
                one(g * SUBLANES + u, u % 2)
            return carry

        n = lax.shift_right_logical(n_rows_t[jj], 3)

        @pl.when(n == chunks)
        def _():
            lax.fori_loop(0, chunks // SUBLANES, body_unrolled, 0)

        @pl.when(n != chunks)
        def _():
            lax.fori_loop(0, n, body, 0)

    def wait_rows(jj, slot, copy):
        n = n_rows_t[jj]
        size = tm
        while size >= SUBLANES:
            @pl.when((n & size) != 0)
            def _(size=size):
                copy(0, 0, size, slot).wait()
            size //= 2

    slot = j % 2

    @pl.when(j == 0)
    def _():
        xin[...] = jnp.zeros_like(xin)
        start_chunks(0, 0, gather)
        zeros[...] = jnp.zeros_like(zeros)

        def clear_tail(i, copy_op):
            used = used_t[i]
            tail = tok_block - used
            for size in (256, 128, 64, 32, 16, 8):
                @pl.when((tail & size) != 0)
                def _():
                    at = pl.multiple_of(i * tok_block + used + (tail & ~(2 * size - 1)), SUBLANES)
                    copy_op(pltpu.make_async_copy(zeros.at[pl.ds(0, size)], ys_hbm.at[pl.ds(at, size)], sem_zero))

        def start_clear(i, c):
            clear_tail(i, lambda cp: cp.start())
            return c

        def wait_clear(i, c):
            clear_tail(i, lambda cp: cp.wait())
            return c

        lax.fori_loop(0, n_tok_tiles, start_clear, 0)
        lax.fori_loop(0, n_tok_tiles, wait_clear, 0)

    @pl.when(j + 1 < na)
    def _():
        start_chunks(j + 1, 1 - slot, gather)

    @pl.when(j < na)
    def _():
        e = tile_e[j]

        @pl.when(first_t[j] == 1)
        def _():
            wslot = group_t[j] % 2

            @pl.when(j == 0)
            def _():
                for cp in weight_copies(e, wslot):
                    cp.start()

            for cp in weight_copies(e, wslot):
                cp.wait()
            wgu_s[...] = wgu_f[wslot].astype(BF16)
            wdn_s[...] = wdn_f[wslot].astype(BF16)

            @pl.when(next_t[j] >= 0)
            def _():
                for cp in weight_copies(next_t[j], 1 - wslot):
                    cp.start()

        wait_rows(j, slot, gather)

        def compute(rows):
            x = xin[slot, pl.ds(0, rows)]
            lane = lax.broadcasted_iota(jnp.int32, (rows, LANES), 1)
            p = jnp.sum(jnp.where(lane % N_EXPERTS == e, x[:, D_MODEL:], 0.0), axis=-1, keepdims=True)
            gu = jnp.dot(x[:, :D_MODEL].astype(BF16), wgu_s[...], preferred_element_type=F32) + bgu_ref[0, 0]
            f = gu.shape[1] // 2
            gate = jnp.minimum(gu[:, :f], SWIGLU_LIMIT)
            up = jnp.clip(gu[:, f:], -SWIGLU_LIMIT, SWIGLU_LIMIT)
            hid = (up + 1.0) * gate * jax.nn.sigmoid(SWIGLU_ALPHA * gate)
            y = jnp.dot(hid.astype(BF16), wdn_s[...], preferred_element_type=F32) + bdn_ref[0, 0]
            yout[slot, pl.ds(0, rows)] = y * p

        @pl.when(n_rows_t[j] > tm // 2)
        def _():
            compute(tm)

        @pl.when(n_rows_t[j] <= tm // 2)
        def _():
            compute(tm // 2)

        start_chunks(j, slot, scatter)

        @pl.when(j >= 1)
        def _():
            wait_rows(j - 1, 1 - slot, scatter)

        @pl.when(j == na - 1)
        def _():
            wait_rows(j, slot, scatter)


def _experts(xs, plan, layer, w_gu, b_gu, w_dn, b_dn):
    tile_e, n_rows_tile, n_act, src, used, first, group, next_e, n_tiles = plan
    tm = TM_EXPERT
    f2 = w_gu.shape[-1]

    def expert(j, te, nr, na, *_):
        return (layer, te[jnp.minimum(j, na[0] - 1)], 0, 0)

    in_specs = [pl.BlockSpec(memory_space=pl.ANY),
                pl.BlockSpec(memory_space=pl.ANY),
                pl.BlockSpec((1, 1, 1, f2), expert),
                pl.BlockSpec(memory_space=pl.ANY),
                pl.BlockSpec((1, 1, 1, D_MODEL), expert)]
    grid_spec = pltpu.PrefetchScalarGridSpec(
        num_scalar_prefetch=8, grid=(n_tiles,), in_specs=in_specs,
        out_specs=pl.BlockSpec(memory_space=pl.ANY),
        scratch_shapes=[pltpu.VMEM((2, tm, XS_WIDTH), F32), pltpu.VMEM((2, tm, D_MODEL), F32),
                        pltpu.VMEM((2, D_MODEL, f2), F32), pltpu.VMEM((2, f2 // 2, D_MODEL), F32),
                        pltpu.VMEM((D_MODEL, f2), BF16), pltpu.VMEM((f2 // 2, D_MODEL), BF16),
                        pltpu.VMEM((ZERO_ROWS, D_MODEL), F32),
                        pltpu.SemaphoreType.DMA((2,)), pltpu.SemaphoreType.DMA((2,)), pltpu.SemaphoreType.DMA(()),
                        pltpu.SemaphoreType.DMA((2,)), pltpu.SemaphoreType.DMA((2,))])
    return pl.pallas_call(
        functools.partial(_expert_kernel, tm=tm, tok_block=TOK_BLOCK, n_tok_tiles=xs.shape[0] // TOK_BLOCK,
                          layer=layer),
        grid_spec=grid_spec,
        out_shape=jax.ShapeDtypeStruct((xs.shape[0], D_MODEL), F32),
        compiler_params=_params(1),
        name="moe_experts",
    )(tile_e, n_rows_tile, n_act, src, used, first, group, next_e,
      xs, w_gu, b_gu.reshape(DEPTH, N_EXPERTS, 1, f2), w_dn, b_dn.reshape(DEPTH, N_EXPERTS, 1, D_MODEL))


def _combine_kernel(ys_ref, pos_ref, h1_ref, g2_ref, l2g_ref, l2b_ref, o_ref):
    tm = pos_ref.shape[0]
    n_sorted = ys_ref.shape[0]
    pos = pos_ref[...]
    col = lax.broadcasted_iota(jnp.int32, (tm, n_sorted), 1).astype(F32)
    sel = jnp.zeros((tm, n_sorted), F32)
    for kk in range(TOP_K):
        sel = sel + (col == pos[:, kk:kk + 1]).astype(F32)
    sel = sel.astype(BF16)
    ys = ys_ref[...]
    hi = ys.astype(BF16)
    rest = ys - hi.astype(F32)
    mid = rest.astype(BF16)
    lo = (rest - mid.astype(F32)).astype(BF16)
    y2 = (jnp.dot(sel, hi, preferred_element_type=F32) + jnp.dot(sel, mid, preferred_element_type=F32)
          + jnp.dot(sel, lo, preferred_element_type=F32))
    o_ref[...] = _layer_norm(DEEPNORM_ALPHA * h1_ref[...] + g2_ref[0] * y2) * l2g_ref[...] + l2b_ref[...]


def _combine(ys, pos4, h1, mod3, mod_base, l2g, l2b, n_lat, seq):
    n_rows = h1.shape[0]
    tm = TM_POST
    tiles_per_seq = seq // tm
    n_lat_tiles = n_lat // tm
    n_groups_lat = n_lat // seq

    def group(i):
        return jnp.where(i < n_lat_tiles, i // tiles_per_seq, n_groups_lat)

    in_specs = [pl.BlockSpec((TOK_BLOCK, D_MODEL), lambda i: (i, 0)),
                pl.BlockSpec((tm, LANES), lambda i: (i, 0)),
                pl.BlockSpec((tm, D_MODEL), lambda i: (i, 0)),
                pl.BlockSpec((1, 1, D_MODEL), lambda i: (mod_base + group(i) * 6 + 5, 0, 0)),
                pl.BlockSpec((1, D_MODEL), lambda i: (0, 0)),
                pl.BlockSpec((1, D_MODEL), lambda i: (0, 0))]
    return pl.pallas_call(
        _combine_kernel,
        grid=(n_rows // tm,),
        in_specs=in_specs,
        out_specs=pl.BlockSpec((tm, D_MODEL), lambda i: (i, 0)),
        out_shape=jax.ShapeDtypeStruct((n_rows, D_MODEL), F32),
        compiler_params=_params(1),
        name="moe_combine",
    )(ys, pos4, h1, mod3, l2g.reshape(1, D_MODEL), l2b.reshape(1, D_MODEL))


def _block_diag(w):
    two, n, d, e = w.shape
    eye = jnp.eye(n, dtype=w.dtype)
    return (w[:, :, :, None, :] * eye[None, :, None, :, None]).reshape(two, n * d, n * e)


def kernel(x, c, ctx, c_ctx, w_mod, b_mod, w_in, b_in, na_rpb, w_proj_attn, w_proj_conv, w_proj_lru, sc_conv_w, lru_conv_w, lru_conv_b, lru_lambda, lru_w_r, lru_b_r, lru_w_i, lru_b_i, w_o, b_o, ln1_g, ln1_b, router_w, router_b, exp_w_gu, exp_b_gu, exp_w_dn, exp_b_dn, ln2_g, ln2_b):
    n_batch, seq, d = x.shape
    n_ctx = ctx.shape[1]
    n_lat = n_batch * seq
    n_all = n_lat + n_batch * n_ctx
    assert d == D_MODEL and n_batch + 1 <= SUBLANES

    cc = jnp.concatenate([c, c_ctx[None], jnp.zeros((SUBLANES - n_batch - 1, d), F32)], axis=0)
    mod = _modulation(cc, w_mod, b_mod)
    groups = n_batch + 1
    mod3 = mod.reshape(DEPTH, SUBLANES, 6, d)[:, :groups].reshape(DEPTH * groups * 6, 1, d)

    cos_t, sin_t = _make_rope(seq, TM_INPROJ)
    h = (x.reshape(n_lat, d), ctx.reshape(n_batch * n_ctx, d))

    for layer in range(DEPTH):
        last = layer == DEPTH - 1
        mod_base = layer * groups * 6
        w_in_bf = w_in[layer].astype(BF16)
        q, k, v, sb, rest = _input_projection(h, mod3, mod_base, w_in_bf, b_in[layer], N_EARLY,
                                              cos_t, sin_t, n_lat, seq)
        sp = jax.nn.softplus(-lru_lambda[layer])
        zb, hf, hb = _conv_scan(sb, rest, sc_conv_w[layer], lru_conv_w[layer], lru_conv_b[layer], sp,
                                _block_diag(lru_w_r[layer]).astype(BF16), _block_diag(lru_w_i[layer]).astype(BF16),
                                lru_b_r[layer], lru_b_i[layer], n_batch, seq, n_ctx)
        att = _attention(q, k, v, _attention_bias(na_rpb[layer], seq), n_batch, seq, n_ctx, not last)
        n_rows = n_lat if last else n_all
        rw_full = jnp.pad(router_w[layer], ((0, 0), (0, LANES - N_EXPERTS)))
        rw_hi = rw_full.astype(BF16)
        rw_pad = jnp.concatenate([rw_hi, (rw_full - rw_hi.astype(F32)).astype(BF16)], axis=1)
        rb_pad = jnp.concatenate([router_b[layer], jnp.full((LANES - N_EXPERTS,), NEG_BIG, F32)]).reshape(1, LANES)
        h1, xs, pos4, cnt_out = _post_mixer(
            h, att, zb, hf, hb, rest, mod3, mod_base, w_in_bf, b_in[layer],
            w_proj_attn[layer].astype(BF16), w_proj_conv[layer].astype(BF16), w_proj_lru[layer].astype(BF16),
            w_o[layer].astype(BF16), b_o[layer], ln1_g[layer], ln1_b[layer], rw_pad, rb_pad, n_rows, n_lat, seq)
        plan = _expert_plan(cnt_out, n_rows // TM_POST, n_rows)
        ys = _experts(xs, plan, layer, exp_w_gu, exp_b_gu, exp_w_dn, exp_b_dn)
        h = (_combine(ys, pos4, h1, mod3, mod_base, ln2_g[layer], ln2_b[layer], n_lat, seq),)
    return h[0].reshape(n_batch, seq, d)
```

```python
import functools

import numpy as np
import jax
import jax.numpy as jnp
from jax import lax
from jax.experimental import pallas as pl
from jax.experimental.pallas import tpu as pltpu

D_MODEL = 1024
DEPTH = 2
GRID_W = 64
NA_HEADS = 8
NA_HEAD_DIM = 64
NA_WIDTH = NA_HEADS * NA_HEAD_DIM
NA_WIN_ROWS = 8
NA_WIN_COLS = 16
ROPE_BASE = 10000.0
BRANCH_WIDTH = 512
LRU_BLOCKS = 8
LRU_C = 8.0
P_TOTAL = 7168
N_EARLY = 8 * BRANCH_WIDTH
N_EXPERTS = 32
TOP_K = 4
SWIGLU_LIMIT = 7.0
SWIGLU_ALPHA = 1.702
LN_EPS = 1e-5
DEEPNORM_ALPHA = (2 * DEPTH) ** 0.25
NEG_BIG = -1e30

LANES = 128
SUBLANES = 8
VMEM_LIMIT_BYTES = 56 * 1024 * 1024

TM_INPROJ = 512
SCAN_CHUNK = 256
ATT_ROWS = 4
ATT_QROWS = ATT_ROWS * GRID_W
TM_POST = 256
TM_EXPERT = 512
XS_WIDTH = D_MODEL + LANES
TOK_BLOCK = TM_POST * TOP_K + N_EXPERTS * SUBLANES
ZERO_ROWS = 256

F32 = jnp.float32
BF16 = jnp.bfloat16


def _params(n_axes):
    return pltpu.CompilerParams(dimension_semantics=("arbitrary",) * n_axes,
                                vmem_limit_bytes=VMEM_LIMIT_BYTES)


def _layer_norm(x):
    mu = jnp.mean(x, axis=-1, keepdims=True)
    xc = x - mu
    var = jnp.mean(xc * xc, axis=-1, keepdims=True)
    return xc * lax.rsqrt(var + LN_EPS)


def _mod_kernel(c_ref, w_ref, b_ref, o_ref):
    c = c_ref[...]
    s = (c * jax.nn.sigmoid(c)).astype(BF16)
    o_ref[0] = jnp.dot(s, w_ref[0].astype(BF16), preferred_element_type=F32) + b_ref[0]


def _modulation(cc, w_mod, b_mod):
    n_out = w_mod.shape[-1]
    return pl.pallas_call(
        _mod_kernel,
        grid=(DEPTH, n_out // D_MODEL),
        in_specs=[pl.BlockSpec((SUBLANES, D_MODEL), lambda l, j: (0, 0)),
                  pl.BlockSpec((1, D_MODEL, D_MODEL), lambda l, j: (l, 0, j)),
                  pl.BlockSpec((1, 1, D_MODEL), lambda l, j: (l, 0, j))],
        out_specs=pl.BlockSpec((1, SUBLANES, D_MODEL), lambda l, j: (l, 0, j)),
        out_shape=jax.ShapeDtypeStruct((DEPTH, SUBLANES, n_out), F32),
        compiler_params=_params(2),
        name="modulation",
    )(cc, w_mod, b_mod.reshape(DEPTH, 1, n_out))


def _rope_half(x, cos, sin_signed):
    lane = lax.broadcasted_iota(jnp.int32, (x.shape[0], LANES), 1)
    first = (lane % 32) < 16
    outs = []
    for cidx in range(x.shape[1] // LANES):
        xc = x[:, cidx * LANES:(cidx + 1) * LANES]
        partner = jnp.where(first, pltpu.roll(xc, LANES - 16, 1), pltpu.roll(xc, 16, 1))
        outs.append(xc * cos + partner * sin_signed)
    return jnp.concatenate(outs, axis=1)


def _inproj_kernel(ha_ref, hb_ref, sh_ref, sc_ref, w_ref, b_ref, cos_ref, sin_ref,
                   q_ref, k_ref, v_ref, sb_ref, rest_ref, *, split):
    h = jnp.where(pl.program_id(0) < split, ha_ref[...], hb_ref[...])
    xn = (_layer_norm(h) * (1.0 + sc_ref[0]) + sh_ref[0]).astype(BF16)
    half = BRANCH_WIDTH

    def columns(lo, width):
        return jnp.dot(xn, w_ref[:, lo:lo + width], preferred_element_type=F32) + b_ref[:, lo:lo + width]

    cos = cos_ref[...]
    sin = sin_ref[...]
    q_ref[...] = _rope_half(columns(0, half), cos, sin).astype(BF16)
    k_ref[...] = _rope_half(columns(half, half), cos, sin).astype(BF16)
    v_ref[...] = columns(2 * half, half).astype(BF16)
    sb_ref[...] = columns(3 * half, half)
    n_rest = rest_ref.shape[1]
    for lo in range(0, n_rest, 2 * half):
        rest_ref[:, lo:lo + 2 * half] = columns(4 * half + lo, 2 * half)


def _input_projection(h_parts, mod3, mod_base, w_in_bf, b_in, n_cols, cos_t, sin_t, n_lat, seq):
    tm = TM_INPROJ
    ha, hb = h_parts[0], h_parts[-1]
    split = ha.shape[0] // tm
    m = ha.shape[0] + (hb.shape[0] if len(h_parts) == 2 else 0)
    n_lat_tiles = n_lat // tm
    tiles_per_seq = seq // tm
    n_groups_lat = n_lat // seq

    def group(i):
        return jnp.where(i < n_lat_tiles, i // tiles_per_seq, n_groups_lat)

    def rope_blk(i):
        return jnp.where(i < n_lat_tiles, i % tiles_per_seq, tiles_per_seq)

    half = BRANCH_WIDTH
    return pl.pallas_call(
        functools.partial(_inproj_kernel, split=split),
        grid=(m // tm,),
        in_specs=[pl.BlockSpec((tm, D_MODEL), lambda i: (jnp.minimum(i, split - 1), 0)),
                  pl.BlockSpec((tm, D_MODEL), lambda i: (jnp.maximum(i - split, 0), 0)),
                  pl.BlockSpec((1, 1, D_MODEL), lambda i: (mod_base + group(i) * 6 + 0, 0, 0)),
                  pl.BlockSpec((1, 1, D_MODEL), lambda i: (mod_base + group(i) * 6 + 1, 0, 0)),
                  pl.BlockSpec((D_MODEL, n_cols), lambda i: (0, 0)),
                  pl.BlockSpec((1, n_cols), lambda i: (0, 0)),
                  pl.BlockSpec((tm, LANES), lambda i: (rope_blk(i), 0)),
                  pl.BlockSpec((tm, LANES), lambda i: (rope_blk(i), 0))],
        out_specs=[pl.BlockSpec((tm, half), lambda i: (i, 0)),
                   pl.BlockSpec((tm, half), lambda i: (i, 0)),
                   pl.BlockSpec((tm, half), lambda i: (i, 0)),
                   pl.BlockSpec((tm, half), lambda i: (i, 0)),
                   pl.BlockSpec((tm, n_cols - 4 * half), lambda i: (i, 0))],
        out_shape=[jax.ShapeDtypeStruct((m, half), BF16),
                   jax.ShapeDtypeStruct((m, half), BF16),
                   jax.ShapeDtypeStruct((m, half), BF16),
                   jax.ShapeDtypeStruct((m, half), F32),
                   jax.ShapeDtypeStruct((m, n_cols - 4 * half), F32)],
        compiler_params=_params(1),
        name="input_projection",
    )(ha, hb, mod3, mod3, w_in_bf, b_in.reshape(1, -1), cos_t, sin_t)


def _make_rope(seq, tm):
    t = np.arange(seq)
    row_pos, col_pos = t // GRID_W, t % GRID_W
    d = np.arange(LANES) % NA_HEAD_DIM
    m = NA_HEAD_DIM // 4
    inv_freq = (ROPE_BASE ** (-jnp.arange(m, dtype=F32) / m))[d % m]
    pos = np.where((d < 2 * m)[None, :], row_pos[:, None], col_pos[:, None])
    ang = jnp.asarray(pos).astype(F32) * inv_freq[None, :]
    cos = jnp.cos(ang)
    sin = jnp.sin(ang)
    sin_signed = jnp.where(jnp.asarray((d % (2 * m)) < m)[None, :], -sin, sin)
    cos = jnp.concatenate([cos, jnp.ones((tm, LANES), F32)], axis=0)
    sin_signed = jnp.concatenate([sin_signed, jnp.zeros((tm, LANES), F32)], axis=0)
    return cos, sin_signed


def _scan_kernel(fblk, bblk, first, last, seqb,
                 sb_ref, scg_ref, sx_ref, lxf_ref,
                 scgp_ref, sxp_ref, lxfp_ref, scgn_ref, sxn_ref, lxfn_ref,
                 lxb_ref, lxbp_ref, lxbn_ref,
                 scw_ref, cw_ref, cb_ref, sp_ref, wr_ref, wi_ref, br_ref, bi_ref,
                 zb_ref, hf_ref, hb_ref,
                 a_s, b_s, hc_s, st_s, *, ch, n_ctx_items):
    it = pl.program_id(0)
    is_first = first[it] == 1
    is_last = last[it] == 1
    is_ctx = it < n_ctx_items
    b = seqb[it]
    width = BRANCH_WIDTH
    row = lax.broadcasted_iota(jnp.int32, (ch, width), 0)
    not_first = jnp.where(is_first, 0.0, 1.0).astype(F32)
    not_last = jnp.where(is_last, 0.0, 1.0).astype(F32)

    def back1(u, prev_row):
        return jnp.where(row == 0, prev_row, pltpu.roll(u, 1, 0))

    def back2(u, prev2, prev1):
        return jnp.where(row == 0, prev2, jnp.where(row == 1, prev1, pltpu.roll(u, 2, 0)))

    def fwd1(u, next_row):
        return jnp.where(row == ch - 1, next_row, pltpu.roll(u, ch - 1, 0))

    def lru_input(lx_ref, lxp_ref, lxn_ref, prev_ok, next_ok):
        x = lx_ref[...]
        p = lxp_ref[...] * prev_ok
        n = lxn_ref[...] * next_ok
        return (cw_ref[0:1] * back2(x, p[6:7], p[7:8]) + cw_ref[1:2] * back1(x, p[7:8])
                + cw_ref[2:3] * x + cw_ref[3:4] * fwd1(x, n[0:1]) + cb_ref[...])

    def coeffs(d, xm):
        xb = xm.astype(BF16)
        r = jax.nn.sigmoid(jnp.dot(xb, wr_ref[d], preferred_element_type=F32) + br_ref[d:d + 1])
        g = jax.nn.sigmoid(jnp.dot(xb, wi_ref[d], preferred_element_type=F32) + bi_ref[d:d + 1])
        log_a = (-LRU_C * sp_ref[d:d + 1]) * r
        a = jnp.exp(log_a)
        a_s[d] = a
        b_s[d] = jnp.sqrt(-jnp.tanh(log_a) * (a * a + 1.0)) * (g * xm)

    @pl.when(jnp.logical_and(is_first, is_ctx))
    def _():
        hc_s[...] = jnp.zeros_like(hc_s)

    @pl.when(jnp.logical_and(is_first, jnp.logical_not(is_ctx)))
    def _():
        hc_s[0:1] = st_s[pl.ds(2 * b, 1), :]
        hc_s[1:2] = st_s[pl.ds(2 * b + 1, 1), :]

    coeffs(0, lru_input(lxf_ref, lxfp_ref, lxfn_ref, not_first, not_last))
    hf = hc_s[0:1]
    for t in range(ch):
        hf = a_s[0, t:t + 1, :] * hf + b_s[0, t:t + 1, :]
        hf_ref[t:t + 1, :] = hf
    hc_s[0:1] = hf

    coeffs(1, lru_input(lxb_ref, lxbp_ref, lxbn_ref, not_last, not_first))

    u = scg_ref[...] * sx_ref[...]
    u_prev = scgp_ref[7:8] * sxp_ref[7:8] * not_first
    u_next = scgn_ref[0:1] * sxn_ref[0:1] * not_last
    conv = scw_ref[0:1] * back1(u, u_prev) + scw_ref[1:2] * u + scw_ref[2:3] * fwd1(u, u_next)
    zb_ref[...] = (sb_ref[...] * conv).astype(BF16)

    hb = hc_s[1:2]
    for t in range(ch - 1, -1, -1):
        hb = a_s[1, t:t + 1, :] * hb + b_s[1, t:t + 1, :]
        hb_ref[t:t + 1, :] = hb
    hc_s[1:2] = hb

    @pl.when(is_ctx)
    def _():
        st_s[pl.ds(2 * b, 1), :] = hf
        st_s[pl.ds(2 * b + 1, 1), :] = hb


def _scan_tables(n_batch, seq, n_ctx, ch):
    assert n_ctx == ch
    nc = seq // ch
    ctx0 = n_batch * seq // ch
    fblk, bblk, first, last, seqb = [], [], [], [], []
    for b in range(n_batch):
        fblk.append(ctx0 + b); bblk.append(ctx0 + b); first.append(1); last.append(1); seqb.append(b)
    for b in range(n_batch):
        for c in range(nc):
            fblk.append(b * nc + c); bblk.append(b * nc + nc - 1 - c)
            first.append(int(c == 0)); last.append(int(c == nc - 1)); seqb.append(b)
    return [np.asarray(a, np.int32) for a in (fblk, bblk, first, last, seqb)]


def _conv_scan(sb, rest, sc_w, lru_cw, lru_cb, sp, wr_bd, wi_bd, b_r, b_i, n_batch, seq, n_ctx):
    m = sb.shape[0]
    ch = SCAN_CHUNK
    width = BRANCH_WIDTH
    tables = _scan_tables(n_batch, seq, n_ctx, ch)
    n_items = len(tables[0])
    halo_per_chunk = ch // SUBLANES
    last_halo = m // SUBLANES - 1

    def cur(col, which):
        return pl.BlockSpec((ch, width), lambda i, f, bk, *_: ((f, bk)[which][i], col))

    def prev(col, which):
        return pl.BlockSpec((SUBLANES, width),
                            lambda i, f, bk, *_: (jnp.maximum((f, bk)[which][i] * halo_per_chunk - 1, 0), col))

    def nxt(col, which):
        return pl.BlockSpec((SUBLANES, width),
                            lambda i, f, bk, *_: (jnp.minimum(((f, bk)[which][i] + 1) * halo_per_chunk, last_halo), col))

    def full(shape):
        return pl.BlockSpec(shape, lambda i, *_: (0,) * len(shape))

    in_specs = [cur(0, 0), cur(0, 0), cur(1, 0), cur(2, 0),
                prev(0, 0), prev(1, 0), prev(2, 0), nxt(0, 0), nxt(1, 0), nxt(2, 0),
                cur(2, 1), prev(2, 1), nxt(2, 1),
                full(sc_w.shape), full(lru_cw.shape), full((1, width)), full(sp.shape),
                full(wr_bd.shape), full(wi_bd.shape), full(b_r.shape), full(b_i.shape)]
    out_specs = [cur(0, 0), cur(0, 0), cur(0, 1)]
    grid_spec = pltpu.PrefetchScalarGridSpec(
        num_scalar_prefetch=5, grid=(n_items,), in_specs=in_specs, out_specs=out_specs,
        scratch_shapes=[pltpu.VMEM((2, ch, width), F32), pltpu.VMEM((2, ch, width), F32),
                        pltpu.VMEM((SUBLANES, width), F32), pltpu.VMEM((2 * n_batch, width), F32)])
    return pl.pallas_call(
        functools.partial(_scan_kernel, ch=ch, n_ctx_items=n_batch),
        grid_spec=grid_spec,
        out_shape=[jax.ShapeDtypeStruct((m, width), BF16),
                   jax.ShapeDtypeStruct((m, width), F32),
                   jax.ShapeDtypeStruct((m, width), F32)],
        compiler_params=_params(1),
        name="conv_scan",
    )(*[jnp.asarray(t) for t in tables],
      sb, rest, rest, rest, rest, rest, rest, rest, rest, rest, rest, rest, rest,
      sc_w, lru_cw, lru_cb.reshape(1, width), sp, wr_bd, wi_bd, b_r, b_i)


def _attention_kernel(qblk, bat, r0t, cls, q_ref, k_ref, v_ref, kc_ref, vc_ref, *rest, band):
    bias_refs, o_ref = rest[:ATT_ROWS], rest[ATT_ROWS]
    it = pl.program_id(0)
    nq = GRID_W
    lane = lax.broadcasted_iota(jnp.int32, (nq, LANES), 1)
    low = lane < NA_HEAD_DIM
    scale = NA_HEAD_DIM ** -0.5
    nt = (((1,), (1,)), ((), ()))
    for row in range(ATT_ROWS):
        start = pl.multiple_of(r0t[it * ATT_ROWS + row] * GRID_W, GRID_W)
        bias_ref = bias_refs[row]
        qrows = slice(row * nq, (row + 1) * nq)
        for hp in range(NA_HEADS // 2):
            cols = slice(hp * LANES, (hp + 1) * LANES)
            qp = q_ref[qrows, cols].astype(F32) * scale
            qs = jnp.concatenate([jnp.where(low, qp, 0.0), jnp.where(low, 0.0, qp)], axis=0).astype(BF16)
            kb = k_ref[pl.ds(start, band), cols]
            vb = v_ref[pl.ds(start, band), cols]
            s_loc = lax.dot_general(qs, kb, nt, preferred_element_type=F32)
            s_ctx = lax.dot_general(qs, kc_ref[:, cols], nt, preferred_element_type=F32)
            bias = jnp.concatenate([bias_ref[0, 2 * hp], bias_ref[0, 2 * hp + 1]], axis=0)
            s_loc = s_loc + bias
            mx =jnp.maximum(jnp.max(s_loc, axis=-1, keepdims=True), jnp.max(s_ctx, axis=-1, keepdims=True))
            e_loc = jnp.exp(s_loc - mx)
            e_ctx = jnp.exp(s_ctx - mx)
            den = jnp.sum(e_loc, axis=-1, keepdims=True) + jnp.sum(e_ctx, axis=-1, keepdims=True)
            o = (jnp.dot(e_loc.astype(BF16), vb, preferred_element_type=F32)
                 + jnp.dot(e_ctx.astype(BF16), vc_ref[:, cols], preferred_element_type=F32)) / den
            o_ref[qrows, cols] = jnp.where(low, o[:nq], o[nq:]).astype(BF16)


def _attention_tables(n_batch, seq, n_ctx, with_ctx_queries):
    rows = seq // GRID_W
    kr = min(NA_WIN_ROWS, rows)
    assert rows % ATT_ROWS == 0 and n_ctx % ATT_QROWS == 0
    qblk, bat, r0t, cls = [], [], [], []
    ctx_q0 = n_batch * seq // ATT_QROWS
    for b in range(n_batch):
        for rg in range(rows // ATT_ROWS):
            qblk.append(b * (rows // ATT_ROWS) + rg); bat.append(b)
            for r in range(rg * ATT_ROWS, (rg + 1) * ATT_ROWS):
                r0 = min(max(r - kr // 2, 0), rows - kr)
                r0t.append(r0); cls.append(r - r0)
        if with_ctx_queries:
            for c in range(n_ctx // ATT_QROWS):
                qblk.append(ctx_q0 + b * (n_ctx // ATT_QROWS) + c); bat.append(b)
                r0t.extend([0] * ATT_ROWS); cls.extend([kr] * ATT_ROWS)
    return [np.asarray(a, np.int32) for a in (qblk, bat, r0t, cls)]


def _attention_bias(rpb, seq):
    rows = seq // GRID_W
    kr = min(NA_WIN_ROWS, rows)
    kc = NA_WIN_COLS
    cq = np.arange(GRID_W)
    c0 = np.clip(cq - kc // 2, 0, GRID_W - kc)
    ck = np.arange(GRID_W)
    inside = (ck[None, :] >= c0[:, None]) & (ck[None, :] < c0[:, None] + kc)
    dc = np.clip(ck[None, :] - cq[:, None] + (NA_WIN_COLS - 1), 0, 2 * NA_WIN_COLS - 2)
    n_dr = 2 * NA_WIN_ROWS - 1
    n_dc = 2 * NA_WIN_COLS - 1
    pick = jnp.asarray((np.arange(n_dc)[:, None] == dc.reshape(1, -1)).astype(np.float32))
    picked = jnp.dot(rpb.reshape(-1, n_dc), pick, precision=lax.Precision.HIGHEST)
    picked = picked.reshape(NA_HEADS, 2 * NA_WIN_ROWS - 1, GRID_W, GRID_W)
    table = jnp.where(jnp.asarray(inside)[None, None], picked, NEG_BIG)
    table = table.transpose(0, 2, 1, 3).reshape(NA_HEADS, GRID_W, n_dr * GRID_W)
    classes = []
    for cl in range(kr):
        lo = (NA_WIN_ROWS - 1 - cl) * GRID_W
        classes.append(table[:, :, lo:lo + kr * GRID_W])
    classes.append(jnp.full((NA_HEADS, GRID_W, kr * GRID_W), NEG_BIG, F32))
    return jnp.stack(classes, axis=0)


def _attention(q, k, v, bias, n_batch, seq, n_ctx, with_ctx_queries):
    m = q.shape[0] if with_ctx_queries else n_batch * seq
    rows = seq // GRID_W
    kr = min(NA_WIN_ROWS, rows)
    band = kr * GRID_W
    tables = _attention_tables(n_batch, seq, n_ctx, with_ctx_queries)
    n_items = len(tables[0])
    ctx_blk0 = n_batch * seq // n_ctx
    width = NA_WIDTH
    in_specs = [pl.BlockSpec((ATT_QROWS, width), lambda i, qb, bt, r0, cl: (qb[i], 0)),
                pl.BlockSpec((seq, width), lambda i, qb, bt, r0, cl: (bt[i], 0)),
                pl.BlockSpec((seq, width), lambda i, qb, bt, r0, cl: (bt[i], 0)),
                pl.BlockSpec((n_ctx, width), lambda i, qb, bt, r0, cl: (ctx_blk0 + bt[i], 0)),
                pl.BlockSpec((n_ctx, width), lambda i, qb, bt, r0, cl: (ctx_blk0 + bt[i], 0))]
    for row in range(ATT_ROWS):
        in_specs.append(pl.BlockSpec((1, NA_HEADS, GRID_W, band),
                                     lambda i, qb, bt, r0, cl, row=row: (cl[i * ATT_ROWS + row], 0, 0, 0)))
    out_specs = pl.BlockSpec((ATT_QROWS, width), lambda i, qb, bt, r0, cl: (qb[i], 0))
    grid_spec = pltpu.PrefetchScalarGridSpec(num_scalar_prefetch=4, grid=(n_items,),
                                             in_specs=in_specs, out_specs=out_specs)
    return pl.pallas_call(
        functools.partial(_attention_kernel, band=band),
        grid_spec=grid_spec,
        out_shape=jax.ShapeDtypeStruct((m, width), BF16),
        compiler_params=_params(1),
        name="attention",
    )(*[jnp.asarray(t) for t in tables], q, k, v, k, v, *([bias] * ATT_ROWS))


def _post_kernel(ha_ref, hc_ref, att_ref, zb_ref, hf_ref, hb_ref, lg_ref,
                 sh1_ref, sc1_ref, g1_ref, sh2_ref, sc2_ref,
                 wgl0_ref, wgl1_ref, wgl2_ref, bgl0_ref, bgl1_ref, bgl2_ref,
                 wpa_ref, wpc_ref, wpl_ref, wo_ref, bo_ref, l1g_ref, l1b_ref,
                 rw_ref, rb_ref, tri_ref, upper_ref,
                 h1_ref, xs_ref, pos_ref, cnt_ref, *, split):
    h = jnp.where(pl.program_id(0) < split, ha_ref[...], hc_ref[...])
    u1 = (_layer_norm(h) * (1.0 + sc1_ref[0]) + sh1_ref[0]).astype(BF16)
    y_a = jnp.dot(att_ref[...], wpa_ref[...], preferred_element_type=F32)
    y_b = jnp.dot(zb_ref[...], wpc_ref[...], preferred_element_type=F32)
    zc = jax.nn.gelu(lg_ref[...]) * (hf_ref[...] + hb_ref[...])
    y_c = jnp.dot(zc.astype(BF16), wpl_ref[...], preferred_element_type=F32)
    merged = (jax.nn.sigmoid(jnp.dot(u1, wgl0_ref[...], preferred_element_type=F32) + bgl0_ref[...]) * y_a
              + jax.nn.sigmoid(jnp.dot(u1, wgl1_ref[...], preferred_element_type=F32) + bgl1_ref[...]) * y_b
              + jax.nn.sigmoid(jnp.dot(u1, wgl2_ref[...], preferred_element_type=F32) + bgl2_ref[...]) * y_c)
    y = jnp.dot(merged.astype(BF16), wo_ref[...], preferred_element_type=F32) + bo_ref[...]
    h1 = _layer_norm(DEEPNORM_ALPHA * h + g1_ref[0] * y) * l1g_ref[...] + l1b_ref[...]
    h1_ref[...] = h1
    u2 = _layer_norm(h1) * (1.0 + sc2_ref[0]) + sh2_ref[0]

    u_hi = u2.astype(BF16)
    u_lo = (u2 - u_hi.astype(F32)).astype(BF16)
    by_hi = jnp.dot(u_hi, rw_ref[...], preferred_element_type=F32)
    logits = (by_hi[:, :LANES] + by_hi[:, LANES:]
              + jnp.dot(u_lo, rw_ref[:, :LANES], preferred_element_type=F32) + rb_ref[...])
    tm = logits.shape[0]
    lane = lax.broadcasted_iota(jnp.int32, (tm, LANES), 1)
    lane_f = lane.astype(F32)
    work = logits
    tops, idxs, hots = [], [], []
    for _ in range(TOP_K):
        mx = jnp.max(work, axis=-1, keepdims=True)
        idx = jnp.min(jnp.where(work == mx, lane_f, float(LANES)), axis=-1, keepdims=True)
        hot = lane_f == idx
        work = jnp.where(hot, -3e38, work)
        tops.append(mx); idxs.append(idx); hots.append(hot)
    exps = [jnp.exp(t - tops[0]) for t in tops]
    den = exps[0] + exps[1] + exps[2] + exps[3]
    hot_all = jnp.zeros((tm, LANES), F32)
    for hot in hots:
        hot_all = hot_all + hot.astype(F32)
    cnt = jnp.sum(hot_all, axis=0, keepdims=True)
    cnt_pad = jnp.floor((cnt + (SUBLANES - 1.0)) * (1.0 / SUBLANES)) * SUBLANES
    off = jnp.dot(jnp.broadcast_to(cnt_pad, (SUBLANES, LANES)), upper_ref[...],
                  precision=lax.Precision.HIGHEST, preferred_element_type=F32)[0:1]
    slot = off + jnp.dot(tri_ref[...], hot_all.astype(BF16), preferred_element_type=F32)
    pos4 = jnp.zeros((tm, LANES), F32)
    w_tile = jnp.zeros((tm, LANES), F32)
    for kk in range(TOP_K):
        pos_k = jnp.sum(jnp.where(hots[kk], slot, 0.0), axis=-1, keepdims=True)
        pos4 = jnp.where(lane == kk, pos_k, pos4)
        p = exps[kk] / den
        p_hi = p.astype(BF16).astype(F32)
        p_mid = (p - p_hi).astype(BF16).astype(F32)
        p_lo = p - p_hi - p_mid
        w_tile = jnp.where(hots[kk], p_hi, w_tile)
        w_tile = jnp.where(lane_f == idxs[kk] + float(N_EXPERTS), p_mid, w_tile)
        w_tile = jnp.where(lane_f == idxs[kk] + float(2 * N_EXPERTS), p_lo, w_tile)
    pos_ref[...] = pos4
    sub = lax.broadcasted_iota(jnp.int32, (SUBLANES, LANES), 0)
    cnt_ref[...] = jnp.where(sub == 0, cnt, jnp.where(sub == 1, off, 0.0))

    n_sorted = xs_ref.shape[0]
    pos_t = pos4.T
    r_iota = lax.broadcasted_iota(jnp.int32, (n_sorted, tm), 0).astype(F32)
    hit = r_iota == pos_t[0:1, :]
    for kk in range(1, TOP_K):
        hit = jnp.logical_or(hit, r_iota == pos_t[kk:kk + 1, :])
    perm = jnp.where(hit, 1.0, 0.0).astype(BF16)
    feats = jnp.concatenate([u_hi, w_tile.astype(BF16)], axis=1)
    xs_ref[...] = jnp.dot(perm, feats, preferred_element_type=F32)


def _post_mixer(h_parts, att, zb, hf, hb, rest, mod3, mod_base, w_in_bf, b_in, wpa, wpc, wpl, wo, b_o, l1g, l1b,
                rw_pad, rb_pad, n_rows, n_lat, seq):
    tm = TM_POST
    width = BRANCH_WIDTH
    ha, hc = h_parts[0], h_parts[-1]
    split = min(ha.shape[0], n_rows) // tm
    gate_col0 = N_EARLY // D_MODEL
    tiles_per_seq = seq // tm
    n_lat_tiles = n_lat // tm
    n_groups_lat = n_lat // seq
    tri = jnp.asarray(np.tril(np.ones((tm, tm), np.float32), -1), BF16)
    upper = jnp.asarray(np.triu(np.ones((LANES, LANES), np.float32), 1))
    n_tiles = n_rows // tm

    def group(i):
        return jnp.where(i < n_lat_tiles, i // tiles_per_seq, n_groups_lat)

    def rows(wd, col=0):
        return pl.BlockSpec((tm, wd), lambda i: (i, col))

    def full(shape):
        return pl.BlockSpec(shape, lambda i: (0,) * len(shape))

    def mod(which):
        return pl.BlockSpec((1, 1, D_MODEL), lambda i: (mod_base + group(i) * 6 + which, 0, 0))

    in_specs = [pl.BlockSpec((tm, D_MODEL), lambda i: (jnp.minimum(i, split - 1), 0)),
                pl.BlockSpec((tm, D_MODEL), lambda i: (jnp.maximum(i - split, 0), 0)),
                rows(width), rows(width), rows(width), rows(width), rows(width, 3),
                mod(0), mod(1), mod(2), mod(3), mod(4)]
    in_specs += [pl.BlockSpec((D_MODEL, D_MODEL), lambda i, c=c: (0, gate_col0 + c)) for c in range(3)]
    in_specs += [pl.BlockSpec((1, D_MODEL), lambda i, c=c: (0, gate_col0 + c)) for c in range(3)]
    in_specs += [full(wpa.shape), full(wpc.shape), full(wpl.shape), full(wo.shape), full((1, D_MODEL)),
                full((1, D_MODEL)), full((1, D_MODEL)), full(rw_pad.shape), full(rb_pad.shape), full(tri.shape),
                full(upper.shape)]
    out_specs = [rows(D_MODEL), pl.BlockSpec((TOK_BLOCK, XS_WIDTH), lambda i: (i, 0)), rows(LANES),
                 pl.BlockSpec((SUBLANES, LANES), lambda i: (i, 0))]
    return pl.pallas_call(
        functools.partial(_post_kernel, split=split),
        grid=(n_tiles,),
        in_specs=in_specs, out_specs=out_specs,
        out_shape=[jax.ShapeDtypeStruct((n_rows, D_MODEL), F32),
                   jax.ShapeDtypeStruct((n_tiles * TOK_BLOCK, XS_WIDTH), F32),
                   jax.ShapeDtypeStruct((n_rows, LANES), F32),
                   jax.ShapeDtypeStruct((n_tiles * SUBLANES, LANES), F32)],
        compiler_params=_params(1),
        name="post_mixer",
    )(ha, hc, att, zb, hf, hb, rest, mod3, mod3, mod3, mod3, mod3,
      w_in_bf, w_in_bf, w_in_bf, b_in.reshape(1, -1), b_in.reshape(1, -1), b_in.reshape(1, -1),
      wpa, wpc, wpl, wo, b_o.reshape(1, D_MODEL), l1g.reshape(1, D_MODEL), l1b.reshape(1, D_MODEL),
      rw_pad, rb_pad, tri, upper)


def _expert_plan(cnt_out, n_tok_tiles, n_rows):
    tm = TM_EXPERT
    co = cnt_out.reshape(n_tok_tiles, SUBLANES, LANES)
    cnt = (co[:, 0, :N_EXPERTS].astype(jnp.int32) + SUBLANES - 1) // SUBLANES * SUBLANES
    off = co[:, 1, :N_EXPERTS].astype(jnp.int32)
    cum_end = jnp.cumsum(cnt, axis=0)
    cum = cum_end - cnt
    total = cum_end[-1]
    n_et = (total + tm - 1) // tm
    et_end = jnp.cumsum(n_et)
    n_act = et_end[-1:].astype(jnp.int32)
    n_tiles = -(-n_tok_tiles * TOK_BLOCK // tm) + N_EXPERTS
    j = jnp.arange(n_tiles, dtype=jnp.int32)
    tile_e = jnp.minimum(jnp.sum((et_end[None, :] <= j[:, None]).astype(jnp.int32), axis=1), N_EXPERTS - 1)
    pick_e = (tile_e[:, None] == jnp.arange(N_EXPERTS, dtype=jnp.int32)[None, :]).astype(F32)

    def per_tile(table):
        return jnp.dot(pick_e, table.astype(F32), precision=lax.Precision.HIGHEST).astype(jnp.int32)

    row0 = (j - per_tile(et_end - n_et)) * tm
    n_rows_tile = jnp.clip(per_tile(total) - row0, 0, tm)
    cum_e = per_tile(cum.T)
    cum_end_e = per_tile(cum_end.T)
    delta_e = per_tile((jnp.arange(n_tok_tiles, dtype=jnp.int32)[:, None] * TOK_BLOCK + off - cum).T)
    q = row0[:, None] + SUBLANES * jnp.arange(tm // SUBLANES, dtype=jnp.int32)[None, :]
    inside = jnp.logical_and(cum_e.T[:, :, None] <= q[None], q[None] < cum_end_e.T[:, :, None])
    src = q + jnp.sum(jnp.where(inside, delta_e.T[:, :, None], 0), axis=0)
    used = off[:, N_EXPERTS - 1] + cnt[:, N_EXPERTS - 1]
    first = jnp.concatenate([jnp.ones((1,), jnp.int32), (tile_e[1:] != tile_e[:-1]).astype(jnp.int32)])
    group = jnp.cumsum(first) - 1
    after = per_tile(et_end)
    next_e = jnp.where(after < n_act[0],
                       jnp.minimum(jnp.sum((et_end[None, :] <= after[:, None]).astype(jnp.int32), axis=1),
                                   N_EXPERTS - 1), -1)
    return (tile_e.astype(jnp.int32), n_rows_tile.astype(jnp.int32), n_act, src.reshape(-1).astype(jnp.int32),
            used.astype(jnp.int32), first, group.astype(jnp.int32), next_e.astype(jnp.int32), n_tiles)


def _expert_kernel(tile_e, n_rows_t, n_act, src_t, used_t, first_t, group_t, next_t,
                   xs_hbm, wgu_hbm, bgu_ref, wdn_hbm, bdn_ref, ys_hbm,
                   xin, yout, wgu_f, wdn_f, wgu_s, wdn_s, zeros, sem_in, sem_out, sem_zero, sem_wgu, sem_wdn,
                   *, tm, tok_block, n_tok_tiles, layer):
    j = pl.program_id(0)
    na = n_act[0]
    chunks = tm // SUBLANES

    def weight_copies(e, wslot):
        return (pltpu.make_async_copy(wgu_hbm.at[layer, e], wgu_f.at[wslot], sem_wgu.at[wslot]),
                pltpu.make_async_copy(wdn_hbm.at[layer, e], wdn_f.at[wslot], sem_wdn.at[wslot]))

    def gather(src, dst, size, slot):
        return pltpu.make_async_copy(xs_hbm.at[pl.ds(src, size)], xin.at[slot, pl.ds(dst, size)], sem_in.at[slot])

    def scatter(src, dst, size, slot):
        return pltpu.make_async_copy(yout.at[slot, pl.ds(dst, size)], ys_hbm.at[pl.ds(src, size)], sem_out.at[slot])

    def start_chunks(jj, slot, copy):
        def one(c, priority):
            src = pl.multiple_of(src_t[jj * chunks + c], SUBLANES)
            copy(src, pl.multiple_of(c * SUBLANES, SUBLANES), SUBLANES, slot).start(priority=priority)

        def body(c, carry):
            one(c, 0)
            return carry

        def body_unrolled(g, carry):
            for u in range(SUBLANES):
                one(g * SUBLANES + u, u % 2)
            return carry

        n = lax.shift_right_logical(n_rows_t[jj], 3)

        @pl.when(n == chunks)
        def _():
            lax.fori_loop(0, chunks // SUBLANES, body_unrolled, 0)

        @pl.when(n != chunks)
        def _():
            lax.fori_loop(0, n, body, 0)

    def wait_rows(jj, slot, copy):
        n = n_rows_t[jj]
        size = tm
        while size >= SUBLANES:
            @pl.when((n & size) != 0)
            def _(size=size):
                copy(0, 0, size, slot).wait()
            size //= 2

    slot = j % 2

    @pl.when(j == 0)
    def _():
        xin[...] = jnp.zeros_like(xin)
        start_chunks(0, 0, gather)
        zeros[...] = jnp.zeros_like(zeros)

        def clear_tail(i, copy_op):
            used = used_t[i]
            tail = tok_block - used
            for size in (256, 128, 64, 32, 16, 8):
                @pl.when((tail & size) != 0)
                def _():
                    at = pl.multiple_of(i * tok_block + used + (tail & ~(2 * size - 1)), SUBLANES)
                    copy_op(pltpu.make_async_copy(zeros.at[pl.ds(0, size)], ys_hbm.at[pl.ds(at, size)], sem_zero))

        def start_clear(i, c):
            clear_tail(i, lambda cp: cp.start())
            return c

        def wait_clear(i, c):
            clear_tail(i, lambda cp: cp.wait())
            return c

        lax.fori_loop(0, n_tok_tiles, start_clear, 0)
        lax.fori_loop(0, n_tok_tiles, wait_clear, 0)

    @pl.when(j + 1 < na)
    def _():
        start_chunks(j + 1, 1 - slot, gather)

    @pl.when(j < na)
    def _():
        e = tile_e[j]

        @pl.when(first_t[j] == 1)
        def _():
            wslot = group_t[j] % 2

            @pl.when(j == 0)
            def _():
                for cp in weight_copies(e, wslot):
                    cp.start()

            for cp in weight_copies(e, wslot):
                cp.wait()
            wgu_s[...] = wgu_f[wslot].astype(BF16)
            wdn_s[...] = wdn_f[wslot].astype(BF16)

            @pl.when(next_t[j] >= 0)
            def _():
                for cp in weight_copies(next_t[j], 1 - wslot):
                    cp.start()

        wait_rows(j, slot, gather)

        def compute(rows):
            x = xin[slot, pl.ds(0, rows)]
            lane = lax.broadcasted_iota(jnp.int32, (rows, LANES), 1)
            p = jnp.sum(jnp.where(lane % N_EXPERTS == e, x[:, D_MODEL:], 0.0), axis=-1, keepdims=True)
            gu = jnp.dot(x[:, :D_MODEL].astype(BF16), wgu_s[...], preferred_element_type=F32) + bgu_ref[0, 0]
            f = gu.shape[1] // 2
            gate = jnp.minimum(gu[:, :f], SWIGLU_LIMIT)
            up = jnp.clip(gu[:, f:], -SWIGLU_LIMIT, SWIGLU_LIMIT)
            hid = (up + 1.0) * gate * jax.nn.sigmoid(SWIGLU_ALPHA * gate)
            y = jnp.dot(hid.astype(BF16), wdn_s[...], preferred_element_type=F32) + bdn_ref[0, 0]
            yout[slot, pl.ds(0, rows)] = y * p

        @pl.when(n_rows_t[j] > tm // 2)
        def _():
            compute(tm)

        @pl.when(n_rows_t[j] <= tm // 2)
        def _():
            compute(tm // 2)

        start_chunks(j, slot, scatter)

        @pl.when(j >= 1)
        def _():
            wait_rows(j - 1, 1 - slot, scatter)

        @pl.when(j == na - 1)
        def _():
            wait_rows(j, slot, scatter)


def _experts(xs, plan, layer, w_gu, b_gu, w_dn, b_dn):
    tile_e, n_rows_tile, n_act, src, used, first, group, next_e, n_tiles = plan
    tm = TM_EXPERT
    f2 = w_gu.shape[-1]

    def expert(j, te, nr, na, *_):
        return (layer, te[jnp.minimum(j, na[0] - 1)], 0, 0)

    in_specs = [pl.BlockSpec(memory_space=pl.ANY),
                pl.BlockSpec(memory_space=pl.ANY),
                pl.BlockSpec((1, 1, 1, f2), expert),
                pl.BlockSpec(memory_space=pl.ANY),
                pl.BlockSpec((1, 1, 1, D_MODEL), expert)]
    grid_spec = pltpu.PrefetchScalarGridSpec(
        num_scalar_prefetch=8, grid=(n_tiles,), in_specs=in_specs,
        out_specs=pl.BlockSpec(memory_space=pl.ANY),
        scratch_shapes=[pltpu.VMEM((2, tm, XS_WIDTH), F32), pltpu.VMEM((2, tm, D_MODEL), F32),
                        pltpu.VMEM((2, D_MODEL, f2), F32), pltpu.VMEM((2, f2 // 2, D_MODEL), F32),
                        pltpu.VMEM((D_MODEL, f2), BF16), pltpu.VMEM((f2 // 2, D_MODEL), BF16),
                        pltpu.VMEM((ZERO_ROWS, D_MODEL), F32),
                        pltpu.SemaphoreType.DMA((2,)), pltpu.SemaphoreType.DMA((2,)), pltpu.SemaphoreType.DMA(()),
                        pltpu.SemaphoreType.DMA((2,)), pltpu.SemaphoreType.DMA((2,))])
    return pl.pallas_call(
        functools.partial(_expert_kernel, tm=tm, tok_block=TOK_BLOCK, n_tok_tiles=xs.shape[0] // TOK_BLOCK,
                          layer=layer),
        grid_spec=grid_spec,
        out_shape=jax.ShapeDtypeStruct((xs.shape[0], D_MODEL), F32),
        compiler_params=_params(1),
        name="moe_experts",
    )(tile_e, n_rows_tile, n_act, src, used, first, group, next_e,
      xs, w_gu, b_gu.reshape(DEPTH, N_EXPERTS, 1, f2), w_dn, b_dn.reshape(DEPTH, N_EXPERTS, 1, D_MODEL))


def _combine_kernel(ys_ref, pos_ref, h1_ref, g2_ref, l2g_ref, l2b_ref, o_ref):
    tm = pos_ref.shape[0]
    n_sorted = ys_ref.shape[0]
    pos = pos_ref[...]
    col = lax.broadcasted_iota(jnp.int32, (tm, n_sorted), 1).astype(F32)
    sel = jnp.zeros((tm, n_sorted), F32)
    for kk in range(TOP_K):
        sel = sel + (col == pos[:, kk:kk + 1]).astype(F32)
    sel = sel.astype(BF16)
    ys = ys_ref[...]
    hi = ys.astype(BF16)
    rest = ys - hi.astype(F32)
    mid = rest.astype(BF16)
    lo = (rest - mid.astype(F32)).astype(BF16)
    y2 = (jnp.dot(sel, hi, preferred_element_type=F32) + jnp.dot(sel, mid, preferred_element_type=F32)
          + jnp.dot(sel, lo, preferred_element_type=F32))
    o_ref[...] = _layer_norm(DEEPNORM_ALPHA * h1_ref[...] + g2_ref[0] * y2) * l2g_ref[...] + l2b_ref[...]


def _combine(ys, pos4, h1, mod3, mod_base, l2g, l2b, n_lat, seq):
    n_rows = h1.shape[0]
    tm = TM_POST
    tiles_per_seq = seq // tm
    n_lat_tiles = n_lat // tm
    n_groups_lat = n_lat // seq

    def group(i):
        return jnp.where(i < n_lat_tiles, i // tiles_per_seq, n_groups_lat)

    in_specs = [pl.BlockSpec((TOK_BLOCK, D_MODEL), lambda i: (i, 0)),
                pl.BlockSpec((tm, LANES), lambda i: (i, 0)),
                pl.BlockSpec((tm, D_MODEL), lambda i: (i, 0)),
                pl.BlockSpec((1, 1, D_MODEL), lambda i: (mod_base + group(i) * 6 + 5, 0, 0)),
                pl.BlockSpec((1, D_MODEL), lambda i: (0, 0)),
                pl.BlockSpec((1, D_MODEL), lambda i: (0, 0))]
    return pl.pallas_call(
        _combine_kernel,
        grid=(n_rows // tm,),
        in_specs=in_specs,
        out_specs=pl.BlockSpec((tm, D_MODEL), lambda i: (i, 0)),
        out_shape=jax.ShapeDtypeStruct((n_rows, D_MODEL), F32),
        compiler_params=_params(1),
        name="moe_combine",
    )(ys, pos4, h1, mod3, l2g.reshape(1, D_MODEL), l2b.reshape(1, D_MODEL))


def _block_diag(w):
    two, n, d, e = w.shape
    eye = jnp.eye(n, dtype=w.dtype)
    return (w[:, :, :, None, :] * eye[None, :, None, :, None]).reshape(two, n * d, n * e)


def kernel(x, c, ctx, c_ctx, w_mod, b_mod, w_in, b_in, na_rpb, w_proj_attn, w_proj_conv, w_proj_lru, sc_conv_w, lru_conv_w, lru_conv_b, lru_lambda, lru_w_r, lru_b_r, lru_w_i, lru_b_i, w_o, b_o, ln1_g, ln1_b, router_w, router_b, exp_w_gu, exp_b_gu, exp_w_dn, exp_b_dn, ln2_g, ln2_b):
    n_batch, seq, d = x.shape
    n_ctx = ctx.shape[1]
    n_lat = n_batch * seq
    n_all = n_lat + n_batch * n_ctx
    assert d == D_MODEL and n_batch + 1 <= SUBLANES

    cc = jnp.concatenate([c, c_ctx[None], jnp.zeros((SUBLANES - n_batch - 1, d), F32)], axis=0)
    mod = _modulation(cc, w_mod, b_mod)
    groups = n_batch + 1
    mod3 = mod.reshape(DEPTH, SUBLANES, 6, d)[:, :groups].reshape(DEPTH * groups * 6, 1, d)

    cos_t, sin_t = _make_rope(seq, TM_INPROJ)
    h = (x.reshape(n_lat, d), ctx.reshape(n_batch * n_ctx, d))

    for layer in range(DEPTH):
        last = layer == DEPTH - 1
        mod_base = layer * groups * 6
        w_in_bf = w_in[layer].astype(BF16)
        q, k, v, sb, rest = _input_projection(h, mod3, mod_base, w_in_bf, b_in[layer], N_EARLY,
                                              cos_t, sin_t, n_lat, seq)
        sp = jax.nn.softplus(-lru_lambda[layer])
        zb, hf, hb = _conv_scan(sb, rest, sc_conv_w[layer], lru_conv_w[layer], lru_conv_b[layer], sp,
                                _block_diag(lru_w_r[layer]).astype(BF16), _block_diag(lru_w_i[layer]).astype(BF16),
                                lru_b_r[layer], lru_b_i[layer], n_batch, seq, n_ctx)
        att = _attention(q, k, v, _attention_bias(na_rpb[layer], seq), n_batch, seq, n_ctx, not last)
        n_rows = n_lat if last else n_all
        rw_full = jnp.pad(router_w[layer], ((0, 0), (0, LANES - N_EXPERTS)))
        rw_hi = rw_full.astype(BF16)
        rw_pad = jnp.concatenate([rw_hi, (rw_full - rw_hi.astype(F32)).astype(BF16)], axis=1)
        rb_pad = jnp.concatenate([router_b[layer], jnp.full((LANES - N_EXPERTS,), NEG_BIG, F32)]).reshape(1, LANES)
        h1, xs, pos4, cnt_out = _post_mixer(
            h, att, zb, hf, hb, rest, mod3, mod_base, w_in_bf, b_in[layer],
            w_proj_attn[layer].astype(BF16), w_proj_conv[layer].astype(BF16), w_proj_lru[layer].astype(BF16),
            w_o[layer].astype(BF16), b_o[layer], ln1_g[layer], ln1_b[layer], rw_pad, rb_pad, n_rows, n_lat, seq)
        plan = _expert_plan(cnt_out, n_rows // TM_POST, n_rows)
        ys = _experts(xs, plan, layer, exp_w_gu, exp_b_gu, exp_w_dn, exp_b_dn)
        h = (_combine(ys, pos4, h1, mod3, mod_base, ln2_g[layer], ln2_b[layer], n_lat, seq),)
    return h[0].reshape(n_batch, seq, d)
```

```python
import functools

import numpy as np
import jax
import jax.numpy as jnp
from jax import lax
from jax.experimental import pallas as pl
from jax.experimental.pallas import tpu as pltpu

D_MODEL = 1024
DEPTH = 2
GRID_W = 64
NA_HEADS = 8
NA_HEAD_DIM = 64
NA_WIDTH = NA_HEADS * NA_HEAD_DIM
NA_WIN_ROWS = 8
NA_WIN_COLS = 16
ROPE_BASE = 10000.0
BRANCH_WIDTH = 512
LRU_BLOCKS = 8
LRU_C = 8.0
P_TOTAL = 7168
N_EARLY = 8 * BRANCH_WIDTH
N_EXPERTS = 32
TOP_K = 4
SWIGLU_LIMIT = 7.0
SWIGLU_ALPHA = 1.702
LN_EPS = 1e-5
DEEPNORM_ALPHA = (2 * DEPTH) ** 0.25
NEG_BIG = -1e30

LANES = 128
SUBLANES = 8
VMEM_LIMIT_BYTES = 56 * 1024 * 1024

TM_INPROJ = 512
SCAN_CHUNK = 256
ATT_ROWS = 4
ATT_QROWS = ATT_ROWS * GRID_W
TM_POST = 256
TM_EXPERT = 512
XS_WIDTH = D_MODEL + LANES
TOK_BLOCK = TM_POST * TOP_K + N_EXPERTS * SUBLANES
ZERO_ROWS = 256

F32 = jnp.float32
BF16 = jnp.bfloat16


def _params(n_axes):
    return pltpu.CompilerParams(dimension_semantics=("arbitrary",) * n_axes,
                                vmem_limit_bytes=VMEM_LIMIT_BYTES)


def _layer_norm(x):
    mu = jnp.mean(x, axis=-1, keepdims=True)
    xc = x - mu
    var = jnp.mean(xc * xc, axis=-1, keepdims=True)
    return xc * lax.rsqrt(var + LN_EPS)


def _mod_kernel(c_ref, w_ref, b_ref, o_ref):
    c = c_ref[...]
    s = (c * jax.nn.sigmoid(c)).astype(BF16)
    o_ref[0] = jnp.dot(s, w_ref[0].astype(BF16), preferred_element_type=F32) + b_ref[0]


def _modulation(cc, w_mod, b_mod):
    n_out = w_mod.shape[-1]
    return pl.pallas_call(
        _mod_kernel,
        grid=(DEPTH, n_out // D_MODEL),
        in_specs=[pl.BlockSpec((SUBLANES, D_MODEL), lambda l, j: (0, 0)),
                  pl.BlockSpec((1, D_MODEL, D_MODEL), lambda l, j: (l, 0, j)),
                  pl.BlockSpec((1, 1, D_MODEL), lambda l, j: (l, 0, j))],
        out_specs=pl.BlockSpec((1, SUBLANES, D_MODEL), lambda l, j: (l, 0, j)),
        out_shape=jax.ShapeDtypeStruct((DEPTH, SUBLANES, n_out), F32),
        compiler_params=_params(2),
        name="modulation",
    )(cc, w_mod, b_mod.reshape(DEPTH, 1, n_out))


def _rope_half(x, cos, sin_signed):
    lane = lax.broadcasted_iota(jnp.int32, (x.shape[0], LANES), 1)
    first = (lane % 32) < 16
    outs = []
    for cidx in range(x.shape[1] // LANES):
        xc = x[:, cidx * LANES:(cidx + 1) * LANES]
        partner = jnp.where(first, pltpu.roll(xc, LANES - 16, 1), pltpu.roll(xc, 16, 1))
        outs.append(xc * cos + partner * sin_signed)
    return jnp.concatenate(outs, axis=1)


def _inproj_kernel(ha_ref, hb_ref, sh_ref, sc_ref, w_ref, b_ref, cos_ref, sin_ref,
                   q_ref, k_ref, v_ref, sb_ref, rest_ref, *, split):
    h = jnp.where(pl.program_id(0) < split, ha_ref[...], hb_ref[...])
    xn = (_layer_norm(h) * (1.0 + sc_ref[0]) + sh_ref[0]).astype(BF16)
    half = BRANCH_WIDTH

    def columns(lo, width):
        return jnp.dot(xn, w_ref[:, lo:lo + width], preferred_element_type=F32) + b_ref[:, lo:lo + width]

    cos = cos_ref[...]
    sin = sin_ref[...]
    q_ref[...] = _rope_half(columns(0, half), cos, sin).astype(BF16)
    k_ref[...] = _rope_half(columns(half, half), cos, sin).astype(BF16)
    v_ref[...] = columns(2 * half, half).astype(BF16)
    sb_ref[...] = columns(3 * half, half)
    n_rest = rest_ref.shape[1]
    for lo in range(0, n_rest, 2 * half):
        rest_ref[:, lo:lo + 2 * half] = columns(4 * half + lo, 2 * half)


def _input_projection(h_parts, mod3, mod_base, w_in_bf, b_in, n_cols, cos_t, sin_t, n_lat, seq):
    tm = TM_INPROJ
    ha, hb = h_parts[0], h_parts[-1]
    split = ha.shape[0] // tm
    m = ha.shape[0] + (hb.shape[0] if len(h_parts) == 2 else 0)
    n_lat_tiles = n_lat // tm
    tiles_per_seq = seq // tm
    n_groups_lat = n_lat // seq

    def group(i):
        return jnp.where(i < n_lat_tiles, i // tiles_per_seq, n_groups_lat)

    def rope_blk(i):
        return jnp.where(i < n_lat_tiles, i % tiles_per_seq, tiles_per_seq)

    half = BRANCH_WIDTH
    return pl.pallas_call(
        functools.partial(_inproj_kernel, split=split),
        grid=(m // tm,),
        in_specs=[pl.BlockSpec((tm, D_MODEL), lambda i: (jnp.minimum(i, split - 1), 0)),
                  pl.BlockSpec((tm, D_MODEL), lambda i: (jnp.maximum(i - split, 0), 0)),
                  pl.BlockSpec((1, 1, D_MODEL), lambda i: (mod_base + group(i) * 6 + 0, 0, 0)),
                  pl.BlockSpec((1, 1, D_MODEL), lambda i: (mod_base + group(i) * 6 + 1, 0, 0)),
                  pl.BlockSpec((D_MODEL, n_cols), lambda i: (0, 0)),
                  pl.BlockSpec((1, n_cols), lambda i: (0, 0)),
                  pl.BlockSpec((tm, LANES), lambda i: (rope_blk(i), 0)),
                  pl.BlockSpec((tm, LANES), lambda i: (rope_blk(i), 0))],
        out_specs=[pl.BlockSpec((tm, half), lambda i: (i, 0)),
                   pl.BlockSpec((tm, half), lambda i: (i, 0)),
                   pl.BlockSpec((tm, half), lambda i: (i, 0)),
                   pl.BlockSpec((tm, half), lambda i: (i, 0)),
                   pl.BlockSpec((tm, n_cols - 4 * half), lambda i: (i, 0))],
        out_shape=[jax.ShapeDtypeStruct((m, half), BF16),
                   jax.ShapeDtypeStruct((m, half), BF16),
                   jax.ShapeDtypeStruct((m, half), BF16),
                   jax.ShapeDtypeStruct((m, half), F32),
                   jax.ShapeDtypeStruct((m, n_cols - 4 * half), F32)],
        compiler_params=_params(1),
        name="input_projection",
    )(ha, hb, mod3, mod3, w_in_bf, b_in.reshape(1, -1), cos_t, sin_t)


def _make_rope(seq, tm):
    t = np.arange(seq)
    row_pos, col_pos = t // GRID_W, t % GRID_W
    d = np.arange(LANES) % NA_HEAD_DIM
    m = NA_HEAD_DIM // 4
    inv_freq = (ROPE_BASE ** (-jnp.arange(m, dtype=F32) / m))[d % m]
    pos = np.where((d < 2 * m)[None, :], row_pos[:, None], col_pos[:, None])
    ang = jnp.asarray(pos).astype(F32) * inv_freq[None, :]
    cos = jnp.cos(ang)
    sin = jnp.sin(ang)
    sin_signed = jnp.where(jnp.asarray((d % (2 * m)) < m)[None, :], -sin, sin)
    cos = jnp.concatenate([cos, jnp.ones((tm, LANES), F32)], axis=0)
    sin_signed = jnp.concatenate([sin_signed, jnp.zeros((tm, LANES), F32)], axis=0)
    return cos, sin_signed


def _mixer_kernel(fblk, bblk, first, last, seqb, r0t, cls,
                  sb_ref, scg_ref, sx_ref, lxf_ref,
                  scgp_ref, sxp_ref, lxfp_ref, scgn_ref, sxn_ref, lxfn_ref,
                  lxb_ref, lxbp_ref, lxbn_ref,
                  scw_ref, cw_ref, cb_ref, sp_ref, wr_ref, wi_ref, br_ref, bi_ref,
                  q_ref, k_ref, v_ref, kc_ref, vc_ref, bias0_ref, bias1_ref, bias2_ref, bias3_ref,
                  zb_ref, hf_ref, hb_ref, att_ref,
                  a_s, b_s, hc_s, st_s, *, ch, n_ctx_items, band):
    it = pl.program_id(0)
    is_first = first[it] == 1
    is_last = last[it] == 1
    is_ctx = it < n_ctx_items
    b = seqb[it]
    width = BRANCH_WIDTH
    row = lax.broadcasted_iota(jnp.int32, (ch, width), 0)
    not_first = jnp.where(is_first, 0.0, 1.0).astype(F32)
    not_last = jnp.where(is_last, 0.0, 1.0).astype(F32)

    def back1(u, prev_row):
        return jnp.where(row == 0, prev_row, pltpu.roll(u, 1, 0))

    def back2(u, prev2, prev1):
        return jnp.where(row == 0, prev2, jnp.where(row == 1, prev1, pltpu.roll(u, 2, 0)))

    def fwd1(u, next_row):
        return jnp.where(row == ch - 1, next_row, pltpu.roll(u, ch - 1, 0))

    def lru_input(lx_ref, lxp_ref, lxn_ref, prev_ok, next_ok):
        x = lx_ref[...]
        p = lxp_ref[...] * prev_ok
        n = lxn_ref[...] * next_ok
        return (cw_ref[0:1] * back2(x, p[6:7], p[7:8]) + cw_ref[1:2] * back1(x, p[7:8])
                + cw_ref[2:3] * x + cw_ref[3:4] * fwd1(x, n[0:1]) + cb_ref[...])

    def coeffs(d, xm):
        xb = xm.astype(BF16)
        r = jax.nn.sigmoid(jnp.dot(xb, wr_ref[d], preferred_element_type=F32) + br_ref[d:d + 1])
        g = jax.nn.sigmoid(jnp.dot(xb, wi_ref[d], preferred_element_type=F32) + bi_ref[d:d + 1])
        log_a = (-LRU_C * sp_ref[d:d + 1]) * r
        a = jnp.exp(log_a)
        a_s[d] = a
        b_s[d] = jnp.sqrt(-jnp.tanh(log_a) * (a * a + 1.0)) * (g * xm)

    @pl.when(jnp.logical_and(is_first, is_ctx))
    def _():
        hc_s[...] = jnp.zeros_like(hc_s)

    @pl.when(jnp.logical_and(is_first, jnp.logical_not(is_ctx)))
    def _():
        hc_s[0:1] = st_s[pl.ds(2 * b, 1), :]
        hc_s[1:2] = st_s[pl.ds(2 * b + 1, 1), :]

    coeffs(0, lru_input(lxf_ref, lxfp_ref, lxfn_ref, not_first, not_last))
    hf = hc_s[0:1]
    for t in range(ch):
        hf = a_s[0, t:t + 1, :] * hf + b_s[0, t:t + 1, :]
        hf_ref[t:t + 1, :] = hf
    hc_s[0:1] = hf

    coeffs(1, lru_input(lxb_ref, lxbp_ref, lxbn_ref, not_last, not_first))

    u = scg_ref[...] * sx_ref[...]
    u_prev = scgp_ref[7:8] * sxp_ref[7:8] * not_first
    u_next = scgn_ref[0:1] * sxn_ref[0:1] * not_last
    conv = scw_ref[0:1] * back1(u, u_prev) + scw_ref[1:2] * u + scw_ref[2:3] * fwd1(u, u_next)
    zb_ref[...] = (sb_ref[...] * conv).astype(BF16)

    hb = hc_s[1:2]
    for t in range(ch - 1, -1, -1):
        hb = a_s[1, t:t + 1, :] * hb + b_s[1, t:t + 1, :]
        hb_ref[t:t + 1, :] = hb
    hc_s[1:2] = hb

    _attention_item(it, r0t, cls, q_ref, k_ref, v_ref, kc_ref, vc_ref,
                    (bias0_ref, bias1_ref, bias2_ref, bias3_ref), att_ref, band)

    @pl.when(is_ctx)
    def _():
        st_s[pl.ds(2 * b, 1), :] = hf
        st_s[pl.ds(2 * b + 1, 1), :] = hb


def _mixer_tables(n_batch, seq, n_ctx, ch):
    assert n_ctx == ch and ch == ATT_QROWS
    nc = seq // ch
    rows = seq // GRID_W
    kr = min(NA_WIN_ROWS, rows)
    ctx0 = n_batch * seq // ch
    fblk, bblk, first, last, seqb, r0t, cls = [], [], [], [], [], [], []
    for b in range(n_batch):
        fblk.append(ctx0 + b); bblk.append(ctx0 + b); first.append(1); last.append(1); seqb.append(b)
        r0t.extend([0] * ATT_ROWS); cls.extend([kr] * ATT_ROWS)
    for b in range(n_batch):
        for c in range(nc):
            fblk.append(b * nc + c); bblk.append(b * nc + nc - 1 - c)
            first.append(int(c == 0)); last.append(int(c == nc - 1)); seqb.append(b)
            for r in range(c * ATT_ROWS, (c + 1) * ATT_ROWS):
                r0 = min(max(r - kr // 2, 0), rows - kr)
                r0t.append(r0); cls.append(r - r0)
    return [np.asarray(a, np.int32) for a in (fblk, bblk, first, last, seqb, r0t, cls)]


def _token_mixers(q, k, v, bias, sb, rest, sc_w, lru_cw, lru_cb, sp, wr_bd, wi_bd, b_r, b_i, n_batch, seq, n_ctx):
    m = sb.shape[0]
    ch = SCAN_CHUNK
    width = BRANCH_WIDTH
    tables = _mixer_tables(n_batch, seq, n_ctx, ch)
    n_items = len(tables[0])
    halo_per_chunk = ch // SUBLANES
    last_halo = m // SUBLANES - 1
    band = min(NA_WIN_ROWS, seq // GRID_W) * GRID_W
    ctx_blk0 = n_batch * seq // n_ctx

    def cur(col, which):
        return pl.BlockSpec((ch, width), lambda i, f, bk, *_: ((f, bk)[which][i], col))

    def prev(col, which):
        return pl.BlockSpec((SUBLANES, width),
                            lambda i, f, bk, *_: (jnp.maximum((f, bk)[which][i] * halo_per_chunk - 1, 0), col))

    def nxt(col, which):
        return pl.BlockSpec((SUBLANES, width),
                            lambda i, f, bk, *_: (jnp.minimum(((f, bk)[which][i] + 1) * halo_per_chunk, last_halo), col))

    def full(shape):
        return pl.BlockSpec(shape, lambda i, *_: (0,) * len(shape))

    in_specs = [cur(0, 0), cur(0, 0), cur(1, 0), cur(2, 0),
                prev(0, 0), prev(1, 0), prev(2, 0), nxt(0, 0), nxt(1, 0), nxt(2, 0),
                cur(2, 1), prev(2, 1), nxt(2, 1),
                full(sc_w.shape), full(lru_cw.shape), full((1, width)), full(sp.shape),
                full(wr_bd.shape), full(wi_bd.shape), full(b_r.shape), full(b_i.shape)]
    in_specs += [cur(0, 0),
                 pl.BlockSpec((seq, width), lambda i, f, bk, fi, la, sq, *_: (sq[i], 0)),
                 pl.BlockSpec((seq, width), lambda i, f, bk, fi, la, sq, *_: (sq[i], 0)),
                 pl.BlockSpec((n_ctx, width), lambda i, f, bk, fi, la, sq, *_: (ctx_blk0 + sq[i], 0)),
                 pl.BlockSpec((n_ctx, width), lambda i, f, bk, fi, la, sq, *_: (ctx_blk0 + sq[i], 0))]
    for row in range(ATT_ROWS):
        in_specs.append(pl.BlockSpec((1, NA_HEADS, GRID_W, band),
                                     lambda i, f, bk, fi, la, sq, r0, cl, row=row: (cl[i * ATT_ROWS + row], 0, 0, 0)))
    out_specs = [cur(0, 0), cur(0, 0), cur(0, 1), cur(0, 0)]
    grid_spec = pltpu.PrefetchScalarGridSpec(
        num_scalar_prefetch=7, grid=(n_items,), in_specs=in_specs, out_specs=out_specs,
        scratch_shapes=[pltpu.VMEM((2, ch, width), F32), pltpu.VMEM((2, ch, width), F32),
                        pltpu.VMEM((SUBLANES, width), F32), pltpu.VMEM((2 * n_batch, width), F32)])
    return pl.pallas_call(
        functools.partial(_mixer_kernel, ch=ch, n_ctx_items=n_batch, band=band),
        grid_spec=grid_spec,
        out_shape=[jax.ShapeDtypeStruct((m, width), BF16),
                   jax.ShapeDtypeStruct((m, width), F32),
                   jax.ShapeDtypeStruct((m, width), F32),
                   jax.ShapeDtypeStruct((m, width), BF16)],
        compiler_params=_params(1),
        name="token_mixers",
    )(*[jnp.asarray(t) for t in tables],
      sb, rest, rest, rest, rest, rest, rest, rest, rest, rest, rest, rest, rest,
      sc_w, lru_cw, lru_cb.reshape(1, width), sp, wr_bd, wi_bd, b_r, b_i,
      q, k, v, k, v, bias, bias, bias, bias)


def _attention_item(it, r0t, cls, q_ref, k_ref, v_ref, kc_ref, vc_ref, bias_refs, o_ref, band):
    nq = GRID_W
    lane = lax.broadcasted_iota(jnp.int32, (nq, LANES), 1)
    low = lane < NA_HEAD_DIM
    scale = NA_HEAD_DIM ** -0.5
    nt = (((1,), (1,)), ((), ()))
    for row in range(ATT_ROWS):
        start = pl.multiple_of(r0t[it * ATT_ROWS + row] * GRID_W, GRID_W)
        bias_ref = bias_refs[row]
        qrows = slice(row * nq, (row + 1) * nq)
        for hp in range(NA_HEADS // 2):
            cols = slice(hp * LANES, (hp + 1) * LANES)
            qp = q_ref[qrows, cols].astype(F32) * scale
            qs = jnp.concatenate([jnp.where(low, qp, 0.0), jnp.where(low, 0.0, qp)], axis=0).astype(BF16)
            kb = k_ref[pl.ds(start, band), cols]
            vb = v_ref[pl.ds(start, band), cols]
            s_loc = lax.dot_general(qs, kb, nt, preferred_element_type=F32)
            s_ctx = lax.dot_general(qs, kc_ref[:, cols], nt, preferred_element_type=F32)
            bias = jnp.concatenate([bias_ref[0, 2 * hp], bias_ref[0, 2 * hp + 1]], axis=0)
            s_loc = s_loc + bias
            mx =jnp.maximum(jnp.max(s_loc, axis=-1, keepdims=True), jnp.max(s_ctx, axis=-1, keepdims=True))
            e_loc = jnp.exp(s_loc - mx)
            e_ctx = jnp.exp(s_ctx - mx)
            den = jnp.sum(e_loc, axis=-1, keepdims=True) + jnp.sum(e_ctx, axis=-1, keepdims=True)
            o = (jnp.dot(e_loc.astype(BF16), vb, preferred_element_type=F32)
                 + jnp.dot(e_ctx.astype(BF16), vc_ref[:, cols], preferred_element_type=F32)) / den
            o_ref[qrows, cols] = jnp.where(low, o[:nq], o[nq:]).astype(BF16)


def _attention_bias(rpb, seq):
    rows = seq // GRID_W
    kr = min(NA_WIN_ROWS, rows)
    kc = NA_WIN_COLS
    cq = np.arange(GRID_W)
    c0 = np.clip(cq - kc // 2, 0, GRID_W - kc)
    ck = np.arange(GRID_W)
    inside = (ck[None, :] >= c0[:, None]) & (ck[None, :] < c0[:, None] + kc)
    dc = np.clip(ck[None, :] - cq[:, None] + (NA_WIN_COLS - 1), 0, 2 * NA_WIN_COLS - 2)
    n_dr = 2 * NA_WIN_ROWS - 1
    n_dc = 2 * NA_WIN_COLS - 1
    pick = jnp.asarray((np.arange(n_dc)[:, None] == dc.reshape(1, -1)).astype(np.float32))
    picked = jnp.dot(rpb.reshape(-1, n_dc), pick, precision=lax.Precision.HIGHEST)
    picked = picked.reshape(NA_HEADS, 2 * NA_WIN_ROWS - 1, GRID_W, GRID_W)
    table = jnp.where(jnp.asarray(inside)[None, None], picked, NEG_BIG)
    table = table.transpose(0, 2, 1, 3).reshape(NA_HEADS, GRID_W, n_dr * GRID_W)
    classes = []
    for cl in range(kr):
        lo = (NA_WIN_ROWS - 1 - cl) * GRID_W
        classes.append(table[:, :, lo:lo + kr * GRID_W])
    classes.append(jnp.full((NA_HEADS, GRID_W, kr * GRID_W), NEG_BIG, F32))
    return jnp.stack(classes, axis=0)


def _post_kernel(ha_ref, hc_ref, att_ref, zb_ref, hf_ref, hb_ref, lg_ref,
                 sh1_ref, sc1_ref, g1_ref, sh2_ref, sc2_ref,
                 wgl0_ref, wgl1_ref, wgl2_ref, bgl0_ref, bgl1_ref, bgl2_ref,
                 wpa_ref, wpc_ref, wpl_ref, wo_ref, bo_ref, l1g_ref, l1b_ref,
                 rw_ref, rb_ref, tri_ref, upper_ref,
                 h1_ref, xs_ref, pos_ref, cnt_ref, *, split):
    h = jnp.where(pl.program_id(0) < split, ha_ref[...], hc_ref[...])
    u1 = (_layer_norm(h) * (1.0 + sc1_ref[0]) + sh1_ref[0]).astype(BF16)
    y_a = jnp.dot(att_ref[...], wpa_ref[...], preferred_element_type=F32)
    y_b = jnp.dot(zb_ref[...], wpc_ref[...], preferred_element_type=F32)
    zc = jax.nn.gelu(lg_ref[...]) * (hf_ref[...] + hb_ref[...])
    y_c = jnp.dot(zc.astype(BF16), wpl_ref[...], preferred_element_type=F32)
    merged = (jax.nn.sigmoid(jnp.dot(u1, wgl0_ref[...], preferred_element_type=F32) + bgl0_ref[...]) * y_a
              + jax.nn.sigmoid(jnp.dot(u1, wgl1_ref[...], preferred_element_type=F32) + bgl1_ref[...]) * y_b
              + jax.nn.sigmoid(jnp.dot(u1, wgl2_ref[...], preferred_element_type=F32) + bgl2_ref[...]) * y_c)
    y = jnp.dot(merged.astype(BF16), wo_ref[...], preferred_element_type=F32) + bo_ref[...]
    h1 = _layer_norm(DEEPNORM_ALPHA * h + g1_ref[0] * y) * l1g_ref[...] + l1b_ref[...]
    h1_ref[...] = h1
    u2 = _layer_norm(h1) * (1.0 + sc2_ref[0]) + sh2_ref[0]

    u_hi = u2.astype(BF16)
    u_lo = (u2 - u_hi.astype(F32)).astype(BF16)
    by_hi = jnp.dot(u_hi, rw_ref[...], preferred_element_type=F32)
    logits = (by_hi[:, :LANES] + by_hi[:, LANES:]
              + jnp.dot(u_lo, rw_ref[:, :LANES], preferred_element_type=F32) + rb_ref[...])
    tm = logits.shape[0]
    lane = lax.broadcasted_iota(jnp.int32, (tm, LANES), 1)
    lane_f = lane.astype(F32)
    work = logits
    tops, idxs, hots = [], [], []
    for _ in range(TOP_K):
        mx = jnp.max(work, axis=-1, keepdims=True)
        idx = jnp.min(jnp.where(work == mx, lane_f, float(LANES)), axis=-1, keepdims=True)
        hot = lane_f == idx
        work = jnp.where(hot, -3e38, work)
        tops.append(mx); idxs.append(idx); hots.append(hot)
    exps = [jnp.exp(t - tops[0]) for t in tops]
    den = exps[0] + exps[1] + exps[2] + exps[3]
    hot_all = jnp.zeros((tm, LANES), F32)
    for hot in hots:
        hot_all = hot_all + hot.astype(F32)
    cnt = jnp.sum(hot_all, axis=0, keepdims=True)
    cnt_pad = jnp.floor((cnt + (SUBLANES - 1.0)) * (1.0 / SUBLANES)) * SUBLANES
    off = jnp.dot(jnp.broadcast_to(cnt_pad, (SUBLANES, LANES)), upper_ref[...],
                  precision=lax.Precision.HIGHEST, preferred_element_type=F32)[0:1]
    slot = off + jnp.dot(tri_ref[...], hot_all.astype(BF16), preferred_element_type=F32)
    pos4 = jnp.zeros((tm, LANES), F32)
    w_tile = jnp.zeros((tm, LANES), F32)
    for kk in range(TOP_K):
        pos_k = jnp.sum(jnp.where(hots[kk], slot, 0.0), axis=-1, keepdims=True)
        pos4 = jnp.where(lane == kk, pos_k, pos4)
        p = exps[kk] / den
        p_hi = p.astype(BF16).astype(F32)
        p_mid = (p - p_hi).astype(BF16).astype(F32)
        p_lo = p - p_hi - p_mid
        w_tile = jnp.where(hots[kk], p_hi, w_tile)
        w_tile = jnp.where(lane_f == idxs[kk] + float(N_EXPERTS), p_mid, w_tile)
        w_tile = jnp.where(lane_f == idxs[kk] + float(2 * N_EXPERTS), p_lo, w_tile)
    pos_ref[...] = pos4
    sub = lax.broadcasted_iota(jnp.int32, (SUBLANES, LANES), 0)
    cnt_ref[...] = jnp.where(sub == 0, cnt, jnp.where(sub == 1, off, 0.0))

    n_sorted = xs_ref.shape[0]
    pos_t = pos4.T
    r_iota = lax.broadcasted_iota(jnp.int32, (n_sorted, tm), 0).astype(F32)
    hit = r_iota == pos_t[0:1, :]
    for kk in range(1, TOP_K):
        hit = jnp.logical_or(hit, r_iota == pos_t[kk:kk + 1, :])
    perm = jnp.where(hit, 1.0, 0.0).astype(BF16)
    feats = jnp.concatenate([u_hi, w_tile.astype(BF16)], axis=1)
    xs_ref[...] = jnp.dot(perm, feats, preferred_element_type=F32)


def _post_mixer(h_parts, att, zb, hf, hb, rest, mod3, mod_base, w_in_bf, b_in, wpa, wpc, wpl, wo, b_o, l1g, l1b,
                rw_pad, rb_pad, n_rows, n_lat, seq):
    tm = TM_POST
    width = BRANCH_WIDTH
    ha, hc = h_parts[0], h_parts[-1]
    split = min(ha.shape[0], n_rows) // tm
    gate_col0 = N_EARLY // D_MODEL
    tiles_per_seq = seq // tm
    n_lat_tiles = n_lat // tm
    n_groups_lat = n_lat // seq
    tri = jnp.asarray(np.tril(np.ones((tm, tm), np.float32), -1), BF16)
    upper = jnp.asarray(np.triu(np.ones((LANES, LANES), np.float32), 1))
    n_tiles = n_rows // tm

    def group(i):
        return jnp.where(i < n_lat_tiles, i // tiles_per_seq, n_groups_lat)

    def rows(wd, col=0):
        return pl.BlockSpec((tm, wd), lambda i: (i, col))

    def full(shape):
        return pl.BlockSpec(shape, lambda i: (0,) * len(shape))

    def mod(which):
        return pl.BlockSpec((1, 1, D_MODEL), lambda i: (mod_base + group(i) * 6 + which, 0, 0))

    in_specs = [pl.BlockSpec((tm, D_MODEL), lambda i: (jnp.minimum(i, split - 1), 0)),
                pl.BlockSpec((tm, D_MODEL), lambda i: (jnp.maximum(i - split, 0), 0)),
                rows(width), rows(width), rows(width), rows(width), rows(width, 3),
                mod(0), mod(1), mod(2), mod(3), mod(4)]
    in_specs += [pl.BlockSpec((D_MODEL, D_MODEL), lambda i, c=c: (0, gate_col0 + c)) for c in range(3)]
    in_specs += [pl.BlockSpec((1, D_MODEL), lambda i, c=c: (0, gate_col0 + c)) for c in range(3)]
    in_specs += [full(wpa.shape), full(wpc.shape), full(wpl.shape), full(wo.shape), full((1, D_MODEL)),
                full((1, D_MODEL)), full((1, D_MODEL)), full(rw_pad.shape), full(rb_pad.shape), full(tri.shape),
                full(upper.shape)]
    out_specs = [rows(D_MODEL), pl.BlockSpec((TOK_BLOCK, XS_WIDTH), lambda i: (i, 0)), rows(LANES),
                 pl.BlockSpec((SUBLANES, LANES), lambda i: (i, 0))]
    return pl.pallas_call(
        functools.partial(_post_kernel, split=split),
        grid=(n_tiles,),
        in_specs=in_specs, out_specs=out_specs,
        out_shape=[jax.ShapeDtypeStruct((n_rows, D_MODEL), F32),
                   jax.ShapeDtypeStruct((n_tiles * TOK_BLOCK, XS_WIDTH), F32),
                   jax.ShapeDtypeStruct((n_rows, LANES), F32),
                   jax.ShapeDtypeStruct((n_tiles * SUBLANES, LANES), F32)],
        compiler_params=_params(1),
        name="post_mixer",
    )(ha, hc, att, zb, hf, hb, rest, mod3, mod3, mod3, mod3, mod3,
      w_in_bf, w_in_bf, w_in_bf, b_in.reshape(1, -1), b_in.reshape(1, -1), b_in.reshape(1, -1),
      wpa, wpc, wpl, wo, b_o.reshape(1, D_MODEL), l1g.reshape(1, D_MODEL), l1b.reshape(1, D_MODEL),
      rw_pad, rb_pad, tri, upper)


def _expert_plan(cnt_out, n_tok_tiles, n_rows):
    tm = TM_EXPERT
    co = cnt_out.reshape(n_tok_tiles, SUBLANES, LANES)
    cnt = (co[:, 0, :N_EXPERTS].astype(jnp.int32) + SUBLANES - 1) // SUBLANES * SUBLANES
    off = co[:, 1, :N_EXPERTS].astype(jnp.int32)
    cum_end = jnp.cumsum(cnt, axis=0)
    cum = cum_end - cnt
    total = cum_end[-1]
    n_et = (total + tm - 1) // tm
    et_end = jnp.cumsum(n_et)
    n_act = et_end[-1:].astype(jnp.int32)
    n_tiles = -(-n_tok_tiles * TOK_BLOCK // tm) + N_EXPERTS
    j = jnp.arange(n_tiles, dtype=jnp.int32)
    tile_e = jnp.minimum(jnp.sum((et_end[None, :] <= j[:, None]).astype(jnp.int32), axis=1), N_EXPERTS - 1)
    pick_e = (tile_e[:, None] == jnp.arange(N_EXPERTS, dtype=jnp.int32)[None, :]).astype(F32)

    def per_tile(table):
        return jnp.dot(pick_e, table.astype(F32), precision=lax.Precision.HIGHEST).astype(jnp.int32)

    row0 = (j - per_tile(et_end - n_et)) * tm
    n_rows_tile = jnp.clip(per_tile(total) - row0, 0, tm)
    cum_e = per_tile(cum.T)
    cum_end_e = per_tile(cum_end.T)
    delta_e = per_tile((jnp.arange(n_tok_tiles, dtype=jnp.int32)[:, None] * TOK_BLOCK + off - cum).T)
    q = row0[:, None] + SUBLANES * jnp.arange(tm // SUBLANES, dtype=jnp.int32)[None, :]
    inside = jnp.logical_and(cum_e.T[:, :, None] <= q[None], q[None] < cum_end_e.T[:, :, None])
    src = q + jnp.sum(jnp.where(inside, delta_e.T[:, :, None], 0), axis=0)
    used = off[:, N_EXPERTS - 1] + cnt[:, N_EXPERTS - 1]
    first = jnp.concatenate([jnp.ones((1,), jnp.int32), (tile_e[1:] != tile_e[:-1]).astype(jnp.int32)])
    group = jnp.cumsum(first) - 1
    after = per_tile(et_end)
    next_e = jnp.where(after < n_act[0],
                       jnp.minimum(jnp.sum((et_end[None, :] <= after[:, None]).astype(jnp.int32), axis=1),
                                   N_EXPERTS - 1), -1)
    return (tile_e.astype(jnp.int32), n_rows_tile.astype(jnp.int32), n_act, src.reshape(-1).astype(jnp.int32),
            used.astype(jnp.int32), first, group.astype(jnp.int32), next_e.astype(jnp.int32), n_tiles)


def _expert_kernel(tile_e, n_rows_t, n_act, src_t, used_t, first_t, group_t, next_t,
                   xs_hbm, wgu_hbm, bgu_ref, wdn_hbm, bdn_ref, ys_hbm,
                   xin, yout, wgu_f, wdn_f, wgu_s, wdn_s, zeros, sem_in, sem_out, sem_zero, sem_wgu, sem_wdn,
                   *, tm, tok_block, n_tok_tiles, layer):
    j = pl.program_id(0)
    na = n_act[0]
    chunks = tm // SUBLANES

    def weight_copies(e, wslot):
        return (pltpu.make_async_copy(wgu_hbm.at[layer, e], wgu_f.at[wslot], sem_wgu.at[wslot]),
                pltpu.make_async_copy(wdn_hbm.at[layer, e], wdn_f.at[wslot], sem_wdn.at[wslot]))

    def gather(src, dst, size, slot):
        return pltpu.make_async_copy(xs_hbm.at[pl.ds(src, size)], xin.at[slot, pl.ds(dst, size)], sem_in.at[slot])

    def scatter(src, dst, size, slot):
        return pltpu.make_async_copy(yout.at[slot, pl.ds(dst, size)], ys_hbm.at[pl.ds(src, size)], sem_out.at[slot])

    def start_chunks(jj, slot, copy):
        def one(c, priority):
            src = pl.multiple_of(src_t[jj * chunks + c], SUBLANES)
            copy(src, pl.multiple_of(c * SUBLANES, SUBLANES), SUBLANES, slot).start(priority=priority)

        def body(c, carry):
            one(c, 0)
            return carry

        def body_unrolled(g, carry):
            for u in range(SUBLANES):
                one(g * SUBLANES + u, u % 2)
            return carry

        n = lax.shift_right_logical(n_rows_t[jj], 3)

        @pl.when(n == chunks)
        def _():
            lax.fori_loop(0, chunks // SUBLANES, body_unrolled, 0)

        @pl.when(n != chunks)
        def _():
            lax.fori_loop(0, n, body, 0)

    def wait_rows(jj, slot, copy):
        n = n_rows_t[jj]
        size = tm
        while size >= SUBLANES:
            @pl.when((n & size) != 0)
            def _(size=size):
                copy(0, 0, size, slot).wait()
            size //= 2

    slot = j % 2

    @pl.when(j == 0)
    def _():
        xin[...] = jnp.zeros_like(xin)
        start_chunks(0, 0, gather)
        zeros[...] = jnp.zeros_like(zeros)

        def clear_tail(i, copy_op):
            used = used_t[i]
            tail = tok_block - used
            for size in (256, 128, 64, 32, 16, 8):
                @pl.when((tail & size) != 0)
                def _():
                    at = pl.multiple_of(i * tok_block + used + (tail & ~(2 * size - 1)), SUBLANES)
                    copy_op(pltpu.make_async_copy(zeros.at[pl.ds(0, size)], ys_hbm.at[pl.ds(at, size)], sem_zero))

        def start_clear(i, c):
            clear_tail(i, lambda cp: cp.start())
            return c

        def wait_clear(i, c):
            clear_tail(i, lambda cp: cp.wait())
            return c

        lax.fori_loop(0, n_tok_tiles, start_clear, 0)
        lax.fori_loop(0, n_tok_tiles, wait_clear, 0)

    @pl.when(j + 1 < na)
    def _():
        start_chunks(j + 1, 1 - slot, gather)

    @pl.when(j < na)
    def _():
        e = tile_e[j]

        @pl.when(first_t[j] == 1)
        def _():
            wslot = group_t[j] % 2

            @pl.when(j == 0)
            def _():
                for cp in weight_copies(e, wslot):
                    cp.start()

            for cp in weight_copies(e, wslot):
                cp.wait()
            wgu_s[...] = wgu_f[wslot].astype(BF16)
            wdn_s[...] = wdn_f[wslot].astype(BF16)

            @pl.when(next_t[j] >= 0)
            def _():
                for cp in weight_copies(next_t[j], 1 - wslot):
                    cp.start()

        wait_rows(j, slot, gather)

        def compute(rows):
            x = xin[slot, pl.ds(0, rows)]
            lane = lax.broadcasted_iota(jnp.int32, (rows, LANES), 1)
            p = jnp.sum(jnp.where(lane % N_EXPERTS == e, x[:, D_MODEL:], 0.0), axis=-1, keepdims=True)
            gu = jnp.dot(x[:, :D_MODEL].astype(BF16), wgu_s[...], preferred_element_type=F32) + bgu_ref[0, 0]
            f = gu.shape[1] // 2
            gate = jnp.minimum(gu[:, :f], SWIGLU_LIMIT)
            up = jnp.clip(gu[:, f:], -SWIGLU_LIMIT, SWIGLU_LIMIT)
            hid = (up + 1.0) * gate * jax.nn.sigmoid(SWIGLU_ALPHA * gate)
            y = jnp.dot(hid.astype(BF16), wdn_s[...], preferred_element_type=F32) + bdn_ref[0, 0]
            yout[slot, pl.ds(0, rows)] = y * p

        @pl.when(n_rows_t[j] > tm // 2)
        def _():
            compute(tm)

        @pl.when(n_rows_t[j] <= tm // 2)
        def _():
            compute(tm // 2)

        start_chunks(j, slot, scatter)

        @pl.when(j >= 1)
        def _():
            wait_rows(j - 1, 1 - slot, scatter)

        @pl.when(j == na - 1)
        def _():
            wait_rows(j, slot, scatter)


def _experts(xs, plan, layer, w_gu, b_gu, w_dn, b_dn):
    tile_e, n_rows_tile, n_act, src, used, first, group, next_e, n_tiles = plan
    tm = TM_EXPERT
    f2 = w_gu.shape[-1]

    def expert(j, te, nr, na, *_):
        return (layer, te[jnp.minimum(j, na[0] - 1)], 0, 0)

    in_specs = [pl.BlockSpec(memory_space=pl.ANY),
                pl.BlockSpec(memory_space=pl.ANY),
                pl.BlockSpec((1, 1, 1, f2), expert),
                pl.BlockSpec(memory_space=pl.ANY),
                pl.BlockSpec((1, 1, 1, D_MODEL), expert)]
    grid_spec = pltpu.PrefetchScalarGridSpec(
        num_scalar_prefetch=8, grid=(n_tiles,), in_specs=in_specs,
        out_specs=pl.BlockSpec(memory_space=pl.ANY),
        scratch_shapes=[pltpu.VMEM((2, tm, XS_WIDTH), F32), pltpu.VMEM((2, tm, D_MODEL), F32),
                        pltpu.VMEM((2, D_MODEL, f2), F32), pltpu.VMEM((2, f2 // 2, D_MODEL), F32),
                        pltpu.VMEM((D_MODEL, f2), BF16), pltpu.VMEM((f2 // 2, D_MODEL), BF16),
                        pltpu.VMEM((ZERO_ROWS, D_MODEL), F32),
                        pltpu.SemaphoreType.DMA((2,)), pltpu.SemaphoreType.DMA((2,)), pltpu.SemaphoreType.DMA(()),
                        pltpu.SemaphoreType.DMA((2,)), pltpu.SemaphoreType.DMA((2,))])
    return pl.pallas_call(
        functools.partial(_expert_kernel, tm=tm, tok_block=TOK_BLOCK, n_tok_tiles=xs.shape[0] // TOK_BLOCK,
                          layer=layer),
        grid_spec=grid_spec,
        out_shape=jax.ShapeDtypeStruct((xs.shape[0], D_MODEL), F32),
        compiler_params=_params(1),
        name="moe_experts",
    )(tile_e, n_rows_tile, n_act, src, used, first, group, next_e,
      xs, w_gu, b_gu.reshape(DEPTH, N_EXPERTS, 1, f2), w_dn, b_dn.reshape(DEPTH, N_EXPERTS, 1, D_MODEL))


def _combine_kernel(ys_ref, pos_ref, h1_ref, g2_ref, l2g_ref, l2b_ref, o_ref):
    tm = pos_ref.shape[0]
    n_sorted = ys_ref.shape[0]
    pos = pos_ref[...]
    col = lax.broadcasted_iota(jnp.int32, (tm, n_sorted), 1).astype(F32)
    sel = jnp.zeros((tm, n_sorted), F32)
    for kk in range(TOP_K):
        sel = sel + (col == pos[:, kk:kk + 1]).astype(F32)
    sel = sel.astype(BF16)
    ys = ys_ref[...]
    hi = ys.astype(BF16)
    rest = ys - hi.astype(F32)
    mid = rest.astype(BF16)
    lo = (rest - mid.astype(F32)).astype(BF16)
    y2 = (jnp.dot(sel, hi, preferred_element_type=F32) + jnp.dot(sel, mid, preferred_element_type=F32)
          + jnp.dot(sel, lo, preferred_element_type=F32))
    o_ref[...] = _layer_norm(DEEPNORM_ALPHA * h1_ref[...] + g2_ref[0] * y2) * l2g_ref[...] + l2b_ref[...]


def _combine(ys, pos4, h1, mod3, mod_base, l2g, l2b, n_lat, seq):
    n_rows = h1.shape[0]
    tm = TM_POST
    tiles_per_seq = seq // tm
    n_lat_tiles = n_lat // tm
    n_groups_lat = n_lat // seq

    def group(i):
        return jnp.where(i < n_lat_tiles, i // tiles_per_seq, n_groups_lat)

    in_specs = [pl.BlockSpec((TOK_BLOCK, D_MODEL), lambda i: (i, 0)),
                pl.BlockSpec((tm, LANES), lambda i: (i, 0)),
                pl.BlockSpec((tm, D_MODEL), lambda i: (i, 0)),
                pl.BlockSpec((1, 1, D_MODEL), lambda i: (mod_base + group(i) * 6 + 5, 0, 0)),
                pl.BlockSpec((1, D_MODEL), lambda i: (0, 0)),
                pl.BlockSpec((1, D_MODEL), lambda i: (0, 0))]
    return pl.pallas_call(
        _combine_kernel,
        grid=(n_rows // tm,),
        in_specs=in_specs,
        out_specs=pl.BlockSpec((tm, D_MODEL), lambda i: (i, 0)),
        out_shape=jax.ShapeDtypeStruct((n_rows, D_MODEL), F32),
        compiler_params=_params(1),
        name="moe_combine",
    )(ys, pos4, h1, mod3, l2g.reshape(1, D_MODEL), l2b.reshape(1, D_MODEL))


def _block_diag(w):
    two, n, d, e = w.shape
    eye = jnp.eye(n, dtype=w.dtype)
    return (w[:, :, :, None, :] * eye[None, :, None, :, None]).reshape(two, n * d, n * e)


def kernel(x, c, ctx, c_ctx, w_mod, b_mod, w_in, b_in, na_rpb, w_proj_attn, w_proj_conv, w_proj_lru, sc_conv_w, lru_conv_w, lru_conv_b, lru_lambda, lru_w_r, lru_b_r, lru_w_i, lru_b_i, w_o, b_o, ln1_g, ln1_b, router_w, router_b, exp_w_gu, exp_b_gu, exp_w_dn, exp_b_dn, ln2_g, ln2_b):
    n_batch, seq, d = x.shape
    n_ctx = ctx.shape[1]
    n_lat = n_batch * seq
    n_all = n_lat + n_batch * n_ctx
    assert d == D_MODEL and n_batch + 1 <= SUBLANES

    cc = jnp.concatenate([c, c_ctx[None], jnp.zeros((SUBLANES - n_batch - 1, d), F32)], axis=0)
    mod = _modulation(cc, w_mod, b_mod)
    groups = n_batch + 1
    mod3 = mod.reshape(DEPTH, SUBLANES, 6, d)[:, :groups].reshape(DEPTH * groups * 6, 1, d)

    cos_t, sin_t = _make_rope(seq, TM_INPROJ)
    h = (x.reshape(n_lat, d), ctx.reshape(n_batch * n_ctx, d))

    for layer in range(DEPTH):
        last = layer == DEPTH - 1
        mod_base = layer * groups * 6
        w_in_bf = w_in[layer].astype(BF16)
        q, k, v, sb, rest = _input_projection(h, mod3, mod_base, w_in_bf, b_in[layer], N_EARLY,
                                              cos_t, sin_t, n_lat, seq)
        sp = jax.nn.softplus(-lru_lambda[layer])
        zb, hf, hb, att = _token_mixers(
            q, k, v, _attention_bias(na_rpb[layer], seq), sb, rest, sc_conv_w[layer], lru_conv_w[layer],
            lru_conv_b[layer], sp, _block_diag(lru_w_r[layer]).astype(BF16), _block_diag(lru_w_i[layer]).astype(BF16),
            lru_b_r[layer], lru_b_i[layer], n_batch, seq, n_ctx)
        n_rows = n_lat if last else n_all
        rw_full = jnp.pad(router_w[layer], ((0, 0), (0, LANES - N_EXPERTS)))
        rw_hi = rw_full.astype(BF16)
        rw_pad = jnp.concatenate([rw_hi, (rw_full - rw_hi.astype(F32)).astype(BF16)], axis=1)
        rb_pad = jnp.concatenate([router_b[layer], jnp.full((LANES - N_EXPERTS,), NEG_BIG, F32)]).reshape(1, LANES)
        h1, xs, pos4, cnt_out = _post_mixer(
            h, att, zb, hf, hb, rest, mod3, mod_base, w_in_bf, b_in[layer],
            w_proj_attn[layer].astype(BF16), w_proj_conv[layer].astype(BF16), w_proj_lru[layer].astype(BF16),
            w_o[layer].astype(BF16), b_o[layer], ln1_g[layer], ln1_b[layer], rw_pad, rb_pad, n_rows, n_lat, seq)
        plan = _expert_plan(cnt_out, n_rows // TM_POST, n_rows)
        ys = _experts(xs, plan, layer, exp_w_gu, exp_b_gu, exp_w_dn, exp_b_dn)
        h = (_combine(ys, pos4, h1, mod3, mod_base, ln2_g[layer], ln2_b[layer], n_lat, seq),)
    return h[0].reshape(n_batch, seq, d)
```

```python
import functools

import numpy as np
import jax
import jax.numpy as jnp
from jax import lax
from jax.experimental import pallas as pl
from jax.experimental.pallas import tpu as pltpu

D_MODEL = 1024
DEPTH = 2
GRID_W = 64
NA_HEADS = 8
NA_HEAD_DIM = 64
NA_WIDTH = NA_HEADS * NA_HEAD_DIM
NA_WIN_ROWS = 8
NA_WIN_COLS = 16
ROPE_BASE = 10000.0
BRANCH_WIDTH = 512
LRU_BLOCKS = 8
LRU_C = 8.0
P_TOTAL = 7168
N_EARLY = 8 * BRANCH_WIDTH
N_EXPERTS = 32
TOP_K = 4
SWIGLU_LIMIT = 7.0
SWIGLU_ALPHA = 1.702
LN_EPS = 1e-5
DEEPNORM_ALPHA = (2 * DEPTH) ** 0.25
NEG_BIG = -1e30

LANES = 128
SUBLANES = 8
VMEM_LIMIT_BYTES = 56 * 1024 * 1024

TM_INPROJ = 512
SCAN_CHUNK = 256
ATT_ROWS = 4
ATT_QROWS = ATT_ROWS * GRID_W
TM_POST = 256
TM_EXPERT = 512
XS_WIDTH = D_MODEL + LANES
TOK_BLOCK = TM_POST * TOP_K + N_EXPERTS * SUBLANES
ZERO_ROWS = 256

F32 = jnp.float32
BF16 = jnp.bfloat16


def _params(n_axes):
    return pltpu.CompilerParams(dimension_semantics=("arbitrary",) * n_axes,
                                vmem_limit_bytes=VMEM_LIMIT_BYTES)


def _layer_norm(x):
    mu = jnp.mean(x, axis=-1, keepdims=True)
    xc = x - mu
    var = jnp.mean(xc * xc, axis=-1, keepdims=True)
    return xc * lax.rsqrt(var + LN_EPS)


def _mod_kernel(c_ref, w_ref, b_ref, o_ref):
    c = c_ref[...]
    s = (c * jax.nn.sigmoid(c)).astype(BF16)
    o_ref[0] = jnp.dot(s, w_ref[0].astype(BF16), preferred_element_type=F32) + b_ref[0]


def _modulation(cc, w_mod, b_mod):
    n_out = w_mod.shape[-1]
    return pl.pallas_call(
        _mod_kernel,
        grid=(DEPTH, n_out // D_MODEL),
        in_specs=[pl.BlockSpec((SUBLANES, D_MODEL), lambda l, j: (0, 0)),
                  pl.BlockSpec((1, D_MODEL, D_MODEL), lambda l, j: (l, 0, j)),
                  pl.BlockSpec((1, 1, D_MODEL), lambda l, j: (l, 0, j))],
        out_specs=pl.BlockSpec((1, SUBLANES, D_MODEL), lambda l, j: (l, 0, j)),
        out_shape=jax.ShapeDtypeStruct((DEPTH, SUBLANES, n_out), F32),
        compiler_params=_params(2),
        name="modulation",
    )(cc, w_mod, b_mod.reshape(DEPTH, 1, n_out))


def _rope_half(x, cos, sin_signed):
    lane = lax.broadcasted_iota(jnp.int32, (x.shape[0], LANES), 1)
    first = (lane % 32) < 16
    outs = []
    for cidx in range(x.shape[1] // LANES):
        xc = x[:, cidx * LANES:(cidx + 1) * LANES]
        partner = jnp.where(first, pltpu.roll(xc, LANES - 16, 1), pltpu.roll(xc, 16, 1))
        outs.append(xc * cos + partner * sin_signed)
    return jnp.concatenate(outs, axis=1)


def _inproj_kernel(ha_ref, hb_ref, sh_ref, sc_ref, w_ref, b_ref, cos_ref, sin_ref,
                   q_ref, k_ref, v_ref, sb_ref, rest_ref, *, split):
    h = jnp.where(pl.program_id(0) < split, ha_ref[...], hb_ref[...])
    xn = (_layer_norm(h) * (1.0 + sc_ref[0]) + sh_ref[0]).astype(BF16)
    half = BRANCH_WIDTH

    def columns(lo, width):
        return jnp.dot(xn, w_ref[:, lo:lo + width], preferred_element_type=F32) + b_ref[:, lo:lo + width]

    cos = cos_ref[...]
    sin = sin_ref[...]
    q_ref[...] = _rope_half(columns(0, half), cos, sin).astype(BF16)
    k_ref[...] = _rope_half(columns(half, half), cos, sin).astype(BF16)
    v_ref[...] = columns(2 * half, half).astype(BF16)
    sb_ref[...] = columns(3 * half, half)
    n_rest = rest_ref.shape[1]
    for lo in range(0, n_rest, 2 * half):
        rest_ref[:, lo:lo + 2 * half] = columns(4 * half + lo, 2 * half)


def _input_projection(h_parts, mod3, mod_base, w_in_bf, b_in, n_cols, cos_t, sin_t, n_lat, seq):
    tm = TM_INPROJ
    ha, hb = h_parts[0], h_parts[-1]
    split = ha.shape[0] // tm
    m = ha.shape[0] + (hb.shape[0] if len(h_parts) == 2 else 0)
    n_lat_tiles = n_lat // tm
    tiles_per_seq = seq // tm
    n_groups_lat = n_lat // seq

    def group(i):
        return jnp.where(i < n_lat_tiles, i // tiles_per_seq, n_groups_lat)

    def rope_blk(i):
        return jnp.where(i < n_lat_tiles, i % tiles_per_seq, tiles_per_seq)

    half = BRANCH_WIDTH
    return pl.pallas_call(
        functools.partial(_inproj_kernel, split=split),
        grid=(m // tm,),
        in_specs=[pl.BlockSpec((tm, D_MODEL), lambda i: (jnp.minimum(i, split - 1), 0)),
                  pl.BlockSpec((tm, D_MODEL), lambda i: (jnp.maximum(i - split, 0), 0)),
                  pl.BlockSpec((1, 1, D_MODEL), lambda i: (mod_base + group(i) * 6 + 0, 0, 0)),
                  pl.BlockSpec((1, 1, D_MODEL), lambda i: (mod_base + group(i) * 6 + 1, 0, 0)),
                  pl.BlockSpec((D_MODEL, n_cols), lambda i: (0, 0)),
                  pl.BlockSpec((1, n_cols), lambda i: (0, 0)),
                  pl.BlockSpec((tm, LANES), lambda i: (rope_blk(i), 0)),
                  pl.BlockSpec((tm, LANES), lambda i: (rope_blk(i), 0))],
        out_specs=[pl.BlockSpec((tm, half), lambda i: (i, 0)),
                   pl.BlockSpec((tm, half), lambda i: (i, 0)),
                   pl.BlockSpec((tm, half), lambda i: (i, 0)),
                   pl.BlockSpec((tm, half), lambda i: (i, 0)),
                   pl.BlockSpec((tm, n_cols - 4 * half), lambda i: (i, 0))],
        out_shape=[jax.ShapeDtypeStruct((m, half), BF16),
                   jax.ShapeDtypeStruct((m, half), BF16),
                   jax.ShapeDtypeStruct((m, half), BF16),
                   jax.ShapeDtypeStruct((m, half), F32),
                   jax.ShapeDtypeStruct((m, n_cols - 4 * half), F32)],
        compiler_params=_params(1),
        name="input_projection",
    )(ha, hb, mod3, mod3, w_in_bf, b_in.reshape(1, -1), cos_t, sin_t)


def _make_rope(seq, tm):
    t = np.arange(seq)
    row_pos, col_pos = t // GRID_W, t % GRID_W
    d = np.arange(LANES) % NA_HEAD_DIM
    m = NA_HEAD_DIM // 4
    inv_freq = (ROPE_BASE ** (-jnp.arange(m, dtype=F32) / m))[d % m]
    pos = np.where((d < 2 * m)[None, :], row_pos[:, None], col_pos[:, None])
    ang = jnp.asarray(pos).astype(F32) * inv_freq[None, :]
    cos = jnp.cos(ang)
    sin = jnp.sin(ang)
    sin_signed = jnp.where(jnp.asarray((d % (2 * m)) < m)[None, :], -sin, sin)
    cos = jnp.concatenate([cos, jnp.ones((tm, LANES), F32)], axis=0)
    sin_signed = jnp.concatenate([sin_signed, jnp.zeros((tm, LANES), F32)], axis=0)
    return cos, sin_signed


def _mixer_kernel(fblk, bblk, first, last, seqb, r0t, cls,
                  sb_ref, scg_ref, sx_ref, lxf_ref,
                  scgp_ref, sxp_ref, lxfp_ref, scgn_ref, sxn_ref, lxfn_ref,
                  lxb_ref, lxbp_ref, lxbn_ref,
                  scw_ref, cw_ref, cb_ref, sp_ref, wr_ref, wi_ref, br_ref, bi_ref,
                  q_ref, k_ref, v_ref, kc_ref, vc_ref, bias0_ref, bias1_ref, bias2_ref, bias3_ref,
                  zb_ref, hf_ref, hb_ref, att_ref,
                  a_s, b_s, hc_s, st_s, *, ch, n_ctx_items, band):
    it = pl.program_id(0)
    is_first = first[it] == 1
    is_last = last[it] == 1
    is_ctx = it < n_ctx_items
    b = seqb[it]
    width = BRANCH_WIDTH
    row = lax.broadcasted_iota(jnp.int32, (ch, width), 0)
    not_first = jnp.where(is_first, 0.0, 1.0).astype(F32)
    not_last = jnp.where(is_last, 0.0, 1.0).astype(F32)

    def back1(u, prev_row):
        return jnp.where(row == 0, prev_row, pltpu.roll(u, 1, 0))

    def back2(u, prev2, prev1):
        return jnp.where(row == 0, prev2, jnp.where(row == 1, prev1, pltpu.roll(u, 2, 0)))

    def fwd1(u, next_row):
        return jnp.where(row == ch - 1, next_row, pltpu.roll(u, ch - 1, 0))

    def lru_input(lx_ref, lxp_ref, lxn_ref, prev_ok, next_ok):
        x = lx_ref[...]
        p = lxp_ref[...] * prev_ok
        n = lxn_ref[...] * next_ok
        return (cw_ref[0:1] * back2(x, p[6:7], p[7:8]) + cw_ref[1:2] * back1(x, p[7:8])
                + cw_ref[2:3] * x + cw_ref[3:4] * fwd1(x, n[0:1]) + cb_ref[...])

    def coeffs(d, xm):
        xb = xm.astype(BF16)
        r = jax.nn.sigmoid(jnp.dot(xb, wr_ref[d], preferred_element_type=F32) + br_ref[d:d + 1])
        g = jax.nn.sigmoid(jnp.dot(xb, wi_ref[d], preferred_element_type=F32) + bi_ref[d:d + 1])
        log_a = (-LRU_C * sp_ref[d:d + 1]) * r
        a = jnp.exp(log_a)
        a_s[d] = a
        b_s[d] = jnp.sqrt(-jnp.tanh(log_a) * (a * a + 1.0)) * (g * xm)

    @pl.when(jnp.logical_and(is_first, is_ctx))
    def _():
        hc_s[...] = jnp.zeros_like(hc_s)

    @pl.when(jnp.logical_and(is_first, jnp.logical_not(is_ctx)))
    def _():
        hc_s[0:1] = st_s[pl.ds(2 * b, 1), :]
        hc_s[1:2] = st_s[pl.ds(2 * b + 1, 1), :]

    coeffs(0, lru_input(lxf_ref, lxfp_ref, lxfn_ref, not_first, not_last))
    hf = hc_s[0:1]
    for t in range(ch):
        hf = a_s[0, t:t + 1, :] * hf + b_s[0, t:t + 1, :]
        hf_ref[t:t + 1, :] = hf
    hc_s[0:1] = hf

    coeffs(1, lru_input(lxb_ref, lxbp_ref, lxbn_ref, not_last, not_first))

    u = scg_ref[...] * sx_ref[...]
    u_prev = scgp_ref[7:8] * sxp_ref[7:8] * not_first
    u_next = scgn_ref[0:1] * sxn_ref[0:1] * not_last
    conv = scw_ref[0:1] * back1(u, u_prev) + scw_ref[1:2] * u + scw_ref[2:3] * fwd1(u, u_next)
    zb_ref[...] = (sb_ref[...] * conv).astype(BF16)

    hb = hc_s[1:2]
    for t in range(ch - 1, -1, -1):
        hb = a_s[1, t:t + 1, :] * hb + b_s[1, t:t + 1, :]
        hb_ref[t:t + 1, :] = hb
    hc_s[1:2] = hb

    _attention_item(it, r0t, cls, q_ref, k_ref, v_ref, kc_ref, vc_ref,
                    (bias0_ref, bias1_ref, bias2_ref, bias3_ref), att_ref, band)

    @pl.when(is_ctx)
    def _():
        st_s[pl.ds(2 * b, 1), :] = hf
        st_s[pl.ds(2 * b + 1, 1), :] = hb


def _mixer_tables(n_batch, seq, n_ctx, ch):
    assert n_ctx == ch and ch == ATT_QROWS
    nc = seq // ch
    rows = seq // GRID_W
    kr = min(NA_WIN_ROWS, rows)
    ctx0 = n_batch * seq // ch
    fblk, bblk, first, last, seqb, r0t, cls = [], [], [], [], [], [], []
    for b in range(n_batch):
        fblk.append(ctx0 + b); bblk.append(ctx0 + b); first.append(1); last.append(1); seqb.append(b)
        r0t.extend([0] * ATT_ROWS); cls.extend([kr] * ATT_ROWS)
    for b in range(n_batch):
        for c in range(nc):
            fblk.append(b * nc + c); bblk.append(b * nc + nc - 1 - c)
            first.append(int(c == 0)); last.append(int(c == nc - 1)); seqb.append(b)
            for r in range(c * ATT_ROWS, (c + 1) * ATT_ROWS):
                r0 = min(max(r - kr // 2, 0), rows - kr)
                r0t.append(r0); cls.append(r - r0)
    return [np.asarray(a, np.int32) for a in (fblk, bblk, first, last, seqb, r0t, cls)]


def _token_mixers(q, k, v, bias, sb, rest, sc_w, lru_cw, lru_cb, sp, wr_bd, wi_bd, b_r, b_i, n_batch, seq, n_ctx):
    m = sb.shape[0]
    ch = SCAN_CHUNK
    width = BRANCH_WIDTH
    tables = _mixer_tables(n_batch, seq, n_ctx, ch)
    n_items = len(tables[0])
    halo_per_chunk = ch // SUBLANES
    last_halo = m // SUBLANES - 1
    band = min(NA_WIN_ROWS, seq // GRID_W) * GRID_W
    ctx_blk0 = n_batch * seq // n_ctx

    def cur(col, which):
        return pl.BlockSpec((ch, width), lambda i, f, bk, *_: ((f, bk)[which][i], col))

    def prev(col, which):
        return pl.BlockSpec((SUBLANES, width),
                            lambda i, f, bk, *_: (jnp.maximum((f, bk)[which][i] * halo_per_chunk - 1, 0), col))

    def nxt(col, which):
        return pl.BlockSpec((SUBLANES, width),
                            lambda i, f, bk, *_: (jnp.minimum(((f, bk)[which][i] + 1) * halo_per_chunk, last_halo), col))

    def full(shape):
        return pl.BlockSpec(shape, lambda i, *_: (0,) * len(shape))

    in_specs = [cur(0, 0), cur(0, 0), cur(1, 0), cur(2, 0),
                prev(0, 0), prev(1, 0), prev(2, 0), nxt(0, 0), nxt(1, 0), nxt(2, 0),
                cur(2, 1), prev(2, 1), nxt(2, 1),
                full(sc_w.shape), full(lru_cw.shape), full((1, width)), full(sp.shape),
                full(wr_bd.shape), full(wi_bd.shape), full(b_r.shape), full(b_i.shape)]
    in_specs += [cur(0, 0),
                 pl.BlockSpec((seq, width), lambda i, f, bk, fi, la, sq, *_: (sq[i], 0)),
                 pl.BlockSpec((seq, width), lambda i, f, bk, fi, la, sq, *_: (sq[i], 0)),
                 pl.BlockSpec((n_ctx, width), lambda i, f, bk, fi, la, sq, *_: (ctx_blk0 + sq[i], 0)),
                 pl.BlockSpec((n_ctx, width), lambda i, f, bk, fi, la, sq, *_: (ctx_blk0 + sq[i], 0))]
    for row in range(ATT_ROWS):
        in_specs.append(pl.BlockSpec((1, NA_HEADS, GRID_W, band),
                                     lambda i, f, bk, fi, la, sq, r0, cl, row=row: (cl[i * ATT_ROWS + row], 0, 0, 0)))
    out_specs = [cur(0, 0), cur(0, 0), cur(0, 1), cur(0, 0)]
    grid_spec = pltpu.PrefetchScalarGridSpec(
        num_scalar_prefetch=7, grid=(n_items,), in_specs=in_specs, out_specs=out_specs,
        scratch_shapes=[pltpu.VMEM((2, ch, width), F32), pltpu.VMEM((2, ch, width), F32),
                        pltpu.VMEM((SUBLANES, width), F32), pltpu.VMEM((2 * n_batch, width), F32)])
    return pl.pallas_call(
        functools.partial(_mixer_kernel, ch=ch, n_ctx_items=n_batch, band=band),
        grid_spec=grid_spec,
        out_shape=[jax.ShapeDtypeStruct((m, width), BF16),
                   jax.ShapeDtypeStruct((m, width), F32),
                   jax.ShapeDtypeStruct((m, width), F32),
                   jax.ShapeDtypeStruct((m, width), BF16)],
        compiler_params=_params(1),
        name="token_mixers",
    )(*[jnp.asarray(t) for t in tables],
      sb, rest, rest, rest, rest, rest, rest, rest, rest, rest, rest, rest, rest,
      sc_w, lru_cw, lru_cb.reshape(1, width), sp, wr_bd, wi_bd, b_r, b_i,
      q, k, v, k, v, bias, bias, bias, bias)


def _attention_item(it, r0t, cls, q_ref, k_ref, v_ref, kc_ref, vc_ref, bias_refs, o_ref, band):
    nq = GRID_W
    lane = lax.broadcasted_iota(jnp.int32, (nq, LANES), 1)
    low = lane < NA_HEAD_DIM
    scale = NA_HEAD_DIM ** -0.5
    nt = (((1,), (1,)), ((), ()))
    for row in range(ATT_ROWS):
        start = pl.multiple_of(r0t[it * ATT_ROWS + row] * GRID_W, GRID_W)
        bias_ref = bias_refs[row]
        qrows = slice(row * nq, (row + 1) * nq)
        for hp in range(NA_HEADS // 2):
            cols = slice(hp * LANES, (hp + 1) * LANES)
            qp = q_ref[qrows, cols].astype(F32) * scale
            qs = jnp.concatenate([jnp.where(low, qp, 0.0), jnp.where(low, 0.0, qp)], axis=0).astype(BF16)
            kb = k_ref[pl.ds(start, band), cols]
            vb = v_ref[pl.ds(start, band), cols]
            s_loc = lax.dot_general(qs, kb, nt, preferred_element_type=F32)
            s_ctx = lax.dot_general(qs, kc_ref[:, cols], nt, preferred_element_type=F32)
            bias = jnp.concatenate([bias_ref[0, 2 * hp], bias_ref[0, 2 * hp + 1]], axis=0)
            s_loc = s_loc + bias
            mx =jnp.maximum(jnp.max(s_loc, axis=-1, keepdims=True), jnp.max(s_ctx, axis=-1, keepdims=True))
            e_loc = jnp.exp(s_loc - mx)
            e_ctx = jnp.exp(s_ctx - mx)
            den = jnp.sum(e_loc, axis=-1, keepdims=True) + jnp.sum(e_ctx, axis=-1, keepdims=True)
            o = (jnp.dot(e_loc.astype(BF16), vb, preferred_element_type=F32)
                 + jnp.dot(e_ctx.astype(BF16), vc_ref[:, cols], preferred_element_type=F32)) / den
            o_ref[qrows, cols] = jnp.where(low, o[:nq], o[nq:]).astype(BF16)


def _attention_bias(rpb, seq):
    rows = seq // GRID_W
    kr = min(NA_WIN_ROWS, rows)
    kc = NA_WIN_COLS
    cq = np.arange(GRID_W)
    c0 = np.clip(cq - kc // 2, 0, GRID_W - kc)
    ck = np.arange(GRID_W)
    inside = (ck[None, :] >= c0[:, None]) & (ck[None, :] < c0[:, None] + kc)
    dc = np.clip(ck[None, :] - cq[:, None] + (NA_WIN_COLS - 1), 0, 2 * NA_WIN_COLS - 2)
    n_dr = 2 * NA_WIN_ROWS - 1
    n_dc = 2 * NA_WIN_COLS - 1
    pick = jnp.asarray((np.arange(n_dc)[:, None] == dc.reshape(1, -1)).astype(np.float32))
    picked = jnp.dot(rpb.reshape(-1, n_dc), pick, precision=lax.Precision.HIGHEST)
    picked = picked.reshape(NA_HEADS, 2 * NA_WIN_ROWS - 1, GRID_W, GRID_W)
    table = jnp.where(jnp.asarray(inside)[None, None], picked, NEG_BIG)
    table = table.transpose(0, 2, 1, 3).reshape(NA_HEADS, GRID_W, n_dr * GRID_W)
    classes = []
    for cl in range(kr):
        lo = (NA_WIN_ROWS - 1 - cl) * GRID_W
        classes.append(table[:, :, lo:lo + kr * GRID_W])
    classes.append(jnp.full((NA_HEADS, GRID_W, kr * GRID_W), NEG_BIG, F32))
    return jnp.stack(classes, axis=0)


def _post_kernel(ha_ref, hc_ref, att_ref, zb_ref, hf_ref, hb_ref, lg_ref,
                 sh1_ref, sc1_ref, g1_ref, sh2_ref, sc2_ref,
                 wgl0_ref, wgl1_ref, wgl2_ref, bgl0_ref, bgl1_ref, bgl2_ref,
                 wpa_ref, wpc_ref, wpl_ref, wo_ref, bo_ref, l1g_ref, l1b_ref,
                 rw_ref, rb_ref, tri_ref, upper_ref,
                 h1_ref, xs_ref, pos_ref, cnt_ref, *, split):
    h = jnp.where(pl.program_id(0) < split, ha_ref[...], hc_ref[...])
    u1 = (_layer_norm(h) * (1.0 + sc1_ref[0]) + sh1_ref[0]).astype(BF16)
    y_a = jnp.dot(att_ref[...], wpa_ref[...], preferred_element_type=F32)
    y_b = jnp.dot(zb_ref[...], wpc_ref[...], preferred_element_type=F32)
    zc = jax.nn.gelu(lg_ref[...]) * (hf_ref[...] + hb_ref[...])
    y_c = jnp.dot(zc.astype(BF16), wpl_ref[...], preferred_element_type=F32)
    merged = (jax.nn.sigmoid(jnp.dot(u1, wgl0_ref[...], preferred_element_type=F32) + bgl0_ref[...]) * y_a
              + jax.nn.sigmoid(jnp.dot(u1, wgl1_ref[...], preferred_element_type=F32) + bgl1_ref[...]) * y_b
              + jax.nn.sigmoid(jnp.dot(u1, wgl2_ref[...], preferred_element_type=F32) + bgl2_ref[...]) * y_c)
    y = jnp.dot(merged.astype(BF16), wo_ref[...], preferred_element_type=F32) + bo_ref[...]
    h1 = _layer_norm(DEEPNORM_ALPHA * h + g1_ref[0] * y) * l1g_ref[...] + l1b_ref[...]
    h1_ref[...] = h1
    u2 = _layer_norm(h1) * (1.0 + sc2_ref[0]) + sh2_ref[0]

    u_hi = u2.astype(BF16)
    u_lo = (u2 - u_hi.astype(F32)).astype(BF16)
    by_hi = jnp.dot(u_hi, rw_ref[...], preferred_element_type=F32)
    logits = (by_hi[:, :LANES] + by_hi[:, LANES:]
              + jnp.dot(u_lo, rw_ref[:, :LANES], preferred_element_type=F32) + rb_ref[...])
    tm = logits.shape[0]
    lane = lax.broadcasted_iota(jnp.int32, (tm, LANES), 1)
    lane_f = lane.astype(F32)
    work = logits
    tops, idxs, hots = [], [], []
    for _ in range(TOP_K):
        mx = jnp.max(work, axis=-1, keepdims=True)
        idx = jnp.min(jnp.where(work == mx, lane_f, float(LANES)), axis=-1, keepdims=True)
        hot = lane_f == idx
        work = jnp.where(hot, -3e38, work)
        tops.append(mx); idxs.append(idx); hots.append(hot)
    exps = [jnp.exp(t - tops[0]) for t in tops]
    den = exps[0] + exps[1] + exps[2] + exps[3]
    hot_all = jnp.zeros((tm, LANES), F32)
    for hot in hots:
        hot_all = hot_all + hot.astype(F32)
    cnt = jnp.sum(hot_all, axis=0, keepdims=True)
    cnt_pad = jnp.floor((cnt + (SUBLANES - 1.0)) * (1.0 / SUBLANES)) * SUBLANES
    off = jnp.dot(jnp.broadcast_to(cnt_pad, (SUBLANES, LANES)), upper_ref[...],
                  precision=lax.Precision.HIGHEST, preferred_element_type=F32)[0:1]
    slot = off + jnp.dot(tri_ref[...], hot_all.astype(BF16), preferred_element_type=F32)
    pos4 = jnp.zeros((tm, LANES), F32)
    w_tile = jnp.zeros((tm, LANES), F32)
    for kk in range(TOP_K):
        pos_k = jnp.sum(jnp.where(hots[kk], slot, 0.0), axis=-1, keepdims=True)
        pos4 = jnp.where(lane == kk, pos_k, pos4)
        p = exps[kk] / den
        p_hi = p.astype(BF16).astype(F32)
        p_mid = (p - p_hi).astype(BF16).astype(F32)
        p_lo = p - p_hi - p_mid
        w_tile = jnp.where(hots[kk], p_hi, w_tile)
        w_tile = jnp.where(lane_f == idxs[kk] + float(N_EXPERTS), p_mid, w_tile)
        w_tile = jnp.where(lane_f == idxs[kk] + float(2 * N_EXPERTS), p_lo, w_tile)
    pos_ref[...] = pos4
    sub = lax.broadcasted_iota(jnp.int32, (SUBLANES, LANES), 0)
    cnt_ref[...] = jnp.where(sub == 0, cnt, jnp.where(sub == 1, off, 0.0))

    n_sorted = xs_ref.shape[0]
    pos_t = pos4.T
    r_iota = lax.broadcasted_iota(jnp.int32, (n_sorted, tm), 0).astype(F32)
    hit = r_iota == pos_t[0:1, :]
    for kk in range(1, TOP_K):
        hit = jnp.logical_or(hit, r_iota == pos_t[kk:kk + 1, :])
    perm = jnp.where(hit, 1.0, 0.0).astype(BF16)
    feats = jnp.concatenate([u_hi, w_tile.astype(BF16)], axis=1)
    xs_ref[...] = jnp.dot(perm, feats, preferred_element_type=F32)


def _post_mixer(h_parts, att, zb, hf, hb, rest, mod3, mod_base, w_in_bf, b_in, wpa, wpc, wpl, wo, b_o, l1g, l1b,
                rw_pad, rb_pad, n_rows, n_lat, seq):
    tm = TM_POST
    width = BRANCH_WIDTH
    ha, hc = h_parts[0], h_parts[-1]
    split = min(ha.shape[0], n_rows) // tm
    gate_col0 = N_EARLY // D_MODEL
    tiles_per_seq = seq // tm
    n_lat_tiles = n_lat // tm
    n_groups_lat = n_lat // seq
    tri = jnp.asarray(np.tril(np.ones((tm, tm), np.float32), -1), BF16)
    upper = jnp.asarray(np.triu(np.ones((LANES, LANES), np.float32), 1))
    n_tiles = n_rows // tm

    def group(i):
        return jnp.where(i < n_lat_tiles, i // tiles_per_seq, n_groups_lat)

    def rows(wd, col=0):
        return pl.BlockSpec((tm, wd), lambda i: (i, col))

    def full(shape):
        return pl.BlockSpec(shape, lambda i: (0,) * len(shape))

    def mod(which):
        return pl.BlockSpec((1, 1, D_MODEL), lambda i: (mod_base + group(i) * 6 + which, 0, 0))

    in_specs = [pl.BlockSpec((tm, D_MODEL), lambda i: (jnp.minimum(i, split - 1), 0)),
                pl.BlockSpec((tm, D_MODEL), lambda i: (jnp.maximum(i - split, 0), 0)),
                rows(width), rows(width), rows(width), rows(width), rows(width, 3),
                mod(0), mod(1), mod(2), mod(3), mod(4)]
    in_specs += [pl.BlockSpec((D_MODEL, D_MODEL), lambda i, c=c: (0, gate_col0 + c)) for c in range(3)]
    in_specs += [pl.BlockSpec((1, D_MODEL), lambda i, c=c: (0, gate_col0 + c)) for c in range(3)]
    in_specs += [full(wpa.shape), full(wpc.shape), full(wpl.shape), full(wo.shape), full((1, D_MODEL)),
                full((1, D_MODEL)), full((1, D_MODEL)), full(rw_pad.shape), full(rb_pad.shape), full(tri.shape),
                full(upper.shape)]
    out_specs = [rows(D_MODEL), pl.BlockSpec((TOK_BLOCK, XS_WIDTH), lambda i: (i, 0)), rows(LANES),
                 pl.BlockSpec((SUBLANES, LANES), lambda i: (i, 0))]
    return pl.pallas_call(
        functools.partial(_post_kernel, split=split),
        grid=(n_tiles,),
        in_specs=in_specs, out_specs=out_specs,
        out_shape=[jax.ShapeDtypeStruct((n_rows, D_MODEL), F32),
                   jax.ShapeDtypeStruct((n_tiles * TOK_BLOCK, XS_WIDTH), F32),
                   jax.ShapeDtypeStruct((n_rows, LANES), F32),
                   jax.ShapeDtypeStruct((n_tiles * SUBLANES, LANES), F32)],
        compiler_params=_params(1),
        name="post_mixer",
    )(ha, hc, att, zb, hf, hb, rest, mod3, mod3, mod3, mod3, mod3,
      w_in_bf, w_in_bf, w_in_bf, b_in.reshape(1, -1), b_in.reshape(1, -1), b_in.reshape(1, -1),
      wpa, wpc, wpl, wo, b_o.reshape(1, D_MODEL), l1g.reshape(1, D_MODEL), l1b.reshape(1, D_MODEL),
      rw_pad, rb_pad, tri, upper)


def _expert_plan(cnt_out, n_tok_tiles, n_rows):
    tm = TM_EXPERT
    co = cnt_out.reshape(n_tok_tiles, SUBLANES, LANES)
    cnt = (co[:, 0, :N_EXPERTS].astype(jnp.int32) + SUBLANES - 1) // SUBLANES * SUBLANES
    off = co[:, 1, :N_EXPERTS].astype(jnp.int32)
    cum_end = jnp.cumsum(cnt, axis=0)
    cum = cum_end - cnt
    total = cum_end[-1]
    n_et = (total + tm - 1) // tm
    et_end = jnp.cumsum(n_et)
    n_act = et_end[-1:].astype(jnp.int32)
    n_tiles = -(-n_tok_tiles * TOK_BLOCK // tm) + N_EXPERTS
    j = jnp.arange(n_tiles, dtype=jnp.int32)
    tile_e = jnp.minimum(jnp.sum((et_end[None, :] <= j[:, None]).astype(jnp.int32), axis=1), N_EXPERTS - 1)
    pick_e = (tile_e[:, None] == jnp.arange(N_EXPERTS, dtype=jnp.int32)[None, :]).astype(F32)

    def per_tile(table):
        return jnp.dot(pick_e, table.astype(F32), precision=lax.Precision.HIGHEST).astype(jnp.int32)

    row0 = (j - per_tile(et_end - n_et)) * tm
    n_rows_tile = jnp.clip(per_tile(total) - row0, 0, tm)
    cum_e = per_tile(cum.T)
    cum_end_e = per_tile(cum_end.T)
    delta_e = per_tile((jnp.arange(n_tok_tiles, dtype=jnp.int32)[:, None] * TOK_BLOCK + off - cum).T)
    q = row0[:, None] + SUBLANES * jnp.arange(tm // SUBLANES, dtype=jnp.int32)[None, :]
    inside = jnp.logical_and(cum_e.T[:, :, None] <= q[None], q[None] < cum_end_e.T[:, :, None])
    src = q + jnp.sum(jnp.where(inside, delta_e.T[:, :, None], 0), axis=0)
    used = off[:, N_EXPERTS - 1] + cnt[:, N_EXPERTS - 1]
    first = jnp.concatenate([jnp.ones((1,), jnp.int32), (tile_e[1:] != tile_e[:-1]).astype(jnp.int32)])
    group = jnp.cumsum(first) - 1
    after = per_tile(et_end)
    next_e = jnp.where(after < n_act[0],
                       jnp.minimum(jnp.sum((et_end[None, :] <= after[:, None]).astype(jnp.int32), axis=1),
                                   N_EXPERTS - 1), -1)
    return (tile_e.astype(jnp.int32), n_rows_tile.astype(jnp.int32), n_act, src.reshape(-1).astype(jnp.int32),
            used.astype(jnp.int32), first, group.astype(jnp.int32), next_e.astype(jnp.int32), n_tiles)


def _expert_kernel(tile_e, n_rows_t, n_act, src_t, used_t, first_t, group_t, next_t,
                   xs_hbm, wgu_hbm, bgu_ref, wdn_hbm, bdn_ref, ys_hbm,
                   xin, yout, wgu_f, wdn_f, wgu_s, wdn_s, zeros, sem_in, sem_out, sem_zero, sem_wgu, sem_wdn,
                   *, tm, tok_block, n_tok_tiles, layer):
    j = pl.program_id(0)
    na = n_act[0]
    chunks = tm // SUBLANES

    def weight_copies(e, wslot):
        return (pltpu.make_async_copy(wgu_hbm.at[layer, e], wgu_f.at[wslot], sem_wgu.at[wslot]),
                pltpu.make_async_copy(wdn_hbm.at[layer, e], wdn_f.at[wslot], sem_wdn.at[wslot]))

    def gather(src, dst, size, slot):
        return pltpu.make_async_copy(xs_hbm.at[pl.ds(src, size)], xin.at[slot, pl.ds(dst, size)], sem_in.at[slot])

    def scatter(src, dst, size, slot):
        return pltpu.make_async_copy(yout.at[slot, pl.ds(dst, size)], ys_hbm.at[pl.ds(src, size)], sem_out.at[slot])

    def start_chunks(jj, slot, copy):
        def one(c, priority):
            src = pl.multiple_of(src_t[jj * chunks + c], SUBLANES)
            copy(src, pl.multiple_of(c * SUBLANES, SUBLANES), SUBLANES, slot).start(priority=priority)

        def body(c, carry):
            one(c, 0)
            return carry

        def body_unrolled(g, carry):
            for u in range(SUBLANES):
                one(g * SUBLANES + u, u % 2)
            return carry

        n = lax.shift_right_logical(n_rows_t[jj], 3)

        @pl.when(n == chunks)
        def _():
            lax.fori_loop(0, chunks // SUBLANES, body_unrolled, 0)

        @pl.when(n != chunks)
        def _():
            lax.fori_loop(0, n, body, 0)

    def wait_rows(jj, slot, copy):
        n = n_rows_t[jj]
        size = tm
        while size >= SUBLANES:
            @pl.when((n & size) != 0)
            def _(size=size):
                copy(0, 0, size, slot).wait()
            size //= 2

    slot = j % 2

    @pl.when(j == 0)
    def _():
        xin[...] = jnp.zeros_like(xin)
        start_chunks(0, 0, gather)
        zeros[...] = jnp.zeros_like(zeros)

        def clear_tail(i, copy_op):
            used = used_t[i]
            tail = tok_block - used
            for size in (256, 128, 64, 32, 16, 8):
                @pl.when((tail & size) != 0)
                def _():
                    at = pl.multiple_of(i * tok_block + used + (tail & ~(2 * size - 1)), SUBLANES)
                    copy_op(pltpu.make_async_copy(zeros.at[pl.ds(0, size)], ys_hbm.at[pl.ds(at, size)], sem_zero))

        def start_clear(i, c):
            clear_tail(i, lambda cp: cp.start())
            return c

        def wait_clear(i, c):
            clear_tail(i, lambda cp: cp.wait())
            return c

        lax.fori_loop(0, n_tok_tiles, start_clear, 0)
        lax.fori_loop(0, n_tok_tiles, wait_clear, 0)

    @pl.when(j + 1 < na)
    def _():
        start_chunks(j + 1, 1 - slot, gather)

    @pl.when(j < na)
    def _():
        e = tile_e[j]

        @pl.when(first_t[j] == 1)
        def _():
            wslot = group_t[j] % 2

            @pl.when(j == 0)
            def _():
                for cp in weight_copies(e, wslot):
                    cp.start()

            for cp in weight_copies(e, wslot):
                cp.wait()
            wgu_s[...] = wgu_f[wslot].astype(BF16)
            wdn_s[...] = wdn_f[wslot].astype(BF16)

            @pl.when(next_t[j] >= 0)
            def _():
                for cp in weight_copies(next_t[j], 1 - wslot):
                    cp.start()

        wait_rows(j, slot, gather)

        def compute(rows):
            x = xin[slot, pl.ds(0, rows)]
            lane = lax.broadcasted_iota(jnp.int32, (rows, LANES), 1)
            p = jnp.sum(jnp.where(lane % N_EXPERTS == e, x[:, D_MODEL:], 0.0), axis=-1, keepdims=True)
            gu = (jnp.dot(x[:, :D_MODEL].astype(BF16), wgu_s[...], preferred_element_type=F32)
                  + bgu_ref[0, pl.ds(e, 1), :])
            f = gu.shape[1] // 2
            gate = jnp.minimum(gu[:, :f], SWIGLU_LIMIT)
            up = jnp.clip(gu[:, f:], -SWIGLU_LIMIT, SWIGLU_LIMIT)
            hid = (up + 1.0) * gate * jax.nn.sigmoid(SWIGLU_ALPHA * gate)
            y = (jnp.dot(hid.astype(BF16), wdn_s[...], preferred_element_type=F32)
                 + bdn_ref[0, pl.ds(e, 1), :])
            yout[slot, pl.ds(0, rows)] = y * p

        @pl.when(n_rows_t[j] > tm // 2)
        def _():
            compute(tm)

        @pl.when(n_rows_t[j] <= tm // 2)
        def _():
            compute(tm // 2)

        start_chunks(j, slot, scatter)

        @pl.when(j >= 1)
        def _():
            wait_rows(j - 1, 1 - slot, scatter)

        @pl.when(j == na - 1)
        def _():
            wait_rows(j, slot, scatter)


def _experts(xs, plan, layer, w_gu, b_gu, w_dn, b_dn):
    tile_e, n_rows_tile, n_act, src, used, first, group, next_e, n_tiles = plan
    tm = TM_EXPERT
    f2 = w_gu.shape[-1]

    in_specs = [pl.BlockSpec(memory_space=pl.ANY),
                pl.BlockSpec(memory_space=pl.ANY),
                pl.BlockSpec((1, N_EXPERTS, f2), lambda j, *_: (layer, 0, 0)),
                pl.BlockSpec(memory_space=pl.ANY),
                pl.BlockSpec((1, N_EXPERTS, D_MODEL), lambda j, *_: (layer, 0, 0))]
    grid_spec = pltpu.PrefetchScalarGridSpec(
        num_scalar_prefetch=8, grid=(n_tiles,), in_specs=in_specs,
        out_specs=pl.BlockSpec(memory_space=pl.ANY),
        scratch_shapes=[pltpu.VMEM((2, tm, XS_WIDTH), F32), pltpu.VMEM((2, tm, D_MODEL), F32),
                        pltpu.VMEM((2, D_MODEL, f2), F32), pltpu.VMEM((2, f2 // 2, D_MODEL), F32),
                        pltpu.VMEM((D_MODEL, f2), BF16), pltpu.VMEM((f2 // 2, D_MODEL), BF16),
                        pltpu.VMEM((ZERO_ROWS, D_MODEL), F32),
                        pltpu.SemaphoreType.DMA((2,)), pltpu.SemaphoreType.DMA((2,)), pltpu.SemaphoreType.DMA(()),
                        pltpu.SemaphoreType.DMA((2,)), pltpu.SemaphoreType.DMA((2,))])
    return pl.pallas_call(
        functools.partial(_expert_kernel, tm=tm, tok_block=TOK_BLOCK, n_tok_tiles=xs.shape[0] // TOK_BLOCK,
                          layer=layer),
        grid_spec=grid_spec,
        out_shape=jax.ShapeDtypeStruct((xs.shape[0], D_MODEL), F32),
        compiler_params=_params(1),
        name="moe_experts",
    )(tile_e, n_rows_tile, n_act, src, used, first, group, next_e,
      xs, w_gu, b_gu, w_dn, b_dn)


def _combine_kernel(ys_ref, pos_ref, h1_ref, g2_ref, l2g_ref, l2b_ref, o_ref):
    tm = pos_ref.shape[0]
    n_sorted = ys_ref.shape[0]
    pos = pos_ref[...]
    col = lax.broadcasted_iota(jnp.int32, (tm, n_sorted), 1).astype(F32)
    sel = jnp.zeros((tm, n_sorted), F32)
    for kk in range(TOP_K):
        sel = sel + (col == pos[:, kk:kk + 1]).astype(F32)
    sel = sel.astype(BF16)
    ys = ys_ref[...]
    hi = ys.astype(BF16)
    rest = ys - hi.astype(F32)
    mid = rest.astype(BF16)
    lo = (rest - mid.astype(F32)).astype(BF16)
    y2 = (jnp.dot(sel, hi, preferred_element_type=F32) + jnp.dot(sel, mid, preferred_element_type=F32)
          + jnp.dot(sel, lo, preferred_element_type=F32))
    o_ref[...] = _layer_norm(DEEPNORM_ALPHA * h1_ref[...] + g2_ref[0] * y2) * l2g_ref[...] + l2b_ref[...]


def _combine(ys, pos4, h1, mod3, mod_base, l2g, l2b, n_lat, seq):
    n_rows = h1.shape[0]
    tm = TM_POST
    tiles_per_seq = seq // tm
    n_lat_tiles = n_lat // tm
    n_groups_lat = n_lat // seq

    def group(i):
        return jnp.where(i < n_lat_tiles, i // tiles_per_seq, n_groups_lat)

    in_specs = [pl.BlockSpec((TOK_BLOCK, D_MODEL), lambda i: (i, 0)),
                pl.BlockSpec((tm, LANES), lambda i: (i, 0)),
                pl.BlockSpec((tm, D_MODEL), lambda i: (i, 0)),
                pl.BlockSpec((1, 1, D_MODEL), lambda i: (mod_base + group(i) * 6 + 5, 0, 0)),
                pl.BlockSpec((1, D_MODEL), lambda i: (0, 0)),
                pl.BlockSpec((1, D_MODEL), lambda i: (0, 0))]
    return pl.pallas_call(
        _combine_kernel,
        grid=(n_rows // tm,),
        in_specs=in_specs,
        out_specs=pl.BlockSpec((tm, D_MODEL), lambda i: (i, 0)),
        out_shape=jax.ShapeDtypeStruct((n_rows, D_MODEL), F32),
        compiler_params=_params(1),
        name="moe_combine",
    )(ys, pos4, h1, mod3, l2g.reshape(1, D_MODEL), l2b.reshape(1, D_MODEL))


def _block_diag(w):
    two, n, d, e = w.shape
    eye = jnp.eye(n, dtype=w.dtype)
    return (w[:, :, :, None, :] * eye[None, :, None, :, None]).reshape(two, n * d, n * e)


def kernel(x, c, ctx, c_ctx, w_mod, b_mod, w_in, b_in, na_rpb, w_proj_attn, w_proj_conv, w_proj_lru, sc_conv_w, lru_conv_w, lru_conv_b, lru_lambda, lru_w_r, lru_b_r, lru_w_i, lru_b_i, w_o, b_o, ln1_g, ln1_b, router_w, router_b, exp_w_gu, exp_b_gu, exp_w_dn, exp_b_dn, ln2_g, ln2_b):
    n_batch, seq, d = x.shape
    n_ctx = ctx.shape[1]
    n_lat = n_batch * seq
    n_all = n_lat + n_batch * n_ctx
    assert d == D_MODEL and n_batch + 1 <= SUBLANES

    cc = jnp.concatenate([c, c_ctx[None], jnp.zeros((SUBLANES - n_batch - 1, d), F32)], axis=0)
    mod = _modulation(cc, w_mod, b_mod)
    groups = n_batch + 1
    mod3 = mod.reshape(DEPTH, SUBLANES, 6, d)[:, :groups].reshape(DEPTH * groups * 6, 1, d)

    cos_t, sin_t = _make_rope(seq, TM_INPROJ)
    h = (x.reshape(n_lat, d), ctx.reshape(n_batch * n_ctx, d))

    for layer in range(DEPTH):
        last = layer == DEPTH - 1
        mod_base = layer * groups * 6
        w_in_bf = w_in[layer].astype(BF16)
        q, k, v, sb, rest = _input_projection(h, mod3, mod_base, w_in_bf, b_in[layer], N_EARLY,
                                              cos_t, sin_t, n_lat, seq)
        sp = jax.nn.softplus(-lru_lambda[layer])
        zb, hf, hb, att = _token_mixers(
            q, k, v, _attention_bias(na_rpb[layer], seq), sb, rest, sc_conv_w[layer], lru_conv_w[layer],
            lru_conv_b[layer], sp, _block_diag(lru_w_r[layer]).astype(BF16), _block_diag(lru_w_i[layer]).astype(BF16),
            lru_b_r[layer], lru_b_i[layer], n_batch, seq, n_ctx)
        n_rows = n_lat if last else n_all
        rw_full = jnp.pad(router_w[layer], ((0, 0), (0, LANES - N_EXPERTS)))
        rw_hi = rw_full.astype(BF16)
        rw_pad = jnp.concatenate([rw_hi, (rw_full - rw_hi.astype(F32)).astype(BF16)], axis=1)
        rb_pad = jnp.concatenate([router_b[layer], jnp.full((LANES - N_EXPERTS,), NEG_BIG, F32)]).reshape(1, LANES)
        h1, xs, pos4, cnt_out = _post_mixer(
            h, att, zb, hf, hb, rest, mod3, mod_base, w_in_bf, b_in[layer],
            w_proj_attn[layer].astype(BF16), w_proj_conv[layer].astype(BF16), w_proj_lru[layer].astype(BF16),
            w_o[layer].astype(BF16), b_o[layer], ln1_g[layer], ln1_b[layer], rw_pad, rb_pad, n_rows, n_lat, seq)
        plan = _expert_plan(cnt_out, n_rows // TM_POST, n_rows)
        ys = _experts(xs, plan, layer, exp_w_gu, exp_b_gu, exp_w_dn, exp_b_dn)
        h = (_combine(ys, pos4, h1, mod3, mod_base, ln2_g[layer], ln2_b[layer], n_lat, seq),)
    return h[0].reshape(n_batch, seq, d)
```

```python
import functools

import numpy as np
import jax
import jax.numpy as jnp
from jax import lax
from jax.experimental import pallas as pl
from jax.experimental.pallas import tpu as pltpu

D_MODEL = 1024
DEPTH = 2
GRID_W = 64
NA_HEADS = 8
NA_HEAD_DIM = 64
NA_WIDTH = NA_HEADS * NA_HEAD_DIM
NA_WIN_ROWS = 8
NA_WIN_COLS = 16
ROPE_BASE = 10000.0
BRANCH_WIDTH = 512
LRU_BLOCKS = 8
LRU_C = 8.0
P_TOTAL = 7168
N_EARLY = 8 * BRANCH_WIDTH
N_EXPERTS = 32
TOP_K = 4
SWIGLU_LIMIT = 7.0
SWIGLU_ALPHA = 1.702
LN_EPS = 1e-5
DEEPNORM_ALPHA = (2 * DEPTH) ** 0.25
NEG_BIG = -1e30

LANES = 128
SUBLANES = 8
VMEM_LIMIT_BYTES = 56 * 1024 * 1024

TM_INPROJ = 512
SCAN_CHUNK = 256
ATT_ROWS = 4
ATT_QROWS = ATT_ROWS * GRID_W
TM_POST = 256
TM_EXPERT = 512
XS_WIDTH = D_MODEL + LANES
TOK_BLOCK = TM_POST * TOP_K + N_EXPERTS * SUBLANES
ZERO_ROWS = 256

F32 = jnp.float32
BF16 = jnp.bfloat16


def _params(n_axes):
    return pltpu.CompilerParams(dimension_semantics=("arbitrary",) * n_axes,
                                vmem_limit_bytes=VMEM_LIMIT_BYTES)


def _layer_norm(x):
    mu = jnp.mean(x, axis=-1, keepdims=True)
    xc = x - mu
    var = jnp.mean(xc * xc, axis=-1, keepdims=True)
    return xc * lax.rsqrt(var + LN_EPS)


def _mod_kernel(c_ref, w_ref, b_ref, o_ref):
    c = c_ref[...]
    s = (c * jax.nn.sigmoid(c)).astype(BF16)
    o_ref[0] = jnp.dot(s, w_ref[0].astype(BF16), preferred_element_type=F32) + b_ref[0]


def _modulation(cc, w_mod, b_mod):
    n_out = w_mod.shape[-1]
    return pl.pallas_call(
        _mod_kernel,
        grid=(DEPTH, n_out // D_MODEL),
        in_specs=[pl.BlockSpec((SUBLANES, D_MODEL), lambda l, j: (0, 0)),
                  pl.BlockSpec((1, D_MODEL, D_MODEL), lambda l, j: (l, 0, j)),
                  pl.BlockSpec((1, 1, D_MODEL), lambda l, j: (l, 0, j))],
        out_specs=pl.BlockSpec((1, SUBLANES, D_MODEL), lambda l, j: (l, 0, j)),
        out_shape=jax.ShapeDtypeStruct((DEPTH, SUBLANES, n_out), F32),
        compiler_params=_params(2),
        name="modulation",
    )(cc, w_mod, b_mod.reshape(DEPTH, 1, n_out))


def _rope_half(x, cos, sin_signed):
    lane = lax.broadcasted_iota(jnp.int32, (x.shape[0], LANES), 1)
    first = (lane % 32) < 16
    outs = []
    for cidx in range(x.shape[1] // LANES):
        xc = x[:, cidx * LANES:(cidx + 1) * LANES]
        partner = jnp.where(first, pltpu.roll(xc, LANES - 16, 1), pltpu.roll(xc, 16, 1))
        outs.append(xc * cos + partner * sin_signed)
    return jnp.concatenate(outs, axis=1)


def _inproj_kernel(ha_ref, hb_ref, sh_ref, sc_ref, w_ref, b_ref, cos_ref, sin_ref,
                   q_ref, k_ref, v_ref, sb_ref, rest_ref, *, split):
    h = jnp.where(pl.program_id(0) < split, ha_ref[...], hb_ref[...])
    xn = (_layer_norm(h) * (1.0 + sc_ref[0]) + sh_ref[0]).astype(BF16)
    half = BRANCH_WIDTH

    def columns(lo, width):
        return jnp.dot(xn, w_ref[:, lo:lo + width], preferred_element_type=F32) + b_ref[:, lo:lo + width]

    cos = cos_ref[...]
    sin = sin_ref[...]
    q_ref[...] = _rope_half(columns(0, half), cos, sin).astype(BF16)
    k_ref[...] = _rope_half(columns(half, half), cos, sin).astype(BF16)
    v_ref[...] = columns(2 * half, half).astype(BF16)
    sb_ref[...] = columns(3 * half, half)
    n_rest = rest_ref.shape[1]
    for lo in range(0, n_rest, 2 * half):
        rest_ref[:, lo:lo + 2 * half] = columns(4 * half + lo, 2 * half)


def _input_projection(h_parts, mod3, mod_base, w_in_bf, b_in, n_cols, cos_t, sin_t, n_lat, seq):
    tm = TM_INPROJ
    ha, hb = h_parts[0], h_parts[-1]
    split = ha.shape[0] // tm
    m = ha.shape[0] + (hb.shape[0] if len(h_parts) == 2 else 0)
    n_lat_tiles = n_lat // tm
    tiles_per_seq = seq // tm
    n_groups_lat = n_lat // seq

    def group(i):
        return jnp.where(i < n_lat_tiles, i // tiles_per_seq, n_groups_lat)

    def rope_blk(i):
        return jnp.where(i < n_lat_tiles, i % tiles_per_seq, tiles_per_seq)

    half = BRANCH_WIDTH
    return pl.pallas_call(
        functools.partial(_inproj_kernel, split=split),
        grid=(m // tm,),
        in_specs=[pl.BlockSpec((tm, D_MODEL), lambda i: (jnp.minimum(i, split - 1), 0)),
                  pl.BlockSpec((tm, D_MODEL), lambda i: (jnp.maximum(i - split, 0), 0)),
                  pl.BlockSpec((1, 1, D_MODEL), lambda i: (mod_base + group(i) * 6 + 0, 0, 0)),
                  pl.BlockSpec((1, 1, D_MODEL), lambda i: (mod_base + group(i) * 6 + 1, 0, 0)),
                  pl.BlockSpec((D_MODEL, n_cols), lambda i: (0, 0)),
                  pl.BlockSpec((1, n_cols), lambda i: (0, 0)),
                  pl.BlockSpec((tm, LANES), lambda i: (rope_blk(i), 0)),
                  pl.BlockSpec((tm, LANES), lambda i: (rope_blk(i), 0))],
        out_specs=[pl.BlockSpec((tm, half), lambda i: (i, 0)),
                   pl.BlockSpec((tm, half), lambda i: (i, 0)),
                   pl.BlockSpec((tm, half), lambda i: (i, 0)),
                   pl.BlockSpec((tm, half), lambda i: (i, 0)),
                   pl.BlockSpec((tm, n_cols - 4 * half), lambda i: (i, 0))],
        out_shape=[jax.ShapeDtypeStruct((m, half), BF16),
                   jax.ShapeDtypeStruct((m, half), BF16),
                   jax.ShapeDtypeStruct((m, half), BF16),
                   jax.ShapeDtypeStruct((m, half), F32),
                   jax.ShapeDtypeStruct((m, n_cols - 4 * half), F32)],
        compiler_params=_params(1),
        name="input_projection",
    )(ha, hb, mod3, mod3, w_in_bf, b_in.reshape(1, -1), cos_t, sin_t)


def _make_rope(seq, tm):
    t = np.arange(seq)
    row_pos, col_pos = t // GRID_W, t % GRID_W
    d = np.arange(LANES) % NA_HEAD_DIM
    m = NA_HEAD_DIM // 4
    inv_freq = (ROPE_BASE ** (-jnp.arange(m, dtype=F32) / m))[d % m]
    pos = np.where((d < 2 * m)[None, :], row_pos[:, None], col_pos[:, None])
    ang = jnp.asarray(pos).astype(F32) * inv_freq[None, :]
    cos = jnp.cos(ang)
    sin = jnp.sin(ang)
    sin_signed = jnp.where(jnp.asarray((d % (2 * m)) < m)[None, :], -sin, sin)
    cos = jnp.concatenate([cos, jnp.ones((tm, LANES), F32)], axis=0)
    sin_signed = jnp.concatenate([sin_signed, jnp.zeros((tm, LANES), F32)], axis=0)
    return cos, sin_signed


def _mixer_kernel(fblk, bblk, first, last, seqb, r0t, cls,
                  sb_ref, xf_ref, xfp_ref, xfn_ref, lxb_ref, lxbp_ref, lxbn_ref,
                  scw_ref, cw_ref, cb_ref, sp_ref, wr_ref, wi_ref, br_ref, bi_ref,
                  q_ref, k_ref, v_ref, kc_ref, vc_ref, bias0_ref, bias1_ref, bias2_ref, bias3_ref,
                  zb_ref, hf_ref, hb_ref, att_ref,
                  a_s, b_s, hc_s, st_s, *, ch, n_ctx_items, band):
    it = pl.program_id(0)
    is_first = first[it] == 1
    is_last = last[it] == 1
    is_ctx = it < n_ctx_items
    b = seqb[it]
    width = BRANCH_WIDTH
    row = lax.broadcasted_iota(jnp.int32, (ch, width), 0)
    not_first = jnp.where(is_first, 0.0, 1.0).astype(F32)
    not_last = jnp.where(is_last, 0.0, 1.0).astype(F32)

    def back1(u, prev_row):
        return jnp.where(row == 0, prev_row, pltpu.roll(u, 1, 0))

    def back2(u, prev2, prev1):
        return jnp.where(row == 0, prev2, jnp.where(row == 1, prev1, pltpu.roll(u, 2, 0)))

    def fwd1(u, next_row):
        return jnp.where(row == ch - 1, next_row, pltpu.roll(u, ch - 1, 0))

    gate_cols, value_cols, lru_cols = (slice(s * width, (s + 1) * width) for s in range(3))

    def lru_input(x, p, n, prev_ok, next_ok):
        p = p * prev_ok
        n = n * next_ok
        return (cw_ref[0:1] * back2(x, p[6:7], p[7:8]) + cw_ref[1:2] * back1(x, p[7:8])
                + cw_ref[2:3] * x + cw_ref[3:4] * fwd1(x, n[0:1]) + cb_ref[...])

    def coeffs(d, xm):
        xb = xm.astype(BF16)
        r = jax.nn.sigmoid(jnp.dot(xb, wr_ref[d], preferred_element_type=F32) + br_ref[d:d + 1])
        g = jax.nn.sigmoid(jnp.dot(xb, wi_ref[d], preferred_element_type=F32) + bi_ref[d:d + 1])
        log_a = (-LRU_C * sp_ref[d:d + 1]) * r
        a = jnp.exp(log_a)
        a_s[d] = a
        b_s[d] = jnp.sqrt(-jnp.tanh(log_a) * (a * a + 1.0)) * (g * xm)

    @pl.when(jnp.logical_and(is_first, is_ctx))
    def _():
        hc_s[...] = jnp.zeros_like(hc_s)

    @pl.when(jnp.logical_and(is_first, jnp.logical_not(is_ctx)))
    def _():
        hc_s[0:1] = st_s[pl.ds(2 * b, 1), :]
        hc_s[1:2] = st_s[pl.ds(2 * b + 1, 1), :]

    coeffs(0, lru_input(xf_ref[:, lru_cols], xfp_ref[:, lru_cols], xfn_ref[:, lru_cols], not_first, not_last))
    hf = hc_s[0:1]
    for t in range(ch):
        hf = a_s[0, t:t + 1, :] * hf + b_s[0, t:t + 1, :]
        hf_ref[t:t + 1, :] = hf
    hc_s[0:1] = hf

    coeffs(1, lru_input(lxb_ref[...], lxbp_ref[...], lxbn_ref[...], not_last, not_first))

    u = xf_ref[:, gate_cols] * xf_ref[:, value_cols]
    u_prev = xfp_ref[7:8, gate_cols] * xfp_ref[7:8, value_cols] * not_first
    u_next = xfn_ref[0:1, gate_cols] * xfn_ref[0:1, value_cols] * not_last
    conv = scw_ref[0:1] * back1(u, u_prev) + scw_ref[1:2] * u + scw_ref[2:3] * fwd1(u, u_next)
    zb_ref[...] = (sb_ref[...] * conv).astype(BF16)

    hb = hc_s[1:2]
    for t in range(ch - 1, -1, -1):
        hb = a_s[1, t:t + 1, :] * hb + b_s[1, t:t + 1, :]
        hb_ref[t:t + 1, :] = hb
    hc_s[1:2] = hb

    _attention_item(it, r0t, cls, q_ref, k_ref, v_ref, kc_ref, vc_ref,
                    (bias0_ref, bias1_ref, bias2_ref, bias3_ref), att_ref, band)

    @pl.when(is_ctx)
    def _():
        st_s[pl.ds(2 * b, 1), :] = hf
        st_s[pl.ds(2 * b + 1, 1), :] = hb


def _mixer_tables(n_batch, seq, n_ctx, ch):
    assert n_ctx == ch and ch == ATT_QROWS
    nc = seq // ch
    rows = seq // GRID_W
    kr = min(NA_WIN_ROWS, rows)
    ctx0 = n_batch * seq // ch
    fblk, bblk, first, last, seqb, r0t, cls = [], [], [], [], [], [], []
    for b in range(n_batch):
        fblk.append(ctx0 + b); bblk.append(ctx0 + b); first.append(1); last.append(1); seqb.append(b)
        r0t.extend([0] * ATT_ROWS); cls.extend([kr] * ATT_ROWS)
    for b in range(n_batch):
        for c in range(nc):
            fblk.append(b * nc + c); bblk.append(b * nc + nc - 1 - c)
            first.append(int(c == 0)); last.append(int(c == nc - 1)); seqb.append(b)
            for r in range(c * ATT_ROWS, (c + 1) * ATT_ROWS):
                r0 = min(max(r - kr // 2, 0), rows - kr)
                r0t.append(r0); cls.append(r - r0)
    return [np.asarray(a, np.int32) for a in (fblk, bblk, first, last, seqb, r0t, cls)]


def _token_mixers(q, k, v, bias, sb, rest, sc_w, lru_cw, lru_cb, sp, wr_bd, wi_bd, b_r, b_i, n_batch, seq, n_ctx):
    m = sb.shape[0]
    ch = SCAN_CHUNK
    width = BRANCH_WIDTH
    tables = _mixer_tables(n_batch, seq, n_ctx, ch)
    n_items = len(tables[0])
    halo_per_chunk = ch // SUBLANES
    last_halo = m // SUBLANES - 1
    band = min(NA_WIN_ROWS, seq // GRID_W) * GRID_W
    ctx_blk0 = n_batch * seq // n_ctx

    def cur(col, which, n=1):
        return pl.BlockSpec((ch, n * width), lambda i, f, bk, *_: ((f, bk)[which][i], col))

    def prev(col, which, n=1):
        return pl.BlockSpec((SUBLANES, n * width),
                            lambda i, f, bk, *_: (jnp.maximum((f, bk)[which][i] * halo_per_chunk - 1, 0), col))

    def nxt(col, which, n=1):
        return pl.BlockSpec((SUBLANES, n * width),
                            lambda i, f, bk, *_: (jnp.minimum(((f, bk)[which][i] + 1) * halo_per_chunk, last_halo), col))

    def full(shape):
        return pl.BlockSpec(shape, lambda i, *_: (0,) * len(shape))

    in_specs = [cur(0, 0), cur(0, 0, 3), prev(0, 0, 3), nxt(0, 0, 3),
                cur(2, 1), prev(2, 1), nxt(2, 1),
                full(sc_w.shape), full(lru_cw.shape), full((1, width)), full(sp.shape),
                full(wr_bd.shape), full(wi_bd.shape), full(b_r.shape), full(b_i.shape)]
    in_specs += [cur(0, 0),
                 pl.BlockSpec((seq, width), lambda i, f, bk, fi, la, sq, *_: (sq[i], 0)),
                 pl.BlockSpec((seq, width), lambda i, f, bk, fi, la, sq, *_: (sq[i], 0)),
                 pl.BlockSpec((n_ctx, width), lambda i, f, bk, fi, la, sq, *_: (ctx_blk0 + sq[i], 0)),
                 pl.BlockSpec((n_ctx, width), lambda i, f, bk, fi, la, sq, *_: (ctx_blk0 + sq[i], 0))]
    for row in range(ATT_ROWS):
        in_specs.append(pl.BlockSpec((1, NA_HEADS, GRID_W, band),
                                     lambda i, f, bk, fi, la, sq, r0, cl, row=row: (cl[i * ATT_ROWS + row], 0, 0, 0)))
    out_specs = [cur(0, 0), cur(0, 0), cur(0, 1), cur(0, 0)]
    grid_spec = pltpu.PrefetchScalarGridSpec(
        num_scalar_prefetch=7, grid=(n_items,), in_specs=in_specs, out_specs=out_specs,
        scratch_shapes=[pltpu.VMEM((2, ch, width), F32), pltpu.VMEM((2, ch, width), F32),
                        pltpu.VMEM((SUBLANES, width), F32), pltpu.VMEM((2 * n_batch, width), F32)])
    return pl.pallas_call(
        functools.partial(_mixer_kernel, ch=ch, n_ctx_items=n_batch, band=band),
        grid_spec=grid_spec,
        out_shape=[jax.ShapeDtypeStruct((m, width), BF16),
                   jax.ShapeDtypeStruct((m, width), F32),
                   jax.ShapeDtypeStruct((m, width), F32),
                   jax.ShapeDtypeStruct((m, width), BF16)],
        compiler_params=_params(1),
        name="token_mixers",
    )(*[jnp.asarray(t) for t in tables],
      sb, rest, rest, rest, rest, rest, rest,
      sc_w, lru_cw, lru_cb.reshape(1, width), sp, wr_bd, wi_bd, b_r, b_i,
      q, k, v, k, v, bias, bias, bias, bias)


def _attention_item(it, r0t, cls, q_ref, k_ref, v_ref, kc_ref, vc_ref, bias_refs, o_ref, band):
    nq = GRID_W
    lane = lax.broadcasted_iota(jnp.int32, (nq, LANES), 1)
    low = lane < NA_HEAD_DIM
    scale = NA_HEAD_DIM ** -0.5
    nt = (((1,), (1,)), ((), ()))
    for row in range(ATT_ROWS):
        start = pl.multiple_of(r0t[it * ATT_ROWS + row] * GRID_W, GRID_W)
        bias_ref = bias_refs[row]
        qrows = slice(row * nq, (row + 1) * nq)
        for hp in range(NA_HEADS // 2):
            cols = slice(hp * LANES, (hp + 1) * LANES)
            qp = q_ref[qrows, cols].astype(F32) * scale
            qs = jnp.concatenate([jnp.where(low, qp, 0.0), jnp.where(low, 0.0, qp)], axis=0).astype(BF16)
            kb = k_ref[pl.ds(start, band), cols]
            vb = v_ref[pl.ds(start, band), cols]
            s_loc = lax.dot_general(qs, kb, nt, preferred_element_type=F32)
            s_ctx = lax.dot_general(qs, kc_ref[:, cols], nt, preferred_element_type=F32)
            bias = jnp.concatenate([bias_ref[0, 2 * hp], bias_ref[0, 2 * hp + 1]], axis=0)
            s_loc = s_loc + bias
            mx =jnp.maximum(jnp.max(s_loc, axis=-1, keepdims=True), jnp.max(s_ctx, axis=-1, keepdims=True))
            e_loc = jnp.exp(s_loc - mx)
            e_ctx = jnp.exp(s_ctx - mx)
            den = jnp.sum(e_loc, axis=-1, keepdims=True) + jnp.sum(e_ctx, axis=-1, keepdims=True)
            o = (jnp.dot(e_loc.astype(BF16), vb, preferred_element_type=F32)
                 + jnp.dot(e_ctx.astype(BF16), vc_ref[:, cols], preferred_element_type=F32)) / den
            o_ref[qrows, cols] = jnp.where(low, o[:nq], o[nq:]).astype(BF16)


def _attention_bias(rpb, seq):
    rows = seq // GRID_W
    kr = min(NA_WIN_ROWS, rows)
    kc = NA_WIN_COLS
    cq = np.arange(GRID_W)
    c0 = np.clip(cq - kc // 2, 0, GRID_W - kc)
    ck = np.arange(GRID_W)
    inside = (ck[None, :] >= c0[:, None]) & (ck[None, :] < c0[:, None] + kc)
    dc = np.clip(ck[None, :] - cq[:, None] + (NA_WIN_COLS - 1), 0, 2 * NA_WIN_COLS - 2)
    n_dr = 2 * NA_WIN_ROWS - 1
    n_dc = 2 * NA_WIN_COLS - 1
    pick = jnp.asarray((np.arange(n_dc)[:, None] == dc.reshape(1, -1)).astype(np.float32))
    picked = jnp.dot(rpb.reshape(-1, n_dc), pick, precision=lax.Precision.HIGHEST)
    picked = picked.reshape(NA_HEADS, 2 * NA_WIN_ROWS - 1, GRID_W, GRID_W)
    table = jnp.where(jnp.asarray(inside)[None, None], picked, NEG_BIG)
    table = table.transpose(0, 2, 1, 3).reshape(NA_HEADS, GRID_W, n_dr * GRID_W)
    classes = []
    for cl in range(kr):
        lo = (NA_WIN_ROWS - 1 - cl) * GRID_W
        classes.append(table[:, :, lo:lo + kr * GRID_W])
    classes.append(jnp.full((NA_HEADS, GRID_W, kr * GRID_W), NEG_BIG, F32))
    return jnp.stack(classes, axis=0)


def _post_kernel(ha_ref, hc_ref, att_ref, zb_ref, hf_ref, hb_ref, lg_ref,
                 sh1_ref, sc1_ref, g1_ref, sh2_ref, sc2_ref,
                 wgl0_ref, wgl1_ref, wgl2_ref, bgl0_ref, bgl1_ref, bgl2_ref,
                 wpa_ref, wpc_ref, wpl_ref, wo_ref, bo_ref, l1g_ref, l1b_ref,
                 rw_ref, rb_ref, tri_ref, upper_ref,
                 h1_ref, xs_ref, pos_ref, cnt_ref, *, split):
    h = jnp.where(pl.program_id(0) < split, ha_ref[...], hc_ref[...])
    u1 = (_layer_norm(h) * (1.0 + sc1_ref[0]) + sh1_ref[0]).astype(BF16)
    y_a = jnp.dot(att_ref[...], wpa_ref[...], preferred_element_type=F32)
    y_b = jnp.dot(zb_ref[...], wpc_ref[...], preferred_element_type=F32)
    zc = jax.nn.gelu(lg_ref[...]) * (hf_ref[...] + hb_ref[...])
    y_c = jnp.dot(zc.astype(BF16), wpl_ref[...], preferred_element_type=F32)
    merged = (jax.nn.sigmoid(jnp.dot(u1, wgl0_ref[...], preferred_element_type=F32) + bgl0_ref[...]) * y_a
              + jax.nn.sigmoid(jnp.dot(u1, wgl1_ref[...], preferred_element_type=F32) + bgl1_ref[...]) * y_b
              + jax.nn.sigmoid(jnp.dot(u1, wgl2_ref[...], preferred_element_type=F32) + bgl2_ref[...]) * y_c)
    y = jnp.dot(merged.astype(BF16), wo_ref[...], preferred_element_type=F32) + bo_ref[...]
    h1 = _layer_norm(DEEPNORM_ALPHA * h + g1_ref[0] * y) * l1g_ref[...] + l1b_ref[...]
    h1_ref[...] = h1
    u2 = _layer_norm(h1) * (1.0 + sc2_ref[0]) + sh2_ref[0]

    u_hi = u2.astype(BF16)
    u_lo = (u2 - u_hi.astype(F32)).astype(BF16)
    by_hi = jnp.dot(u_hi, rw_ref[...], preferred_element_type=F32)
    logits = (by_hi[:, :LANES] + by_hi[:, LANES:]
              + jnp.dot(u_lo, rw_ref[:, :LANES], preferred_element_type=F32) + rb_ref[...])
    tm = logits.shape[0]
    lane = lax.broadcasted_iota(jnp.int32, (tm, LANES), 1)
    lane_f = lane.astype(F32)
    work = logits
    tops, idxs, hots = [], [], []
    for _ in range(TOP_K):
        mx = jnp.max(work, axis=-1, keepdims=True)
        idx = jnp.min(jnp.where(work == mx, lane_f, float(LANES)), axis=-1, keepdims=True)
        hot = lane_f == idx
        work = jnp.where(hot, -3e38, work)
        tops.append(mx); idxs.append(idx); hots.append(hot)
    exps = [jnp.exp(t - tops[0]) for t in tops]
    den = exps[0] + exps[1] + exps[2] + exps[3]
    hot_all = jnp.zeros((tm, LANES), F32)
    for hot in hots:
        hot_all = hot_all + hot.astype(F32)
    cnt = jnp.sum(hot_all, axis=0, keepdims=True)
    cnt_pad = jnp.floor((cnt + (SUBLANES - 1.0)) * (1.0 / SUBLANES)) * SUBLANES
    off = jnp.dot(jnp.broadcast_to(cnt_pad, (SUBLANES, LANES)), upper_ref[...],
                  precision=lax.Precision.HIGHEST, preferred_element_type=F32)[0:1]
    slot = off + jnp.dot(tri_ref[...], hot_all.astype(BF16), preferred_element_type=F32)
    pos4 = jnp.zeros((tm, LANES), F32)
    w_tile = jnp.zeros((tm, LANES), F32)
    for kk in range(TOP_K):
        pos_k = jnp.sum(jnp.where(hots[kk], slot, 0.0), axis=-1, keepdims=True)
        pos4 = jnp.where(lane == kk, pos_k, pos4)
        p = exps[kk] / den
        p_hi = p.astype(BF16).astype(F32)
        p_mid = (p - p_hi).astype(BF16).astype(F32)
        p_lo = p - p_hi - p_mid
        w_tile = jnp.where(hots[kk], p_hi, w_tile)
        w_tile = jnp.where(lane_f == idxs[kk] + float(N_EXPERTS), p_mid, w_tile)
        w_tile = jnp.where(lane_f == idxs[kk] + float(2 * N_EXPERTS), p_lo, w_tile)
    pos_ref[...] = pos4
    sub = lax.broadcasted_iota(jnp.int32, (SUBLANES, LANES), 0)
    cnt_ref[...] = jnp.where(sub == 0, cnt, jnp.where(sub == 1, off, 0.0))

    n_sorted = xs_ref.shape[0]
    pos_t = pos4.T
    r_iota = lax.broadcasted_iota(jnp.int32, (n_sorted, tm), 0).astype(F32)
    hit = r_iota == pos_t[0:1, :]
    for kk in range(1, TOP_K):
        hit = jnp.logical_or(hit, r_iota == pos_t[kk:kk + 1, :])
    perm = jnp.where(hit, 1.0, 0.0).astype(BF16)
    feats = jnp.concatenate([u_hi, w_tile.astype(BF16)], axis=1)
    xs_ref[...] = jnp.dot(perm, feats, preferred_element_type=F32)


def _post_mixer(h_parts, att, zb, hf, hb, rest, mod3, mod_base, w_in_bf, b_in, wpa, wpc, wpl, wo, b_o, l1g, l1b,
                rw_pad, rb_pad, n_rows, n_lat, seq):
    tm = TM_POST
    width = BRANCH_WIDTH
    ha, hc = h_parts[0], h_parts[-1]
    split = min(ha.shape[0], n_rows) // tm
    gate_col0 = N_EARLY // D_MODEL
    tiles_per_seq = seq // tm
    n_lat_tiles = n_lat // tm
    n_groups_lat = n_lat // seq
    tri = jnp.asarray(np.tril(np.ones((tm, tm), np.float32), -1), BF16)
    upper = jnp.asarray(np.triu(np.ones((LANES, LANES), np.float32), 1))
    n_tiles = n_rows // tm

    def group(i):
        return jnp.where(i < n_lat_tiles, i // tiles_per_seq, n_groups_lat)

    def rows(wd, col=0):
        return pl.BlockSpec((tm, wd), lambda i: (i, col))

    def full(shape):
        return pl.BlockSpec(shape, lambda i: (0,) * len(shape))

    def mod(which):
        return pl.BlockSpec((1, 1, D_MODEL), lambda i: (mod_base + group(i) * 6 + which, 0, 0))

    in_specs = [pl.BlockSpec((tm, D_MODEL), lambda i: (jnp.minimum(i, split - 1), 0)),
                pl.BlockSpec((tm, D_MODEL), lambda i: (jnp.maximum(i - split, 0), 0)),
                rows(width), rows(width), rows(width), rows(width), rows(width, 3),
                mod(0), mod(1), mod(2), mod(3), mod(4)]
    in_specs += [pl.BlockSpec((D_MODEL, D_MODEL), lambda i, c=c: (0, gate_col0 + c)) for c in range(3)]
    in_specs += [pl.BlockSpec((1, D_MODEL), lambda i, c=c: (0, gate_col0 + c)) for c in range(3)]
    in_specs += [full(wpa.shape), full(wpc.shape), full(wpl.shape), full(wo.shape), full((1, D_MODEL)),
                full((1, D_MODEL)), full((1, D_MODEL)), full(rw_pad.shape), full(rb_pad.shape), full(tri.shape),
                full(upper.shape)]
    out_specs = [rows(D_MODEL), pl.BlockSpec((TOK_BLOCK, XS_WIDTH), lambda i: (i, 0)), rows(LANES),
                 pl.BlockSpec((SUBLANES, LANES), lambda i: (i, 0))]
    return pl.pallas_call(
        functools.partial(_post_kernel, split=split),
        grid=(n_tiles,),
        in_specs=in_specs, out_specs=out_specs,
        out_shape=[jax.ShapeDtypeStruct((n_rows, D_MODEL), F32),
                   jax.ShapeDtypeStruct((n_tiles * TOK_BLOCK, XS_WIDTH), F32),
                   jax.ShapeDtypeStruct((n_rows, LANES), F32),
                   jax.ShapeDtypeStruct((n_tiles * SUBLANES, LANES), F32)],
        compiler_params=_params(1),
        name="post_mixer",
    )(ha, hc, att, zb, hf, hb, rest, mod3, mod3, mod3, mod3, mod3,
      w_in_bf, w_in_bf, w_in_bf, b_in.reshape(1, -1), b_in.reshape(1, -1), b_in.reshape(1, -1),
      wpa, wpc, wpl, wo, b_o.reshape(1, D_MODEL), l1g.reshape(1, D_MODEL), l1b.reshape(1, D_MODEL),
      rw_pad, rb_pad, tri, upper)


def _expert_plan(cnt_out, n_tok_tiles, n_rows):
    tm = TM_EXPERT
    co = cnt_out.reshape(n_tok_tiles, SUBLANES, LANES)
    cnt = (co[:, 0, :N_EXPERTS].astype(jnp.int32) + SUBLANES - 1) // SUBLANES * SUBLANES
    off = co[:, 1, :N_EXPERTS].astype(jnp.int32)
    cum_end = jnp.cumsum(cnt, axis=0)
    cum = cum_end - cnt
    total = cum_end[-1]
    n_et = (total + tm - 1) // tm
    et_end = jnp.cumsum(n_et)
    n_act = et_end[-1:].astype(jnp.int32)
    n_tiles = -(-n_tok_tiles * TOK_BLOCK // tm) + N_EXPERTS
    j = jnp.arange(n_tiles, dtype=jnp.int32)
    tile_e = jnp.minimum(jnp.sum((et_end[None, :] <= j[:, None]).astype(jnp.int32), axis=1), N_EXPERTS - 1)
    pick_e = (tile_e[:, None] == jnp.arange(N_EXPERTS, dtype=jnp.int32)[None, :]).astype(F32)

    def per_tile(table):
        return jnp.dot(pick_e, table.astype(F32), precision=lax.Precision.HIGHEST).astype(jnp.int32)

    row0 = (j - per_tile(et_end - n_et)) * tm
    n_rows_tile = jnp.clip(per_tile(total) - row0, 0, tm)
    cum_e = per_tile(cum.T)
    cum_end_e = per_tile(cum_end.T)
    delta_e = per_tile((jnp.arange(n_tok_tiles, dtype=jnp.int32)[:, None] * TOK_BLOCK + off - cum).T)
    q = row0[:, None] + SUBLANES * jnp.arange(tm // SUBLANES, dtype=jnp.int32)[None, :]
    inside = jnp.logical_and(cum_e.T[:, :, None] <= q[None], q[None] < cum_end_e.T[:, :, None])
    src = q + jnp.sum(jnp.where(inside, delta_e.T[:, :, None], 0), axis=0)
    used = off[:, N_EXPERTS - 1] + cnt[:, N_EXPERTS - 1]
    first = jnp.concatenate([jnp.ones((1,), jnp.int32), (tile_e[1:] != tile_e[:-1]).astype(jnp.int32)])
    group = jnp.cumsum(first) - 1
    after = per_tile(et_end)
    next_e = jnp.where(after < n_act[0],
                       jnp.minimum(jnp.sum((et_end[None, :] <= after[:, None]).astype(jnp.int32), axis=1),
                                   N_EXPERTS - 1), -1)
    return (tile_e.astype(jnp.int32), n_rows_tile.astype(jnp.int32), n_act, src.reshape(-1).astype(jnp.int32),
            used.astype(jnp.int32), first, group.astype(jnp.int32), next_e.astype(jnp.int32), n_tiles)


def _expert_kernel(tile_e, n_rows_t, n_act, src_t, used_t, first_t, group_t, next_t,
                   xs_hbm, wgu_hbm, bgu_ref, wdn_hbm, bdn_ref, ys_hbm,
                   xin, yout, wgu_f, wdn_f, wgu_s, wdn_s, zeros, sem_in, sem_out, sem_zero, sem_wgu, sem_wdn,
                   *, tm, tok_block, n_tok_tiles, layer):
    j = pl.program_id(0)
    na = n_act[0]
    chunks = tm // SUBLANES

    def weight_copies(e, wslot):
        return (pltpu.make_async_copy(wgu_hbm.at[layer, e], wgu_f.at[wslot], sem_wgu.at[wslot]),
                pltpu.make_async_copy(wdn_hbm.at[layer, e], wdn_f.at[wslot], sem_wdn.at[wslot]))

    def gather(src, dst, size, slot):
        return pltpu.make_async_copy(xs_hbm.at[pl.ds(src, size)], xin.at[slot, pl.ds(dst, size)], sem_in.at[slot])

    def scatter(src, dst, size, slot):
        return pltpu.make_async_copy(yout.at[slot, pl.ds(dst, size)], ys_hbm.at[pl.ds(src, size)], sem_out.at[slot])

    def start_chunks(jj, slot, copy):
        def one(c, priority):
            src = pl.multiple_of(src_t[jj * chunks + c], SUBLANES)
            copy(src, pl.multiple_of(c * SUBLANES, SUBLANES), SUBLANES, slot).start(priority=priority)

        def body(c, carry):
            one(c, 0)
            return carry

        def body_unrolled(g, carry):
            for u in range(SUBLANES):
                one(g * SUBLANES + u, u % 2)
            return carry

        n = lax.shift_right_logical(n_rows_t[jj], 3)

        @pl.when(n == chunks)
        def _():
            lax.fori_loop(0, chunks // SUBLANES, body_unrolled, 0)

        @pl.when(n != chunks)
        def _():
            lax.fori_loop(0, n, body, 0)

    def wait_rows(jj, slot, copy):
        n = n_rows_t[jj]
        size = tm
        while size >= SUBLANES:
            @pl.when((n & size) != 0)
            def _(size=size):
                copy(0, 0, size, slot).wait()
            size //= 2

    slot = j % 2

    @pl.when(j == 0)
    def _():
        xin[...] = jnp.zeros_like(xin)
        start_chunks(0, 0, gather)
        zeros[...] = jnp.zeros_like(zeros)

        def clear_tail(i, copy_op):
            used = used_t[i]
            tail = tok_block - used
            for size in (256, 128, 64, 32, 16, 8):
                @pl.when((tail & size) != 0)
                def _():
                    at = pl.multiple_of(i * tok_block + used + (tail & ~(2 * size - 1)), SUBLANES)
                    copy_op(pltpu.make_async_copy(zeros.at[pl.ds(0, size)], ys_hbm.at[pl.ds(at, size)], sem_zero))

        def start_clear(i, c):
            clear_tail(i, lambda cp: cp.start())
            return c

        def wait_clear(i, c):
            clear_tail(i, lambda cp: cp.wait())
            return c

        lax.fori_loop(0, n_tok_tiles, start_clear, 0)
        lax.fori_loop(0, n_tok_tiles, wait_clear, 0)

    @pl.when(j + 1 < na)
    def _():
        start_chunks(j + 1, 1 - slot, gather)

    @pl.when(j < na)
    def _():
        e = tile_e[j]

        @pl.when(first_t[j] == 1)
        def _():
            wslot = group_t[j] % 2

            @pl.when(j == 0)
            def _():
                for cp in weight_copies(e, wslot):
                    cp.start()

            for cp in weight_copies(e, wslot):
                cp.wait()
            wgu_s[...] = wgu_f[wslot].astype(BF16)
            wdn_s[...] = wdn_f[wslot].astype(BF16)

            @pl.when(next_t[j] >= 0)
            def _():
                for cp in weight_copies(next_t[j], 1 - wslot):
                    cp.start()

        wait_rows(j, slot, gather)

        def compute(rows):
            x = xin[slot, pl.ds(0, rows)]
            lane = lax.broadcasted_iota(jnp.int32, (rows, LANES), 1)
            p = jnp.sum(jnp.where(lane % N_EXPERTS == e, x[:, D_MODEL:], 0.0), axis=-1, keepdims=True)
            gu = (jnp.dot(x[:, :D_MODEL].astype(BF16), wgu_s[...], preferred_element_type=F32)
                  + bgu_ref[0, pl.ds(e, 1), :])
            f = gu.shape[1] // 2
            gate = jnp.minimum(gu[:, :f], SWIGLU_LIMIT)
            up = jnp.clip(gu[:, f:], -SWIGLU_LIMIT, SWIGLU_LIMIT)
            hid = (up + 1.0) * gate * jax.nn.sigmoid(SWIGLU_ALPHA * gate)
            y = (jnp.dot(hid.astype(BF16), wdn_s[...], preferred_element_type=F32)
                 + bdn_ref[0, pl.ds(e, 1), :])
            yout[slot, pl.ds(0, rows)] = y * p

        quarter = tm // 4
        for part in range(1, 5):
            @pl.when(jnp.logical_and(n_rows_t[j] > (part - 1) * quarter, n_rows_t[j] <= part * quarter))
            def _(part=part):
                compute(part * quarter)

        start_chunks(j, slot, scatter)

        @pl.when(j >= 1)
        def _():
            wait_rows(j - 1, 1 - slot, scatter)

        @pl.when(j == na - 1)
        def _():
            wait_rows(j, slot, scatter)


def _experts(xs, plan, layer, w_gu, b_gu, w_dn, b_dn):
    tile_e, n_rows_tile, n_act, src, used, first, group, next_e, n_tiles = plan
    tm = TM_EXPERT
    f2 = w_gu.shape[-1]

    in_specs = [pl.BlockSpec(memory_space=pl.ANY),
                pl.BlockSpec(memory_space=pl.ANY),
                pl.BlockSpec((1, N_EXPERTS, f2), lambda j, *_: (layer, 0, 0)),
                pl.BlockSpec(memory_space=pl.ANY),
                pl.BlockSpec((1, N_EXPERTS, D_MODEL), lambda j, *_: (layer, 0, 0))]
    grid_spec = pltpu.PrefetchScalarGridSpec(
        num_scalar_prefetch=8, grid=(n_tiles,), in_specs=in_specs,
        out_specs=pl.BlockSpec(memory_space=pl.ANY),
        scratch_shapes=[pltpu.VMEM((2, tm, XS_WIDTH), F32), pltpu.VMEM((2, tm, D_MODEL), F32),
                        pltpu.VMEM((2, D_MODEL, f2), F32), pltpu.VMEM((2, f2 // 2, D_MODEL), F32),
                        pltpu.VMEM((D_MODEL, f2), BF16), pltpu.VMEM((f2 // 2, D_MODEL), BF16),
                        pltpu.VMEM((ZERO_ROWS, D_MODEL), F32),
                        pltpu.SemaphoreType.DMA((2,)), pltpu.SemaphoreType.DMA((2,)), pltpu.SemaphoreType.DMA(()),
                        pltpu.SemaphoreType.DMA((2,)), pltpu.SemaphoreType.DMA((2,))])
    return pl.pallas_call(
        functools.partial(_expert_kernel, tm=tm, tok_block=TOK_BLOCK, n_tok_tiles=xs.shape[0] // TOK_BLOCK,
                          layer=layer),
        grid_spec=grid_spec,
        out_shape=jax.ShapeDtypeStruct((xs.shape[0], D_MODEL), F32),
        compiler_params=_params(1),
        name="moe_experts",
    )(tile_e, n_rows_tile, n_act, src, used, first, group, next_e,
      xs, w_gu, b_gu, w_dn, b_dn)


def _combine_kernel(ys_ref, pos_ref, h1_ref, g2_ref, l2g_ref, l2b_ref, o_ref):
    tm = pos_ref.shape[0]
    n_sorted = ys_ref.shape[0]
    pos = pos_ref[...]
    col = lax.broadcasted_iota(jnp.int32, (tm, n_sorted), 1).astype(F32)
    sel = jnp.zeros((tm, n_sorted), F32)
    for kk in range(TOP_K):
        sel = sel + (col == pos[:, kk:kk + 1]).astype(F32)
    sel = sel.astype(BF16)
    ys = ys_ref[...]
    hi = ys.astype(BF16)
    rest = ys - hi.astype(F32)
    mid = rest.astype(BF16)
    lo = (rest - mid.astype(F32)).astype(BF16)
    y2 = (jnp.dot(sel, hi, preferred_element_type=F32) + jnp.dot(sel, mid, preferred_element_type=F32)
          + jnp.dot(sel, lo, preferred_element_type=F32))
    o_ref[...] = _layer_norm(DEEPNORM_ALPHA * h1_ref[...] + g2_ref[0] * y2) * l2g_ref[...] + l2b_ref[...]


def _combine(ys, pos4, h1, mod3, mod_base, l2g, l2b, n_lat, seq):
    n_rows = h1.shape[0]
    tm = TM_POST
    tiles_per_seq = seq // tm
    n_lat_tiles = n_lat // tm
    n_groups_lat = n_lat // seq

    def group(i):
        return jnp.where(i < n_lat_tiles, i // tiles_per_seq, n_groups_lat)

    in_specs = [pl.BlockSpec((TOK_BLOCK, D_MODEL), lambda i: (i, 0)),
                pl.BlockSpec((tm, LANES), lambda i: (i, 0)),
                pl.BlockSpec((tm, D_MODEL), lambda i: (i, 0)),
                pl.BlockSpec((1, 1, D_MODEL), lambda i: (mod_base + group(i) * 6 + 5, 0, 0)),
                pl.BlockSpec((1, D_MODEL), lambda i: (0, 0)),
                pl.BlockSpec((1, D_MODEL), lambda i: (0, 0))]
    return pl.pallas_call(
        _combine_kernel,
        grid=(n_rows // tm,),
        in_specs=in_specs,
        out_specs=pl.BlockSpec((tm, D_MODEL), lambda i: (i, 0)),
        out_shape=jax.ShapeDtypeStruct((n_rows, D_MODEL), F32),
        compiler_params=_params(1),
        name="moe_combine",
    )(ys, pos4, h1, mod3, l2g.reshape(1, D_MODEL), l2b.reshape(1, D_MODEL))


def _block_diag(w):
    two, n, d, e = w.shape
    eye = jnp.eye(n, dtype=w.dtype)
    return (w[:, :, :, None, :] * eye[None, :, None, :, None]).reshape(two, n * d, n * e)


def kernel(x, c, ctx, c_ctx, w_mod, b_mod, w_in, b_in, na_rpb, w_proj_attn, w_proj_conv, w_proj_lru, sc_conv_w, lru_conv_w, lru_conv_b, lru_lambda, lru_w_r, lru_b_r, lru_w_i, lru_b_i, w_o, b_o, ln1_g, ln1_b, router_w, router_b, exp_w_gu, exp_b_gu, exp_w_dn, exp_b_dn, ln2_g, ln2_b):
    n_batch, seq, d = x.shape
    n_ctx = ctx.shape[1]
    n_lat = n_batch * seq
    n_all = n_lat + n_batch * n_ctx
    assert d == D_MODEL and n_batch + 1 <= SUBLANES

    cc = jnp.concatenate([c, c_ctx[None], jnp.zeros((SUBLANES - n_batch - 1, d), F32)], axis=0)
    mod = _modulation(cc, w_mod, b_mod)
    groups = n_batch + 1
    mod3 = mod.reshape(DEPTH, SUBLANES, 6, d)[:, :groups].reshape(DEPTH * groups * 6, 1, d)

    cos_t, sin_t = _make_rope(seq, TM_INPROJ)
    h = (x.reshape(n_lat, d), ctx.reshape(n_batch * n_ctx, d))

    for layer in range(DEPTH):
        last = layer == DEPTH - 1
        mod_base = layer * groups * 6
        w_in_bf = w_in[layer].astype(BF16)
        q, k, v, sb, rest = _input_projection(h, mod3, mod_base, w_in_bf, b_in[layer], N_EARLY,
                                              cos_t, sin_t, n_lat, seq)
        sp = jax.nn.softplus(-lru_lambda[layer])
        zb, hf, hb, att = _token_mixers(
            q, k, v, _attention_bias(na_rpb[layer], seq), sb, rest, sc_conv_w[layer], lru_conv_w[layer],
            lru_conv_b[layer], sp, _block_diag(lru_w_r[layer]).astype(BF16), _block_diag(lru_w_i[layer]).astype(BF16),
            lru_b_r[layer], lru_b_i[layer], n_batch, seq, n_ctx)
        n_rows = n_lat if last else n_all
        rw_full = jnp.pad(router_w[layer], ((0, 0), (0, LANES - N_EXPERTS)))
        rw_hi = rw_full.astype(BF16)
        rw_pad = jnp.concatenate([rw_hi, (rw_full - rw_hi.astype(F32)).astype(BF16)], axis=1)
        rb_pad = jnp.concatenate([router_b[layer], jnp.full((LANES - N_EXPERTS,), NEG_BIG, F32)]).reshape(1, LANES)
        h1, xs, pos4, cnt_out = _post_mixer(
            h, att, zb, hf, hb, rest, mod3, mod_base, w_in_bf, b_in[layer],
            w_proj_attn[layer].astype(BF16), w_proj_conv[layer].astype(BF16), w_proj_lru[layer].astype(BF16),
            w_o[layer].astype(BF16), b_o[layer], ln1_g[layer], ln1_b[layer], rw_pad, rb_pad, n_rows, n_lat, seq)
        plan = _expert_plan(cnt_out, n_rows // TM_POST, n_rows)
        ys = _experts(xs, plan, layer, exp_w_gu, exp_b_gu, exp_w_dn, exp_b_dn)
        h = (_combine(ys, pos4, h1, mod3, mod_base, ln2_g[layer], ln2_b[layer], n_lat, seq),)
    return h[0].reshape(n_batch, seq, d)
```

```python
import functools

import numpy as np
import jax
import jax.numpy as jnp
from jax import lax
from jax.experimental import pallas as pl
from jax.experimental.pallas import tpu as pltpu

D_MODEL = 1024
DEPTH = 2
GRID_W = 64
NA_HEADS = 8
NA_HEAD_DIM = 64
NA_WIN_ROWS = 8
NA_WIN_COLS = 16
ROPE_BASE = 10000.0
BRANCH_WIDTH = 512
LRU_C = 8.0
N_EARLY = 8 * BRANCH_WIDTH
N_EXPERTS = 32
TOP_K = 4
SWIGLU_LIMIT = 7.0
SWIGLU_ALPHA = 1.702
LN_EPS = 1e-5
DEEPNORM_ALPHA = (2 * DEPTH) ** 0.25
NEG_BIG = -1e30

LANES = 128
SUBLANES = 8
VMEM_LIMIT_BYTES = 56 * 1024 * 1024

TM_INPROJ = 512
SCAN_CHUNK = 256
ATT_ROWS = 4
ATT_QROWS = ATT_ROWS * GRID_W
TM_POST = 256
TM_EXPERT = 512
XS_WIDTH = D_MODEL + LANES
TOK_BLOCK = TM_POST * TOP_K + N_EXPERTS * SUBLANES
ZERO_ROWS = TOK_BLOCK - TM_POST * TOP_K

F32 = jnp.float32
BF16 = jnp.bfloat16


def _params(n_axes):
    return pltpu.CompilerParams(dimension_semantics=("arbitrary",) * n_axes,
                                vmem_limit_bytes=VMEM_LIMIT_BYTES)


def _layer_norm(x):
    mu = jnp.mean(x, axis=-1, keepdims=True)
    xc = x - mu
    var = jnp.mean(xc * xc, axis=-1, keepdims=True)
    return xc * lax.rsqrt(var + LN_EPS)


def _mod_kernel(c_ref, w_ref, b_ref, o_ref):
    c = c_ref[...]
    s = (c * jax.nn.sigmoid(c)).astype(BF16)
    o_ref[0] = jnp.dot(s, w_ref[0].astype(BF16), preferred_element_type=F32) + b_ref[0]


def _modulation(cc, w_mod, b_mod):
    n_out = w_mod.shape[-1]
    return pl.pallas_call(
        _mod_kernel,
        grid=(DEPTH, n_out // D_MODEL),
        in_specs=[pl.BlockSpec((SUBLANES, D_MODEL), lambda l, j: (0, 0)),
                  pl.BlockSpec((1, D_MODEL, D_MODEL), lambda l, j: (l, 0, j)),
                  pl.BlockSpec((1, 1, D_MODEL), lambda l, j: (l, 0, j))],
        out_specs=pl.BlockSpec((1, SUBLANES, D_MODEL), lambda l, j: (l, 0, j)),
        out_shape=jax.ShapeDtypeStruct((DEPTH, SUBLANES, n_out), F32),
        compiler_params=_params(2),
        name="modulation",
    )(cc, w_mod, b_mod.reshape(DEPTH, 1, n_out))


def _rope_half(x, cos, sin_signed):
    m = NA_HEAD_DIM // 4
    lane = lax.broadcasted_iota(jnp.int32, (x.shape[0], LANES), 1)
    first = (lane % (2 * m)) < m
    outs = []
    for cidx in range(x.shape[1] // LANES):
        xc = x[:, cidx * LANES:(cidx + 1) * LANES]
        partner = jnp.where(first, pltpu.roll(xc, LANES - m, 1), pltpu.roll(xc, m, 1))
        outs.append(xc * cos + partner * sin_signed)
    return jnp.concatenate(outs, axis=1)


def _inproj_kernel(ha_ref, hb_ref, sh_ref, sc_ref, w_ref, b_ref, cos_ref, sin_ref,
                   q_ref, k_ref, v_ref, sb_ref, rest_ref, *, split):
    h = jnp.where(pl.program_id(0) < split, ha_ref[...], hb_ref[...])
    xn = (_layer_norm(h) * (1.0 + sc_ref[0]) + sh_ref[0]).astype(BF16)
    half = BRANCH_WIDTH

    def columns(lo, width):
        return jnp.dot(xn, w_ref[:, lo:lo + width], preferred_element_type=F32) + b_ref[:, lo:lo + width]

    cos = cos_ref[...]
    sin = sin_ref[...]
    q_ref[...] = _rope_half(columns(0, half), cos, sin).astype(BF16)
    k_ref[...] = _rope_half(columns(half, half), cos, sin).astype(BF16)
    v_ref[...] = columns(2 * half, half).astype(BF16)
    sb_ref[...] = columns(3 * half, half)
    n_rest = rest_ref.shape[1]
    for lo in range(0, n_rest, 2 * half):
        rest_ref[:, lo:lo + 2 * half] = columns(4 * half + lo, 2 * half)


def _input_projection(h_parts, mod3, mod_base, w_in_bf, b_in, n_cols, cos_t, sin_t, n_lat, seq):
    tm = TM_INPROJ
    ha, hb = h_parts[0], h_parts[-1]
    split = ha.shape[0] // tm
    m = ha.shape[0] + (hb.shape[0] if len(h_parts) == 2 else 0)
    n_lat_tiles = n_lat // tm
    tiles_per_seq = seq // tm
    n_groups_lat = n_lat // seq

    def group(i):
        return jnp.where(i < n_lat_tiles, i // tiles_per_seq, n_groups_lat)

    def rope_blk(i):
        return jnp.where(i < n_lat_tiles, i % tiles_per_seq, tiles_per_seq)

    half = BRANCH_WIDTH
    return pl.pallas_call(
        functools.partial(_inproj_kernel, split=split),
        grid=(m // tm,),
        in_specs=[pl.BlockSpec((tm, D_MODEL), lambda i: (jnp.minimum(i, split - 1), 0)),
                  pl.BlockSpec((tm, D_MODEL), lambda i: (jnp.maximum(i - split, 0), 0)),
                  pl.BlockSpec((1, 1, D_MODEL), lambda i: (mod_base + group(i) * 6 + 0, 0, 0)),
                  pl.BlockSpec((1, 1, D_MODEL), lambda i: (mod_base + group(i) * 6 + 1, 0, 0)),
                  pl.BlockSpec((D_MODEL, n_cols), lambda i: (0, 0)),
                  pl.BlockSpec((1, n_cols), lambda i: (0, 0)),
                  pl.BlockSpec((tm, LANES), lambda i: (rope_blk(i), 0)),
                  pl.BlockSpec((tm, LANES), lambda i: (rope_blk(i), 0))],
        out_specs=[pl.BlockSpec((tm, half), lambda i: (i, 0)),
                   pl.BlockSpec((tm, half), lambda i: (i, 0)),
                   pl.BlockSpec((tm, half), lambda i: (i, 0)),
                   pl.BlockSpec((tm, half), lambda i: (i, 0)),
                   pl.BlockSpec((tm, n_cols - 4 * half), lambda i: (i, 0))],
        out_shape=[jax.ShapeDtypeStruct((m, half), BF16),
                   jax.ShapeDtypeStruct((m, half), BF16),
                   jax.ShapeDtypeStruct((m, half), BF16),
                   jax.ShapeDtypeStruct((m, half), F32),
                   jax.ShapeDtypeStruct((m, n_cols - 4 * half), F32)],
        compiler_params=_params(1),
        name="input_projection",
    )(ha, hb, mod3, mod3, w_in_bf, b_in.reshape(1, -1), cos_t, sin_t)


def _make_rope(seq, tm):
    t = np.arange(seq)
    row_pos, col_pos = t // GRID_W, t % GRID_W
    d = np.arange(LANES) % NA_HEAD_DIM
    m = NA_HEAD_DIM // 4
    inv_freq = (ROPE_BASE ** (-jnp.arange(m, dtype=F32) / m))[d % m]
    pos = np.where((d < 2 * m)[None, :], row_pos[:, None], col_pos[:, None])
    ang = jnp.asarray(pos).astype(F32) * inv_freq[None, :]
    cos = jnp.cos(ang)
    sin = jnp.sin(ang)
    sin_signed = jnp.where(jnp.asarray((d % (2 * m)) < m)[None, :], -sin, sin)
    cos = jnp.concatenate([cos, jnp.ones((tm, LANES), F32)], axis=0)
    sin_signed = jnp.concatenate([sin_signed, jnp.zeros((tm, LANES), F32)], axis=0)
    return cos, sin_signed


def _mixer_kernel(fblk, bblk, first, last, seqb, r0t, cls,
                  sb_ref, xf_ref, xfp_ref, xfn_ref, lxb_ref, lxbp_ref, lxbn_ref,
                  scw_ref, cw_ref, cb_ref, sp_ref, wr_ref, wi_ref, br_ref, bi_ref,
                  q_ref, k_ref, v_ref, kc_ref, vc_ref, bias0_ref, bias1_ref, bias2_ref, bias3_ref,
                  zb_ref, hf_ref, hb_ref, att_ref,
                  a_s, b_s, hc_s, st_s, *, ch, n_ctx_items, band):
    it = pl.program_id(0)
    is_first = first[it] == 1
    is_last = last[it] == 1
    is_ctx = it < n_ctx_items
    b = seqb[it]
    width = BRANCH_WIDTH
    row = lax.broadcasted_iota(jnp.int32, (ch, width), 0)
    not_first = jnp.where(is_first, 0.0, 1.0).astype(F32)
    not_last = jnp.where(is_last, 0.0, 1.0).astype(F32)

    def back1(u, prev_row):
        return jnp.where(row == 0, prev_row, pltpu.roll(u, 1, 0))

    def back2(u, prev2, prev1):
        return jnp.where(row == 0, prev2, jnp.where(row == 1, prev1, pltpu.roll(u, 2, 0)))

    def fwd1(u, next_row):
        return jnp.where(row == ch - 1, next_row, pltpu.roll(u, ch - 1, 0))

    gate_cols, value_cols, lru_cols = (slice(s * width, (s + 1) * width) for s in range(3))

    def lru_input(x, p, n, prev_ok, next_ok):
        p = p * prev_ok
        n = n * next_ok
        return (cw_ref[0:1] * back2(x, p[6:7], p[7:8]) + cw_ref[1:2] * back1(x, p[7:8])
                + cw_ref[2:3] * x + cw_ref[3:4] * fwd1(x, n[0:1]) + cb_ref[...])

    def coeffs(d, xm):
        xb = xm.astype(BF16)
        r = jax.nn.sigmoid(jnp.dot(xb, wr_ref[d], preferred_element_type=F32) + br_ref[d:d + 1])
        g = jax.nn.sigmoid(jnp.dot(xb, wi_ref[d], preferred_element_type=F32) + bi_ref[d:d + 1])
        log_a = (-LRU_C * sp_ref[d:d + 1]) * r
        a = jnp.exp(log_a)
        a_s[d] = a
        b_s[d] = jnp.sqrt(-jnp.tanh(log_a) * (a * a + 1.0)) * (g * xm)

    @pl.when(jnp.logical_and(is_first, is_ctx))
    def _():
        hc_s[...] = jnp.zeros_like(hc_s)

    @pl.when(jnp.logical_and(is_first, jnp.logical_not(is_ctx)))
    def _():
        hc_s[0:1] = st_s[pl.ds(2 * b, 1), :]
        hc_s[1:2] = st_s[pl.ds(2 * b + 1, 1), :]

    coeffs(0, lru_input(xf_ref[:, lru_cols], xfp_ref[:, lru_cols], xfn_ref[:, lru_cols], not_first, not_last))
    hf = hc_s[0:1]
    for t in range(ch):
        hf = a_s[0, t:t + 1, :] * hf + b_s[0, t:t + 1, :]
        hf_ref[t:t + 1, :] = hf
    hc_s[0:1] = hf

    coeffs(1, lru_input(lxb_ref[...], lxbp_ref[...], lxbn_ref[...], not_last, not_first))

    u = xf_ref[:, gate_cols] * xf_ref[:, value_cols]
    u_prev = xfp_ref[7:8, gate_cols] * xfp_ref[7:8, value_cols] * not_first
    u_next = xfn_ref[0:1, gate_cols] * xfn_ref[0:1, value_cols] * not_last
    conv = scw_ref[0:1] * back1(u, u_prev) + scw_ref[1:2] * u + scw_ref[2:3] * fwd1(u, u_next)
    zb_ref[...] = (sb_ref[...] * conv).astype(BF16)

    hb = hc_s[1:2]
    for t in range(ch - 1, -1, -1):
        hb = a_s[1, t:t + 1, :] * hb + b_s[1, t:t + 1, :]
        hb_ref[t:t + 1, :] = hb
    hc_s[1:2] = hb

    _attention_item(it, r0t, cls, q_ref, k_ref, v_ref, kc_ref, vc_ref,
                    (bias0_ref, bias1_ref, bias2_ref, bias3_ref), att_ref, band)

    @pl.when(is_ctx)
    def _():
        st_s[pl.ds(2 * b, 1), :] = hf
        st_s[pl.ds(2 * b + 1, 1), :] = hb


def _mixer_tables(n_batch, seq, n_ctx, ch):
    assert n_ctx == ch and ch == ATT_QROWS
    nc = seq // ch
    rows = seq // GRID_W
    kr = min(NA_WIN_ROWS, rows)
    ctx0 = n_batch * seq // ch
    fblk, bblk, first, last, seqb, r0t, cls = [], [], [], [], [], [], []
    for b in range(n_batch):
        fblk.append(ctx0 + b); bblk.append(ctx0 + b); first.append(1); last.append(1); seqb.append(b)
        r0t.extend([0] * ATT_ROWS); cls.extend([kr] * ATT_ROWS)
    for b in range(n_batch):
        for c in range(nc):
            fblk.append(b * nc + c); bblk.append(b * nc + nc - 1 - c)
            first.append(int(c == 0)); last.append(int(c == nc - 1)); seqb.append(b)
            for r in range(c * ATT_ROWS, (c + 1) * ATT_ROWS):
                r0 = min(max(r - kr // 2, 0), rows - kr)
                r0t.append(r0); cls.append(r - r0)
    return [np.asarray(a, np.int32) for a in (fblk, bblk, first, last, seqb, r0t, cls)]


def _token_mixers(q, k, v, bias, sb, rest, sc_w, lru_cw, lru_cb, sp, wr_bd, wi_bd, b_r, b_i, n_batch, seq, n_ctx):
    m = sb.shape[0]
    ch = SCAN_CHUNK
    width = BRANCH_WIDTH
    tables = _mixer_tables(n_batch, seq, n_ctx, ch)
    n_items = len(tables[0])
    halo_per_chunk = ch // SUBLANES
    last_halo = m // SUBLANES - 1
    band = min(NA_WIN_ROWS, seq // GRID_W) * GRID_W
    ctx_blk0 = n_batch * seq // n_ctx

    def cur(col, which, n=1):
        return pl.BlockSpec((ch, n * width), lambda i, f, bk, *_: ((f, bk)[which][i], col))

    def prev(col, which, n=1):
        return pl.BlockSpec((SUBLANES, n * width),
                            lambda i, f, bk, *_: (jnp.maximum((f, bk)[which][i] * halo_per_chunk - 1, 0), col))

    def nxt(col, which, n=1):
        return pl.BlockSpec((SUBLANES, n * width),
                            lambda i, f, bk, *_: (jnp.minimum(((f, bk)[which][i] + 1) * halo_per_chunk, last_halo), col))

    def full(shape):
        return pl.BlockSpec(shape, lambda i, *_: (0,) * len(shape))

    in_specs = [cur(0, 0), cur(0, 0, 3), prev(0, 0, 3), nxt(0, 0, 3),
                cur(2, 1), prev(2, 1), nxt(2, 1),
                full(sc_w.shape), full(lru_cw.shape), full((1, width)), full(sp.shape),
                full(wr_bd.shape), full(wi_bd.shape), full(b_r.shape), full(b_i.shape)]
    in_specs += [cur(0, 0),
                 pl.BlockSpec((seq, width), lambda i, f, bk, fi, la, sq, *_: (sq[i], 0)),
                 pl.BlockSpec((seq, width), lambda i, f, bk, fi, la, sq, *_: (sq[i], 0)),
                 pl.BlockSpec((n_ctx, width), lambda i, f, bk, fi, la, sq, *_: (ctx_blk0 + sq[i], 0)),
                 pl.BlockSpec((n_ctx, width), lambda i, f, bk, fi, la, sq, *_: (ctx_blk0 + sq[i], 0))]
    for row in range(ATT_ROWS):
        in_specs.append(pl.BlockSpec((1, NA_HEADS, GRID_W, band),
                                     lambda i, f, bk, fi, la, sq, r0, cl, row=row: (cl[i * ATT_ROWS + row], 0, 0, 0)))
    out_specs = [cur(0, 0), cur(0, 0), cur(0, 1), cur(0, 0)]
    grid_spec = pltpu.PrefetchScalarGridSpec(
        num_scalar_prefetch=7, grid=(n_items,), in_specs=in_specs, out_specs=out_specs,
        scratch_shapes=[pltpu.VMEM((2, ch, width), F32), pltpu.VMEM((2, ch, width), F32),
                        pltpu.VMEM((SUBLANES, width), F32), pltpu.VMEM((2 * n_batch, width), F32)])
    return pl.pallas_call(
        functools.partial(_mixer_kernel, ch=ch, n_ctx_items=n_batch, band=band),
        grid_spec=grid_spec,
        out_shape=[jax.ShapeDtypeStruct((m, width), BF16),
                   jax.ShapeDtypeStruct((m, width), F32),
                   jax.ShapeDtypeStruct((m, width), F32),
                   jax.ShapeDtypeStruct((m, width), BF16)],
        compiler_params=_params(1),
        name="token_mixers",
    )(*[jnp.asarray(t) for t in tables],
      sb, rest, rest, rest, rest, rest, rest,
      sc_w, lru_cw, lru_cb.reshape(1, width), sp, wr_bd, wi_bd, b_r, b_i,
      q, k, v, k, v, bias, bias, bias, bias)


def _attention_item(it, r0t, cls, q_ref, k_ref, v_ref, kc_ref, vc_ref, bias_refs, o_ref, band):
    nq = GRID_W
    lane = lax.broadcasted_iota(jnp.int32, (nq, LANES), 1)
    low = lane < NA_HEAD_DIM
    scale = NA_HEAD_DIM ** -0.5
    nt = (((1,), (1,)), ((), ()))
    for row in range(ATT_ROWS):
        start = pl.multiple_of(r0t[it * ATT_ROWS + row] * GRID_W, GRID_W)
        bias_ref = bias_refs[row]
        qrows = slice(row * nq, (row + 1) * nq)
        for hp in range(NA_HEADS // 2):
            cols = slice(hp * LANES, (hp + 1) * LANES)
            qp = q_ref[qrows, cols].astype(F32) * scale
            qs = jnp.concatenate([jnp.where(low, qp, 0.0), jnp.where(low, 0.0, qp)], axis=0).astype(BF16)
            kb = k_ref[pl.ds(start, band), cols]
            vb = v_ref[pl.ds(start, band), cols]
            s_loc = lax.dot_general(qs, kb, nt, preferred_element_type=F32)
            s_ctx = lax.dot_general(qs, kc_ref[:, cols], nt, preferred_element_type=F32)
            bias = jnp.concatenate([bias_ref[0, 2 * hp], bias_ref[0, 2 * hp + 1]], axis=0)
            s_loc = s_loc + bias
            mx =jnp.maximum(jnp.max(s_loc, axis=-1, keepdims=True), jnp.max(s_ctx, axis=-1, keepdims=True))
            e_loc = jnp.exp(s_loc - mx)
            e_ctx = jnp.exp(s_ctx - mx)
            den = jnp.sum(e_loc, axis=-1, keepdims=True) + jnp.sum(e_ctx, axis=-1, keepdims=True)
            o = (jnp.dot(e_loc.astype(BF16), vb, preferred_element_type=F32)
                 + jnp.dot(e_ctx.astype(BF16), vc_ref[:, cols], preferred_element_type=F32)) / den
            o_ref[qrows, cols] = jnp.where(low, o[:nq], o[nq:]).astype(BF16)


def _attention_bias(rpb, seq):
    rows = seq // GRID_W
    kr = min(NA_WIN_ROWS, rows)
    kc = NA_WIN_COLS
    cq = np.arange(GRID_W)
    c0 = np.clip(cq - kc // 2, 0, GRID_W - kc)
    ck = np.arange(GRID_W)
    inside = (ck[None, :] >= c0[:, None]) & (ck[None, :] < c0[:, None] + kc)
    dc = np.clip(ck[None, :] - cq[:, None] + (NA_WIN_COLS - 1), 0, 2 * NA_WIN_COLS - 2)
    n_dr = 2 * NA_WIN_ROWS - 1
    n_dc = 2 * NA_WIN_COLS - 1
    pick = jnp.asarray((np.arange(n_dc)[:, None] == dc.reshape(1, -1)).astype(np.float32))
    picked = jnp.dot(rpb.reshape(-1, n_dc), pick, precision=lax.Precision.HIGHEST)
    picked = picked.reshape(NA_HEADS, 2 * NA_WIN_ROWS - 1, GRID_W, GRID_W)
    table = jnp.where(jnp.asarray(inside)[None, None], picked, NEG_BIG)
    table = table.transpose(0, 2, 1, 3).reshape(NA_HEADS, GRID_W, n_dr * GRID_W)
    classes = []
    for cl in range(kr):
        lo = (NA_WIN_ROWS - 1 - cl) * GRID_W
        classes.append(table[:, :, lo:lo + kr * GRID_W])
    classes.append(jnp.full((NA_HEADS, GRID_W, kr * GRID_W), NEG_BIG, F32))
    return jnp.stack(classes, axis=0)


def _post_kernel(ha_ref, hc_ref, att_ref, zb_ref, hf_ref, hb_ref, lg_ref,
                 sh1_ref, sc1_ref, g1_ref, sh2_ref, sc2_ref,
                 wgl0_ref, wgl1_ref, wgl2_ref, bgl0_ref, bgl1_ref, bgl2_ref,
                 wpa_ref, wpc_ref, wpl_ref, wo_ref, bo_ref, l1g_ref, l1b_ref,
                 rw_ref, rb_ref, tri_ref, upper_ref,
                 h1_ref, xs_ref, pos_ref, cnt_ref, *, split):
    h = jnp.where(pl.program_id(0) < split, ha_ref[...], hc_ref[...])
    u1 = (_layer_norm(h) * (1.0 + sc1_ref[0]) + sh1_ref[0]).astype(BF16)
    y_a = jnp.dot(att_ref[...], wpa_ref[...], preferred_element_type=F32)
    y_b = jnp.dot(zb_ref[...], wpc_ref[...], preferred_element_type=F32)
    zc = jax.nn.gelu(lg_ref[...]) * (hf_ref[...] + hb_ref[...])
    y_c = jnp.dot(zc.astype(BF16), wpl_ref[...], preferred_element_type=F32)
    merged = (jax.nn.sigmoid(jnp.dot(u1, wgl0_ref[...], preferred_element_type=F32) + bgl0_ref[...]) * y_a
              + jax.nn.sigmoid(jnp.dot(u1, wgl1_ref[...], preferred_element_type=F32) + bgl1_ref[...]) * y_b
              + jax.nn.sigmoid(jnp.dot(u1, wgl2_ref[...], preferred_element_type=F32) + bgl2_ref[...]) * y_c)
    y = jnp.dot(merged.astype(BF16), wo_ref[...], preferred_element_type=F32) + bo_ref[...]
    h1 = _layer_norm(DEEPNORM_ALPHA * h + g1_ref[0] * y) * l1g_ref[...] + l1b_ref[...]
    h1_ref[...] = h1
    u2 = _layer_norm(h1) * (1.0 + sc2_ref[0]) + sh2_ref[0]

    u_hi = u2.astype(BF16)
    u_lo = (u2 - u_hi.astype(F32)).astype(BF16)
    by_hi = jnp.dot(u_hi, rw_ref[...], preferred_element_type=F32)
    logits = (by_hi[:, :LANES] + by_hi[:, LANES:]
              + jnp.dot(u_lo, rw_ref[:, :LANES], preferred_element_type=F32) + rb_ref[...])
    tm = logits.shape[0]
    lane = lax.broadcasted_iota(jnp.int32, (tm, LANES), 1)
    lane_f = lane.astype(F32)
    work = logits
    tops, idxs, hots = [], [], []
    for _ in range(TOP_K):
        mx = jnp.max(work, axis=-1, keepdims=True)
        idx = jnp.min(jnp.where(work == mx, lane_f, float(LANES)), axis=-1, keepdims=True)
        hot = lane_f == idx
        work = jnp.where(hot, -3e38, work)
        tops.append(mx); idxs.append(idx); hots.append(hot)
    exps = [jnp.exp(t - tops[0]) for t in tops]
    den = exps[0] + exps[1] + exps[2] + exps[3]
    hot_all = jnp.zeros((tm, LANES), F32)
    for hot in hots:
        hot_all = hot_all + hot.astype(F32)
    cnt = jnp.sum(hot_all, axis=0, keepdims=True)
    cnt_pad = jnp.floor((cnt + (SUBLANES - 1.0)) * (1.0 / SUBLANES)) * SUBLANES
    off = jnp.dot(jnp.broadcast_to(cnt_pad, (SUBLANES, LANES)), upper_ref[...],
                  precision=lax.Precision.HIGHEST, preferred_element_type=F32)[0:1]
    slot = off + jnp.dot(tri_ref[...], hot_all.astype(BF16), preferred_element_type=F32)
    pos4 = jnp.zeros((tm, LANES), F32)
    w_tile = jnp.zeros((tm, LANES), F32)
    for kk in range(TOP_K):
        pos_k = jnp.sum(jnp.where(hots[kk], slot, 0.0), axis=-1, keepdims=True)
        pos4 = jnp.where(lane == kk, pos_k, pos4)
        p = exps[kk] / den
        p_hi = p.astype(BF16).astype(F32)
        p_mid = (p - p_hi).astype(BF16).astype(F32)
        p_lo = p - p_hi - p_mid
        w_tile = jnp.where(hots[kk], p_hi, w_tile)
        w_tile = jnp.where(lane_f == idxs[kk] + float(N_EXPERTS), p_mid, w_tile)
        w_tile = jnp.where(lane_f == idxs[kk] + float(2 * N_EXPERTS), p_lo, w_tile)
    pos_ref[...] = pos4
    sub = lax.broadcasted_iota(jnp.int32, (SUBLANES, LANES), 0)
    cnt_ref[...] = jnp.where(sub == 0, cnt, jnp.where(sub == 1, off, 0.0))

    n_sorted = xs_ref.shape[0]
    pos_t = pos4.T
    r_iota = lax.broadcasted_iota(jnp.int32, (n_sorted, tm), 0).astype(F32)
    hit = r_iota == pos_t[0:1, :]
    for kk in range(1, TOP_K):
        hit = jnp.logical_or(hit, r_iota == pos_t[kk:kk + 1, :])
    perm = jnp.where(hit, 1.0, 0.0).astype(BF16)
    feats = jnp.concatenate([u_hi, w_tile.astype(BF16)], axis=1)
    xs_ref[...] = jnp.dot(perm, feats, preferred_element_type=F32)


def _post_mixer(h_parts, att, zb, hf, hb, rest, mod3, mod_base, w_in_bf, b_in, wpa, wpc, wpl, wo, b_o, l1g, l1b,
                rw_pad, rb_pad, n_rows, n_lat, seq):
    tm = TM_POST
    width = BRANCH_WIDTH
    ha, hc = h_parts[0], h_parts[-1]
    split = min(ha.shape[0], n_rows) // tm
    gate_col0 = N_EARLY // D_MODEL
    tiles_per_seq = seq // tm
    n_lat_tiles = n_lat // tm
    n_groups_lat = n_lat // seq
    tri = jnp.asarray(np.tril(np.ones((tm, tm), np.float32), -1), BF16)
    upper = jnp.asarray(np.triu(np.ones((LANES, LANES), np.float32), 1))
    n_tiles = n_rows // tm

    def group(i):
        return jnp.where(i < n_lat_tiles, i // tiles_per_seq, n_groups_lat)

    def rows(wd, col=0):
        return pl.BlockSpec((tm, wd), lambda i: (i, col))

    def full(shape):
        return pl.BlockSpec(shape, lambda i: (0,) * len(shape))

    def mod(which):
        return pl.BlockSpec((1, 1, D_MODEL), lambda i: (mod_base + group(i) * 6 + which, 0, 0))

    in_specs = [pl.BlockSpec((tm, D_MODEL), lambda i: (jnp.minimum(i, split - 1), 0)),
                pl.BlockSpec((tm, D_MODEL), lambda i: (jnp.maximum(i - split, 0), 0)),
                rows(width), rows(width), rows(width), rows(width), rows(width, 3),
                mod(0), mod(1), mod(2), mod(3), mod(4)]
    in_specs += [pl.BlockSpec((D_MODEL, D_MODEL), lambda i, c=c: (0, gate_col0 + c)) for c in range(3)]
    in_specs += [pl.BlockSpec((1, D_MODEL), lambda i, c=c: (0, gate_col0 + c)) for c in range(3)]
    in_specs += [full(wpa.shape), full(wpc.shape), full(wpl.shape), full(wo.shape), full((1, D_MODEL)),
                full((1, D_MODEL)), full((1, D_MODEL)), full(rw_pad.shape), full(rb_pad.shape), full(tri.shape),
                full(upper.shape)]
    out_specs = [rows(D_MODEL), pl.BlockSpec((TOK_BLOCK, XS_WIDTH), lambda i: (i, 0)), rows(LANES),
                 pl.BlockSpec((SUBLANES, LANES), lambda i: (i, 0))]
    return pl.pallas_call(
        functools.partial(_post_kernel, split=split),
        grid=(n_tiles,),
        in_specs=in_specs, out_specs=out_specs,
        out_shape=[jax.ShapeDtypeStruct((n_rows, D_MODEL), F32),
                   jax.ShapeDtypeStruct((n_tiles * TOK_BLOCK, XS_WIDTH), F32),
                   jax.ShapeDtypeStruct((n_rows, LANES), F32),
                   jax.ShapeDtypeStruct((n_tiles * SUBLANES, LANES), F32)],
        compiler_params=_params(1),
        name="post_mixer",
    )(ha, hc, att, zb, hf, hb, rest, mod3, mod3, mod3, mod3, mod3,
      w_in_bf, w_in_bf, w_in_bf, b_in.reshape(1, -1), b_in.reshape(1, -1), b_in.reshape(1, -1),
      wpa, wpc, wpl, wo, b_o.reshape(1, D_MODEL), l1g.reshape(1, D_MODEL), l1b.reshape(1, D_MODEL),
      rw_pad, rb_pad, tri, upper)


def _expert_plan(cnt_out, n_tok_tiles, n_rows):
    tm = TM_EXPERT
    co = cnt_out.reshape(n_tok_tiles, SUBLANES, LANES)
    cnt = (co[:, 0, :N_EXPERTS].astype(jnp.int32) + SUBLANES - 1) // SUBLANES * SUBLANES
    off = co[:, 1, :N_EXPERTS].astype(jnp.int32)
    cum_end = jnp.cumsum(cnt, axis=0)
    cum = cum_end - cnt
    total = cum_end[-1]
    n_et = (total + tm - 1) // tm
    et_end = jnp.cumsum(n_et)
    n_act = et_end[-1:].astype(jnp.int32)
    n_tiles = -(-n_tok_tiles * TOK_BLOCK // tm) + N_EXPERTS
    j = jnp.arange(n_tiles, dtype=jnp.int32)
    tile_e = jnp.minimum(jnp.sum((et_end[None, :] <= j[:, None]).astype(jnp.int32), axis=1), N_EXPERTS - 1)
    pick_e = (tile_e[:, None] == jnp.arange(N_EXPERTS, dtype=jnp.int32)[None, :]).astype(F32)

    def per_tile(table):
        return jnp.dot(pick_e, table.astype(F32), precision=lax.Precision.HIGHEST).astype(jnp.int32)

    row0 = (j - per_tile(et_end - n_et)) * tm
    n_rows_tile = jnp.clip(per_tile(total) - row0, 0, tm)
    cum_e = per_tile(cum.T)
    cum_end_e = per_tile(cum_end.T)
    delta_e = per_tile((jnp.arange(n_tok_tiles, dtype=jnp.int32)[:, None] * TOK_BLOCK + off - cum).T)
    q = row0[:, None] + SUBLANES * jnp.arange(tm // SUBLANES, dtype=jnp.int32)[None, :]
    inside = jnp.logical_and(cum_e.T[:, :, None] <= q[None], q[None] < cum_end_e.T[:, :, None])
    src = q + jnp.sum(jnp.where(inside, delta_e.T[:, :, None], 0), axis=0)
    used = off[:, N_EXPERTS - 1] + cnt[:, N_EXPERTS - 1]
    first = jnp.concatenate([jnp.ones((1,), jnp.int32), (tile_e[1:] != tile_e[:-1]).astype(jnp.int32)])
    group = jnp.cumsum(first) - 1
    after = per_tile(et_end)
    next_e = jnp.where(after < n_act[0],
                       jnp.minimum(jnp.sum((et_end[None, :] <= after[:, None]).astype(jnp.int32), axis=1),
                                   N_EXPERTS - 1), -1)
    return (tile_e.astype(jnp.int32), n_rows_tile.astype(jnp.int32), n_act, src.reshape(-1).astype(jnp.int32),
            used.astype(jnp.int32), first, group.astype(jnp.int32), next_e.astype(jnp.int32), n_tiles)


def _expert_kernel(tile_e, n_rows_t, n_act, src_t, used_t, first_t, group_t, next_t,
                   xs_hbm, wgu_hbm, bgu_ref, wdn_hbm, bdn_ref, ys_hbm,
                   xin, yout, wgu_f, wdn_f, wgu_s, wdn_s, zeros, sem_in, sem_out, sem_zero, sem_wgu, sem_wdn,
                   *, tm, tok_block, n_tok_tiles, layer):
    j = pl.program_id(0)
    na = n_act[0]
    chunks = tm // SUBLANES

    def weight_copies(e, wslot):
        return (pltpu.make_async_copy(wgu_hbm.at[layer, e], wgu_f.at[wslot], sem_wgu.at[wslot]),
                pltpu.make_async_copy(wdn_hbm.at[layer, e], wdn_f.at[wslot], sem_wdn.at[wslot]))

    def gather(src, dst, size, slot):
        return pltpu.make_async_copy(xs_hbm.at[pl.ds(src, size)], xin.at[slot, pl.ds(dst, size)], sem_in.at[slot])

    def scatter(src, dst, size, slot):
        return pltpu.make_async_copy(yout.at[slot, pl.ds(dst, size)], ys_hbm.at[pl.ds(src, size)], sem_out.at[slot])

    def start_chunks(jj, slot, copy):
        def one(c, priority):
            src = pl.multiple_of(src_t[jj * chunks + c], SUBLANES)
            copy(src, pl.multiple_of(c * SUBLANES, SUBLANES), SUBLANES, slot).start(priority=priority)

        def body(c, carry):
            one(c, 0)
            return carry

        def body_unrolled(g, carry):
            for u in range(SUBLANES):
                one(g * SUBLANES + u, u % 2)
            return carry

        n = lax.shift_right_logical(n_rows_t[jj], 3)

        @pl.when(n == chunks)
        def _():
            lax.fori_loop(0, chunks // SUBLANES, body_unrolled, 0)

        @pl.when(n != chunks)
        def _():
            lax.fori_loop(0, n, body, 0)

    def wait_rows(jj, slot, copy):
        n = n_rows_t[jj]
        size = tm
        while size >= SUBLANES:
            @pl.when((n & size) != 0)
            def _(size=size):
                copy(0, 0, size, slot).wait()
            size //= 2

    slot = j % 2

    @pl.when(j == 0)
    def _():
        xin[...] = jnp.zeros_like(xin)
        start_chunks(0, 0, gather)
        zeros[...] = jnp.zeros_like(zeros)

        def clear_tail(i, copy_op):
            used = used_t[i]
            tail = tok_block - used
            size = ZERO_ROWS
            sizes = []
            while size >= SUBLANES:
                sizes.append(size)
                size //= 2
            for size in sizes:
                @pl.when((tail & size) != 0)
                def _(size=size):
                    at = pl.multiple_of(i * tok_block + used + (tail & ~(2 * size - 1)), SUBLANES)
                    copy_op(pltpu.make_async_copy(zeros.at[pl.ds(0, size)], ys_hbm.at[pl.ds(at, size)], sem_zero))

        def start_clear(i, c):
            clear_tail(i, lambda cp: cp.start())
            return c

        def wait_clear(i, c):
            clear_tail(i, lambda cp: cp.wait())
            return c

        lax.fori_loop(0, n_tok_tiles, start_clear, 0)
        lax.fori_loop(0, n_tok_tiles, wait_clear, 0)

    @pl.when(j + 1 < na)
    def _():
        start_chunks(j + 1, 1 - slot, gather)

    @pl.when(j < na)
    def _():
        e = tile_e[j]

        @pl.when(first_t[j] == 1)
        def _():
            wslot = group_t[j] % 2

            @pl.when(j == 0)
            def _():
                for cp in weight_copies(e, wslot):
                    cp.start()

            for cp in weight_copies(e, wslot):
                cp.wait()
            wgu_s[...] = wgu_f[wslot].astype(BF16)
            wdn_s[...] = wdn_f[wslot].astype(BF16)

            @pl.when(next_t[j] >= 0)
            def _():
                for cp in weight_copies(next_t[j], 1 - wslot):
                    cp.start()

        wait_rows(j, slot, gather)

        def compute(rows):
            x = xin[slot, pl.ds(0, rows)]
            lane = lax.broadcasted_iota(jnp.int32, (rows, LANES), 1)
            p = jnp.sum(jnp.where(lane % N_EXPERTS == e, x[:, D_MODEL:], 0.0), axis=-1, keepdims=True)
            gu = (jnp.dot(x[:, :D_MODEL].astype(BF16), wgu_s[...], preferred_element_type=F32)
                  + bgu_ref[0, pl.ds(e, 1), :])
            f = gu.shape[1] // 2
            gate = jnp.minimum(gu[:, :f], SWIGLU_LIMIT)
            up = jnp.clip(gu[:, f:], -SWIGLU_LIMIT, SWIGLU_LIMIT)
            hid = (up + 1.0) * gate * jax.nn.sigmoid(SWIGLU_ALPHA * gate)
            y = (jnp.dot(hid.astype(BF16), wdn_s[...], preferred_element_type=F32)
                 + bdn_ref[0, pl.ds(e, 1), :])
            yout[slot, pl.ds(0, rows)] = y * p

        quarter = tm // 4
        for part in range(1, 5):
            @pl.when(jnp.logical_and(n_rows_t[j] > (part - 1) * quarter, n_rows_t[j] <= part * quarter))
            def _(part=part):
                compute(part * quarter)

        start_chunks(j, slot, scatter)

        @pl.when(j >= 1)
        def _():
            wait_rows(j - 1, 1 - slot, scatter)

        @pl.when(j == na - 1)
        def _():
            wait_rows(j, slot, scatter)


def _experts(xs, plan, layer, w_gu, b_gu, w_dn, b_dn):
    tile_e, n_rows_tile, n_act, src, used, first, group, next_e, n_tiles = plan
    tm = TM_EXPERT
    f2 = w_gu.shape[-1]

    in_specs = [pl.BlockSpec(memory_space=pl.ANY),
                pl.BlockSpec(memory_space=pl.ANY),
                pl.BlockSpec((1, N_EXPERTS, f2), lambda j, *_: (layer, 0, 0)),
                pl.BlockSpec(memory_space=pl.ANY),
                pl.BlockSpec((1, N_EXPERTS, D_MODEL), lambda j, *_: (layer, 0, 0))]
    grid_spec = pltpu.PrefetchScalarGridSpec(
        num_scalar_prefetch=8, grid=(n_tiles,), in_specs=in_specs,
        out_specs=pl.BlockSpec(memory_space=pl.ANY),
        scratch_shapes=[pltpu.VMEM((2, tm, XS_WIDTH), F32), pltpu.VMEM((2, tm, D_MODEL), F32),
                        pltpu.VMEM((2, D_MODEL, f2), F32), pltpu.VMEM((2, f2 // 2, D_MODEL), F32),
                        pltpu.VMEM((D_MODEL, f2), BF16), pltpu.VMEM((f2 // 2, D_MODEL), BF16),
                        pltpu.VMEM((ZERO_ROWS, D_MODEL), F32),
                        pltpu.SemaphoreType.DMA((2,)), pltpu.SemaphoreType.DMA((2,)), pltpu.SemaphoreType.DMA(()),
                        pltpu.SemaphoreType.DMA((2,)), pltpu.SemaphoreType.DMA((2,))])
    return pl.pallas_call(
        functools.partial(_expert_kernel, tm=tm, tok_block=TOK_BLOCK, n_tok_tiles=xs.shape[0] // TOK_BLOCK,
                          layer=layer),
        grid_spec=grid_spec,
        out_shape=jax.ShapeDtypeStruct((xs.shape[0], D_MODEL), F32),
        compiler_params=_params(1),
        name="moe_experts",
    )(tile_e, n_rows_tile, n_act, src, used, first, group, next_e,
      xs, w_gu, b_gu, w_dn, b_dn)


def _combine_kernel(ys_ref, pos_ref, h1_ref, g2_ref, l2g_ref, l2b_ref, o_ref):
    tm = pos_ref.shape[0]
    n_sorted = ys_ref.shape[0]
    pos = pos_ref[...]
    col = lax.broadcasted_iota(jnp.int32, (tm, n_sorted), 1).astype(F32)
    sel = jnp.zeros((tm, n_sorted), F32)
    for kk in range(TOP_K):
        sel = sel + (col == pos[:, kk:kk + 1]).astype(F32)
    sel = sel.astype(BF16)
    ys = ys_ref[...]
    hi = ys.astype(BF16)
    rest = ys - hi.astype(F32)
    mid = rest.astype(BF16)
    lo = (rest - mid.astype(F32)).astype(BF16)
    y2 = (jnp.dot(sel, hi, preferred_element_type=F32) + jnp.dot(sel, mid, preferred_element_type=F32)
          + jnp.dot(sel, lo, preferred_element_type=F32))
    o_ref[...] = _layer_norm(DEEPNORM_ALPHA * h1_ref[...] + g2_ref[0] * y2) * l2g_ref[...] + l2b_ref[...]


def _combine(ys, pos4, h1, mod3, mod_base, l2g, l2b, n_lat, seq):
    n_rows = h1.shape[0]
    tm = TM_POST
    tiles_per_seq = seq // tm
    n_lat_tiles = n_lat // tm
    n_groups_lat = n_lat // seq

    def group(i):
        return jnp.where(i < n_lat_tiles, i // tiles_per_seq, n_groups_lat)

    in_specs = [pl.BlockSpec((TOK_BLOCK, D_MODEL), lambda i: (i, 0)),
                pl.BlockSpec((tm, LANES), lambda i: (i, 0)),
                pl.BlockSpec((tm, D_MODEL), lambda i: (i, 0)),
                pl.BlockSpec((1, 1, D_MODEL), lambda i: (mod_base + group(i) * 6 + 5, 0, 0)),
                pl.BlockSpec((1, D_MODEL), lambda i: (0, 0)),
                pl.BlockSpec((1, D_MODEL), lambda i: (0, 0))]
    return pl.pallas_call(
        _combine_kernel,
        grid=(n_rows // tm,),
        in_specs=in_specs,
        out_specs=pl.BlockSpec((tm, D_MODEL), lambda i: (i, 0)),
        out_shape=jax.ShapeDtypeStruct((n_rows, D_MODEL), F32),
        compiler_params=_params(1),
        name="moe_combine",
    )(ys, pos4, h1, mod3, l2g.reshape(1, D_MODEL), l2b.reshape(1, D_MODEL))


def _block_diag(w):
    two, n, d, e = w.shape
    eye = jnp.eye(n, dtype=w.dtype)
    return (w[:, :, :, None, :] * eye[None, :, None, :, None]).reshape(two, n * d, n * e)


def kernel(x, c, ctx, c_ctx, w_mod, b_mod, w_in, b_in, na_rpb, w_proj_attn, w_proj_conv, w_proj_lru, sc_conv_w, lru_conv_w, lru_conv_b, lru_lambda, lru_w_r, lru_b_r, lru_w_i, lru_b_i, w_o, b_o, ln1_g, ln1_b, router_w, router_b, exp_w_gu, exp_b_gu, exp_w_dn, exp_b_dn, ln2_g, ln2_b):
    n_batch, seq, d = x.shape
    n_ctx = ctx.shape[1]
    n_lat = n_batch * seq
    n_all = n_lat + n_batch * n_ctx
    assert d == D_MODEL and n_batch + 1 <= SUBLANES

    cc = jnp.concatenate([c, c_ctx[None], jnp.zeros((SUBLANES - n_batch - 1, d), F32)], axis=0)
    mod = _modulation(cc, w_mod, b_mod)
    groups = n_batch + 1
    mod3 = mod.reshape(DEPTH, SUBLANES, 6, d)[:, :groups].reshape(DEPTH * groups * 6, 1, d)

    cos_t, sin_t = _make_rope(seq, TM_INPROJ)
    h = (x.reshape(n_lat, d), ctx.reshape(n_batch * n_ctx, d))

    for layer in range(DEPTH):
        last = layer == DEPTH - 1
        mod_base = layer * groups * 6
        w_in_bf = w_in[layer].astype(BF16)
        q, k, v, sb, rest = _input_projection(h, mod3, mod_base, w_in_bf, b_in[layer], N_EARLY,
                                              cos_t, sin_t, n_lat, seq)
        sp = jax.nn.softplus(-lru_lambda[layer])
        zb, hf, hb, att = _token_mixers(
            q, k, v, _attention_bias(na_rpb[layer], seq), sb, rest, sc_conv_w[layer], lru_conv_w[layer],
            lru_conv_b[layer], sp, _block_diag(lru_w_r[layer]).astype(BF16), _block_diag(lru_w_i[layer]).astype(BF16),
            lru_b_r[layer], lru_b_i[layer], n_batch, seq, n_ctx)
        n_rows = n_lat if last else n_all
        rw_full = jnp.pad(router_w[layer], ((0, 0), (0, LANES - N_EXPERTS)))
        rw_hi = rw_full.astype(BF16)
        rw_pad = jnp.concatenate([rw_hi, (rw_full - rw_hi.astype(F32)).astype(BF16)], axis=1)
        rb_pad = jnp.concatenate([router_b[layer], jnp.full((LANES - N_EXPERTS,), NEG_BIG, F32)]).reshape(1, LANES)
        h1, xs, pos4, cnt_out = _post_mixer(
            h, att, zb, hf, hb, rest, mod3, mod_base, w_in_bf, b_in[layer],
            w_proj_attn[layer].astype(BF16), w_proj_conv[layer].astype(BF16), w_proj_lru[layer].astype(BF16),
            w_o[layer].astype(BF16), b_o[layer], ln1_g[layer], ln1_b[layer], rw_pad, rb_pad, n_rows, n_lat, seq)
        plan = _expert_plan(cnt_out, n_rows // TM_POST, n_rows)
        ys = _experts(xs, plan, layer, exp_w_gu, exp_b_gu, exp_w_dn, exp_b_dn)
        h = (_combine(ys, pos4, h1, mod3, mod_base, ln2_g[layer], ln2_b[layer], n_lat, seq),)
    return h[0].reshape(n_batch, seq, d)
```

```python
import functools

import numpy as np
import jax
import jax.numpy as jnp
from jax import lax
from jax.experimental import pallas as pl
from jax.experimental.pallas import tpu as pltpu

D_MODEL = 1024
DEPTH = 2
GRID_W = 64
NA_HEADS = 8
NA_HEAD_DIM = 64
NA_WIN_ROWS = 8
NA_WIN_COLS = 16
ROPE_BASE = 10000.0
BRANCH_WIDTH = 512
LRU_C = 8.0
N_EARLY = 8 * BRANCH_WIDTH
N_EXPERTS = 32
TOP_K = 4
SWIGLU_LIMIT = 7.0
SWIGLU_ALPHA = 1.702
LN_EPS = 1e-5
DEEPNORM_ALPHA = (2 * DEPTH) ** 0.25
NEG_BIG = -1e30

LANES = 128
SUBLANES = 8
VMEM_LIMIT_BYTES = 56 * 1024 * 1024

TM_INPROJ = 512
SCAN_CHUNK = 256
ATT_ROWS = 4
ATT_QROWS = ATT_ROWS * GRID_W
TM_POST = 256
TM_EXPERT = 512
XS_WIDTH = D_MODEL + LANES
TOK_BLOCK = TM_POST * TOP_K + N_EXPERTS * SUBLANES
ZERO_ROWS = TOK_BLOCK - TM_POST * TOP_K

F32 = jnp.float32
BF16 = jnp.bfloat16


def _params(n_axes):
    return pltpu.CompilerParams(dimension_semantics=("arbitrary",) * n_axes,
                                vmem_limit_bytes=VMEM_LIMIT_BYTES)


def _layer_norm(x):
    mu = jnp.mean(x, axis=-1, keepdims=True)
    xc = x - mu
    var = jnp.mean(xc * xc, axis=-1, keepdims=True)
    return xc * lax.rsqrt(var + LN_EPS)


def _mod_kernel(c_ref, w_ref, b_ref, o_ref):
    c = c_ref[...]
    s = (c * jax.nn.sigmoid(c)).astype(BF16)
    o_ref[0] = jnp.dot(s, w_ref[0].astype(BF16), preferred_element_type=F32) + b_ref[0]


def _modulation(cc, w_mod, b_mod):
    n_out = w_mod.shape[-1]
    return pl.pallas_call(
        _mod_kernel,
        grid=(DEPTH, n_out // D_MODEL),
        in_specs=[pl.BlockSpec((SUBLANES, D_MODEL), lambda l, j: (0, 0)),
                  pl.BlockSpec((1, D_MODEL, D_MODEL), lambda l, j: (l, 0, j)),
                  pl.BlockSpec((1, 1, D_MODEL), lambda l, j: (l, 0, j))],
        out_specs=pl.BlockSpec((1, SUBLANES, D_MODEL), lambda l, j: (l, 0, j)),
        out_shape=jax.ShapeDtypeStruct((DEPTH, SUBLANES, n_out), F32),
        compiler_params=_params(2),
        name="modulation",
    )(cc, w_mod, b_mod.reshape(DEPTH, 1, n_out))


def _rope_half(x, cos, sin_signed):
    m = NA_HEAD_DIM // 4
    lane = lax.broadcasted_iota(jnp.int32, (x.shape[0], LANES), 1)
    first = (lane % (2 * m)) < m
    outs = []
    for cidx in range(x.shape[1] // LANES):
        xc = x[:, cidx * LANES:(cidx + 1) * LANES]
        partner = jnp.where(first, pltpu.roll(xc, LANES - m, 1), pltpu.roll(xc, m, 1))
        outs.append(xc * cos + partner * sin_signed)
    return jnp.concatenate(outs, axis=1)


def _inproj_kernel(ha_ref, hb_ref, sh_ref, sc_ref, w_ref, b_ref, cos_ref, sin_ref,
                   q_ref, k_ref, v_ref, sb_ref, rest_ref, *, split):
    h = jnp.where(pl.program_id(0) < split, ha_ref[...], hb_ref[...])
    xn = (_layer_norm(h) * (1.0 + sc_ref[0]) + sh_ref[0]).astype(BF16)
    half = BRANCH_WIDTH

    def columns(lo, width):
        return jnp.dot(xn, w_ref[0, :, lo:lo + width], preferred_element_type=F32) + b_ref[0, :, lo:lo + width]

    cos = cos_ref[...]
    sin = sin_ref[...]
    q_ref[...] = _rope_half(columns(0, half), cos, sin).astype(BF16)
    k_ref[...] = _rope_half(columns(half, half), cos, sin).astype(BF16)
    v_ref[...] = columns(2 * half, half).astype(BF16)
    sb_ref[...] = columns(3 * half, half)
    n_rest = rest_ref.shape[1]
    for lo in range(0, n_rest, 2 * half):
        rest_ref[:, lo:lo + 2 * half] = columns(4 * half + lo, 2 * half)


def _input_projection(h_parts, mod3, mod_base, layer, w_in_bf, b_in, n_cols, cos_t, sin_t, n_lat, seq):
    tm = TM_INPROJ
    ha, hb = h_parts[0], h_parts[-1]
    split = ha.shape[0] // tm
    m = ha.shape[0] + (hb.shape[0] if len(h_parts) == 2 else 0)
    n_lat_tiles = n_lat // tm
    tiles_per_seq = seq // tm
    n_groups_lat = n_lat // seq

    def group(i):
        return jnp.where(i < n_lat_tiles, i // tiles_per_seq, n_groups_lat)

    def rope_blk(i):
        return jnp.where(i < n_lat_tiles, i % tiles_per_seq, tiles_per_seq)

    half = BRANCH_WIDTH
    return pl.pallas_call(
        functools.partial(_inproj_kernel, split=split),
        grid=(m // tm,),
        in_specs=[pl.BlockSpec((tm, D_MODEL), lambda i: (jnp.minimum(i, split - 1), 0)),
                  pl.BlockSpec((tm, D_MODEL), lambda i: (jnp.maximum(i - split, 0), 0)),
                  pl.BlockSpec((1, 1, D_MODEL), lambda i: (mod_base + group(i) * 6 + 0, 0, 0)),
                  pl.BlockSpec((1, 1, D_MODEL), lambda i: (mod_base + group(i) * 6 + 1, 0, 0)),
                  pl.BlockSpec((1, D_MODEL, n_cols), lambda i: (layer, 0, 0)),
                  pl.BlockSpec((1, 1, n_cols), lambda i: (layer, 0, 0)),
                  pl.BlockSpec((tm, LANES), lambda i: (rope_blk(i), 0)),
                  pl.BlockSpec((tm, LANES), lambda i: (rope_blk(i), 0))],
        out_specs=[pl.BlockSpec((tm, half), lambda i: (i, 0)),
                   pl.BlockSpec((tm, half), lambda i: (i, 0)),
                   pl.BlockSpec((tm, half), lambda i: (i, 0)),
                   pl.BlockSpec((tm, half), lambda i: (i, 0)),
                   pl.BlockSpec((tm, n_cols - 4 * half), lambda i: (i, 0))],
        out_shape=[jax.ShapeDtypeStruct((m, half), BF16),
                   jax.ShapeDtypeStruct((m, half), BF16),
                   jax.ShapeDtypeStruct((m, half), BF16),
                   jax.ShapeDtypeStruct((m, half), F32),
                   jax.ShapeDtypeStruct((m, n_cols - 4 * half), F32)],
        compiler_params=_params(1),
        name="input_projection",
    )(ha, hb, mod3, mod3, w_in_bf, b_in.reshape(DEPTH, 1, -1), cos_t, sin_t)


def _make_rope(seq, tm):
    t = np.arange(seq)
    row_pos, col_pos = t // GRID_W, t % GRID_W
    d = np.arange(LANES) % NA_HEAD_DIM
    m = NA_HEAD_DIM // 4
    inv_freq = (ROPE_BASE ** (-jnp.arange(m, dtype=F32) / m))[d % m]
    pos = np.where((d < 2 * m)[None, :], row_pos[:, None], col_pos[:, None])
    ang = jnp.asarray(pos).astype(F32) * inv_freq[None, :]
    cos = jnp.cos(ang)
    sin = jnp.sin(ang)
    sin_signed = jnp.where(jnp.asarray((d % (2 * m)) < m)[None, :], -sin, sin)
    cos = jnp.concatenate([cos, jnp.ones((tm, LANES), F32)], axis=0)
    sin_signed = jnp.concatenate([sin_signed, jnp.zeros((tm, LANES), F32)], axis=0)
    return cos, sin_signed


def _mixer_kernel(fblk, bblk, first, last, seqb, r0t, cls,
                  sb_ref, xf_ref, xfp_ref, xfn_ref, lxb_ref, lxbp_ref, lxbn_ref,
                  scw_ref, cw_ref, cb_ref, sp_ref, wr_ref, wi_ref, br_ref, bi_ref,
                  q_ref, k_ref, v_ref, kc_ref, vc_ref, bias0_ref, bias1_ref, bias2_ref, bias3_ref,
                  zb_ref, hf_ref, hb_ref, att_ref,
                  a_s, b_s, hc_s, st_s, *, ch, n_ctx_items, band):
    it = pl.program_id(0)
    is_first = first[it] == 1
    is_last = last[it] == 1
    is_ctx = it < n_ctx_items
    b = seqb[it]
    width = BRANCH_WIDTH
    row = lax.broadcasted_iota(jnp.int32, (ch, width), 0)
    not_first = jnp.where(is_first, 0.0, 1.0).astype(F32)
    not_last = jnp.where(is_last, 0.0, 1.0).astype(F32)

    def back1(u, prev_row):
        return jnp.where(row == 0, prev_row, pltpu.roll(u, 1, 0))

    def back2(u, prev2, prev1):
        return jnp.where(row == 0, prev2, jnp.where(row == 1, prev1, pltpu.roll(u, 2, 0)))

    def fwd1(u, next_row):
        return jnp.where(row == ch - 1, next_row, pltpu.roll(u, ch - 1, 0))

    gate_cols, value_cols, lru_cols = (slice(s * width, (s + 1) * width) for s in range(3))

    def lru_input(x, p, n, prev_ok, next_ok):
        p = p * prev_ok
        n = n * next_ok
        return (cw_ref[0:1] * back2(x, p[6:7], p[7:8]) + cw_ref[1:2] * back1(x, p[7:8])
                + cw_ref[2:3] * x + cw_ref[3:4] * fwd1(x, n[0:1]) + cb_ref[...])

    def coeffs(d, xm):
        xb = xm.astype(BF16)
        r = jax.nn.sigmoid(jnp.dot(xb, wr_ref[d], preferred_element_type=F32) + br_ref[d:d + 1])
        g = jax.nn.sigmoid(jnp.dot(xb, wi_ref[d], preferred_element_type=F32) + bi_ref[d:d + 1])
        log_a = (-LRU_C * sp_ref[d:d + 1]) * r
        a = jnp.exp(log_a)
        a_s[d] = a
        b_s[d] = jnp.sqrt(-jnp.tanh(log_a) * (a * a + 1.0)) * (g * xm)

    @pl.when(jnp.logical_and(is_first, is_ctx))
    def _():
        hc_s[...] = jnp.zeros_like(hc_s)

    @pl.when(jnp.logical_and(is_first, jnp.logical_not(is_ctx)))
    def _():
        hc_s[0:1] = st_s[pl.ds(2 * b, 1), :]
        hc_s[1:2] = st_s[pl.ds(2 * b + 1, 1), :]

    coeffs(0, lru_input(xf_ref[:, lru_cols], xfp_ref[:, lru_cols], xfn_ref[:, lru_cols], not_first, not_last))
    hf = hc_s[0:1]
    for t in range(ch):
        hf = a_s[0, t:t + 1, :] * hf + b_s[0, t:t + 1, :]
        hf_ref[t:t + 1, :] = hf
    hc_s[0:1] = hf

    coeffs(1, lru_input(lxb_ref[...], lxbp_ref[...], lxbn_ref[...], not_last, not_first))

    u = xf_ref[:, gate_cols] * xf_ref[:, value_cols]
    u_prev = xfp_ref[7:8, gate_cols] * xfp_ref[7:8, value_cols] * not_first
    u_next = xfn_ref[0:1, gate_cols] * xfn_ref[0:1, value_cols] * not_last
    conv = scw_ref[0:1] * back1(u, u_prev) + scw_ref[1:2] * u + scw_ref[2:3] * fwd1(u, u_next)
    zb_ref[...] = (sb_ref[...] * conv).astype(BF16)

    hb = hc_s[1:2]
    for t in range(ch - 1, -1, -1):
        hb = a_s[1, t:t + 1, :] * hb + b_s[1, t:t + 1, :]
        hb_ref[t:t + 1, :] = hb
    hc_s[1:2] = hb

    _attention_item(it, r0t, cls, q_ref, k_ref, v_ref, kc_ref, vc_ref,
                    (bias0_ref, bias1_ref, bias2_ref, bias3_ref), att_ref, band)

    @pl.when(is_ctx)
    def _():
        st_s[pl.ds(2 * b, 1), :] = hf
        st_s[pl.ds(2 * b + 1, 1), :] = hb


def _mixer_tables(n_batch, seq, n_ctx, ch):
    assert n_ctx == ch and ch == ATT_QROWS
    nc = seq // ch
    rows = seq // GRID_W
    kr = min(NA_WIN_ROWS, rows)
    ctx0 = n_batch * seq // ch
    fblk, bblk, first, last, seqb, r0t, cls = [], [], [], [], [], [], []
    for b in range(n_batch):
        fblk.append(ctx0 + b); bblk.append(ctx0 + b); first.append(1); last.append(1); seqb.append(b)
        r0t.extend([0] * ATT_ROWS); cls.extend([kr] * ATT_ROWS)
    for b in range(n_batch):
        for c in range(nc):
            fblk.append(b * nc + c); bblk.append(b * nc + nc - 1 - c)
            first.append(int(c == 0)); last.append(int(c == nc - 1)); seqb.append(b)
            for r in range(c * ATT_ROWS, (c + 1) * ATT_ROWS):
                r0 = min(max(r - kr // 2, 0), rows - kr)
                r0t.append(r0); cls.append(r - r0)
    return [np.asarray(a, np.int32) for a in (fblk, bblk, first, last, seqb, r0t, cls)]


def _token_mixers(q, k, v, bias, sb, rest, sc_w, lru_cw, lru_cb, sp, wr_bd, wi_bd, b_r, b_i, n_batch, seq, n_ctx):
    m = sb.shape[0]
    ch = SCAN_CHUNK
    width = BRANCH_WIDTH
    tables = _mixer_tables(n_batch, seq, n_ctx, ch)
    n_items = len(tables[0])
    halo_per_chunk = ch // SUBLANES
    last_halo = m // SUBLANES - 1
    band = min(NA_WIN_ROWS, seq // GRID_W) * GRID_W
    ctx_blk0 = n_batch * seq // n_ctx

    def cur(col, which, n=1):
        return pl.BlockSpec((ch, n * width), lambda i, f, bk, *_: ((f, bk)[which][i], col))

    def prev(col, which, n=1):
        return pl.BlockSpec((SUBLANES, n * width),
                            lambda i, f, bk, *_: (jnp.maximum((f, bk)[which][i] * halo_per_chunk - 1, 0), col))

    def nxt(col, which, n=1):
        return pl.BlockSpec((SUBLANES, n * width),
                            lambda i, f, bk, *_: (jnp.minimum(((f, bk)[which][i] + 1) * halo_per_chunk, last_halo), col))

    def full(shape):
        return pl.BlockSpec(shape, lambda i, *_: (0,) * len(shape))

    in_specs = [cur(0, 0), cur(0, 0, 3), prev(0, 0, 3), nxt(0, 0, 3),
                cur(2, 1), prev(2, 1), nxt(2, 1),
                full(sc_w.shape), full(lru_cw.shape), full((1, width)), full(sp.shape),
                full(wr_bd.shape), full(wi_bd.shape), full(b_r.shape), full(b_i.shape)]
    in_specs += [cur(0, 0),
                 pl.BlockSpec((seq, width), lambda i, f, bk, fi, la, sq, *_: (sq[i], 0)),
                 pl.BlockSpec((seq, width), lambda i, f, bk, fi, la, sq, *_: (sq[i], 0)),
                 pl.BlockSpec((n_ctx, width), lambda i, f, bk, fi, la, sq, *_: (ctx_blk0 + sq[i], 0)),
                 pl.BlockSpec((n_ctx, width), lambda i, f, bk, fi, la, sq, *_: (ctx_blk0 + sq[i], 0))]
    for row in range(ATT_ROWS):
        in_specs.append(pl.BlockSpec((1, NA_HEADS, GRID_W, band),
                                     lambda i, f, bk, fi, la, sq, r0, cl, row=row: (cl[i * ATT_ROWS + row], 0, 0, 0)))
    out_specs = [cur(0, 0), cur(0, 0), cur(0, 1), cur(0, 0)]
    grid_spec = pltpu.PrefetchScalarGridSpec(
        num_scalar_prefetch=7, grid=(n_items,), in_specs=in_specs, out_specs=out_specs,
        scratch_shapes=[pltpu.VMEM((2, ch, width), F32), pltpu.VMEM((2, ch, width), F32),
                        pltpu.VMEM((SUBLANES, width), F32), pltpu.VMEM((2 * n_batch, width), F32)])
    return pl.pallas_call(
        functools.partial(_mixer_kernel, ch=ch, n_ctx_items=n_batch, band=band),
        grid_spec=grid_spec,
        out_shape=[jax.ShapeDtypeStruct((m, width), BF16),
                   jax.ShapeDtypeStruct((m, width), F32),
                   jax.ShapeDtypeStruct((m, width), F32),
                   jax.ShapeDtypeStruct((m, width), BF16)],
        compiler_params=_params(1),
        name="token_mixers",
    )(*[jnp.asarray(t) for t in tables],
      sb, rest, rest, rest, rest, rest, rest,
      sc_w, lru_cw, lru_cb.reshape(1, width), sp, wr_bd, wi_bd, b_r, b_i,
      q, k, v, k, v, bias, bias, bias, bias)


def _attention_item(it, r0t, cls, q_ref, k_ref, v_ref, kc_ref, vc_ref, bias_refs, o_ref, band):
    nq = GRID_W
    lane = lax.broadcasted_iota(jnp.int32, (nq, LANES), 1)
    low = lane < NA_HEAD_DIM
    scale = NA_HEAD_DIM ** -0.5
    nt = (((1,), (1,)), ((), ()))
    for row in range(ATT_ROWS):
        start = pl.multiple_of(r0t[it * ATT_ROWS + row] * GRID_W, GRID_W)
        bias_ref = bias_refs[row]
        qrows = slice(row * nq, (row + 1) * nq)
        for hp in range(NA_HEADS // 2):
            cols = slice(hp * LANES, (hp + 1) * LANES)
            qp = q_ref[qrows, cols].astype(F32) * scale
            qs = jnp.concatenate([jnp.where(low, qp, 0.0), jnp.where(low, 0.0, qp)], axis=0).astype(BF16)
            kb = k_ref[pl.ds(start, band), cols]
            vb = v_ref[pl.ds(start, band), cols]
            s_loc = lax.dot_general(qs, kb, nt, preferred_element_type=F32)
            s_ctx = lax.dot_general(qs, kc_ref[:, cols], nt, preferred_element_type=F32)
            bias = jnp.concatenate([bias_ref[0, 2 * hp], bias_ref[0, 2 * hp + 1]], axis=0)
            s_loc = s_loc + bias
            mx =jnp.maximum(jnp.max(s_loc, axis=-1, keepdims=True), jnp.max(s_ctx, axis=-1, keepdims=True))
            e_loc = jnp.exp(s_loc - mx)
            e_ctx = jnp.exp(s_ctx - mx)
            den = jnp.sum(e_loc, axis=-1, keepdims=True) + jnp.sum(e_ctx, axis=-1, keepdims=True)
            o = (jnp.dot(e_loc.astype(BF16), vb, preferred_element_type=F32)
                 + jnp.dot(e_ctx.astype(BF16), vc_ref[:, cols], preferred_element_type=F32)) / den
            o_ref[qrows, cols] = jnp.where(low, o[:nq], o[nq:]).astype(BF16)


def _attention_bias(rpb, seq):
    rows = seq // GRID_W
    kr = min(NA_WIN_ROWS, rows)
    kc = NA_WIN_COLS
    cq = np.arange(GRID_W)
    c0 = np.clip(cq - kc // 2, 0, GRID_W - kc)
    ck = np.arange(GRID_W)
    inside = (ck[None, :] >= c0[:, None]) & (ck[None, :] < c0[:, None] + kc)
    dc = np.clip(ck[None, :] - cq[:, None] + (NA_WIN_COLS - 1), 0, 2 * NA_WIN_COLS - 2)
    n_dr = 2 * NA_WIN_ROWS - 1
    n_dc = 2 * NA_WIN_COLS - 1
    pick = jnp.asarray((np.arange(n_dc)[:, None] == dc.reshape(1, -1)).astype(np.float32))
    picked = jnp.dot(rpb.reshape(-1, n_dc), pick, precision=lax.Precision.HIGHEST)
    picked = picked.reshape(NA_HEADS, 2 * NA_WIN_ROWS - 1, GRID_W, GRID_W)
    table = jnp.where(jnp.asarray(inside)[None, None], picked, NEG_BIG)
    table = table.transpose(0, 2, 1, 3).reshape(NA_HEADS, GRID_W, n_dr * GRID_W)
    classes = []
    for cl in range(kr):
        lo = (NA_WIN_ROWS - 1 - cl) * GRID_W
        classes.append(table[:, :, lo:lo + kr * GRID_W])
    classes.append(jnp.full((NA_HEADS, GRID_W, kr * GRID_W), NEG_BIG, F32))
    return jnp.stack(classes, axis=0)


def _post_kernel(ha_ref, hc_ref, att_ref, zb_ref, hf_ref, hb_ref, lg_ref,
                 sh1_ref, sc1_ref, g1_ref, sh2_ref, sc2_ref,
                 wgl0_ref, wgl1_ref, wgl2_ref, bgl0_ref, bgl1_ref, bgl2_ref,
                 wpa_ref, wpc_ref, wpl_ref, wo_ref, bo_ref, l1g_ref, l1b_ref,
                 rw_ref, rb_ref, tri_ref, upper_ref,
                 h1_ref, xs_ref, pos_ref, cnt_ref, *, split):
    h = jnp.where(pl.program_id(0) < split, ha_ref[...], hc_ref[...])
    u1 = (_layer_norm(h) * (1.0 + sc1_ref[0]) + sh1_ref[0]).astype(BF16)
    y_a = jnp.dot(att_ref[...], wpa_ref[...], preferred_element_type=F32)
    y_b = jnp.dot(zb_ref[...], wpc_ref[...], preferred_element_type=F32)
    zc = jax.nn.gelu(lg_ref[...]) * (hf_ref[...] + hb_ref[...])
    y_c = jnp.dot(zc.astype(BF16), wpl_ref[...], preferred_element_type=F32)
    merged = (jax.nn.sigmoid(jnp.dot(u1, wgl0_ref[0], preferred_element_type=F32) + bgl0_ref[0]) * y_a
              + jax.nn.sigmoid(jnp.dot(u1, wgl1_ref[0], preferred_element_type=F32) + bgl1_ref[0]) * y_b
              + jax.nn.sigmoid(jnp.dot(u1, wgl2_ref[0], preferred_element_type=F32) + bgl2_ref[0]) * y_c)
    y = jnp.dot(merged.astype(BF16), wo_ref[...], preferred_element_type=F32) + bo_ref[...]
    h1 = _layer_norm(DEEPNORM_ALPHA * h + g1_ref[0] * y) * l1g_ref[...] + l1b_ref[...]
    h1_ref[...] = h1
    u2 = _layer_norm(h1) * (1.0 + sc2_ref[0]) + sh2_ref[0]

    u_hi = u2.astype(BF16)
    u_lo = (u2 - u_hi.astype(F32)).astype(BF16)
    by_hi = jnp.dot(u_hi, rw_ref[...], preferred_element_type=F32)
    logits = (by_hi[:, :LANES] + by_hi[:, LANES:]
              + jnp.dot(u_lo, rw_ref[:, :LANES], preferred_element_type=F32) + rb_ref[...])
    tm = logits.shape[0]
    lane = lax.broadcasted_iota(jnp.int32, (tm, LANES), 1)
    lane_f = lane.astype(F32)
    work = logits
    tops, idxs, hots = [], [], []
    for _ in range(TOP_K):
        mx = jnp.max(work, axis=-1, keepdims=True)
        idx = jnp.min(jnp.where(work == mx, lane_f, float(LANES)), axis=-1, keepdims=True)
        hot = lane_f == idx
        work = jnp.where(hot, -3e38, work)
        tops.append(mx); idxs.append(idx); hots.append(hot)
    exps = [jnp.exp(t - tops[0]) for t in tops]
    den = exps[0] + exps[1] + exps[2] + exps[3]
    hot_all = jnp.zeros((tm, LANES), F32)
    for hot in hots:
        hot_all = hot_all + hot.astype(F32)
    cnt = jnp.sum(hot_all, axis=0, keepdims=True)
    cnt_pad = jnp.floor((cnt + (SUBLANES - 1.0)) * (1.0 / SUBLANES)) * SUBLANES
    off = jnp.dot(jnp.broadcast_to(cnt_pad, (SUBLANES, LANES)), upper_ref[...],
                  precision=lax.Precision.HIGHEST, preferred_element_type=F32)[0:1]
    slot = off + jnp.dot(tri_ref[...], hot_all.astype(BF16), preferred_element_type=F32)
    pos4 = jnp.zeros((tm, LANES), F32)
    w_tile = jnp.zeros((tm, LANES), F32)
    for kk in range(TOP_K):
        pos_k = jnp.sum(jnp.where(hots[kk], slot, 0.0), axis=-1, keepdims=True)
        pos4 = jnp.where(lane == kk, pos_k, pos4)
        p = exps[kk] / den
        p_hi = p.astype(BF16).astype(F32)
        p_mid = (p - p_hi).astype(BF16).astype(F32)
        p_lo = p - p_hi - p_mid
        w_tile = jnp.where(hots[kk], p_hi, w_tile)
        w_tile = jnp.where(lane_f == idxs[kk] + float(N_EXPERTS), p_mid, w_tile)
        w_tile = jnp.where(lane_f == idxs[kk] + float(2 * N_EXPERTS), p_lo, w_tile)
    pos_ref[...] = pos4
    sub = lax.broadcasted_iota(jnp.int32, (SUBLANES, LANES), 0)
    cnt_ref[...] = jnp.where(sub == 0, cnt, jnp.where(sub == 1, off, 0.0))

    n_sorted = xs_ref.shape[0]
    pos_t = pos4.T
    r_iota = lax.broadcasted_iota(jnp.int32, (n_sorted, tm), 0).astype(F32)
    hit = r_iota == pos_t[0:1, :]
    for kk in range(1, TOP_K):
        hit = jnp.logical_or(hit, r_iota == pos_t[kk:kk + 1, :])
    perm = jnp.where(hit, 1.0, 0.0).astype(BF16)
    feats = jnp.concatenate([u_hi, w_tile.astype(BF16)], axis=1)
    xs_ref[...] = jnp.dot(perm, feats, preferred_element_type=F32)


def _post_mixer(h_parts, att, zb, hf, hb, rest, mod3, mod_base, layer, w_in_bf, b_in, wpa, wpc, wpl, wo, b_o, l1g,
                l1b, rw_pad, rb_pad, n_rows, n_lat, seq):
    tm = TM_POST
    width = BRANCH_WIDTH
    ha, hc = h_parts[0], h_parts[-1]
    split = min(ha.shape[0], n_rows) // tm
    gate_col0 = N_EARLY // D_MODEL
    tiles_per_seq = seq // tm
    n_lat_tiles = n_lat // tm
    n_groups_lat = n_lat // seq
    tri = jnp.asarray(np.tril(np.ones((tm, tm), np.float32), -1), BF16)
    upper = jnp.asarray(np.triu(np.ones((LANES, LANES), np.float32), 1))
    n_tiles = n_rows // tm

    def group(i):
        return jnp.where(i < n_lat_tiles, i // tiles_per_seq, n_groups_lat)

    def rows(wd, col=0):
        return pl.BlockSpec((tm, wd), lambda i: (i, col))

    def full(shape):
        return pl.BlockSpec(shape, lambda i: (0,) * len(shape))

    def mod(which):
        return pl.BlockSpec((1, 1, D_MODEL), lambda i: (mod_base + group(i) * 6 + which, 0, 0))

    in_specs = [pl.BlockSpec((tm, D_MODEL), lambda i: (jnp.minimum(i, split - 1), 0)),
                pl.BlockSpec((tm, D_MODEL), lambda i: (jnp.maximum(i - split, 0), 0)),
                rows(width), rows(width), rows(width), rows(width), rows(width, 3),
                mod(0), mod(1), mod(2), mod(3), mod(4)]
    in_specs += [pl.BlockSpec((1, D_MODEL, D_MODEL), lambda i, c=c: (layer, 0, gate_col0 + c)) for c in range(3)]
    in_specs += [pl.BlockSpec((1, 1, D_MODEL), lambda i, c=c: (layer, 0, gate_col0 + c)) for c in range(3)]
    in_specs += [full(wpa.shape), full(wpc.shape), full(wpl.shape), full(wo.shape), full((1, D_MODEL)),
                full((1, D_MODEL)), full((1, D_MODEL)), full(rw_pad.shape), full(rb_pad.shape), full(tri.shape),
                full(upper.shape)]
    out_specs = [rows(D_MODEL), pl.BlockSpec((TOK_BLOCK, XS_WIDTH), lambda i: (i, 0)), rows(LANES),
                 pl.BlockSpec((SUBLANES, LANES), lambda i: (i, 0))]
    return pl.pallas_call(
        functools.partial(_post_kernel, split=split),
        grid=(n_tiles,),
        in_specs=in_specs, out_specs=out_specs,
        out_shape=[jax.ShapeDtypeStruct((n_rows, D_MODEL), F32),
                   jax.ShapeDtypeStruct((n_tiles * TOK_BLOCK, XS_WIDTH), F32),
                   jax.ShapeDtypeStruct((n_rows, LANES), F32),
                   jax.ShapeDtypeStruct((n_tiles * SUBLANES, LANES), F32)],
        compiler_params=_params(1),
        name="post_mixer",
    )(ha, hc, att, zb, hf, hb, rest, mod3, mod3, mod3, mod3, mod3,
      w_in_bf, w_in_bf, w_in_bf, b_in.reshape(DEPTH, 1, -1), b_in.reshape(DEPTH, 1, -1), b_in.reshape(DEPTH, 1, -1),
      wpa, wpc, wpl, wo, b_o.reshape(1, D_MODEL), l1g.reshape(1, D_MODEL), l1b.reshape(1, D_MODEL),
      rw_pad, rb_pad, tri, upper)


def _expert_plan(cnt_out, n_tok_tiles, n_rows):
    tm = TM_EXPERT
    co = cnt_out.reshape(n_tok_tiles, SUBLANES, LANES)
    cnt = (co[:, 0, :N_EXPERTS].astype(jnp.int32) + SUBLANES - 1) // SUBLANES * SUBLANES
    off = co[:, 1, :N_EXPERTS].astype(jnp.int32)
    cum_end = jnp.cumsum(cnt, axis=0)
    cum = cum_end - cnt
    total = cum_end[-1]
    n_et = (total + tm - 1) // tm
    et_end = jnp.cumsum(n_et)
    n_act = et_end[-1:].astype(jnp.int32)
    n_tiles = -(-n_tok_tiles * TOK_BLOCK // tm) + N_EXPERTS
    j = jnp.arange(n_tiles, dtype=jnp.int32)
    tile_e = jnp.minimum(jnp.sum((et_end[None, :] <= j[:, None]).astype(jnp.int32), axis=1), N_EXPERTS - 1)
    pick_e = (tile_e[:, None] == jnp.arange(N_EXPERTS, dtype=jnp.int32)[None, :]).astype(F32)

    def per_tile(table):
        return jnp.dot(pick_e, table.astype(F32), precision=lax.Precision.HIGHEST).astype(jnp.int32)

    row0 = (j - per_tile(et_end - n_et)) * tm
    n_rows_tile = jnp.clip(per_tile(total) - row0, 0, tm)
    cum_e = per_tile(cum.T)
    cum_end_e = per_tile(cum_end.T)
    delta_e = per_tile((jnp.arange(n_tok_tiles, dtype=jnp.int32)[:, None] * TOK_BLOCK + off - cum).T)
    q = row0[:, None] + SUBLANES * jnp.arange(tm // SUBLANES, dtype=jnp.int32)[None, :]
    inside = jnp.logical_and(cum_e.T[:, :, None] <= q[None], q[None] < cum_end_e.T[:, :, None])
    src = q + jnp.sum(jnp.where(inside, delta_e.T[:, :, None], 0), axis=0)
    used = off[:, N_EXPERTS - 1] + cnt[:, N_EXPERTS - 1]
    first = jnp.concatenate([jnp.ones((1,), jnp.int32), (tile_e[1:] != tile_e[:-1]).astype(jnp.int32)])
    group = jnp.cumsum(first) - 1
    after = per_tile(et_end)
    next_e = jnp.where(after < n_act[0],
                       jnp.minimum(jnp.sum((et_end[None, :] <= after[:, None]).astype(jnp.int32), axis=1),
                                   N_EXPERTS - 1), -1)
    return (tile_e.astype(jnp.int32), n_rows_tile.astype(jnp.int32), n_act, src.reshape(-1).astype(jnp.int32),
            used.astype(jnp.int32), first, group.astype(jnp.int32), next_e.astype(jnp.int32), n_tiles)


def _expert_kernel(tile_e, n_rows_t, n_act, src_t, used_t, first_t, group_t, next_t,
                   xs_hbm, wgu_hbm, bgu_ref, wdn_hbm, bdn_ref, ys_hbm,
                   xin, yout, wgu_f, wdn_f, wgu_s, wdn_s, zeros, sem_in, sem_out, sem_zero, sem_wgu, sem_wdn,
                   *, tm, tok_block, n_tok_tiles, layer):
    j = pl.program_id(0)
    na = n_act[0]
    chunks = tm // SUBLANES

    def weight_copies(e, wslot):
        return (pltpu.make_async_copy(wgu_hbm.at[layer, e], wgu_f.at[wslot], sem_wgu.at[wslot]),
                pltpu.make_async_copy(wdn_hbm.at[layer, e], wdn_f.at[wslot], sem_wdn.at[wslot]))

    def gather(src, dst, size, slot):
        return pltpu.make_async_copy(xs_hbm.at[pl.ds(src, size)], xin.at[slot, pl.ds(dst, size)], sem_in.at[slot])

    def scatter(src, dst, size, slot):
        return pltpu.make_async_copy(yout.at[slot, pl.ds(dst, size)], ys_hbm.at[pl.ds(src, size)], sem_out.at[slot])

    def start_chunks(jj, slot, copy):
        def one(c, priority):
            src = pl.multiple_of(src_t[jj * chunks + c], SUBLANES)
            copy(src, pl.multiple_of(c * SUBLANES, SUBLANES), SUBLANES, slot).start(priority=priority)

        def body(c, carry):
            one(c, 0)
            return carry

        def body_unrolled(g, carry):
            for u in range(SUBLANES):
                one(g * SUBLANES + u, u % 2)
            return carry

        n = lax.shift_right_logical(n_rows_t[jj], 3)

        @pl.when(n == chunks)
        def _():
            lax.fori_loop(0, chunks // SUBLANES, body_unrolled, 0)

        @pl.when(n != chunks)
        def _():
            lax.fori_loop(0, n, body, 0)

    def wait_rows(jj, slot, copy):
        n = n_rows_t[jj]
        size = tm
        while size >= SUBLANES:
            @pl.when((n & size) != 0)
            def _(size=size):
                copy(0, 0, size, slot).wait()
            size //= 2

    slot = j % 2

    @pl.when(j == 0)
    def _():
        xin[...] = jnp.zeros_like(xin)
        start_chunks(0, 0, gather)
        zeros[...] = jnp.zeros_like(zeros)

        def clear_tail(i, copy_op):
            used = used_t[i]
            tail = tok_block - used
            size = ZERO_ROWS
            sizes = []
            while size >= SUBLANES:
                sizes.append(size)
                size //= 2
            for size in sizes:
                @pl.when((tail & size) != 0)
                def _(size=size):
                    at = pl.multiple_of(i * tok_block + used + (tail & ~(2 * size - 1)), SUBLANES)
                    copy_op(pltpu.make_async_copy(zeros.at[pl.ds(0, size)], ys_hbm.at[pl.ds(at, size)], sem_zero))

        def start_clear(i, c):
            clear_tail(i, lambda cp: cp.start())
            return c

        def wait_clear(i, c):
            clear_tail(i, lambda cp: cp.wait())
            return c

        lax.fori_loop(0, n_tok_tiles, start_clear, 0)
        lax.fori_loop(0, n_tok_tiles, wait_clear, 0)

    @pl.when(j + 1 < na)
    def _():
        start_chunks(j + 1, 1 - slot, gather)

    @pl.when(j < na)
    def _():
        e = tile_e[j]

        @pl.when(first_t[j] == 1)
        def _():
            wslot = group_t[j] % 2

            @pl.when(j == 0)
            def _():
                for cp in weight_copies(e, wslot):
                    cp.start()

            for cp in weight_copies(e, wslot):
                cp.wait()
            wgu_s[...] = wgu_f[wslot].astype(BF16)
            wdn_s[...] = wdn_f[wslot].astype(BF16)

            @pl.when(next_t[j] >= 0)
            def _():
                for cp in weight_copies(next_t[j], 1 - wslot):
                    cp.start()

        wait_rows(j, slot, gather)

        def compute(rows):
            x = xin[slot, pl.ds(0, rows)]
            lane = lax.broadcasted_iota(jnp.int32, (rows, LANES), 1)
            p = jnp.sum(jnp.where(lane % N_EXPERTS == e, x[:, D_MODEL:], 0.0), axis=-1, keepdims=True)
            gu = (jnp.dot(x[:, :D_MODEL].astype(BF16), wgu_s[...], preferred_element_type=F32)
                  + bgu_ref[0, pl.ds(e, 1), :])
            f = gu.shape[1] // 2
            gate = jnp.minimum(gu[:, :f], SWIGLU_LIMIT)
            up = jnp.clip(gu[:, f:], -SWIGLU_LIMIT, SWIGLU_LIMIT)
            hid = (up + 1.0) * gate * jax.nn.sigmoid(SWIGLU_ALPHA * gate)
            y = (jnp.dot(hid.astype(BF16), wdn_s[...], preferred_element_type=F32)
                 + bdn_ref[0, pl.ds(e, 1), :])
            yout[slot, pl.ds(0, rows)] = y * p

        quarter = tm // 4
        for part in range(1, 5):
            @pl.when(jnp.logical_and(n_rows_t[j] > (part - 1) * quarter, n_rows_t[j] <= part * quarter))
            def _(part=part):
                compute(part * quarter)

        start_chunks(j, slot, scatter)

        @pl.when(j >= 1)
        def _():
            wait_rows(j - 1, 1 - slot, scatter)

        @pl.when(j == na - 1)
        def _():
            wait_rows(j, slot, scatter)


def _experts(xs, plan, layer, w_gu, b_gu, w_dn, b_dn):
    tile_e, n_rows_tile, n_act, src, used, first, group, next_e, n_tiles = plan
    tm = TM_EXPERT
    f2 = w_gu.shape[-1]

    in_specs = [pl.BlockSpec(memory_space=pl.ANY),
                pl.BlockSpec(memory_space=pl.ANY),
                pl.BlockSpec((1, N_EXPERTS, f2), lambda j, *_: (layer, 0, 0)),
                pl.BlockSpec(memory_space=pl.ANY),
                pl.BlockSpec((1, N_EXPERTS, D_MODEL), lambda j, *_: (layer, 0, 0))]
    grid_spec = pltpu.PrefetchScalarGridSpec(
        num_scalar_prefetch=8, grid=(n_tiles,), in_specs=in_specs,
        out_specs=pl.BlockSpec(memory_space=pl.ANY),
        scratch_shapes=[pltpu.VMEM((2, tm, XS_WIDTH), F32), pltpu.VMEM((2, tm, D_MODEL), F32),
                        pltpu.VMEM((2, D_MODEL, f2), F32), pltpu.VMEM((2, f2 // 2, D_MODEL), F32),
                        pltpu.VMEM((D_MODEL, f2), BF16), pltpu.VMEM((f2 // 2, D_MODEL), BF16),
                        pltpu.VMEM((ZERO_ROWS, D_MODEL), F32),
                        pltpu.SemaphoreType.DMA((2,)), pltpu.SemaphoreType.DMA((2,)), pltpu.SemaphoreType.DMA(()),
                        pltpu.SemaphoreType.DMA((2,)), pltpu.SemaphoreType.DMA((2,))])
    return pl.pallas_call(
        functools.partial(_expert_kernel, tm=tm, tok_block=TOK_BLOCK, n_tok_tiles=xs.shape[0] // TOK_BLOCK,
                          layer=layer),
        grid_spec=grid_spec,
        out_shape=jax.ShapeDtypeStruct((xs.shape[0], D_MODEL), F32),
        compiler_params=_params(1),
        name="moe_experts",
    )(tile_e, n_rows_tile, n_act, src, used, first, group, next_e,
      xs, w_gu, b_gu, w_dn, b_dn)


def _combine_kernel(ys_ref, pos_ref, h1_ref, g2_ref, l2g_ref, l2b_ref, o_ref):
    tm = pos_ref.shape[0]
    n_sorted = ys_ref.shape[0]
    pos = pos_ref[...]
    col = lax.broadcasted_iota(jnp.int32, (tm, n_sorted), 1).astype(F32)
    sel = jnp.zeros((tm, n_sorted), F32)
    for kk in range(TOP_K):
        sel = sel + (col == pos[:, kk:kk + 1]).astype(F32)
    sel = sel.astype(BF16)
    ys = ys_ref[...]
    hi = ys.astype(BF16)
    rest = ys - hi.astype(F32)
    mid = rest.astype(BF16)
    lo = (rest - mid.astype(F32)).astype(BF16)
    y2 = (jnp.dot(sel, hi, preferred_element_type=F32) + jnp.dot(sel, mid, preferred_element_type=F32)
          + jnp.dot(sel, lo, preferred_element_type=F32))
    o_ref[...] = _layer_norm(DEEPNORM_ALPHA * h1_ref[...] + g2_ref[0] * y2) * l2g_ref[...] + l2b_ref[...]


def _combine(ys, pos4, h1, mod3, mod_base, l2g, l2b, n_lat, seq):
    n_rows = h1.shape[0]
    tm = TM_POST
    tiles_per_seq = seq // tm
    n_lat_tiles = n_lat // tm
    n_groups_lat = n_lat // seq

    def group(i):
        return jnp.where(i < n_lat_tiles, i // tiles_per_seq, n_groups_lat)

    in_specs = [pl.BlockSpec((TOK_BLOCK, D_MODEL), lambda i: (i, 0)),
                pl.BlockSpec((tm, LANES), lambda i: (i, 0)),
                pl.BlockSpec((tm, D_MODEL), lambda i: (i, 0)),
                pl.BlockSpec((1, 1, D_MODEL), lambda i: (mod_base + group(i) * 6 + 5, 0, 0)),
                pl.BlockSpec((1, D_MODEL), lambda i: (0, 0)),
                pl.BlockSpec((1, D_MODEL), lambda i: (0, 0))]
    return pl.pallas_call(
        _combine_kernel,
        grid=(n_rows // tm,),
        in_specs=in_specs,
        out_specs=pl.BlockSpec((tm, D_MODEL), lambda i: (i, 0)),
        out_shape=jax.ShapeDtypeStruct((n_rows, D_MODEL), F32),
        compiler_params=_params(1),
        name="moe_combine",
    )(ys, pos4, h1, mod3, l2g.reshape(1, D_MODEL), l2b.reshape(1, D_MODEL))


def _block_diag(w):
    two, n, d, e = w.shape
    eye = jnp.eye(n, dtype=w.dtype)
    return (w[:, :, :, None, :] * eye[None, :, None, :, None]).reshape(two, n * d, n * e)


def kernel(x, c, ctx, c_ctx, w_mod, b_mod, w_in, b_in, na_rpb, w_proj_attn, w_proj_conv, w_proj_lru, sc_conv_w, lru_conv_w, lru_conv_b, lru_lambda, lru_w_r, lru_b_r, lru_w_i, lru_b_i, w_o, b_o, ln1_g, ln1_b, router_w, router_b, exp_w_gu, exp_b_gu, exp_w_dn, exp_b_dn, ln2_g, ln2_b):
    n_batch, seq, d = x.shape
    n_ctx = ctx.shape[1]
    n_lat = n_batch * seq
    n_all = n_lat + n_batch * n_ctx
    assert d == D_MODEL and n_batch + 1 <= SUBLANES

    cc = jnp.concatenate([c, c_ctx[None], jnp.zeros((SUBLANES - n_batch - 1, d), F32)], axis=0)
    mod = _modulation(cc, w_mod, b_mod)
    groups = n_batch + 1
    mod3 = mod.reshape(DEPTH, SUBLANES, 6, d)[:, :groups].reshape(DEPTH * groups * 6, 1, d)

    cos_t, sin_t = _make_rope(seq, TM_INPROJ)
    h = (x.reshape(n_lat, d), ctx.reshape(n_batch * n_ctx, d))
    w_in_bf = w_in.astype(BF16)

    for layer in range(DEPTH):
        last = layer == DEPTH - 1
        mod_base = layer * groups * 6
        q, k, v, sb, rest = _input_projection(h, mod3, mod_base, layer, w_in_bf, b_in, N_EARLY,
                                              cos_t, sin_t, n_lat, seq)
        sp = jax.nn.softplus(-lru_lambda[layer])
        zb, hf, hb, att = _token_mixers(
            q, k, v, _attention_bias(na_rpb[layer], seq), sb, rest, sc_conv_w[layer], lru_conv_w[layer],
            lru_conv_b[layer], sp, _block_diag(lru_w_r[layer]).astype(BF16), _block_diag(lru_w_i[layer]).astype(BF16),
            lru_b_r[layer], lru_b_i[layer], n_batch, seq, n_ctx)
        n_rows = n_lat if last else n_all
        rw_full = jnp.pad(router_w[layer], ((0, 0), (0, LANES - N_EXPERTS)))
        rw_hi = rw_full.astype(BF16)
        rw_pad = jnp.concatenate([rw_hi, (rw_full - rw_hi.astype(F32)).astype(BF16)], axis=1)
        rb_pad = jnp.concatenate([router_b[layer], jnp.full((LANES - N_EXPERTS,), NEG_BIG, F32)]).reshape(1, LANES)
        h1, xs, pos4, cnt_out = _post_mixer(
            h, att, zb, hf, hb, rest, mod3, mod_base, layer, w_in_bf, b_in,
            w_proj_attn[layer].astype(BF16), w_proj_conv[layer].astype(BF16), w_proj_lru[layer].astype(BF16),
            w_o[layer].astype(BF16), b_o[layer], ln1_g[layer], ln1_b[layer], rw_pad, rb_pad, n_rows, n_lat, seq)
        plan = _expert_plan(cnt_out, n_rows // TM_POST, n_rows)
        ys = _experts(xs, plan, layer, exp_w_gu, exp_b_gu, exp_w_dn, exp_b_dn)
        h = (_combine(ys, pos4, h1, mod3, mod_base, ln2_g[layer], ln2_b[layer], n_lat, seq),)
    return h[0].reshape(n_batch, seq, d)
```

```python
import functools

import numpy as np
import jax
import jax.numpy as jnp
from jax import lax
from jax.experimental import pallas as pl
from jax.experimental.pallas import tpu as pltpu

D_MODEL = 1024
DEPTH = 2
GRID_W = 64
NA_HEADS = 8
NA_HEAD_DIM = 64
NA_WIN_ROWS = 8
NA_WIN_COLS = 16
ROPE_BASE = 10000.0
BRANCH_WIDTH = 512
LRU_C = 8.0
N_EARLY = 8 * BRANCH_WIDTH
N_EXPERTS = 32
TOP_K = 4
SWIGLU_LIMIT = 7.0
SWIGLU_ALPHA = 1.702
LN_EPS = 1e-5
DEEPNORM_ALPHA = (2 * DEPTH) ** 0.25
NEG_BIG = -1e30

LANES = 128
SUBLANES = 8
VMEM_LIMIT_BYTES = 56 * 1024 * 1024

TM_INPROJ = 512
SCAN_CHUNK = 256
ATT_ROWS = 4
ATT_QROWS = ATT_ROWS * GRID_W
TM_POST = 256
TM_EXPERT = 512
XS_WIDTH = D_MODEL + LANES
TOK_BLOCK = TM_POST * TOP_K + N_EXPERTS * SUBLANES
ZERO_ROWS = TOK_BLOCK - TM_POST * TOP_K

F32 = jnp.float32
BF16 = jnp.bfloat16


def _params(n_axes):
    return pltpu.CompilerParams(dimension_semantics=("arbitrary",) * n_axes,
                                vmem_limit_bytes=VMEM_LIMIT_BYTES)


def _layer_norm(x):
    mu = jnp.mean(x, axis=-1, keepdims=True)
    xc = x - mu
    var = jnp.mean(xc * xc, axis=-1, keepdims=True)
    return xc * lax.rsqrt(var + LN_EPS)


def _mod_kernel(c_ref, w_ref, b_ref, o_ref):
    c = c_ref[...]
    s = (c * jax.nn.sigmoid(c)).astype(BF16)
    o_ref[0] = jnp.dot(s, w_ref[0].astype(BF16), preferred_element_type=F32) + b_ref[0]


def _modulation(cc, w_mod, b_mod):
    n_out = w_mod.shape[-1]
    return pl.pallas_call(
        _mod_kernel,
        grid=(DEPTH, n_out // D_MODEL),
        in_specs=[pl.BlockSpec((SUBLANES, D_MODEL), lambda l, j: (0, 0)),
                  pl.BlockSpec((1, D_MODEL, D_MODEL), lambda l, j: (l, 0, j)),
                  pl.BlockSpec((1, 1, D_MODEL), lambda l, j: (l, 0, j))],
        out_specs=pl.BlockSpec((1, SUBLANES, D_MODEL), lambda l, j: (l, 0, j)),
        out_shape=jax.ShapeDtypeStruct((DEPTH, SUBLANES, n_out), F32),
        compiler_params=_params(2),
        name="modulation",
    )(cc, w_mod, b_mod.reshape(DEPTH, 1, n_out))


def _rope_half(x, cos, sin_signed):
    m = NA_HEAD_DIM // 4
    lane = lax.broadcasted_iota(jnp.int32, (x.shape[0], LANES), 1)
    first = (lane % (2 * m)) < m
    outs = []
    for cidx in range(x.shape[1] // LANES):
        xc = x[:, cidx * LANES:(cidx + 1) * LANES]
        partner = jnp.where(first, pltpu.roll(xc, LANES - m, 1), pltpu.roll(xc, m, 1))
        outs.append(xc * cos + partner * sin_signed)
    return jnp.concatenate(outs, axis=1)


def _inproj_kernel(ha_ref, hb_ref, sh_ref, sc_ref, w_ref, b_ref, cos_ref, sin_ref,
                   q_ref, k_ref, v_ref, sb_ref, rest_ref, *, split):
    h = jnp.where(pl.program_id(0) < split, ha_ref[...], hb_ref[...])
    xn = (_layer_norm(h) * (1.0 + sc_ref[0]) + sh_ref[0]).astype(BF16)
    half = BRANCH_WIDTH

    def columns(lo, width):
        return jnp.dot(xn, w_ref[0, :, lo:lo + width], preferred_element_type=F32) + b_ref[0, :, lo:lo + width]

    cos = cos_ref[...]
    sin = sin_ref[...]
    q_ref[...] = _rope_half(columns(0, half), cos, sin).astype(BF16)
    k_ref[...] = _rope_half(columns(half, half), cos, sin).astype(BF16)
    v_ref[...] = columns(2 * half, half).astype(BF16)
    sb_ref[...] = columns(3 * half, half)
    n_rest = rest_ref.shape[1]
    for lo in range(0, n_rest, 2 * half):
        rest_ref[:, lo:lo + 2 * half] = columns(4 * half + lo, 2 * half)


def _input_projection(h_parts, mod3, mod_base, layer, w_in_bf, b_in, n_cols, cos_t, sin_t, n_lat, seq):
    tm = TM_INPROJ
    ha, hb = h_parts[0], h_parts[-1]
    split = ha.shape[0] // tm
    m = ha.shape[0] + (hb.shape[0] if len(h_parts) == 2 else 0)
    n_lat_tiles = n_lat // tm
    tiles_per_seq = seq // tm
    n_groups_lat = n_lat // seq

    def group(i):
        return jnp.where(i < n_lat_tiles, i // tiles_per_seq, n_groups_lat)

    def rope_blk(i):
        return jnp.where(i < n_lat_tiles, i % tiles_per_seq, tiles_per_seq)

    half = BRANCH_WIDTH
    return pl.pallas_call(
        functools.partial(_inproj_kernel, split=split),
        grid=(m // tm,),
        in_specs=[pl.BlockSpec((tm, D_MODEL), lambda i: (jnp.minimum(i, split - 1), 0)),
                  pl.BlockSpec((tm, D_MODEL), lambda i: (jnp.maximum(i - split, 0), 0)),
                  pl.BlockSpec((1, 1, D_MODEL), lambda i: (mod_base + group(i) * 6 + 0, 0, 0)),
                  pl.BlockSpec((1, 1, D_MODEL), lambda i: (mod_base + group(i) * 6 + 1, 0, 0)),
                  pl.BlockSpec((1, D_MODEL, n_cols), lambda i: (layer, 0, 0)),
                  pl.BlockSpec((1, 1, n_cols), lambda i: (layer, 0, 0)),
                  pl.BlockSpec((tm, LANES), lambda i: (rope_blk(i), 0)),
                  pl.BlockSpec((tm, LANES), lambda i: (rope_blk(i), 0))],
        out_specs=[pl.BlockSpec((tm, half), lambda i: (i, 0)),
                   pl.BlockSpec((tm, half), lambda i: (i, 0)),
                   pl.BlockSpec((tm, half), lambda i: (i, 0)),
                   pl.BlockSpec((tm, half), lambda i: (i, 0)),
                   pl.BlockSpec((tm, n_cols - 4 * half), lambda i: (i, 0))],
        out_shape=[jax.ShapeDtypeStruct((m, half), BF16),
                   jax.ShapeDtypeStruct((m, half), BF16),
                   jax.ShapeDtypeStruct((m, half), BF16),
                   jax.ShapeDtypeStruct((m, half), F32),
                   jax.ShapeDtypeStruct((m, n_cols - 4 * half), F32)],
        compiler_params=_params(1),
        name="input_projection",
    )(ha, hb, mod3, mod3, w_in_bf, b_in.reshape(DEPTH, 1, -1), cos_t, sin_t)


def _make_rope(seq, tm):
    t = np.arange(seq)
    row_pos, col_pos = t // GRID_W, t % GRID_W
    d = np.arange(LANES) % NA_HEAD_DIM
    m = NA_HEAD_DIM // 4
    inv_freq = (ROPE_BASE ** (-jnp.arange(m, dtype=F32) / m))[d % m]
    pos = np.where((d < 2 * m)[None, :], row_pos[:, None], col_pos[:, None])
    ang = jnp.asarray(pos).astype(F32) * inv_freq[None, :]
    cos = jnp.cos(ang)
    sin = jnp.sin(ang)
    sin_signed = jnp.where(jnp.asarray((d % (2 * m)) < m)[None, :], -sin, sin)
    cos = jnp.concatenate([cos, jnp.ones((tm, LANES), F32)], axis=0)
    sin_signed = jnp.concatenate([sin_signed, jnp.zeros((tm, LANES), F32)], axis=0)
    return cos, sin_signed


def _mixer_kernel(fblk, bblk, first, last, seqb, r0t, cls,
                  sb_ref, xf_ref, xfp_ref, xfn_ref, lxb_ref, lxbp_ref, lxbn_ref,
                  scw_ref, cw_ref, cb_ref, sp_ref, wr_ref, wi_ref, br_ref, bi_ref,
                  q_ref, k_ref, v_ref, kc_ref, vc_ref, bias0_ref, bias1_ref, bias2_ref, bias3_ref,
                  zb_ref, hf_ref, hb_ref, att_ref,
                  a_s, b_s, hc_s, st_s, *, ch, n_ctx_items, band):
    it = pl.program_id(0)
    is_first = first[it] == 1
    is_last = last[it] == 1
    is_ctx = it < n_ctx_items
    b = seqb[it]
    width = BRANCH_WIDTH
    row = lax.broadcasted_iota(jnp.int32, (ch, width), 0)
    not_first = jnp.where(is_first, 0.0, 1.0).astype(F32)
    not_last = jnp.where(is_last, 0.0, 1.0).astype(F32)

    def back1(u, prev_row):
        return jnp.where(row == 0, prev_row, pltpu.roll(u, 1, 0))

    def back2(u, prev2, prev1):
        return jnp.where(row == 0, prev2, jnp.where(row == 1, prev1, pltpu.roll(u, 2, 0)))

    def fwd1(u, next_row):
        return jnp.where(row == ch - 1, next_row, pltpu.roll(u, ch - 1, 0))

    gate_cols, value_cols, lru_cols = (slice(s * width, (s + 1) * width) for s in range(3))

    def lru_input(x, p, n, prev_ok, next_ok):
        p = p * prev_ok
        n = n * next_ok
        return (cw_ref[0:1] * back2(x, p[6:7], p[7:8]) + cw_ref[1:2] * back1(x, p[7:8])
                + cw_ref[2:3] * x + cw_ref[3:4] * fwd1(x, n[0:1]) + cb_ref[...])

    def coeffs(d, xm):
        xb = xm.astype(BF16)
        r = jax.nn.sigmoid(jnp.dot(xb, wr_ref[d], preferred_element_type=F32) + br_ref[d:d + 1])
        g = jax.nn.sigmoid(jnp.dot(xb, wi_ref[d], preferred_element_type=F32) + bi_ref[d:d + 1])
        log_a = (-LRU_C * sp_ref[d:d + 1]) * r
        a = jnp.exp(log_a)
        a_s[d] = a
        b_s[d] = jnp.sqrt(-jnp.tanh(log_a) * (a * a + 1.0)) * (g * xm)

    @pl.when(jnp.logical_and(is_first, is_ctx))
    def _():
        hc_s[...] = jnp.zeros_like(hc_s)

    @pl.when(jnp.logical_and(is_first, jnp.logical_not(is_ctx)))
    def _():
        hc_s[0:1] = st_s[pl.ds(2 * b, 1), :]
        hc_s[1:2] = st_s[pl.ds(2 * b + 1, 1), :]

    coeffs(0, lru_input(xf_ref[:, lru_cols], xfp_ref[:, lru_cols], xfn_ref[:, lru_cols], not_first, not_last))
    hf = hc_s[0:1]
    for t in range(ch):
        hf = a_s[0, t:t + 1, :] * hf + b_s[0, t:t + 1, :]
        hf_ref[t:t + 1, :] = hf
    hc_s[0:1] = hf

    coeffs(1, lru_input(lxb_ref[...], lxbp_ref[...], lxbn_ref[...], not_last, not_first))

    u = xf_ref[:, gate_cols] * xf_ref[:, value_cols]
    u_prev = xfp_ref[7:8, gate_cols] * xfp_ref[7:8, value_cols] * not_first
    u_next = xfn_ref[0:1, gate_cols] * xfn_ref[0:1, value_cols] * not_last
    conv = scw_ref[0:1] * back1(u, u_prev) + scw_ref[1:2] * u + scw_ref[2:3] * fwd1(u, u_next)
    zb_ref[...] = (sb_ref[...] * conv).astype(BF16)

    hb = hc_s[1:2]
    for t in range(ch - 1, -1, -1):
        hb = a_s[1, t:t + 1, :] * hb + b_s[1, t:t + 1, :]
        hb_ref[t:t + 1, :] = hb
    hc_s[1:2] = hb

    _attention_item(it, r0t, cls, q_ref, k_ref, v_ref, kc_ref, vc_ref,
                    (bias0_ref, bias1_ref, bias2_ref, bias3_ref), att_ref, band)

    @pl.when(is_ctx)
    def _():
        st_s[pl.ds(2 * b, 1), :] = hf
        st_s[pl.ds(2 * b + 1, 1), :] = hb


def _mixer_tables(n_batch, seq, n_ctx, ch):
    assert n_ctx == ch and ch == ATT_QROWS
    nc = seq // ch
    rows = seq // GRID_W
    kr = min(NA_WIN_ROWS, rows)
    ctx0 = n_batch * seq // ch
    fblk, bblk, first, last, seqb, r0t, cls = [], [], [], [], [], [], []
    for b in range(n_batch):
        fblk.append(ctx0 + b); bblk.append(ctx0 + b); first.append(1); last.append(1); seqb.append(b)
        r0t.extend([0] * ATT_ROWS); cls.extend([kr] * ATT_ROWS)
    for b in range(n_batch):
        for c in range(nc):
            fblk.append(b * nc + c); bblk.append(b * nc + nc - 1 - c)
            first.append(int(c == 0)); last.append(int(c == nc - 1)); seqb.append(b)
            for r in range(c * ATT_ROWS, (c + 1) * ATT_ROWS):
                r0 = min(max(r - kr // 2, 0), rows - kr)
                r0t.append(r0); cls.append(r - r0)
    return [np.asarray(a, np.int32) for a in (fblk, bblk, first, last, seqb, r0t, cls)]


def _token_mixers(q, k, v, bias, sb, rest, sc_w, lru_cw, lru_cb, sp, wr_bd, wi_bd, b_r, b_i, n_batch, seq, n_ctx):
    m = sb.shape[0]
    ch = SCAN_CHUNK
    width = BRANCH_WIDTH
    tables = _mixer_tables(n_batch, seq, n_ctx, ch)
    n_items = len(tables[0])
    halo_per_chunk = ch // SUBLANES
    last_halo = m // SUBLANES - 1
    band = min(NA_WIN_ROWS, seq // GRID_W) * GRID_W
    ctx_blk0 = n_batch * seq // n_ctx

    def cur(col, which, n=1):
        return pl.BlockSpec((ch, n * width), lambda i, f, bk, *_: ((f, bk)[which][i], col))

    def prev(col, which, n=1):
        return pl.BlockSpec((SUBLANES, n * width),
                            lambda i, f, bk, *_: (jnp.maximum((f, bk)[which][i] * halo_per_chunk - 1, 0), col))

    def nxt(col, which, n=1):
        return pl.BlockSpec((SUBLANES, n * width),
                            lambda i, f, bk, *_: (jnp.minimum(((f, bk)[which][i] + 1) * halo_per_chunk, last_halo), col))

    def full(shape):
        return pl.BlockSpec(shape, lambda i, *_: (0,) * len(shape))

    in_specs = [cur(0, 0), cur(0, 0, 3), prev(0, 0, 3), nxt(0, 0, 3),
                cur(2, 1), prev(2, 1), nxt(2, 1),
                full(sc_w.shape), full(lru_cw.shape), full((1, width)), full(sp.shape),
                full(wr_bd.shape), full(wi_bd.shape), full(b_r.shape), full(b_i.shape)]
    in_specs += [cur(0, 0),
                 pl.BlockSpec((seq, width), lambda i, f, bk, fi, la, sq, *_: (sq[i], 0)),
                 pl.BlockSpec((seq, width), lambda i, f, bk, fi, la, sq, *_: (sq[i], 0)),
                 pl.BlockSpec((n_ctx, width), lambda i, f, bk, fi, la, sq, *_: (ctx_blk0 + sq[i], 0)),
                 pl.BlockSpec((n_ctx, width), lambda i, f, bk, fi, la, sq, *_: (ctx_blk0 + sq[i], 0))]
    for row in range(ATT_ROWS):
        in_specs.append(pl.BlockSpec((1, NA_HEADS, GRID_W, band),
                                     lambda i, f, bk, fi, la, sq, r0, cl, row=row: (cl[i * ATT_ROWS + row], 0, 0, 0)))
    out_specs = [cur(0, 0), cur(0, 0), cur(0, 1), cur(0, 0)]
    grid_spec = pltpu.PrefetchScalarGridSpec(
        num_scalar_prefetch=7, grid=(n_items,), in_specs=in_specs, out_specs=out_specs,
        scratch_shapes=[pltpu.VMEM((2, ch, width), F32), pltpu.VMEM((2, ch, width), F32),
                        pltpu.VMEM((SUBLANES, width), F32), pltpu.VMEM((2 * n_batch, width), F32)])
    return pl.pallas_call(
        functools.partial(_mixer_kernel, ch=ch, n_ctx_items=n_batch, band=band),
        grid_spec=grid_spec,
        out_shape=[jax.ShapeDtypeStruct((m, width), BF16),
                   jax.ShapeDtypeStruct((m, width), F32),
                   jax.ShapeDtypeStruct((m, width), F32),
                   jax.ShapeDtypeStruct((m, width), BF16)],
        compiler_params=_params(1),
        name="token_mixers",
    )(*[jnp.asarray(t) for t in tables],
      sb, rest, rest, rest, rest, rest, rest,
      sc_w, lru_cw, lru_cb.reshape(1, width), sp, wr_bd, wi_bd, b_r, b_i,
      q, k, v, k, v, bias, bias, bias, bias)


def _attention_item(it, r0t, cls, q_ref, k_ref, v_ref, kc_ref, vc_ref, bias_refs, o_ref, band):
    nq = GRID_W
    lane = lax.broadcasted_iota(jnp.int32, (nq, LANES), 1)
    low = lane < NA_HEAD_DIM
    scale = NA_HEAD_DIM ** -0.5
    nt = (((1,), (1,)), ((), ()))
    for row in range(ATT_ROWS):
        start = pl.multiple_of(r0t[it * ATT_ROWS + row] * GRID_W, GRID_W)
        bias_ref = bias_refs[row]
        qrows = slice(row * nq, (row + 1) * nq)
        for hp in range(NA_HEADS // 2):
            cols = slice(hp * LANES, (hp + 1) * LANES)
            qp = q_ref[qrows, cols].astype(F32) * scale
            qs = jnp.concatenate([jnp.where(low, qp, 0.0), jnp.where(low, 0.0, qp)], axis=0).astype(BF16)
            kb = k_ref[pl.ds(start, band), cols]
            vb = v_ref[pl.ds(start, band), cols]
            s_loc = lax.dot_general(qs, kb, nt, preferred_element_type=F32)
            s_ctx = lax.dot_general(qs, kc_ref[:, cols], nt, preferred_element_type=F32)
            bias = jnp.concatenate([bias_ref[0, 2 * hp], bias_ref[0, 2 * hp + 1]], axis=0)
            s_loc = s_loc + bias
            mx =jnp.maximum(jnp.max(s_loc, axis=-1, keepdims=True), jnp.max(s_ctx, axis=-1, keepdims=True))
            e_loc = jnp.exp(s_loc - mx)
            e_ctx = jnp.exp(s_ctx - mx)
            den = jnp.sum(e_loc, axis=-1, keepdims=True) + jnp.sum(e_ctx, axis=-1, keepdims=True)
            o = (jnp.dot(e_loc.astype(BF16), vb, preferred_element_type=F32)
                 + jnp.dot(e_ctx.astype(BF16), vc_ref[:, cols], preferred_element_type=F32)) / den
            o_ref[qrows, cols] = jnp.where(low, o[:nq], o[nq:]).astype(BF16)


def _attention_bias(rpb, seq):
    rows = seq // GRID_W
    kr = min(NA_WIN_ROWS, rows)
    kc = NA_WIN_COLS
    cq = np.arange(GRID_W)
    c0 = np.clip(cq - kc // 2, 0, GRID_W - kc)
    ck = np.arange(GRID_W)
    inside = (ck[None, :] >= c0[:, None]) & (ck[None, :] < c0[:, None] + kc)
    dc = np.clip(ck[None, :] - cq[:, None] + (NA_WIN_COLS - 1), 0, 2 * NA_WIN_COLS - 2)
    n_dr = 2 * NA_WIN_ROWS - 1
    n_dc = 2 * NA_WIN_COLS - 1
    pick = jnp.asarray((np.arange(n_dc)[:, None] == dc.reshape(1, -1)).astype(np.float32))
    picked = jnp.dot(rpb.reshape(-1, n_dc), pick, precision=lax.Precision.HIGHEST)
    picked = picked.reshape(NA_HEADS, 2 * NA_WIN_ROWS - 1, GRID_W, GRID_W)
    table = jnp.where(jnp.asarray(inside)[None, None], picked, NEG_BIG)
    table = table.transpose(0, 2, 1, 3).reshape(NA_HEADS, GRID_W, n_dr * GRID_W)
    classes = []
    for cl in range(kr):
        lo = (NA_WIN_ROWS - 1 - cl) * GRID_W
        classes.append(table[:, :, lo:lo + kr * GRID_W])
    classes.append(jnp.full((NA_HEADS, GRID_W, kr * GRID_W), NEG_BIG, F32))
    return jnp.stack(classes, axis=0)


def _post_kernel(ha_ref, hc_ref, att_ref, zb_ref, hf_ref, hb_ref, lg_ref,
                 sh1_ref, sc1_ref, g1_ref, sh2_ref, sc2_ref,
                 wgl0_ref, wgl1_ref, wgl2_ref, bgl0_ref, bgl1_ref, bgl2_ref,
                 wpa_ref, wpc_ref, wpl_ref, wo_ref, bo_ref, l1g_ref, l1b_ref,
                 rw_ref, rb_ref, tri_ref, upper_ref,
                 h1_ref, xs_ref, pos_ref, cnt_ref, u_s, logit_s, *, split, n_tiles):
    i = pl.program_id(0)

    @pl.when(i == 0)
    def _():
        u_s[...] = jnp.zeros_like(u_s)
        logit_s[...] = jnp.zeros_like(logit_s)

    _route_and_sort(u_s[...], logit_s[...], tri_ref, upper_ref, xs_ref, pos_ref, cnt_ref)

    h = jnp.where(jnp.minimum(i, n_tiles - 1) < split, ha_ref[...], hc_ref[...])
    u1 = (_layer_norm(h) * (1.0 + sc1_ref[0]) + sh1_ref[0]).astype(BF16)
    y_a = jnp.dot(att_ref[...], wpa_ref[...], preferred_element_type=F32)
    y_b = jnp.dot(zb_ref[...], wpc_ref[...], preferred_element_type=F32)
    zc = jax.nn.gelu(lg_ref[...]) * (hf_ref[...] + hb_ref[...])
    y_c = jnp.dot(zc.astype(BF16), wpl_ref[...], preferred_element_type=F32)
    merged = (jax.nn.sigmoid(jnp.dot(u1, wgl0_ref[0], preferred_element_type=F32) + bgl0_ref[0]) * y_a
              + jax.nn.sigmoid(jnp.dot(u1, wgl1_ref[0], preferred_element_type=F32) + bgl1_ref[0]) * y_b
              + jax.nn.sigmoid(jnp.dot(u1, wgl2_ref[0], preferred_element_type=F32) + bgl2_ref[0]) * y_c)
    y = jnp.dot(merged.astype(BF16), wo_ref[...], preferred_element_type=F32) + bo_ref[...]
    h1 = _layer_norm(DEEPNORM_ALPHA * h + g1_ref[0] * y) * l1g_ref[...] + l1b_ref[...]
    h1_ref[...] = h1
    u2 = _layer_norm(h1) * (1.0 + sc2_ref[0]) + sh2_ref[0]

    u_hi = u2.astype(BF16)
    u_lo = (u2 - u_hi.astype(F32)).astype(BF16)
    by_hi = jnp.dot(u_hi, rw_ref[...], preferred_element_type=F32)
    logit_s[...] = (by_hi[:, :LANES] + by_hi[:, LANES:]
                    + jnp.dot(u_lo, rw_ref[:, :LANES], preferred_element_type=F32) + rb_ref[...])
    u_s[...] = u_hi


def _route_and_sort(u_hi, logits, tri_ref, upper_ref, xs_ref, pos_ref, cnt_ref):
    tm = logits.shape[0]
    lane = lax.broadcasted_iota(jnp.int32, (tm, LANES), 1)
    lane_f = lane.astype(F32)
    work = logits
    tops, idxs, hots = [], [], []
    for _ in range(TOP_K):
        mx = jnp.max(work, axis=-1, keepdims=True)
        idx = jnp.min(jnp.where(work == mx, lane_f, float(LANES)), axis=-1, keepdims=True)
        hot = lane_f == idx
        work = jnp.where(hot, -3e38, work)
        tops.append(mx); idxs.append(idx); hots.append(hot)
    exps = [jnp.exp(t - tops[0]) for t in tops]
    den = exps[0] + exps[1] + exps[2] + exps[3]
    hot_all = jnp.zeros((tm, LANES), F32)
    for hot in hots:
        hot_all = hot_all + hot.astype(F32)
    cnt = jnp.sum(hot_all, axis=0, keepdims=True)
    cnt_pad = jnp.floor((cnt + (SUBLANES - 1.0)) * (1.0 / SUBLANES)) * SUBLANES
    off = jnp.dot(jnp.broadcast_to(cnt_pad, (SUBLANES, LANES)), upper_ref[...],
                  precision=lax.Precision.HIGHEST, preferred_element_type=F32)[0:1]
    slot = off + jnp.dot(tri_ref[...], hot_all.astype(BF16), preferred_element_type=F32)
    pos4 = jnp.zeros((tm, LANES), F32)
    w_tile = jnp.zeros((tm, LANES), F32)
    for kk in range(TOP_K):
        pos_k = jnp.sum(jnp.where(hots[kk], slot, 0.0), axis=-1, keepdims=True)
        pos4 = jnp.where(lane == kk, pos_k, pos4)
        p = exps[kk] / den
        p_hi = p.astype(BF16).astype(F32)
        p_mid = (p - p_hi).astype(BF16).astype(F32)
        p_lo = p - p_hi - p_mid
        w_tile = jnp.where(hots[kk], p_hi, w_tile)
        w_tile = jnp.where(lane_f == idxs[kk] + float(N_EXPERTS), p_mid, w_tile)
        w_tile = jnp.where(lane_f == idxs[kk] + float(2 * N_EXPERTS), p_lo, w_tile)
    pos_ref[...] = pos4
    sub = lax.broadcasted_iota(jnp.int32, (SUBLANES, LANES), 0)
    cnt_ref[...] = jnp.where(sub == 0, cnt, jnp.where(sub == 1, off, 0.0))

    n_sorted = xs_ref.shape[0]
    pos_t = pos4.T
    r_iota = lax.broadcasted_iota(jnp.int32, (n_sorted, tm), 0).astype(F32)
    hit = r_iota == pos_t[0:1, :]
    for kk in range(1, TOP_K):
        hit = jnp.logical_or(hit, r_iota == pos_t[kk:kk + 1, :])
    perm = jnp.where(hit, 1.0, 0.0).astype(BF16)
    feats = jnp.concatenate([u_hi, w_tile.astype(BF16)], axis=1)
    xs_ref[...] = jnp.dot(perm, feats, preferred_element_type=F32)


def _post_mixer(h_parts, att, zb, hf, hb, rest, mod3, mod_base, layer, w_in_bf, b_in, wpa, wpc, wpl, wo, b_o, l1g,
                l1b, rw_pad, rb_pad, n_rows, n_lat, seq):
    tm = TM_POST
    width = BRANCH_WIDTH
    ha, hc = h_parts[0], h_parts[-1]
    split = min(ha.shape[0], n_rows) // tm
    gate_col0 = N_EARLY // D_MODEL
    tiles_per_seq = seq // tm
    n_lat_tiles = n_lat // tm
    n_groups_lat = n_lat // seq
    tri = jnp.asarray(np.tril(np.ones((tm, tm), np.float32), -1), BF16)
    upper = jnp.asarray(np.triu(np.ones((LANES, LANES), np.float32), 1))
    n_tiles = n_rows // tm

    def dense(i):
        return jnp.minimum(i, n_tiles - 1)

    def routed(i):
        return jnp.maximum(i - 1, 0)

    def group(i):
        return jnp.where(dense(i) < n_lat_tiles, dense(i) // tiles_per_seq, n_groups_lat)

    def rows(wd, col=0):
        return pl.BlockSpec((tm, wd), lambda i: (dense(i), col))

    def full(shape):
        return pl.BlockSpec(shape, lambda i: (0,) * len(shape))

    def mod(which):
        return pl.BlockSpec((1, 1, D_MODEL), lambda i: (mod_base + group(i) * 6 + which, 0, 0))

    in_specs = [pl.BlockSpec((tm, D_MODEL), lambda i: (jnp.minimum(dense(i), split - 1), 0)),
                pl.BlockSpec((tm, D_MODEL), lambda i: (jnp.maximum(dense(i) - split, 0), 0)),
                rows(width), rows(width), rows(width), rows(width), rows(width, 3),
                mod(0), mod(1), mod(2), mod(3), mod(4)]
    in_specs += [pl.BlockSpec((1, D_MODEL, D_MODEL), lambda i, c=c: (layer, 0, gate_col0 + c)) for c in range(3)]
    in_specs += [pl.BlockSpec((1, 1, D_MODEL), lambda i, c=c: (layer, 0, gate_col0 + c)) for c in range(3)]
    in_specs += [full(wpa.shape), full(wpc.shape), full(wpl.shape), full(wo.shape), full((1, D_MODEL)),
                full((1, D_MODEL)), full((1, D_MODEL)), full(rw_pad.shape), full(rb_pad.shape), full(tri.shape),
                full(upper.shape)]
    out_specs = [rows(D_MODEL), pl.BlockSpec((TOK_BLOCK, XS_WIDTH), lambda i: (routed(i), 0)),
                 pl.BlockSpec((tm, LANES), lambda i: (routed(i), 0)),
                 pl.BlockSpec((SUBLANES, LANES), lambda i: (routed(i), 0))]
    return pl.pallas_call(
        functools.partial(_post_kernel, split=split, n_tiles=n_tiles),
        grid=(n_tiles + 1,),
        in_specs=in_specs, out_specs=out_specs,
        out_shape=[jax.ShapeDtypeStruct((n_rows, D_MODEL), F32),
                   jax.ShapeDtypeStruct((n_tiles * TOK_BLOCK, XS_WIDTH), F32),
                   jax.ShapeDtypeStruct((n_rows, LANES), F32),
                   jax.ShapeDtypeStruct((n_tiles * SUBLANES, LANES), F32)],
        scratch_shapes=[pltpu.VMEM((tm, D_MODEL), BF16), pltpu.VMEM((tm, LANES), F32)],
        compiler_params=_params(1),
        name="post_mixer",
    )(ha, hc, att, zb, hf, hb, rest, mod3, mod3, mod3, mod3, mod3,
      w_in_bf, w_in_bf, w_in_bf, b_in.reshape(DEPTH, 1, -1), b_in.reshape(DEPTH, 1, -1), b_in.reshape(DEPTH, 1, -1),
      wpa, wpc, wpl, wo, b_o.reshape(1, D_MODEL), l1g.reshape(1, D_MODEL), l1b.reshape(1, D_MODEL),
      rw_pad, rb_pad, tri, upper)


def _expert_plan(cnt_out, n_tok_tiles, n_rows):
    tm = TM_EXPERT
    co = cnt_out.reshape(n_tok_tiles, SUBLANES, LANES)
    cnt = (co[:, 0, :N_EXPERTS].astype(jnp.int32) + SUBLANES - 1) // SUBLANES * SUBLANES
    off = co[:, 1, :N_EXPERTS].astype(jnp.int32)
    cum_end = jnp.cumsum(cnt, axis=0)
    cum = cum_end - cnt
    total = cum_end[-1]
    n_et = (total + tm - 1) // tm
    et_end = jnp.cumsum(n_et)
    n_act = et_end[-1:].astype(jnp.int32)
    n_tiles = -(-n_tok_tiles * TOK_BLOCK // tm) + N_EXPERTS
    j = jnp.arange(n_tiles, dtype=jnp.int32)
    tile_e = jnp.minimum(jnp.sum((et_end[None, :] <= j[:, None]).astype(jnp.int32), axis=1), N_EXPERTS - 1)
    pick_e = (tile_e[:, None] == jnp.arange(N_EXPERTS, dtype=jnp.int32)[None, :]).astype(F32)

    def per_tile(table):
        return jnp.dot(pick_e, table.astype(F32), precision=lax.Precision.HIGHEST).astype(jnp.int32)

    row0 = (j - per_tile(et_end - n_et)) * tm
    n_rows_tile = jnp.clip(per_tile(total) - row0, 0, tm)
    cum_e = per_tile(cum.T)
    cum_end_e = per_tile(cum_end.T)
    delta_e = per_tile((jnp.arange(n_tok_tiles, dtype=jnp.int32)[:, None] * TOK_BLOCK + off - cum).T)
    q = row0[:, None] + SUBLANES * jnp.arange(tm // SUBLANES, dtype=jnp.int32)[None, :]
    inside = jnp.logical_and(cum_e.T[:, :, None] <= q[None], q[None] < cum_end_e.T[:, :, None])
    src = q + jnp.sum(jnp.where(inside, delta_e.T[:, :, None], 0), axis=0)
    used = off[:, N_EXPERTS - 1] + cnt[:, N_EXPERTS - 1]
    first = jnp.concatenate([jnp.ones((1,), jnp.int32), (tile_e[1:] != tile_e[:-1]).astype(jnp.int32)])
    group = jnp.cumsum(first) - 1
    after = per_tile(et_end)
    next_e = jnp.where(after < n_act[0],
                       jnp.minimum(jnp.sum((et_end[None, :] <= after[:, None]).astype(jnp.int32), axis=1),
                                   N_EXPERTS - 1), -1)
    return (tile_e.astype(jnp.int32), n_rows_tile.astype(jnp.int32), n_act, src.reshape(-1).astype(jnp.int32),
            used.astype(jnp.int32), first, group.astype(jnp.int32), next_e.astype(jnp.int32), n_tiles)


def _expert_kernel(tile_e, n_rows_t, n_act, src_t, used_t, first_t, group_t, next_t,
                   xs_hbm, wgu_hbm, bgu_ref, wdn_hbm, bdn_ref, ys_hbm,
                   xin, yout, wgu_f, wdn_f, wgu_s, wdn_s, zeros, sem_in, sem_out, sem_zero, sem_wgu, sem_wdn,
                   *, tm, tok_block, n_tok_tiles, layer):
    j = pl.program_id(0)
    na = n_act[0]
    chunks = tm // SUBLANES

    def weight_copies(e, wslot):
        return (pltpu.make_async_copy(wgu_hbm.at[layer, e], wgu_f.at[wslot], sem_wgu.at[wslot]),
                pltpu.make_async_copy(wdn_hbm.at[layer, e], wdn_f.at[wslot], sem_wdn.at[wslot]))

    def gather(src, dst, size, slot):
        return pltpu.make_async_copy(xs_hbm.at[pl.ds(src, size)], xin.at[slot, pl.ds(dst, size)], sem_in.at[slot])

    def scatter(src, dst, size, slot):
        return pltpu.make_async_copy(yout.at[slot, pl.ds(dst, size)], ys_hbm.at[pl.ds(src, size)], sem_out.at[slot])

    def start_chunks(jj, slot, copy):
        def one(c, priority):
            src = pl.multiple_of(src_t[jj * chunks + c], SUBLANES)
            copy(src, pl.multiple_of(c * SUBLANES, SUBLANES), SUBLANES, slot).start(priority=priority)

        def body(c, carry):
            one(c, 0)
            return carry

        def body_unrolled(g, carry):
            for u in range(SUBLANES):
                one(g * SUBLANES + u, u % 2)
            return carry

        n = lax.shift_right_logical(n_rows_t[jj], 3)

        @pl.when(n == chunks)
        def _():
            lax.fori_loop(0, chunks // SUBLANES, body_unrolled, 0)

        @pl.when(n != chunks)
        def _():
            lax.fori_loop(0, n, body, 0)

    def wait_rows(jj, slot, copy):
        n = n_rows_t[jj]
        size = tm
        while size >= SUBLANES:
            @pl.when((n & size) != 0)
            def _(size=size):
                copy(0, 0, size, slot).wait()
            size //= 2

    slot = j % 2

    @pl.when(j == 0)
    def _():
        xin[...] = jnp.zeros_like(xin)
        start_chunks(0, 0, gather)
        zeros[...] = jnp.zeros_like(zeros)

        def clear_tail(i, copy_op):
            used = used_t[i]
            tail = tok_block - used
            size = ZERO_ROWS
            sizes = []
            while size >= SUBLANES:
                sizes.append(size)
                size //= 2
            for size in sizes:
                @pl.when((tail & size) != 0)
                def _(size=size):
                    at = pl.multiple_of(i * tok_block + used + (tail & ~(2 * size - 1)), SUBLANES)
                    copy_op(pltpu.make_async_copy(zeros.at[pl.ds(0, size)], ys_hbm.at[pl.ds(at, size)], sem_zero))

        def start_clear(i, c):
            clear_tail(i, lambda cp: cp.start())
            return c

        def wait_clear(i, c):
            clear_tail(i, lambda cp: cp.wait())
            return c

        lax.fori_loop(0, n_tok_tiles, start_clear, 0)
        lax.fori_loop(0, n_tok_tiles, wait_clear, 0)

    @pl.when(j + 1 < na)
    def _():
        start_chunks(j + 1, 1 - slot, gather)

    @pl.when(j < na)
    def _():
        e = tile_e[j]

        @pl.when(first_t[j] == 1)
        def _():
            wslot = group_t[j] % 2

            @pl.when(j == 0)
            def _():
                for cp in weight_copies(e, wslot):
                    cp.start()

            for cp in weight_copies(e, wslot):
                cp.wait()
            wgu_s[...] = wgu_f[wslot].astype(BF16)
            wdn_s[...] = wdn_f[wslot].astype(BF16)

            @pl.when(next_t[j] >= 0)
            def _():
                for cp in weight_copies(next_t[j], 1 - wslot):
                    cp.start()

        wait_rows(j, slot, gather)

        def compute(rows):
            x = xin[slot, pl.ds(0, rows)]
            lane = lax.broadcasted_iota(jnp.int32, (rows, LANES), 1)
            p = jnp.sum(jnp.where(lane % N_EXPERTS == e, x[:, D_MODEL:], 0.0), axis=-1, keepdims=True)
            gu = (jnp.dot(x[:, :D_MODEL].astype(BF16), wgu_s[...], preferred_element_type=F32)
                  + bgu_ref[0, pl.ds(e, 1), :])
            f = gu.shape[1] // 2
            gate = jnp.minimum(gu[:, :f], SWIGLU_LIMIT)
            up = jnp.clip(gu[:, f:], -SWIGLU_LIMIT, SWIGLU_LIMIT)
            hid = (up + 1.0) * gate * jax.nn.sigmoid(SWIGLU_ALPHA * gate)
            y = (jnp.dot(hid.astype(BF16), wdn_s[...], preferred_element_type=F32)
                 + bdn_ref[0, pl.ds(e, 1), :])
            yout[slot, pl.ds(0, rows)] = y * p

        quarter = tm // 4
        for part in range(1, 5):
            @pl.when(jnp.logical_and(n_rows_t[j] > (part - 1) * quarter, n_rows_t[j] <= part * quarter))
            def _(part=part):
                compute(part * quarter)

        start_chunks(j, slot, scatter)

        @pl.when(j >= 1)
        def _():
            wait_rows(j - 1, 1 - slot, scatter)

        @pl.when(j == na - 1)
        def _():
            wait_rows(j, slot, scatter)


def _experts(xs, plan, layer, w_gu, b_gu, w_dn, b_dn):
    tile_e, n_rows_tile, n_act, src, used, first, group, next_e, n_tiles = plan
    tm = TM_EXPERT
    f2 = w_gu.shape[-1]

    in_specs = [pl.BlockSpec(memory_space=pl.ANY),
                pl.BlockSpec(memory_space=pl.ANY),
                pl.BlockSpec((1, N_EXPERTS, f2), lambda j, *_: (layer, 0, 0)),
                pl.BlockSpec(memory_space=pl.ANY),
                pl.BlockSpec((1, N_EXPERTS, D_MODEL), lambda j, *_: (layer, 0, 0))]
    grid_spec = pltpu.PrefetchScalarGridSpec(
        num_scalar_prefetch=8, grid=(n_tiles,), in_specs=in_specs,
        out_specs=pl.BlockSpec(memory_space=pl.ANY),
        scratch_shapes=[pltpu.VMEM((2, tm, XS_WIDTH), F32), pltpu.VMEM((2, tm, D_MODEL), F32),
                        pltpu.VMEM((2, D_MODEL, f2), F32), pltpu.VMEM((2, f2 // 2, D_MODEL), F32),
                        pltpu.VMEM((D_MODEL, f2), BF16), pltpu.VMEM((f2 // 2, D_MODEL), BF16),
                        pltpu.VMEM((ZERO_ROWS, D_MODEL), F32),
                        pltpu.SemaphoreType.DMA((2,)), pltpu.SemaphoreType.DMA((2,)), pltpu.SemaphoreType.DMA(()),
                        pltpu.SemaphoreType.DMA((2,)), pltpu.SemaphoreType.DMA((2,))])
    return pl.pallas_call(
        functools.partial(_expert_kernel, tm=tm, tok_block=TOK_BLOCK, n_tok_tiles=xs.shape[0] // TOK_BLOCK,
                          layer=layer),
        grid_spec=grid_spec,
        out_shape=jax.ShapeDtypeStruct((xs.shape[0], D_MODEL), F32),
        compiler_params=_params(1),
        name="moe_experts",
    )(tile_e, n_rows_tile, n_act, src, used, first, group, next_e,
      xs, w_gu, b_gu, w_dn, b_dn)


def _combine_kernel(ys_ref, pos_ref, h1_ref, g2_ref, l2g_ref, l2b_ref, o_ref):
    tm = pos_ref.shape[0]
    n_sorted = ys_ref.shape[0]
    pos = pos_ref[...]
    col = lax.broadcasted_iota(jnp.int32, (tm, n_sorted), 1).astype(F32)
    sel = jnp.zeros((tm, n_sorted), F32)
    for kk in range(TOP_K):
        sel = sel + (col == pos[:, kk:kk + 1]).astype(F32)
    sel = sel.astype(BF16)
    ys = ys_ref[...]
    hi = ys.astype(BF16)
    rest = ys - hi.astype(F32)
    mid = rest.astype(BF16)
    lo = (rest - mid.astype(F32)).astype(BF16)
    y2 = (jnp.dot(sel, hi, preferred_element_type=F32) + jnp.dot(sel, mid, preferred_element_type=F32)
          + jnp.dot(sel, lo, preferred_element_type=F32))
    o_ref[...] = _layer_norm(DEEPNORM_ALPHA * h1_ref[...] + g2_ref[0] * y2) * l2g_ref[...] + l2b_ref[...]


def _combine(ys, pos4, h1, mod3, mod_base, l2g, l2b, n_lat, seq):
    n_rows = h1.shape[0]
    tm = TM_POST
    tiles_per_seq = seq // tm
    n_lat_tiles = n_lat // tm
    n_groups_lat = n_lat // seq

    def group(i):
        return jnp.where(i < n_lat_tiles, i // tiles_per_seq, n_groups_lat)

    in_specs = [pl.BlockSpec((TOK_BLOCK, D_MODEL), lambda i: (i, 0)),
                pl.BlockSpec((tm, LANES), lambda i: (i, 0)),
                pl.BlockSpec((tm, D_MODEL), lambda i: (i, 0)),
                pl.BlockSpec((1, 1, D_MODEL), lambda i: (mod_base + group(i) * 6 + 5, 0, 0)),
                pl.BlockSpec((1, D_MODEL), lambda i: (0, 0)),
                pl.BlockSpec((1, D_MODEL), lambda i: (0, 0))]
    return pl.pallas_call(
        _combine_kernel,
        grid=(n_rows // tm,),
        in_specs=in_specs,
        out_specs=pl.BlockSpec((tm, D_MODEL), lambda i: (i, 0)),
        out_shape=jax.ShapeDtypeStruct((n_rows, D_MODEL), F32),
        compiler_params=_params(1),
        name="moe_combine",
    )(ys, pos4, h1, mod3, l2g.reshape(1, D_MODEL), l2b.reshape(1, D_MODEL))


def _block_diag(w):
    two, n, d, e = w.shape
    eye = jnp.eye(n, dtype=w.dtype)
    return (w[:, :, :, None, :] * eye[None, :, None, :, None]).reshape(two, n * d, n * e)


def kernel(x, c, ctx, c_ctx, w_mod, b_mod, w_in, b_in, na_rpb, w_proj_attn, w_proj_conv, w_proj_lru, sc_conv_w, lru_conv_w, lru_conv_b, lru_lambda, lru_w_r, lru_b_r, lru_w_i, lru_b_i, w_o, b_o, ln1_g, ln1_b, router_w, router_b, exp_w_gu, exp_b_gu, exp_w_dn, exp_b_dn, ln2_g, ln2_b):
    n_batch, seq, d = x.shape
    n_ctx = ctx.shape[1]
    n_lat = n_batch * seq
    n_all = n_lat + n_batch * n_ctx
    assert d == D_MODEL and n_batch + 1 <= SUBLANES

    cc = jnp.concatenate([c, c_ctx[None], jnp.zeros((SUBLANES - n_batch - 1, d), F32)], axis=0)
    mod = _modulation(cc, w_mod, b_mod)
    groups = n_batch + 1
    mod3 = mod.reshape(DEPTH, SUBLANES, 6, d)[:, :groups].reshape(DEPTH * groups * 6, 1, d)

    cos_t, sin_t = _make_rope(seq, TM_INPROJ)
    h = (x.reshape(n_lat, d), ctx.reshape(n_batch * n_ctx, d))
    w_in_bf = w_in.astype(BF16)

    for layer in range(DEPTH):
        last = layer == DEPTH - 1
        mod_base = layer * groups * 6
        q, k, v, sb, rest = _input_projection(h, mod3, mod_base, layer, w_in_bf, b_in, N_EARLY,
                                              cos_t, sin_t, n_lat, seq)
        sp = jax.nn.softplus(-lru_lambda[layer])
        zb, hf, hb, att = _token_mixers(
            q, k, v, _attention_bias(na_rpb[layer], seq), sb, rest, sc_conv_w[layer], lru_conv_w[layer],
            lru_conv_b[layer], sp, _block_diag(lru_w_r[layer]).astype(BF16), _block_diag(lru_w_i[layer]).astype(BF16),
            lru_b_r[layer], lru_b_i[layer], n_batch, seq, n_ctx)
        n_rows = n_lat if last else n_all
        rw_full = jnp.pad(router_w[layer], ((0, 0), (0, LANES - N_EXPERTS)))
        rw_hi = rw_full.astype(BF16)
        rw_pad = jnp.concatenate([rw_hi, (rw_full - rw_hi.astype(F32)).astype(BF16)], axis=1)
        rb_pad = jnp.concatenate([router_b[layer], jnp.full((LANES - N_EXPERTS,), NEG_BIG, F32)]).reshape(1, LANES)
        h1, xs, pos4, cnt_out = _post_mixer(
            h, att, zb, hf, hb, rest, mod3, mod_base, layer, w_in_bf, b_in,
            w_proj_attn[layer].astype(BF16), w_proj_conv[layer].astype(BF16), w_proj_lru[layer].astype(BF16),
            w_o[layer].astype(BF16), b_o[layer], ln1_g[layer], ln1_b[layer], rw_pad, rb_pad, n_rows, n_lat, seq)
        plan = _expert_plan(cnt_out, n_rows // TM_POST, n_rows)
        ys = _experts(xs, plan, layer, exp_w_gu, exp_b_gu, exp_w_dn, exp_b_dn)
        h = (_combine(ys, pos4, h1, mod3, mod_base, ln2_g[layer], ln2_b[layer], n_lat, seq),)
    return h[0].reshape(n_batch, seq, d)
```

```python
import functools

import numpy as np
import jax
import jax.numpy as jnp
from jax import lax
from jax.experimental import pallas as pl
from jax.experimental.pallas import tpu as pltpu

D_MODEL = 1024
DEPTH = 2
GRID_W = 64
NA_HEADS = 8
NA_HEAD_DIM = 64
NA_WIN_ROWS = 8
NA_WIN_COLS = 16
ROPE_BASE = 10000.0
BRANCH_WIDTH = 512
LRU_C = 8.0
N_EARLY = 8 * BRANCH_WIDTH
N_EXPERTS = 32
TOP_K = 4
SWIGLU_LIMIT = 7.0
SWIGLU_ALPHA = 1.702
LN_EPS = 1e-5
DEEPNORM_ALPHA = (2 * DEPTH) ** 0.25
NEG_BIG = -1e30

LANES = 128
SUBLANES = 8
VMEM_LIMIT_BYTES = 56 * 1024 * 1024

TM_INPROJ = 512
SCAN_CHUNK = 256
ATT_ROWS = 4
ATT_QROWS = ATT_ROWS * GRID_W
TM_POST = 256
TM_EXPERT = 512
XS_WIDTH = D_MODEL + LANES
TOK_BLOCK = TM_POST * TOP_K + N_EXPERTS * SUBLANES
ZERO_ROWS = TOK_BLOCK - TM_POST * TOP_K

F32 = jnp.float32
BF16 = jnp.bfloat16


def _params(n_axes):
    return pltpu.CompilerParams(dimension_semantics=("arbitrary",) * n_axes,
                                vmem_limit_bytes=VMEM_LIMIT_BYTES)


def _layer_norm(x):
    mu = jnp.mean(x, axis=-1, keepdims=True)
    xc = x - mu
    var = jnp.mean(xc * xc, axis=-1, keepdims=True)
    return xc * lax.rsqrt(var + LN_EPS)


def _mod_kernel(c_ref, w_ref, b_ref, o_ref):
    c = c_ref[...]
    s = (c * jax.nn.sigmoid(c)).astype(BF16)
    o_ref[0] = jnp.dot(s, w_ref[0].astype(BF16), preferred_element_type=F32) + b_ref[0]


def _modulation(cc, w_mod, b_mod):
    n_out = w_mod.shape[-1]
    return pl.pallas_call(
        _mod_kernel,
        grid=(DEPTH, n_out // D_MODEL),
        in_specs=[pl.BlockSpec((SUBLANES, D_MODEL), lambda l, j: (0, 0)),
                  pl.BlockSpec((1, D_MODEL, D_MODEL), lambda l, j: (l, 0, j)),
                  pl.BlockSpec((1, 1, D_MODEL), lambda l, j: (l, 0, j))],
        out_specs=pl.BlockSpec((1, SUBLANES, D_MODEL), lambda l, j: (l, 0, j)),
        out_shape=jax.ShapeDtypeStruct((DEPTH, SUBLANES, n_out), F32),
        compiler_params=_params(2),
        name="modulation",
    )(cc, w_mod, b_mod.reshape(DEPTH, 1, n_out))


def _rope_half(x, cos, sin_signed):
    m = NA_HEAD_DIM // 4
    lane = lax.broadcasted_iota(jnp.int32, (x.shape[0], LANES), 1)
    first = (lane % (2 * m)) < m
    outs = []
    for cidx in range(x.shape[1] // LANES):
        xc = x[:, cidx * LANES:(cidx + 1) * LANES]
        partner = jnp.where(first, pltpu.roll(xc, LANES - m, 1), pltpu.roll(xc, m, 1))
        outs.append(xc * cos + partner * sin_signed)
    return jnp.concatenate(outs, axis=1)


def _inproj_kernel(ha_ref, hb_ref, sh_ref, sc_ref, w_ref, b_ref, cos_ref, sin_ref,
                   q_ref, k_ref, v_ref, sb_ref, rest_ref, *, split):
    h = jnp.where(pl.program_id(0) < split, ha_ref[...], hb_ref[...])
    xn = (_layer_norm(h) * (1.0 + sc_ref[0]) + sh_ref[0]).astype(BF16)
    half = BRANCH_WIDTH

    def columns(lo, width):
        return jnp.dot(xn, w_ref[0, :, lo:lo + width], preferred_element_type=F32) + b_ref[0, :, lo:lo + width]

    cos = cos_ref[...]
    sin = sin_ref[...]
    q_ref[...] = _rope_half(columns(0, half), cos, sin).astype(BF16)
    k_ref[...] = _rope_half(columns(half, half), cos, sin).astype(BF16)
    v_ref[...] = columns(2 * half, half).astype(BF16)
    sb_ref[...] = columns(3 * half, half)
    n_rest = rest_ref.shape[1]
    for lo in range(0, n_rest, 2 * half):
        rest_ref[:, lo:lo + 2 * half] = columns(4 * half + lo, 2 * half)


def _input_projection(h_parts, mod3, mod_base, layer, w_in_bf, b_in, n_cols, cos_t, sin_t, n_lat, seq):
    tm = TM_INPROJ
    ha, hb = h_parts[0], h_parts[-1]
    split = ha.shape[0] // tm
    m = ha.shape[0] + (hb.shape[0] if len(h_parts) == 2 else 0)
    n_lat_tiles = n_lat // tm
    tiles_per_seq = seq // tm
    n_groups_lat = n_lat // seq

    def group(i):
        return jnp.where(i < n_lat_tiles, i // tiles_per_seq, n_groups_lat)

    def rope_blk(i):
        return jnp.where(i < n_lat_tiles, i % tiles_per_seq, tiles_per_seq)

    half = BRANCH_WIDTH
    return pl.pallas_call(
        functools.partial(_inproj_kernel, split=split),
        grid=(m // tm,),
        in_specs=[pl.BlockSpec((tm, D_MODEL), lambda i: (jnp.minimum(i, split - 1), 0)),
                  pl.BlockSpec((tm, D_MODEL), lambda i: (jnp.maximum(i - split, 0), 0)),
                  pl.BlockSpec((1, 1, D_MODEL), lambda i: (mod_base + group(i) * 6 + 0, 0, 0)),
                  pl.BlockSpec((1, 1, D_MODEL), lambda i: (mod_base + group(i) * 6 + 1, 0, 0)),
                  pl.BlockSpec((1, D_MODEL, n_cols), lambda i: (layer, 0, 0)),
                  pl.BlockSpec((1, 1, n_cols), lambda i: (layer, 0, 0)),
                  pl.BlockSpec((tm, LANES), lambda i: (rope_blk(i), 0)),
                  pl.BlockSpec((tm, LANES), lambda i: (rope_blk(i), 0))],
        out_specs=[pl.BlockSpec((tm, half), lambda i: (i, 0)),
                   pl.BlockSpec((tm, half), lambda i: (i, 0)),
                   pl.BlockSpec((tm, half), lambda i: (i, 0)),
                   pl.BlockSpec((tm, half), lambda i: (i, 0)),
                   pl.BlockSpec((tm, n_cols - 4 * half), lambda i: (i, 0))],
        out_shape=[jax.ShapeDtypeStruct((m, half), BF16),
                   jax.ShapeDtypeStruct((m, half), BF16),
                   jax.ShapeDtypeStruct((m, half), BF16),
                   jax.ShapeDtypeStruct((m, half), F32),
                   jax.ShapeDtypeStruct((m, n_cols - 4 * half), F32)],
        compiler_params=_params(1),
        name="input_projection",
    )(ha, hb, mod3, mod3, w_in_bf, b_in.reshape(DEPTH, 1, -1), cos_t, sin_t)


def _make_rope(seq, tm):
    rows = seq // GRID_W
    d = np.arange(LANES) % NA_HEAD_DIM
    m = NA_HEAD_DIM // 4
    inv_freq = (ROPE_BASE ** (-jnp.arange(m, dtype=F32) / m))[d % m]
    n_pos = max(rows, GRID_W)
    ang = jnp.arange(n_pos, dtype=F32)[:, None] * inv_freq[None, :]
    by_row = jnp.asarray(d < 2 * m)[None, None, :]

    def expand(tab):
        full = jnp.where(by_row, tab[:rows, None, :], tab[None, :GRID_W, :])
        return full.reshape(seq, LANES)

    cos = expand(jnp.cos(ang))
    sin = expand(jnp.sin(ang))
    sin_signed = jnp.where(jnp.asarray((d % (2 * m)) < m)[None, :], -sin, sin)
    cos = jnp.concatenate([cos, jnp.ones((tm, LANES), F32)], axis=0)
    sin_signed = jnp.concatenate([sin_signed, jnp.zeros((tm, LANES), F32)], axis=0)
    return cos, sin_signed


def _mixer_kernel(fblk, bblk, first, last, seqb, r0t, cls,
                  sb_ref, xf_ref, xfp_ref, xfn_ref, lxb_ref, lxbp_ref, lxbn_ref,
                  scw_ref, cw_ref, cb_ref, sp_ref, wr_ref, wi_ref, br_ref, bi_ref,
                  q_ref, k_ref, v_ref, kc_ref, vc_ref, bias0_ref, bias1_ref, bias2_ref, bias3_ref,
                  zb_ref, hf_ref, hb_ref, att_ref,
                  a_s, b_s, hc_s, st_s, *, ch, n_ctx_items, band):
    it = pl.program_id(0)
    is_first = first[it] == 1
    is_last = last[it] == 1
    is_ctx = it < n_ctx_items
    b = seqb[it]
    width = BRANCH_WIDTH
    row = lax.broadcasted_iota(jnp.int32, (ch, width), 0)
    not_first = jnp.where(is_first, 0.0, 1.0).astype(F32)
    not_last = jnp.where(is_last, 0.0, 1.0).astype(F32)

    def back1(u, prev_row):
        return jnp.where(row == 0, prev_row, pltpu.roll(u, 1, 0))

    def back2(u, prev2, prev1):
        return jnp.where(row == 0, prev2, jnp.where(row == 1, prev1, pltpu.roll(u, 2, 0)))

    def fwd1(u, next_row):
        return jnp.where(row == ch - 1, next_row, pltpu.roll(u, ch - 1, 0))

    gate_cols, value_cols, lru_cols = (slice(s * width, (s + 1) * width) for s in range(3))

    def lru_input(x, p, n, prev_ok, next_ok):
        p = p * prev_ok
        n = n * next_ok
        return (cw_ref[0:1] * back2(x, p[6:7], p[7:8]) + cw_ref[1:2] * back1(x, p[7:8])
                + cw_ref[2:3] * x + cw_ref[3:4] * fwd1(x, n[0:1]) + cb_ref[...])

    def coeffs(d, xm):
        xb = xm.astype(BF16)
        r = jax.nn.sigmoid(jnp.dot(xb, wr_ref[d], preferred_element_type=F32) + br_ref[d:d + 1])
        g = jax.nn.sigmoid(jnp.dot(xb, wi_ref[d], preferred_element_type=F32) + bi_ref[d:d + 1])
        log_a = (-LRU_C * sp_ref[d:d + 1]) * r
        a = jnp.exp(log_a)
        a_s[d] = a
        b_s[d] = jnp.sqrt(-jnp.tanh(log_a) * (a * a + 1.0)) * (g * xm)

    @pl.when(jnp.logical_and(is_first, is_ctx))
    def _():
        hc_s[...] = jnp.zeros_like(hc_s)

    @pl.when(jnp.logical_and(is_first, jnp.logical_not(is_ctx)))
    def _():
        hc_s[0:1] = st_s[pl.ds(2 * b, 1), :]
        hc_s[1:2] = st_s[pl.ds(2 * b + 1, 1), :]

    coeffs(0, lru_input(xf_ref[:, lru_cols], xfp_ref[:, lru_cols], xfn_ref[:, lru_cols], not_first, not_last))
    hf = hc_s[0:1]
    for t in range(ch):
        hf = a_s[0, t:t + 1, :] * hf + b_s[0, t:t + 1, :]
        hf_ref[t:t + 1, :] = hf
    hc_s[0:1] = hf

    coeffs(1, lru_input(lxb_ref[...], lxbp_ref[...], lxbn_ref[...], not_last, not_first))

    u = xf_ref[:, gate_cols] * xf_ref[:, value_cols]
    u_prev = xfp_ref[7:8, gate_cols] * xfp_ref[7:8, value_cols] * not_first
    u_next = xfn_ref[0:1, gate_cols] * xfn_ref[0:1, value_cols] * not_last
    conv = scw_ref[0:1] * back1(u, u_prev) + scw_ref[1:2] * u + scw_ref[2:3] * fwd1(u, u_next)
    zb_ref[...] = (sb_ref[...] * conv).astype(BF16)

    hb = hc_s[1:2]
    for t in range(ch - 1, -1, -1):
        hb = a_s[1, t:t + 1, :] * hb + b_s[1, t:t + 1, :]
        hb_ref[t:t + 1, :] = hb
    hc_s[1:2] = hb

    _attention_item(it, r0t, cls, q_ref, k_ref, v_ref, kc_ref, vc_ref,
                    (bias0_ref, bias1_ref, bias2_ref, bias3_ref), att_ref, band)

    @pl.when(is_ctx)
    def _():
        st_s[pl.ds(2 * b, 1), :] = hf
        st_s[pl.ds(2 * b + 1, 1), :] = hb


def _mixer_tables(n_batch, seq, n_ctx, ch):
    assert n_ctx == ch and ch == ATT_QROWS
    nc = seq // ch
    rows = seq // GRID_W
    kr = min(NA_WIN_ROWS, rows)
    ctx0 = n_batch * seq // ch
    fblk, bblk, first, last, seqb, r0t, cls = [], [], [], [], [], [], []
    for b in range(n_batch):
        fblk.append(ctx0 + b); bblk.append(ctx0 + b); first.append(1); last.append(1); seqb.append(b)
        r0t.extend([0] * ATT_ROWS); cls.extend([kr] * ATT_ROWS)
    for b in range(n_batch):
        for c in range(nc):
            fblk.append(b * nc + c); bblk.append(b * nc + nc - 1 - c)
            first.append(int(c == 0)); last.append(int(c == nc - 1)); seqb.append(b)
            for r in range(c * ATT_ROWS, (c + 1) * ATT_ROWS):
                r0 = min(max(r - kr // 2, 0), rows - kr)
                r0t.append(r0); cls.append(r - r0)
    return [np.asarray(a, np.int32) for a in (fblk, bblk, first, last, seqb, r0t, cls)]


def _token_mixers(q, k, v, bias, sb, rest, sc_w, lru_cw, lru_cb, sp, wr_bd, wi_bd, b_r, b_i, n_batch, seq, n_ctx):
    m = sb.shape[0]
    ch = SCAN_CHUNK
    width = BRANCH_WIDTH
    tables = _mixer_tables(n_batch, seq, n_ctx, ch)
    n_items = len(tables[0])
    halo_per_chunk = ch // SUBLANES
    last_halo = m // SUBLANES - 1
    band = min(NA_WIN_ROWS, seq // GRID_W) * GRID_W
    ctx_blk0 = n_batch * seq // n_ctx

    def cur(col, which, n=1):
        return pl.BlockSpec((ch, n * width), lambda i, f, bk, *_: ((f, bk)[which][i], col))

    def prev(col, which, n=1):
        return pl.BlockSpec((SUBLANES, n * width),
                            lambda i, f, bk, *_: (jnp.maximum((f, bk)[which][i] * halo_per_chunk - 1, 0), col))

    def nxt(col, which, n=1):
        return pl.BlockSpec((SUBLANES, n * width),
                            lambda i, f, bk, *_: (jnp.minimum(((f, bk)[which][i] + 1) * halo_per_chunk, last_halo), col))

    def full(shape):
        return pl.BlockSpec(shape, lambda i, *_: (0,) * len(shape))

    in_specs = [cur(0, 0), cur(0, 0, 3), prev(0, 0, 3), nxt(0, 0, 3),
                cur(2, 1), prev(2, 1), nxt(2, 1),
                full(sc_w.shape), full(lru_cw.shape), full((1, width)), full(sp.shape),
                full(wr_bd.shape), full(wi_bd.shape), full(b_r.shape), full(b_i.shape)]
    in_specs += [cur(0, 0),
                 pl.BlockSpec((seq, width), lambda i, f, bk, fi, la, sq, *_: (sq[i], 0)),
                 pl.BlockSpec((seq, width), lambda i, f, bk, fi, la, sq, *_: (sq[i], 0)),
                 pl.BlockSpec((n_ctx, width), lambda i, f, bk, fi, la, sq, *_: (ctx_blk0 + sq[i], 0)),
                 pl.BlockSpec((n_ctx, width), lambda i, f, bk, fi, la, sq, *_: (ctx_blk0 + sq[i], 0))]
    for row in range(ATT_ROWS):
        in_specs.append(pl.BlockSpec((1, NA_HEADS, GRID_W, band),
                                     lambda i, f, bk, fi, la, sq, r0, cl, row=row: (cl[i * ATT_ROWS + row], 0, 0, 0)))
    out_specs = [cur(0, 0), cur(0, 0), cur(0, 1), cur(0, 0)]
    grid_spec = pltpu.PrefetchScalarGridSpec(
        num_scalar_prefetch=7, grid=(n_items,), in_specs=in_specs, out_specs=out_specs,
        scratch_shapes=[pltpu.VMEM((2, ch, width), F32), pltpu.VMEM((2, ch, width), F32),
                        pltpu.VMEM((SUBLANES, width), F32), pltpu.VMEM((2 * n_batch, width), F32)])
    return pl.pallas_call(
        functools.partial(_mixer_kernel, ch=ch, n_ctx_items=n_batch, band=band),
        grid_spec=grid_spec,
        out_shape=[jax.ShapeDtypeStruct((m, width), BF16),
                   jax.ShapeDtypeStruct((m, width), F32),
                   jax.ShapeDtypeStruct((m, width), F32),
                   jax.ShapeDtypeStruct((m, width), BF16)],
        compiler_params=_params(1),
        name="token_mixers",
    )(*[jnp.asarray(t) for t in tables],
      sb, rest, rest, rest, rest, rest, rest,
      sc_w, lru_cw, lru_cb.reshape(1, width), sp, wr_bd, wi_bd, b_r, b_i,
      q, k, v, k, v, bias, bias, bias, bias)


def _attention_item(it, r0t, cls, q_ref, k_ref, v_ref, kc_ref, vc_ref, bias_refs, o_ref, band):
    nq = GRID_W
    lane = lax.broadcasted_iota(jnp.int32, (nq, LANES), 1)
    low = lane < NA_HEAD_DIM
    scale = NA_HEAD_DIM ** -0.5
    nt = (((1,), (1,)), ((), ()))
    for row in range(ATT_ROWS):
        start = pl.multiple_of(r0t[it * ATT_ROWS + row] * GRID_W, GRID_W)
        bias_ref = bias_refs[row]
        qrows = slice(row * nq, (row + 1) * nq)
        for hp in range(NA_HEADS // 2):
            cols = slice(hp * LANES, (hp + 1) * LANES)
            qp = q_ref[qrows, cols].astype(F32) * scale
            qs = jnp.concatenate([jnp.where(low, qp, 0.0), jnp.where(low, 0.0, qp)], axis=0).astype(BF16)
            kb = k_ref[pl.ds(start, band), cols]
            vb = v_ref[pl.ds(start, band), cols]
            s_loc = lax.dot_general(qs, kb, nt, preferred_element_type=F32)
            s_ctx = lax.dot_general(qs, kc_ref[:, cols], nt, preferred_element_type=F32)
            bias = jnp.concatenate([bias_ref[0, 2 * hp], bias_ref[0, 2 * hp + 1]], axis=0)
            s_loc = s_loc + bias
            mx =jnp.maximum(jnp.max(s_loc, axis=-1, keepdims=True), jnp.max(s_ctx, axis=-1, keepdims=True))
            e_loc = jnp.exp(s_loc - mx)
            e_ctx = jnp.exp(s_ctx - mx)
            den = jnp.sum(e_loc, axis=-1, keepdims=True) + jnp.sum(e_ctx, axis=-1, keepdims=True)
            o = (jnp.dot(e_loc.astype(BF16), vb, preferred_element_type=F32)
                 + jnp.dot(e_ctx.astype(BF16), vc_ref[:, cols], preferred_element_type=F32)) / den
            o_ref[qrows, cols] = jnp.where(low, o[:nq], o[nq:]).astype(BF16)


def _attention_bias(rpb, seq):
    rows = seq // GRID_W
    kr = min(NA_WIN_ROWS, rows)
    kc = NA_WIN_COLS
    cq = np.arange(GRID_W)
    c0 = np.clip(cq - kc // 2, 0, GRID_W - kc)
    ck = np.arange(GRID_W)
    inside = (ck[None, :] >= c0[:, None]) & (ck[None, :] < c0[:, None] + kc)
    dc = np.clip(ck[None, :] - cq[:, None] + (NA_WIN_COLS - 1), 0, 2 * NA_WIN_COLS - 2)
    n_dr = 2 * NA_WIN_ROWS - 1
    n_dc = 2 * NA_WIN_COLS - 1
    pick = jnp.asarray((np.arange(n_dc)[:, None] == dc.reshape(1, -1)).astype(np.float32))
    picked = jnp.dot(rpb.reshape(-1, n_dc), pick, precision=lax.Precision.HIGHEST)
    picked = picked.reshape(NA_HEADS, 2 * NA_WIN_ROWS - 1, GRID_W, GRID_W)
    table = jnp.where(jnp.asarray(inside)[None, None], picked, NEG_BIG)
    table = table.transpose(0, 2, 1, 3).reshape(NA_HEADS, GRID_W, n_dr * GRID_W)
    classes = []
    for cl in range(kr):
        lo = (NA_WIN_ROWS - 1 - cl) * GRID_W
        classes.append(table[:, :, lo:lo + kr * GRID_W])
    classes.append(jnp.full((NA_HEADS, GRID_W, kr * GRID_W), NEG_BIG, F32))
    return jnp.stack(classes, axis=0)


def _post_kernel(ha_ref, hc_ref, att_ref, zb_ref, hf_ref, hb_ref, lg_ref,
                 sh1_ref, sc1_ref, g1_ref, sh2_ref, sc2_ref,
                 wgl0_ref, wgl1_ref, wgl2_ref, bgl0_ref, bgl1_ref, bgl2_ref,
                 wpa_ref, wpc_ref, wpl_ref, wo_ref, bo_ref, l1g_ref, l1b_ref,
                 rw_ref, rb_ref, tri_ref, upper_ref,
                 h1_ref, xs_ref, pos_ref, cnt_ref, u_s, logit_s, *, split, n_tiles):
    i = pl.program_id(0)

    @pl.when(i == 0)
    def _():
        u_s[...] = jnp.zeros_like(u_s)
        logit_s[...] = jnp.zeros_like(logit_s)

    _route_and_sort(u_s[...], logit_s[...], tri_ref, upper_ref, xs_ref, pos_ref, cnt_ref)

    h = jnp.where(jnp.minimum(i, n_tiles - 1) < split, ha_ref[...], hc_ref[...])
    u1 = (_layer_norm(h) * (1.0 + sc1_ref[0]) + sh1_ref[0]).astype(BF16)
    y_a = jnp.dot(att_ref[...], wpa_ref[...], preferred_element_type=F32)
    y_b = jnp.dot(zb_ref[...], wpc_ref[...], preferred_element_type=F32)
    zc = jax.nn.gelu(lg_ref[...]) * (hf_ref[...] + hb_ref[...])
    y_c = jnp.dot(zc.astype(BF16), wpl_ref[...], preferred_element_type=F32)
    merged = (jax.nn.sigmoid(jnp.dot(u1, wgl0_ref[0], preferred_element_type=F32) + bgl0_ref[0]) * y_a
              + jax.nn.sigmoid(jnp.dot(u1, wgl1_ref[0], preferred_element_type=F32) + bgl1_ref[0]) * y_b
              + jax.nn.sigmoid(jnp.dot(u1, wgl2_ref[0], preferred_element_type=F32) + bgl2_ref[0]) * y_c)
    y = jnp.dot(merged.astype(BF16), wo_ref[...], preferred_element_type=F32) + bo_ref[...]
    h1 = _layer_norm(DEEPNORM_ALPHA * h + g1_ref[0] * y) * l1g_ref[...] + l1b_ref[...]
    h1_ref[...] = h1
    u2 = _layer_norm(h1) * (1.0 + sc2_ref[0]) + sh2_ref[0]

    u_hi = u2.astype(BF16)
    u_lo = (u2 - u_hi.astype(F32)).astype(BF16)
    by_hi = jnp.dot(u_hi, rw_ref[...], preferred_element_type=F32)
    logit_s[...] = (by_hi[:, :LANES] + by_hi[:, LANES:]
                    + jnp.dot(u_lo, rw_ref[:, :LANES], preferred_element_type=F32) + rb_ref[...])
    u_s[...] = u_hi


def _route_and_sort(u_hi, logits, tri_ref, upper_ref, xs_ref, pos_ref, cnt_ref):
    tm = logits.shape[0]
    lane = lax.broadcasted_iota(jnp.int32, (tm, LANES), 1)
    lane_f = lane.astype(F32)
    work = logits
    tops, idxs, hots = [], [], []
    for _ in range(TOP_K):
        mx = jnp.max(work, axis=-1, keepdims=True)
        idx = jnp.min(jnp.where(work == mx, lane_f, float(LANES)), axis=-1, keepdims=True)
        hot = lane_f == idx
        work = jnp.where(hot, -3e38, work)
        tops.append(mx); idxs.append(idx); hots.append(hot)
    exps = [jnp.exp(t - tops[0]) for t in tops]
    den = exps[0] + exps[1] + exps[2] + exps[3]
    hot_all = jnp.zeros((tm, LANES), F32)
    for hot in hots:
        hot_all = hot_all + hot.astype(F32)
    cnt = jnp.sum(hot_all, axis=0, keepdims=True)
    cnt_pad = jnp.floor((cnt + (SUBLANES - 1.0)) * (1.0 / SUBLANES)) * SUBLANES
    off = jnp.dot(jnp.broadcast_to(cnt_pad, (SUBLANES, LANES)), upper_ref[...],
                  precision=lax.Precision.HIGHEST, preferred_element_type=F32)[0:1]
    slot = off + jnp.dot(tri_ref[...], hot_all.astype(BF16), preferred_element_type=F32)
    pos4 = jnp.zeros((tm, LANES), F32)
    w_tile = jnp.zeros((tm, LANES), F32)
    for kk in range(TOP_K):
        pos_k = jnp.sum(jnp.where(hots[kk], slot, 0.0), axis=-1, keepdims=True)
        pos4 = jnp.where(lane == kk, pos_k, pos4)
        p = exps[kk] / den
        p_hi = p.astype(BF16).astype(F32)
        p_mid = (p - p_hi).astype(BF16).astype(F32)
        p_lo = p - p_hi - p_mid
        w_tile = jnp.where(hots[kk], p_hi, w_tile)
        w_tile = jnp.where(lane_f == idxs[kk] + float(N_EXPERTS), p_mid, w_tile)
        w_tile = jnp.where(lane_f == idxs[kk] + float(2 * N_EXPERTS), p_lo, w_tile)
    pos_ref[...] = pos4
    sub = lax.broadcasted_iota(jnp.int32, (SUBLANES, LANES), 0)
    cnt_ref[...] = jnp.where(sub == 0, cnt, jnp.where(sub == 1, off, 0.0))

    n_sorted = xs_ref.shape[0]
    pos_t = pos4.T
    r_iota = lax.broadcasted_iota(jnp.int32, (n_sorted, tm), 0).astype(F32)
    hit = r_iota == pos_t[0:1, :]
    for kk in range(1, TOP_K):
        hit = jnp.logical_or(hit, r_iota == pos_t[kk:kk + 1, :])
    perm = jnp.where(hit, 1.0, 0.0).astype(BF16)
    feats = jnp.concatenate([u_hi, w_tile.astype(BF16)], axis=1)
    xs_ref[...] = jnp.dot(perm, feats, preferred_element_type=F32)


def _post_mixer(h_parts, att, zb, hf, hb, rest, mod3, mod_base, layer, w_in_bf, b_in, wpa, wpc, wpl, wo, b_o, l1g,
                l1b, rw_pad, rb_pad, n_rows, n_lat, seq):
    tm = TM_POST
    width = BRANCH_WIDTH
    ha, hc = h_parts[0], h_parts[-1]
    split = min(ha.shape[0], n_rows) // tm
    gate_col0 = N_EARLY // D_MODEL
    tiles_per_seq = seq // tm
    n_lat_tiles = n_lat // tm
    n_groups_lat = n_lat // seq
    tri = jnp.asarray(np.tril(np.ones((tm, tm), np.float32), -1), BF16)
    upper = jnp.asarray(np.triu(np.ones((LANES, LANES), np.float32), 1))
    n_tiles = n_rows // tm

    def dense(i):
        return jnp.minimum(i, n_tiles - 1)

    def routed(i):
        return jnp.maximum(i - 1, 0)

    def group(i):
        return jnp.where(dense(i) < n_lat_tiles, dense(i) // tiles_per_seq, n_groups_lat)

    def rows(wd, col=0):
        return pl.BlockSpec((tm, wd), lambda i: (dense(i), col))

    def full(shape):
        return pl.BlockSpec(shape, lambda i: (0,) * len(shape))

    def mod(which):
        return pl.BlockSpec((1, 1, D_MODEL), lambda i: (mod_base + group(i) * 6 + which, 0, 0))

    in_specs = [pl.BlockSpec((tm, D_MODEL), lambda i: (jnp.minimum(dense(i), split - 1), 0)),
                pl.BlockSpec((tm, D_MODEL), lambda i: (jnp.maximum(dense(i) - split, 0), 0)),
                rows(width), rows(width), rows(width), rows(width), rows(width, 3),
                mod(0), mod(1), mod(2), mod(3), mod(4)]
    in_specs += [pl.BlockSpec((1, D_MODEL, D_MODEL), lambda i, c=c: (layer, 0, gate_col0 + c)) for c in range(3)]
    in_specs += [pl.BlockSpec((1, 1, D_MODEL), lambda i, c=c: (layer, 0, gate_col0 + c)) for c in range(3)]
    in_specs += [full(wpa.shape), full(wpc.shape), full(wpl.shape), full(wo.shape), full((1, D_MODEL)),
                full((1, D_MODEL)), full((1, D_MODEL)), full(rw_pad.shape), full(rb_pad.shape), full(tri.shape),
                full(upper.shape)]
    out_specs = [rows(D_MODEL), pl.BlockSpec((TOK_BLOCK, XS_WIDTH), lambda i: (routed(i), 0)),
                 pl.BlockSpec((tm, LANES), lambda i: (routed(i), 0)),
                 pl.BlockSpec((SUBLANES, LANES), lambda i: (routed(i), 0))]
    return pl.pallas_call(
        functools.partial(_post_kernel, split=split, n_tiles=n_tiles),
        grid=(n_tiles + 1,),
        in_specs=in_specs, out_specs=out_specs,
        out_shape=[jax.ShapeDtypeStruct((n_rows, D_MODEL), F32),
                   jax.ShapeDtypeStruct((n_tiles * TOK_BLOCK, XS_WIDTH), F32),
                   jax.ShapeDtypeStruct((n_rows, LANES), F32),
                   jax.ShapeDtypeStruct((n_tiles * SUBLANES, LANES), F32)],
        scratch_shapes=[pltpu.VMEM((tm, D_MODEL), BF16), pltpu.VMEM((tm, LANES), F32)],
        compiler_params=_params(1),
        name="post_mixer",
    )(ha, hc, att, zb, hf, hb, rest, mod3, mod3, mod3, mod3, mod3,
      w_in_bf, w_in_bf, w_in_bf, b_in.reshape(DEPTH, 1, -1), b_in.reshape(DEPTH, 1, -1), b_in.reshape(DEPTH, 1, -1),
      wpa, wpc, wpl, wo, b_o.reshape(1, D_MODEL), l1g.reshape(1, D_MODEL), l1b.reshape(1, D_MODEL),
      rw_pad, rb_pad, tri, upper)


def _expert_plan(cnt_out, n_tok_tiles, n_rows):
    tm = TM_EXPERT
    co = cnt_out.reshape(n_tok_tiles, SUBLANES, LANES)
    cnt = (co[:, 0, :N_EXPERTS].astype(jnp.int32) + SUBLANES - 1) // SUBLANES * SUBLANES
    off = co[:, 1, :N_EXPERTS].astype(jnp.int32)
    cum_end = jnp.cumsum(cnt, axis=0)
    cum = cum_end - cnt
    total = cum_end[-1]
    n_et = (total + tm - 1) // tm
    et_end = jnp.cumsum(n_et)
    n_act = et_end[-1:].astype(jnp.int32)
    n_tiles = -(-n_tok_tiles * TOK_BLOCK // tm) + N_EXPERTS
    j = jnp.arange(n_tiles, dtype=jnp.int32)
    tile_e = jnp.minimum(jnp.sum((et_end[None, :] <= j[:, None]).astype(jnp.int32), axis=1), N_EXPERTS - 1)
    pick_e = (tile_e[:, None] == jnp.arange(N_EXPERTS, dtype=jnp.int32)[None, :]).astype(F32)

    def per_tile(table):
        return jnp.dot(pick_e, table.astype(F32), precision=lax.Precision.HIGHEST).astype(jnp.int32)

    row0 = (j - per_tile(et_end - n_et)) * tm
    n_rows_tile = jnp.clip(per_tile(total) - row0, 0, tm)
    cum_e = per_tile(cum.T)
    cum_end_e = per_tile(cum_end.T)
    delta_e = per_tile((jnp.arange(n_tok_tiles, dtype=jnp.int32)[:, None] * TOK_BLOCK + off - cum).T)
    q = row0[:, None] + SUBLANES * jnp.arange(tm // SUBLANES, dtype=jnp.int32)[None, :]
    inside = jnp.logical_and(cum_e.T[:, :, None] <= q[None], q[None] < cum_end_e.T[:, :, None])
    src = q + jnp.sum(jnp.where(inside, delta_e.T[:, :, None], 0), axis=0)
    used = off[:, N_EXPERTS - 1] + cnt[:, N_EXPERTS - 1]
    first = jnp.concatenate([jnp.ones((1,), jnp.int32), (tile_e[1:] != tile_e[:-1]).astype(jnp.int32)])
    group = jnp.cumsum(first) - 1
    after = per_tile(et_end)
    next_e = jnp.where(after < n_act[0],
                       jnp.minimum(jnp.sum((et_end[None, :] <= after[:, None]).astype(jnp.int32), axis=1),
                                   N_EXPERTS - 1), -1)
    return (tile_e.astype(jnp.int32), n_rows_tile.astype(jnp.int32), n_act, src.reshape(-1).astype(jnp.int32),
            used.astype(jnp.int32), first, group.astype(jnp.int32), next_e.astype(jnp.int32), n_tiles)


def _expert_kernel(tile_e, n_rows_t, n_act, src_t, used_t, first_t, group_t, next_t,
                   xs_hbm, wgu_hbm, bgu_ref, wdn_hbm, bdn_ref, ys_hbm,
                   xin, yout, wgu_f, wdn_f, wgu_s, wdn_s, zeros, sem_in, sem_out, sem_zero, sem_wgu, sem_wdn,
                   *, tm, tok_block, n_tok_tiles, layer):
    j = pl.program_id(0)
    na = n_act[0]
    chunks = tm // SUBLANES

    def weight_copies(e, wslot):
        return (pltpu.make_async_copy(wgu_hbm.at[layer, e], wgu_f.at[wslot], sem_wgu.at[wslot]),
                pltpu.make_async_copy(wdn_hbm.at[layer, e], wdn_f.at[wslot], sem_wdn.at[wslot]))

    def gather(src, dst, size, slot):
        return pltpu.make_async_copy(xs_hbm.at[pl.ds(src, size)], xin.at[slot, pl.ds(dst, size)], sem_in.at[slot])

    def scatter(src, dst, size, slot):
        return pltpu.make_async_copy(yout.at[slot, pl.ds(dst, size)], ys_hbm.at[pl.ds(src, size)], sem_out.at[slot])

    def start_chunks(jj, slot, copy):
        def one(c, priority):
            src = pl.multiple_of(src_t[jj * chunks + c], SUBLANES)
            copy(src, pl.multiple_of(c * SUBLANES, SUBLANES), SUBLANES, slot).start(priority=priority)

        def body(c, carry):
            one(c, 0)
            return carry

        def body_unrolled(g, carry):
            for u in range(SUBLANES):
                one(g * SUBLANES + u, u % 2)
            return carry

        n = lax.shift_right_logical(n_rows_t[jj], 3)

        @pl.when(n == chunks)
        def _():
            lax.fori_loop(0, chunks // SUBLANES, body_unrolled, 0)

        @pl.when(n != chunks)
        def _():
            lax.fori_loop(0, n, body, 0)

    def wait_rows(jj, slot, copy):
        n = n_rows_t[jj]
        size = tm
        while size >= SUBLANES:
            @pl.when((n & size) != 0)
            def _(size=size):
                copy(0, 0, size, slot).wait()
            size //= 2

    slot = j % 2

    @pl.when(j == 0)
    def _():
        xin[...] = jnp.zeros_like(xin)
        start_chunks(0, 0, gather)
        zeros[...] = jnp.zeros_like(zeros)

        def clear_tail(i, copy_op):
            used = used_t[i]
            tail = tok_block - used
            size = ZERO_ROWS
            sizes = []
            while size >= SUBLANES:
                sizes.append(size)
                size //= 2
            for size in sizes:
                @pl.when((tail & size) != 0)
                def _(size=size):
                    at = pl.multiple_of(i * tok_block + used + (tail & ~(2 * size - 1)), SUBLANES)
                    copy_op(pltpu.make_async_copy(zeros.at[pl.ds(0, size)], ys_hbm.at[pl.ds(at, size)], sem_zero))

        def start_clear(i, c):
            clear_tail(i, lambda cp: cp.start())
            return c

        def wait_clear(i, c):
            clear_tail(i, lambda cp: cp.wait())
            return c

        lax.fori_loop(0, n_tok_tiles, start_clear, 0)
        lax.fori_loop(0, n_tok_tiles, wait_clear, 0)

    @pl.when(j + 1 < na)
    def _():
        start_chunks(j + 1, 1 - slot, gather)

    @pl.when(j < na)
    def _():
        e = tile_e[j]

        @pl.when(first_t[j] == 1)
        def _():
            wslot = group_t[j] % 2

            @pl.when(j == 0)
            def _():
                for cp in weight_copies(e, wslot):
                    cp.start()

            for cp in weight_copies(e, wslot):
                cp.wait()
            wgu_s[...] = wgu_f[wslot].astype(BF16)
            wdn_s[...] = wdn_f[wslot].astype(BF16)

            @pl.when(next_t[j] >= 0)
            def _():
                for cp in weight_copies(next_t[j], 1 - wslot):
                    cp.start()

        wait_rows(j, slot, gather)

        def compute(rows):
            x = xin[slot, pl.ds(0, rows)]
            lane = lax.broadcasted_iota(jnp.int32, (rows, LANES), 1)
            p = jnp.sum(jnp.where(lane % N_EXPERTS == e, x[:, D_MODEL:], 0.0), axis=-1, keepdims=True)
            gu = (jnp.dot(x[:, :D_MODEL].astype(BF16), wgu_s[...], preferred_element_type=F32)
                  + bgu_ref[0, pl.ds(e, 1), :])
            f = gu.shape[1] // 2
            gate = jnp.minimum(gu[:, :f], SWIGLU_LIMIT)
            up = jnp.clip(gu[:, f:], -SWIGLU_LIMIT, SWIGLU_LIMIT)
            hid = (up + 1.0) * gate * jax.nn.sigmoid(SWIGLU_ALPHA * gate)
            y = (jnp.dot(hid.astype(BF16), wdn_s[...], preferred_element_type=F32)
                 + bdn_ref[0, pl.ds(e, 1), :])
            yout[slot, pl.ds(0, rows)] = y * p

        quarter = tm // 4
        for part in range(1, 5):
            @pl.when(jnp.logical_and(n_rows_t[j] > (part - 1) * quarter, n_rows_t[j] <= part * quarter))
            def _(part=part):
                compute(part * quarter)

        start_chunks(j, slot, scatter)

        @pl.when(j >= 1)
        def _():
            wait_rows(j - 1, 1 - slot, scatter)

        @pl.when(j == na - 1)
        def _():
            wait_rows(j, slot, scatter)


def _experts(xs, plan, layer, w_gu, b_gu, w_dn, b_dn):
    tile_e, n_rows_tile, n_act, src, used, first, group, next_e, n_tiles = plan
    tm = TM_EXPERT
    f2 = w_gu.shape[-1]

    in_specs = [pl.BlockSpec(memory_space=pl.ANY),
                pl.BlockSpec(memory_space=pl.ANY),
                pl.BlockSpec((1, N_EXPERTS, f2), lambda j, *_: (layer, 0, 0)),
                pl.BlockSpec(memory_space=pl.ANY),
                pl.BlockSpec((1, N_EXPERTS, D_MODEL), lambda j, *_: (layer, 0, 0))]
    grid_spec = pltpu.PrefetchScalarGridSpec(
        num_scalar_prefetch=8, grid=(n_tiles,), in_specs=in_specs,
        out_specs=pl.BlockSpec(memory_space=pl.ANY),
        scratch_shapes=[pltpu.VMEM((2, tm, XS_WIDTH), F32), pltpu.VMEM((2, tm, D_MODEL), F32),
                        pltpu.VMEM((2, D_MODEL, f2), F32), pltpu.VMEM((2, f2 // 2, D_MODEL), F32),
                        pltpu.VMEM((D_MODEL, f2), BF16), pltpu.VMEM((f2 // 2, D_MODEL), BF16),
                        pltpu.VMEM((ZERO_ROWS, D_MODEL), F32),
                        pltpu.SemaphoreType.DMA((2,)), pltpu.SemaphoreType.DMA((2,)), pltpu.SemaphoreType.DMA(()),
                        pltpu.SemaphoreType.DMA((2,)), pltpu.SemaphoreType.DMA((2,))])
    return pl.pallas_call(
        functools.partial(_expert_kernel, tm=tm, tok_block=TOK_BLOCK, n_tok_tiles=xs.shape[0] // TOK_BLOCK,
                          layer=layer),
        grid_spec=grid_spec,
        out_shape=jax.ShapeDtypeStruct((xs.shape[0], D_MODEL), F32),
        compiler_params=_params(1),
        name="moe_experts",
    )(tile_e, n_rows_tile, n_act, src, used, first, group, next_e,
      xs, w_gu, b_gu, w_dn, b_dn)


def _combine_kernel(ys_ref, pos_ref, h1_ref, g2_ref, l2g_ref, l2b_ref, o_ref):
    tm = pos_ref.shape[0]
    n_sorted = ys_ref.shape[0]
    pos = pos_ref[...]
    col = lax.broadcasted_iota(jnp.int32, (tm, n_sorted), 1).astype(F32)
    sel = jnp.zeros((tm, n_sorted), F32)
    for kk in range(TOP_K):
        sel = sel + (col == pos[:, kk:kk + 1]).astype(F32)
    sel = sel.astype(BF16)
    ys = ys_ref[...]
    hi = ys.astype(BF16)
    rest = ys - hi.astype(F32)
    mid = rest.astype(BF16)
    lo = (rest - mid.astype(F32)).astype(BF16)
    y2 = (jnp.dot(sel, hi, preferred_element_type=F32) + jnp.dot(sel, mid, preferred_element_type=F32)
          + jnp.dot(sel, lo, preferred_element_type=F32))
    o_ref[...] = _layer_norm(DEEPNORM_ALPHA * h1_ref[...] + g2_ref[0] * y2) * l2g_ref[...] + l2b_ref[...]


def _combine(ys, pos4, h1, mod3, mod_base, l2g, l2b, n_lat, seq):
    n_rows = h1.shape[0]
    tm = TM_POST
    tiles_per_seq = seq // tm
    n_lat_tiles = n_lat // tm
    n_groups_lat = n_lat // seq

    def group(i):
        return jnp.where(i < n_lat_tiles, i // tiles_per_seq, n_groups_lat)

    in_specs = [pl.BlockSpec((TOK_BLOCK, D_MODEL), lambda i: (i, 0)),
                pl.BlockSpec((tm, LANES), lambda i: (i, 0)),
                pl.BlockSpec((tm, D_MODEL), lambda i: (i, 0)),
                pl.BlockSpec((1, 1, D_MODEL), lambda i: (mod_base + group(i) * 6 + 5, 0, 0)),
                pl.BlockSpec((1, D_MODEL), lambda i: (0, 0)),
                pl.BlockSpec((1, D_MODEL), lambda i: (0, 0))]
    return pl.pallas_call(
        _combine_kernel,
        grid=(n_rows // tm,),
        in_specs=in_specs,
        out_specs=pl.BlockSpec((tm, D_MODEL), lambda i: (i, 0)),
        out_shape=jax.ShapeDtypeStruct((n_rows, D_MODEL), F32),
        compiler_params=_params(1),
        name="moe_combine",
    )(ys, pos4, h1, mod3, l2g.reshape(1, D_MODEL), l2b.reshape(1, D_MODEL))


def _block_diag(w):
    two, n, d, e = w.shape
    eye = jnp.eye(n, dtype=w.dtype)
    return (w[:, :, :, None, :] * eye[None, :, None, :, None]).reshape(two, n * d, n * e)


def kernel(x, c, ctx, c_ctx, w_mod, b_mod, w_in, b_in, na_rpb, w_proj_attn, w_proj_conv, w_proj_lru, sc_conv_w, lru_conv_w, lru_conv_b, lru_lambda, lru_w_r, lru_b_r, lru_w_i, lru_b_i, w_o, b_o, ln1_g, ln1_b, router_w, router_b, exp_w_gu, exp_b_gu, exp_w_dn, exp_b_dn, ln2_g, ln2_b):
    n_batch, seq, d = x.shape
    n_ctx = ctx.shape[1]
    n_lat = n_batch * seq
    n_all = n_lat + n_batch * n_ctx
    assert d == D_MODEL and n_batch + 1 <= SUBLANES

    cc = jnp.concatenate([c, c_ctx[None], jnp.zeros((SUBLANES - n_batch - 1, d), F32)], axis=0)
    mod = _modulation(cc, w_mod, b_mod)
    groups = n_batch + 1
    mod3 = mod.reshape(DEPTH, SUBLANES, 6, d)[:, :groups].reshape(DEPTH * groups * 6, 1, d)

    cos_t, sin_t = _make_rope(seq, TM_INPROJ)
    h = (x.reshape(n_lat, d), ctx.reshape(n_batch * n_ctx, d))
    w_in_bf = w_in.astype(BF16)

    for layer in range(DEPTH):
        last = layer == DEPTH - 1
        mod_base = layer * groups * 6
        q, k, v, sb, rest = _input_projection(h, mod3, mod_base, layer, w_in_bf, b_in, N_EARLY,
                                              cos_t, sin_t, n_lat, seq)
        sp = jax.nn.softplus(-lru_lambda[layer])
        zb, hf, hb, att = _token_mixers(
            q, k, v, _attention_bias(na_rpb[layer], seq), sb, rest, sc_conv_w[layer], lru_conv_w[layer],
            lru_conv_b[layer], sp, _block_diag(lru_w_r[layer]).astype(BF16), _block_diag(lru_w_i[layer]).astype(BF16),
            lru_b_r[layer], lru_b_i[layer], n_batch, seq, n_ctx)
        n_rows = n_lat if last else n_all
        rw_full = jnp.pad(router_w[layer], ((0, 0), (0, LANES - N_EXPERTS)))
        rw_hi = rw_full.astype(BF16)
        rw_pad = jnp.concatenate([rw_hi, (rw_full - rw_hi.astype(F32)).astype(BF16)], axis=1)
        rb_pad = jnp.concatenate([router_b[layer], jnp.full((LANES - N_EXPERTS,), NEG_BIG, F32)]).reshape(1, LANES)
        h1, xs, pos4, cnt_out = _post_mixer(
            h, att, zb, hf, hb, rest, mod3, mod_base, layer, w_in_bf, b_in,
            w_proj_attn[layer].astype(BF16), w_proj_conv[layer].astype(BF16), w_proj_lru[layer].astype(BF16),
            w_o[layer].astype(BF16), b_o[layer], ln1_g[layer], ln1_b[layer], rw_pad, rb_pad, n_rows, n_lat, seq)
        plan = _expert_plan(cnt_out, n_rows // TM_POST, n_rows)
        ys = _experts(xs, plan, layer, exp_w_gu, exp_b_gu, exp_w_dn, exp_b_dn)
        h = (_combine(ys, pos4, h1, mod3, mod_base, ln2_g[layer], ln2_b[layer], n_lat, seq),)
    return h[0].reshape(n_batch, seq, d)
```

```python
import functools

import numpy as np
import jax
import jax.numpy as jnp
from jax import lax
from jax.experimental import pallas as pl
from jax.experimental.pallas import tpu as pltpu

D_MODEL = 1024
DEPTH = 2
GRID_W = 64
NA_HEADS = 8
NA_HEAD_DIM = 64
NA_WIN_ROWS = 8
NA_WIN_COLS = 16
ROPE_BASE = 10000.0
BRANCH_WIDTH = 512
LRU_C = 8.0
N_EARLY = 8 * BRANCH_WIDTH
N_EXPERTS = 32
TOP_K = 4
SWIGLU_LIMIT = 7.0
SWIGLU_ALPHA = 1.702
LN_EPS = 1e-5
DEEPNORM_ALPHA = (2 * DEPTH) ** 0.25
NEG_BIG = -1e30

LANES = 128
SUBLANES = 8
VMEM_LIMIT_BYTES = 56 * 1024 * 1024

TM_INPROJ = 512
SCAN_CHUNK = 256
ATT_ROWS = 4
ATT_QROWS = ATT_ROWS * GRID_W
TM_POST = 256
TM_EXPERT = 512
XS_WIDTH = D_MODEL + LANES
TOK_BLOCK = TM_POST * TOP_K + N_EXPERTS * SUBLANES
ZERO_ROWS = TOK_BLOCK - TM_POST * TOP_K

F32 = jnp.float32
BF16 = jnp.bfloat16


def _params(n_axes):
    return pltpu.CompilerParams(dimension_semantics=("arbitrary",) * n_axes,
                                vmem_limit_bytes=VMEM_LIMIT_BYTES)


def _layer_norm(x):
    mu = jnp.mean(x, axis=-1, keepdims=True)
    xc = x - mu
    var = jnp.mean(xc * xc, axis=-1, keepdims=True)
    return xc * lax.rsqrt(var + LN_EPS)


def _mod_kernel(c_ref, w_ref, b_ref, o_ref):
    c = c_ref[...]
    s = (c * jax.nn.sigmoid(c)).astype(BF16)
    o_ref[0] = jnp.dot(s, w_ref[0].astype(BF16), preferred_element_type=F32) + b_ref[0]


def _modulation(cc, w_mod, b_mod):
    n_out = w_mod.shape[-1]
    return pl.pallas_call(
        _mod_kernel,
        grid=(DEPTH, n_out // D_MODEL),
        in_specs=[pl.BlockSpec((SUBLANES, D_MODEL), lambda l, j: (0, 0)),
                  pl.BlockSpec((1, D_MODEL, D_MODEL), lambda l, j: (l, 0, j)),
                  pl.BlockSpec((1, 1, D_MODEL), lambda l, j: (l, 0, j))],
        out_specs=pl.BlockSpec((1, SUBLANES, D_MODEL), lambda l, j: (l, 0, j)),
        out_shape=jax.ShapeDtypeStruct((DEPTH, SUBLANES, n_out), F32),
        compiler_params=_params(2),
        name="modulation",
    )(cc, w_mod, b_mod.reshape(DEPTH, 1, n_out))


def _rope_half(x, cos, sin_signed):
    m = NA_HEAD_DIM // 4
    lane = lax.broadcasted_iota(jnp.int32, (x.shape[0], LANES), 1)
    first = (lane % (2 * m)) < m
    outs = []
    for cidx in range(x.shape[1] // LANES):
        xc = x[:, cidx * LANES:(cidx + 1) * LANES]
        partner = jnp.where(first, pltpu.roll(xc, LANES - m, 1), pltpu.roll(xc, m, 1))
        outs.append(xc * cos + partner * sin_signed)
    return jnp.concatenate(outs, axis=1)


def _inproj_kernel(ha_ref, hb_ref, sh_ref, sc_ref, w_ref, b_ref, cos_ref, sin_ref,
                   q_ref, k_ref, v_ref, sb_ref, rest_ref, *, split):
    h = jnp.where(pl.program_id(0) < split, ha_ref[...], hb_ref[...])
    xn = (_layer_norm(h) * (1.0 + sc_ref[0]) + sh_ref[0]).astype(BF16)
    half = BRANCH_WIDTH

    def columns(lo, width):
        return jnp.dot(xn, w_ref[0, :, lo:lo + width], preferred_element_type=F32) + b_ref[0, :, lo:lo + width]

    cos = cos_ref[...]
    sin = sin_ref[...]
    q_ref[...] = _rope_half(columns(0, half), cos, sin).astype(BF16)
    k_ref[...] = _rope_half(columns(half, half), cos, sin).astype(BF16)
    v_ref[...] = columns(2 * half, half).astype(BF16)
    sb_ref[...] = columns(3 * half, half)
    n_rest = rest_ref.shape[1]
    for lo in range(0, n_rest, 2 * half):
        rest_ref[:, lo:lo + 2 * half] = columns(4 * half + lo, 2 * half)


def _input_projection(h_parts, mod3, mod_base, layer, w_in_bf, b_in, n_cols, cos_t, sin_t, n_lat, seq):
    tm = TM_INPROJ
    ha, hb = h_parts[0], h_parts[-1]
    split = ha.shape[0] // tm
    m = ha.shape[0] + (hb.shape[0] if len(h_parts) == 2 else 0)
    n_lat_tiles = n_lat // tm
    tiles_per_seq = seq // tm
    n_groups_lat = n_lat // seq

    def group(i):
        return jnp.where(i < n_lat_tiles, i // tiles_per_seq, n_groups_lat)

    def rope_blk(i):
        return jnp.where(i < n_lat_tiles, i % tiles_per_seq, tiles_per_seq)

    half = BRANCH_WIDTH
    return pl.pallas_call(
        functools.partial(_inproj_kernel, split=split),
        grid=(m // tm,),
        in_specs=[pl.BlockSpec((tm, D_MODEL), lambda i: (jnp.minimum(i, split - 1), 0)),
                  pl.BlockSpec((tm, D_MODEL), lambda i: (jnp.maximum(i - split, 0), 0)),
                  pl.BlockSpec((1, 1, D_MODEL), lambda i: (mod_base + group(i) * 6 + 0, 0, 0)),
                  pl.BlockSpec((1, 1, D_MODEL), lambda i: (mod_base + group(i) * 6 + 1, 0, 0)),
                  pl.BlockSpec((1, D_MODEL, n_cols), lambda i: (layer, 0, 0)),
                  pl.BlockSpec((1, 1, n_cols), lambda i: (layer, 0, 0)),
                  pl.BlockSpec((tm, LANES), lambda i: (rope_blk(i), 0)),
                  pl.BlockSpec((tm, LANES), lambda i: (rope_blk(i), 0))],
        out_specs=[pl.BlockSpec((tm, half), lambda i: (i, 0)),
                   pl.BlockSpec((tm, half), lambda i: (i, 0)),
                   pl.BlockSpec((tm, half), lambda i: (i, 0)),
                   pl.BlockSpec((tm, half), lambda i: (i, 0)),
                   pl.BlockSpec((tm, n_cols - 4 * half), lambda i: (i, 0))],
        out_shape=[jax.ShapeDtypeStruct((m, half), BF16),
                   jax.ShapeDtypeStruct((m, half), BF16),
                   jax.ShapeDtypeStruct((m, half), BF16),
                   jax.ShapeDtypeStruct((m, half), F32),
                   jax.ShapeDtypeStruct((m, n_cols - 4 * half), F32)],
        compiler_params=_params(1),
        name="input_projection",
    )(ha, hb, mod3, mod3, w_in_bf, b_in.reshape(DEPTH, 1, -1), cos_t, sin_t)


def _make_rope(seq, tm):
    rows = seq // GRID_W
    d = np.arange(LANES) % NA_HEAD_DIM
    m = NA_HEAD_DIM // 4
    inv_freq = (ROPE_BASE ** (-jnp.arange(m, dtype=F32) / m))[d % m]
    n_pos = max(rows, GRID_W)
    ang = jnp.arange(n_pos, dtype=F32)[:, None] * inv_freq[None, :]
    by_row = jnp.asarray(d < 2 * m)[None, None, :]

    def expand(tab):
        full = jnp.where(by_row, tab[:rows, None, :], tab[None, :GRID_W, :])
        return full.reshape(seq, LANES)

    cos = expand(jnp.cos(ang))
    sin = expand(jnp.sin(ang))
    sin_signed = jnp.where(jnp.asarray((d % (2 * m)) < m)[None, :], -sin, sin)
    cos = jnp.concatenate([cos, jnp.ones((tm, LANES), F32)], axis=0)
    sin_signed = jnp.concatenate([sin_signed, jnp.zeros((tm, LANES), F32)], axis=0)
    return cos, sin_signed


def _mixer_kernel(fblk, bblk, first, last, seqb, r0t, cls,
                  sb_ref, xf_ref, xfp_ref, xfn_ref, lxb_ref, lxbp_ref, lxbn_ref,
                  scw_ref, cw_ref, cb_ref, sp_ref, wr_ref, wi_ref, br_ref, bi_ref,
                  q_ref, k_ref, v_ref, kc_ref, vc_ref, bias0_ref, bias1_ref, bias2_ref, bias3_ref,
                  zb_ref, hf_ref, hb_ref, att_ref,
                  a_s, b_s, hc_s, st_s, *, ch, n_ctx_items, band):
    it = pl.program_id(0)
    is_first = first[it] == 1
    is_last = last[it] == 1
    is_ctx = it < n_ctx_items
    b = seqb[it]
    width = BRANCH_WIDTH
    row = lax.broadcasted_iota(jnp.int32, (ch, width), 0)
    not_first = jnp.where(is_first, 0.0, 1.0).astype(F32)
    not_last = jnp.where(is_last, 0.0, 1.0).astype(F32)

    def back1(u, prev_row):
        return jnp.where(row == 0, prev_row, pltpu.roll(u, 1, 0))

    def back2(u, prev2, prev1):
        return jnp.where(row == 0, prev2, jnp.where(row == 1, prev1, pltpu.roll(u, 2, 0)))

    def fwd1(u, next_row):
        return jnp.where(row == ch - 1, next_row, pltpu.roll(u, ch - 1, 0))

    gate_cols, value_cols, lru_cols = (slice(s * width, (s + 1) * width) for s in range(3))

    def lru_input(x, p, n, prev_ok, next_ok):
        p = p * prev_ok
        n = n * next_ok
        return (cw_ref[0:1] * back2(x, p[6:7], p[7:8]) + cw_ref[1:2] * back1(x, p[7:8])
                + cw_ref[2:3] * x + cw_ref[3:4] * fwd1(x, n[0:1]) + cb_ref[...])

    def coeffs(d, xm):
        xb = xm.astype(BF16)
        r = jax.nn.sigmoid(jnp.dot(xb, wr_ref[d], preferred_element_type=F32) + br_ref[d:d + 1])
        g = jax.nn.sigmoid(jnp.dot(xb, wi_ref[d], preferred_element_type=F32) + bi_ref[d:d + 1])
        log_a = (-LRU_C * sp_ref[d:d + 1]) * r
        a = jnp.exp(log_a)
        a_s[d] = a
        b_s[d] = jnp.sqrt(-jnp.tanh(log_a) * (a * a + 1.0)) * (g * xm)

    @pl.when(jnp.logical_and(is_first, is_ctx))
    def _():
        hc_s[...] = jnp.zeros_like(hc_s)

    @pl.when(jnp.logical_and(is_first, jnp.logical_not(is_ctx)))
    def _():
        hc_s[0:1] = st_s[pl.ds(2 * b, 1), :]
        hc_s[1:2] = st_s[pl.ds(2 * b + 1, 1), :]

    coeffs(0, lru_input(xf_ref[:, lru_cols], xfp_ref[:, lru_cols], xfn_ref[:, lru_cols], not_first, not_last))
    hf = hc_s[0:1]
    for t in range(ch):
        hf = a_s[0, t:t + 1, :] * hf + b_s[0, t:t + 1, :]
        hf_ref[t:t + 1, :] = hf
    hc_s[0:1] = hf

    coeffs(1, lru_input(lxb_ref[...], lxbp_ref[...], lxbn_ref[...], not_last, not_first))

    u = xf_ref[:, gate_cols] * xf_ref[:, value_cols]
    u_prev = xfp_ref[7:8, gate_cols] * xfp_ref[7:8, value_cols] * not_first
    u_next = xfn_ref[0:1, gate_cols] * xfn_ref[0:1, value_cols] * not_last
    conv = scw_ref[0:1] * back1(u, u_prev) + scw_ref[1:2] * u + scw_ref[2:3] * fwd1(u, u_next)
    zb_ref[...] = (sb_ref[...] * conv).astype(BF16)

    hb = hc_s[1:2]
    for t in range(ch - 1, -1, -1):
        hb = a_s[1, t:t + 1, :] * hb + b_s[1, t:t + 1, :]
        hb_ref[t:t + 1, :] = hb
    hc_s[1:2] = hb

    _attention_item(it, r0t, cls, q_ref, k_ref, v_ref, kc_ref, vc_ref,
                    (bias0_ref, bias1_ref, bias2_ref, bias3_ref), att_ref, band)

    @pl.when(is_ctx)
    def _():
        st_s[pl.ds(2 * b, 1), :] = hf
        st_s[pl.ds(2 * b + 1, 1), :] = hb


def _mixer_tables(n_batch, seq, n_ctx, ch):
    assert n_ctx == ch and ch == ATT_QROWS
    nc = seq // ch
    rows = seq // GRID_W
    kr = min(NA_WIN_ROWS, rows)
    ctx0 = n_batch * seq // ch
    fblk, bblk, first, last, seqb, r0t, cls = [], [], [], [], [], [], []
    for b in range(n_batch):
        fblk.append(ctx0 + b); bblk.append(ctx0 + b); first.append(1); last.append(1); seqb.append(b)
        r0t.extend([0] * ATT_ROWS); cls.extend([kr] * ATT_ROWS)
    for b in range(n_batch):
        for c in range(nc):
            fblk.append(b * nc + c); bblk.append(b * nc + nc - 1 - c)
            first.append(int(c == 0)); last.append(int(c == nc - 1)); seqb.append(b)
            for r in range(c * ATT_ROWS, (c + 1) * ATT_ROWS):
                r0 = min(max(r - kr // 2, 0), rows - kr)
                r0t.append(r0); cls.append(r - r0)
    return [np.asarray(a, np.int32) for a in (fblk, bblk, first, last, seqb, r0t, cls)]


def _token_mixers(q, k, v, bias, sb, rest, sc_w, lru_cw, lru_cb, sp, wr_bd, wi_bd, b_r, b_i, n_batch, seq, n_ctx):
    m = sb.shape[0]
    ch = SCAN_CHUNK
    width = BRANCH_WIDTH
    tables = _mixer_tables(n_batch, seq, n_ctx, ch)
    n_items = len(tables[0])
    halo_per_chunk = ch // SUBLANES
    last_halo = m // SUBLANES - 1
    band = min(NA_WIN_ROWS, seq // GRID_W) * GRID_W
    ctx_blk0 = n_batch * seq // n_ctx

    def cur(col, which, n=1):
        return pl.BlockSpec((ch, n * width), lambda i, f, bk, *_: ((f, bk)[which][i], col))

    def prev(col, which, n=1):
        return pl.BlockSpec((SUBLANES, n * width),
                            lambda i, f, bk, *_: (jnp.maximum((f, bk)[which][i] * halo_per_chunk - 1, 0), col))

    def nxt(col, which, n=1):
        return pl.BlockSpec((SUBLANES, n * width),
                            lambda i, f, bk, *_: (jnp.minimum(((f, bk)[which][i] + 1) * halo_per_chunk, last_halo), col))

    def full(shape):
        return pl.BlockSpec(shape, lambda i, *_: (0,) * len(shape))

    in_specs = [cur(0, 0), cur(0, 0, 3), prev(0, 0, 3), nxt(0, 0, 3),
                cur(2, 1), prev(2, 1), nxt(2, 1),
                full(sc_w.shape), full(lru_cw.shape), full((1, width)), full(sp.shape),
                full(wr_bd.shape), full(wi_bd.shape), full(b_r.shape), full(b_i.shape)]
    in_specs += [cur(0, 0),
                 pl.BlockSpec((seq, width), lambda i, f, bk, fi, la, sq, *_: (sq[i], 0)),
                 pl.BlockSpec((seq, width), lambda i, f, bk, fi, la, sq, *_: (sq[i], 0)),
                 pl.BlockSpec((n_ctx, width), lambda i, f, bk, fi, la, sq, *_: (ctx_blk0 + sq[i], 0)),
                 pl.BlockSpec((n_ctx, width), lambda i, f, bk, fi, la, sq, *_: (ctx_blk0 + sq[i], 0))]
    for row in range(ATT_ROWS):
        in_specs.append(pl.BlockSpec((1, NA_HEADS, GRID_W, band),
                                     lambda i, f, bk, fi, la, sq, r0, cl, row=row: (cl[i * ATT_ROWS + row], 0, 0, 0)))
    out_specs = [cur(0, 0), cur(0, 0), cur(0, 1), cur(0, 0)]
    grid_spec = pltpu.PrefetchScalarGridSpec(
        num_scalar_prefetch=7, grid=(n_items,), in_specs=in_specs, out_specs=out_specs,
        scratch_shapes=[pltpu.VMEM((2, ch, width), F32), pltpu.VMEM((2, ch, width), F32),
                        pltpu.VMEM((SUBLANES, width), F32), pltpu.VMEM((2 * n_batch, width), F32)])
    return pl.pallas_call(
        functools.partial(_mixer_kernel, ch=ch, n_ctx_items=n_batch, band=band),
        grid_spec=grid_spec,
        out_shape=[jax.ShapeDtypeStruct((m, width), BF16),
                   jax.ShapeDtypeStruct((m, width), F32),
                   jax.ShapeDtypeStruct((m, width), F32),
                   jax.ShapeDtypeStruct((m, width), BF16)],
        compiler_params=_params(1),
        name="token_mixers",
    )(*[jnp.asarray(t) for t in tables],
      sb, rest, rest, rest, rest, rest, rest,
      sc_w, lru_cw, lru_cb.reshape(1, width), sp, wr_bd, wi_bd, b_r, b_i,
      q, k, v, k, v, bias, bias, bias, bias)


def _attention_item(it, r0t, cls, q_ref, k_ref, v_ref, kc_ref, vc_ref, bias_refs, o_ref, band):
    nq = GRID_W
    lane = lax.broadcasted_iota(jnp.int32, (nq, LANES), 1)
    low = lane < NA_HEAD_DIM
    scale = NA_HEAD_DIM ** -0.5
    nt = (((1,), (1,)), ((), ()))
    starts = [pl.multiple_of(r0t[it * ATT_ROWS + row] * GRID_W, GRID_W) for row in range(ATT_ROWS)]
    for hp in range(NA_HEADS // 2):
        cols = slice(hp * LANES, (hp + 1) * LANES)
        stacked = []
        for row in range(ATT_ROWS):
            qp = q_ref[row * nq:(row + 1) * nq, cols].astype(F32) * scale
            stacked += [jnp.where(low, qp, 0.0), jnp.where(low, 0.0, qp)]
        qs_all = jnp.concatenate(stacked, axis=0).astype(BF16)
        s_ctx_all = lax.dot_general(qs_all, kc_ref[:, cols], nt, preferred_element_type=F32)
        e_ctx_rows, e_loc_rows, dens = [], [], []
        for row in range(ATT_ROWS):
            part = slice(row * 2 * nq, (row + 1) * 2 * nq)
            kb = k_ref[pl.ds(starts[row], band), cols]
            s_loc = lax.dot_general(qs_all[part], kb, nt, preferred_element_type=F32)
            bias_ref = bias_refs[row]
            s_loc = s_loc + jnp.concatenate([bias_ref[0, 2 * hp], bias_ref[0, 2 * hp + 1]], axis=0)
            s_ctx = s_ctx_all[part]
            mx = jnp.maximum(jnp.max(s_loc, axis=-1, keepdims=True), jnp.max(s_ctx, axis=-1, keepdims=True))
            e_loc = jnp.exp(s_loc - mx)
            e_ctx = jnp.exp(s_ctx - mx)
            dens.append(jnp.sum(e_loc, axis=-1, keepdims=True) + jnp.sum(e_ctx, axis=-1, keepdims=True))
            e_loc_rows.append(e_loc.astype(BF16))
            e_ctx_rows.append(e_ctx.astype(BF16))
        o_ctx_all = jnp.dot(jnp.concatenate(e_ctx_rows, axis=0), vc_ref[:, cols], preferred_element_type=F32)
        for row in range(ATT_ROWS):
            part = slice(row * 2 * nq, (row + 1) * 2 * nq)
            vb = v_ref[pl.ds(starts[row], band), cols]
            o = (jnp.dot(e_loc_rows[row], vb, preferred_element_type=F32) + o_ctx_all[part]) / dens[row]
            o_ref[row * nq:(row + 1) * nq, cols] = jnp.where(low, o[:nq], o[nq:]).astype(BF16)


def _attention_bias(rpb, seq):
    rows = seq // GRID_W
    kr = min(NA_WIN_ROWS, rows)
    kc = NA_WIN_COLS
    cq = np.arange(GRID_W)
    c0 = np.clip(cq - kc // 2, 0, GRID_W - kc)
    ck = np.arange(GRID_W)
    inside = (ck[None, :] >= c0[:, None]) & (ck[None, :] < c0[:, None] + kc)
    dc = np.clip(ck[None, :] - cq[:, None] + (NA_WIN_COLS - 1), 0, 2 * NA_WIN_COLS - 2)
    n_dr = 2 * NA_WIN_ROWS - 1
    n_dc = 2 * NA_WIN_COLS - 1
    pick = jnp.asarray((np.arange(n_dc)[:, None] == dc.reshape(1, -1)).astype(np.float32))
    picked = jnp.dot(rpb.reshape(-1, n_dc), pick, precision=lax.Precision.HIGHEST)
    picked = picked.reshape(NA_HEADS, 2 * NA_WIN_ROWS - 1, GRID_W, GRID_W)
    table = jnp.where(jnp.asarray(inside)[None, None], picked, NEG_BIG)
    table = table.transpose(0, 2, 1, 3).reshape(NA_HEADS, GRID_W, n_dr * GRID_W)
    classes = []
    for cl in range(kr):
        lo = (NA_WIN_ROWS - 1 - cl) * GRID_W
        classes.append(table[:, :, lo:lo + kr * GRID_W])
    classes.append(jnp.full((NA_HEADS, GRID_W, kr * GRID_W), NEG_BIG, F32))
    return jnp.stack(classes, axis=0)


def _post_kernel(ha_ref, hc_ref, att_ref, zb_ref, hf_ref, hb_ref, lg_ref,
                 sh1_ref, sc1_ref, g1_ref, sh2_ref, sc2_ref,
                 wgl0_ref, wgl1_ref, wgl2_ref, bgl0_ref, bgl1_ref, bgl2_ref,
                 wpa_ref, wpc_ref, wpl_ref, wo_ref, bo_ref, l1g_ref, l1b_ref,
                 rw_ref, rb_ref, tri_ref, upper_ref,
                 h1_ref, xs_ref, pos_ref, cnt_ref, u_s, logit_s, *, split, n_tiles):
    i = pl.program_id(0)

    @pl.when(i == 0)
    def _():
        u_s[...] = jnp.zeros_like(u_s)
        logit_s[...] = jnp.zeros_like(logit_s)

    _route_and_sort(u_s[...], logit_s[...], tri_ref, upper_ref, xs_ref, pos_ref, cnt_ref)

    h = jnp.where(jnp.minimum(i, n_tiles - 1) < split, ha_ref[...], hc_ref[...])
    u1 = (_layer_norm(h) * (1.0 + sc1_ref[0]) + sh1_ref[0]).astype(BF16)
    y_a = jnp.dot(att_ref[...], wpa_ref[...], preferred_element_type=F32)
    y_b = jnp.dot(zb_ref[...], wpc_ref[...], preferred_element_type=F32)
    zc = jax.nn.gelu(lg_ref[...]) * (hf_ref[...] + hb_ref[...])
    y_c = jnp.dot(zc.astype(BF16), wpl_ref[...], preferred_element_type=F32)
    merged = (jax.nn.sigmoid(jnp.dot(u1, wgl0_ref[0], preferred_element_type=F32) + bgl0_ref[0]) * y_a
              + jax.nn.sigmoid(jnp.dot(u1, wgl1_ref[0], preferred_element_type=F32) + bgl1_ref[0]) * y_b
              + jax.nn.sigmoid(jnp.dot(u1, wgl2_ref[0], preferred_element_type=F32) + bgl2_ref[0]) * y_c)
    y = jnp.dot(merged.astype(BF16), wo_ref[...], preferred_element_type=F32) + bo_ref[...]
    h1 = _layer_norm(DEEPNORM_ALPHA * h + g1_ref[0] * y) * l1g_ref[...] + l1b_ref[...]
    h1_ref[...] = h1
    u2 = _layer_norm(h1) * (1.0 + sc2_ref[0]) + sh2_ref[0]

    u_hi = u2.astype(BF16)
    u_lo = (u2 - u_hi.astype(F32)).astype(BF16)
    by_hi = jnp.dot(u_hi, rw_ref[...], preferred_element_type=F32)
    logit_s[...] = (by_hi[:, :LANES] + by_hi[:, LANES:]
                    + jnp.dot(u_lo, rw_ref[:, :LANES], preferred_element_type=F32) + rb_ref[...])
    u_s[...] = u_hi


def _route_and_sort(u_hi, logits, tri_ref, upper_ref, xs_ref, pos_ref, cnt_ref):
    tm = logits.shape[0]
    lane = lax.broadcasted_iota(jnp.int32, (tm, LANES), 1)
    lane_f = lane.astype(F32)
    work = logits
    tops, idxs, hots = [], [], []
    for _ in range(TOP_K):
        mx = jnp.max(work, axis=-1, keepdims=True)
        idx = jnp.min(jnp.where(work == mx, lane_f, float(LANES)), axis=-1, keepdims=True)
        hot = lane_f == idx
        work = jnp.where(hot, -3e38, work)
        tops.append(mx); idxs.append(idx); hots.append(hot)
    exps = [jnp.exp(t - tops[0]) for t in tops]
    den = exps[0] + exps[1] + exps[2] + exps[3]
    hot_all = jnp.zeros((tm, LANES), F32)
    for hot in hots:
        hot_all = hot_all + hot.astype(F32)
    cnt = jnp.sum(hot_all, axis=0, keepdims=True)
    cnt_pad = jnp.floor((cnt + (SUBLANES - 1.0)) * (1.0 / SUBLANES)) * SUBLANES
    off = jnp.dot(jnp.broadcast_to(cnt_pad, (SUBLANES, LANES)), upper_ref[...],
                  precision=lax.Precision.HIGHEST, preferred_element_type=F32)[0:1]
    slot = off + jnp.dot(tri_ref[...], hot_all.astype(BF16), preferred_element_type=F32)
    pos4 = jnp.zeros((tm, LANES), F32)
    w_tile = jnp.zeros((tm, LANES), F32)
    for kk in range(TOP_K):
        pos_k = jnp.sum(jnp.where(hots[kk], slot, 0.0), axis=-1, keepdims=True)
        pos4 = jnp.where(lane == kk, pos_k, pos4)
        p = exps[kk] / den
        p_hi = p.astype(BF16).astype(F32)
        p_mid = (p - p_hi).astype(BF16).astype(F32)
        p_lo = p - p_hi - p_mid
        w_tile = jnp.where(hots[kk], p_hi, w_tile)
        w_tile = jnp.where(lane_f == idxs[kk] + float(N_EXPERTS), p_mid, w_tile)
        w_tile = jnp.where(lane_f == idxs[kk] + float(2 * N_EXPERTS), p_lo, w_tile)
    pos_ref[...] = pos4
    sub = lax.broadcasted_iota(jnp.int32, (SUBLANES, LANES), 0)
    cnt_ref[...] = jnp.where(sub == 0, cnt, jnp.where(sub == 1, off, 0.0))

    n_sorted = xs_ref.shape[0]
    pos_t = pos4.T
    r_iota = lax.broadcasted_iota(jnp.int32, (n_sorted, tm), 0).astype(F32)
    hit = r_iota == pos_t[0:1, :]
    for kk in range(1, TOP_K):
        hit = jnp.logical_or(hit, r_iota == pos_t[kk:kk + 1, :])
    perm = jnp.where(hit, 1.0, 0.0).astype(BF16)
    feats = jnp.concatenate([u_hi, w_tile.astype(BF16)], axis=1)
    xs_ref[...] = jnp.dot(perm, feats, preferred_element_type=F32)


def _post_mixer(h_parts, att, zb, hf, hb, rest, mod3, mod_base, layer, w_in_bf, b_in, wpa, wpc, wpl, wo, b_o, l1g,
                l1b, rw_pad, rb_pad, n_rows, n_lat, seq):
    tm = TM_POST
    width = BRANCH_WIDTH
    ha, hc = h_parts[0], h_parts[-1]
    split = min(ha.shape[0], n_rows) // tm
    gate_col0 = N_EARLY // D_MODEL
    tiles_per_seq = seq // tm
    n_lat_tiles = n_lat // tm
    n_groups_lat = n_lat // seq
    tri = jnp.asarray(np.tril(np.ones((tm, tm), np.float32), -1), BF16)
    upper = jnp.asarray(np.triu(np.ones((LANES, LANES), np.float32), 1))
    n_tiles = n_rows // tm

    def dense(i):
        return jnp.minimum(i, n_tiles - 1)

    def routed(i):
        return jnp.maximum(i - 1, 0)

    def group(i):
        return jnp.where(dense(i) < n_lat_tiles, dense(i) // tiles_per_seq, n_groups_lat)

    def rows(wd, col=0):
        return pl.BlockSpec((tm, wd), lambda i: (dense(i), col))

    def full(shape):
        return pl.BlockSpec(shape, lambda i: (0,) * len(shape))

    def mod(which):
        return pl.BlockSpec((1, 1, D_MODEL), lambda i: (mod_base + group(i) * 6 + which, 0, 0))

    in_specs = [pl.BlockSpec((tm, D_MODEL), lambda i: (jnp.minimum(dense(i), split - 1), 0)),
                pl.BlockSpec((tm, D_MODEL), lambda i: (jnp.maximum(dense(i) - split, 0), 0)),
                rows(width), rows(width), rows(width), rows(width), rows(width, 3),
                mod(0), mod(1), mod(2), mod(3), mod(4)]
    in_specs += [pl.BlockSpec((1, D_MODEL, D_MODEL), lambda i, c=c: (layer, 0, gate_col0 + c)) for c in range(3)]
    in_specs += [pl.BlockSpec((1, 1, D_MODEL), lambda i, c=c: (layer, 0, gate_col0 + c)) for c in range(3)]
    in_specs += [full(wpa.shape), full(wpc.shape), full(wpl.shape), full(wo.shape), full((1, D_MODEL)),
                full((1, D_MODEL)), full((1, D_MODEL)), full(rw_pad.shape), full(rb_pad.shape), full(tri.shape),
                full(upper.shape)]
    out_specs = [rows(D_MODEL), pl.BlockSpec((TOK_BLOCK, XS_WIDTH), lambda i: (routed(i), 0)),
                 pl.BlockSpec((tm, LANES), lambda i: (routed(i), 0)),
                 pl.BlockSpec((SUBLANES, LANES), lambda i: (routed(i), 0))]
    return pl.pallas_call(
        functools.partial(_post_kernel, split=split, n_tiles=n_tiles),
        grid=(n_tiles + 1,),
        in_specs=in_specs, out_specs=out_specs,
        out_shape=[jax.ShapeDtypeStruct((n_rows, D_MODEL), F32),
                   jax.ShapeDtypeStruct((n_tiles * TOK_BLOCK, XS_WIDTH), F32),
                   jax.ShapeDtypeStruct((n_rows, LANES), F32),
                   jax.ShapeDtypeStruct((n_tiles * SUBLANES, LANES), F32)],
        scratch_shapes=[pltpu.VMEM((tm, D_MODEL), BF16), pltpu.VMEM((tm, LANES), F32)],
        compiler_params=_params(1),
        name="post_mixer",
    )(ha, hc, att, zb, hf, hb, rest, mod3, mod3, mod3, mod3, mod3,
      w_in_bf, w_in_bf, w_in_bf, b_in.reshape(DEPTH, 1, -1), b_in.reshape(DEPTH, 1, -1), b_in.reshape(DEPTH, 1, -1),
      wpa, wpc, wpl, wo, b_o.reshape(1, D_MODEL), l1g.reshape(1, D_MODEL), l1b.reshape(1, D_MODEL),
      rw_pad, rb_pad, tri, upper)


def _expert_plan(cnt_out, n_tok_tiles, n_rows):
    tm = TM_EXPERT
    co = cnt_out.reshape(n_tok_tiles, SUBLANES, LANES)
    cnt = (co[:, 0, :N_EXPERTS].astype(jnp.int32) + SUBLANES - 1) // SUBLANES * SUBLANES
    off = co[:, 1, :N_EXPERTS].astype(jnp.int32)
    cum_end = jnp.cumsum(cnt, axis=0)
    cum = cum_end - cnt
    total = cum_end[-1]
    n_et = (total + tm - 1) // tm
    et_end = jnp.cumsum(n_et)
    n_act = et_end[-1:].astype(jnp.int32)
    n_tiles = -(-n_tok_tiles * TOK_BLOCK // tm) + N_EXPERTS
    j = jnp.arange(n_tiles, dtype=jnp.int32)
    tile_e = jnp.minimum(jnp.sum((et_end[None, :] <= j[:, None]).astype(jnp.int32), axis=1), N_EXPERTS - 1)
    pick_e = (tile_e[:, None] == jnp.arange(N_EXPERTS, dtype=jnp.int32)[None, :]).astype(F32)

    def per_tile(table):
        return jnp.dot(pick_e, table.astype(F32), precision=lax.Precision.HIGHEST).astype(jnp.int32)

    row0 = (j - per_tile(et_end - n_et)) * tm
    n_rows_tile = jnp.clip(per_tile(total) - row0, 0, tm)
    cum_e = per_tile(cum.T)
    cum_end_e = per_tile(cum_end.T)
    delta_e = per_tile((jnp.arange(n_tok_tiles, dtype=jnp.int32)[:, None] * TOK_BLOCK + off - cum).T)
    q = row0[:, None] + SUBLANES * jnp.arange(tm // SUBLANES, dtype=jnp.int32)[None, :]
    inside = jnp.logical_and(cum_e.T[:, :, None] <= q[None], q[None] < cum_end_e.T[:, :, None])
    src = q + jnp.sum(jnp.where(inside, delta_e.T[:, :, None], 0), axis=0)
    used = off[:, N_EXPERTS - 1] + cnt[:, N_EXPERTS - 1]
    first = jnp.concatenate([jnp.ones((1,), jnp.int32), (tile_e[1:] != tile_e[:-1]).astype(jnp.int32)])
    group = jnp.cumsum(first) - 1
    after = per_tile(et_end)
    next_e = jnp.where(after < n_act[0],
                       jnp.minimum(jnp.sum((et_end[None, :] <= after[:, None]).astype(jnp.int32), axis=1),
                                   N_EXPERTS - 1), -1)
    return (tile_e.astype(jnp.int32), n_rows_tile.astype(jnp.int32), n_act, src.reshape(-1).astype(jnp.int32),
            used.astype(jnp.int32), first, group.astype(jnp.int32), next_e.astype(jnp.int32), n_tiles)


def _expert_kernel(tile_e, n_rows_t, n_act, src_t, used_t, first_t, group_t, next_t,
                   xs_hbm, wgu_hbm, bgu_ref, wdn_hbm, bdn_ref, ys_hbm,
                   xin, yout, wgu_f, wdn_f, wgu_s, wdn_s, zeros, sem_in, sem_out, sem_zero, sem_wgu, sem_wdn,
                   *, tm, tok_block, n_tok_tiles, layer):
    j = pl.program_id(0)
    na = n_act[0]
    chunks = tm // SUBLANES

    def weight_copies(e, wslot):
        return (pltpu.make_async_copy(wgu_hbm.at[layer, e], wgu_f.at[wslot], sem_wgu.at[wslot]),
                pltpu.make_async_copy(wdn_hbm.at[layer, e], wdn_f.at[wslot], sem_wdn.at[wslot]))

    def gather(src, dst, size, slot):
        return pltpu.make_async_copy(xs_hbm.at[pl.ds(src, size)], xin.at[slot, pl.ds(dst, size)], sem_in.at[slot])

    def scatter(src, dst, size, slot):
        return pltpu.make_async_copy(yout.at[slot, pl.ds(dst, size)], ys_hbm.at[pl.ds(src, size)], sem_out.at[slot])

    def start_chunks(jj, slot, copy):
        def one(c, priority):
            src = pl.multiple_of(src_t[jj * chunks + c], SUBLANES)
            copy(src, pl.multiple_of(c * SUBLANES, SUBLANES), SUBLANES, slot).start(priority=priority)

        def body(c, carry):
            one(c, 0)
            return carry

        def body_unrolled(g, carry):
            for u in range(SUBLANES):
                one(g * SUBLANES + u, u % 2)
            return carry

        n = lax.shift_right_logical(n_rows_t[jj], 3)

        @pl.when(n == chunks)
        def _():
            lax.fori_loop(0, chunks // SUBLANES, body_unrolled, 0)

        @pl.when(n != chunks)
        def _():
            lax.fori_loop(0, n, body, 0)

    def wait_rows(jj, slot, copy):
        n = n_rows_t[jj]
        size = tm
        while size >= SUBLANES:
            @pl.when((n & size) != 0)
            def _(size=size):
                copy(0, 0, size, slot).wait()
            size //= 2

    slot = j % 2

    @pl.when(j == 0)
    def _():
        xin[...] = jnp.zeros_like(xin)
        start_chunks(0, 0, gather)
        zeros[...] = jnp.zeros_like(zeros)

        def clear_tail(i, copy_op):
            used = used_t[i]
            tail = tok_block - used
            size = ZERO_ROWS
            sizes = []
            while size >= SUBLANES:
                sizes.append(size)
                size //= 2
            for size in sizes:
                @pl.when((tail & size) != 0)
                def _(size=size):
                    at = pl.multiple_of(i * tok_block + used + (tail & ~(2 * size - 1)), SUBLANES)
                    copy_op(pltpu.make_async_copy(zeros.at[pl.ds(0, size)], ys_hbm.at[pl.ds(at, size)], sem_zero))

        def start_clear(i, c):
            clear_tail(i, lambda cp: cp.start())
            return c

        def wait_clear(i, c):
            clear_tail(i, lambda cp: cp.wait())
            return c

        lax.fori_loop(0, n_tok_tiles, start_clear, 0)
        lax.fori_loop(0, n_tok_tiles, wait_clear, 0)

    @pl.when(j + 1 < na)
    def _():
        start_chunks(j + 1, 1 - slot, gather)

    @pl.when(j < na)
    def _():
        e = tile_e[j]

        @pl.when(first_t[j] == 1)
        def _():
            wslot = group_t[j] % 2

            @pl.when(j == 0)
            def _():
                for cp in weight_copies(e, wslot):
                    cp.start()

            for cp in weight_copies(e, wslot):
                cp.wait()
            wgu_s[...] = wgu_f[wslot].astype(BF16)
            wdn_s[...] = wdn_f[wslot].astype(BF16)

            @pl.when(next_t[j] >= 0)
            def _():
                for cp in weight_copies(next_t[j], 1 - wslot):
                    cp.start()

        wait_rows(j, slot, gather)

        def compute(rows):
            x = xin[slot, pl.ds(0, rows)]
            lane = lax.broadcasted_iota(jnp.int32, (rows, LANES), 1)
            p = jnp.sum(jnp.where(lane % N_EXPERTS == e, x[:, D_MODEL:], 0.0), axis=-1, keepdims=True)
            gu = (jnp.dot(x[:, :D_MODEL].astype(BF16), wgu_s[...], preferred_element_type=F32)
                  + bgu_ref[0, pl.ds(e, 1), :])
            f = gu.shape[1] // 2
            gate = jnp.minimum(gu[:, :f], SWIGLU_LIMIT)
            up = jnp.clip(gu[:, f:], -SWIGLU_LIMIT, SWIGLU_LIMIT)
            hid = (up + 1.0) * gate * jax.nn.sigmoid(SWIGLU_ALPHA * gate)
            y = (jnp.dot(hid.astype(BF16), wdn_s[...], preferred_element_type=F32)
                 + bdn_ref[0, pl.ds(e, 1), :])
            yout[slot, pl.ds(0, rows)] = y * p

        quarter = tm // 4
        for part in range(1, 5):
            @pl.when(jnp.logical_and(n_rows_t[j] > (part - 1) * quarter, n_rows_t[j] <= part * quarter))
            def _(part=part):
                compute(part * quarter)

        start_chunks(j, slot, scatter)

        @pl.when(j >= 1)
        def _():
            wait_rows(j - 1, 1 - slot, scatter)

        @pl.when(j == na - 1)
        def _():
            wait_rows(j, slot, scatter)


def _experts(xs, plan, layer, w_gu, b_gu, w_dn, b_dn):
    tile_e, n_rows_tile, n_act, src, used, first, group, next_e, n_tiles = plan
    tm = TM_EXPERT
    f2 = w_gu.shape[-1]

    in_specs = [pl.BlockSpec(memory_space=pl.ANY),
                pl.BlockSpec(memory_space=pl.ANY),
                pl.BlockSpec((1, N_EXPERTS, f2), lambda j, *_: (layer, 0, 0)),
                pl.BlockSpec(memory_space=pl.ANY),
                pl.BlockSpec((1, N_EXPERTS, D_MODEL), lambda j, *_: (layer, 0, 0))]
    grid_spec = pltpu.PrefetchScalarGridSpec(
        num_scalar_prefetch=8, grid=(n_tiles,), in_specs=in_specs,
        out_specs=pl.BlockSpec(memory_space=pl.ANY),
        scratch_shapes=[pltpu.VMEM((2, tm, XS_WIDTH), F32), pltpu.VMEM((2, tm, D_MODEL), F32),
                        pltpu.VMEM((2, D_MODEL, f2), F32), pltpu.VMEM((2, f2 // 2, D_MODEL), F32),
                        pltpu.VMEM((D_MODEL, f2), BF16), pltpu.VMEM((f2 // 2, D_MODEL), BF16),
                        pltpu.VMEM((ZERO_ROWS, D_MODEL), F32),
                        pltpu.SemaphoreType.DMA((2,)), pltpu.SemaphoreType.DMA((2,)), pltpu.SemaphoreType.DMA(()),
                        pltpu.SemaphoreType.DMA((2,)), pltpu.SemaphoreType.DMA((2,))])
    return pl.pallas_call(
        functools.partial(_expert_kernel, tm=tm, tok_block=TOK_BLOCK, n_tok_tiles=xs.shape[0] // TOK_BLOCK,
                          layer=layer),
        grid_spec=grid_spec,
        out_shape=jax.ShapeDtypeStruct((xs.shape[0], D_MODEL), F32),
        compiler_params=_params(1),
        name="moe_experts",
    )(tile_e, n_rows_tile, n_act, src, used, first, group, next_e,
      xs, w_gu, b_gu, w_dn, b_dn)


def _combine_kernel(ys_ref, pos_ref, h1_ref, g2_ref, l2g_ref, l2b_ref, o_ref):
    tm = pos_ref.shape[0]
    n_sorted = ys_ref.shape[0]
    pos = pos_ref[...]
    col = lax.broadcasted_iota(jnp.int32, (tm, n_sorted), 1).astype(F32)
    sel = jnp.zeros((tm, n_sorted), F32)
    for kk in range(TOP_K):
        sel = sel + (col == pos[:, kk:kk + 1]).astype(F32)
    sel = sel.astype(BF16)
    ys = ys_ref[...]
    hi = ys.astype(BF16)
    rest = ys - hi.astype(F32)
    mid = rest.astype(BF16)
    lo = (rest - mid.astype(F32)).astype(BF16)
    y2 = (jnp.dot(sel, hi, preferred_element_type=F32) + jnp.dot(sel, mid, preferred_element_type=F32)
          + jnp.dot(sel, lo, preferred_element_type=F32))
    o_ref[...] = _layer_norm(DEEPNORM_ALPHA * h1_ref[...] + g2_ref[0] * y2) * l2g_ref[...] + l2b_ref[...]


def _combine(ys, pos4, h1, mod3, mod_base, l2g, l2b, n_lat, seq):
    n_rows = h1.shape[0]
    tm = TM_POST
    tiles_per_seq = seq // tm
    n_lat_tiles = n_lat // tm
    n_groups_lat = n_lat // seq

    def group(i):
        return jnp.where(i < n_lat_tiles, i // tiles_per_seq, n_groups_lat)

    in_specs = [pl.BlockSpec((TOK_BLOCK, D_MODEL), lambda i: (i, 0)),
                pl.BlockSpec((tm, LANES), lambda i: (i, 0)),
                pl.BlockSpec((tm, D_MODEL), lambda i: (i, 0)),
                pl.BlockSpec((1, 1, D_MODEL), lambda i: (mod_base + group(i) * 6 + 5, 0, 0)),
                pl.BlockSpec((1, D_MODEL), lambda i: (0, 0)),
                pl.BlockSpec((1, D_MODEL), lambda i: (0, 0))]
    return pl.pallas_call(
        _combine_kernel,
        grid=(n_rows // tm,),
        in_specs=in_specs,
        out_specs=pl.BlockSpec((tm, D_MODEL), lambda i: (i, 0)),
        out_shape=jax.ShapeDtypeStruct((n_rows, D_MODEL), F32),
        compiler_params=_params(1),
        name="moe_combine",
    )(ys, pos4, h1, mod3, l2g.reshape(1, D_MODEL), l2b.reshape(1, D_MODEL))


def _block_diag(w):
    two, n, d, e = w.shape
    eye = jnp.eye(n, dtype=w.dtype)
    return (w[:, :, :, None, :] * eye[None, :, None, :, None]).reshape(two, n * d, n * e)


def kernel(x, c, ctx, c_ctx, w_mod, b_mod, w_in, b_in, na_rpb, w_proj_attn, w_proj_conv, w_proj_lru, sc_conv_w, lru_conv_w, lru_conv_b, lru_lambda, lru_w_r, lru_b_r, lru_w_i, lru_b_i, w_o, b_o, ln1_g, ln1_b, router_w, router_b, exp_w_gu, exp_b_gu, exp_w_dn, exp_b_dn, ln2_g, ln2_b):
    n_batch, seq, d = x.shape
    n_ctx = ctx.shape[1]
    n_lat = n_batch * seq
    n_all = n_lat + n_batch * n_ctx
    assert d == D_MODEL and n_batch + 1 <= SUBLANES

    cc = jnp.concatenate([c, c_ctx[None], jnp.zeros((SUBLANES - n_batch - 1, d), F32)], axis=0)
    mod = _modulation(cc, w_mod, b_mod)
    groups = n_batch + 1
    mod3 = mod.reshape(DEPTH, SUBLANES, 6, d)[:, :groups].reshape(DEPTH * groups * 6, 1, d)

    cos_t, sin_t = _make_rope(seq, TM_INPROJ)
    h = (x.reshape(n_lat, d), ctx.reshape(n_batch * n_ctx, d))
    w_in_bf = w_in.astype(BF16)

    for layer in range(DEPTH):
        last = layer == DEPTH - 1
        mod_base = layer * groups * 6
        q, k, v, sb, rest = _input_projection(h, mod3, mod_base, layer, w_in_bf, b_in, N_EARLY,
                                              cos_t, sin_t, n_lat, seq)
        sp = jax.nn.softplus(-lru_lambda[layer])
        zb, hf, hb, att = _token_mixers(
            q, k, v, _attention_bias(na_rpb[layer], seq), sb, rest, sc_conv_w[layer], lru_conv_w[layer],
            lru_conv_b[layer], sp, _block_diag(lru_w_r[layer]).astype(BF16), _block_diag(lru_w_i[layer]).astype(BF16),
            lru_b_r[layer], lru_b_i[layer], n_batch, seq, n_ctx)
        n_rows = n_lat if last else n_all
        rw_full = jnp.pad(router_w[layer], ((0, 0), (0, LANES - N_EXPERTS)))
        rw_hi = rw_full.astype(BF16)
        rw_pad = jnp.concatenate([rw_hi, (rw_full - rw_hi.astype(F32)).astype(BF16)], axis=1)
        rb_pad = jnp.concatenate([router_b[layer], jnp.full((LANES - N_EXPERTS,), NEG_BIG, F32)]).reshape(1, LANES)
        h1, xs, pos4, cnt_out = _post_mixer(
            h, att, zb, hf, hb, rest, mod3, mod_base, layer, w_in_bf, b_in,
            w_proj_attn[layer].astype(BF16), w_proj_conv[layer].astype(BF16), w_proj_lru[layer].astype(BF16),
            w_o[layer].astype(BF16), b_o[layer], ln1_g[layer], ln1_b[layer], rw_pad, rb_pad, n_rows, n_lat, seq)
        plan = _expert_plan(cnt_out, n_rows // TM_POST, n_rows)
        ys = _experts(xs, plan, layer, exp_w_gu, exp_b_gu, exp_w_dn, exp_b_dn)
        h = (_combine(ys, pos4, h1, mod3, mod_base, ln2_g[layer], ln2_b[layer], n_lat, seq),)
    return h[0].reshape(n_batch, seq, d)
```

```python
import functools

import numpy as np
import jax
import jax.numpy as jnp
from jax import lax
from jax.experimental import pallas as pl
from jax.experimental.pallas import tpu as pltpu

D_MODEL = 1024
DEPTH = 2
GRID_W = 64
NA_HEADS = 8
NA_HEAD_DIM = 64
NA_WIN_ROWS = 8
NA_WIN_COLS = 16
ROPE_BASE = 10000.0
BRANCH_WIDTH = 512
LRU_C = 8.0
N_EARLY = 8 * BRANCH_WIDTH
N_EXPERTS = 32
TOP_K = 4
SWIGLU_LIMIT = 7.0
SWIGLU_ALPHA = 1.702
LN_EPS = 1e-5
DEEPNORM_ALPHA = (2 * DEPTH) ** 0.25
NEG_BIG = -1e30

LANES = 128
SUBLANES = 8
VMEM_LIMIT_BYTES = 56 * 1024 * 1024

TM_INPROJ = 512
SCAN_CHUNK = 256
ATT_ROWS = 4
ATT_QROWS = ATT_ROWS * GRID_W
TM_POST = 256
TM_EXPERT = 512
XS_WIDTH = D_MODEL + LANES
TOK_BLOCK = TM_POST * TOP_K + N_EXPERTS * SUBLANES
ZERO_ROWS = TOK_BLOCK - TM_POST * TOP_K

F32 = jnp.float32
BF16 = jnp.bfloat16


def _params(n_axes):
    return pltpu.CompilerParams(dimension_semantics=("arbitrary",) * n_axes,
                                vmem_limit_bytes=VMEM_LIMIT_BYTES)


def _layer_norm(x):
    mu = jnp.mean(x, axis=-1, keepdims=True)
    xc = x - mu
    var = jnp.mean(xc * xc, axis=-1, keepdims=True)
    return xc * lax.rsqrt(var + LN_EPS)


def _mod_kernel(c_ref, w_ref, b_ref, o_ref):
    c = c_ref[...]
    s = (c * jax.nn.sigmoid(c)).astype(BF16)
    o_ref[0] = jnp.dot(s, w_ref[0].astype(BF16), preferred_element_type=F32) + b_ref[0]


def _modulation(cc, w_mod, b_mod):
    n_out = w_mod.shape[-1]
    return pl.pallas_call(
        _mod_kernel,
        grid=(DEPTH, n_out // D_MODEL),
        in_specs=[pl.BlockSpec((SUBLANES, D_MODEL), lambda l, j: (0, 0)),
                  pl.BlockSpec((1, D_MODEL, D_MODEL), lambda l, j: (l, 0, j)),
                  pl.BlockSpec((1, 1, D_MODEL), lambda l, j: (l, 0, j))],
        out_specs=pl.BlockSpec((1, SUBLANES, D_MODEL), lambda l, j: (l, 0, j)),
        out_shape=jax.ShapeDtypeStruct((DEPTH, SUBLANES, n_out), F32),
        compiler_params=_params(2),
        name="modulation",
    )(cc, w_mod, b_mod.reshape(DEPTH, 1, n_out))


def _rope_half(x, cos, sin_signed):
    m = NA_HEAD_DIM // 4
    lane = lax.broadcasted_iota(jnp.int32, (x.shape[0], LANES), 1)
    first = (lane % (2 * m)) < m
    outs = []
    for cidx in range(x.shape[1] // LANES):
        xc = x[:, cidx * LANES:(cidx + 1) * LANES]
        partner = jnp.where(first, pltpu.roll(xc, LANES - m, 1), pltpu.roll(xc, m, 1))
        outs.append(xc * cos + partner * sin_signed)
    return jnp.concatenate(outs, axis=1)


def _inproj_kernel(ha_ref, hb_ref, sh_ref, sc_ref, w_ref, b_ref, cos_ref, sin_ref,
                   q_ref, k_ref, v_ref, sb_ref, rest_ref, *, split):
    h = jnp.where(pl.program_id(0) < split, ha_ref[...], hb_ref[...])
    xn = (_layer_norm(h) * (1.0 + sc_ref[0]) + sh_ref[0]).astype(BF16)
    half = BRANCH_WIDTH

    def columns(lo, width):
        return jnp.dot(xn, w_ref[0, :, lo:lo + width], preferred_element_type=F32) + b_ref[0, :, lo:lo + width]

    cos = cos_ref[...]
    sin = sin_ref[...]
    q_ref[...] = _rope_half(columns(0, half), cos, sin).astype(BF16)
    k_ref[...] = _rope_half(columns(half, half), cos, sin).astype(BF16)
    v_ref[...] = columns(2 * half, half).astype(BF16)
    sb_ref[...] = columns(3 * half, half)
    n_rest = rest_ref.shape[1]
    for lo in range(0, n_rest, 2 * half):
        rest_ref[:, lo:lo + 2 * half] = columns(4 * half + lo, 2 * half)


def _input_projection(h_parts, mod3, mod_base, layer, w_in_bf, b_in, n_cols, cos_t, sin_t, n_lat, seq):
    tm = TM_INPROJ
    ha, hb = h_parts[0], h_parts[-1]
    split = ha.shape[0] // tm
    m = ha.shape[0] + (hb.shape[0] if len(h_parts) == 2 else 0)
    n_lat_tiles = n_lat // tm
    tiles_per_seq = seq // tm
    n_groups_lat = n_lat // seq

    def group(i):
        return jnp.where(i < n_lat_tiles, i // tiles_per_seq, n_groups_lat)

    def rope_blk(i):
        return jnp.where(i < n_lat_tiles, i % tiles_per_seq, tiles_per_seq)

    half = BRANCH_WIDTH
    return pl.pallas_call(
        functools.partial(_inproj_kernel, split=split),
        grid=(m // tm,),
        in_specs=[pl.BlockSpec((tm, D_MODEL), lambda i: (jnp.minimum(i, split - 1), 0)),
                  pl.BlockSpec((tm, D_MODEL), lambda i: (jnp.maximum(i - split, 0), 0)),
                  pl.BlockSpec((1, 1, D_MODEL), lambda i: (mod_base + group(i) * 6 + 0, 0, 0)),
                  pl.BlockSpec((1, 1, D_MODEL), lambda i: (mod_base + group(i) * 6 + 1, 0, 0)),
                  pl.BlockSpec((1, D_MODEL, n_cols), lambda i: (layer, 0, 0)),
                  pl.BlockSpec((1, 1, n_cols), lambda i: (layer, 0, 0)),
                  pl.BlockSpec((tm, LANES), lambda i: (rope_blk(i), 0)),
                  pl.BlockSpec((tm, LANES), lambda i: (rope_blk(i), 0))],
        out_specs=[pl.BlockSpec((tm, half), lambda i: (i, 0)),
                   pl.BlockSpec((tm, half), lambda i: (i, 0)),
                   pl.BlockSpec((tm, half), lambda i: (i, 0)),
                   pl.BlockSpec((tm, half), lambda i: (i, 0)),
                   pl.BlockSpec((tm, n_cols - 4 * half), lambda i: (i, 0))],
        out_shape=[jax.ShapeDtypeStruct((m, half), BF16),
                   jax.ShapeDtypeStruct((m, half), BF16),
                   jax.ShapeDtypeStruct((m, half), BF16),
                   jax.ShapeDtypeStruct((m, half), F32),
                   jax.ShapeDtypeStruct((m, n_cols - 4 * half), F32)],
        compiler_params=_params(1),
        name="input_projection",
    )(ha, hb, mod3, mod3, w_in_bf, b_in.reshape(DEPTH, 1, -1), cos_t, sin_t)


def _make_rope(seq, tm):
    rows = seq // GRID_W
    d = np.arange(LANES) % NA_HEAD_DIM
    m = NA_HEAD_DIM // 4
    inv_freq = (ROPE_BASE ** (-jnp.arange(m, dtype=F32) / m))[d % m]
    n_pos = max(rows, GRID_W)
    ang = jnp.arange(n_pos, dtype=F32)[:, None] * inv_freq[None, :]
    by_row = jnp.asarray(d < 2 * m)[None, None, :]

    def expand(tab):
        full = jnp.where(by_row, tab[:rows, None, :], tab[None, :GRID_W, :])
        return full.reshape(seq, LANES)

    cos = expand(jnp.cos(ang))
    sin = expand(jnp.sin(ang))
    sin_signed = jnp.where(jnp.asarray((d % (2 * m)) < m)[None, :], -sin, sin)
    cos = jnp.concatenate([cos, jnp.ones((tm, LANES), F32)], axis=0)
    sin_signed = jnp.concatenate([sin_signed, jnp.zeros((tm, LANES), F32)], axis=0)
    return cos, sin_signed


def _mixer_kernel(fblk, bblk, first, last, seqb, r0t, cls,
                  sb_ref, xf_ref, xfp_ref, xfn_ref, lxb_ref, lxbp_ref, lxbn_ref,
                  scw_ref, cw_ref, cb_ref, sp_ref, wr_ref, wi_ref, br_ref, bi_ref,
                  q_ref, k_ref, v_ref, kc_ref, vc_ref, bias0_ref, bias1_ref, bias2_ref, bias3_ref,
                  zb_ref, hf_ref, hb_ref, att_ref,
                  a_s, b_s, hc_s, st_s, *, ch, n_ctx_items, band):
    it = pl.program_id(0)
    is_first = first[it] == 1
    is_last = last[it] == 1
    is_ctx = it < n_ctx_items
    b = seqb[it]
    width = BRANCH_WIDTH
    row = lax.broadcasted_iota(jnp.int32, (ch, width), 0)
    not_first = jnp.where(is_first, 0.0, 1.0).astype(F32)
    not_last = jnp.where(is_last, 0.0, 1.0).astype(F32)

    def back1(u, prev_row):
        return jnp.where(row == 0, prev_row, pltpu.roll(u, 1, 0))

    def back2(u, prev2, prev1):
        return jnp.where(row == 0, prev2, jnp.where(row == 1, prev1, pltpu.roll(u, 2, 0)))

    def fwd1(u, next_row):
        return jnp.where(row == ch - 1, next_row, pltpu.roll(u, ch - 1, 0))

    gate_cols, value_cols, lru_cols = (slice(s * width, (s + 1) * width) for s in range(3))

    def lru_input(x, p, n, prev_ok, next_ok):
        p = p * prev_ok
        n = n * next_ok
        return (cw_ref[0:1] * back2(x, p[6:7], p[7:8]) + cw_ref[1:2] * back1(x, p[7:8])
                + cw_ref[2:3] * x + cw_ref[3:4] * fwd1(x, n[0:1]) + cb_ref[...])

    def coeffs(d, xm):
        xb = xm.astype(BF16)
        r = jax.nn.sigmoid(jnp.dot(xb, wr_ref[d], preferred_element_type=F32) + br_ref[d:d + 1])
        g = jax.nn.sigmoid(jnp.dot(xb, wi_ref[d], preferred_element_type=F32) + bi_ref[d:d + 1])
        log_a = (-LRU_C * sp_ref[d:d + 1]) * r
        a = jnp.exp(log_a)
        a_s[d] = a
        b_s[d] = jnp.sqrt(-jnp.tanh(log_a) * (a * a + 1.0)) * (g * xm)

    @pl.when(jnp.logical_and(is_first, is_ctx))
    def _():
        hc_s[...] = jnp.zeros_like(hc_s)

    @pl.when(jnp.logical_and(is_first, jnp.logical_not(is_ctx)))
    def _():
        hc_s[0:1] = st_s[pl.ds(2 * b, 1), :]
        hc_s[1:2] = st_s[pl.ds(2 * b + 1, 1), :]

    coeffs(0, lru_input(xf_ref[:, lru_cols], xfp_ref[:, lru_cols], xfn_ref[:, lru_cols], not_first, not_last))
    hf = hc_s[0:1]
    for t in range(ch):
        hf = a_s[0, t:t + 1, :] * hf + b_s[0, t:t + 1, :]
        hf_ref[t:t + 1, :] = hf
    hc_s[0:1] = hf

    coeffs(1, lru_input(lxb_ref[...], lxbp_ref[...], lxbn_ref[...], not_last, not_first))

    u = xf_ref[:, gate_cols] * xf_ref[:, value_cols]
    u_prev = xfp_ref[7:8, gate_cols] * xfp_ref[7:8, value_cols] * not_first
    u_next = xfn_ref[0:1, gate_cols] * xfn_ref[0:1, value_cols] * not_last
    conv = scw_ref[0:1] * back1(u, u_prev) + scw_ref[1:2] * u + scw_ref[2:3] * fwd1(u, u_next)
    zb_ref[...] = (sb_ref[...] * conv).astype(BF16)

    hb = hc_s[1:2]
    for t in range(ch - 1, -1, -1):
        hb = a_s[1, t:t + 1, :] * hb + b_s[1, t:t + 1, :]
        hb_ref[t:t + 1, :] = hb
    hc_s[1:2] = hb

    _attention_item(it, r0t, cls, q_ref, k_ref, v_ref, kc_ref, vc_ref,
                    (bias0_ref, bias1_ref, bias2_ref, bias3_ref), att_ref, band)

    @pl.when(is_ctx)
    def _():
        st_s[pl.ds(2 * b, 1), :] = hf
        st_s[pl.ds(2 * b + 1, 1), :] = hb


def _mixer_tables(n_batch, seq, n_ctx, ch):
    assert n_ctx == ch and ch == ATT_QROWS
    nc = seq // ch
    rows = seq // GRID_W
    kr = min(NA_WIN_ROWS, rows)
    ctx0 = n_batch * seq // ch
    fblk, bblk, first, last, seqb, r0t, cls = [], [], [], [], [], [], []
    for b in range(n_batch):
        fblk.append(ctx0 + b); bblk.append(ctx0 + b); first.append(1); last.append(1); seqb.append(b)
        r0t.extend([0] * ATT_ROWS); cls.extend([kr] * ATT_ROWS)
    for b in range(n_batch):
        for c in range(nc):
            fblk.append(b * nc + c); bblk.append(b * nc + nc - 1 - c)
            first.append(int(c == 0)); last.append(int(c == nc - 1)); seqb.append(b)
            for r in range(c * ATT_ROWS, (c + 1) * ATT_ROWS):
                r0 = min(max(r - kr // 2, 0), rows - kr)
                r0t.append(r0); cls.append(r - r0)
    return [np.asarray(a, np.int32) for a in (fblk, bblk, first, last, seqb, r0t, cls)]


def _token_mixers(q, k, v, bias, sb, rest, sc_w, lru_cw, lru_cb, sp, wr_bd, wi_bd, b_r, b_i, n_batch, seq, n_ctx):
    m = sb.shape[0]
    ch = SCAN_CHUNK
    width = BRANCH_WIDTH
    tables = _mixer_tables(n_batch, seq, n_ctx, ch)
    n_items = len(tables[0])
    halo_per_chunk = ch // SUBLANES
    last_halo = m // SUBLANES - 1
    band = min(NA_WIN_ROWS, seq // GRID_W) * GRID_W
    ctx_blk0 = n_batch * seq // n_ctx

    def cur(col, which, n=1):
        return pl.BlockSpec((ch, n * width), lambda i, f, bk, *_: ((f, bk)[which][i], col))

    def prev(col, which, n=1):
        return pl.BlockSpec((SUBLANES, n * width),
                            lambda i, f, bk, *_: (jnp.maximum((f, bk)[which][i] * halo_per_chunk - 1, 0), col))

    def nxt(col, which, n=1):
        return pl.BlockSpec((SUBLANES, n * width),
                            lambda i, f, bk, *_: (jnp.minimum(((f, bk)[which][i] + 1) * halo_per_chunk, last_halo), col))

    def full(shape):
        return pl.BlockSpec(shape, lambda i, *_: (0,) * len(shape))

    in_specs = [cur(0, 0), cur(0, 0, 3), prev(0, 0, 3), nxt(0, 0, 3),
                cur(2, 1), prev(2, 1), nxt(2, 1),
                full(sc_w.shape), full(lru_cw.shape), full((1, width)), full(sp.shape),
                full(wr_bd.shape), full(wi_bd.shape), full(b_r.shape), full(b_i.shape)]
    in_specs += [cur(0, 0),
                 pl.BlockSpec((seq, width), lambda i, f, bk, fi, la, sq, *_: (sq[i], 0)),
                 pl.BlockSpec((seq, width), lambda i, f, bk, fi, la, sq, *_: (sq[i], 0)),
                 pl.BlockSpec((n_ctx, width), lambda i, f, bk, fi, la, sq, *_: (ctx_blk0 + sq[i], 0)),
                 pl.BlockSpec((n_ctx, width), lambda i, f, bk, fi, la, sq, *_: (ctx_blk0 + sq[i], 0))]
    for row in range(ATT_ROWS):
        in_specs.append(pl.BlockSpec((1, NA_HEADS, GRID_W, band),
                                     lambda i, f, bk, fi, la, sq, r0, cl, row=row: (cl[i * ATT_ROWS + row], 0, 0, 0)))
    out_specs = [cur(0, 0), cur(0, 0), cur(0, 1), cur(0, 0)]
    grid_spec = pltpu.PrefetchScalarGridSpec(
        num_scalar_prefetch=7, grid=(n_items,), in_specs=in_specs, out_specs=out_specs,
        scratch_shapes=[pltpu.VMEM((2, ch, width), F32), pltpu.VMEM((2, ch, width), F32),
                        pltpu.VMEM((SUBLANES, width), F32), pltpu.VMEM((2 * n_batch, width), F32)])
    return pl.pallas_call(
        functools.partial(_mixer_kernel, ch=ch, n_ctx_items=n_batch, band=band),
        grid_spec=grid_spec,
        out_shape=[jax.ShapeDtypeStruct((m, width), BF16),
                   jax.ShapeDtypeStruct((m, width), F32),
                   jax.ShapeDtypeStruct((m, width), F32),
                   jax.ShapeDtypeStruct((m, width), BF16)],
        compiler_params=_params(1),
        name="token_mixers",
    )(*[jnp.asarray(t) for t in tables],
      sb, rest, rest, rest, rest, rest, rest,
      sc_w, lru_cw, lru_cb.reshape(1, width), sp, wr_bd, wi_bd, b_r, b_i,
      q, k, v, k, v, bias, bias, bias, bias)


def _attention_item(it, r0t, cls, q_ref, k_ref, v_ref, kc_ref, vc_ref, bias_refs, o_ref, band):
    nq = GRID_W
    lane = lax.broadcasted_iota(jnp.int32, (nq, LANES), 1)
    low = lane < NA_HEAD_DIM
    scale = NA_HEAD_DIM ** -0.5
    nt = (((1,), (1,)), ((), ()))
    starts = [pl.multiple_of(r0t[it * ATT_ROWS + row] * GRID_W, GRID_W) for row in range(ATT_ROWS)]
    for hp in range(NA_HEADS // 2):
        cols = slice(hp * LANES, (hp + 1) * LANES)
        stacked = []
        for row in range(ATT_ROWS):
            qp = q_ref[row * nq:(row + 1) * nq, cols].astype(F32) * scale
            stacked += [jnp.where(low, qp, 0.0), jnp.where(low, 0.0, qp)]
        qs_all = jnp.concatenate(stacked, axis=0).astype(BF16)
        s_ctx_all = lax.dot_general(qs_all, kc_ref[:, cols], nt, preferred_element_type=F32)
        e_ctx_rows, e_loc_rows, dens = [], [], []
        for row in range(ATT_ROWS):
            part = slice(row * 2 * nq, (row + 1) * 2 * nq)
            kb = k_ref[pl.ds(starts[row], band), cols]
            s_loc = lax.dot_general(qs_all[part], kb, nt, preferred_element_type=F32)
            bias_ref = bias_refs[row]
            s_loc = s_loc + jnp.concatenate([bias_ref[0, 2 * hp], bias_ref[0, 2 * hp + 1]], axis=0)
            s_ctx = s_ctx_all[part]
            mx = jnp.maximum(jnp.max(s_loc, axis=-1, keepdims=True), jnp.max(s_ctx, axis=-1, keepdims=True))
            e_loc = jnp.exp(s_loc - mx)
            e_ctx = jnp.exp(s_ctx - mx)
            dens.append(jnp.sum(e_loc, axis=-1, keepdims=True) + jnp.sum(e_ctx, axis=-1, keepdims=True))
            e_loc_rows.append(e_loc.astype(BF16))
            e_ctx_rows.append(e_ctx.astype(BF16))
        o_ctx_all = jnp.dot(jnp.concatenate(e_ctx_rows, axis=0), vc_ref[:, cols], preferred_element_type=F32)
        for row in range(ATT_ROWS):
            part = slice(row * 2 * nq, (row + 1) * 2 * nq)
            vb = v_ref[pl.ds(starts[row], band), cols]
            o = (jnp.dot(e_loc_rows[row], vb, preferred_element_type=F32) + o_ctx_all[part]) / dens[row]
            o_ref[row * nq:(row + 1) * nq, cols] = jnp.where(low, o[:nq], o[nq:]).astype(BF16)


def _attention_bias(rpb, seq):
    rows = seq // GRID_W
    kr = min(NA_WIN_ROWS, rows)
    kc = NA_WIN_COLS
    cq = np.arange(GRID_W)
    c0 = np.clip(cq - kc // 2, 0, GRID_W - kc)
    ck = np.arange(GRID_W)
    inside = (ck[None, :] >= c0[:, None]) & (ck[None, :] < c0[:, None] + kc)
    dc = np.clip(ck[None, :] - cq[:, None] + (NA_WIN_COLS - 1), 0, 2 * NA_WIN_COLS - 2)
    n_dr = 2 * NA_WIN_ROWS - 1
    n_dc = 2 * NA_WIN_COLS - 1
    pick = jnp.asarray((np.arange(n_dc)[:, None] == dc.reshape(1, -1)).astype(np.float32))
    picked = jnp.dot(rpb.reshape(-1, n_dc), pick, precision=lax.Precision.HIGHEST)
    picked = picked.reshape(NA_HEADS, 2 * NA_WIN_ROWS - 1, GRID_W, GRID_W)
    table = jnp.where(jnp.asarray(inside)[None, None], picked, NEG_BIG)
    table = table.transpose(0, 2, 1, 3).reshape(NA_HEADS, GRID_W, n_dr * GRID_W)
    classes = []
    for cl in range(kr):
        lo = (NA_WIN_ROWS - 1 - cl) * GRID_W
        classes.append(table[:, :, lo:lo + kr * GRID_W])
    classes.append(jnp.full((NA_HEADS, GRID_W, kr * GRID_W), NEG_BIG, F32))
    return jnp.stack(classes, axis=0)


def _post_kernel(ha_ref, hc_ref, att_ref, zb_ref, hf_ref, hb_ref, lg_ref,
                 sh1_ref, sc1_ref, g1_ref, sh2_ref, sc2_ref,
                 wgl0_ref, wgl1_ref, wgl2_ref, bgl0_ref, bgl1_ref, bgl2_ref,
                 wpa_ref, wpc_ref, wpl_ref, wo_ref, bo_ref, l1g_ref, l1b_ref,
                 rw_ref, rb_ref, tri_ref, upper_ref,
                 h1_ref, xs_ref, pos_ref, cnt_ref, u_s, logit_s, *, split, n_tiles):
    i = pl.program_id(0)

    @pl.when(i == 0)
    def _():
        u_s[...] = jnp.zeros_like(u_s)
        logit_s[...] = jnp.zeros_like(logit_s)

    _route_and_sort(u_s[...], logit_s[...], tri_ref, upper_ref, xs_ref, pos_ref, cnt_ref)

    h = jnp.where(jnp.minimum(i, n_tiles - 1) < split, ha_ref[...], hc_ref[...])
    u1 = (_layer_norm(h) * (1.0 + sc1_ref[0]) + sh1_ref[0]).astype(BF16)
    y_a = jnp.dot(att_ref[...], wpa_ref[...], preferred_element_type=F32)
    y_b = jnp.dot(zb_ref[...], wpc_ref[...], preferred_element_type=F32)
    zc = jax.nn.gelu(lg_ref[...]) * (hf_ref[...] + hb_ref[...])
    y_c = jnp.dot(zc.astype(BF16), wpl_ref[...], preferred_element_type=F32)
    merged = (jax.nn.sigmoid(jnp.dot(u1, wgl0_ref[0], preferred_element_type=F32) + bgl0_ref[0]) * y_a
              + jax.nn.sigmoid(jnp.dot(u1, wgl1_ref[0], preferred_element_type=F32) + bgl1_ref[0]) * y_b
              + jax.nn.sigmoid(jnp.dot(u1, wgl2_ref[0], preferred_element_type=F32) + bgl2_ref[0]) * y_c)
    y = jnp.dot(merged.astype(BF16), wo_ref[...], preferred_element_type=F32) + bo_ref[...]
    h1 = _layer_norm(DEEPNORM_ALPHA * h + g1_ref[0] * y) * l1g_ref[...] + l1b_ref[...]
    h1_ref[...] = h1
    u2 = _layer_norm(h1) * (1.0 + sc2_ref[0]) + sh2_ref[0]

    u_hi = u2.astype(BF16)
    u_lo = (u2 - u_hi.astype(F32)).astype(BF16)
    by_hi = jnp.dot(u_hi, rw_ref[...], preferred_element_type=F32)
    logit_s[...] = (by_hi[:, :LANES] + by_hi[:, LANES:]
                    + jnp.dot(u_lo, rw_ref[:, :LANES], preferred_element_type=F32) + rb_ref[...])
    u_s[...] = u_hi


def _route_and_sort(u_hi, logits, tri_ref, upper_ref, xs_ref, pos_ref, cnt_ref):
    tm = logits.shape[0]
    lane = lax.broadcasted_iota(jnp.int32, (tm, LANES), 1)
    lane_f = lane.astype(F32)
    work = logits
    tops, idxs, hots = [], [], []
    for _ in range(TOP_K):
        mx = jnp.max(work, axis=-1, keepdims=True)
        idx = jnp.min(jnp.where(work == mx, lane_f, float(LANES)), axis=-1, keepdims=True)
        hot = lane_f == idx
        work = jnp.where(hot, -3e38, work)
        tops.append(mx); idxs.append(idx); hots.append(hot)
    exps = [jnp.exp(t - tops[0]) for t in tops]
    den = exps[0] + exps[1] + exps[2] + exps[3]
    hot_all = jnp.zeros((tm, LANES), F32)
    for hot in hots:
        hot_all = hot_all + hot.astype(F32)
    cnt = jnp.sum(hot_all, axis=0, keepdims=True)
    cnt_pad = jnp.floor((cnt + (SUBLANES - 1.0)) * (1.0 / SUBLANES)) * SUBLANES
    off = jnp.dot(jnp.broadcast_to(cnt_pad, (SUBLANES, LANES)), upper_ref[...],
                  precision=lax.Precision.HIGHEST, preferred_element_type=F32)[0:1]
    slot = off + jnp.dot(tri_ref[...], hot_all.astype(BF16), preferred_element_type=F32)
    pos4 = jnp.zeros((tm, LANES), F32)
    w_tile = jnp.zeros((tm, LANES), F32)
    for kk in range(TOP_K):
        pos_k = jnp.sum(jnp.where(hots[kk], slot, 0.0), axis=-1, keepdims=True)
        pos4 = jnp.where(lane == kk, pos_k, pos4)
        p = exps[kk] / den
        p_hi = p.astype(BF16).astype(F32)
        p_mid = (p - p_hi).astype(BF16).astype(F32)
        p_lo = p - p_hi - p_mid
        w_tile = jnp.where(hots[kk], p_hi, w_tile)
        w_tile = jnp.where(lane_f == idxs[kk] + float(N_EXPERTS), p_mid, w_tile)
        w_tile = jnp.where(lane_f == idxs[kk] + float(2 * N_EXPERTS), p_lo, w_tile)
    pos_ref[...] = pos4
    sub = lax.broadcasted_iota(jnp.int32, (SUBLANES, LANES), 0)
    cnt_ref[...] = jnp.where(sub == 0, cnt, jnp.where(sub == 1, off, 0.0))

    n_sorted = xs_ref.shape[0]
    pos_t = pos4.T
    r_iota = lax.broadcasted_iota(jnp.int32, (n_sorted, tm), 0).astype(F32)
    hit = r_iota == pos_t[0:1, :]
    for kk in range(1, TOP_K):
        hit = jnp.logical_or(hit, r_iota == pos_t[kk:kk + 1, :])
    perm = jnp.where(hit, 1.0, 0.0).astype(BF16)
    feats = jnp.concatenate([u_hi, w_tile.astype(BF16)], axis=1)
    xs_ref[...] = jnp.dot(perm, feats, preferred_element_type=F32)


def _post_mixer(h_parts, att, zb, hf, hb, rest, mod3, mod_base, layer, w_in_bf, b_in, wpa, wpc, wpl, wo, b_o, l1g,
                l1b, rw_pad, rb_pad, n_rows, n_lat, seq):
    tm = TM_POST
    width = BRANCH_WIDTH
    ha, hc = h_parts[0], h_parts[-1]
    split = min(ha.shape[0], n_rows) // tm
    gate_col0 = N_EARLY // D_MODEL
    tiles_per_seq = seq // tm
    n_lat_tiles = n_lat // tm
    n_groups_lat = n_lat // seq
    tri = jnp.asarray(np.tril(np.ones((tm, tm), np.float32), -1), BF16)
    upper = jnp.asarray(np.triu(np.ones((LANES, LANES), np.float32), 1))
    n_tiles = n_rows // tm

    def dense(i):
        return jnp.minimum(i, n_tiles - 1)

    def routed(i):
        return jnp.maximum(i - 1, 0)

    def group(i):
        return jnp.where(dense(i) < n_lat_tiles, dense(i) // tiles_per_seq, n_groups_lat)

    def rows(wd, col=0):
        return pl.BlockSpec((tm, wd), lambda i: (dense(i), col))

    def full(shape):
        return pl.BlockSpec(shape, lambda i: (0,) * len(shape))

    def mod(which):
        return pl.BlockSpec((1, 1, D_MODEL), lambda i: (mod_base + group(i) * 6 + which, 0, 0))

    in_specs = [pl.BlockSpec((tm, D_MODEL), lambda i: (jnp.minimum(dense(i), split - 1), 0)),
                pl.BlockSpec((tm, D_MODEL), lambda i: (jnp.maximum(dense(i) - split, 0), 0)),
                rows(width), rows(width), rows(width), rows(width), rows(width, 3),
                mod(0), mod(1), mod(2), mod(3), mod(4)]
    in_specs += [pl.BlockSpec((1, D_MODEL, D_MODEL), lambda i, c=c: (layer, 0, gate_col0 + c)) for c in range(3)]
    in_specs += [pl.BlockSpec((1, 1, D_MODEL), lambda i, c=c: (layer, 0, gate_col0 + c)) for c in range(3)]
    in_specs += [full(wpa.shape), full(wpc.shape), full(wpl.shape), full(wo.shape), full((1, D_MODEL)),
                full((1, D_MODEL)), full((1, D_MODEL)), full(rw_pad.shape), full(rb_pad.shape), full(tri.shape),
                full(upper.shape)]
    out_specs = [rows(D_MODEL), pl.BlockSpec((TOK_BLOCK, XS_WIDTH), lambda i: (routed(i), 0)),
                 pl.BlockSpec((tm, LANES), lambda i: (routed(i), 0)),
                 pl.BlockSpec((SUBLANES, LANES), lambda i: (routed(i), 0))]
    return pl.pallas_call(
        functools.partial(_post_kernel, split=split, n_tiles=n_tiles),
        grid=(n_tiles + 1,),
        in_specs=in_specs, out_specs=out_specs,
        out_shape=[jax.ShapeDtypeStruct((n_rows, D_MODEL), F32),
                   jax.ShapeDtypeStruct((n_tiles * TOK_BLOCK, XS_WIDTH), F32),
                   jax.ShapeDtypeStruct((n_rows, LANES), F32),
                   jax.ShapeDtypeStruct((n_tiles * SUBLANES, LANES), F32)],
        scratch_shapes=[pltpu.VMEM((tm, D_MODEL), BF16), pltpu.VMEM((tm, LANES), F32)],
        compiler_params=_params(1),
        name="post_mixer",
    )(ha, hc, att, zb, hf, hb, rest, mod3, mod3, mod3, mod3, mod3,
      w_in_bf, w_in_bf, w_in_bf, b_in.reshape(DEPTH, 1, -1), b_in.reshape(DEPTH, 1, -1), b_in.reshape(DEPTH, 1, -1),
      wpa, wpc, wpl, wo, b_o.reshape(1, D_MODEL), l1g.reshape(1, D_MODEL), l1b.reshape(1, D_MODEL),
      rw_pad, rb_pad, tri, upper)


def _expert_plan(cnt_out, n_tok_tiles, n_rows):
    tm = TM_EXPERT
    co = cnt_out.reshape(n_tok_tiles, SUBLANES, LANES)
    cnt = (co[:, 0, :N_EXPERTS].astype(jnp.int32) + SUBLANES - 1) // SUBLANES * SUBLANES
    off = co[:, 1, :N_EXPERTS].astype(jnp.int32)
    cum_end = jnp.cumsum(cnt, axis=0)
    cum = cum_end - cnt
    total = cum_end[-1]
    n_et = (total + tm - 1) // tm
    et_end = jnp.cumsum(n_et)
    n_act = et_end[-1:].astype(jnp.int32)
    n_tiles = -(-n_tok_tiles * TOK_BLOCK // tm) + N_EXPERTS
    j = jnp.arange(n_tiles, dtype=jnp.int32)
    tile_e = jnp.minimum(jnp.sum((et_end[None, :] <= j[:, None]).astype(jnp.int32), axis=1), N_EXPERTS - 1)
    pick_e = (tile_e[:, None] == jnp.arange(N_EXPERTS, dtype=jnp.int32)[None, :]).astype(F32)

    def per_tile(table):
        return jnp.dot(pick_e, table.astype(F32), precision=lax.Precision.HIGHEST).astype(jnp.int32)

    row0 = (j - per_tile(et_end - n_et)) * tm
    n_rows_tile = jnp.clip(per_tile(total) - row0, 0, tm)
    cum_e = per_tile(cum.T)
    cum_end_e = per_tile(cum_end.T)
    delta_e = per_tile((jnp.arange(n_tok_tiles, dtype=jnp.int32)[:, None] * TOK_BLOCK + off - cum).T)
    q = row0[:, None] + SUBLANES * jnp.arange(tm // SUBLANES, dtype=jnp.int32)[None, :]
    inside = jnp.logical_and(cum_e.T[:, :, None] <= q[None], q[None] < cum_end_e.T[:, :, None])
    src = q + jnp.sum(jnp.where(inside, delta_e.T[:, :, None], 0), axis=0)
    used = off[:, N_EXPERTS - 1] + cnt[:, N_EXPERTS - 1]
    first = jnp.concatenate([jnp.ones((1,), jnp.int32), (tile_e[1:] != tile_e[:-1]).astype(jnp.int32)])
    group = jnp.cumsum(first) - 1
    after = per_tile(et_end)
    next_e = jnp.where(after < n_act[0],
                       jnp.minimum(jnp.sum((et_end[None, :] <= after[:, None]).astype(jnp.int32), axis=1),
                                   N_EXPERTS - 1), -1)
    return (tile_e.astype(jnp.int32), n_rows_tile.astype(jnp.int32), n_act, src.reshape(-1).astype(jnp.int32),
            used.astype(jnp.int32), first, group.astype(jnp.int32), next_e.astype(jnp.int32), n_tiles)


def _expert_kernel(tile_e, n_rows_t, n_act, src_t, used_t, first_t, group_t, next_t,
                   xs_hbm, wgu_hbm, bgu_ref, wdn_hbm, bdn_ref, ys_hbm,
                   xin, yout, wgu_f, wdn_f, wgu_s, wdn_s, zeros, sem_in, sem_out, sem_zero, sem_wgu, sem_wdn,
                   *, tm, tok_block, n_tok_tiles, layer):
    j = pl.program_id(0)
    na = n_act[0]
    chunks = tm // SUBLANES

    def weight_copies(e, wslot):
        return (pltpu.make_async_copy(wgu_hbm.at[layer, e], wgu_f.at[wslot], sem_wgu.at[wslot]),
                pltpu.make_async_copy(wdn_hbm.at[layer, e], wdn_f.at[wslot], sem_wdn.at[wslot]))

    def gather(src, dst, size, slot):
        return pltpu.make_async_copy(xs_hbm.at[pl.ds(src, size)], xin.at[slot, pl.ds(dst, size)], sem_in.at[slot])

    def scatter(src, dst, size, slot):
        return pltpu.make_async_copy(yout.at[slot, pl.ds(dst, size)], ys_hbm.at[pl.ds(src, size)], sem_out.at[slot])

    def start_chunks(jj, slot, copy):
        def one(c, priority):
            src = pl.multiple_of(src_t[jj * chunks + c], SUBLANES)
            copy(src, pl.multiple_of(c * SUBLANES, SUBLANES), SUBLANES, slot).start(priority=priority)

        def body(c, carry):
            one(c, 0)
            return carry

        def body_unrolled(g, carry):
            for u in range(SUBLANES):
                one(g * SUBLANES + u, u % 2)
            return carry

        n = lax.shift_right_logical(n_rows_t[jj], 3)

        @pl.when(n == chunks)
        def _():
            lax.fori_loop(0, chunks // SUBLANES, body_unrolled, 0)

        @pl.when(n != chunks)
        def _():
            lax.fori_loop(0, n, body, 0)

    def wait_rows(jj, slot, copy):
        n = n_rows_t[jj]
        size = tm
        while size >= SUBLANES:
            @pl.when((n & size) != 0)
            def _(size=size):
                copy(0, 0, size, slot).wait()
            size //= 2

    slot = j % 2

    @pl.when(j == 0)
    def _():
        xin[...] = jnp.zeros_like(xin)
        start_chunks(0, 0, gather)
        zeros[...] = jnp.zeros_like(zeros)

        def clear_tail(i, copy_op):
            used = used_t[i]
            tail = tok_block - used
            size = ZERO_ROWS
            sizes = []
            while size >= SUBLANES:
                sizes.append(size)
                size //= 2
            for size in sizes:
                @pl.when((tail & size) != 0)
                def _(size=size):
                    at = pl.multiple_of(i * tok_block + used + (tail & ~(2 * size - 1)), SUBLANES)
                    copy_op(pltpu.make_async_copy(zeros.at[pl.ds(0, size)], ys_hbm.at[pl.ds(at, size)], sem_zero))

        def start_clear(i, c):
            clear_tail(i, lambda cp: cp.start())
            return c

        def wait_clear(i, c):
            clear_tail(i, lambda cp: cp.wait())
            return c

        lax.fori_loop(0, n_tok_tiles, start_clear, 0)
        lax.fori_loop(0, n_tok_tiles, wait_clear, 0)

    @pl.when(j + 1 < na)
    def _():
        start_chunks(j + 1, 1 - slot, gather)

    @pl.when(j < na)
    def _():
        e = tile_e[j]

        @pl.when(first_t[j] == 1)
        def _():
            wslot = group_t[j] % 2

            @pl.when(j == 0)
            def _():
                for cp in weight_copies(e, wslot):
                    cp.start()

            for cp in weight_copies(e, wslot):
                cp.wait()
            wgu_s[...] = wgu_f[wslot].astype(BF16)
            wdn_s[...] = wdn_f[wslot].astype(BF16)

            @pl.when(next_t[j] >= 0)
            def _():
                for cp in weight_copies(next_t[j], 1 - wslot):
                    cp.start()

        wait_rows(j, slot, gather)

        def compute(rows):
            x = xin[slot, pl.ds(0, rows)]
            lane = lax.broadcasted_iota(jnp.int32, (rows, LANES), 1)
            p = jnp.sum(jnp.where(lane % N_EXPERTS == e, x[:, D_MODEL:], 0.0), axis=-1, keepdims=True)
            gu = (jnp.dot(x[:, :D_MODEL].astype(BF16), wgu_s[...], preferred_element_type=F32)
                  + bgu_ref[0, pl.ds(e, 1), :])
            f = gu.shape[1] // 2
            gate = jnp.minimum(gu[:, :f], SWIGLU_LIMIT)
            up = jnp.clip(gu[:, f:], -SWIGLU_LIMIT, SWIGLU_LIMIT)
            hid = (up + 1.0) * gate * jax.nn.sigmoid(SWIGLU_ALPHA * gate)
            y = (jnp.dot(hid.astype(BF16), wdn_s[...], preferred_element_type=F32)
                 + bdn_ref[0, pl.ds(e, 1), :])
            yout[slot, pl.ds(0, rows)] = y * p

        quarter = tm // 4
        for part in range(1, 5):
            @pl.when(jnp.logical_and(n_rows_t[j] > (part - 1) * quarter, n_rows_t[j] <= part * quarter))
            def _(part=part):
                compute(part * quarter)

        start_chunks(j, slot, scatter)

        @pl.when(j >= 1)
        def _():
            wait_rows(j - 1, 1 - slot, scatter)

        @pl.when(j == na - 1)
        def _():
            wait_rows(j, slot, scatter)


def _experts(xs, plan, layer, w_gu, b_gu, w_dn, b_dn):
    tile_e, n_rows_tile, n_act, src, used, first, group, next_e, n_tiles = plan
    tm = TM_EXPERT
    f2 = w_gu.shape[-1]

    in_specs = [pl.BlockSpec(memory_space=pl.ANY),
                pl.BlockSpec(memory_space=pl.ANY),
                pl.BlockSpec((1, N_EXPERTS, f2), lambda j, *_: (layer, 0, 0)),
                pl.BlockSpec(memory_space=pl.ANY),
                pl.BlockSpec((1, N_EXPERTS, D_MODEL), lambda j, *_: (layer, 0, 0))]
    grid_spec = pltpu.PrefetchScalarGridSpec(
        num_scalar_prefetch=8, grid=(n_tiles,), in_specs=in_specs,
        out_specs=pl.BlockSpec(memory_space=pl.ANY),
        scratch_shapes=[pltpu.VMEM((2, tm, XS_WIDTH), F32), pltpu.VMEM((2, tm, D_MODEL), F32),
                        pltpu.VMEM((2, D_MODEL, f2), F32), pltpu.VMEM((2, f2 // 2, D_MODEL), F32),
                        pltpu.VMEM((D_MODEL, f2), BF16), pltpu.VMEM((f2 // 2, D_MODEL), BF16),
                        pltpu.VMEM((ZERO_ROWS, D_MODEL), F32),
                        pltpu.SemaphoreType.DMA((2,)), pltpu.SemaphoreType.DMA((2,)), pltpu.SemaphoreType.DMA(()),
                        pltpu.SemaphoreType.DMA((2,)), pltpu.SemaphoreType.DMA((2,))])
    return pl.pallas_call(
        functools.partial(_expert_kernel, tm=tm, tok_block=TOK_BLOCK, n_tok_tiles=xs.shape[0] // TOK_BLOCK,
                          layer=layer),
        grid_spec=grid_spec,
        out_shape=jax.ShapeDtypeStruct((xs.shape[0], D_MODEL), F32),
        compiler_params=_params(1),
        name="moe_experts",
    )(tile_e, n_rows_tile, n_act, src, used, first, group, next_e,
      xs, w_gu, b_gu, w_dn, b_dn)


def _combine_kernel(ys_ref, pos_ref, h1_ref, g2_ref, l2g_ref, l2b_ref, o_ref):
    tm = pos_ref.shape[0]
    n_sorted = ys_ref.shape[0]
    pos = pos_ref[...]
    col = lax.broadcasted_iota(jnp.int32, (tm, n_sorted), 1).astype(F32)
    sel = jnp.zeros((tm, n_sorted), F32)
    for kk in range(TOP_K):
        sel = sel + (col == pos[:, kk:kk + 1]).astype(F32)
    sel = sel.astype(BF16)
    ys = ys_ref[...]
    hi = ys.astype(BF16)
    mid = (ys - hi.astype(F32)).astype(BF16)
    y2 = jnp.dot(sel, hi, preferred_element_type=F32) + jnp.dot(sel, mid, preferred_element_type=F32)
    o_ref[...] = _layer_norm(DEEPNORM_ALPHA * h1_ref[...] + g2_ref[0] * y2) * l2g_ref[...] + l2b_ref[...]


def _combine(ys, pos4, h1, mod3, mod_base, l2g, l2b, n_lat, seq):
    n_rows = h1.shape[0]
    tm = TM_POST
    tiles_per_seq = seq // tm
    n_lat_tiles = n_lat // tm
    n_groups_lat = n_lat // seq

    def group(i):
        return jnp.where(i < n_lat_tiles, i // tiles_per_seq, n_groups_lat)

    in_specs = [pl.BlockSpec((TOK_BLOCK, D_MODEL), lambda i: (i, 0)),
                pl.BlockSpec((tm, LANES), lambda i: (i, 0)),
                pl.BlockSpec((tm, D_MODEL), lambda i: (i, 0)),
                pl.BlockSpec((1, 1, D_MODEL), lambda i: (mod_base + group(i) * 6 + 5, 0, 0)),
                pl.BlockSpec((1, D_MODEL), lambda i: (0, 0)),
                pl.BlockSpec((1, D_MODEL), lambda i: (0, 0))]
    return pl.pallas_call(
        _combine_kernel,
        grid=(n_rows // tm,),
        in_specs=in_specs,
        out_specs=pl.BlockSpec((tm, D_MODEL), lambda i: (i, 0)),
        out_shape=jax.ShapeDtypeStruct((n_rows, D_MODEL), F32),
        compiler_params=_params(1),
        name="moe_combine",
    )(ys, pos4, h1, mod3, l2g.reshape(1, D_MODEL), l2b.reshape(1, D_MODEL))


def _block_diag(w):
    two, n, d, e = w.shape
    eye = jnp.eye(n, dtype=w.dtype)
    return (w[:, :, :, None, :] * eye[None, :, None, :, None]).reshape(two, n * d, n * e)


def kernel(x, c, ctx, c_ctx, w_mod, b_mod, w_in, b_in, na_rpb, w_proj_attn, w_proj_conv, w_proj_lru, sc_conv_w, lru_conv_w, lru_conv_b, lru_lambda, lru_w_r, lru_b_r, lru_w_i, lru_b_i, w_o, b_o, ln1_g, ln1_b, router_w, router_b, exp_w_gu, exp_b_gu, exp_w_dn, exp_b_dn, ln2_g, ln2_b):
    n_batch, seq, d = x.shape
    n_ctx = ctx.shape[1]
    n_lat = n_batch * seq
    n_all = n_lat + n_batch * n_ctx
    assert d == D_MODEL and n_batch + 1 <= SUBLANES

    cc = jnp.concatenate([c, c_ctx[None], jnp.zeros((SUBLANES - n_batch - 1, d), F32)], axis=0)
    mod = _modulation(cc, w_mod, b_mod)
    groups = n_batch + 1
    mod3 = mod.reshape(DEPTH, SUBLANES, 6, d)[:, :groups].reshape(DEPTH * groups * 6, 1, d)

    cos_t, sin_t = _make_rope(seq, TM_INPROJ)
    h = (x.reshape(n_lat, d), ctx.reshape(n_batch * n_ctx, d))
    w_in_bf = w_in.astype(BF16)

    for layer in range(DEPTH):
        last = layer == DEPTH - 1
        mod_base = layer * groups * 6
        q, k, v, sb, rest = _input_projection(h, mod3, mod_base, layer, w_in_bf, b_in, N_EARLY,
                                              cos_t, sin_t, n_lat, seq)
        sp = jax.nn.softplus(-lru_lambda[layer])
        zb, hf, hb, att = _token_mixers(
            q, k, v, _attention_bias(na_rpb[layer], seq), sb, rest, sc_conv_w[layer], lru_conv_w[layer],
            lru_conv_b[layer], sp, _block_diag(lru_w_r[layer]).astype(BF16), _block_diag(lru_w_i[layer]).astype(BF16),
            lru_b_r[layer], lru_b_i[layer], n_batch, seq, n_ctx)
        n_rows = n_lat if last else n_all
        rw_full = jnp.pad(router_w[layer], ((0, 0), (0, LANES - N_EXPERTS)))
        rw_hi = rw_full.astype(BF16)
        rw_pad = jnp.concatenate([rw_hi, (rw_full - rw_hi.astype(F32)).astype(BF16)], axis=1)
        rb_pad = jnp.concatenate([router_b[layer], jnp.full((LANES - N_EXPERTS,), NEG_BIG, F32)]).reshape(1, LANES)
        h1, xs, pos4, cnt_out = _post_mixer(
            h, att, zb, hf, hb, rest, mod3, mod_base, layer, w_in_bf, b_in,
            w_proj_attn[layer].astype(BF16), w_proj_conv[layer].astype(BF16), w_proj_lru[layer].astype(BF16),
            w_o[layer].astype(BF16), b_o[layer], ln1_g[layer], ln1_b[layer], rw_pad, rb_pad, n_rows, n_lat, seq)
        plan = _expert_plan(cnt_out, n_rows // TM_POST, n_rows)
        ys = _experts(xs, plan, layer, exp_w_gu, exp_b_gu, exp_w_dn, exp_b_dn)
        h = (_combine(ys, pos4, h1, mod3, mod_base, ln2_g[layer], ln2_b[layer], n_lat, seq),)
    return h[0].reshape(n_batch, seq, d)
```

```python
import functools

import numpy as np
import jax
import jax.numpy as jnp
from jax import lax
from jax.experimental import pallas as pl
from jax.experimental.pallas import tpu as pltpu

D_MODEL = 1024
DEPTH = 2
GRID_W = 64
NA_HEADS = 8
NA_HEAD_DIM = 64
NA_WIN_ROWS = 8
NA_WIN_COLS = 16
ROPE_BASE = 10000.0
BRANCH_WIDTH = 512
LRU_C = 8.0
N_EARLY = 8 * BRANCH_WIDTH
N_EXPERTS = 32
TOP_K = 4
SWIGLU_LIMIT = 7.0
SWIGLU_ALPHA = 1.702
LN_EPS = 1e-5
DEEPNORM_ALPHA = (2 * DEPTH) ** 0.25
NEG_BIG = -1e30

LANES = 128
SUBLANES = 8
VMEM_LIMIT_BYTES = 56 * 1024 * 1024

TM_INPROJ = 512
SCAN_CHUNK = 256
ATT_ROWS = 4
ATT_QROWS = ATT_ROWS * GRID_W
TM_POST = 256
TM_EXPERT = 512
XS_WIDTH = D_MODEL + LANES
TOK_BLOCK = TM_POST * TOP_K + N_EXPERTS * SUBLANES
ZERO_ROWS = TOK_BLOCK - TM_POST * TOP_K

F32 = jnp.float32
BF16 = jnp.bfloat16


def _params(n_axes):
    return pltpu.CompilerParams(dimension_semantics=("arbitrary",) * n_axes,
                                vmem_limit_bytes=VMEM_LIMIT_BYTES)


def _layer_norm(x):
    mu = jnp.mean(x, axis=-1, keepdims=True)
    xc = x - mu
    var = jnp.mean(xc * xc, axis=-1, keepdims=True)
    return xc * lax.rsqrt(var + LN_EPS)


def _mod_kernel(c_ref, w_ref, b_ref, o_ref):
    c = c_ref[...]
    s = (c * jax.nn.sigmoid(c)).astype(BF16)
    o_ref[0] = jnp.dot(s, w_ref[0].astype(BF16), preferred_element_type=F32) + b_ref[0]


def _modulation(cc, w_mod, b_mod):
    n_out = w_mod.shape[-1]
    return pl.pallas_call(
        _mod_kernel,
        grid=(DEPTH, n_out // D_MODEL),
        in_specs=[pl.BlockSpec((SUBLANES, D_MODEL), lambda l, j: (0, 0)),
                  pl.BlockSpec((1, D_MODEL, D_MODEL), lambda l, j: (l, 0, j)),
                  pl.BlockSpec((1, 1, D_MODEL), lambda l, j: (l, 0, j))],
        out_specs=pl.BlockSpec((1, SUBLANES, D_MODEL), lambda l, j: (l, 0, j)),
        out_shape=jax.ShapeDtypeStruct((DEPTH, SUBLANES, n_out), F32),
        compiler_params=_params(2),
        name="modulation",
    )(cc, w_mod, b_mod.reshape(DEPTH, 1, n_out))


def _rope_half(x, cos, sin_signed):
    m = NA_HEAD_DIM // 4
    lane = lax.broadcasted_iota(jnp.int32, (x.shape[0], LANES), 1)
    first = (lane % (2 * m)) < m
    outs = []
    for cidx in range(x.shape[1] // LANES):
        xc = x[:, cidx * LANES:(cidx + 1) * LANES]
        partner = jnp.where(first, pltpu.roll(xc, LANES - m, 1), pltpu.roll(xc, m, 1))
        outs.append(xc * cos + partner * sin_signed)
    return jnp.concatenate(outs, axis=1)


def _inproj_kernel(ha_ref, hb_ref, sh_ref, sc_ref, w_ref, b_ref, cos_ref, sin_ref,
                   q_ref, k_ref, v_ref, sb_ref, rest_ref, *, split):
    h = jnp.where(pl.program_id(0) < split, ha_ref[...], hb_ref[...])
    xn = (_layer_norm(h) * (1.0 + sc_ref[0]) + sh_ref[0]).astype(BF16)
    half = BRANCH_WIDTH

    def columns(lo, width):
        return jnp.dot(xn, w_ref[0, :, lo:lo + width], preferred_element_type=F32) + b_ref[0, :, lo:lo + width]

    cos = cos_ref[...]
    sin = sin_ref[...]
    q_ref[...] = _rope_half(columns(0, half), cos, sin).astype(BF16)
    k_ref[...] = _rope_half(columns(half, half), cos, sin).astype(BF16)
    v_ref[...] = columns(2 * half, half).astype(BF16)
    sb_ref[...] = columns(3 * half, half)
    n_rest = rest_ref.shape[1]
    for lo in range(0, n_rest, 2 * half):
        rest_ref[:, lo:lo + 2 * half] = columns(4 * half + lo, 2 * half)


def _input_projection(h_parts, mod3, mod_base, layer, w_in_bf, b_in, n_cols, cos_t, sin_t, n_lat, seq):
    tm = TM_INPROJ
    ha, hb = h_parts[0], h_parts[-1]
    split = ha.shape[0] // tm
    m = ha.shape[0] + (hb.shape[0] if len(h_parts) == 2 else 0)
    n_lat_tiles = n_lat // tm
    tiles_per_seq = seq // tm
    n_groups_lat = n_lat // seq

    def group(i):
        return jnp.where(i < n_lat_tiles, i // tiles_per_seq, n_groups_lat)

    def rope_blk(i):
        return jnp.where(i < n_lat_tiles, i % tiles_per_seq, tiles_per_seq)

    half = BRANCH_WIDTH
    return pl.pallas_call(
        functools.partial(_inproj_kernel, split=split),
        grid=(m // tm,),
        in_specs=[pl.BlockSpec((tm, D_MODEL), lambda i: (jnp.minimum(i, split - 1), 0)),
                  pl.BlockSpec((tm, D_MODEL), lambda i: (jnp.maximum(i - split, 0), 0)),
                  pl.BlockSpec((1, 1, D_MODEL), lambda i: (mod_base + group(i) * 6 + 0, 0, 0)),
                  pl.BlockSpec((1, 1, D_MODEL), lambda i: (mod_base + group(i) * 6 + 1, 0, 0)),
                  pl.BlockSpec((1, D_MODEL, n_cols), lambda i: (layer, 0, 0)),
                  pl.BlockSpec((1, 1, n_cols), lambda i: (layer, 0, 0)),
                  pl.BlockSpec((tm, LANES), lambda i: (rope_blk(i), 0)),
                  pl.BlockSpec((tm, LANES), lambda i: (rope_blk(i), 0))],
        out_specs=[pl.BlockSpec((tm, half), lambda i: (i, 0)),
                   pl.BlockSpec((tm, half), lambda i: (i, 0)),
                   pl.BlockSpec((tm, half), lambda i: (i, 0)),
                   pl.BlockSpec((tm, half), lambda i: (i, 0)),
                   pl.BlockSpec((tm, n_cols - 4 * half), lambda i: (i, 0))],
        out_shape=[jax.ShapeDtypeStruct((m, half), BF16),
                   jax.ShapeDtypeStruct((m, half), BF16),
                   jax.ShapeDtypeStruct((m, half), BF16),
                   jax.ShapeDtypeStruct((m, half), F32),
                   jax.ShapeDtypeStruct((m, n_cols - 4 * half), F32)],
        compiler_params=_params(1),
        name="input_projection",
    )(ha, hb, mod3, mod3, w_in_bf, b_in.reshape(DEPTH, 1, -1), cos_t, sin_t)


def _make_rope(seq, tm):
    rows = seq // GRID_W
    d = np.arange(LANES) % NA_HEAD_DIM
    m = NA_HEAD_DIM // 4
    inv_freq = (ROPE_BASE ** (-jnp.arange(m, dtype=F32) / m))[d % m]
    n_pos = max(rows, GRID_W)
    ang = jnp.arange(n_pos, dtype=F32)[:, None] * inv_freq[None, :]
    by_row = jnp.asarray(d < 2 * m)[None, None, :]

    def expand(tab):
        full = jnp.where(by_row, tab[:rows, None, :], tab[None, :GRID_W, :])
        return full.reshape(seq, LANES)

    cos = expand(jnp.cos(ang))
    sin = expand(jnp.sin(ang))
    sin_signed = jnp.where(jnp.asarray((d % (2 * m)) < m)[None, :], -sin, sin)
    cos = jnp.concatenate([cos, jnp.ones((tm, LANES), F32)], axis=0)
    sin_signed = jnp.concatenate([sin_signed, jnp.zeros((tm, LANES), F32)], axis=0)
    return cos, sin_signed


def _mixer_kernel(fblk, bblk, first, last, seqb, r0t, cls,
                  sb_ref, xf_ref, xfp_ref, xfn_ref, lxb_ref, lxbp_ref, lxbn_ref,
                  scw_ref, cw_ref, cb_ref, sp_ref, wr_ref, wi_ref, br_ref, bi_ref,
                  q_ref, k_ref, v_ref, kc_ref, vc_ref, bias0_ref, bias1_ref, bias2_ref, bias3_ref,
                  zb_ref, hf_ref, hb_ref, att_ref,
                  a_s, b_s, hc_s, st_s, *, ch, n_ctx_items, band):
    it = pl.program_id(0)
    is_first = first[it] == 1
    is_last = last[it] == 1
    is_ctx = it < n_ctx_items
    b = seqb[it]
    width = BRANCH_WIDTH
    row = lax.broadcasted_iota(jnp.int32, (ch, width), 0)
    not_first = jnp.where(is_first, 0.0, 1.0).astype(F32)
    not_last = jnp.where(is_last, 0.0, 1.0).astype(F32)

    def back1(u, prev_row):
        return jnp.where(row == 0, prev_row, pltpu.roll(u, 1, 0))

    def back2(u, prev2, prev1):
        return jnp.where(row == 0, prev2, jnp.where(row == 1, prev1, pltpu.roll(u, 2, 0)))

    def fwd1(u, next_row):
        return jnp.where(row == ch - 1, next_row, pltpu.roll(u, ch - 1, 0))

    gate_cols, value_cols, lru_cols = (slice(s * width, (s + 1) * width) for s in range(3))

    def lru_input(x, p, n, prev_ok, next_ok):
        p = p * prev_ok
        n = n * next_ok
        return (cw_ref[0:1] * back2(x, p[6:7], p[7:8]) + cw_ref[1:2] * back1(x, p[7:8])
                + cw_ref[2:3] * x + cw_ref[3:4] * fwd1(x, n[0:1]) + cb_ref[...])

    def coeffs(d, xm):
        xb = xm.astype(BF16)
        r = jax.nn.sigmoid(jnp.dot(xb, wr_ref[d], preferred_element_type=F32) + br_ref[d:d + 1])
        g = jax.nn.sigmoid(jnp.dot(xb, wi_ref[d], preferred_element_type=F32) + bi_ref[d:d + 1])
        log_a = (-LRU_C * sp_ref[d:d + 1]) * r
        a = jnp.exp(log_a)
        a_s[d] = a
        b_s[d] = jnp.sqrt(-jnp.tanh(log_a) * (a * a + 1.0)) * (g * xm)

    @pl.when(jnp.logical_and(is_first, is_ctx))
    def _():
        hc_s[...] = jnp.zeros_like(hc_s)

    @pl.when(jnp.logical_and(is_first, jnp.logical_not(is_ctx)))
    def _():
        hc_s[0:1] = st_s[pl.ds(2 * b, 1), :]
        hc_s[1:2] = st_s[pl.ds(2 * b + 1, 1), :]

    coeffs(0, lru_input(xf_ref[:, lru_cols], xfp_ref[:, lru_cols], xfn_ref[:, lru_cols], not_first, not_last))
    hf = hc_s[0:1]
    for t in range(ch):
        hf = a_s[0, t:t + 1, :] * hf + b_s[0, t:t + 1, :]
        hf_ref[t:t + 1, :] = hf
    hc_s[0:1] = hf

    coeffs(1, lru_input(lxb_ref[...], lxbp_ref[...], lxbn_ref[...], not_last, not_first))

    u = xf_ref[:, gate_cols] * xf_ref[:, value_cols]
    u_prev = xfp_ref[7:8, gate_cols] * xfp_ref[7:8, value_cols] * not_first
    u_next = xfn_ref[0:1, gate_cols] * xfn_ref[0:1, value_cols] * not_last
    conv = scw_ref[0:1] * back1(u, u_prev) + scw_ref[1:2] * u + scw_ref[2:3] * fwd1(u, u_next)
    zb_ref[...] = (sb_ref[...] * conv).astype(BF16)

    hb = hc_s[1:2]
    for t in range(ch - 1, -1, -1):
        hb = a_s[1, t:t + 1, :] * hb + b_s[1, t:t + 1, :]
        hb_ref[t:t + 1, :] = hb
    hc_s[1:2] = hb

    _attention_item(it, r0t, cls, q_ref, k_ref, v_ref, kc_ref, vc_ref,
                    (bias0_ref, bias1_ref, bias2_ref, bias3_ref), att_ref, band)

    @pl.when(is_ctx)
    def _():
        st_s[pl.ds(2 * b, 1), :] = hf
        st_s[pl.ds(2 * b + 1, 1), :] = hb


def _mixer_tables(n_batch, seq, n_ctx, ch):
    assert n_ctx == ch and ch == ATT_QROWS
    nc = seq // ch
    rows = seq // GRID_W
    kr = min(NA_WIN_ROWS, rows)
    ctx0 = n_batch * seq // ch
    fblk, bblk, first, last, seqb, r0t, cls = [], [], [], [], [], [], []
    for b in range(n_batch):
        fblk.append(ctx0 + b); bblk.append(ctx0 + b); first.append(1); last.append(1); seqb.append(b)
        r0t.extend([0] * ATT_ROWS); cls.extend([kr] * ATT_ROWS)
    for b in range(n_batch):
        for c in range(nc):
            fblk.append(b * nc + c); bblk.append(b * nc + nc - 1 - c)
            first.append(int(c == 0)); last.append(int(c == nc - 1)); seqb.append(b)
            for r in range(c * ATT_ROWS, (c + 1) * ATT_ROWS):
                r0 = min(max(r - kr // 2, 0), rows - kr)
                r0t.append(r0); cls.append(r - r0)
    return [np.asarray(a, np.int32) for a in (fblk, bblk, first, last, seqb, r0t, cls)]


def _token_mixers(q, k, v, bias, sb, rest, sc_w, lru_cw, lru_cb, sp, wr_bd, wi_bd, b_r, b_i, n_batch, seq, n_ctx):
    m = sb.shape[0]
    ch = SCAN_CHUNK
    width = BRANCH_WIDTH
    tables = _mixer_tables(n_batch, seq, n_ctx, ch)
    n_items = len(tables[0])
    halo_per_chunk = ch // SUBLANES
    last_halo = m // SUBLANES - 1
    band = min(NA_WIN_ROWS, seq // GRID_W) * GRID_W
    ctx_blk0 = n_batch * seq // n_ctx

    def cur(col, which, n=1):
        return pl.BlockSpec((ch, n * width), lambda i, f, bk, *_: ((f, bk)[which][i], col))

    def prev(col, which, n=1):
        return pl.BlockSpec((SUBLANES, n * width),
                            lambda i, f, bk, *_: (jnp.maximum((f, bk)[which][i] * halo_per_chunk - 1, 0), col))

    def nxt(col, which, n=1):
        return pl.BlockSpec((SUBLANES, n * width),
                            lambda i, f, bk, *_: (jnp.minimum(((f, bk)[which][i] + 1) * halo_per_chunk, last_halo), col))

    def full(shape):
        return pl.BlockSpec(shape, lambda i, *_: (0,) * len(shape))

    in_specs = [cur(0, 0), cur(0, 0, 3), prev(0, 0, 3), nxt(0, 0, 3),
                cur(2, 1), prev(2, 1), nxt(2, 1),
                full(sc_w.shape), full(lru_cw.shape), full((1, width)), full(sp.shape),
                full(wr_bd.shape), full(wi_bd.shape), full(b_r.shape), full(b_i.shape)]
    in_specs += [cur(0, 0),
                 pl.BlockSpec((seq, width), lambda i, f, bk, fi, la, sq, *_: (sq[i], 0)),
                 pl.BlockSpec((seq, width), lambda i, f, bk, fi, la, sq, *_: (sq[i], 0)),
                 pl.BlockSpec((n_ctx, width), lambda i, f, bk, fi, la, sq, *_: (ctx_blk0 + sq[i], 0)),
                 pl.BlockSpec((n_ctx, width), lambda i, f, bk, fi, la, sq, *_: (ctx_blk0 + sq[i], 0))]
    for row in range(ATT_ROWS):
        in_specs.append(pl.BlockSpec((1, NA_HEADS, GRID_W, band),
                                     lambda i, f, bk, fi, la, sq, r0, cl, row=row: (cl[i * ATT_ROWS + row], 0, 0, 0)))
    out_specs = [cur(0, 0), cur(0, 0), cur(0, 1), cur(0, 0)]
    grid_spec = pltpu.PrefetchScalarGridSpec(
        num_scalar_prefetch=7, grid=(n_items,), in_specs=in_specs, out_specs=out_specs,
        scratch_shapes=[pltpu.VMEM((2, ch, width), F32), pltpu.VMEM((2, ch, width), F32),
                        pltpu.VMEM((SUBLANES, width), F32), pltpu.VMEM((2 * n_batch, width), F32)])
    return pl.pallas_call(
        functools.partial(_mixer_kernel, ch=ch, n_ctx_items=n_batch, band=band),
        grid_spec=grid_spec,
        out_shape=[jax.ShapeDtypeStruct((m, width), BF16),
                   jax.ShapeDtypeStruct((m, width), F32),
                   jax.ShapeDtypeStruct((m, width), F32),
                   jax.ShapeDtypeStruct((m, width), BF16)],
        compiler_params=_params(1),
        name="token_mixers",
    )(*[jnp.asarray(t) for t in tables],
      sb, rest, rest, rest, rest, rest, rest,
      sc_w, lru_cw, lru_cb.reshape(1, width), sp, wr_bd, wi_bd, b_r, b_i,
      q, k, v, k, v, bias, bias, bias, bias)


def _attention_item(it, r0t, cls, q_ref, k_ref, v_ref, kc_ref, vc_ref, bias_refs, o_ref, band):
    nq = GRID_W
    lane = lax.broadcasted_iota(jnp.int32, (nq, LANES), 1)
    low = lane < NA_HEAD_DIM
    scale = NA_HEAD_DIM ** -0.5
    nt = (((1,), (1,)), ((), ()))
    starts = [pl.multiple_of(r0t[it * ATT_ROWS + row] * GRID_W, GRID_W) for row in range(ATT_ROWS)]
    for hp in range(NA_HEADS // 2):
        cols = slice(hp * LANES, (hp + 1) * LANES)
        stacked = []
        for row in range(ATT_ROWS):
            qp = q_ref[row * nq:(row + 1) * nq, cols].astype(F32) * scale
            stacked += [jnp.where(low, qp, 0.0), jnp.where(low, 0.0, qp)]
        qs_all = jnp.concatenate(stacked, axis=0).astype(BF16)
        s_ctx_all = lax.dot_general(qs_all, kc_ref[:, cols], nt, preferred_element_type=F32)
        e_ctx_rows, e_loc_rows, dens = [], [], []
        for row in range(ATT_ROWS):
            part = slice(row * 2 * nq, (row + 1) * 2 * nq)
            kb = k_ref[pl.ds(starts[row], band), cols]
            s_loc = lax.dot_general(qs_all[part], kb, nt, preferred_element_type=F32)
            bias_ref = bias_refs[row]
            s_loc = s_loc + jnp.concatenate([bias_ref[0, 2 * hp], bias_ref[0, 2 * hp + 1]], axis=0)
            s_ctx = s_ctx_all[part]
            mx = jnp.maximum(jnp.max(s_loc, axis=-1, keepdims=True), jnp.max(s_ctx, axis=-1, keepdims=True))
            e_loc = jnp.exp(s_loc - mx)
            e_ctx = jnp.exp(s_ctx - mx)
            dens.append(jnp.sum(e_loc, axis=-1, keepdims=True) + jnp.sum(e_ctx, axis=-1, keepdims=True))
            e_loc_rows.append(e_loc.astype(BF16))
            e_ctx_rows.append(e_ctx.astype(BF16))
        o_ctx_all = jnp.dot(jnp.concatenate(e_ctx_rows, axis=0), vc_ref[:, cols], preferred_element_type=F32)
        for row in range(ATT_ROWS):
            part = slice(row * 2 * nq, (row + 1) * 2 * nq)
            vb = v_ref[pl.ds(starts[row], band), cols]
            o = (jnp.dot(e_loc_rows[row], vb, preferred_element_type=F32) + o_ctx_all[part]) / dens[row]
            o_ref[row * nq:(row + 1) * nq, cols] = jnp.where(low, o[:nq], o[nq:]).astype(BF16)


def _attention_bias(rpb, seq):
    rows = seq // GRID_W
    kr = min(NA_WIN_ROWS, rows)
    kc = NA_WIN_COLS
    cq = np.arange(GRID_W)
    c0 = np.clip(cq - kc // 2, 0, GRID_W - kc)
    ck = np.arange(GRID_W)
    inside = (ck[None, :] >= c0[:, None]) & (ck[None, :] < c0[:, None] + kc)
    dc = np.clip(ck[None, :] - cq[:, None] + (NA_WIN_COLS - 1), 0, 2 * NA_WIN_COLS - 2)
    n_dr = 2 * NA_WIN_ROWS - 1
    n_dc = 2 * NA_WIN_COLS - 1
    pick = jnp.asarray((np.arange(n_dc)[:, None] == dc.reshape(1, -1)).astype(np.float32))
    picked = jnp.dot(rpb.reshape(-1, n_dc), pick, precision=lax.Precision.HIGHEST)
    picked = picked.reshape(NA_HEADS, 2 * NA_WIN_ROWS - 1, GRID_W, GRID_W)
    table = jnp.where(jnp.asarray(inside)[None, None], picked, NEG_BIG)
    table = table.transpose(0, 2, 1, 3).reshape(NA_HEADS, GRID_W, n_dr * GRID_W)
    classes = []
    for cl in range(kr):
        lo = (NA_WIN_ROWS - 1 - cl) * GRID_W
        classes.append(table[:, :, lo:lo + kr * GRID_W])
    classes.append(jnp.full((NA_HEADS, GRID_W, kr * GRID_W), NEG_BIG, F32))
    return jnp.stack(classes, axis=0)


def _post_kernel(ha_ref, hc_ref, att_ref, zb_ref, hf_ref, hb_ref, lg_ref,
                 sh1_ref, sc1_ref, g1_ref, sh2_ref, sc2_ref,
                 wgl0_ref, wgl1_ref, wgl2_ref, bgl0_ref, bgl1_ref, bgl2_ref,
                 wpa_ref, wpc_ref, wpl_ref, wo_ref, bo_ref, l1g_ref, l1b_ref,
                 rw_ref, rb_ref, tri_ref, upper_ref,
                 h1_ref, xs_ref, pos_ref, cnt_ref, u_s, logit_s, *, split, n_tiles):
    i = pl.program_id(0)

    @pl.when(i == 0)
    def _():
        u_s[...] = jnp.zeros_like(u_s)
        logit_s[...] = jnp.zeros_like(logit_s)

    prev_u = u_s[...]
    prev_pos, prev_w = _route(logit_s[...], tri_ref, upper_ref, pos_ref, cnt_ref)

    h = jnp.where(jnp.minimum(i, n_tiles - 1) < split, ha_ref[...], hc_ref[...])
    u1 = (_layer_norm(h) * (1.0 + sc1_ref[0]) + sh1_ref[0]).astype(BF16)
    y_a = jnp.dot(att_ref[...], wpa_ref[...], preferred_element_type=F32)
    y_b = jnp.dot(zb_ref[...], wpc_ref[...], preferred_element_type=F32)
    zc = jax.nn.gelu(lg_ref[...]) * (hf_ref[...] + hb_ref[...])
    y_c = jnp.dot(zc.astype(BF16), wpl_ref[...], preferred_element_type=F32)
    merged = (jax.nn.sigmoid(jnp.dot(u1, wgl0_ref[0], preferred_element_type=F32) + bgl0_ref[0]) * y_a
              + jax.nn.sigmoid(jnp.dot(u1, wgl1_ref[0], preferred_element_type=F32) + bgl1_ref[0]) * y_b
              + jax.nn.sigmoid(jnp.dot(u1, wgl2_ref[0], preferred_element_type=F32) + bgl2_ref[0]) * y_c)
    y = jnp.dot(merged.astype(BF16), wo_ref[...], preferred_element_type=F32) + bo_ref[...]
    h1 = _layer_norm(DEEPNORM_ALPHA * h + g1_ref[0] * y) * l1g_ref[...] + l1b_ref[...]
    h1_ref[...] = h1
    u2 = _layer_norm(h1) * (1.0 + sc2_ref[0]) + sh2_ref[0]
    _sort_rows(prev_u, prev_pos, prev_w, xs_ref)

    u_hi = u2.astype(BF16)
    u_lo = (u2 - u_hi.astype(F32)).astype(BF16)
    by_hi = jnp.dot(u_hi, rw_ref[...], preferred_element_type=F32)
    logit_s[...] = (by_hi[:, :LANES] + by_hi[:, LANES:]
                    + jnp.dot(u_lo, rw_ref[:, :LANES], preferred_element_type=F32) + rb_ref[...])
    u_s[...] = u_hi


def _route(logits, tri_ref, upper_ref, pos_ref, cnt_ref):
    tm = logits.shape[0]
    lane = lax.broadcasted_iota(jnp.int32, (tm, LANES), 1)
    lane_f = lane.astype(F32)
    work = logits
    tops, idxs, hots = [], [], []
    for _ in range(TOP_K):
        mx = jnp.max(work, axis=-1, keepdims=True)
        idx = jnp.min(jnp.where(work == mx, lane_f, float(LANES)), axis=-1, keepdims=True)
        hot = lane_f == idx
        work = jnp.where(hot, -3e38, work)
        tops.append(mx); idxs.append(idx); hots.append(hot)
    exps = [jnp.exp(t - tops[0]) for t in tops]
    den = exps[0] + exps[1] + exps[2] + exps[3]
    hot_all = jnp.zeros((tm, LANES), F32)
    for hot in hots:
        hot_all = hot_all + hot.astype(F32)
    cnt = jnp.sum(hot_all, axis=0, keepdims=True)
    cnt_pad = jnp.floor((cnt + (SUBLANES - 1.0)) * (1.0 / SUBLANES)) * SUBLANES
    off = jnp.dot(jnp.broadcast_to(cnt_pad, (SUBLANES, LANES)), upper_ref[...],
                  precision=lax.Precision.HIGHEST, preferred_element_type=F32)[0:1]
    slot = off + jnp.dot(tri_ref[...], hot_all.astype(BF16), preferred_element_type=F32)
    pos4 = jnp.zeros((tm, LANES), F32)
    w_tile = jnp.zeros((tm, LANES), F32)
    for kk in range(TOP_K):
        pos_k = jnp.sum(jnp.where(hots[kk], slot, 0.0), axis=-1, keepdims=True)
        pos4 = jnp.where(lane == kk, pos_k, pos4)
        p = exps[kk] / den
        p_hi = p.astype(BF16).astype(F32)
        p_mid = (p - p_hi).astype(BF16).astype(F32)
        p_lo = p - p_hi - p_mid
        w_tile = jnp.where(hots[kk], p_hi, w_tile)
        w_tile = jnp.where(lane_f == idxs[kk] + float(N_EXPERTS), p_mid, w_tile)
        w_tile = jnp.where(lane_f == idxs[kk] + float(2 * N_EXPERTS), p_lo, w_tile)
    pos_ref[...] = pos4
    sub = lax.broadcasted_iota(jnp.int32, (SUBLANES, LANES), 0)
    cnt_ref[...] = jnp.where(sub == 0, cnt, jnp.where(sub == 1, off, 0.0))
    return pos4, w_tile


def _sort_rows(u_hi, pos4, w_tile, xs_ref):
    tm = pos4.shape[0]
    n_sorted = xs_ref.shape[0]
    pos_t = pos4.T
    r_iota = lax.broadcasted_iota(jnp.int32, (n_sorted, tm), 0).astype(F32)
    hit = r_iota == pos_t[0:1, :]
    for kk in range(1, TOP_K):
        hit = jnp.logical_or(hit, r_iota == pos_t[kk:kk + 1, :])
    perm = jnp.where(hit, 1.0, 0.0).astype(BF16)
    feats = jnp.concatenate([u_hi, w_tile.astype(BF16)], axis=1)
    xs_ref[...] = jnp.dot(perm, feats, preferred_element_type=F32)


def _post_mixer(h_parts, att, zb, hf, hb, rest, mod3, mod_base, layer, w_in_bf, b_in, wpa, wpc, wpl, wo, b_o, l1g,
                l1b, rw_pad, rb_pad, n_rows, n_lat, seq):
    tm = TM_POST
    width = BRANCH_WIDTH
    ha, hc = h_parts[0], h_parts[-1]
    split = min(ha.shape[0], n_rows) // tm
    gate_col0 = N_EARLY // D_MODEL
    tiles_per_seq = seq // tm
    n_lat_tiles = n_lat // tm
    n_groups_lat = n_lat // seq
    tri = jnp.asarray(np.tril(np.ones((tm, tm), np.float32), -1), BF16)
    upper = jnp.asarray(np.triu(np.ones((LANES, LANES), np.float32), 1))
    n_tiles = n_rows // tm

    def dense(i):
        return jnp.minimum(i, n_tiles - 1)

    def routed(i):
        return jnp.maximum(i - 1, 0)

    def group(i):
        return jnp.where(dense(i) < n_lat_tiles, dense(i) // tiles_per_seq, n_groups_lat)

    def rows(wd, col=0):
        return pl.BlockSpec((tm, wd), lambda i: (dense(i), col))

    def full(shape):
        return pl.BlockSpec(shape, lambda i: (0,) * len(shape))

    def mod(which):
        return pl.BlockSpec((1, 1, D_MODEL), lambda i: (mod_base + group(i) * 6 + which, 0, 0))

    in_specs = [pl.BlockSpec((tm, D_MODEL), lambda i: (jnp.minimum(dense(i), split - 1), 0)),
                pl.BlockSpec((tm, D_MODEL), lambda i: (jnp.maximum(dense(i) - split, 0), 0)),
                rows(width), rows(width), rows(width), rows(width), rows(width, 3),
                mod(0), mod(1), mod(2), mod(3), mod(4)]
    in_specs += [pl.BlockSpec((1, D_MODEL, D_MODEL), lambda i, c=c: (layer, 0, gate_col0 + c)) for c in range(3)]
    in_specs += [pl.BlockSpec((1, 1, D_MODEL), lambda i, c=c: (layer, 0, gate_col0 + c)) for c in range(3)]
    in_specs += [full(wpa.shape), full(wpc.shape), full(wpl.shape), full(wo.shape), full((1, D_MODEL)),
                full((1, D_MODEL)), full((1, D_MODEL)), full(rw_pad.shape), full(rb_pad.shape), full(tri.shape),
                full(upper.shape)]
    out_specs = [rows(D_MODEL), pl.BlockSpec((TOK_BLOCK, XS_WIDTH), lambda i: (routed(i), 0)),
                 pl.BlockSpec((tm, LANES), lambda i: (routed(i), 0)),
                 pl.BlockSpec((SUBLANES, LANES), lambda i: (routed(i), 0))]
    return pl.pallas_call(
        functools.partial(_post_kernel, split=split, n_tiles=n_tiles),
        grid=(n_tiles + 1,),
        in_specs=in_specs, out_specs=out_specs,
        out_shape=[jax.ShapeDtypeStruct((n_rows, D_MODEL), F32),
                   jax.ShapeDtypeStruct((n_tiles * TOK_BLOCK, XS_WIDTH), F32),
                   jax.ShapeDtypeStruct((n_rows, LANES), F32),
                   jax.ShapeDtypeStruct((n_tiles * SUBLANES, LANES), F32)],
        scratch_shapes=[pltpu.VMEM((tm, D_MODEL), BF16), pltpu.VMEM((tm, LANES), F32)],
        compiler_params=_params(1),
        name="post_mixer",
    )(ha, hc, att, zb, hf, hb, rest, mod3, mod3, mod3, mod3, mod3,
      w_in_bf, w_in_bf, w_in_bf, b_in.reshape(DEPTH, 1, -1), b_in.reshape(DEPTH, 1, -1), b_in.reshape(DEPTH, 1, -1),
      wpa, wpc, wpl, wo, b_o.reshape(1, D_MODEL), l1g.reshape(1, D_MODEL), l1b.reshape(1, D_MODEL),
      rw_pad, rb_pad, tri, upper)


def _expert_plan(cnt_out, n_tok_tiles, n_rows):
    tm = TM_EXPERT
    co = cnt_out.reshape(n_tok_tiles, SUBLANES, LANES)
    cnt = (co[:, 0, :N_EXPERTS].astype(jnp.int32) + SUBLANES - 1) // SUBLANES * SUBLANES
    off = co[:, 1, :N_EXPERTS].astype(jnp.int32)
    cum_end = jnp.cumsum(cnt, axis=0)
    cum = cum_end - cnt
    total = cum_end[-1]
    n_et = (total + tm - 1) // tm
    et_end = jnp.cumsum(n_et)
    n_act = et_end[-1:].astype(jnp.int32)
    n_tiles = -(-n_tok_tiles * TOK_BLOCK // tm) + N_EXPERTS
    j = jnp.arange(n_tiles, dtype=jnp.int32)
    tile_e = jnp.minimum(jnp.sum((et_end[None, :] <= j[:, None]).astype(jnp.int32), axis=1), N_EXPERTS - 1)
    pick_e = (tile_e[:, None] == jnp.arange(N_EXPERTS, dtype=jnp.int32)[None, :]).astype(F32)

    def per_tile(table):
        return jnp.dot(pick_e, table.astype(F32), precision=lax.Precision.HIGHEST).astype(jnp.int32)

    row0 = (j - per_tile(et_end - n_et)) * tm
    n_rows_tile = jnp.clip(per_tile(total) - row0, 0, tm)
    cum_e = per_tile(cum.T)
    cum_end_e = per_tile(cum_end.T)
    delta_e = per_tile((jnp.arange(n_tok_tiles, dtype=jnp.int32)[:, None] * TOK_BLOCK + off - cum).T)
    q = row0[:, None] + SUBLANES * jnp.arange(tm // SUBLANES, dtype=jnp.int32)[None, :]
    inside = jnp.logical_and(cum_e.T[:, :, None] <= q[None], q[None] < cum_end_e.T[:, :, None])
    src = q + jnp.sum(jnp.where(inside, delta_e.T[:, :, None], 0), axis=0)
    used = off[:, N_EXPERTS - 1] + cnt[:, N_EXPERTS - 1]
    first = jnp.concatenate([jnp.ones((1,), jnp.int32), (tile_e[1:] != tile_e[:-1]).astype(jnp.int32)])
    group = jnp.cumsum(first) - 1
    after = per_tile(et_end)
    next_e = jnp.where(after < n_act[0],
                       jnp.minimum(jnp.sum((et_end[None, :] <= after[:, None]).astype(jnp.int32), axis=1),
                                   N_EXPERTS - 1), -1)
    return (tile_e.astype(jnp.int32), n_rows_tile.astype(jnp.int32), n_act, src.reshape(-1).astype(jnp.int32),
            used.astype(jnp.int32), first, group.astype(jnp.int32), next_e.astype(jnp.int32), n_tiles)


def _expert_kernel(tile_e, n_rows_t, n_act, src_t, used_t, first_t, group_t, next_t,
                   xs_hbm, wgu_hbm, bgu_ref, wdn_hbm, bdn_ref, ys_hbm,
                   xin, yout, wgu_f, wdn_f, wgu_s, wdn_s, zeros, sem_in, sem_out, sem_zero, sem_wgu, sem_wdn,
                   *, tm, tok_block, n_tok_tiles, layer):
    j = pl.program_id(0)
    na = n_act[0]
    chunks = tm // SUBLANES

    def weight_copies(e, wslot):
        return (pltpu.make_async_copy(wgu_hbm.at[layer, e], wgu_f.at[wslot], sem_wgu.at[wslot]),
                pltpu.make_async_copy(wdn_hbm.at[layer, e], wdn_f.at[wslot], sem_wdn.at[wslot]))

    def gather(src, dst, size, slot):
        return pltpu.make_async_copy(xs_hbm.at[pl.ds(src, size)], xin.at[slot, pl.ds(dst, size)], sem_in.at[slot])

    def scatter(src, dst, size, slot):
        return pltpu.make_async_copy(yout.at[slot, pl.ds(dst, size)], ys_hbm.at[pl.ds(src, size)], sem_out.at[slot])

    def start_chunks(jj, slot, copy):
        def one(c, priority):
            src = pl.multiple_of(src_t[jj * chunks + c], SUBLANES)
            copy(src, pl.multiple_of(c * SUBLANES, SUBLANES), SUBLANES, slot).start(priority=priority)

        def body(c, carry):
            one(c, 0)
            return carry

        def body_unrolled(g, carry):
            for u in range(SUBLANES):
                one(g * SUBLANES + u, u % 2)
            return carry

        n = lax.shift_right_logical(n_rows_t[jj], 3)

        @pl.when(n == chunks)
        def _():
            lax.fori_loop(0, chunks // SUBLANES, body_unrolled, 0)

        @pl.when(n != chunks)
        def _():
            lax.fori_loop(0, n, body, 0)

    def wait_rows(jj, slot, copy):
        n = n_rows_t[jj]
        size = tm
        while size >= SUBLANES:
            @pl.when((n & size) != 0)
            def _(size=size):
                copy(0, 0, size, slot).wait()
            size //= 2

    slot = j % 2

    @pl.when(j == 0)
    def _():
        xin[...] = jnp.zeros_like(xin)
        start_chunks(0, 0, gather)
        zeros[...] = jnp.zeros_like(zeros)

        def clear_tail(i, copy_op):
            used = used_t[i]
            tail = tok_block - used
            size = ZERO_ROWS
            sizes = []
            while size >= SUBLANES:
                sizes.append(size)
                size //= 2
            for size in sizes:
                @pl.when((tail & size) != 0)
                def _(size=size):
                    at = pl.multiple_of(i * tok_block + used + (tail & ~(2 * size - 1)), SUBLANES)
                    copy_op(pltpu.make_async_copy(zeros.at[pl.ds(0, size)], ys_hbm.at[pl.ds(at, size)], sem_zero))

        def start_clear(i, c):
            clear_tail(i, lambda cp: cp.start())
            return c

        def wait_clear(i, c):
            clear_tail(i, lambda cp: cp.wait())
            return c

        lax.fori_loop(0, n_tok_tiles, start_clear, 0)
        lax.fori_loop(0, n_tok_tiles, wait_clear, 0)

    @pl.when(j + 1 < na)
    def _():
        start_chunks(j + 1, 1 - slot, gather)

    @pl.when(j < na)
    def _():
        e = tile_e[j]

        @pl.when(first_t[j] == 1)
        def _():
            wslot = group_t[j] % 2

            @pl.when(j == 0)
            def _():
                for cp in weight_copies(e, wslot):
                    cp.start()

            for cp in weight_copies(e, wslot):
                cp.wait()
            wgu_s[...] = wgu_f[wslot].astype(BF16)
            wdn_s[...] = wdn_f[wslot].astype(BF16)

            @pl.when(next_t[j] >= 0)
            def _():
                for cp in weight_copies(next_t[j], 1 - wslot):
                    cp.start()

        wait_rows(j, slot, gather)

        def compute(rows):
            x = xin[slot, pl.ds(0, rows)]
            lane = lax.broadcasted_iota(jnp.int32, (rows, LANES), 1)
            p = jnp.sum(jnp.where(lane % N_EXPERTS == e, x[:, D_MODEL:], 0.0), axis=-1, keepdims=True)
            gu = (jnp.dot(x[:, :D_MODEL].astype(BF16), wgu_s[...], preferred_element_type=F32)
                  + bgu_ref[0, pl.ds(e, 1), :])
            f = gu.shape[1] // 2
            gate = jnp.minimum(gu[:, :f], SWIGLU_LIMIT)
            up = jnp.clip(gu[:, f:], -SWIGLU_LIMIT, SWIGLU_LIMIT)
            hid = (up + 1.0) * gate * jax.nn.sigmoid(SWIGLU_ALPHA * gate)
            y = (jnp.dot(hid.astype(BF16), wdn_s[...], preferred_element_type=F32)
                 + bdn_ref[0, pl.ds(e, 1), :])
            yout[slot, pl.ds(0, rows)] = y * p

        quarter = tm // 4
        for part in range(1, 5):
            @pl.when(jnp.logical_and(n_rows_t[j] > (part - 1) * quarter, n_rows_t[j] <= part * quarter))
            def _(part=part):
                compute(part * quarter)

        start_chunks(j, slot, scatter)

        @pl.when(j >= 1)
        def _():
            wait_rows(j - 1, 1 - slot, scatter)

        @pl.when(j == na - 1)
        def _():
            wait_rows(j, slot, scatter)


def _experts(xs, plan, layer, w_gu, b_gu, w_dn, b_dn):
    tile_e, n_rows_tile, n_act, src, used, first, group, next_e, n_tiles = plan
    tm = TM_EXPERT
    f2 = w_gu.shape[-1]

    in_specs = [pl.BlockSpec(memory_space=pl.ANY),
                pl.BlockSpec(memory_space=pl.ANY),
                pl.BlockSpec((1, N_EXPERTS, f2), lambda j, *_: (layer, 0, 0)),
                pl.BlockSpec(memory_space=pl.ANY),
                pl.BlockSpec((1, N_EXPERTS, D_MODEL), lambda j, *_: (layer, 0, 0))]
    grid_spec = pltpu.PrefetchScalarGridSpec(
        num_scalar_prefetch=8, grid=(n_tiles,), in_specs=in_specs,
        out_specs=pl.BlockSpec(memory_space=pl.ANY),
        scratch_shapes=[pltpu.VMEM((2, tm, XS_WIDTH), F32), pltpu.VMEM((2, tm, D_MODEL), F32),
                        pltpu.VMEM((2, D_MODEL, f2), F32), pltpu.VMEM((2, f2 // 2, D_MODEL), F32),
                        pltpu.VMEM((D_MODEL, f2), BF16), pltpu.VMEM((f2 // 2, D_MODEL), BF16),
                        pltpu.VMEM((ZERO_ROWS, D_MODEL), F32),
                        pltpu.SemaphoreType.DMA((2,)), pltpu.SemaphoreType.DMA((2,)), pltpu.SemaphoreType.DMA(()),
                        pltpu.SemaphoreType.DMA((2,)), pltpu.SemaphoreType.DMA((2,))])
    return pl.pallas_call(
        functools.partial(_expert_kernel, tm=tm, tok_block=TOK_BLOCK, n_tok_tiles=xs.shape[0] // TOK_BLOCK,
                          layer=layer),
        grid_spec=grid_spec,
        out_shape=jax.ShapeDtypeStruct((xs.shape[0], D_MODEL), F32),
        compiler_params=_params(1),
        name="moe_experts",
    )(tile_e, n_rows_tile, n_act, src, used, first, group, next_e,
      xs, w_gu, b_gu, w_dn, b_dn)


def _combine_kernel(ys_ref, pos_ref, h1_ref, g2_ref, l2g_ref, l2b_ref, o_ref):
    tm = pos_ref.shape[0]
    n_sorted = ys_ref.shape[0]
    pos = pos_ref[...]
    col = lax.broadcasted_iota(jnp.int32, (tm, n_sorted), 1).astype(F32)
    sel = jnp.zeros((tm, n_sorted), F32)
    for kk in range(TOP_K):
        sel = sel + (col == pos[:, kk:kk + 1]).astype(F32)
    sel = sel.astype(BF16)
    ys = ys_ref[...]
    hi = ys.astype(BF16)
    mid = (ys - hi.astype(F32)).astype(BF16)
    y2 = jnp.dot(sel, hi, preferred_element_type=F32) + jnp.dot(sel, mid, preferred_element_type=F32)
    o_ref[...] = _layer_norm(DEEPNORM_ALPHA * h1_ref[...] + g2_ref[0] * y2) * l2g_ref[...] + l2b_ref[...]


def _combine(ys, pos4, h1, mod3, mod_base, l2g, l2b, n_lat, seq):
    n_rows = h1.shape[0]
    tm = TM_POST
    tiles_per_seq = seq // tm
    n_lat_tiles = n_lat // tm
    n_groups_lat = n_lat // seq

    def group(i):
        return jnp.where(i < n_lat_tiles, i // tiles_per_seq, n_groups_lat)

    in_specs = [pl.BlockSpec((TOK_BLOCK, D_MODEL), lambda i: (i, 0)),
                pl.BlockSpec((tm, LANES), lambda i: (i, 0)),
                pl.BlockSpec((tm, D_MODEL), lambda i: (i, 0)),
                pl.BlockSpec((1, 1, D_MODEL), lambda i: (mod_base + group(i) * 6 + 5, 0, 0)),
                pl.BlockSpec((1, D_MODEL), lambda i: (0, 0)),
                pl.BlockSpec((1, D_MODEL), lambda i: (0, 0))]
    return pl.pallas_call(
        _combine_kernel,
        grid=(n_rows // tm,),
        in_specs=in_specs,
        out_specs=pl.BlockSpec((tm, D_MODEL), lambda i: (i, 0)),
        out_shape=jax.ShapeDtypeStruct((n_rows, D_MODEL), F32),
        compiler_params=_params(1),
        name="moe_combine",
    )(ys, pos4, h1, mod3, l2g.reshape(1, D_MODEL), l2b.reshape(1, D_MODEL))


def _block_diag(w):
    two, n, d, e = w.shape
    eye = jnp.eye(n, dtype=w.dtype)
    return (w[:, :, :, None, :] * eye[None, :, None, :, None]).reshape(two, n * d, n * e)


def kernel(x, c, ctx, c_ctx, w_mod, b_mod, w_in, b_in, na_rpb, w_proj_attn, w_proj_conv, w_proj_lru, sc_conv_w, lru_conv_w, lru_conv_b, lru_lambda, lru_w_r, lru_b_r, lru_w_i, lru_b_i, w_o, b_o, ln1_g, ln1_b, router_w, router_b, exp_w_gu, exp_b_gu, exp_w_dn, exp_b_dn, ln2_g, ln2_b):
    n_batch, seq, d = x.shape
    n_ctx = ctx.shape[1]
    n_lat = n_batch * seq
    n_all = n_lat + n_batch * n_ctx
    assert d == D_MODEL and n_batch + 1 <= SUBLANES

    cc = jnp.concatenate([c, c_ctx[None], jnp.zeros((SUBLANES - n_batch - 1, d), F32)], axis=0)
    mod = _modulation(cc, w_mod, b_mod)
    groups = n_batch + 1
    mod3 = mod.reshape(DEPTH, SUBLANES, 6, d)[:, :groups].reshape(DEPTH * groups * 6, 1, d)

    cos_t, sin_t = _make_rope(seq, TM_INPROJ)
    h = (x.reshape(n_lat, d), ctx.reshape(n_batch * n_ctx, d))
    w_in_bf = w_in.astype(BF16)

    for layer in range(DEPTH):
        last = layer == DEPTH - 1
        mod_base = layer * groups * 6
        q, k, v, sb, rest = _input_projection(h, mod3, mod_base, layer, w_in_bf, b_in, N_EARLY,
                                              cos_t, sin_t, n_lat, seq)
        sp = jax.nn.softplus(-lru_lambda[layer])
        zb, hf, hb, att = _token_mixers(
            q, k, v, _attention_bias(na_rpb[layer], seq), sb, rest, sc_conv_w[layer], lru_conv_w[layer],
            lru_conv_b[layer], sp, _block_diag(lru_w_r[layer]).astype(BF16), _block_diag(lru_w_i[layer]).astype(BF16),
            lru_b_r[layer], lru_b_i[layer], n_batch, seq, n_ctx)
        n_rows = n_lat if last else n_all
        rw_full = jnp.pad(router_w[layer], ((0, 0), (0, LANES - N_EXPERTS)))
        rw_hi = rw_full.astype(BF16)
        rw_pad = jnp.concatenate([rw_hi, (rw_full - rw_hi.astype(F32)).astype(BF16)], axis=1)
        rb_pad = jnp.concatenate([router_b[layer], jnp.full((LANES - N_EXPERTS,), NEG_BIG, F32)]).reshape(1, LANES)
        h1, xs, pos4, cnt_out = _post_mixer(
            h, att, zb, hf, hb, rest, mod3, mod_base, layer, w_in_bf, b_in,
            w_proj_attn[layer].astype(BF16), w_proj_conv[layer].astype(BF16), w_proj_lru[layer].astype(BF16),
            w_o[layer].astype(BF16), b_o[layer], ln1_g[layer], ln1_b[layer], rw_pad, rb_pad, n_rows, n_lat, seq)
        plan = _expert_plan(cnt_out, n_rows // TM_POST, n_rows)
        ys = _experts(xs, plan, layer, exp_w_gu, exp_b_gu, exp_w_dn, exp_b_dn)
        h = (_combine(ys, pos4, h1, mod3, mod_base, ln2_g[layer], ln2_b[layer], n_lat, seq),)
    return h[0].reshape(n_batch, seq, d)
```

```python
import functools

import numpy as np
import jax
import jax.numpy as jnp
from jax import lax
from jax.experimental import pallas as pl
from jax.experimental.pallas import tpu as pltpu

D_MODEL = 1024
DEPTH = 2
GRID_W = 64
NA_HEADS = 8
NA_HEAD_DIM = 64
NA_WIN_ROWS = 8
NA_WIN_COLS = 16
ROPE_BASE = 10000.0
BRANCH_WIDTH = 512
LRU_C = 8.0
N_EARLY = 8 * BRANCH_WIDTH
N_EXPERTS = 32
TOP_K = 4
SWIGLU_LIMIT = 7.0
SWIGLU_ALPHA = 1.702
LN_EPS = 1e-5
DEEPNORM_ALPHA = (2 * DEPTH) ** 0.25
NEG_BIG = -1e30

LANES = 128
SUBLANES = 8
VMEM_LIMIT_BYTES = 56 * 1024 * 1024

TM_INPROJ = 512
SCAN_CHUNK = 256
ATT_ROWS = 4
ATT_QROWS = ATT_ROWS * GRID_W
TM_POST = 256
TM_EXPERT = 512
XS_WIDTH = D_MODEL + LANES
TOK_BLOCK = TM_POST * TOP_K + N_EXPERTS * SUBLANES
ZERO_ROWS = TOK_BLOCK - TM_POST * TOP_K

F32 = jnp.float32
BF16 = jnp.bfloat16


def _params(n_axes):
    return pltpu.CompilerParams(dimension_semantics=("arbitrary",) * n_axes,
                                vmem_limit_bytes=VMEM_LIMIT_BYTES)


def _layer_norm(x):
    mu = jnp.mean(x, axis=-1, keepdims=True)
    xc = x - mu
    var = jnp.mean(xc * xc, axis=-1, keepdims=True)
    return xc * lax.rsqrt(var + LN_EPS)


def _mod_kernel(c_ref, w_ref, b_ref, o_ref):
    c = c_ref[...]
    s = (c * jax.nn.sigmoid(c)).astype(BF16)
    o_ref[0] = jnp.dot(s, w_ref[0].astype(BF16), preferred_element_type=F32) + b_ref[0]


def _modulation(cc, w_mod, b_mod):
    n_out = w_mod.shape[-1]
    return pl.pallas_call(
        _mod_kernel,
        grid=(DEPTH, n_out // D_MODEL),
        in_specs=[pl.BlockSpec((SUBLANES, D_MODEL), lambda l, j: (0, 0)),
                  pl.BlockSpec((1, D_MODEL, D_MODEL), lambda l, j: (l, 0, j)),
                  pl.BlockSpec((1, 1, D_MODEL), lambda l, j: (l, 0, j))],
        out_specs=pl.BlockSpec((1, SUBLANES, D_MODEL), lambda l, j: (l, 0, j)),
        out_shape=jax.ShapeDtypeStruct((DEPTH, SUBLANES, n_out), F32),
        compiler_params=_params(2),
        name="modulation",
    )(cc, w_mod, b_mod.reshape(DEPTH, 1, n_out))


def _rope_half(x, cos, sin_signed):
    m = NA_HEAD_DIM // 4
    lane = lax.broadcasted_iota(jnp.int32, (x.shape[0], LANES), 1)
    first = (lane % (2 * m)) < m
    outs = []
    for cidx in range(x.shape[1] // LANES):
        xc = x[:, cidx * LANES:(cidx + 1) * LANES]
        partner = jnp.where(first, pltpu.roll(xc, LANES - m, 1), pltpu.roll(xc, m, 1))
        outs.append(xc * cos + partner * sin_signed)
    return jnp.concatenate(outs, axis=1)


def _inproj_kernel(ha_ref, hb_ref, sh_ref, sc_ref, w_ref, b_ref, cos_ref, sin_ref,
                   q_ref, k_ref, v_ref, sb_ref, rest_ref, *, split):
    h = jnp.where(pl.program_id(0) < split, ha_ref[...], hb_ref[...])
    xn = (_layer_norm(h) * (1.0 + sc_ref[0]) + sh_ref[0]).astype(BF16)
    half = BRANCH_WIDTH

    def columns(lo, width):
        return jnp.dot(xn, w_ref[0, :, lo:lo + width], preferred_element_type=F32) + b_ref[0, :, lo:lo + width]

    cos = cos_ref[...]
    sin = sin_ref[...]
    q_ref[...] = _rope_half(columns(0, half), cos, sin).astype(BF16)
    k_ref[...] = _rope_half(columns(half, half), cos, sin).astype(BF16)
    v_ref[...] = columns(2 * half, half).astype(BF16)
    sb_ref[...] = columns(3 * half, half)
    n_rest = rest_ref.shape[1]
    for lo in range(0, n_rest, 2 * half):
        rest_ref[:, lo:lo + 2 * half] = columns(4 * half + lo, 2 * half)


def _input_projection(h_parts, mod3, mod_base, layer, w_in_bf, b_in, n_cols, cos_t, sin_t, n_lat, seq):
    tm = TM_INPROJ
    ha, hb = h_parts[0], h_parts[-1]
    split = ha.shape[0] // tm
    m = ha.shape[0] + (hb.shape[0] if len(h_parts) == 2 else 0)
    n_lat_tiles = n_lat // tm
    tiles_per_seq = seq // tm
    n_groups_lat = n_lat // seq

    def group(i):
        return jnp.where(i < n_lat_tiles, i // tiles_per_seq, n_groups_lat)

    def rope_blk(i):
        return jnp.where(i < n_lat_tiles, i % tiles_per_seq, tiles_per_seq)

    half = BRANCH_WIDTH
    return pl.pallas_call(
        functools.partial(_inproj_kernel, split=split),
        grid=(m // tm,),
        in_specs=[pl.BlockSpec((tm, D_MODEL), lambda i: (jnp.minimum(i, split - 1), 0)),
                  pl.BlockSpec((tm, D_MODEL), lambda i: (jnp.maximum(i - split, 0), 0)),
                  pl.BlockSpec((1, 1, D_MODEL), lambda i: (mod_base + group(i) * 6 + 0, 0, 0)),
                  pl.BlockSpec((1, 1, D_MODEL), lambda i: (mod_base + group(i) * 6 + 1, 0, 0)),
                  pl.BlockSpec((1, D_MODEL, n_cols), lambda i: (layer, 0, 0)),
                  pl.BlockSpec((1, 1, n_cols), lambda i: (layer, 0, 0)),
                  pl.BlockSpec((tm, LANES), lambda i: (rope_blk(i), 0)),
                  pl.BlockSpec((tm, LANES), lambda i: (rope_blk(i), 0))],
        out_specs=[pl.BlockSpec((tm, half), lambda i: (i, 0)),
                   pl.BlockSpec((tm, half), lambda i: (i, 0)),
                   pl.BlockSpec((tm, half), lambda i: (i, 0)),
                   pl.BlockSpec((tm, half), lambda i: (i, 0)),
                   pl.BlockSpec((tm, n_cols - 4 * half), lambda i: (i, 0))],
        out_shape=[jax.ShapeDtypeStruct((m, half), BF16),
                   jax.ShapeDtypeStruct((m, half), BF16),
                   jax.ShapeDtypeStruct((m, half), BF16),
                   jax.ShapeDtypeStruct((m, half), F32),
                   jax.ShapeDtypeStruct((m, n_cols - 4 * half), F32)],
        compiler_params=_params(1),
        name="input_projection",
    )(ha, hb, mod3, mod3, w_in_bf, b_in.reshape(DEPTH, 1, -1), cos_t, sin_t)


def _make_rope(seq, tm):
    rows = seq // GRID_W
    d = np.arange(LANES) % NA_HEAD_DIM
    m = NA_HEAD_DIM // 4
    inv_freq = (ROPE_BASE ** (-jnp.arange(m, dtype=F32) / m))[d % m]
    n_pos = max(rows, GRID_W)
    ang = jnp.arange(n_pos, dtype=F32)[:, None] * inv_freq[None, :]
    by_row = jnp.asarray(d < 2 * m)[None, None, :]

    def expand(tab):
        full = jnp.where(by_row, tab[:rows, None, :], tab[None, :GRID_W, :])
        return full.reshape(seq, LANES)

    cos = expand(jnp.cos(ang))
    sin = expand(jnp.sin(ang))
    sin_signed = jnp.where(jnp.asarray((d % (2 * m)) < m)[None, :], -sin, sin)
    cos = jnp.concatenate([cos, jnp.ones((tm, LANES), F32)], axis=0)
    sin_signed = jnp.concatenate([sin_signed, jnp.zeros((tm, LANES), F32)], axis=0)
    return cos, sin_signed


def _mixer_kernel(fblk, bblk, first, last, seqb, r0t, cls,
                  sb_ref, xf_ref, xfp_ref, xfn_ref, lxb_ref, lxbp_ref, lxbn_ref,
                  scw_ref, cw_ref, cb_ref, sp_ref, wr_ref, wi_ref, br_ref, bi_ref,
                  q_ref, k_ref, v_ref, kc_ref, vc_ref, bias0_ref, bias1_ref, bias2_ref, bias3_ref,
                  zb_ref, hf_ref, hb_ref, att_ref,
                  a_s, b_s, hc_s, st_s, *, ch, n_ctx_items, band):
    it = pl.program_id(0)
    is_first = first[it] == 1
    is_last = last[it] == 1
    is_ctx = it < n_ctx_items
    b = seqb[it]
    width = BRANCH_WIDTH
    row = lax.broadcasted_iota(jnp.int32, (ch, width), 0)
    not_first = jnp.where(is_first, 0.0, 1.0).astype(F32)
    not_last = jnp.where(is_last, 0.0, 1.0).astype(F32)

    def back1(u, prev_row):
        return jnp.where(row == 0, prev_row, pltpu.roll(u, 1, 0))

    def back2(u, prev2, prev1):
        return jnp.where(row == 0, prev2, jnp.where(row == 1, prev1, pltpu.roll(u, 2, 0)))

    def fwd1(u, next_row):
        return jnp.where(row == ch - 1, next_row, pltpu.roll(u, ch - 1, 0))

    gate_cols, value_cols, lru_cols = (slice(s * width, (s + 1) * width) for s in range(3))

    def lru_input(x, p, n, prev_ok, next_ok):
        p = p * prev_ok
        n = n * next_ok
        return (cw_ref[0:1] * back2(x, p[6:7], p[7:8]) + cw_ref[1:2] * back1(x, p[7:8])
                + cw_ref[2:3] * x + cw_ref[3:4] * fwd1(x, n[0:1]) + cb_ref[...])

    def coeffs(d, xm):
        xb = xm.astype(BF16)
        r = jax.nn.sigmoid(jnp.dot(xb, wr_ref[d], preferred_element_type=F32) + br_ref[d:d + 1])
        g = jax.nn.sigmoid(jnp.dot(xb, wi_ref[d], preferred_element_type=F32) + bi_ref[d:d + 1])
        log_a = (-LRU_C * sp_ref[d:d + 1]) * r
        a = jnp.exp(log_a)
        a_s[d] = a
        b_s[d] = jnp.sqrt(-jnp.tanh(log_a) * (a * a + 1.0)) * (g * xm)

    @pl.when(jnp.logical_and(is_first, is_ctx))
    def _():
        hc_s[...] = jnp.zeros_like(hc_s)

    @pl.when(jnp.logical_and(is_first, jnp.logical_not(is_ctx)))
    def _():
        hc_s[0:1] = st_s[pl.ds(2 * b, 1), :]
        hc_s[1:2] = st_s[pl.ds(2 * b + 1, 1), :]

    coeffs(0, lru_input(xf_ref[:, lru_cols], xfp_ref[:, lru_cols], xfn_ref[:, lru_cols], not_first, not_last))
    hf = hc_s[0:1]
    for t in range(ch):
        hf = a_s[0, t:t + 1, :] * hf + b_s[0, t:t + 1, :]
        hf_ref[t:t + 1, :] = hf
    hc_s[0:1] = hf

    coeffs(1, lru_input(lxb_ref[...], lxbp_ref[...], lxbn_ref[...], not_last, not_first))

    u = xf_ref[:, gate_cols] * xf_ref[:, value_cols]
    u_prev = xfp_ref[7:8, gate_cols] * xfp_ref[7:8, value_cols] * not_first
    u_next = xfn_ref[0:1, gate_cols] * xfn_ref[0:1, value_cols] * not_last
    conv = scw_ref[0:1] * back1(u, u_prev) + scw_ref[1:2] * u + scw_ref[2:3] * fwd1(u, u_next)
    zb_ref[...] = (sb_ref[...] * conv).astype(BF16)

    hb = hc_s[1:2]
    for t in range(ch - 1, -1, -1):
        hb = a_s[1, t:t + 1, :] * hb + b_s[1, t:t + 1, :]
        hb_ref[t:t + 1, :] = hb
    hc_s[1:2] = hb

    _attention_item(it, r0t, cls, q_ref, k_ref, v_ref, kc_ref, vc_ref,
                    (bias0_ref, bias1_ref, bias2_ref, bias3_ref), att_ref, band)

    @pl.when(is_ctx)
    def _():
        st_s[pl.ds(2 * b, 1), :] = hf
        st_s[pl.ds(2 * b + 1, 1), :] = hb


def _mixer_tables(n_batch, seq, n_ctx, ch):
    assert n_ctx == ch and ch == ATT_QROWS
    nc = seq // ch
    rows = seq // GRID_W
    kr = min(NA_WIN_ROWS, rows)
    ctx0 = n_batch * seq // ch
    fblk, bblk, first, last, seqb, r0t, cls = [], [], [], [], [], [], []
    for b in range(n_batch):
        fblk.append(ctx0 + b); bblk.append(ctx0 + b); first.append(1); last.append(1); seqb.append(b)
        r0t.extend([0] * ATT_ROWS); cls.extend([kr] * ATT_ROWS)
    for b in range(n_batch):
        for c in range(nc):
            fblk.append(b * nc + c); bblk.append(b * nc + nc - 1 - c)
            first.append(int(c == 0)); last.append(int(c == nc - 1)); seqb.append(b)
            for r in range(c * ATT_ROWS, (c + 1) * ATT_ROWS):
                r0 = min(max(r - kr // 2, 0), rows - kr)
                r0t.append(r0); cls.append(r - r0)
    return [np.asarray(a, np.int32) for a in (fblk, bblk, first, last, seqb, r0t, cls)]


def _token_mixers(q, k, v, bias, sb, rest, sc_w, lru_cw, lru_cb, sp, wr_bd, wi_bd, b_r, b_i, n_batch, seq, n_ctx):
    m = sb.shape[0]
    ch = SCAN_CHUNK
    width = BRANCH_WIDTH
    tables = _mixer_tables(n_batch, seq, n_ctx, ch)
    n_items = len(tables[0])
    halo_per_chunk = ch // SUBLANES
    last_halo = m // SUBLANES - 1
    band = min(NA_WIN_ROWS, seq // GRID_W) * GRID_W
    ctx_blk0 = n_batch * seq // n_ctx

    def cur(col, which, n=1):
        return pl.BlockSpec((ch, n * width), lambda i, f, bk, *_: ((f, bk)[which][i], col))

    def prev(col, which, n=1):
        return pl.BlockSpec((SUBLANES, n * width),
                            lambda i, f, bk, *_: (jnp.maximum((f, bk)[which][i] * halo_per_chunk - 1, 0), col))

    def nxt(col, which, n=1):
        return pl.BlockSpec((SUBLANES, n * width),
                            lambda i, f, bk, *_: (jnp.minimum(((f, bk)[which][i] + 1) * halo_per_chunk, last_halo), col))

    def full(shape):
        return pl.BlockSpec(shape, lambda i, *_: (0,) * len(shape))

    in_specs = [cur(0, 0), cur(0, 0, 3), prev(0, 0, 3), nxt(0, 0, 3),
                cur(2, 1), prev(2, 1), nxt(2, 1),
                full(sc_w.shape), full(lru_cw.shape), full((1, width)), full(sp.shape),
                full(wr_bd.shape), full(wi_bd.shape), full(b_r.shape), full(b_i.shape)]
    in_specs += [cur(0, 0),
                 pl.BlockSpec((seq, width), lambda i, f, bk, fi, la, sq, *_: (sq[i], 0)),
                 pl.BlockSpec((seq, width), lambda i, f, bk, fi, la, sq, *_: (sq[i], 0)),
                 pl.BlockSpec((n_ctx, width), lambda i, f, bk, fi, la, sq, *_: (ctx_blk0 + sq[i], 0)),
                 pl.BlockSpec((n_ctx, width), lambda i, f, bk, fi, la, sq, *_: (ctx_blk0 + sq[i], 0))]
    for row in range(ATT_ROWS):
        in_specs.append(pl.BlockSpec((1, NA_HEADS, GRID_W, band),
                                     lambda i, f, bk, fi, la, sq, r0, cl, row=row: (cl[i * ATT_ROWS + row], 0, 0, 0)))
    out_specs = [cur(0, 0), cur(0, 0), cur(0, 1), cur(0, 0)]
    grid_spec = pltpu.PrefetchScalarGridSpec(
        num_scalar_prefetch=7, grid=(n_items,), in_specs=in_specs, out_specs=out_specs,
        scratch_shapes=[pltpu.VMEM((2, ch, width), F32), pltpu.VMEM((2, ch, width), F32),
                        pltpu.VMEM((SUBLANES, width), F32), pltpu.VMEM((2 * n_batch, width), F32)])
    return pl.pallas_call(
        functools.partial(_mixer_kernel, ch=ch, n_ctx_items=n_batch, band=band),
        grid_spec=grid_spec,
        out_shape=[jax.ShapeDtypeStruct((m, width), BF16),
                   jax.ShapeDtypeStruct((m, width), F32),
                   jax.ShapeDtypeStruct((m, width), F32),
                   jax.ShapeDtypeStruct((m, width), BF16)],
        compiler_params=_params(1),
        name="token_mixers",
    )(*[jnp.asarray(t) for t in tables],
      sb, rest, rest, rest, rest, rest, rest,
      sc_w, lru_cw, lru_cb.reshape(1, width), sp, wr_bd, wi_bd, b_r, b_i,
      q, k, v, k, v, bias, bias, bias, bias)


def _attention_item(it, r0t, cls, q_ref, k_ref, v_ref, kc_ref, vc_ref, bias_refs, o_ref, band):
    nq = GRID_W
    lane = lax.broadcasted_iota(jnp.int32, (nq, LANES), 1)
    low = lane < NA_HEAD_DIM
    scale = NA_HEAD_DIM ** -0.5
    nt = (((1,), (1,)), ((), ()))
    starts = [pl.multiple_of(r0t[it * ATT_ROWS + row] * GRID_W, GRID_W) for row in range(ATT_ROWS)]
    for hp in range(NA_HEADS // 2):
        cols = slice(hp * LANES, (hp + 1) * LANES)
        stacked = []
        for row in range(ATT_ROWS):
            qp = q_ref[row * nq:(row + 1) * nq, cols].astype(F32) * scale
            stacked += [jnp.where(low, qp, 0.0), jnp.where(low, 0.0, qp)]
        qs_all = jnp.concatenate(stacked, axis=0).astype(BF16)
        s_ctx_all = lax.dot_general(qs_all, kc_ref[:, cols], nt, preferred_element_type=F32)
        e_ctx_rows, e_loc_rows, dens = [], [], []
        for row in range(ATT_ROWS):
            part = slice(row * 2 * nq, (row + 1) * 2 * nq)
            kb = k_ref[pl.ds(starts[row], band), cols]
            s_loc = lax.dot_general(qs_all[part], kb, nt, preferred_element_type=F32)
            bias_ref = bias_refs[row]
            s_loc = s_loc + jnp.concatenate([bias_ref[0, 2 * hp], bias_ref[0, 2 * hp + 1]], axis=0)
            s_ctx = s_ctx_all[part]
            mx = jnp.maximum(jnp.max(s_loc, axis=-1, keepdims=True), jnp.max(s_ctx, axis=-1, keepdims=True))
            e_loc = jnp.exp(s_loc - mx)
            e_ctx = jnp.exp(s_ctx - mx)
            dens.append(jnp.sum(e_loc, axis=-1, keepdims=True) + jnp.sum(e_ctx, axis=-1, keepdims=True))
            e_loc_rows.append(e_loc.astype(BF16))
            e_ctx_rows.append(e_ctx.astype(BF16))
        o_ctx_all = jnp.dot(jnp.concatenate(e_ctx_rows, axis=0), vc_ref[:, cols], preferred_element_type=F32)
        for row in range(ATT_ROWS):
            part = slice(row * 2 * nq, (row + 1) * 2 * nq)
            vb = v_ref[pl.ds(starts[row], band), cols]
            o = (jnp.dot(e_loc_rows[row], vb, preferred_element_type=F32) + o_ctx_all[part]) / dens[row]
            o_ref[row * nq:(row + 1) * nq, cols] = jnp.where(low, o[:nq], o[nq:]).astype(BF16)


def _attention_bias(rpb, seq):
    rows = seq // GRID_W
    kr = min(NA_WIN_ROWS, rows)
    kc = NA_WIN_COLS
    cq = np.arange(GRID_W)
    c0 = np.clip(cq - kc // 2, 0, GRID_W - kc)
    ck = np.arange(GRID_W)
    inside = (ck[None, :] >= c0[:, None]) & (ck[None, :] < c0[:, None] + kc)
    dc = np.clip(ck[None, :] - cq[:, None] + (NA_WIN_COLS - 1), 0, 2 * NA_WIN_COLS - 2)
    n_dr = 2 * NA_WIN_ROWS - 1
    n_dc = 2 * NA_WIN_COLS - 1
    pick = jnp.asarray((np.arange(n_dc)[:, None] == dc.reshape(1, -1)).astype(np.float32))
    picked = jnp.dot(rpb.reshape(-1, n_dc), pick, precision=lax.Precision.HIGHEST)
    picked = picked.reshape(NA_HEADS, 2 * NA_WIN_ROWS - 1, GRID_W, GRID_W)
    table = jnp.where(jnp.asarray(inside)[None, None], picked, NEG_BIG)
    table = table.transpose(0, 2, 1, 3).reshape(NA_HEADS, GRID_W, n_dr * GRID_W)
    classes = []
    for cl in range(kr):
        lo = (NA_WIN_ROWS - 1 - cl) * GRID_W
        classes.append(table[:, :, lo:lo + kr * GRID_W])
    classes.append(jnp.full((NA_HEADS, GRID_W, kr * GRID_W), NEG_BIG, F32))
    return jnp.stack(classes, axis=0)


def _post_kernel(ha_ref, hc_ref, att_ref, zb_ref, hf_ref, hb_ref, lg_ref,
                 sh1_ref, sc1_ref, g1_ref, sh2_ref, sc2_ref,
                 wgl0_ref, wgl1_ref, wgl2_ref, bgl0_ref, bgl1_ref, bgl2_ref,
                 wpa_ref, wpc_ref, wpl_ref, wo_ref, bo_ref, l1g_ref, l1b_ref,
                 rw_ref, rb_ref, tri_ref, upper_ref,
                 h1_ref, xs_ref, pos_ref, cnt_ref, u_s, logit_s, *, split, n_tiles):
    i = pl.program_id(0)

    @pl.when(i == 0)
    def _():
        u_s[...] = jnp.zeros_like(u_s)
        logit_s[...] = jnp.zeros_like(logit_s)

    prev_u = u_s[...]
    prev_logits = logit_s[...]

    h = jnp.where(jnp.minimum(i, n_tiles - 1) < split, ha_ref[...], hc_ref[...])
    u1 = (_layer_norm(h) * (1.0 + sc1_ref[0]) + sh1_ref[0]).astype(BF16)
    y_a = jnp.dot(att_ref[...], wpa_ref[...], preferred_element_type=F32)
    y_b = jnp.dot(zb_ref[...], wpc_ref[...], preferred_element_type=F32)
    prev_pos, prev_w = _route(prev_logits, tri_ref, upper_ref, pos_ref, cnt_ref)
    zc = jax.nn.gelu(lg_ref[...]) * (hf_ref[...] + hb_ref[...])
    y_c = jnp.dot(zc.astype(BF16), wpl_ref[...], preferred_element_type=F32)
    merged = (jax.nn.sigmoid(jnp.dot(u1, wgl0_ref[0], preferred_element_type=F32) + bgl0_ref[0]) * y_a
              + jax.nn.sigmoid(jnp.dot(u1, wgl1_ref[0], preferred_element_type=F32) + bgl1_ref[0]) * y_b
              + jax.nn.sigmoid(jnp.dot(u1, wgl2_ref[0], preferred_element_type=F32) + bgl2_ref[0]) * y_c)
    y = jnp.dot(merged.astype(BF16), wo_ref[...], preferred_element_type=F32) + bo_ref[...]
    h1 = _layer_norm(DEEPNORM_ALPHA * h + g1_ref[0] * y) * l1g_ref[...] + l1b_ref[...]
    h1_ref[...] = h1
    u2 = _layer_norm(h1) * (1.0 + sc2_ref[0]) + sh2_ref[0]
    _sort_rows(prev_u, prev_pos, prev_w, xs_ref)

    u_hi = u2.astype(BF16)
    u_lo = (u2 - u_hi.astype(F32)).astype(BF16)
    by_hi = jnp.dot(u_hi, rw_ref[...], preferred_element_type=F32)
    logit_s[...] = (by_hi[:, :LANES] + by_hi[:, LANES:]
                    + jnp.dot(u_lo, rw_ref[:, :LANES], preferred_element_type=F32) + rb_ref[...])
    u_s[...] = u_hi


def _route(logits, tri_ref, upper_ref, pos_ref, cnt_ref):
    tm = logits.shape[0]
    lane = lax.broadcasted_iota(jnp.int32, (tm, LANES), 1)
    lane_f = lane.astype(F32)
    work = logits
    tops, idxs, hots = [], [], []
    for _ in range(TOP_K):
        mx = jnp.max(work, axis=-1, keepdims=True)
        idx = jnp.min(jnp.where(work == mx, lane_f, float(LANES)), axis=-1, keepdims=True)
        hot = lane_f == idx
        work = jnp.where(hot, -3e38, work)
        tops.append(mx); idxs.append(idx); hots.append(hot)
    exps = [jnp.exp(t - tops[0]) for t in tops]
    den = exps[0] + exps[1] + exps[2] + exps[3]
    hot_all = jnp.zeros((tm, LANES), F32)
    for hot in hots:
        hot_all = hot_all + hot.astype(F32)
    cnt = jnp.sum(hot_all, axis=0, keepdims=True)
    cnt_pad = jnp.floor((cnt + (SUBLANES - 1.0)) * (1.0 / SUBLANES)) * SUBLANES
    off = jnp.dot(jnp.broadcast_to(cnt_pad, (SUBLANES, LANES)), upper_ref[...],
                  precision=lax.Precision.HIGHEST, preferred_element_type=F32)[0:1]
    slot = off + jnp.dot(tri_ref[...], hot_all.astype(BF16), preferred_element_type=F32)
    pos4 = jnp.zeros((tm, LANES), F32)
    w_tile = jnp.zeros((tm, LANES), F32)
    for kk in range(TOP_K):
        pos_k = jnp.sum(jnp.where(hots[kk], slot, 0.0), axis=-1, keepdims=True)
        pos4 = jnp.where(lane == kk, pos_k, pos4)
        p = exps[kk] / den
        p_hi = p.astype(BF16).astype(F32)
        p_mid = (p - p_hi).astype(BF16).astype(F32)
        p_lo = p - p_hi - p_mid
        w_tile = jnp.where(hots[kk], p_hi, w_tile)
        w_tile = jnp.where(lane_f == idxs[kk] + float(N_EXPERTS), p_mid, w_tile)
        w_tile = jnp.where(lane_f == idxs[kk] + float(2 * N_EXPERTS), p_lo, w_tile)
    pos_ref[...] = pos4
    sub = lax.broadcasted_iota(jnp.int32, (SUBLANES, LANES), 0)
    cnt_ref[...] = jnp.where(sub == 0, cnt, jnp.where(sub == 1, off, 0.0))
    return pos4, w_tile


def _sort_rows(u_hi, pos4, w_tile, xs_ref):
    tm = pos4.shape[0]
    n_sorted = xs_ref.shape[0]
    pos_t = pos4.T
    r_iota = lax.broadcasted_iota(jnp.int32, (n_sorted, tm), 0).astype(F32)
    hit = r_iota == pos_t[0:1, :]
    for kk in range(1, TOP_K):
        hit = jnp.logical_or(hit, r_iota == pos_t[kk:kk + 1, :])
    perm = jnp.where(hit, 1.0, 0.0).astype(BF16)
    feats = jnp.concatenate([u_hi, w_tile.astype(BF16)], axis=1)
    xs_ref[...] = jnp.dot(perm, feats, preferred_element_type=F32)


def _post_mixer(h_parts, att, zb, hf, hb, rest, mod3, mod_base, layer, w_in_bf, b_in, wpa, wpc, wpl, wo, b_o, l1g,
                l1b, rw_pad, rb_pad, n_rows, n_lat, seq):
    tm = TM_POST
    width = BRANCH_WIDTH
    ha, hc = h_parts[0], h_parts[-1]
    split = min(ha.shape[0], n_rows) // tm
    gate_col0 = N_EARLY // D_MODEL
    tiles_per_seq = seq // tm
    n_lat_tiles = n_lat // tm
    n_groups_lat = n_lat // seq
    tri = jnp.asarray(np.tril(np.ones((tm, tm), np.float32), -1), BF16)
    upper = jnp.asarray(np.triu(np.ones((LANES, LANES), np.float32), 1))
    n_tiles = n_rows // tm

    def dense(i):
        return jnp.minimum(i, n_tiles - 1)

    def routed(i):
        return jnp.maximum(i - 1, 0)

    def group(i):
        return jnp.where(dense(i) < n_lat_tiles, dense(i) // tiles_per_seq, n_groups_lat)

    def rows(wd, col=0):
        return pl.BlockSpec((tm, wd), lambda i: (dense(i), col))

    def full(shape):
        return pl.BlockSpec(shape, lambda i: (0,) * len(shape))

    def mod(which):
        return pl.BlockSpec((1, 1, D_MODEL), lambda i: (mod_base + group(i) * 6 + which, 0, 0))

    in_specs = [pl.BlockSpec((tm, D_MODEL), lambda i: (jnp.minimum(dense(i), split - 1), 0)),
                pl.BlockSpec((tm, D_MODEL), lambda i: (jnp.maximum(dense(i) - split, 0), 0)),
                rows(width), rows(width), rows(width), rows(width), rows(width, 3),
                mod(0), mod(1), mod(2), mod(3), mod(4)]
    in_specs += [pl.BlockSpec((1, D_MODEL, D_MODEL), lambda i, c=c: (layer, 0, gate_col0 + c)) for c in range(3)]
    in_specs += [pl.BlockSpec((1, 1, D_MODEL), lambda i, c=c: (layer, 0, gate_col0 + c)) for c in range(3)]
    in_specs += [full(wpa.shape), full(wpc.shape), full(wpl.shape), full(wo.shape), full((1, D_MODEL)),
                full((1, D_MODEL)), full((1, D_MODEL)), full(rw_pad.shape), full(rb_pad.shape), full(tri.shape),
                full(upper.shape)]
    out_specs = [rows(D_MODEL), pl.BlockSpec((TOK_BLOCK, XS_WIDTH), lambda i: (routed(i), 0)),
                 pl.BlockSpec((tm, LANES), lambda i: (routed(i), 0)),
                 pl.BlockSpec((SUBLANES, LANES), lambda i: (routed(i), 0))]
    return pl.pallas_call(
        functools.partial(_post_kernel, split=split, n_tiles=n_tiles),
        grid=(n_tiles + 1,),
        in_specs=in_specs, out_specs=out_specs,
        out_shape=[jax.ShapeDtypeStruct((n_rows, D_MODEL), F32),
                   jax.ShapeDtypeStruct((n_tiles * TOK_BLOCK, XS_WIDTH), F32),
                   jax.ShapeDtypeStruct((n_rows, LANES), F32),
                   jax.ShapeDtypeStruct((n_tiles * SUBLANES, LANES), F32)],
        scratch_shapes=[pltpu.VMEM((tm, D_MODEL), BF16), pltpu.VMEM((tm, LANES), F32)],
        compiler_params=_params(1),
        name="post_mixer",
    )(ha, hc, att, zb, hf, hb, rest, mod3, mod3, mod3, mod3, mod3,
      w_in_bf, w_in_bf, w_in_bf, b_in.reshape(DEPTH, 1, -1), b_in.reshape(DEPTH, 1, -1), b_in.reshape(DEPTH, 1, -1),
      wpa, wpc, wpl, wo, b_o.reshape(1, D_MODEL), l1g.reshape(1, D_MODEL), l1b.reshape(1, D_MODEL),
      rw_pad, rb_pad, tri, upper)


def _expert_plan(cnt_out, n_tok_tiles, n_rows):
    tm = TM_EXPERT
    co = cnt_out.reshape(n_tok_tiles, SUBLANES, LANES)
    cnt = (co[:, 0, :N_EXPERTS].astype(jnp.int32) + SUBLANES - 1) // SUBLANES * SUBLANES
    off = co[:, 1, :N_EXPERTS].astype(jnp.int32)
    cum_end = jnp.cumsum(cnt, axis=0)
    cum = cum_end - cnt
    total = cum_end[-1]
    n_et = (total + tm - 1) // tm
    et_end = jnp.cumsum(n_et)
    n_act = et_end[-1:].astype(jnp.int32)
    n_tiles = -(-n_tok_tiles * TOK_BLOCK // tm) + N_EXPERTS
    j = jnp.arange(n_tiles, dtype=jnp.int32)
    tile_e = jnp.minimum(jnp.sum((et_end[None, :] <= j[:, None]).astype(jnp.int32), axis=1), N_EXPERTS - 1)
    pick_e = (tile_e[:, None] == jnp.arange(N_EXPERTS, dtype=jnp.int32)[None, :]).astype(F32)

    def per_tile(table):
        return jnp.dot(pick_e, table.astype(F32), precision=lax.Precision.HIGHEST).astype(jnp.int32)

    row0 = (j - per_tile(et_end - n_et)) * tm
    n_rows_tile = jnp.clip(per_tile(total) - row0, 0, tm)
    cum_e = per_tile(cum.T)
    cum_end_e = per_tile(cum_end.T)
    delta_e = per_tile((jnp.arange(n_tok_tiles, dtype=jnp.int32)[:, None] * TOK_BLOCK + off - cum).T)
    q = row0[:, None] + SUBLANES * jnp.arange(tm // SUBLANES, dtype=jnp.int32)[None, :]
    inside = jnp.logical_and(cum_e.T[:, :, None] <= q[None], q[None] < cum_end_e.T[:, :, None])
    src = q + jnp.sum(jnp.where(inside, delta_e.T[:, :, None], 0), axis=0)
    used = off[:, N_EXPERTS - 1] + cnt[:, N_EXPERTS - 1]
    first = jnp.concatenate([jnp.ones((1,), jnp.int32), (tile_e[1:] != tile_e[:-1]).astype(jnp.int32)])
    group = jnp.cumsum(first) - 1
    after = per_tile(et_end)
    next_e = jnp.where(after < n_act[0],
                       jnp.minimum(jnp.sum((et_end[None, :] <= after[:, None]).astype(jnp.int32), axis=1),
                                   N_EXPERTS - 1), -1)
    return (tile_e.astype(jnp.int32), n_rows_tile.astype(jnp.int32), n_act, src.reshape(-1).astype(jnp.int32),
            used.astype(jnp.int32), first, group.astype(jnp.int32), next_e.astype(jnp.int32), n_tiles)


def _expert_kernel(tile_e, n_rows_t, n_act, src_t, used_t, first_t, group_t, next_t,
                   xs_hbm, wgu_hbm, bgu_ref, wdn_hbm, bdn_ref, ys_hbm,
                   xin, yout, wgu_f, wdn_f, wgu_s, wdn_s, zeros, sem_in, sem_out, sem_zero, sem_wgu, sem_wdn,
                   *, tm, tok_block, n_tok_tiles, layer):
    j = pl.program_id(0)
    na = n_act[0]
    chunks = tm // SUBLANES

    def weight_copies(e, wslot):
        return (pltpu.make_async_copy(wgu_hbm.at[layer, e], wgu_f.at[wslot], sem_wgu.at[wslot]),
                pltpu.make_async_copy(wdn_hbm.at[layer, e], wdn_f.at[wslot], sem_wdn.at[wslot]))

    def gather(src, dst, size, slot):
        return pltpu.make_async_copy(xs_hbm.at[pl.ds(src, size)], xin.at[slot, pl.ds(dst, size)], sem_in.at[slot])

    def scatter(src, dst, size, slot):
        return pltpu.make_async_copy(yout.at[slot, pl.ds(dst, size)], ys_hbm.at[pl.ds(src, size)], sem_out.at[slot])

    def start_chunks(jj, slot, copy):
        def one(c, priority):
            src = pl.multiple_of(src_t[jj * chunks + c], SUBLANES)
            copy(src, pl.multiple_of(c * SUBLANES, SUBLANES), SUBLANES, slot).start(priority=priority)

        def body(c, carry):
            one(c, 0)
            return carry

        def body_unrolled(g, carry):
            for u in range(SUBLANES):
                one(g * SUBLANES + u, u % 2)
            return carry

        n = lax.shift_right_logical(n_rows_t[jj], 3)

        @pl.when(n == chunks)
        def _():
            lax.fori_loop(0, chunks // SUBLANES, body_unrolled, 0)

        @pl.when(n != chunks)
        def _():
            lax.fori_loop(0, n, body, 0)

    def wait_rows(jj, slot, copy):
        n = n_rows_t[jj]
        size = tm
        while size >= SUBLANES:
            @pl.when((n & size) != 0)
            def _(size=size):
                copy(0, 0, size, slot).wait()
            size //= 2

    slot = j % 2

    @pl.when(j == 0)
    def _():
        xin[...] = jnp.zeros_like(xin)
        start_chunks(0, 0, gather)
        zeros[...] = jnp.zeros_like(zeros)

        def clear_tail(i, copy_op):
            used = used_t[i]
            tail = tok_block - used
            size = ZERO_ROWS
            sizes = []
            while size >= SUBLANES:
                sizes.append(size)
                size //= 2
            for size in sizes:
                @pl.when((tail & size) != 0)
                def _(size=size):
                    at = pl.multiple_of(i * tok_block + used + (tail & ~(2 * size - 1)), SUBLANES)
                    copy_op(pltpu.make_async_copy(zeros.at[pl.ds(0, size)], ys_hbm.at[pl.ds(at, size)], sem_zero))

        def start_clear(i, c):
            clear_tail(i, lambda cp: cp.start())
            return c

        def wait_clear(i, c):
            clear_tail(i, lambda cp: cp.wait())
            return c

        lax.fori_loop(0, n_tok_tiles, start_clear, 0)
        lax.fori_loop(0, n_tok_tiles, wait_clear, 0)

    @pl.when(j + 1 < na)
    def _():
        start_chunks(j + 1, 1 - slot, gather)

    @pl.when(j < na)
    def _():
        e = tile_e[j]

        @pl.when(first_t[j] == 1)
        def _():
            wslot = group_t[j] % 2

            @pl.when(j == 0)
            def _():
                for cp in weight_copies(e, wslot):
                    cp.start()

            for cp in weight_copies(e, wslot):
                cp.wait()
            wgu_s[...] = wgu_f[wslot].astype(BF16)
            wdn_s[...] = wdn_f[wslot].astype(BF16)

            @pl.when(next_t[j] >= 0)
            def _():
                for cp in weight_copies(next_t[j], 1 - wslot):
                    cp.start()

        wait_rows(j, slot, gather)

        def compute(rows):
            x = xin[slot, pl.ds(0, rows)]
            lane = lax.broadcasted_iota(jnp.int32, (rows, LANES), 1)
            p = jnp.sum(jnp.where(lane % N_EXPERTS == e, x[:, D_MODEL:], 0.0), axis=-1, keepdims=True)
            gu = (jnp.dot(x[:, :D_MODEL].astype(BF16), wgu_s[...], preferred_element_type=F32)
                  + bgu_ref[0, pl.ds(e, 1), :])
            f = gu.shape[1] // 2
            gate = jnp.minimum(gu[:, :f], SWIGLU_LIMIT)
            up = jnp.clip(gu[:, f:], -SWIGLU_LIMIT, SWIGLU_LIMIT)
            hid = (up + 1.0) * gate * jax.nn.sigmoid(SWIGLU_ALPHA * gate)
            y = (jnp.dot(hid.astype(BF16), wdn_s[...], preferred_element_type=F32)
                 + bdn_ref[0, pl.ds(e, 1), :])
            yout[slot, pl.ds(0, rows)] = y * p

        quarter = tm // 4
        for part in range(1, 5):
            @pl.when(jnp.logical_and(n_rows_t[j] > (part - 1) * quarter, n_rows_t[j] <= part * quarter))
            def _(part=part):
                compute(part * quarter)

        start_chunks(j, slot, scatter)

        @pl.when(j >= 1)
        def _():
            wait_rows(j - 1, 1 - slot, scatter)

        @pl.when(j == na - 1)
        def _():
            wait_rows(j, slot, scatter)


def _experts(xs, plan, layer, w_gu, b_gu, w_dn, b_dn):
    tile_e, n_rows_tile, n_act, src, used, first, group, next_e, n_tiles = plan
    tm = TM_EXPERT
    f2 = w_gu.shape[-1]

    in_specs = [pl.BlockSpec(memory_space=pl.ANY),
                pl.BlockSpec(memory_space=pl.ANY),
                pl.BlockSpec((1, N_EXPERTS, f2), lambda j, *_: (layer, 0, 0)),
                pl.BlockSpec(memory_space=pl.ANY),
                pl.BlockSpec((1, N_EXPERTS, D_MODEL), lambda j, *_: (layer, 0, 0))]
    grid_spec = pltpu.PrefetchScalarGridSpec(
        num_scalar_prefetch=8, grid=(n_tiles,), in_specs=in_specs,
        out_specs=pl.BlockSpec(memory_space=pl.ANY),
        scratch_shapes=[pltpu.VMEM((2, tm, XS_WIDTH), F32), pltpu.VMEM((2, tm, D_MODEL), F32),
                        pltpu.VMEM((2, D_MODEL, f2), F32), pltpu.VMEM((2, f2 // 2, D_MODEL), F32),
                        pltpu.VMEM((D_MODEL, f2), BF16), pltpu.VMEM((f2 // 2, D_MODEL), BF16),
                        pltpu.VMEM((ZERO_ROWS, D_MODEL), F32),
                        pltpu.SemaphoreType.DMA((2,)), pltpu.SemaphoreType.DMA((2,)), pltpu.SemaphoreType.DMA(()),
                        pltpu.SemaphoreType.DMA((2,)), pltpu.SemaphoreType.DMA((2,))])
    return pl.pallas_call(
        functools.partial(_expert_kernel, tm=tm, tok_block=TOK_BLOCK, n_tok_tiles=xs.shape[0] // TOK_BLOCK,
                          layer=layer),
        grid_spec=grid_spec,
        out_shape=jax.ShapeDtypeStruct((xs.shape[0], D_MODEL), F32),
        compiler_params=_params(1),
        name="moe_experts",
    )(tile_e, n_rows_tile, n_act, src, used, first, group, next_e,
      xs, w_gu, b_gu, w_dn, b_dn)


def _combine_kernel(ys_ref, pos_ref, h1_ref, g2_ref, l2g_ref, l2b_ref, o_ref):
    tm = pos_ref.shape[0]
    n_sorted = ys_ref.shape[0]
    pos = pos_ref[...]
    col = lax.broadcasted_iota(jnp.int32, (tm, n_sorted), 1).astype(F32)
    sel = jnp.zeros((tm, n_sorted), F32)
    for kk in range(TOP_K):
        sel = sel + (col == pos[:, kk:kk + 1]).astype(F32)
    sel = sel.astype(BF16)
    ys = ys_ref[...]
    hi = ys.astype(BF16)
    mid = (ys - hi.astype(F32)).astype(BF16)
    y2 = jnp.dot(sel, hi, preferred_element_type=F32) + jnp.dot(sel, mid, preferred_element_type=F32)
    o_ref[...] = _layer_norm(DEEPNORM_ALPHA * h1_ref[...] + g2_ref[0] * y2) * l2g_ref[...] + l2b_ref[...]


def _combine(ys, pos4, h1, mod3, mod_base, l2g, l2b, n_lat, seq):
    n_rows = h1.shape[0]
    tm = TM_POST
    tiles_per_seq = seq // tm
    n_lat_tiles = n_lat // tm
    n_groups_lat = n_lat // seq

    def group(i):
        return jnp.where(i < n_lat_tiles, i // tiles_per_seq, n_groups_lat)

    in_specs = [pl.BlockSpec((TOK_BLOCK, D_MODEL), lambda i: (i, 0)),
                pl.BlockSpec((tm, LANES), lambda i: (i, 0)),
                pl.BlockSpec((tm, D_MODEL), lambda i: (i, 0)),
                pl.BlockSpec((1, 1, D_MODEL), lambda i: (mod_base + group(i) * 6 + 5, 0, 0)),
                pl.BlockSpec((1, D_MODEL), lambda i: (0, 0)),
                pl.BlockSpec((1, D_MODEL), lambda i: (0, 0))]
    return pl.pallas_call(
        _combine_kernel,
        grid=(n_rows // tm,),
        in_specs=in_specs,
        out_specs=pl.BlockSpec((tm, D_MODEL), lambda i: (i, 0)),
        out_shape=jax.ShapeDtypeStruct((n_rows, D_MODEL), F32),
        compiler_params=_params(1),
        name="moe_combine",
    )(ys, pos4, h1, mod3, l2g.reshape(1, D_MODEL), l2b.reshape(1, D_MODEL))


def _block_diag(w):
    two, n, d, e = w.shape
    eye = jnp.eye(n, dtype=w.dtype)
    return (w[:, :, :, None, :] * eye[None, :, None, :, None]).reshape(two, n * d, n * e)


def kernel(x, c, ctx, c_ctx, w_mod, b_mod, w_in, b_in, na_rpb, w_proj_attn, w_proj_conv, w_proj_lru, sc_conv_w, lru_conv_w, lru_conv_b, lru_lambda, lru_w_r, lru_b_r, lru_w_i, lru_b_i, w_o, b_o, ln1_g, ln1_b, router_w, router_b, exp_w_gu, exp_b_gu, exp_w_dn, exp_b_dn, ln2_g, ln2_b):
    n_batch, seq, d = x.shape
    n_ctx = ctx.shape[1]
    n_lat = n_batch * seq
    n_all = n_lat + n_batch * n_ctx
    assert d == D_MODEL and n_batch + 1 <= SUBLANES

    cc = jnp.concatenate([c, c_ctx[None], jnp.zeros((SUBLANES - n_batch - 1, d), F32)], axis=0)
    mod = _modulation(cc, w_mod, b_mod)
    groups = n_batch + 1
    mod3 = mod.reshape(DEPTH, SUBLANES, 6, d)[:, :groups].reshape(DEPTH * groups * 6, 1, d)

    cos_t, sin_t = _make_rope(seq, TM_INPROJ)
    h = (x.reshape(n_lat, d), ctx.reshape(n_batch * n_ctx, d))
    w_in_bf = w_in.astype(BF16)

    for layer in range(DEPTH):
        last = layer == DEPTH - 1
        mod_base = layer * groups * 6
        q, k, v, sb, rest = _input_projection(h, mod3, mod_base, layer, w_in_bf, b_in, N_EARLY,
                                              cos_t, sin_t, n_lat, seq)
        sp = jax.nn.softplus(-lru_lambda[layer])
        zb, hf, hb, att = _token_mixers(
            q, k, v, _attention_bias(na_rpb[layer], seq), sb, rest, sc_conv_w[layer], lru_conv_w[layer],
            lru_conv_b[layer], sp, _block_diag(lru_w_r[layer]).astype(BF16), _block_diag(lru_w_i[layer]).astype(BF16),
            lru_b_r[layer], lru_b_i[layer], n_batch, seq, n_ctx)
        n_rows = n_lat if last else n_all
        rw_full = jnp.pad(router_w[layer], ((0, 0), (0, LANES - N_EXPERTS)))
        rw_hi = rw_full.astype(BF16)
        rw_pad = jnp.concatenate([rw_hi, (rw_full - rw_hi.astype(F32)).astype(BF16)], axis=1)
        rb_pad = jnp.concatenate([router_b[layer], jnp.full((LANES - N_EXPERTS,), NEG_BIG, F32)]).reshape(1, LANES)
        h1, xs, pos4, cnt_out = _post_mixer(
            h, att, zb, hf, hb, rest, mod3, mod_base, layer, w_in_bf, b_in,
            w_proj_attn[layer].astype(BF16), w_proj_conv[layer].astype(BF16), w_proj_lru[layer].astype(BF16),
            w_o[layer].astype(BF16), b_o[layer], ln1_g[layer], ln1_b[layer], rw_pad, rb_pad, n_rows, n_lat, seq)
        plan = _expert_plan(cnt_out, n_rows // TM_POST, n_rows)
        ys = _experts(xs, plan, layer, exp_w_gu, exp_b_gu, exp_w_dn, exp_b_dn)
        h = (_combine(ys, pos4, h1, mod3, mod_base, ln2_g[layer], ln2_b[layer], n_lat, seq),)
    return h[0].reshape(n_batch, seq, d)
```

```python
import functools

import numpy as np
import jax
import jax.numpy as jnp
from jax import lax
from jax.experimental import pallas as pl
from jax.experimental.pallas import tpu as pltpu

D_MODEL = 1024
DEPTH = 2
GRID_W = 64
NA_HEADS = 8
NA_HEAD_DIM = 64
NA_WIN_ROWS = 8
NA_WIN_COLS = 16
ROPE_BASE = 10000.0
BRANCH_WIDTH = 512
LRU_C = 8.0
N_EARLY = 8 * BRANCH_WIDTH
N_EXPERTS = 32
TOP_K = 4
SWIGLU_LIMIT = 7.0
SWIGLU_ALPHA = 1.702
LN_EPS = 1e-5
DEEPNORM_ALPHA = (2 * DEPTH) ** 0.25
NEG_BIG = -1e30

LANES = 128
SUBLANES = 8
VMEM_LIMIT_BYTES = 56 * 1024 * 1024

TM_INPROJ = 512
SCAN_CHUNK = 256
ATT_ROWS = 4
ATT_QROWS = ATT_ROWS * GRID_W
TM_POST = 256
TM_EXPERT = 512
XS_WIDTH = D_MODEL + LANES
TOK_BLOCK = TM_POST * TOP_K + N_EXPERTS * SUBLANES
ZERO_ROWS = TOK_BLOCK - TM_POST * TOP_K

F32 = jnp.float32
BF16 = jnp.bfloat16


def _params(n_axes):
    return pltpu.CompilerParams(dimension_semantics=("arbitrary",) * n_axes,
                                vmem_limit_bytes=VMEM_LIMIT_BYTES)


def _layer_norm(x):
    mu = jnp.mean(x, axis=-1, keepdims=True)
    xc = x - mu
    var = jnp.mean(xc * xc, axis=-1, keepdims=True)
    return xc * lax.rsqrt(var + LN_EPS)


def _mod_kernel(c_ref, w_ref, b_ref, o_ref):
    c = c_ref[...]
    s = (c * jax.nn.sigmoid(c)).astype(BF16)
    o_ref[0] = jnp.dot(s, w_ref[0].astype(BF16), preferred_element_type=F32) + b_ref[0]


def _modulation(cc, w_mod, b_mod):
    n_out = w_mod.shape[-1]
    return pl.pallas_call(
        _mod_kernel,
        grid=(DEPTH, n_out // D_MODEL),
        in_specs=[pl.BlockSpec((SUBLANES, D_MODEL), lambda l, j: (0, 0)),
                  pl.BlockSpec((1, D_MODEL, D_MODEL), lambda l, j: (l, 0, j)),
                  pl.BlockSpec((1, 1, D_MODEL), lambda l, j: (l, 0, j))],
        out_specs=pl.BlockSpec((1, SUBLANES, D_MODEL), lambda l, j: (l, 0, j)),
        out_shape=jax.ShapeDtypeStruct((DEPTH, SUBLANES, n_out), F32),
        compiler_params=_params(2),
        name="modulation",
    )(cc, w_mod, b_mod.reshape(DEPTH, 1, n_out))


def _rope_half(x, cos, sin_signed):
    m = NA_HEAD_DIM // 4
    lane = lax.broadcasted_iota(jnp.int32, (x.shape[0], LANES), 1)
    first = (lane % (2 * m)) < m
    outs = []
    for cidx in range(x.shape[1] // LANES):
        xc = x[:, cidx * LANES:(cidx + 1) * LANES]
        partner = jnp.where(first, pltpu.roll(xc, LANES - m, 1), pltpu.roll(xc, m, 1))
        outs.append(xc * cos + partner * sin_signed)
    return jnp.concatenate(outs, axis=1)


def _inproj_kernel(ha_ref, hb_ref, sh_ref, sc_ref, w_ref, b_ref, cos_ref, sin_ref,
                   q_ref, k_ref, v_ref, sb_ref, rest_ref, *, split):
    h = jnp.where(pl.program_id(0) < split, ha_ref[...], hb_ref[...])
    xn = (_layer_norm(h) * (1.0 + sc_ref[0]) + sh_ref[0]).astype(BF16)
    half = BRANCH_WIDTH

    def columns(lo, width):
        return jnp.dot(xn, w_ref[0, :, lo:lo + width], preferred_element_type=F32) + b_ref[0, :, lo:lo + width]

    cos = cos_ref[...]
    sin = sin_ref[...]
    q_ref[...] = _rope_half(columns(0, half), cos, sin).astype(BF16)
    k_ref[...] = _rope_half(columns(half, half), cos, sin).astype(BF16)
    v_ref[...] = columns(2 * half, half).astype(BF16)
    sb_ref[...] = columns(3 * half, half)
    n_rest = rest_ref.shape[1]
    for lo in range(0, n_rest, 2 * half):
        rest_ref[:, lo:lo + 2 * half] = columns(4 * half + lo, 2 * half)


def _input_projection(h_parts, mod3, mod_base, layer, w_in_bf, b_in, n_cols, cos_t, sin_t, n_lat, seq):
    tm = TM_INPROJ
    ha, hb = h_parts[0], h_parts[-1]
    split = ha.shape[0] // tm
    m = ha.shape[0] + (hb.shape[0] if len(h_parts) == 2 else 0)
    n_lat_tiles = n_lat // tm
    tiles_per_seq = seq // tm
    n_groups_lat = n_lat // seq

    def group(i):
        return jnp.where(i < n_lat_tiles, i // tiles_per_seq, n_groups_lat)

    def rope_blk(i):
        return jnp.where(i < n_lat_tiles, i % tiles_per_seq, tiles_per_seq)

    half = BRANCH_WIDTH
    return pl.pallas_call(
        functools.partial(_inproj_kernel, split=split),
        grid=(m // tm,),
        in_specs=[pl.BlockSpec((tm, D_MODEL), lambda i: (jnp.minimum(i, split - 1), 0)),
                  pl.BlockSpec((tm, D_MODEL), lambda i: (jnp.maximum(i - split, 0), 0)),
                  pl.BlockSpec((1, 1, D_MODEL), lambda i: (mod_base + group(i) * 6 + 0, 0, 0)),
                  pl.BlockSpec((1, 1, D_MODEL), lambda i: (mod_base + group(i) * 6 + 1, 0, 0)),
                  pl.BlockSpec((1, D_MODEL, n_cols), lambda i: (layer, 0, 0)),
                  pl.BlockSpec((1, 1, n_cols), lambda i: (layer, 0, 0)),
                  pl.BlockSpec((tm, LANES), lambda i: (rope_blk(i), 0)),
                  pl.BlockSpec((tm, LANES), lambda i: (rope_blk(i), 0))],
        out_specs=[pl.BlockSpec((tm, half), lambda i: (i, 0)),
                   pl.BlockSpec((tm, half), lambda i: (i, 0)),
                   pl.BlockSpec((tm, half), lambda i: (i, 0)),
                   pl.BlockSpec((tm, half), lambda i: (i, 0)),
                   pl.BlockSpec((tm, n_cols - 4 * half), lambda i: (i, 0))],
        out_shape=[jax.ShapeDtypeStruct((m, half), BF16),
                   jax.ShapeDtypeStruct((m, half), BF16),
                   jax.ShapeDtypeStruct((m, half), BF16),
                   jax.ShapeDtypeStruct((m, half), F32),
                   jax.ShapeDtypeStruct((m, n_cols - 4 * half), F32)],
        compiler_params=_params(1),
        name="input_projection",
    )(ha, hb, mod3, mod3, w_in_bf, b_in.reshape(DEPTH, 1, -1), cos_t, sin_t)


def _make_rope(seq, tm):
    rows = seq // GRID_W
    d = np.arange(LANES) % NA_HEAD_DIM
    m = NA_HEAD_DIM // 4
    inv_freq = (ROPE_BASE ** (-jnp.arange(m, dtype=F32) / m))[d % m]
    n_pos = max(rows, GRID_W)
    ang = jnp.arange(n_pos, dtype=F32)[:, None] * inv_freq[None, :]
    by_row = jnp.asarray(d < 2 * m)[None, None, :]

    def expand(tab):
        full = jnp.where(by_row, tab[:rows, None, :], tab[None, :GRID_W, :])
        return full.reshape(seq, LANES)

    cos = expand(jnp.cos(ang))
    sin = expand(jnp.sin(ang))
    sin_signed = jnp.where(jnp.asarray((d % (2 * m)) < m)[None, :], -sin, sin)
    cos = jnp.concatenate([cos, jnp.ones((tm, LANES), F32)], axis=0)
    sin_signed = jnp.concatenate([sin_signed, jnp.zeros((tm, LANES), F32)], axis=0)
    return cos, sin_signed


def _mixer_kernel(fblk, bblk, first, last, seqb, r0t, cls,
                  sb_ref, xf_ref, xfp_ref, xfn_ref, lxb_ref, lxbp_ref, lxbn_ref,
                  scw_ref, cw_ref, cb_ref, sp_ref, wr_ref, wi_ref, br_ref, bi_ref,
                  q_ref, k_ref, v_ref, kc_ref, vc_ref, bias0_ref, bias1_ref, bias2_ref, bias3_ref,
                  zb_ref, hf_ref, hb_ref, att_ref,
                  a_s, b_s, hc_s, st_s, *, ch, n_ctx_items, band):
    it = pl.program_id(0)
    is_first = first[it] == 1
    is_last = last[it] == 1
    is_ctx = it < n_ctx_items
    b = seqb[it]
    width = BRANCH_WIDTH
    row = lax.broadcasted_iota(jnp.int32, (ch, width), 0)
    not_first = jnp.where(is_first, 0.0, 1.0).astype(F32)
    not_last = jnp.where(is_last, 0.0, 1.0).astype(F32)

    def back1(u, prev_row):
        return jnp.where(row == 0, prev_row, pltpu.roll(u, 1, 0))

    def back2(u, prev2, prev1):
        return jnp.where(row == 0, prev2, jnp.where(row == 1, prev1, pltpu.roll(u, 2, 0)))

    def fwd1(u, next_row):
        return jnp.where(row == ch - 1, next_row, pltpu.roll(u, ch - 1, 0))

    gate_cols, value_cols, lru_cols = (slice(s * width, (s + 1) * width) for s in range(3))

    def lru_input(x, p, n, prev_ok, next_ok):
        p = p * prev_ok
        n = n * next_ok
        return (cw_ref[0:1] * back2(x, p[6:7], p[7:8]) + cw_ref[1:2] * back1(x, p[7:8])
                + cw_ref[2:3] * x + cw_ref[3:4] * fwd1(x, n[0:1]) + cb_ref[...])

    def coeffs(d, xm):
        xb = xm.astype(BF16)
        r = jax.nn.sigmoid(jnp.dot(xb, wr_ref[d], preferred_element_type=F32) + br_ref[d:d + 1])
        g = jax.nn.sigmoid(jnp.dot(xb, wi_ref[d], preferred_element_type=F32) + bi_ref[d:d + 1])
        log_a = (-LRU_C * sp_ref[d:d + 1]) * r
        a = jnp.exp(log_a)
        a_s[d] = a
        b_s[d] = jnp.sqrt(-jnp.tanh(log_a) * (a * a + 1.0)) * (g * xm)

    @pl.when(jnp.logical_and(is_first, is_ctx))
    def _():
        hc_s[...] = jnp.zeros_like(hc_s)

    @pl.when(jnp.logical_and(is_first, jnp.logical_not(is_ctx)))
    def _():
        hc_s[0:1] = st_s[pl.ds(2 * b, 1), :]
        hc_s[1:2] = st_s[pl.ds(2 * b + 1, 1), :]

    coeffs(0, lru_input(xf_ref[:, lru_cols], xfp_ref[:, lru_cols], xfn_ref[:, lru_cols], not_first, not_last))
    hf = hc_s[0:1]
    for t in range(ch):
        hf = a_s[0, t:t + 1, :] * hf + b_s[0, t:t + 1, :]
        hf_ref[t:t + 1, :] = hf
    hc_s[0:1] = hf

    coeffs(1, lru_input(lxb_ref[...], lxbp_ref[...], lxbn_ref[...], not_last, not_first))

    u = xf_ref[:, gate_cols] * xf_ref[:, value_cols]
    u_prev = xfp_ref[7:8, gate_cols] * xfp_ref[7:8, value_cols] * not_first
    u_next = xfn_ref[0:1, gate_cols] * xfn_ref[0:1, value_cols] * not_last
    conv = scw_ref[0:1] * back1(u, u_prev) + scw_ref[1:2] * u + scw_ref[2:3] * fwd1(u, u_next)
    zb_ref[...] = (sb_ref[...] * conv).astype(BF16)

    hb = hc_s[1:2]
    for t in range(ch - 1, -1, -1):
        hb = a_s[1, t:t + 1, :] * hb + b_s[1, t:t + 1, :]
        hb_ref[t:t + 1, :] = hb
    hc_s[1:2] = hb

    _attention_item(it, r0t, cls, q_ref, k_ref, v_ref, kc_ref, vc_ref,
                    (bias0_ref, bias1_ref, bias2_ref, bias3_ref), att_ref, band)

    @pl.when(is_ctx)
    def _():
        st_s[pl.ds(2 * b, 1), :] = hf
        st_s[pl.ds(2 * b + 1, 1), :] = hb


def _mixer_tables(n_batch, seq, n_ctx, ch):
    assert n_ctx == ch and ch == ATT_QROWS
    nc = seq // ch
    rows = seq // GRID_W
    kr = min(NA_WIN_ROWS, rows)
    ctx0 = n_batch * seq // ch
    fblk, bblk, first, last, seqb, r0t, cls = [], [], [], [], [], [], []
    for b in range(n_batch):
        fblk.append(ctx0 + b); bblk.append(ctx0 + b); first.append(1); last.append(1); seqb.append(b)
        r0t.extend([0] * ATT_ROWS); cls.extend([kr] * ATT_ROWS)
    for b in range(n_batch):
        for c in range(nc):
            fblk.append(b * nc + c); bblk.append(b * nc + nc - 1 - c)
            first.append(int(c == 0)); last.append(int(c == nc - 1)); seqb.append(b)
            for r in range(c * ATT_ROWS, (c + 1) * ATT_ROWS):
                r0 = min(max(r - kr // 2, 0), rows - kr)
                r0t.append(r0); cls.append(r - r0)
    return [np.asarray(a, np.int32) for a in (fblk, bblk, first, last, seqb, r0t, cls)]


def _token_mixers(q, k, v, bias, sb, rest, sc_w, lru_cw, lru_cb, sp, wr_bd, wi_bd, b_r, b_i, n_batch, seq, n_ctx):
    m = sb.shape[0]
    ch = SCAN_CHUNK
    width = BRANCH_WIDTH
    tables = _mixer_tables(n_batch, seq, n_ctx, ch)
    n_items = len(tables[0])
    halo_per_chunk = ch // SUBLANES
    last_halo = m // SUBLANES - 1
    band = min(NA_WIN_ROWS, seq // GRID_W) * GRID_W
    ctx_blk0 = n_batch * seq // n_ctx

    def cur(col, which, n=1):
        return pl.BlockSpec((ch, n * width), lambda i, f, bk, *_: ((f, bk)[which][i], col))

    def prev(col, which, n=1):
        return pl.BlockSpec((SUBLANES, n * width),
                            lambda i, f, bk, *_: (jnp.maximum((f, bk)[which][i] * halo_per_chunk - 1, 0), col))

    def nxt(col, which, n=1):
        return pl.BlockSpec((SUBLANES, n * width),
                            lambda i, f, bk, *_: (jnp.minimum(((f, bk)[which][i] + 1) * halo_per_chunk, last_halo), col))

    def full(shape):
        return pl.BlockSpec(shape, lambda i, *_: (0,) * len(shape))

    in_specs = [cur(0, 0), cur(0, 0, 3), prev(0, 0, 3), nxt(0, 0, 3),
                cur(2, 1), prev(2, 1), nxt(2, 1),
                full(sc_w.shape), full(lru_cw.shape), full((1, width)), full(sp.shape),
                full(wr_bd.shape), full(wi_bd.shape), full(b_r.shape), full(b_i.shape)]
    in_specs += [cur(0, 0),
                 pl.BlockSpec((seq, width), lambda i, f, bk, fi, la, sq, *_: (sq[i], 0)),
                 pl.BlockSpec((seq, width), lambda i, f, bk, fi, la, sq, *_: (sq[i], 0)),
                 pl.BlockSpec((n_ctx, width), lambda i, f, bk, fi, la, sq, *_: (ctx_blk0 + sq[i], 0)),
                 pl.BlockSpec((n_ctx, width), lambda i, f, bk, fi, la, sq, *_: (ctx_blk0 + sq[i], 0))]
    for row in range(ATT_ROWS):
        in_specs.append(pl.BlockSpec((1, NA_HEADS, GRID_W, band),
                                     lambda i, f, bk, fi, la, sq, r0, cl, row=row: (cl[i * ATT_ROWS + row], 0, 0, 0)))
    out_specs = [cur(0, 0), cur(0, 0), cur(0, 1), cur(0, 0)]
    grid_spec = pltpu.PrefetchScalarGridSpec(
        num_scalar_prefetch=7, grid=(n_items,), in_specs=in_specs, out_specs=out_specs,
        scratch_shapes=[pltpu.VMEM((2, ch, width), F32), pltpu.VMEM((2, ch, width), F32),
                        pltpu.VMEM((SUBLANES, width), F32), pltpu.VMEM((2 * n_batch, width), F32)])
    return pl.pallas_call(
        functools.partial(_mixer_kernel, ch=ch, n_ctx_items=n_batch, band=band),
        grid_spec=grid_spec,
        out_shape=[jax.ShapeDtypeStruct((m, width), BF16),
                   jax.ShapeDtypeStruct((m, width), F32),
                   jax.ShapeDtypeStruct((m, width), F32),
                   jax.ShapeDtypeStruct((m, width), BF16)],
        compiler_params=_params(1),
        name="token_mixers",
    )(*[jnp.asarray(t) for t in tables],
      sb, rest, rest, rest, rest, rest, rest,
      sc_w, lru_cw, lru_cb.reshape(1, width), sp, wr_bd, wi_bd, b_r, b_i,
      q, k, v, k, v, bias, bias, bias, bias)


def _attention_item(it, r0t, cls, q_ref, k_ref, v_ref, kc_ref, vc_ref, bias_refs, o_ref, band):
    nq = GRID_W
    lane = lax.broadcasted_iota(jnp.int32, (nq, LANES), 1)
    low = lane < NA_HEAD_DIM
    scale = NA_HEAD_DIM ** -0.5
    nt = (((1,), (1,)), ((), ()))
    starts = [pl.multiple_of(r0t[it * ATT_ROWS + row] * GRID_W, GRID_W) for row in range(ATT_ROWS)]
    for hp in range(NA_HEADS // 2):
        cols = slice(hp * LANES, (hp + 1) * LANES)
        stacked = []
        for row in range(ATT_ROWS):
            qp = q_ref[row * nq:(row + 1) * nq, cols].astype(F32) * scale
            stacked += [jnp.where(low, qp, 0.0), jnp.where(low, 0.0, qp)]
        qs_all = jnp.concatenate(stacked, axis=0).astype(BF16)
        s_ctx_all = lax.dot_general(qs_all, kc_ref[:, cols], nt, preferred_element_type=F32)
        e_ctx_rows, e_loc_rows, dens = [], [], []
        for row in range(ATT_ROWS):
            part = slice(row * 2 * nq, (row + 1) * 2 * nq)
            kb = k_ref[pl.ds(starts[row], band), cols]
            s_loc = lax.dot_general(qs_all[part], kb, nt, preferred_element_type=F32)
            bias_ref = bias_refs[row]
            s_loc = s_loc + jnp.concatenate([bias_ref[0, 2 * hp], bias_ref[0, 2 * hp + 1]], axis=0)
            s_ctx = s_ctx_all[part]
            mx = jnp.maximum(jnp.max(s_loc, axis=-1, keepdims=True), jnp.max(s_ctx, axis=-1, keepdims=True))
            e_loc = jnp.exp(s_loc - mx)
            e_ctx = jnp.exp(s_ctx - mx)
            dens.append(jnp.sum(e_loc, axis=-1, keepdims=True) + jnp.sum(e_ctx, axis=-1, keepdims=True))
            e_loc_rows.append(e_loc.astype(BF16))
            e_ctx_rows.append(e_ctx.astype(BF16))
        o_ctx_all = jnp.dot(jnp.concatenate(e_ctx_rows, axis=0), vc_ref[:, cols], preferred_element_type=F32)
        for row in range(ATT_ROWS):
            part = slice(row * 2 * nq, (row + 1) * 2 * nq)
            vb = v_ref[pl.ds(starts[row], band), cols]
            o = (jnp.dot(e_loc_rows[row], vb, preferred_element_type=F32) + o_ctx_all[part]) / dens[row]
            o_ref[row * nq:(row + 1) * nq, cols] = jnp.where(low, o[:nq], o[nq:]).astype(BF16)


def _attention_bias(rpb, seq):
    rows = seq // GRID_W
    kr = min(NA_WIN_ROWS, rows)
    kc = NA_WIN_COLS
    cq = np.arange(GRID_W)
    c0 = np.clip(cq - kc // 2, 0, GRID_W - kc)
    ck = np.arange(GRID_W)
    inside = (ck[None, :] >= c0[:, None]) & (ck[None, :] < c0[:, None] + kc)
    dc = np.clip(ck[None, :] - cq[:, None] + (NA_WIN_COLS - 1), 0, 2 * NA_WIN_COLS - 2)
    n_dr = 2 * NA_WIN_ROWS - 1
    n_dc = 2 * NA_WIN_COLS - 1
    pick = jnp.asarray((np.arange(n_dc)[:, None] == dc.reshape(1, -1)).astype(np.float32))
    picked = jnp.dot(rpb.reshape(-1, n_dc), pick, precision=lax.Precision.HIGHEST)
    picked = picked.reshape(NA_HEADS, 2 * NA_WIN_ROWS - 1, GRID_W, GRID_W)
    table = jnp.where(jnp.asarray(inside)[None, None], picked, NEG_BIG)
    table = table.transpose(0, 2, 1, 3).reshape(NA_HEADS, GRID_W, n_dr * GRID_W)
    classes = []
    for cl in range(kr):
        lo = (NA_WIN_ROWS - 1 - cl) * GRID_W
        classes.append(table[:, :, lo:lo + kr * GRID_W])
    classes.append(jnp.full((NA_HEADS, GRID_W, kr * GRID_W), NEG_BIG, F32))
    return jnp.stack(classes, axis=0)


def _post_kernel(ha_ref, hc_ref, att_ref, zb_ref, hf_ref, hb_ref, lg_ref,
                 sh1_ref, sc1_ref, g1_ref, sh2_ref, sc2_ref,
                 wgl0_ref, wgl1_ref, wgl2_ref, bgl0_ref, bgl1_ref, bgl2_ref,
                 wpa_ref, wpc_ref, wpl_ref, wo_ref, bo_ref, l1g_ref, l1b_ref,
                 rw_ref, rb_ref, tri_ref, upper_ref,
                 h1_ref, xs_ref, pos_ref, cnt_ref, u_s, logit_s, *, split, n_tiles):
    i = pl.program_id(0)

    @pl.when(i == 0)
    def _():
        u_s[...] = jnp.zeros_like(u_s)
        logit_s[...] = jnp.zeros_like(logit_s)

    prev_u = u_s[...]
    prev_logits = logit_s[...]

    h = jnp.where(jnp.minimum(i, n_tiles - 1) < split, ha_ref[...], hc_ref[...])
    u1 = (_layer_norm(h) * (1.0 + sc1_ref[0]) + sh1_ref[0]).astype(BF16)
    y_a = jnp.dot(att_ref[...], wpa_ref[...], preferred_element_type=F32)
    y_b = jnp.dot(zb_ref[...], wpc_ref[...], preferred_element_type=F32)
    prev_pos, prev_w = _route(prev_logits, tri_ref, upper_ref, pos_ref, cnt_ref)
    zc = jax.nn.gelu(lg_ref[...]) * (hf_ref[...] + hb_ref[...])
    y_c = jnp.dot(zc.astype(BF16), wpl_ref[...], preferred_element_type=F32)
    merged = (jax.nn.sigmoid(jnp.dot(u1, wgl0_ref[0], preferred_element_type=F32) + bgl0_ref[0]) * y_a
              + jax.nn.sigmoid(jnp.dot(u1, wgl1_ref[0], preferred_element_type=F32) + bgl1_ref[0]) * y_b
              + jax.nn.sigmoid(jnp.dot(u1, wgl2_ref[0], preferred_element_type=F32) + bgl2_ref[0]) * y_c)
    y = jnp.dot(merged.astype(BF16), wo_ref[...], preferred_element_type=F32) + bo_ref[...]
    h1 = _layer_norm(DEEPNORM_ALPHA * h + g1_ref[0] * y) * l1g_ref[...] + l1b_ref[...]
    h1_ref[...] = h1
    u2 = _layer_norm(h1) * (1.0 + sc2_ref[0]) + sh2_ref[0]
    _sort_rows(prev_u, prev_pos, prev_w, xs_ref)

    u_hi = u2.astype(BF16)
    u_lo = (u2 - u_hi.astype(F32)).astype(BF16)
    by_hi = jnp.dot(u_hi, rw_ref[...], preferred_element_type=F32)
    logit_s[...] = (by_hi[:, :LANES] + by_hi[:, LANES:]
                    + jnp.dot(u_lo, rw_ref[:, :LANES], preferred_element_type=F32) + rb_ref[...])
    u_s[...] = u_hi


def _route(logits, tri_ref, upper_ref, pos_ref, cnt_ref):
    tm = logits.shape[0]
    lane = lax.broadcasted_iota(jnp.int32, (tm, LANES), 1)
    lane_f = lane.astype(F32)
    work = logits
    tops, idxs, hots = [], [], []
    for _ in range(TOP_K):
        mx = jnp.max(work, axis=-1, keepdims=True)
        idx = jnp.min(jnp.where(work == mx, lane_f, float(LANES)), axis=-1, keepdims=True)
        hot = lane_f == idx
        work = jnp.where(hot, -3e38, work)
        tops.append(mx); idxs.append(idx); hots.append(hot)
    exps = [jnp.exp(t - tops[0]) for t in tops]
    den = exps[0] + exps[1] + exps[2] + exps[3]
    hot_all = jnp.zeros((tm, LANES), F32)
    for hot in hots:
        hot_all = hot_all + hot.astype(F32)
    cnt = jnp.sum(hot_all, axis=0, keepdims=True)
    cnt_pad = jnp.floor((cnt + (SUBLANES - 1.0)) * (1.0 / SUBLANES)) * SUBLANES
    off = jnp.dot(jnp.broadcast_to(cnt_pad, (SUBLANES, LANES)), upper_ref[...],
                  precision=lax.Precision.HIGHEST, preferred_element_type=F32)[0:1]
    slot = off + jnp.dot(tri_ref[...], hot_all.astype(BF16), preferred_element_type=F32)
    pos4 = jnp.zeros((tm, LANES), F32)
    w_tile = jnp.zeros((tm, LANES), F32)
    for kk in range(TOP_K):
        pos_k = jnp.sum(jnp.where(hots[kk], slot, 0.0), axis=-1, keepdims=True)
        pos4 = jnp.where(lane == kk, pos_k, pos4)
        p = exps[kk] / den
        p_hi = p.astype(BF16).astype(F32)
        p_mid = (p - p_hi).astype(BF16).astype(F32)
        p_lo = p - p_hi - p_mid
        w_tile = jnp.where(hots[kk], p_hi, w_tile)
        w_tile = jnp.where(lane_f == idxs[kk] + float(N_EXPERTS), p_mid, w_tile)
        w_tile = jnp.where(lane_f == idxs[kk] + float(2 * N_EXPERTS), p_lo, w_tile)
    pos_ref[...] = pos4
    sub = lax.broadcasted_iota(jnp.int32, (SUBLANES, LANES), 0)
    cnt_ref[...] = jnp.where(sub == 0, cnt, jnp.where(sub == 1, off, 0.0))
    return pos4, w_tile


def _sort_rows(u_hi, pos4, w_tile, xs_ref):
    tm = pos4.shape[0]
    n_sorted = xs_ref.shape[0]
    pos_t = pos4.T
    r_iota = lax.broadcasted_iota(jnp.int32, (n_sorted, tm), 0).astype(F32)
    hit = r_iota == pos_t[0:1, :]
    for kk in range(1, TOP_K):
        hit = jnp.logical_or(hit, r_iota == pos_t[kk:kk + 1, :])
    perm = jnp.where(hit, 1.0, 0.0).astype(BF16)
    feats = jnp.concatenate([u_hi, w_tile.astype(BF16)], axis=1)
    xs_ref[...] = jnp.dot(perm, feats, preferred_element_type=F32)


def _post_mixer(h_parts, att, zb, hf, hb, rest, mod3, mod_base, layer, w_in_bf, b_in, wpa, wpc, wpl, wo, b_o, l1g,
                l1b, rw_pad, rb_pad, n_rows, n_lat, seq):
    tm = TM_POST
    width = BRANCH_WIDTH
    ha, hc = h_parts[0], h_parts[-1]
    split = min(ha.shape[0], n_rows) // tm
    gate_col0 = N_EARLY // D_MODEL
    tiles_per_seq = seq // tm
    n_lat_tiles = n_lat // tm
    n_groups_lat = n_lat // seq
    tri = jnp.asarray(np.tril(np.ones((tm, tm), np.float32), -1), BF16)
    upper = jnp.asarray(np.triu(np.ones((LANES, LANES), np.float32), 1))
    n_tiles = n_rows // tm

    def dense(i):
        return jnp.minimum(i, n_tiles - 1)

    def routed(i):
        return jnp.maximum(i - 1, 0)

    def group(i):
        return jnp.where(dense(i) < n_lat_tiles, dense(i) // tiles_per_seq, n_groups_lat)

    def rows(wd, col=0):
        return pl.BlockSpec((tm, wd), lambda i: (dense(i), col))

    def full(shape):
        return pl.BlockSpec(shape, lambda i: (0,) * len(shape))

    def mod(which):
        return pl.BlockSpec((1, 1, D_MODEL), lambda i: (mod_base + group(i) * 6 + which, 0, 0))

    in_specs = [pl.BlockSpec((tm, D_MODEL), lambda i: (jnp.minimum(dense(i), split - 1), 0)),
                pl.BlockSpec((tm, D_MODEL), lambda i: (jnp.maximum(dense(i) - split, 0), 0)),
                rows(width), rows(width), rows(width), rows(width), rows(width, 3),
                mod(0), mod(1), mod(2), mod(3), mod(4)]
    in_specs += [pl.BlockSpec((1, D_MODEL, D_MODEL), lambda i, c=c: (layer, 0, gate_col0 + c)) for c in range(3)]
    in_specs += [pl.BlockSpec((1, 1, D_MODEL), lambda i, c=c: (layer, 0, gate_col0 + c)) for c in range(3)]
    in_specs += [full(wpa.shape), full(wpc.shape), full(wpl.shape), full(wo.shape), full((1, D_MODEL)),
                full((1, D_MODEL)), full((1, D_MODEL)), full(rw_pad.shape), full(rb_pad.shape), full(tri.shape),
                full(upper.shape)]
    out_specs = [rows(D_MODEL), pl.BlockSpec((TOK_BLOCK, XS_WIDTH), lambda i: (routed(i), 0)),
                 pl.BlockSpec((tm, LANES), lambda i: (routed(i), 0)),
                 pl.BlockSpec((SUBLANES, LANES), lambda i: (routed(i), 0))]
    return pl.pallas_call(
        functools.partial(_post_kernel, split=split, n_tiles=n_tiles),
        grid=(n_tiles + 1,),
        in_specs=in_specs, out_specs=out_specs,
        out_shape=[jax.ShapeDtypeStruct((n_rows, D_MODEL), F32),
                   jax.ShapeDtypeStruct((n_tiles * TOK_BLOCK, XS_WIDTH), F32),
                   jax.ShapeDtypeStruct((n_rows, LANES), F32),
                   jax.ShapeDtypeStruct((n_tiles * SUBLANES, LANES), F32)],
        scratch_shapes=[pltpu.VMEM((tm, D_MODEL), BF16), pltpu.VMEM((tm, LANES), F32)],
        compiler_params=_params(1),
        name="post_mixer",
    )(ha, hc, att, zb, hf, hb, rest, mod3, mod3, mod3, mod3, mod3,
      w_in_bf, w_in_bf, w_in_bf, b_in.reshape(DEPTH, 1, -1), b_in.reshape(DEPTH, 1, -1), b_in.reshape(DEPTH, 1, -1),
      wpa, wpc, wpl, wo, b_o.reshape(1, D_MODEL), l1g.reshape(1, D_MODEL), l1b.reshape(1, D_MODEL),
      rw_pad, rb_pad, tri, upper)


def _expert_plan(cnt_out, n_tok_tiles, n_rows):
    tm = TM_EXPERT
    co = cnt_out.reshape(n_tok_tiles, SUBLANES, LANES)
    cnt = (co[:, 0, :N_EXPERTS].astype(jnp.int32) + SUBLANES - 1) // SUBLANES * SUBLANES
    off = co[:, 1, :N_EXPERTS].astype(jnp.int32)
    cum_end = jnp.cumsum(cnt, axis=0)
    cum = cum_end - cnt
    total = cum_end[-1]
    n_et = (total + tm - 1) // tm
    et_end = jnp.cumsum(n_et)
    n_act = et_end[-1:].astype(jnp.int32)
    n_tiles = -(-n_tok_tiles * TOK_BLOCK // tm) + N_EXPERTS
    j = jnp.arange(n_tiles, dtype=jnp.int32)
    tile_e = jnp.minimum(jnp.sum((et_end[None, :] <= j[:, None]).astype(jnp.int32), axis=1), N_EXPERTS - 1)
    pick_e = (tile_e[:, None] == jnp.arange(N_EXPERTS, dtype=jnp.int32)[None, :]).astype(F32)

    def per_tile(table):
        return jnp.dot(pick_e, table.astype(F32), precision=lax.Precision.HIGHEST).astype(jnp.int32)

    row0 = (j - per_tile(et_end - n_et)) * tm
    n_rows_tile = jnp.clip(per_tile(total) - row0, 0, tm)
    cum_e = per_tile(cum.T)
    cum_end_e = per_tile(cum_end.T)
    delta_e = per_tile((jnp.arange(n_tok_tiles, dtype=jnp.int32)[:, None] * TOK_BLOCK + off - cum).T)
    q = row0[:, None] + SUBLANES * jnp.arange(tm // SUBLANES, dtype=jnp.int32)[None, :]
    inside = jnp.logical_and(cum_e.T[:, :, None] <= q[None], q[None] < cum_end_e.T[:, :, None])
    src = q + jnp.sum(jnp.where(inside, delta_e.T[:, :, None], 0), axis=0)
    used = off[:, N_EXPERTS - 1] + cnt[:, N_EXPERTS - 1]
    first = jnp.concatenate([jnp.ones((1,), jnp.int32), (tile_e[1:] != tile_e[:-1]).astype(jnp.int32)])
    group = jnp.cumsum(first) - 1
    after = per_tile(et_end)
    next_e = jnp.where(after < n_act[0],
                       jnp.minimum(jnp.sum((et_end[None, :] <= after[:, None]).astype(jnp.int32), axis=1),
                                   N_EXPERTS - 1), -1)
    return (tile_e.astype(jnp.int32), n_rows_tile.astype(jnp.int32), n_act, src.reshape(-1).astype(jnp.int32),
            used.astype(jnp.int32), first, group.astype(jnp.int32), next_e.astype(jnp.int32), n_tiles)


def _expert_kernel(tile_e, n_rows_t, n_act, src_t, used_t, first_t, group_t, next_t,
                   xs_hbm, wgu_hbm, bgu_ref, wdn_hbm, bdn_ref, ys_hbm,
                   xin, yout, wgu_f, wdn_f, wgu_s, wdn_s, zeros, sem_in, sem_out, sem_zero, sem_wgu, sem_wdn,
                   *, tm, tok_block, n_tok_tiles, layer):
    j = pl.program_id(0)
    na = n_act[0]
    chunks = tm // SUBLANES

    def weight_copies(e, wslot):
        return (pltpu.make_async_copy(wgu_hbm.at[layer, e], wgu_f.at[wslot], sem_wgu.at[wslot]),
                pltpu.make_async_copy(wdn_hbm.at[layer, e], wdn_f.at[wslot], sem_wdn.at[wslot]))

    def gather(src, dst, size, slot):
        return pltpu.make_async_copy(xs_hbm.at[pl.ds(src, size)], xin.at[slot, pl.ds(dst, size)], sem_in.at[slot])

    def scatter(src, dst, size, slot):
        return pltpu.make_async_copy(yout.at[slot, pl.ds(dst, size)], ys_hbm.at[pl.ds(src, size)], sem_out.at[slot])

    def start_chunks(jj, slot, copy):
        def one(c, priority):
            src = pl.multiple_of(src_t[jj * chunks + c], SUBLANES)
            copy(src, pl.multiple_of(c * SUBLANES, SUBLANES), SUBLANES, slot).start(priority=priority)

        def body(c, carry):
            one(c, 0)
            return carry

        def body_unrolled(g, carry):
            for u in range(SUBLANES):
                one(g * SUBLANES + u, u % 2)
            return carry

        n = lax.shift_right_logical(n_rows_t[jj], 3)

        @pl.when(n == chunks)
        def _():
            lax.fori_loop(0, chunks // SUBLANES, body_unrolled, 0)

        @pl.when(n != chunks)
        def _():
            lax.fori_loop(0, n, body, 0)

    def wait_rows(jj, slot, copy):
        n = n_rows_t[jj]
        size = tm
        while size >= SUBLANES:
            @pl.when((n & size) != 0)
            def _(size=size):
                copy(0, 0, size, slot).wait()
            size //= 2

    slot = j % 2

    @pl.when(j == 0)
    def _():
        xin[...] = jnp.zeros_like(xin)
        start_chunks(0, 0, gather)
        zeros[...] = jnp.zeros_like(zeros)

        def clear_tail(i, copy_op):
            used = used_t[i]
            tail = tok_block - used
            size = ZERO_ROWS
            sizes = []
            while size >= SUBLANES:
                sizes.append(size)
                size //= 2
            for size in sizes:
                @pl.when((tail & size) != 0)
                def _(size=size):
                    at = pl.multiple_of(i * tok_block + used + (tail & ~(2 * size - 1)), SUBLANES)
                    copy_op(pltpu.make_async_copy(zeros.at[pl.ds(0, size)], ys_hbm.at[pl.ds(at, size)], sem_zero))

        def start_clear(i, c):
            clear_tail(i, lambda cp: cp.start())
            return c

        def wait_clear(i, c):
            clear_tail(i, lambda cp: cp.wait())
            return c

        lax.fori_loop(0, n_tok_tiles, start_clear, 0)
        lax.fori_loop(0, n_tok_tiles, wait_clear, 0)

    @pl.when(j + 1 < na)
    def _():
        start_chunks(j + 1, 1 - slot, gather)

    @pl.when(j < na)
    def _():
        e = tile_e[j]

        @pl.when(first_t[j] == 1)
        def _():
            wslot = group_t[j] % 2

            @pl.when(j == 0)
            def _():
                for cp in weight_copies(e, wslot):
                    cp.start()

            for cp in weight_copies(e, wslot):
                cp.wait()
            wgu_s[...] = wgu_f[wslot].astype(BF16)
            wdn_s[...] = wdn_f[wslot].astype(BF16)

            @pl.when(next_t[j] >= 0)
            def _():
                for cp in weight_copies(next_t[j], 1 - wslot):
                    cp.start()

        wait_rows(j, slot, gather)

        def compute(rows):
            x = xin[slot, pl.ds(0, rows)]
            lane = lax.broadcasted_iota(jnp.int32, (rows, LANES), 1)
            p = jnp.sum(jnp.where(lane % N_EXPERTS == e, x[:, D_MODEL:], 0.0), axis=-1, keepdims=True)
            gu = (jnp.dot(x[:, :D_MODEL].astype(BF16), wgu_s[...], preferred_element_type=F32)
                  + bgu_ref[0, pl.ds(e, 1), :])
            f = gu.shape[1] // 2
            gate = jnp.minimum(gu[:, :f], SWIGLU_LIMIT)
            up = jnp.clip(gu[:, f:], -SWIGLU_LIMIT, SWIGLU_LIMIT)
            hid = (up + 1.0) * gate * jax.nn.sigmoid(SWIGLU_ALPHA * gate)
            y = (jnp.dot(hid.astype(BF16), wdn_s[...], preferred_element_type=F32)
                 + bdn_ref[0, pl.ds(e, 1), :])
            yout[slot, pl.ds(0, rows)] = y * p

        quarter = tm // 4
        for part in range(1, 5):
            @pl.when(jnp.logical_and(n_rows_t[j] > (part - 1) * quarter, n_rows_t[j] <= part * quarter))
            def _(part=part):
                compute(part * quarter)

        start_chunks(j, slot, scatter)

        @pl.when(j >= 1)
        def _():
            wait_rows(j - 1, 1 - slot, scatter)

        @pl.when(j == na - 1)
        def _():
            wait_rows(j, slot, scatter)


def _experts(xs, plan, layer, w_gu, b_gu, w_dn, b_dn):
    tile_e, n_rows_tile, n_act, src, used, first, group, next_e, n_tiles = plan
    tm = TM_EXPERT
    f2 = w_gu.shape[-1]

    in_specs = [pl.BlockSpec(memory_space=pl.ANY),
                pl.BlockSpec(memory_space=pl.ANY),
                pl.BlockSpec((1, N_EXPERTS, f2), lambda j, *_: (layer, 0, 0)),
                pl.BlockSpec(memory_space=pl.ANY),
                pl.BlockSpec((1, N_EXPERTS, D_MODEL), lambda j, *_: (layer, 0, 0))]
    grid_spec = pltpu.PrefetchScalarGridSpec(
        num_scalar_prefetch=8, grid=(n_tiles,), in_specs=in_specs,
        out_specs=pl.BlockSpec(memory_space=pl.ANY),
        scratch_shapes=[pltpu.VMEM((2, tm, XS_WIDTH), F32), pltpu.VMEM((2, tm, D_MODEL), F32),
                        pltpu.VMEM((2, D_MODEL, f2), F32), pltpu.VMEM((2, f2 // 2, D_MODEL), F32),
                        pltpu.VMEM((D_MODEL, f2), BF16), pltpu.VMEM((f2 // 2, D_MODEL), BF16),
                        pltpu.VMEM((ZERO_ROWS, D_MODEL), F32),
                        pltpu.SemaphoreType.DMA((2,)), pltpu.SemaphoreType.DMA((2,)), pltpu.SemaphoreType.DMA(()),
                        pltpu.SemaphoreType.DMA((2,)), pltpu.SemaphoreType.DMA((2,))])
    return pl.pallas_call(
        functools.partial(_expert_kernel, tm=tm, tok_block=TOK_BLOCK, n_tok_tiles=xs.shape[0] // TOK_BLOCK,
                          layer=layer),
        grid_spec=grid_spec,
        out_shape=jax.ShapeDtypeStruct((xs.shape[0], D_MODEL), F32),
        compiler_params=_params(1),
        name="moe_experts",
    )(tile_e, n_rows_tile, n_act, src, used, first, group, next_e,
      xs, w_gu, b_gu, w_dn, b_dn)


def _combine_kernel(ys_ref, pos_ref, h1_ref, g2_ref, l2g_ref, l2b_ref, o_ref):
    tm = pos_ref.shape[0]
    n_sorted = ys_ref.shape[0]
    pos = pos_ref[...]
    col = lax.broadcasted_iota(jnp.int32, (tm, n_sorted), 1).astype(F32)
    sel = jnp.zeros((tm, n_sorted), F32)
    for kk in range(TOP_K):
        sel = sel + (col == pos[:, kk:kk + 1]).astype(F32)
    sel = sel.astype(BF16)
    ys = ys_ref[...]
    hi = ys.astype(BF16)
    rest = ys - hi.astype(F32)
    mid = rest.astype(BF16)
    lo = (rest - mid.astype(F32)).astype(BF16)
    y2 = (jnp.dot(sel, hi, preferred_element_type=F32) + jnp.dot(sel, mid, preferred_element_type=F32)
          + jnp.dot(sel, lo, preferred_element_type=F32))
    o_ref[...] = _layer_norm(DEEPNORM_ALPHA * h1_ref[...] + g2_ref[0] * y2) * l2g_ref[...] + l2b_ref[...]


def _combine(ys, pos4, h1, mod3, mod_base, l2g, l2b, n_lat, seq):
    n_rows = h1.shape[0]
    tm = TM_POST
    tiles_per_seq = seq // tm
    n_lat_tiles = n_lat // tm
    n_groups_lat = n_lat // seq

    def group(i):
        return jnp.where(i < n_lat_tiles, i // tiles_per_seq, n_groups_lat)

    in_specs = [pl.BlockSpec((TOK_BLOCK, D_MODEL), lambda i: (i, 0)),
                pl.BlockSpec((tm, LANES), lambda i: (i, 0)),
                pl.BlockSpec((tm, D_MODEL), lambda i: (i, 0)),
                pl.BlockSpec((1, 1, D_MODEL), lambda i: (mod_base + group(i) * 6 + 5, 0, 0)),
                pl.BlockSpec((1, D_MODEL), lambda i: (0, 0)),
                pl.BlockSpec((1, D_MODEL), lambda i: (0, 0))]
    return pl.pallas_call(
        _combine_kernel,
        grid=(n_rows // tm,),
        in_specs=in_specs,
        out_specs=pl.BlockSpec((tm, D_MODEL), lambda i: (i, 0)),
        out_shape=jax.ShapeDtypeStruct((n_rows, D_MODEL), F32),
        compiler_params=_params(1),
        name="moe_combine",
    )(ys, pos4, h1, mod3, l2g.reshape(1, D_MODEL), l2b.reshape(1, D_MODEL))


def _block_diag(w):
    two, n, d, e = w.shape
    eye = jnp.eye(n, dtype=w.dtype)
    return (w[:, :, :, None, :] * eye[None, :, None, :, None]).reshape(two, n * d, n * e)


def kernel(x, c, ctx, c_ctx, w_mod, b_mod, w_in, b_in, na_rpb, w_proj_attn, w_proj_conv, w_proj_lru, sc_conv_w, lru_conv_w, lru_conv_b, lru_lambda, lru_w_r, lru_b_r, lru_w_i, lru_b_i, w_o, b_o, ln1_g, ln1_b, router_w, router_b, exp_w_gu, exp_b_gu, exp_w_dn, exp_b_dn, ln2_g, ln2_b):
    n_batch, seq, d = x.shape
    n_ctx = ctx.shape[1]
    n_lat = n_batch * seq
    n_all = n_lat + n_batch * n_ctx
    assert d == D_MODEL and n_batch + 1 <= SUBLANES

    cc = jnp.concatenate([c, c_ctx[None], jnp.zeros((SUBLANES - n_batch - 1, d), F32)], axis=0)
    mod = _modulation(cc, w_mod, b_mod)
    groups = n_batch + 1
    mod3 = mod.reshape(DEPTH, SUBLANES, 6, d)[:, :groups].reshape(DEPTH * groups * 6, 1, d)

    cos_t, sin_t = _make_rope(seq, TM_INPROJ)
    h = (x.reshape(n_lat, d), ctx.reshape(n_batch * n_ctx, d))
    w_in_bf = w_in.astype(BF16)

    for layer in range(DEPTH):
        last = layer == DEPTH - 1
        mod_base = layer * groups * 6
        q, k, v, sb, rest = _input_projection(h, mod3, mod_base, layer, w_in_bf, b_in, N_EARLY,
                                              cos_t, sin_t, n_lat, seq)
        sp = jax.nn.softplus(-lru_lambda[layer])
        zb, hf, hb, att = _token_mixers(
            q, k, v, _attention_bias(na_rpb[layer], seq), sb, rest, sc_conv_w[layer], lru_conv_w[layer],
            lru_conv_b[layer], sp, _block_diag(lru_w_r[layer]).astype(BF16), _block_diag(lru_w_i[layer]).astype(BF16),
            lru_b_r[layer], lru_b_i[layer], n_batch, seq, n_ctx)
        n_rows = n_lat if last else n_all
        rw_full = jnp.pad(router_w[layer], ((0, 0), (0, LANES - N_EXPERTS)))
        rw_hi = rw_full.astype(BF16)
        rw_pad = jnp.concatenate([rw_hi, (rw_full - rw_hi.astype(F32)).astype(BF16)], axis=1)
        rb_pad = jnp.concatenate([router_b[layer], jnp.full((LANES - N_EXPERTS,), NEG_BIG, F32)]).reshape(1, LANES)
        h1, xs, pos4, cnt_out = _post_mixer(
            h, att, zb, hf, hb, rest, mod3, mod_base, layer, w_in_bf, b_in,
            w_proj_attn[layer].astype(BF16), w_proj_conv[layer].astype(BF16), w_proj_lru[layer].astype(BF16),
            w_o[layer].astype(BF16), b_o[layer], ln1_g[layer], ln1_b[layer], rw_pad, rb_pad, n_rows, n_lat, seq)
        plan = _expert_plan(cnt_out, n_rows // TM_POST, n_rows)
        ys = _experts(xs, plan, layer, exp_w_gu, exp_b_gu, exp_w_dn, exp_b_dn)
        h = (_combine(ys, pos4, h1, mod3, mod_base, ln2_g[layer], ln2_b[layer], n_lat, seq),)
    return h[0].reshape(n_batch, seq, d)
```

```python
import functools

import numpy as np
import jax
import jax.numpy as jnp
from jax import lax
from jax.experimental import pallas as pl
from jax.experimental.pallas import tpu as pltpu

D_MODEL = 1024
DEPTH = 2
GRID_W = 64
NA_HEADS = 8
NA_HEAD_DIM = 64
NA_WIN_ROWS = 8
NA_WIN_COLS = 16
ROPE_BASE = 10000.0
BRANCH_WIDTH = 512
LRU_C = 8.0
N_EARLY = 8 * BRANCH_WIDTH
N_EXPERTS = 32
TOP_K = 4
SWIGLU_LIMIT = 7.0
SWIGLU_ALPHA = 1.702
LN_EPS = 1e-5
DEEPNORM_ALPHA = (2 * DEPTH) ** 0.25
NEG_BIG = -1e30

LANES = 128
SUBLANES = 8
VMEM_LIMIT_BYTES = 56 * 1024 * 1024

TM_INPROJ = 512
SCAN_CHUNK = 256
ATT_ROWS = 4
ATT_QROWS = ATT_ROWS * GRID_W
TM_POST = 256
TM_EXPERT = 512
XS_WIDTH = D_MODEL + LANES
TOK_BLOCK = TM_POST * TOP_K + N_EXPERTS * SUBLANES
ZERO_ROWS = TOK_BLOCK - TM_POST * TOP_K

F32 = jnp.float32
BF16 = jnp.bfloat16


def _params(n_axes):
    return pltpu.CompilerParams(dimension_semantics=("arbitrary",) * n_axes,
                                vmem_limit_bytes=VMEM_LIMIT_BYTES)


def _layer_norm(x):
    mu = jnp.mean(x, axis=-1, keepdims=True)
    xc = x - mu
    var = jnp.mean(xc * xc, axis=-1, keepdims=True)
    return xc * lax.rsqrt(var + LN_EPS)


def _mod_kernel(c_ref, w_ref, b_ref, o_ref):
    c = c_ref[...]
    s = (c * jax.nn.sigmoid(c)).astype(BF16)
    o_ref[0] = jnp.dot(s, w_ref[0].astype(BF16), preferred_element_type=F32) + b_ref[0]


def _modulation(cc, w_mod, b_mod):
    n_out = w_mod.shape[-1]
    return pl.pallas_call(
        _mod_kernel,
        grid=(DEPTH, n_out // D_MODEL),
        in_specs=[pl.BlockSpec((SUBLANES, D_MODEL), lambda l, j: (0, 0)),
                  pl.BlockSpec((1, D_MODEL, D_MODEL), lambda l, j: (l, 0, j)),
                  pl.BlockSpec((1, 1, D_MODEL), lambda l, j: (l, 0, j))],
        out_specs=pl.BlockSpec((1, SUBLANES, D_MODEL), lambda l, j: (l, 0, j)),
        out_shape=jax.ShapeDtypeStruct((DEPTH, SUBLANES, n_out), F32),
        compiler_params=_params(2),
        name="modulation",
    )(cc, w_mod, b_mod.reshape(DEPTH, 1, n_out))


def _rope_half(x, cos, sin_signed):
    m = NA_HEAD_DIM // 4
    lane = lax.broadcasted_iota(jnp.int32, (x.shape[0], LANES), 1)
    first = (lane % (2 * m)) < m
    outs = []
    for cidx in range(x.shape[1] // LANES):
        xc = x[:, cidx * LANES:(cidx + 1) * LANES]
        partner = jnp.where(first, pltpu.roll(xc, LANES - m, 1), pltpu.roll(xc, m, 1))
        outs.append(xc * cos + partner * sin_signed)
    return jnp.concatenate(outs, axis=1)


def _inproj_kernel(ha_ref, hb_ref, sh_ref, sc_ref, w_ref, b_ref, cos_ref, sin_ref,
                   q_ref, k_ref, v_ref, sb_ref, rest_ref, *, split):
    h = jnp.where(pl.program_id(0) < split, ha_ref[...], hb_ref[...])
    xn = (_layer_norm(h) * (1.0 + sc_ref[0]) + sh_ref[0]).astype(BF16)
    half = BRANCH_WIDTH

    def columns(lo, width):
        return jnp.dot(xn, w_ref[0, :, lo:lo + width], preferred_element_type=F32) + b_ref[0, :, lo:lo + width]

    cos = cos_ref[...]
    sin = sin_ref[...]
    q_ref[...] = _rope_half(columns(0, half), cos, sin).astype(BF16)
    k_ref[...] = _rope_half(columns(half, half), cos, sin).astype(BF16)
    v_ref[...] = columns(2 * half, half).astype(BF16)
    sb_ref[...] = columns(3 * half, half)
    n_rest = rest_ref.shape[1]
    for lo in range(0, n_rest, 2 * half):
        rest_ref[:, lo:lo + 2 * half] = columns(4 * half + lo, 2 * half)


def _input_projection(h_parts, mod3, mod_base, layer, w_in_bf, b_in, n_cols, cos_t, sin_t, n_lat, seq):
    tm = TM_INPROJ
    ha, hb = h_parts[0], h_parts[-1]
    split = ha.shape[0] // tm
    m = ha.shape[0] + (hb.shape[0] if len(h_parts) == 2 else 0)
    n_lat_tiles = n_lat // tm
    tiles_per_seq = seq // tm
    n_groups_lat = n_lat // seq

    def group(i):
        return jnp.where(i < n_lat_tiles, i // tiles_per_seq, n_groups_lat)

    def rope_blk(i):
        return jnp.where(i < n_lat_tiles, i % tiles_per_seq, tiles_per_seq)

    half = BRANCH_WIDTH
    return pl.pallas_call(
        functools.partial(_inproj_kernel, split=split),
        grid=(m // tm,),
        in_specs=[pl.BlockSpec((tm, D_MODEL), lambda i: (jnp.minimum(i, split - 1), 0)),
                  pl.BlockSpec((tm, D_MODEL), lambda i: (jnp.maximum(i - split, 0), 0)),
                  pl.BlockSpec((1, 1, D_MODEL), lambda i: (mod_base + group(i) * 6 + 0, 0, 0)),
                  pl.BlockSpec((1, 1, D_MODEL), lambda i: (mod_base + group(i) * 6 + 1, 0, 0)),
                  pl.BlockSpec((1, D_MODEL, n_cols), lambda i: (layer, 0, 0)),
                  pl.BlockSpec((1, 1, n_cols), lambda i: (layer, 0, 0)),
                  pl.BlockSpec((tm, LANES), lambda i: (rope_blk(i), 0)),
                  pl.BlockSpec((tm, LANES), lambda i: (rope_blk(i), 0))],
        out_specs=[pl.BlockSpec((tm, half), lambda i: (i, 0)),
                   pl.BlockSpec((tm, half), lambda i: (i, 0)),
                   pl.BlockSpec((tm, half), lambda i: (i, 0)),
                   pl.BlockSpec((tm, half), lambda i: (i, 0)),
                   pl.BlockSpec((tm, n_cols - 4 * half), lambda i: (i, 0))],
        out_shape=[jax.ShapeDtypeStruct((m, half), BF16),
                   jax.ShapeDtypeStruct((m, half), BF16),
                   jax.ShapeDtypeStruct((m, half), BF16),
                   jax.ShapeDtypeStruct((m, half), F32),
                   jax.ShapeDtypeStruct((m, n_cols - 4 * half), F32)],
        compiler_params=_params(1),
        name="input_projection",
    )(ha, hb, mod3, mod3, w_in_bf, b_in.reshape(DEPTH, 1, -1), cos_t, sin_t)


def _make_rope(seq, tm):
    rows = seq // GRID_W
    d = np.arange(LANES) % NA_HEAD_DIM
    m = NA_HEAD_DIM // 4
    inv_freq = (ROPE_BASE ** (-jnp.arange(m, dtype=F32) / m))[d % m]
    n_pos = max(rows, GRID_W)
    ang = jnp.arange(n_pos, dtype=F32)[:, None] * inv_freq[None, :]
    by_row = jnp.asarray(d < 2 * m)[None, None, :]

    def expand(tab):
        full = jnp.where(by_row, tab[:rows, None, :], tab[None, :GRID_W, :])
        return full.reshape(seq, LANES)

    cos = expand(jnp.cos(ang))
    sin = expand(jnp.sin(ang))
    sin_signed = jnp.where(jnp.asarray((d % (2 * m)) < m)[None, :], -sin, sin)
    cos = jnp.concatenate([cos, jnp.ones((tm, LANES), F32)], axis=0)
    sin_signed = jnp.concatenate([sin_signed, jnp.zeros((tm, LANES), F32)], axis=0)
    return cos, sin_signed


def _mixer_kernel(fblk, bblk, first, last, seqb, r0t, cls,
                  sb_ref, xf_ref, xfp_ref, xfn_ref, lxb_ref, lxbp_ref, lxbn_ref,
                  scw_ref, cw_ref, cb_ref, sp_ref, wr_ref, wi_ref, br_ref, bi_ref,
                  q_ref, k_ref, v_ref, kc_ref, vc_ref, bias0_ref, bias1_ref, bias2_ref, bias3_ref,
                  zb_ref, hf_ref, hb_ref, att_ref,
                  a_s, b_s, hc_s, st_s, *, ch, n_ctx_items, band):
    it = pl.program_id(0)
    is_first = first[it] == 1
    is_last = last[it] == 1
    is_ctx = it < n_ctx_items
    b = seqb[it]
    width = BRANCH_WIDTH
    row = lax.broadcasted_iota(jnp.int32, (ch, width), 0)
    not_first = jnp.where(is_first, 0.0, 1.0).astype(F32)
    not_last = jnp.where(is_last, 0.0, 1.0).astype(F32)

    def back1(u, prev_row):
        return jnp.where(row == 0, prev_row, pltpu.roll(u, 1, 0))

    def back2(u, prev2, prev1):
        return jnp.where(row == 0, prev2, jnp.where(row == 1, prev1, pltpu.roll(u, 2, 0)))

    def fwd1(u, next_row):
        return jnp.where(row == ch - 1, next_row, pltpu.roll(u, ch - 1, 0))

    gate_cols, value_cols, lru_cols = (slice(s * width, (s + 1) * width) for s in range(3))

    def lru_input(x, p, n, prev_ok, next_ok):
        p = p * prev_ok
        n = n * next_ok
        return (cw_ref[0:1] * back2(x, p[6:7], p[7:8]) + cw_ref[1:2] * back1(x, p[7:8])
                + cw_ref[2:3] * x + cw_ref[3:4] * fwd1(x, n[0:1]) + cb_ref[...])

    def coeffs(d, xm):
        xb = xm.astype(BF16)
        r = jax.nn.sigmoid(jnp.dot(xb, wr_ref[d], preferred_element_type=F32) + br_ref[d:d + 1])
        g = jax.nn.sigmoid(jnp.dot(xb, wi_ref[d], preferred_element_type=F32) + bi_ref[d:d + 1])
        log_a = (-LRU_C * sp_ref[d:d + 1]) * r
        a = jnp.exp(log_a)
        a_s[d] = a
        b_s[d] = jnp.sqrt(-jnp.tanh(log_a) * (a * a + 1.0)) * (g * xm)

    @pl.when(jnp.logical_and(is_first, is_ctx))
    def _():
        hc_s[...] = jnp.zeros_like(hc_s)

    @pl.when(jnp.logical_and(is_first, jnp.logical_not(is_ctx)))
    def _():
        hc_s[0:1] = st_s[pl.ds(2 * b, 1), :]
        hc_s[1:2] = st_s[pl.ds(2 * b + 1, 1), :]

    coeffs(0, lru_input(xf_ref[:, lru_cols], xfp_ref[:, lru_cols], xfn_ref[:, lru_cols], not_first, not_last))
    hf = hc_s[0:1]
    for t in range(ch):
        hf = a_s[0, t:t + 1, :] * hf + b_s[0, t:t + 1, :]
        hf_ref[t:t + 1, :] = hf
    hc_s[0:1] = hf

    coeffs(1, lru_input(lxb_ref[...], lxbp_ref[...], lxbn_ref[...], not_last, not_first))

    u = xf_ref[:, gate_cols] * xf_ref[:, value_cols]
    u_prev = xfp_ref[7:8, gate_cols] * xfp_ref[7:8, value_cols] * not_first
    u_next = xfn_ref[0:1, gate_cols] * xfn_ref[0:1, value_cols] * not_last
    conv = scw_ref[0:1] * back1(u, u_prev) + scw_ref[1:2] * u + scw_ref[2:3] * fwd1(u, u_next)
    zb_ref[...] = (sb_ref[...] * conv).astype(BF16)

    hb = hc_s[1:2]
    for t in range(ch - 1, -1, -1):
        hb = a_s[1, t:t + 1, :] * hb + b_s[1, t:t + 1, :]
        hb_ref[t:t + 1, :] = hb
    hc_s[1:2] = hb

    _attention_item(it, r0t, cls, q_ref, k_ref, v_ref, kc_ref, vc_ref,
                    (bias0_ref, bias1_ref, bias2_ref, bias3_ref), att_ref, band)

    @pl.when(is_ctx)
    def _():
        st_s[pl.ds(2 * b, 1), :] = hf
        st_s[pl.ds(2 * b + 1, 1), :] = hb


def _mixer_tables(n_batch, seq, n_ctx, ch):
    assert n_ctx == ch and ch == ATT_QROWS
    nc = seq // ch
    rows = seq // GRID_W
    kr = min(NA_WIN_ROWS, rows)
    ctx0 = n_batch * seq // ch
    fblk, bblk, first, last, seqb, r0t, cls = [], [], [], [], [], [], []
    for b in range(n_batch):
        fblk.append(ctx0 + b); bblk.append(ctx0 + b); first.append(1); last.append(1); seqb.append(b)
        r0t.extend([0] * ATT_ROWS); cls.extend([kr] * ATT_ROWS)
    for b in range(n_batch):
        for c in range(nc):
            fblk.append(b * nc + c); bblk.append(b * nc + nc - 1 - c)
            first.append(int(c == 0)); last.append(int(c == nc - 1)); seqb.append(b)
            for r in range(c * ATT_ROWS, (c + 1) * ATT_ROWS):
                r0 = min(max(r - kr // 2, 0), rows - kr)
                r0t.append(r0); cls.append(r - r0)
    return [np.asarray(a, np.int32) for a in (fblk, bblk, first, last, seqb, r0t, cls)]


def _token_mixers(q, k, v, bias, sb, rest, sc_w, lru_cw, lru_cb, sp, wr_bd, wi_bd, b_r, b_i, n_batch, seq, n_ctx):
    m = sb.shape[0]
    ch = SCAN_CHUNK
    width = BRANCH_WIDTH
    tables = _mixer_tables(n_batch, seq, n_ctx, ch)
    n_items = len(tables[0])
    halo_per_chunk = ch // SUBLANES
    last_halo = m // SUBLANES - 1
    band = min(NA_WIN_ROWS, seq // GRID_W) * GRID_W
    ctx_blk0 = n_batch * seq // n_ctx

    def cur(col, which, n=1):
        return pl.BlockSpec((ch, n * width), lambda i, f, bk, *_: ((f, bk)[which][i], col))

    def prev(col, which, n=1):
        return pl.BlockSpec((SUBLANES, n * width),
                            lambda i, f, bk, *_: (jnp.maximum((f, bk)[which][i] * halo_per_chunk - 1, 0), col))

    def nxt(col, which, n=1):
        return pl.BlockSpec((SUBLANES, n * width),
                            lambda i, f, bk, *_: (jnp.minimum(((f, bk)[which][i] + 1) * halo_per_chunk, last_halo), col))

    def full(shape):
        return pl.BlockSpec(shape, lambda i, *_: (0,) * len(shape))

    in_specs = [cur(0, 0), cur(0, 0, 3), prev(0, 0, 3), nxt(0, 0, 3),
                cur(2, 1), prev(2, 1), nxt(2, 1),
                full(sc_w.shape), full(lru_cw.shape), full((1, width)), full(sp.shape),
                full(wr_bd.shape), full(wi_bd.shape), full(b_r.shape), full(b_i.shape)]
    in_specs += [cur(0, 0),
                 pl.BlockSpec((seq, width), lambda i, f, bk, fi, la, sq, *_: (sq[i], 0)),
                 pl.BlockSpec((seq, width), lambda i, f, bk, fi, la, sq, *_: (sq[i], 0)),
                 pl.BlockSpec((n_ctx, width), lambda i, f, bk, fi, la, sq, *_: (ctx_blk0 + sq[i], 0)),
                 pl.BlockSpec((n_ctx, width), lambda i, f, bk, fi, la, sq, *_: (ctx_blk0 + sq[i], 0))]
    for row in range(ATT_ROWS):
        in_specs.append(pl.BlockSpec((1, NA_HEADS, GRID_W, band),
                                     lambda i, f, bk, fi, la, sq, r0, cl, row=row: (cl[i * ATT_ROWS + row], 0, 0, 0)))
    out_specs = [cur(0, 0), cur(0, 0), cur(0, 1), cur(0, 0)]
    grid_spec = pltpu.PrefetchScalarGridSpec(
        num_scalar_prefetch=7, grid=(n_items,), in_specs=in_specs, out_specs=out_specs,
        scratch_shapes=[pltpu.VMEM((2, ch, width), F32), pltpu.VMEM((2, ch, width), F32),
                        pltpu.VMEM((SUBLANES, width), F32), pltpu.VMEM((2 * n_batch, width), F32)])
    return pl.pallas_call(
        functools.partial(_mixer_kernel, ch=ch, n_ctx_items=n_batch, band=band),
        grid_spec=grid_spec,
        out_shape=[jax.ShapeDtypeStruct((m, width), BF16),
                   jax.ShapeDtypeStruct((m, width), F32),
                   jax.ShapeDtypeStruct((m, width), F32),
                   jax.ShapeDtypeStruct((m, width), BF16)],
        compiler_params=_params(1),
        name="token_mixers",
    )(*[jnp.asarray(t) for t in tables],
      sb, rest, rest, rest, rest, rest, rest,
      sc_w, lru_cw, lru_cb.reshape(1, width), sp, wr_bd, wi_bd, b_r, b_i,
      q, k, v, k, v, bias, bias, bias, bias)


def _attention_item(it, r0t, cls, q_ref, k_ref, v_ref, kc_ref, vc_ref, bias_refs, o_ref, band):
    nq = GRID_W
    lane = lax.broadcasted_iota(jnp.int32, (nq, LANES), 1)
    low = lane < NA_HEAD_DIM
    scale = NA_HEAD_DIM ** -0.5
    nt = (((1,), (1,)), ((), ()))
    starts = [pl.multiple_of(r0t[it * ATT_ROWS + row] * GRID_W, GRID_W) for row in range(ATT_ROWS)]
    for hp in range(NA_HEADS // 2):
        cols = slice(hp * LANES, (hp + 1) * LANES)
        stacked = []
        for row in range(ATT_ROWS):
            qp = q_ref[row * nq:(row + 1) * nq, cols].astype(F32) * scale
            stacked += [jnp.where(low, qp, 0.0), jnp.where(low, 0.0, qp)]
        qs_all = jnp.concatenate(stacked, axis=0).astype(BF16)
        s_ctx_all = lax.dot_general(qs_all, kc_ref[:, cols], nt, preferred_element_type=F32)
        e_ctx_rows, e_loc_rows, dens = [], [], []
        for row in range(ATT_ROWS):
            part = slice(row * 2 * nq, (row + 1) * 2 * nq)
            kb = k_ref[pl.ds(starts[row], band), cols]
            s_loc = lax.dot_general(qs_all[part], kb, nt, preferred_element_type=F32)
            bias_ref = bias_refs[row]
            s_loc = s_loc + jnp.concatenate([bias_ref[0, 2 * hp], bias_ref[0, 2 * hp + 1]], axis=0)
            s_ctx = s_ctx_all[part]
            mx = jnp.maximum(jnp.max(s_loc, axis=-1, keepdims=True), jnp.max(s_ctx, axis=-1, keepdims=True))
            e_loc = jnp.exp(s_loc - mx)
            e_ctx = jnp.exp(s_ctx - mx)
            dens.append(jnp.sum(e_loc, axis=-1, keepdims=True) + jnp.sum(e_ctx, axis=-1, keepdims=True))
            e_loc_rows.append(e_loc.astype(BF16))
            e_ctx_rows.append(e_ctx.astype(BF16))
        o_ctx_all = jnp.dot(jnp.concatenate(e_ctx_rows, axis=0), vc_ref[:, cols], preferred_element_type=F32)
        for row in range(ATT_ROWS):
            part = slice(row * 2 * nq, (row + 1) * 2 * nq)
            vb = v_ref[pl.ds(starts[row], band), cols]
            o = (jnp.dot(e_loc_rows[row], vb, preferred_element_type=F32) + o_ctx_all[part]) / dens[row]
            o_ref[row * nq:(row + 1) * nq, cols] = jnp.where(low, o[:nq], o[nq:]).astype(BF16)


def _attention_bias(rpb, seq):
    rows = seq // GRID_W
    kr = min(NA_WIN_ROWS, rows)
    kc = NA_WIN_COLS
    cq = np.arange(GRID_W)
    c0 = np.clip(cq - kc // 2, 0, GRID_W - kc)
    ck = np.arange(GRID_W)
    inside = (ck[None, :] >= c0[:, None]) & (ck[None, :] < c0[:, None] + kc)
    dc = np.clip(ck[None, :] - cq[:, None] + (NA_WIN_COLS - 1), 0, 2 * NA_WIN_COLS - 2)
    n_dr = 2 * NA_WIN_ROWS - 1
    n_dc = 2 * NA_WIN_COLS - 1
    pick = jnp.asarray((np.arange(n_dc)[:, None] == dc.reshape(1, -1)).astype(np.float32))
    picked = jnp.dot(rpb.reshape(-1, n_dc), pick, precision=lax.Precision.HIGHEST)
    picked = picked.reshape(NA_HEADS, 2 * NA_WIN_ROWS - 1, GRID_W, GRID_W)
    table = jnp.where(jnp.asarray(inside)[None, None], picked, NEG_BIG)
    table = table.transpose(0, 2, 1, 3).reshape(NA_HEADS, GRID_W, n_dr * GRID_W)
    classes = []
    for cl in range(kr):
        lo = (NA_WIN_ROWS - 1 - cl) * GRID_W
        classes.append(table[:, :, lo:lo + kr * GRID_W])
    classes.append(jnp.full((NA_HEADS, GRID_W, kr * GRID_W), NEG_BIG, F32))
    return jnp.stack(classes, axis=0)


def _post_kernel(ha_ref, hc_ref, att_ref, zb_ref, hf_ref, hb_ref, lg_ref,
                 sh1_ref, sc1_ref, g1_ref, sh2_ref, sc2_ref,
                 wgl0_ref, wgl1_ref, wgl2_ref, bgl0_ref, bgl1_ref, bgl2_ref,
                 wpa_ref, wpc_ref, wpl_ref, wo_ref, bo_ref, l1g_ref, l1b_ref,
                 rw_ref, rb_ref, tri_ref, upper_ref,
                 h1_ref, xs_ref, pos_ref, cnt_ref, u_s, logit_s, *, split, n_tiles):
    i = pl.program_id(0)

    @pl.when(i == 0)
    def _():
        u_s[...] = jnp.zeros_like(u_s)
        logit_s[...] = jnp.zeros_like(logit_s)

    prev_u = u_s[...]
    prev_logits = logit_s[...]

    h = jnp.where(jnp.minimum(i, n_tiles - 1) < split, ha_ref[...], hc_ref[...])
    u1 = (_layer_norm(h) * (1.0 + sc1_ref[0]) + sh1_ref[0]).astype(BF16)
    y_a = jnp.dot(att_ref[...], wpa_ref[...], preferred_element_type=F32)
    y_b = jnp.dot(zb_ref[...], wpc_ref[...], preferred_element_type=F32)
    prev_pos, prev_w = _route(prev_logits, tri_ref, upper_ref, pos_ref, cnt_ref)
    zc = jax.nn.gelu(lg_ref[...]) * (hf_ref[...] + hb_ref[...])
    y_c = jnp.dot(zc.astype(BF16), wpl_ref[...], preferred_element_type=F32)
    merged = (jax.nn.sigmoid(jnp.dot(u1, wgl0_ref[0], preferred_element_type=F32) + bgl0_ref[0]) * y_a
              + jax.nn.sigmoid(jnp.dot(u1, wgl1_ref[0], preferred_element_type=F32) + bgl1_ref[0]) * y_b
              + jax.nn.sigmoid(jnp.dot(u1, wgl2_ref[0], preferred_element_type=F32) + bgl2_ref[0]) * y_c)
    y = jnp.dot(merged.astype(BF16), wo_ref[...], preferred_element_type=F32) + bo_ref[...]
    h1 = _layer_norm(DEEPNORM_ALPHA * h + g1_ref[0] * y) * l1g_ref[...] + l1b_ref[...]
    h1_ref[...] = h1
    u2 = _layer_norm(h1) * (1.0 + sc2_ref[0]) + sh2_ref[0]
    _sort_rows(prev_u, prev_pos, prev_w, xs_ref)

    u_hi = u2.astype(BF16)
    u_lo = (u2 - u_hi.astype(F32)).astype(BF16)
    by_hi = jnp.dot(u_hi, rw_ref[...], preferred_element_type=F32)
    logit_s[...] = (by_hi[:, :LANES] + by_hi[:, LANES:]
                    + jnp.dot(u_lo, rw_ref[:, :LANES], preferred_element_type=F32) + rb_ref[...])
    u_s[...] = u_hi


def _route(logits, tri_ref, upper_ref, pos_ref, cnt_ref):
    tm = logits.shape[0]
    lane = lax.broadcasted_iota(jnp.int32, (tm, LANES), 1)
    lane_f = lane.astype(F32)
    work = logits
    tops, idxs, hots = [], [], []
    for _ in range(TOP_K):
        mx = jnp.max(work, axis=-1, keepdims=True)
        idx = jnp.min(jnp.where(work == mx, lane_f, float(LANES)), axis=-1, keepdims=True)
        hot = lane_f == idx
        work = jnp.where(hot, -3e38, work)
        tops.append(mx); idxs.append(idx); hots.append(hot)
    exps = [jnp.exp(t - tops[0]) for t in tops]
    den = exps[0] + exps[1] + exps[2] + exps[3]
    hot_all = jnp.zeros((tm, LANES), F32)
    for hot in hots:
        hot_all = hot_all + hot.astype(F32)
    cnt = jnp.sum(hot_all, axis=0, keepdims=True)
    cnt_pad = jnp.floor((cnt + (SUBLANES - 1.0)) * (1.0 / SUBLANES)) * SUBLANES
    off = jnp.dot(jnp.broadcast_to(cnt_pad, (SUBLANES, LANES)), upper_ref[...],
                  precision=lax.Precision.HIGHEST, preferred_element_type=F32)[0:1]
    slot = off + jnp.dot(tri_ref[...], hot_all.astype(BF16), preferred_element_type=F32)
    pos4 = jnp.zeros((tm, LANES), F32)
    w_tile = jnp.zeros((tm, LANES), F32)
    for kk in range(TOP_K):
        pos_k = jnp.sum(jnp.where(hots[kk], slot, 0.0), axis=-1, keepdims=True)
        pos4 = jnp.where(lane == kk, pos_k, pos4)
        p = exps[kk] / den
        p_hi = p.astype(BF16).astype(F32)
        p_mid = (p - p_hi).astype(BF16).astype(F32)
        p_lo = p - p_hi - p_mid
        w_tile = jnp.where(hots[kk], p_hi, w_tile)
        w_tile = jnp.where(lane_f == idxs[kk] + float(N_EXPERTS), p_mid, w_tile)
        w_tile = jnp.where(lane_f == idxs[kk] + float(2 * N_EXPERTS), p_lo, w_tile)
    pos_ref[...] = pos4
    sub = lax.broadcasted_iota(jnp.int32, (SUBLANES, LANES), 0)
    cnt_ref[...] = jnp.where(sub == 0, cnt, jnp.where(sub == 1, off, 0.0))
    return pos4, w_tile


def _sort_rows(u_hi, pos4, w_tile, xs_ref):
    tm = pos4.shape[0]
    n_sorted = xs_ref.shape[0]
    pos_t = pos4.T
    r_iota = lax.broadcasted_iota(jnp.int32, (n_sorted, tm), 0).astype(F32)
    hit = r_iota == pos_t[0:1, :]
    for kk in range(1, TOP_K):
        hit = jnp.logical_or(hit, r_iota == pos_t[kk:kk + 1, :])
    perm = jnp.where(hit, 1.0, 0.0).astype(BF16)
    feats = jnp.concatenate([u_hi, w_tile.astype(BF16)], axis=1)
    xs_ref[...] = jnp.dot(perm, feats, preferred_element_type=F32)


def _post_mixer(h_parts, att, zb, hf, hb, rest, mod3, mod_base, layer, w_in_bf, b_in, wpa, wpc, wpl, wo, b_o, l1g,
                l1b, rw_pad, rb_pad, n_rows, n_lat, seq):
    tm = TM_POST
    width = BRANCH_WIDTH
    ha, hc = h_parts[0], h_parts[-1]
    split = min(ha.shape[0], n_rows) // tm
    gate_col0 = N_EARLY // D_MODEL
    tiles_per_seq = seq // tm
    n_lat_tiles = n_lat // tm
    n_groups_lat = n_lat // seq
    tri = jnp.asarray(np.tril(np.ones((tm, tm), np.float32), -1), BF16)
    upper = jnp.asarray(np.triu(np.ones((LANES, LANES), np.float32), 1))
    n_tiles = n_rows // tm

    def dense(i):
        return jnp.minimum(i, n_tiles - 1)

    def routed(i):
        return jnp.maximum(i - 1, 0)

    def group(i):
        return jnp.where(dense(i) < n_lat_tiles, dense(i) // tiles_per_seq, n_groups_lat)

    def rows(wd, col=0):
        return pl.BlockSpec((tm, wd), lambda i: (dense(i), col))

    def full(shape):
        return pl.BlockSpec(shape, lambda i: (0,) * len(shape))

    def mod(which):
        return pl.BlockSpec((1, 1, D_MODEL), lambda i: (mod_base + group(i) * 6 + which, 0, 0))

    in_specs = [pl.BlockSpec((tm, D_MODEL), lambda i: (jnp.minimum(dense(i), split - 1), 0)),
                pl.BlockSpec((tm, D_MODEL), lambda i: (jnp.maximum(dense(i) - split, 0), 0)),
                rows(width), rows(width), rows(width), rows(width), rows(width, 3),
                mod(0), mod(1), mod(2), mod(3), mod(4)]
    in_specs += [pl.BlockSpec((1, D_MODEL, D_MODEL), lambda i, c=c: (layer, 0, gate_col0 + c)) for c in range(3)]
    in_specs += [pl.BlockSpec((1, 1, D_MODEL), lambda i, c=c: (layer, 0, gate_col0 + c)) for c in range(3)]
    in_specs += [full(wpa.shape), full(wpc.shape), full(wpl.shape), full(wo.shape), full((1, D_MODEL)),
                full((1, D_MODEL)), full((1, D_MODEL)), full(rw_pad.shape), full(rb_pad.shape), full(tri.shape),
                full(upper.shape)]
    out_specs = [rows(D_MODEL), pl.BlockSpec((TOK_BLOCK, XS_WIDTH), lambda i: (routed(i), 0)),
                 pl.BlockSpec((tm, LANES), lambda i: (routed(i), 0)),
                 pl.BlockSpec((SUBLANES, LANES), lambda i: (routed(i), 0))]
    return pl.pallas_call(
        functools.partial(_post_kernel, split=split, n_tiles=n_tiles),
        grid=(n_tiles + 1,),
        in_specs=in_specs, out_specs=out_specs,
        out_shape=[jax.ShapeDtypeStruct((n_rows, D_MODEL), F32),
                   jax.ShapeDtypeStruct((n_tiles * TOK_BLOCK, XS_WIDTH), F32),
                   jax.ShapeDtypeStruct((n_rows, LANES), F32),
                   jax.ShapeDtypeStruct((n_tiles * SUBLANES, LANES), F32)],
        scratch_shapes=[pltpu.VMEM((tm, D_MODEL), BF16), pltpu.VMEM((tm, LANES), F32)],
        compiler_params=_params(1),
        name="post_mixer",
    )(ha, hc, att, zb, hf, hb, rest, mod3, mod3, mod3, mod3, mod3,
      w_in_bf, w_in_bf, w_in_bf, b_in.reshape(DEPTH, 1, -1), b_in.reshape(DEPTH, 1, -1), b_in.reshape(DEPTH, 1, -1),
      wpa, wpc, wpl, wo, b_o.reshape(1, D_MODEL), l1g.reshape(1, D_MODEL), l1b.reshape(1, D_MODEL),
      rw_pad, rb_pad, tri, upper)


def _expert_plan(cnt_out, n_tok_tiles, n_rows):
    tm = TM_EXPERT
    co = cnt_out.reshape(n_tok_tiles, SUBLANES, LANES)
    cnt = (co[:, 0, :N_EXPERTS].astype(jnp.int32) + SUBLANES - 1) // SUBLANES * SUBLANES
    off = co[:, 1, :N_EXPERTS].astype(jnp.int32)
    cum_end = jnp.cumsum(cnt, axis=0)
    cum = cum_end - cnt
    total = cum_end[-1]
    n_et = (total + tm - 1) // tm
    et_end = jnp.cumsum(n_et)
    n_act = et_end[-1:].astype(jnp.int32)
    n_tiles = -(-n_tok_tiles * TOK_BLOCK // tm) + N_EXPERTS
    j = jnp.arange(n_tiles, dtype=jnp.int32)
    tile_e = jnp.minimum(jnp.sum((et_end[None, :] <= j[:, None]).astype(jnp.int32), axis=1), N_EXPERTS - 1)
    pick_e = (tile_e[:, None] == jnp.arange(N_EXPERTS, dtype=jnp.int32)[None, :]).astype(F32)

    def per_tile(table):
        return jnp.dot(pick_e, table.astype(F32), precision=lax.Precision.HIGHEST).astype(jnp.int32)

    row0 = (j - per_tile(et_end - n_et)) * tm
    n_rows_tile = jnp.clip(per_tile(total) - row0, 0, tm)
    cum_e = per_tile(cum.T)
    cum_end_e = per_tile(cum_end.T)
    delta_e = per_tile((jnp.arange(n_tok_tiles, dtype=jnp.int32)[:, None] * TOK_BLOCK + off - cum).T)
    q = row0[:, None] + SUBLANES * jnp.arange(tm // SUBLANES, dtype=jnp.int32)[None, :]
    inside = jnp.logical_and(cum_e.T[:, :, None] <= q[None], q[None] < cum_end_e.T[:, :, None])
    src = q + jnp.sum(jnp.where(inside, delta_e.T[:, :, None], 0), axis=0)
    used = off[:, N_EXPERTS - 1] + cnt[:, N_EXPERTS - 1]
    first = jnp.concatenate([jnp.ones((1,), jnp.int32), (tile_e[1:] != tile_e[:-1]).astype(jnp.int32)])
    group = jnp.cumsum(first) - 1
    after = per_tile(et_end)
    next_e = jnp.where(after < n_act[0],
                       jnp.minimum(jnp.sum((et_end[None, :] <= after[:, None]).astype(jnp.int32), axis=1),
                                   N_EXPERTS - 1), -1)
    return (tile_e.astype(jnp.int32), n_rows_tile.astype(jnp.int32), n_act, src.reshape(-1).astype(jnp.int32),
            used.astype(jnp.int32), first, group.astype(jnp.int32), next_e.astype(jnp.int32), n_tiles)


def _expert_kernel(tile_e, n_rows_t, n_act, src_t, used_t, first_t, group_t, next_t,
                   xs_hbm, wgu_hbm, bgu_ref, wdn_hbm, bdn_ref, ys_hbm,
                   xin, yout, wgu_f, wdn_f, wgu_s, wdn_s, zeros, sem_in, sem_out, sem_zero, sem_wgu, sem_wdn,
                   *, tm, tok_block, n_tok_tiles, layer):
    j = pl.program_id(0)
    na = n_act[0]
    chunks = tm // SUBLANES

    def weight_copies(e, wslot):
        return (pltpu.make_async_copy(wgu_hbm.at[layer, e], wgu_f.at[wslot], sem_wgu.at[wslot]),
                pltpu.make_async_copy(wdn_hbm.at[layer, e], wdn_f.at[wslot], sem_wdn.at[wslot]))

    def gather(src, dst, size, slot):
        return pltpu.make_async_copy(xs_hbm.at[pl.ds(src, size)], xin.at[slot, pl.ds(dst, size)], sem_in.at[slot])

    def scatter(src, dst, size, slot):
        return pltpu.make_async_copy(yout.at[slot, pl.ds(dst, size)], ys_hbm.at[pl.ds(src, size)], sem_out.at[slot])

    def start_chunks(jj, slot, copy):
        def one(c, priority):
            src = pl.multiple_of(src_t[jj * chunks + c], SUBLANES)
            copy(src, pl.multiple_of(c * SUBLANES, SUBLANES), SUBLANES, slot).start(priority=priority)

        def body(c, carry):
            one(c, 0)
            return carry

        def body_unrolled(g, carry):
            for u in range(SUBLANES):
                one(g * SUBLANES + u, u % 2)
            return carry

        n = lax.shift_right_logical(n_rows_t[jj], 3)

        @pl.when(n == chunks)
        def _():
            lax.fori_loop(0, chunks // SUBLANES, body_unrolled, 0)

        @pl.when(n != chunks)
        def _():
            lax.fori_loop(0, n, body, 0)

    def wait_rows(jj, slot, copy):
        n = n_rows_t[jj]
        size = tm
        while size >= SUBLANES:
            @pl.when((n & size) != 0)
            def _(size=size):
                copy(0, 0, size, slot).wait()
            size //= 2

    slot = j % 2

    @pl.when(j == 0)
    def _():
        xin[...] = jnp.zeros_like(xin)
        start_chunks(0, 0, gather)
        zeros[...] = jnp.zeros_like(zeros)

        def clear_tail(i, copy_op):
            used = used_t[i]
            tail = tok_block - used
            size = ZERO_ROWS
            sizes = []
            while size >= SUBLANES:
                sizes.append(size)
                size //= 2
            for size in sizes:
                @pl.when((tail & size) != 0)
                def _(size=size):
                    at = pl.multiple_of(i * tok_block + used + (tail & ~(2 * size - 1)), SUBLANES)
                    copy_op(pltpu.make_async_copy(zeros.at[pl.ds(0, size)], ys_hbm.at[pl.ds(at, size)], sem_zero))

        def start_clear(i, c):
            clear_tail(i, lambda cp: cp.start())
            return c

        def wait_clear(i, c):
            clear_tail(i, lambda cp: cp.wait())
            return c

        lax.fori_loop(0, n_tok_tiles, start_clear, 0)
        lax.fori_loop(0, n_tok_tiles, wait_clear, 0)

    @pl.when(j + 1 < na)
    def _():
        start_chunks(j + 1, 1 - slot, gather)

    @pl.when(j < na)
    def _():
        e = tile_e[j]

        @pl.when(first_t[j] == 1)
        def _():
            wslot = group_t[j] % 2

            @pl.when(j == 0)
            def _():
                for cp in weight_copies(e, wslot):
                    cp.start()

            for cp in weight_copies(e, wslot):
                cp.wait()
            wgu_s[...] = wgu_f[wslot].astype(BF16)
            wdn_s[...] = wdn_f[wslot].astype(BF16)

            @pl.when(next_t[j] >= 0)
            def _():
                for cp in weight_copies(next_t[j], 1 - wslot):
                    cp.start()

        wait_rows(j, slot, gather)

        def compute(rows):
            x = xin[slot, pl.ds(0, rows)]
            lane = lax.broadcasted_iota(jnp.int32, (rows, LANES), 1)
            p = jnp.sum(jnp.where(lane % N_EXPERTS == e, x[:, D_MODEL:], 0.0), axis=-1, keepdims=True)
            gu = (jnp.dot(x[:, :D_MODEL].astype(BF16), wgu_s[...], preferred_element_type=F32)
                  + bgu_ref[0, pl.ds(e, 1), :])
            f = gu.shape[1] // 2
            gate = jnp.minimum(gu[:, :f], SWIGLU_LIMIT)
            up = jnp.clip(gu[:, f:], -SWIGLU_LIMIT, SWIGLU_LIMIT)
            hid = (up + 1.0) * gate * jax.nn.sigmoid(SWIGLU_ALPHA * gate)
            y = (jnp.dot(hid.astype(BF16), wdn_s[...], preferred_element_type=F32)
                 + bdn_ref[0, pl.ds(e, 1), :])
            yout[slot, pl.ds(0, rows)] = y * p

        quarter = tm // 4
        for part in range(1, 5):
            @pl.when(jnp.logical_and(n_rows_t[j] > (part - 1) * quarter, n_rows_t[j] <= part * quarter))
            def _(part=part):
                compute(part * quarter)

        start_chunks(j, slot, scatter)

        @pl.when(j >= 1)
        def _():
            wait_rows(j - 1, 1 - slot, scatter)

        @pl.when(j == na - 1)
        def _():
            wait_rows(j, slot, scatter)


def _experts(xs, plan, layer, w_gu, b_gu, w_dn, b_dn):
    tile_e, n_rows_tile, n_act, src, used, first, group, next_e, n_tiles = plan
    tm = TM_EXPERT
    f2 = w_gu.shape[-1]

    in_specs = [pl.BlockSpec(memory_space=pl.ANY),
                pl.BlockSpec(memory_space=pl.ANY),
                pl.BlockSpec((1, N_EXPERTS, f2), lambda j, *_: (layer, 0, 0)),
                pl.BlockSpec(memory_space=pl.ANY),
                pl.BlockSpec((1, N_EXPERTS, D_MODEL), lambda j, *_: (layer, 0, 0))]
    grid_spec = pltpu.PrefetchScalarGridSpec(
        num_scalar_prefetch=8, grid=(n_tiles,), in_specs=in_specs,
        out_specs=pl.BlockSpec(memory_space=pl.ANY),
        scratch_shapes=[pltpu.VMEM((2, tm, XS_WIDTH), F32), pltpu.VMEM((2, tm, D_MODEL), F32),
                        pltpu.VMEM((2, D_MODEL, f2), F32), pltpu.VMEM((2, f2 // 2, D_MODEL), F32),
                        pltpu.VMEM((D_MODEL, f2), BF16), pltpu.VMEM((f2 // 2, D_MODEL), BF16),
                        pltpu.VMEM((ZERO_ROWS, D_MODEL), F32),
                        pltpu.SemaphoreType.DMA((2,)), pltpu.SemaphoreType.DMA((2,)), pltpu.SemaphoreType.DMA(()),
                        pltpu.SemaphoreType.DMA((2,)), pltpu.SemaphoreType.DMA((2,))])
    return pl.pallas_call(
        functools.partial(_expert_kernel, tm=tm, tok_block=TOK_BLOCK, n_tok_tiles=xs.shape[0] // TOK_BLOCK,
                          layer=layer),
        grid_spec=grid_spec,
        out_shape=jax.ShapeDtypeStruct((xs.shape[0], D_MODEL), F32),
        compiler_params=_params(1),
        name="moe_experts",
    )(tile_e, n_rows_tile, n_act, src, used, first, group, next_e,
      xs, w_gu, b_gu, w_dn, b_dn)


def _combine_kernel(ys_ref, pos_ref, h1_ref, g2_ref, l2g_ref, l2b_ref, o_ref):
    tm = pos_ref.shape[0]
    n_sorted = ys_ref.shape[0]
    pos = pos_ref[...]
    col = lax.broadcasted_iota(jnp.int32, (tm, tm), 1).astype(F32)
    y2 = None
    for r0 in range(0, n_sorted, tm):
        ys = ys_ref[r0:r0 + tm]
        sel_b = jnp.zeros((tm, tm), F32)
        for kk in range(TOP_K):
            sel_b = sel_b + (col == pos[:, kk:kk + 1] - float(r0)).astype(F32)
        sel_b = sel_b.astype(BF16)
        hi = ys.astype(BF16)
        rest = ys - hi.astype(F32)
        mid = rest.astype(BF16)
        lo = (rest - mid.astype(F32)).astype(BF16)
        part = (jnp.dot(sel_b, hi, preferred_element_type=F32) + jnp.dot(sel_b, mid, preferred_element_type=F32)
                + jnp.dot(sel_b, lo, preferred_element_type=F32))
        y2 = part if y2 is None else y2 + part
    o_ref[...] = _layer_norm(DEEPNORM_ALPHA * h1_ref[...] + g2_ref[0] * y2) * l2g_ref[...] + l2b_ref[...]


def _combine(ys, pos4, h1, mod3, mod_base, l2g, l2b, n_lat, seq):
    n_rows = h1.shape[0]
    tm = TM_POST
    tiles_per_seq = seq // tm
    n_lat_tiles = n_lat // tm
    n_groups_lat = n_lat // seq

    def group(i):
        return jnp.where(i < n_lat_tiles, i // tiles_per_seq, n_groups_lat)

    in_specs = [pl.BlockSpec((TOK_BLOCK, D_MODEL), lambda i: (i, 0)),
                pl.BlockSpec((tm, LANES), lambda i: (i, 0)),
                pl.BlockSpec((tm, D_MODEL), lambda i: (i, 0)),
                pl.BlockSpec((1, 1, D_MODEL), lambda i: (mod_base + group(i) * 6 + 5, 0, 0)),
                pl.BlockSpec((1, D_MODEL), lambda i: (0, 0)),
                pl.BlockSpec((1, D_MODEL), lambda i: (0, 0))]
    return pl.pallas_call(
        _combine_kernel,
        grid=(n_rows // tm,),
        in_specs=in_specs,
        out_specs=pl.BlockSpec((tm, D_MODEL), lambda i: (i, 0)),
        out_shape=jax.ShapeDtypeStruct((n_rows, D_MODEL), F32),
        compiler_params=_params(1),
        name="moe_combine",
    )(ys, pos4, h1, mod3, l2g.reshape(1, D_MODEL), l2b.reshape(1, D_MODEL))


def _block_diag(w):
    two, n, d, e = w.shape
    eye = jnp.eye(n, dtype=w.dtype)
    return (w[:, :, :, None, :] * eye[None, :, None, :, None]).reshape(two, n * d, n * e)


def kernel(x, c, ctx, c_ctx, w_mod, b_mod, w_in, b_in, na_rpb, w_proj_attn, w_proj_conv, w_proj_lru, sc_conv_w, lru_conv_w, lru_conv_b, lru_lambda, lru_w_r, lru_b_r, lru_w_i, lru_b_i, w_o, b_o, ln1_g, ln1_b, router_w, router_b, exp_w_gu, exp_b_gu, exp_w_dn, exp_b_dn, ln2_g, ln2_b):
    n_batch, seq, d = x.shape
    n_ctx = ctx.shape[1]
    n_lat = n_batch * seq
    n_all = n_lat + n_batch * n_ctx
    assert d == D_MODEL and n_batch + 1 <= SUBLANES

    cc = jnp.concatenate([c, c_ctx[None], jnp.zeros((SUBLANES - n_batch - 1, d), F32)], axis=0)
    mod = _modulation(cc, w_mod, b_mod)
    groups = n_batch + 1
    mod3 = mod.reshape(DEPTH, SUBLANES, 6, d)[:, :groups].reshape(DEPTH * groups * 6, 1, d)

    cos_t, sin_t = _make_rope(seq, TM_INPROJ)
    h = (x.reshape(n_lat, d), ctx.reshape(n_batch * n_ctx, d))
    w_in_bf = w_in.astype(BF16)

    for layer in range(DEPTH):
        last = layer == DEPTH - 1
        mod_base = layer * groups * 6
        q, k, v, sb, rest = _input_projection(h, mod3, mod_base, layer, w_in_bf, b_in, N_EARLY,
                                              cos_t, sin_t, n_lat, seq)
        sp = jax.nn.softplus(-lru_lambda[layer])
        zb, hf, hb, att = _token_mixers(
            q, k, v, _attention_bias(na_rpb[layer], seq), sb, rest, sc_conv_w[layer], lru_conv_w[layer],
            lru_conv_b[layer], sp, _block_diag(lru_w_r[layer]).astype(BF16), _block_diag(lru_w_i[layer]).astype(BF16),
            lru_b_r[layer], lru_b_i[layer], n_batch, seq, n_ctx)
        n_rows = n_lat if last else n_all
        rw_full = jnp.pad(router_w[layer], ((0, 0), (0, LANES - N_EXPERTS)))
        rw_hi = rw_full.astype(BF16)
        rw_pad = jnp.concatenate([rw_hi, (rw_full - rw_hi.astype(F32)).astype(BF16)], axis=1)
        rb_pad = jnp.concatenate([router_b[layer], jnp.full((LANES - N_EXPERTS,), NEG_BIG, F32)]).reshape(1, LANES)
        h1, xs, pos4, cnt_out = _post_mixer(
            h, att, zb, hf, hb, rest, mod3, mod_base, layer, w_in_bf, b_in,
            w_proj_attn[layer].astype(BF16), w_proj_conv[layer].astype(BF16), w_proj_lru[layer].astype(BF16),
            w_o[layer].astype(BF16), b_o[layer], ln1_g[layer], ln1_b[layer], rw_pad, rb_pad, n_rows, n_lat, seq)
        plan = _expert_plan(cnt_out, n_rows // TM_POST, n_rows)
        ys = _experts(xs, plan, layer, exp_w_gu, exp_b_gu, exp_w_dn, exp_b_dn)
        h = (_combine(ys, pos4, h1, mod3, mod_base, ln2_g[layer], ln2_b[layer], n_lat, seq),)
    return h[0].reshape(n_batch, seq, d)
```

```python
import functools

import numpy as np
import jax
import jax.numpy as jnp
from jax import lax
from jax.experimental import pallas as pl
from jax.experimental.pallas import tpu as pltpu

D_MODEL = 1024
DEPTH = 2
GRID_W = 64
NA_HEADS = 8
NA_HEAD_DIM = 64
NA_WIN_ROWS = 8
NA_WIN_COLS = 16
ROPE_BASE = 10000.0
BRANCH_WIDTH = 512
LRU_C = 8.0
N_EARLY = 8 * BRANCH_WIDTH
N_EXPERTS = 32
TOP_K = 4
SWIGLU_LIMIT = 7.0
SWIGLU_ALPHA = 1.702
LN_EPS = 1e-5
DEEPNORM_ALPHA = (2 * DEPTH) ** 0.25
NEG_BIG = -1e30

LANES = 128
SUBLANES = 8
VMEM_LIMIT_BYTES = 56 * 1024 * 1024

TM_INPROJ = 512
SCAN_CHUNK = 256
ATT_ROWS = 4
ATT_QROWS = ATT_ROWS * GRID_W
TM_POST = 256
TM_EXPERT = 512
GATHER_DEPTH = 3
XS_WIDTH = D_MODEL + LANES
TOK_BLOCK = TM_POST * TOP_K + N_EXPERTS * SUBLANES
ZERO_ROWS = TOK_BLOCK - TM_POST * TOP_K

F32 = jnp.float32
BF16 = jnp.bfloat16


def _params(n_axes):
    return pltpu.CompilerParams(dimension_semantics=("arbitrary",) * n_axes,
                                vmem_limit_bytes=VMEM_LIMIT_BYTES)


def _layer_norm(x):
    mu = jnp.mean(x, axis=-1, keepdims=True)
    xc = x - mu
    var = jnp.mean(xc * xc, axis=-1, keepdims=True)
    return xc * lax.rsqrt(var + LN_EPS)


def _mod_kernel(c_ref, w_ref, b_ref, o_ref):
    c = c_ref[...]
    s = (c * jax.nn.sigmoid(c)).astype(BF16)
    o_ref[0] = jnp.dot(s, w_ref[0].astype(BF16), preferred_element_type=F32) + b_ref[0]


def _modulation(cc, w_mod, b_mod):
    n_out = w_mod.shape[-1]
    return pl.pallas_call(
        _mod_kernel,
        grid=(DEPTH, n_out // D_MODEL),
        in_specs=[pl.BlockSpec((SUBLANES, D_MODEL), lambda l, j: (0, 0)),
                  pl.BlockSpec((1, D_MODEL, D_MODEL), lambda l, j: (l, 0, j)),
                  pl.BlockSpec((1, 1, D_MODEL), lambda l, j: (l, 0, j))],
        out_specs=pl.BlockSpec((1, SUBLANES, D_MODEL), lambda l, j: (l, 0, j)),
        out_shape=jax.ShapeDtypeStruct((DEPTH, SUBLANES, n_out), F32),
        compiler_params=_params(2),
        name="modulation",
    )(cc, w_mod, b_mod.reshape(DEPTH, 1, n_out))


def _rope_half(x, cos, sin_signed):
    m = NA_HEAD_DIM // 4
    lane = lax.broadcasted_iota(jnp.int32, (x.shape[0], LANES), 1)
    first = (lane % (2 * m)) < m
    outs = []
    for cidx in range(x.shape[1] // LANES):
        xc = x[:, cidx * LANES:(cidx + 1) * LANES]
        partner = jnp.where(first, pltpu.roll(xc, LANES - m, 1), pltpu.roll(xc, m, 1))
        outs.append(xc * cos + partner * sin_signed)
    return jnp.concatenate(outs, axis=1)


def _inproj_kernel(ha_ref, hb_ref, sh_ref, sc_ref, w_ref, b_ref, cos_ref, sin_ref,
                   q_ref, k_ref, v_ref, sb_ref, rest_ref, *, split):
    h = jnp.where(pl.program_id(0) < split, ha_ref[...], hb_ref[...])
    xn = (_layer_norm(h) * (1.0 + sc_ref[0]) + sh_ref[0]).astype(BF16)
    half = BRANCH_WIDTH

    def columns(lo, width):
        return jnp.dot(xn, w_ref[0, :, lo:lo + width], preferred_element_type=F32) + b_ref[0, :, lo:lo + width]

    cos = cos_ref[...]
    sin = sin_ref[...]
    q_ref[...] = _rope_half(columns(0, half), cos, sin).astype(BF16)
    k_ref[...] = _rope_half(columns(half, half), cos, sin).astype(BF16)
    v_ref[...] = columns(2 * half, half).astype(BF16)
    sb_ref[...] = columns(3 * half, half)
    n_rest = rest_ref.shape[1]
    for lo in range(0, n_rest, 2 * half):
        rest_ref[:, lo:lo + 2 * half] = columns(4 * half + lo, 2 * half)


def _input_projection(h_parts, mod3, mod_base, layer, w_in_bf, b_in, n_cols, cos_t, sin_t, n_lat, seq):
    tm = TM_INPROJ
    ha, hb = h_parts[0], h_parts[-1]
    split = ha.shape[0] // tm
    m = ha.shape[0] + (hb.shape[0] if len(h_parts) == 2 else 0)
    n_lat_tiles = n_lat // tm
    tiles_per_seq = seq // tm
    n_groups_lat = n_lat // seq

    def group(i):
        return jnp.where(i < n_lat_tiles, i // tiles_per_seq, n_groups_lat)

    def rope_blk(i):
        return jnp.where(i < n_lat_tiles, i % tiles_per_seq, tiles_per_seq)

    half = BRANCH_WIDTH
    return pl.pallas_call(
        functools.partial(_inproj_kernel, split=split),
        grid=(m // tm,),
        in_specs=[pl.BlockSpec((tm, D_MODEL), lambda i: (jnp.minimum(i, split - 1), 0)),
                  pl.BlockSpec((tm, D_MODEL), lambda i: (jnp.maximum(i - split, 0), 0)),
                  pl.BlockSpec((1, 1, D_MODEL), lambda i: (mod_base + group(i) * 6 + 0, 0, 0)),
                  pl.BlockSpec((1, 1, D_MODEL), lambda i: (mod_base + group(i) * 6 + 1, 0, 0)),
                  pl.BlockSpec((1, D_MODEL, n_cols), lambda i: (layer, 0, 0)),
                  pl.BlockSpec((1, 1, n_cols), lambda i: (layer, 0, 0)),
                  pl.BlockSpec((tm, LANES), lambda i: (rope_blk(i), 0)),
                  pl.BlockSpec((tm, LANES), lambda i: (rope_blk(i), 0))],
        out_specs=[pl.BlockSpec((tm, half), lambda i: (i, 0)),
                   pl.BlockSpec((tm, half), lambda i: (i, 0)),
                   pl.BlockSpec((tm, half), lambda i: (i, 0)),
                   pl.BlockSpec((tm, half), lambda i: (i, 0)),
                   pl.BlockSpec((tm, n_cols - 4 * half), lambda i: (i, 0))],
        out_shape=[jax.ShapeDtypeStruct((m, half), BF16),
                   jax.ShapeDtypeStruct((m, half), BF16),
                   jax.ShapeDtypeStruct((m, half), BF16),
                   jax.ShapeDtypeStruct((m, half), F32),
                   jax.ShapeDtypeStruct((m, n_cols - 4 * half), F32)],
        compiler_params=_params(1),
        name="input_projection",
    )(ha, hb, mod3, mod3, w_in_bf, b_in.reshape(DEPTH, 1, -1), cos_t, sin_t)


def _make_rope(seq, tm):
    rows = seq // GRID_W
    d = np.arange(LANES) % NA_HEAD_DIM
    m = NA_HEAD_DIM // 4
    inv_freq = (ROPE_BASE ** (-jnp.arange(m, dtype=F32) / m))[d % m]
    n_pos = max(rows, GRID_W)
    ang = jnp.arange(n_pos, dtype=F32)[:, None] * inv_freq[None, :]
    by_row = jnp.asarray(d < 2 * m)[None, None, :]

    def expand(tab):
        full = jnp.where(by_row, tab[:rows, None, :], tab[None, :GRID_W, :])
        return full.reshape(seq, LANES)

    cos = expand(jnp.cos(ang))
    sin = expand(jnp.sin(ang))
    sin_signed = jnp.where(jnp.asarray((d % (2 * m)) < m)[None, :], -sin, sin)
    cos = jnp.concatenate([cos, jnp.ones((tm, LANES), F32)], axis=0)
    sin_signed = jnp.concatenate([sin_signed, jnp.zeros((tm, LANES), F32)], axis=0)
    return cos, sin_signed


def _mixer_kernel(fblk, bblk, first, last, seqb, r0t, cls,
                  sb_ref, xf_ref, xfp_ref, xfn_ref, lxb_ref, lxbp_ref, lxbn_ref,
                  scw_ref, cw_ref, cb_ref, sp_ref, wr_ref, wi_ref, br_ref, bi_ref,
                  q_ref, k_ref, v_ref, kc_ref, vc_ref, bias0_ref, bias1_ref, bias2_ref, bias3_ref,
                  zb_ref, hf_ref, hb_ref, att_ref,
                  a_s, b_s, hc_s, st_s, *, ch, n_ctx_items, band):
    it = pl.program_id(0)
    is_first = first[it] == 1
    is_last = last[it] == 1
    is_ctx = it < n_ctx_items
    b = seqb[it]
    width = BRANCH_WIDTH
    row = lax.broadcasted_iota(jnp.int32, (ch, width), 0)
    not_first = jnp.where(is_first, 0.0, 1.0).astype(F32)
    not_last = jnp.where(is_last, 0.0, 1.0).astype(F32)

    def back1(u, prev_row):
        return jnp.where(row == 0, prev_row, pltpu.roll(u, 1, 0))

    def back2(u, prev2, prev1):
        return jnp.where(row == 0, prev2, jnp.where(row == 1, prev1, pltpu.roll(u, 2, 0)))

    def fwd1(u, next_row):
        return jnp.where(row == ch - 1, next_row, pltpu.roll(u, ch - 1, 0))

    gate_cols, value_cols, lru_cols = (slice(s * width, (s + 1) * width) for s in range(3))

    def lru_input(x, p, n, prev_ok, next_ok):
        p = p * prev_ok
        n = n * next_ok
        return (cw_ref[0:1] * back2(x, p[6:7], p[7:8]) + cw_ref[1:2] * back1(x, p[7:8])
                + cw_ref[2:3] * x + cw_ref[3:4] * fwd1(x, n[0:1]) + cb_ref[...])

    def coeffs(d, xm):
        xb = xm.astype(BF16)
        r = jax.nn.sigmoid(jnp.dot(xb, wr_ref[d], preferred_element_type=F32) + br_ref[d:d + 1])
        g = jax.nn.sigmoid(jnp.dot(xb, wi_ref[d], preferred_element_type=F32) + bi_ref[d:d + 1])
        log_a = (-LRU_C * sp_ref[d:d + 1]) * r
        a = jnp.exp(log_a)
        a_s[d] = a
        b_s[d] = jnp.sqrt(-jnp.tanh(log_a) * (a * a + 1.0)) * (g * xm)

    @pl.when(jnp.logical_and(is_first, is_ctx))
    def _():
        hc_s[...] = jnp.zeros_like(hc_s)

    @pl.when(jnp.logical_and(is_first, jnp.logical_not(is_ctx)))
    def _():
        hc_s[0:1] = st_s[pl.ds(2 * b, 1), :]
        hc_s[1:2] = st_s[pl.ds(2 * b + 1, 1), :]

    coeffs(0, lru_input(xf_ref[:, lru_cols], xfp_ref[:, lru_cols], xfn_ref[:, lru_cols], not_first, not_last))
    hf = hc_s[0:1]
    for t in range(ch):
        hf = a_s[0, t:t + 1, :] * hf + b_s[0, t:t + 1, :]
        hf_ref[t:t + 1, :] = hf
    hc_s[0:1] = hf

    coeffs(1, lru_input(lxb_ref[...], lxbp_ref[...], lxbn_ref[...], not_last, not_first))

    u = xf_ref[:, gate_cols] * xf_ref[:, value_cols]
    u_prev = xfp_ref[7:8, gate_cols] * xfp_ref[7:8, value_cols] * not_first
    u_next = xfn_ref[0:1, gate_cols] * xfn_ref[0:1, value_cols] * not_last
    conv = scw_ref[0:1] * back1(u, u_prev) + scw_ref[1:2] * u + scw_ref[2:3] * fwd1(u, u_next)
    zb_ref[...] = (sb_ref[...] * conv).astype(BF16)

    hb = hc_s[1:2]
    for t in range(ch - 1, -1, -1):
        hb = a_s[1, t:t + 1, :] * hb + b_s[1, t:t + 1, :]
        hb_ref[t:t + 1, :] = hb
    hc_s[1:2] = hb

    _attention_item(it, r0t, cls, q_ref, k_ref, v_ref, kc_ref, vc_ref,
                    (bias0_ref, bias1_ref, bias2_ref, bias3_ref), att_ref, band)

    @pl.when(is_ctx)
    def _():
        st_s[pl.ds(2 * b, 1), :] = hf
        st_s[pl.ds(2 * b + 1, 1), :] = hb


def _mixer_tables(n_batch, seq, n_ctx, ch):
    assert n_ctx == ch and ch == ATT_QROWS
    nc = seq // ch
    rows = seq // GRID_W
    kr = min(NA_WIN_ROWS, rows)
    ctx0 = n_batch * seq // ch
    fblk, bblk, first, last, seqb, r0t, cls = [], [], [], [], [], [], []
    for b in range(n_batch):
        fblk.append(ctx0 + b); bblk.append(ctx0 + b); first.append(1); last.append(1); seqb.append(b)
        r0t.extend([0] * ATT_ROWS); cls.extend([kr] * ATT_ROWS)
    for b in range(n_batch):
        for c in range(nc):
            fblk.append(b * nc + c); bblk.append(b * nc + nc - 1 - c)
            first.append(int(c == 0)); last.append(int(c == nc - 1)); seqb.append(b)
            for r in range(c * ATT_ROWS, (c + 1) * ATT_ROWS):
                r0 = min(max(r - kr // 2, 0), rows - kr)
                r0t.append(r0); cls.append(r - r0)
    return [np.asarray(a, np.int32) for a in (fblk, bblk, first, last, seqb, r0t, cls)]


def _token_mixers(q, k, v, bias, sb, rest, sc_w, lru_cw, lru_cb, sp, wr_bd, wi_bd, b_r, b_i, n_batch, seq, n_ctx):
    m = sb.shape[0]
    ch = SCAN_CHUNK
    width = BRANCH_WIDTH
    tables = _mixer_tables(n_batch, seq, n_ctx, ch)
    n_items = len(tables[0])
    halo_per_chunk = ch // SUBLANES
    last_halo = m // SUBLANES - 1
    band = min(NA_WIN_ROWS, seq // GRID_W) * GRID_W
    ctx_blk0 = n_batch * seq // n_ctx

    def cur(col, which, n=1):
        return pl.BlockSpec((ch, n * width), lambda i, f, bk, *_: ((f, bk)[which][i], col))

    def prev(col, which, n=1):
        return pl.BlockSpec((SUBLANES, n * width),
                            lambda i, f, bk, *_: (jnp.maximum((f, bk)[which][i] * halo_per_chunk - 1, 0), col))

    def nxt(col, which, n=1):
        return pl.BlockSpec((SUBLANES, n * width),
                            lambda i, f, bk, *_: (jnp.minimum(((f, bk)[which][i] + 1) * halo_per_chunk, last_halo), col))

    def full(shape):
        return pl.BlockSpec(shape, lambda i, *_: (0,) * len(shape))

    in_specs = [cur(0, 0), cur(0, 0, 3), prev(0, 0, 3), nxt(0, 0, 3),
                cur(2, 1), prev(2, 1), nxt(2, 1),
                full(sc_w.shape), full(lru_cw.shape), full((1, width)), full(sp.shape),
                full(wr_bd.shape), full(wi_bd.shape), full(b_r.shape), full(b_i.shape)]
    in_specs += [cur(0, 0),
                 pl.BlockSpec((seq, width), lambda i, f, bk, fi, la, sq, *_: (sq[i], 0)),
                 pl.BlockSpec((seq, width), lambda i, f, bk, fi, la, sq, *_: (sq[i], 0)),
                 pl.BlockSpec((n_ctx, width), lambda i, f, bk, fi, la, sq, *_: (ctx_blk0 + sq[i], 0)),
                 pl.BlockSpec((n_ctx, width), lambda i, f, bk, fi, la, sq, *_: (ctx_blk0 + sq[i], 0))]
    for row in range(ATT_ROWS):
        in_specs.append(pl.BlockSpec((1, NA_HEADS, GRID_W, band),
                                     lambda i, f, bk, fi, la, sq, r0, cl, row=row: (cl[i * ATT_ROWS + row], 0, 0, 0)))
    out_specs = [cur(0, 0), cur(0, 0), cur(0, 1), cur(0, 0)]
    grid_spec = pltpu.PrefetchScalarGridSpec(
        num_scalar_prefetch=7, grid=(n_items,), in_specs=in_specs, out_specs=out_specs,
        scratch_shapes=[pltpu.VMEM((2, ch, width), F32), pltpu.VMEM((2, ch, width), F32),
                        pltpu.VMEM((SUBLANES, width), F32), pltpu.VMEM((2 * n_batch, width), F32)])
    return pl.pallas_call(
        functools.partial(_mixer_kernel, ch=ch, n_ctx_items=n_batch, band=band),
        grid_spec=grid_spec,
        out_shape=[jax.ShapeDtypeStruct((m, width), BF16),
                   jax.ShapeDtypeStruct((m, width), F32),
                   jax.ShapeDtypeStruct((m, width), F32),
                   jax.ShapeDtypeStruct((m, width), BF16)],
        compiler_params=_params(1),
        name="token_mixers",
    )(*[jnp.asarray(t) for t in tables],
      sb, rest, rest, rest, rest, rest, rest,
      sc_w, lru_cw, lru_cb.reshape(1, width), sp, wr_bd, wi_bd, b_r, b_i,
      q, k, v, k, v, bias, bias, bias, bias)


def _attention_item(it, r0t, cls, q_ref, k_ref, v_ref, kc_ref, vc_ref, bias_refs, o_ref, band):
    nq = GRID_W
    lane = lax.broadcasted_iota(jnp.int32, (nq, LANES), 1)
    low = lane < NA_HEAD_DIM
    scale = NA_HEAD_DIM ** -0.5
    nt = (((1,), (1,)), ((), ()))
    starts = [pl.multiple_of(r0t[it * ATT_ROWS + row] * GRID_W, GRID_W) for row in range(ATT_ROWS)]
    for hp in range(NA_HEADS // 2):
        cols = slice(hp * LANES, (hp + 1) * LANES)
        stacked = []
        for row in range(ATT_ROWS):
            qp = q_ref[row * nq:(row + 1) * nq, cols].astype(F32) * scale
            stacked += [jnp.where(low, qp, 0.0), jnp.where(low, 0.0, qp)]
        qs_all = jnp.concatenate(stacked, axis=0).astype(BF16)
        s_ctx_all = lax.dot_general(qs_all, kc_ref[:, cols], nt, preferred_element_type=F32)
        e_ctx_rows, e_loc_rows, dens = [], [], []
        for row in range(ATT_ROWS):
            part = slice(row * 2 * nq, (row + 1) * 2 * nq)
            kb = k_ref[pl.ds(starts[row], band), cols]
            s_loc = lax.dot_general(qs_all[part], kb, nt, preferred_element_type=F32)
            bias_ref = bias_refs[row]
            s_loc = s_loc + jnp.concatenate([bias_ref[0, 2 * hp], bias_ref[0, 2 * hp + 1]], axis=0)
            s_ctx = s_ctx_all[part]
            mx = jnp.maximum(jnp.max(s_loc, axis=-1, keepdims=True), jnp.max(s_ctx, axis=-1, keepdims=True))
            e_loc = jnp.exp(s_loc - mx)
            e_ctx = jnp.exp(s_ctx - mx)
            dens.append(jnp.sum(e_loc, axis=-1, keepdims=True) + jnp.sum(e_ctx, axis=-1, keepdims=True))
            e_loc_rows.append(e_loc.astype(BF16))
            e_ctx_rows.append(e_ctx.astype(BF16))
        o_ctx_all = jnp.dot(jnp.concatenate(e_ctx_rows, axis=0), vc_ref[:, cols], preferred_element_type=F32)
        for row in range(ATT_ROWS):
            part = slice(row * 2 * nq, (row + 1) * 2 * nq)
            vb = v_ref[pl.ds(starts[row], band), cols]
            o = (jnp.dot(e_loc_rows[row], vb, preferred_element_type=F32) + o_ctx_all[part]) / dens[row]
            o_ref[row * nq:(row + 1) * nq, cols] = jnp.where(low, o[:nq], o[nq:]).astype(BF16)


def _attention_bias(rpb, seq):
    rows = seq // GRID_W
    kr = min(NA_WIN_ROWS, rows)
    kc = NA_WIN_COLS
    cq = np.arange(GRID_W)
    c0 = np.clip(cq - kc // 2, 0, GRID_W - kc)
    ck = np.arange(GRID_W)
    inside = (ck[None, :] >= c0[:, None]) & (ck[None, :] < c0[:, None] + kc)
    dc = np.clip(ck[None, :] - cq[:, None] + (NA_WIN_COLS - 1), 0, 2 * NA_WIN_COLS - 2)
    n_dr = 2 * NA_WIN_ROWS - 1
    n_dc = 2 * NA_WIN_COLS - 1
    pick = jnp.asarray((np.arange(n_dc)[:, None] == dc.reshape(1, -1)).astype(np.float32))
    picked = jnp.dot(rpb.reshape(-1, n_dc), pick, precision=lax.Precision.HIGHEST)
    picked = picked.reshape(NA_HEADS, 2 * NA_WIN_ROWS - 1, GRID_W, GRID_W)
    table = jnp.where(jnp.asarray(inside)[None, None], picked, NEG_BIG)
    table = table.transpose(0, 2, 1, 3).reshape(NA_HEADS, GRID_W, n_dr * GRID_W)
    classes = []
    for cl in range(kr):
        lo = (NA_WIN_ROWS - 1 - cl) * GRID_W
        classes.append(table[:, :, lo:lo + kr * GRID_W])
    classes.append(jnp.full((NA_HEADS, GRID_W, kr * GRID_W), NEG_BIG, F32))
    return jnp.stack(classes, axis=0)


def _post_kernel(ha_ref, hc_ref, att_ref, zb_ref, hf_ref, hb_ref, lg_ref,
                 sh1_ref, sc1_ref, g1_ref, sh2_ref, sc2_ref,
                 wgl0_ref, wgl1_ref, wgl2_ref, bgl0_ref, bgl1_ref, bgl2_ref,
                 wpa_ref, wpc_ref, wpl_ref, wo_ref, bo_ref, l1g_ref, l1b_ref,
                 rw_ref, rb_ref, tri_ref, upper_ref,
                 h1_ref, xs_ref, pos_ref, cnt_ref, u_s, logit_s, *, split, n_tiles):
    i = pl.program_id(0)

    @pl.when(i == 0)
    def _():
        u_s[...] = jnp.zeros_like(u_s)
        logit_s[...] = jnp.zeros_like(logit_s)

    prev_u = u_s[...]
    prev_logits = logit_s[...]

    h = jnp.where(jnp.minimum(i, n_tiles - 1) < split, ha_ref[...], hc_ref[...])
    u1 = (_layer_norm(h) * (1.0 + sc1_ref[0]) + sh1_ref[0]).astype(BF16)
    y_a = jnp.dot(att_ref[...], wpa_ref[...], preferred_element_type=F32)
    y_b = jnp.dot(zb_ref[...], wpc_ref[...], preferred_element_type=F32)
    prev_pos, prev_w = _route(prev_logits, tri_ref, upper_ref, pos_ref, cnt_ref)
    zc = jax.nn.gelu(lg_ref[...]) * (hf_ref[...] + hb_ref[...])
    y_c = jnp.dot(zc.astype(BF16), wpl_ref[...], preferred_element_type=F32)
    merged = (jax.nn.sigmoid(jnp.dot(u1, wgl0_ref[0], preferred_element_type=F32) + bgl0_ref[0]) * y_a
              + jax.nn.sigmoid(jnp.dot(u1, wgl1_ref[0], preferred_element_type=F32) + bgl1_ref[0]) * y_b
              + jax.nn.sigmoid(jnp.dot(u1, wgl2_ref[0], preferred_element_type=F32) + bgl2_ref[0]) * y_c)
    y = jnp.dot(merged.astype(BF16), wo_ref[...], preferred_element_type=F32) + bo_ref[...]
    h1 = _layer_norm(DEEPNORM_ALPHA * h + g1_ref[0] * y) * l1g_ref[...] + l1b_ref[...]
    h1_ref[...] = h1
    u2 = _layer_norm(h1) * (1.0 + sc2_ref[0]) + sh2_ref[0]
    _sort_rows(prev_u, prev_pos, prev_w, xs_ref)

    u_hi = u2.astype(BF16)
    u_lo = (u2 - u_hi.astype(F32)).astype(BF16)
    by_hi = jnp.dot(u_hi, rw_ref[...], preferred_element_type=F32)
    logit_s[...] = (by_hi[:, :LANES] + by_hi[:, LANES:]
                    + jnp.dot(u_lo, rw_ref[:, :LANES], preferred_element_type=F32) + rb_ref[...])
    u_s[...] = u_hi


def _route(logits, tri_ref, upper_ref, pos_ref, cnt_ref):
    tm = logits.shape[0]
    lane = lax.broadcasted_iota(jnp.int32, (tm, LANES), 1)
    lane_f = lane.astype(F32)
    work = logits
    tops, idxs, hots = [], [], []
    for _ in range(TOP_K):
        mx = jnp.max(work, axis=-1, keepdims=True)
        idx = jnp.min(jnp.where(work == mx, lane_f, float(LANES)), axis=-1, keepdims=True)
        hot = lane_f == idx
        work = jnp.where(hot, -3e38, work)
        tops.append(mx); idxs.append(idx); hots.append(hot)
    exps = [jnp.exp(t - tops[0]) for t in tops]
    den = exps[0] + exps[1] + exps[2] + exps[3]
    hot_all = jnp.zeros((tm, LANES), F32)
    for hot in hots:
        hot_all = hot_all + hot.astype(F32)
    cnt = jnp.sum(hot_all, axis=0, keepdims=True)
    cnt_pad = jnp.floor((cnt + (SUBLANES - 1.0)) * (1.0 / SUBLANES)) * SUBLANES
    off = jnp.dot(jnp.broadcast_to(cnt_pad, (SUBLANES, LANES)), upper_ref[...],
                  precision=lax.Precision.HIGHEST, preferred_element_type=F32)[0:1]
    slot = off + jnp.dot(tri_ref[...], hot_all.astype(BF16), preferred_element_type=F32)
    pos4 = jnp.zeros((tm, LANES), F32)
    w_tile = jnp.zeros((tm, LANES), F32)
    for kk in range(TOP_K):
        pos_k = jnp.sum(jnp.where(hots[kk], slot, 0.0), axis=-1, keepdims=True)
        pos4 = jnp.where(lane == kk, pos_k, pos4)
        p = exps[kk] / den
        p_hi = p.astype(BF16).astype(F32)
        p_mid = (p - p_hi).astype(BF16).astype(F32)
        p_lo = p - p_hi - p_mid
        w_tile = jnp.where(hots[kk], p_hi, w_tile)
        w_tile = jnp.where(lane_f == idxs[kk] + float(N_EXPERTS), p_mid, w_tile)
        w_tile = jnp.where(lane_f == idxs[kk] + float(2 * N_EXPERTS), p_lo, w_tile)
    pos_ref[...] = pos4
    sub = lax.broadcasted_iota(jnp.int32, (SUBLANES, LANES), 0)
    cnt_ref[...] = jnp.where(sub == 0, cnt, jnp.where(sub == 1, off, 0.0))
    return pos4, w_tile


def _sort_rows(u_hi, pos4, w_tile, xs_ref):
    tm = pos4.shape[0]
    n_sorted = xs_ref.shape[0]
    pos_t = pos4.T
    r_iota = lax.broadcasted_iota(jnp.int32, (n_sorted, tm), 0).astype(F32)
    hit = r_iota == pos_t[0:1, :]
    for kk in range(1, TOP_K):
        hit = jnp.logical_or(hit, r_iota == pos_t[kk:kk + 1, :])
    perm = jnp.where(hit, 1.0, 0.0).astype(BF16)
    feats = jnp.concatenate([u_hi, w_tile.astype(BF16)], axis=1)
    xs_ref[...] = jnp.dot(perm, feats, preferred_element_type=F32)


def _post_mixer(h_parts, att, zb, hf, hb, rest, mod3, mod_base, layer, w_in_bf, b_in, wpa, wpc, wpl, wo, b_o, l1g,
                l1b, rw_pad, rb_pad, n_rows, n_lat, seq):
    tm = TM_POST
    width = BRANCH_WIDTH
    ha, hc = h_parts[0], h_parts[-1]
    split = min(ha.shape[0], n_rows) // tm
    gate_col0 = N_EARLY // D_MODEL
    tiles_per_seq = seq // tm
    n_lat_tiles = n_lat // tm
    n_groups_lat = n_lat // seq
    tri = jnp.asarray(np.tril(np.ones((tm, tm), np.float32), -1), BF16)
    upper = jnp.asarray(np.triu(np.ones((LANES, LANES), np.float32), 1))
    n_tiles = n_rows // tm

    def dense(i):
        return jnp.minimum(i, n_tiles - 1)

    def routed(i):
        return jnp.maximum(i - 1, 0)

    def group(i):
        return jnp.where(dense(i) < n_lat_tiles, dense(i) // tiles_per_seq, n_groups_lat)

    def rows(wd, col=0):
        return pl.BlockSpec((tm, wd), lambda i: (dense(i), col))

    def full(shape):
        return pl.BlockSpec(shape, lambda i: (0,) * len(shape))

    def mod(which):
        return pl.BlockSpec((1, 1, D_MODEL), lambda i: (mod_base + group(i) * 6 + which, 0, 0))

    in_specs = [pl.BlockSpec((tm, D_MODEL), lambda i: (jnp.minimum(dense(i), split - 1), 0)),
                pl.BlockSpec((tm, D_MODEL), lambda i: (jnp.maximum(dense(i) - split, 0), 0)),
                rows(width), rows(width), rows(width), rows(width), rows(width, 3),
                mod(0), mod(1), mod(2), mod(3), mod(4)]
    in_specs += [pl.BlockSpec((1, D_MODEL, D_MODEL), lambda i, c=c: (layer, 0, gate_col0 + c)) for c in range(3)]
    in_specs += [pl.BlockSpec((1, 1, D_MODEL), lambda i, c=c: (layer, 0, gate_col0 + c)) for c in range(3)]
    in_specs += [full(wpa.shape), full(wpc.shape), full(wpl.shape), full(wo.shape), full((1, D_MODEL)),
                full((1, D_MODEL)), full((1, D_MODEL)), full(rw_pad.shape), full(rb_pad.shape), full(tri.shape),
                full(upper.shape)]
    out_specs = [rows(D_MODEL), pl.BlockSpec((TOK_BLOCK, XS_WIDTH), lambda i: (routed(i), 0)),
                 pl.BlockSpec((tm, LANES), lambda i: (routed(i), 0)),
                 pl.BlockSpec((SUBLANES, LANES), lambda i: (routed(i), 0))]
    return pl.pallas_call(
        functools.partial(_post_kernel, split=split, n_tiles=n_tiles),
        grid=(n_tiles + 1,),
        in_specs=in_specs, out_specs=out_specs,
        out_shape=[jax.ShapeDtypeStruct((n_rows, D_MODEL), F32),
                   jax.ShapeDtypeStruct((n_tiles * TOK_BLOCK, XS_WIDTH), F32),
                   jax.ShapeDtypeStruct((n_rows, LANES), F32),
                   jax.ShapeDtypeStruct((n_tiles * SUBLANES, LANES), F32)],
        scratch_shapes=[pltpu.VMEM((tm, D_MODEL), BF16), pltpu.VMEM((tm, LANES), F32)],
        compiler_params=_params(1),
        name="post_mixer",
    )(ha, hc, att, zb, hf, hb, rest, mod3, mod3, mod3, mod3, mod3,
      w_in_bf, w_in_bf, w_in_bf, b_in.reshape(DEPTH, 1, -1), b_in.reshape(DEPTH, 1, -1), b_in.reshape(DEPTH, 1, -1),
      wpa, wpc, wpl, wo, b_o.reshape(1, D_MODEL), l1g.reshape(1, D_MODEL), l1b.reshape(1, D_MODEL),
      rw_pad, rb_pad, tri, upper)


def _expert_plan(cnt_out, n_tok_tiles, n_rows):
    tm = TM_EXPERT
    co = cnt_out.reshape(n_tok_tiles, SUBLANES, LANES)
    cnt = (co[:, 0, :N_EXPERTS].astype(jnp.int32) + SUBLANES - 1) // SUBLANES * SUBLANES
    off = co[:, 1, :N_EXPERTS].astype(jnp.int32)
    cum_end = jnp.cumsum(cnt, axis=0)
    cum = cum_end - cnt
    total = cum_end[-1]
    n_et = (total + tm - 1) // tm
    et_end = jnp.cumsum(n_et)
    n_act = et_end[-1:].astype(jnp.int32)
    n_tiles = -(-n_tok_tiles * TOK_BLOCK // tm) + N_EXPERTS
    j = jnp.arange(n_tiles, dtype=jnp.int32)
    tile_e = jnp.minimum(jnp.sum((et_end[None, :] <= j[:, None]).astype(jnp.int32), axis=1), N_EXPERTS - 1)
    pick_e = (tile_e[:, None] == jnp.arange(N_EXPERTS, dtype=jnp.int32)[None, :]).astype(F32)

    def per_tile(table):
        return jnp.dot(pick_e, table.astype(F32), precision=lax.Precision.HIGHEST).astype(jnp.int32)

    row0 = (j - per_tile(et_end - n_et)) * tm
    n_rows_tile = jnp.clip(per_tile(total) - row0, 0, tm)
    cum_e = per_tile(cum.T)
    cum_end_e = per_tile(cum_end.T)
    delta_e = per_tile((jnp.arange(n_tok_tiles, dtype=jnp.int32)[:, None] * TOK_BLOCK + off - cum).T)
    q = row0[:, None] + SUBLANES * jnp.arange(tm // SUBLANES, dtype=jnp.int32)[None, :]
    inside = jnp.logical_and(cum_e.T[:, :, None] <= q[None], q[None] < cum_end_e.T[:, :, None])
    src = q + jnp.sum(jnp.where(inside, delta_e.T[:, :, None], 0), axis=0)
    used = off[:, N_EXPERTS - 1] + cnt[:, N_EXPERTS - 1]
    first = jnp.concatenate([jnp.ones((1,), jnp.int32), (tile_e[1:] != tile_e[:-1]).astype(jnp.int32)])
    group = jnp.cumsum(first) - 1
    after = per_tile(et_end)
    next_e = jnp.where(after < n_act[0],
                       jnp.minimum(jnp.sum((et_end[None, :] <= after[:, None]).astype(jnp.int32), axis=1),
                                   N_EXPERTS - 1), -1)
    return (tile_e.astype(jnp.int32), n_rows_tile.astype(jnp.int32), n_act, src.reshape(-1).astype(jnp.int32),
            used.astype(jnp.int32), first, group.astype(jnp.int32), next_e.astype(jnp.int32), n_tiles)


def _expert_kernel(tile_e, n_rows_t, n_act, src_t, used_t, first_t, group_t, next_t,
                   xs_hbm, wgu_hbm, bgu_ref, wdn_hbm, bdn_ref, ys_hbm,
                   xin, yout, wgu_f, wdn_f, wgu_s, wdn_s, zeros, sem_in, sem_out, sem_zero, sem_wgu, sem_wdn,
                   *, tm, tok_block, n_tok_tiles, layer):
    j = pl.program_id(0)
    na = n_act[0]
    chunks = tm // SUBLANES

    def weight_copies(e, wslot):
        return (pltpu.make_async_copy(wgu_hbm.at[layer, e], wgu_f.at[wslot], sem_wgu.at[wslot]),
                pltpu.make_async_copy(wdn_hbm.at[layer, e], wdn_f.at[wslot], sem_wdn.at[wslot]))

    def gather(src, dst, size, slot):
        return pltpu.make_async_copy(xs_hbm.at[pl.ds(src, size)], xin.at[slot, pl.ds(dst, size)], sem_in.at[slot])

    def scatter(src, dst, size, slot):
        return pltpu.make_async_copy(yout.at[slot, pl.ds(dst, size)], ys_hbm.at[pl.ds(src, size)], sem_out.at[slot])

    def start_chunks(jj, slot, copy):
        def one(c, priority):
            src = pl.multiple_of(src_t[jj * chunks + c], SUBLANES)
            copy(src, pl.multiple_of(c * SUBLANES, SUBLANES), SUBLANES, slot).start(priority=priority)

        def body(c, carry):
            one(c, 0)
            return carry

        def body_unrolled(g, carry):
            for u in range(SUBLANES):
                one(g * SUBLANES + u, u % 2)
            return carry

        n = lax.shift_right_logical(n_rows_t[jj], 3)

        @pl.when(n == chunks)
        def _():
            lax.fori_loop(0, chunks // SUBLANES, body_unrolled, 0)

        @pl.when(n != chunks)
        def _():
            lax.fori_loop(0, n, body, 0)

    def wait_rows(jj, slot, copy):
        n = n_rows_t[jj]
        size = tm
        while size >= SUBLANES:
            @pl.when((n & size) != 0)
            def _(size=size):
                copy(0, 0, size, slot).wait()
            size //= 2

    slot = j % 2
    in_slot = j % GATHER_DEPTH

    @pl.when(j == 0)
    def _():
        xin[...] = jnp.zeros_like(xin)
        start_chunks(0, 0, gather)

        @pl.when(na > 1)
        def _():
            start_chunks(1, 1, gather)
        zeros[...] = jnp.zeros_like(zeros)

        def clear_tail(i, copy_op):
            used = used_t[i]
            tail = tok_block - used
            size = ZERO_ROWS
            sizes = []
            while size >= SUBLANES:
                sizes.append(size)
                size //= 2
            for size in sizes:
                @pl.when((tail & size) != 0)
                def _(size=size):
                    at = pl.multiple_of(i * tok_block + used + (tail & ~(2 * size - 1)), SUBLANES)
                    copy_op(pltpu.make_async_copy(zeros.at[pl.ds(0, size)], ys_hbm.at[pl.ds(at, size)], sem_zero))

        def start_clear(i, c):
            clear_tail(i, lambda cp: cp.start())
            return c

        def wait_clear(i, c):
            clear_tail(i, lambda cp: cp.wait())
            return c

        lax.fori_loop(0, n_tok_tiles, start_clear, 0)
        lax.fori_loop(0, n_tok_tiles, wait_clear, 0)

    @pl.when(j + GATHER_DEPTH - 1 < na)
    def _():
        start_chunks(j + GATHER_DEPTH - 1, (j + GATHER_DEPTH - 1) % GATHER_DEPTH, gather)

    @pl.when(j < na)
    def _():
        e = tile_e[j]

        @pl.when(first_t[j] == 1)
        def _():
            wslot = group_t[j] % 2

            @pl.when(j == 0)
            def _():
                for cp in weight_copies(e, wslot):
                    cp.start()

            for cp in weight_copies(e, wslot):
                cp.wait()
            wgu_s[...] = wgu_f[wslot].astype(BF16)
            wdn_s[...] = wdn_f[wslot].astype(BF16)

            @pl.when(next_t[j] >= 0)
            def _():
                for cp in weight_copies(next_t[j], 1 - wslot):
                    cp.start()

        wait_rows(j, in_slot, gather)

        def compute(rows):
            x = xin[in_slot, pl.ds(0, rows)]
            lane = lax.broadcasted_iota(jnp.int32, (rows, LANES), 1)
            p = jnp.sum(jnp.where(lane % N_EXPERTS == e, x[:, D_MODEL:], 0.0), axis=-1, keepdims=True)
            gu = (jnp.dot(x[:, :D_MODEL].astype(BF16), wgu_s[...], preferred_element_type=F32)
                  + bgu_ref[0, pl.ds(e, 1), :])
            f = gu.shape[1] // 2
            gate = jnp.minimum(gu[:, :f], SWIGLU_LIMIT)
            up = jnp.clip(gu[:, f:], -SWIGLU_LIMIT, SWIGLU_LIMIT)
            hid = (up + 1.0) * gate * jax.nn.sigmoid(SWIGLU_ALPHA * gate)
            y = (jnp.dot(hid.astype(BF16), wdn_s[...], preferred_element_type=F32)
                 + bdn_ref[0, pl.ds(e, 1), :])
            yout[slot, pl.ds(0, rows)] = y * p

        quarter = tm // 4
        for part in range(1, 5):
            @pl.when(jnp.logical_and(n_rows_t[j] > (part - 1) * quarter, n_rows_t[j] <= part * quarter))
            def _(part=part):
                compute(part * quarter)

        start_chunks(j, slot, scatter)

        @pl.when(j >= 1)
        def _():
            wait_rows(j - 1, 1 - slot, scatter)

        @pl.when(j == na - 1)
        def _():
            wait_rows(j, slot, scatter)


def _experts(xs, plan, layer, w_gu, b_gu, w_dn, b_dn):
    tile_e, n_rows_tile, n_act, src, used, first, group, next_e, n_tiles = plan
    tm = TM_EXPERT
    f2 = w_gu.shape[-1]

    in_specs = [pl.BlockSpec(memory_space=pl.ANY),
                pl.BlockSpec(memory_space=pl.ANY),
                pl.BlockSpec((1, N_EXPERTS, f2), lambda j, *_: (layer, 0, 0)),
                pl.BlockSpec(memory_space=pl.ANY),
                pl.BlockSpec((1, N_EXPERTS, D_MODEL), lambda j, *_: (layer, 0, 0))]
    grid_spec = pltpu.PrefetchScalarGridSpec(
        num_scalar_prefetch=8, grid=(n_tiles,), in_specs=in_specs,
        out_specs=pl.BlockSpec(memory_space=pl.ANY),
        scratch_shapes=[pltpu.VMEM((GATHER_DEPTH, tm, XS_WIDTH), F32), pltpu.VMEM((2, tm, D_MODEL), F32),
                        pltpu.VMEM((2, D_MODEL, f2), F32), pltpu.VMEM((2, f2 // 2, D_MODEL), F32),
                        pltpu.VMEM((D_MODEL, f2), BF16), pltpu.VMEM((f2 // 2, D_MODEL), BF16),
                        pltpu.VMEM((ZERO_ROWS, D_MODEL), F32),
                        pltpu.SemaphoreType.DMA((GATHER_DEPTH,)), pltpu.SemaphoreType.DMA((2,)),
                        pltpu.SemaphoreType.DMA(()),
                        pltpu.SemaphoreType.DMA((2,)), pltpu.SemaphoreType.DMA((2,))])
    return pl.pallas_call(
        functools.partial(_expert_kernel, tm=tm, tok_block=TOK_BLOCK, n_tok_tiles=xs.shape[0] // TOK_BLOCK,
                          layer=layer),
        grid_spec=grid_spec,
        out_shape=jax.ShapeDtypeStruct((xs.shape[0], D_MODEL), F32),
        compiler_params=_params(1),
        name="moe_experts",
    )(tile_e, n_rows_tile, n_act, src, used, first, group, next_e,
      xs, w_gu, b_gu, w_dn, b_dn)


def _combine_kernel(ys_ref, pos_ref, h1_ref, g2_ref, l2g_ref, l2b_ref, o_ref):
    tm = pos_ref.shape[0]
    n_sorted = ys_ref.shape[0]
    pos = pos_ref[...]
    col = lax.broadcasted_iota(jnp.int32, (tm, tm), 1).astype(F32)
    y2 = None
    for r0 in range(0, n_sorted, tm):
        ys = ys_ref[r0:r0 + tm]
        sel_b = jnp.zeros((tm, tm), F32)
        for kk in range(TOP_K):
            sel_b = sel_b + (col == pos[:, kk:kk + 1] - float(r0)).astype(F32)
        sel_b = sel_b.astype(BF16)
        hi = ys.astype(BF16)
        rest = ys - hi.astype(F32)
        mid = rest.astype(BF16)
        lo = (rest - mid.astype(F32)).astype(BF16)
        part = (jnp.dot(sel_b, hi, preferred_element_type=F32) + jnp.dot(sel_b, mid, preferred_element_type=F32)
                + jnp.dot(sel_b, lo, preferred_element_type=F32))
        y2 = part if y2 is None else y2 + part
    o_ref[...] = _layer_norm(DEEPNORM_ALPHA * h1_ref[...] + g2_ref[0] * y2) * l2g_ref[...] + l2b_ref[...]


def _combine(ys, pos4, h1, mod3, mod_base, l2g, l2b, n_lat, seq):
    n_rows = h1.shape[0]
    tm = TM_POST
    tiles_per_seq = seq // tm
    n_lat_tiles = n_lat // tm
    n_groups_lat = n_lat // seq

    def group(i):
        return jnp.where(i < n_lat_tiles, i // tiles_per_seq, n_groups_lat)

    in_specs = [pl.BlockSpec((TOK_BLOCK, D_MODEL), lambda i: (i, 0)),
                pl.BlockSpec((tm, LANES), lambda i: (i, 0)),
                pl.BlockSpec((tm, D_MODEL), lambda i: (i, 0)),
                pl.BlockSpec((1, 1, D_MODEL), lambda i: (mod_base + group(i) * 6 + 5, 0, 0)),
                pl.BlockSpec((1, D_MODEL), lambda i: (0, 0)),
                pl.BlockSpec((1, D_MODEL), lambda i: (0, 0))]
    return pl.pallas_call(
        _combine_kernel,
        grid=(n_rows // tm,),
        in_specs=in_specs,
        out_specs=pl.BlockSpec((tm, D_MODEL), lambda i: (i, 0)),
        out_shape=jax.ShapeDtypeStruct((n_rows, D_MODEL), F32),
        compiler_params=_params(1),
        name="moe_combine",
    )(ys, pos4, h1, mod3, l2g.reshape(1, D_MODEL), l2b.reshape(1, D_MODEL))


def _block_diag(w):
    two, n, d, e = w.shape
    eye = jnp.eye(n, dtype=w.dtype)
    return (w[:, :, :, None, :] * eye[None, :, None, :, None]).reshape(two, n * d, n * e)


def kernel(x, c, ctx, c_ctx, w_mod, b_mod, w_in, b_in, na_rpb, w_proj_attn, w_proj_conv, w_proj_lru, sc_conv_w, lru_conv_w, lru_conv_b, lru_lambda, lru_w_r, lru_b_r, lru_w_i, lru_b_i, w_o, b_o, ln1_g, ln1_b, router_w, router_b, exp_w_gu, exp_b_gu, exp_w_dn, exp_b_dn, ln2_g, ln2_b):
    n_batch, seq, d = x.shape
    n_ctx = ctx.shape[1]
    n_lat = n_batch * seq
    n_all = n_lat + n_batch * n_ctx
    assert d == D_MODEL and n_batch + 1 <= SUBLANES

    cc = jnp.concatenate([c, c_ctx[None], jnp.zeros((SUBLANES - n_batch - 1, d), F32)], axis=0)
    mod = _modulation(cc, w_mod, b_mod)
    groups = n_batch + 1
    mod3 = mod.reshape(DEPTH, SUBLANES, 6, d)[:, :groups].reshape(DEPTH * groups * 6, 1, d)

    cos_t, sin_t = _make_rope(seq, TM_INPROJ)
    h = (x.reshape(n_lat, d), ctx.reshape(n_batch * n_ctx, d))
    w_in_bf = w_in.astype(BF16)

    for layer in range(DEPTH):
        last = layer == DEPTH - 1
        mod_base = layer * groups * 6
        q, k, v, sb, rest = _input_projection(h, mod3, mod_base, layer, w_in_bf, b_in, N_EARLY,
                                              cos_t, sin_t, n_lat, seq)
        sp = jax.nn.softplus(-lru_lambda[layer])
        zb, hf, hb, att = _token_mixers(
            q, k, v, _attention_bias(na_rpb[layer], seq), sb, rest, sc_conv_w[layer], lru_conv_w[layer],
            lru_conv_b[layer], sp, _block_diag(lru_w_r[layer]).astype(BF16), _block_diag(lru_w_i[layer]).astype(BF16),
            lru_b_r[layer], lru_b_i[layer], n_batch, seq, n_ctx)
        n_rows = n_lat if last else n_all
        rw_full = jnp.pad(router_w[layer], ((0, 0), (0, LANES - N_EXPERTS)))
        rw_hi = rw_full.astype(BF16)
        rw_pad = jnp.concatenate([rw_hi, (rw_full - rw_hi.astype(F32)).astype(BF16)], axis=1)
        rb_pad = jnp.concatenate([router_b[layer], jnp.full((LANES - N_EXPERTS,), NEG_BIG, F32)]).reshape(1, LANES)
        h1, xs, pos4, cnt_out = _post_mixer(
            h, att, zb, hf, hb, rest, mod3, mod_base, layer, w_in_bf, b_in,
            w_proj_attn[layer].astype(BF16), w_proj_conv[layer].astype(BF16), w_proj_lru[layer].astype(BF16),
            w_o[layer].astype(BF16), b_o[layer], ln1_g[layer], ln1_b[layer], rw_pad, rb_pad, n_rows, n_lat, seq)
        plan = _expert_plan(cnt_out, n_rows // TM_POST, n_rows)
        ys = _experts(xs, plan, layer, exp_w_gu, exp_b_gu, exp_w_dn, exp_b_dn)
        h = (_combine(ys, pos4, h1, mod3, mod_base, ln2_g[layer], ln2_b[layer], n_lat, seq),)
    return h[0].reshape(n_batch, seq, d)
```

```python
import functools

import numpy as np
import jax
import jax.numpy as jnp
from jax import lax
from jax.experimental import pallas as pl
from jax.experimental.pallas import tpu as pltpu

D_MODEL = 1024
DEPTH = 2
GRID_W = 64
NA_HEADS = 8
NA_HEAD_DIM = 64
NA_WIN_ROWS = 8
NA_WIN_COLS = 16
ROPE_BASE = 10000.0
BRANCH_WIDTH = 512
LRU_C = 8.0
N_EARLY = 8 * BRANCH_WIDTH
N_EXPERTS = 32
TOP_K = 4
SWIGLU_LIMIT = 7.0
SWIGLU_ALPHA = 1.702
LN_EPS = 1e-5
DEEPNORM_ALPHA = (2 * DEPTH) ** 0.25
NEG_BIG = -1e30

LANES = 128
SUBLANES = 8
VMEM_LIMIT_BYTES = 56 * 1024 * 1024

TM_INPROJ = 512
SCAN_CHUNK = 256
ATT_ROWS = 4
ATT_QROWS = ATT_ROWS * GRID_W
TM_POST = 256
TM_EXPERT = 512
GATHER_DEPTH = 3
WEIGHT_DMA_PIECES = 4
XS_WIDTH = D_MODEL + LANES
TOK_BLOCK = TM_POST * TOP_K + N_EXPERTS * SUBLANES
ZERO_ROWS = TOK_BLOCK - TM_POST * TOP_K

F32 = jnp.float32
BF16 = jnp.bfloat16


def _params(n_axes):
    return pltpu.CompilerParams(dimension_semantics=("arbitrary",) * n_axes,
                                vmem_limit_bytes=VMEM_LIMIT_BYTES)


def _layer_norm(x):
    mu = jnp.mean(x, axis=-1, keepdims=True)
    xc = x - mu
    var = jnp.mean(xc * xc, axis=-1, keepdims=True)
    return xc * lax.rsqrt(var + LN_EPS)


def _mod_kernel(c_ref, w_ref, b_ref, o_ref):
    c = c_ref[...]
    s = (c * jax.nn.sigmoid(c)).astype(BF16)
    o_ref[0] = jnp.dot(s, w_ref[0].astype(BF16), preferred_element_type=F32) + b_ref[0]


def _modulation(cc, w_mod, b_mod):
    n_out = w_mod.shape[-1]
    return pl.pallas_call(
        _mod_kernel,
        grid=(DEPTH, n_out // D_MODEL),
        in_specs=[pl.BlockSpec((SUBLANES, D_MODEL), lambda l, j: (0, 0)),
                  pl.BlockSpec((1, D_MODEL, D_MODEL), lambda l, j: (l, 0, j)),
                  pl.BlockSpec((1, 1, D_MODEL), lambda l, j: (l, 0, j))],
        out_specs=pl.BlockSpec((1, SUBLANES, D_MODEL), lambda l, j: (l, 0, j)),
        out_shape=jax.ShapeDtypeStruct((DEPTH, SUBLANES, n_out), F32),
        compiler_params=_params(2),
        name="modulation",
    )(cc, w_mod, b_mod.reshape(DEPTH, 1, n_out))


def _rope_half(x, cos, sin_signed):
    m = NA_HEAD_DIM // 4
    lane = lax.broadcasted_iota(jnp.int32, (x.shape[0], LANES), 1)
    first = (lane % (2 * m)) < m
    outs = []
    for cidx in range(x.shape[1] // LANES):
        xc = x[:, cidx * LANES:(cidx + 1) * LANES]
        partner = jnp.where(first, pltpu.roll(xc, LANES - m, 1), pltpu.roll(xc, m, 1))
        outs.append(xc * cos + partner * sin_signed)
    return jnp.concatenate(outs, axis=1)


def _inproj_kernel(ha_ref, hb_ref, sh_ref, sc_ref, w_ref, b_ref, cos_ref, sin_ref,
                   q_ref, k_ref, v_ref, sb_ref, rest_ref, *, split):
    h = jnp.where(pl.program_id(0) < split, ha_ref[...], hb_ref[...])
    xn = (_layer_norm(h) * (1.0 + sc_ref[0]) + sh_ref[0]).astype(BF16)
    half = BRANCH_WIDTH

    def columns(lo, width):
        return jnp.dot(xn, w_ref[0, :, lo:lo + width], preferred_element_type=F32) + b_ref[0, :, lo:lo + width]

    cos = cos_ref[...]
    sin = sin_ref[...]
    q_ref[...] = _rope_half(columns(0, half), cos, sin).astype(BF16)
    k_ref[...] = _rope_half(columns(half, half), cos, sin).astype(BF16)
    v_ref[...] = columns(2 * half, half).astype(BF16)
    sb_ref[...] = columns(3 * half, half)
    n_rest = rest_ref.shape[1]
    for lo in range(0, n_rest, 2 * half):
        rest_ref[:, lo:lo + 2 * half] = columns(4 * half + lo, 2 * half)


def _input_projection(h_parts, mod3, mod_base, layer, w_in_bf, b_in, n_cols, cos_t, sin_t, n_lat, seq):
    tm = TM_INPROJ
    ha, hb = h_parts[0], h_parts[-1]
    split = ha.shape[0] // tm
    m = ha.shape[0] + (hb.shape[0] if len(h_parts) == 2 else 0)
    n_lat_tiles = n_lat // tm
    tiles_per_seq = seq // tm
    n_groups_lat = n_lat // seq

    def group(i):
        return jnp.where(i < n_lat_tiles, i // tiles_per_seq, n_groups_lat)

    def rope_blk(i):
        return jnp.where(i < n_lat_tiles, i % tiles_per_seq, tiles_per_seq)

    half = BRANCH_WIDTH
    return pl.pallas_call(
        functools.partial(_inproj_kernel, split=split),
        grid=(m // tm,),
        in_specs=[pl.BlockSpec((tm, D_MODEL), lambda i: (jnp.minimum(i, split - 1), 0)),
                  pl.BlockSpec((tm, D_MODEL), lambda i: (jnp.maximum(i - split, 0), 0)),
                  pl.BlockSpec((1, 1, D_MODEL), lambda i: (mod_base + group(i) * 6 + 0, 0, 0)),
                  pl.BlockSpec((1, 1, D_MODEL), lambda i: (mod_base + group(i) * 6 + 1, 0, 0)),
                  pl.BlockSpec((1, D_MODEL, n_cols), lambda i: (layer, 0, 0)),
                  pl.BlockSpec((1, 1, n_cols), lambda i: (layer, 0, 0)),
                  pl.BlockSpec((tm, LANES), lambda i: (rope_blk(i), 0)),
                  pl.BlockSpec((tm, LANES), lambda i: (rope_blk(i), 0))],
        out_specs=[pl.BlockSpec((tm, half), lambda i: (i, 0)),
                   pl.BlockSpec((tm, half), lambda i: (i, 0)),
                   pl.BlockSpec((tm, half), lambda i: (i, 0)),
                   pl.BlockSpec((tm, half), lambda i: (i, 0)),
                   pl.BlockSpec((tm, n_cols - 4 * half), lambda i: (i, 0))],
        out_shape=[jax.ShapeDtypeStruct((m, half), BF16),
                   jax.ShapeDtypeStruct((m, half), BF16),
                   jax.ShapeDtypeStruct((m, half), BF16),
                   jax.ShapeDtypeStruct((m, half), F32),
                   jax.ShapeDtypeStruct((m, n_cols - 4 * half), F32)],
        compiler_params=_params(1),
        name="input_projection",
    )(ha, hb, mod3, mod3, w_in_bf, b_in.reshape(DEPTH, 1, -1), cos_t, sin_t)


def _make_rope(seq, tm):
    rows = seq // GRID_W
    d = np.arange(LANES) % NA_HEAD_DIM
    m = NA_HEAD_DIM // 4
    inv_freq = (ROPE_BASE ** (-jnp.arange(m, dtype=F32) / m))[d % m]
    n_pos = max(rows, GRID_W)
    ang = jnp.arange(n_pos, dtype=F32)[:, None] * inv_freq[None, :]
    by_row = jnp.asarray(d < 2 * m)[None, None, :]

    def expand(tab):
        full = jnp.where(by_row, tab[:rows, None, :], tab[None, :GRID_W, :])
        return full.reshape(seq, LANES)

    cos = expand(jnp.cos(ang))
    sin = expand(jnp.sin(ang))
    sin_signed = jnp.where(jnp.asarray((d % (2 * m)) < m)[None, :], -sin, sin)
    cos = jnp.concatenate([cos, jnp.ones((tm, LANES), F32)], axis=0)
    sin_signed = jnp.concatenate([sin_signed, jnp.zeros((tm, LANES), F32)], axis=0)
    return cos, sin_signed


def _mixer_kernel(fblk, bblk, first, last, seqb, r0t, cls,
                  sb_ref, xf_ref, xfp_ref, xfn_ref, lxb_ref, lxbp_ref, lxbn_ref,
                  scw_ref, cw_ref, cb_ref, sp_ref, wr_ref, wi_ref, br_ref, bi_ref,
                  q_ref, k_ref, v_ref, kc_ref, vc_ref, bias0_ref, bias1_ref, bias2_ref, bias3_ref,
                  zb_ref, hf_ref, hb_ref, att_ref,
                  a_s, b_s, hc_s, st_s, *, ch, n_ctx_items, band):
    it = pl.program_id(0)
    is_first = first[it] == 1
    is_last = last[it] == 1
    is_ctx = it < n_ctx_items
    b = seqb[it]
    width = BRANCH_WIDTH
    row = lax.broadcasted_iota(jnp.int32, (ch, width), 0)
    not_first = jnp.where(is_first, 0.0, 1.0).astype(F32)
    not_last = jnp.where(is_last, 0.0, 1.0).astype(F32)

    def back1(u, prev_row):
        return jnp.where(row == 0, prev_row, pltpu.roll(u, 1, 0))

    def back2(u, prev2, prev1):
        return jnp.where(row == 0, prev2, jnp.where(row == 1, prev1, pltpu.roll(u, 2, 0)))

    def fwd1(u, next_row):
        return jnp.where(row == ch - 1, next_row, pltpu.roll(u, ch - 1, 0))

    gate_cols, value_cols, lru_cols = (slice(s * width, (s + 1) * width) for s in range(3))

    def lru_input(x, p, n, prev_ok, next_ok):
        p = p * prev_ok
        n = n * next_ok
        return (cw_ref[0:1] * back2(x, p[6:7], p[7:8]) + cw_ref[1:2] * back1(x, p[7:8])
                + cw_ref[2:3] * x + cw_ref[3:4] * fwd1(x, n[0:1]) + cb_ref[...])

    def coeffs(d, xm):
        xb = xm.astype(BF16)
        r = jax.nn.sigmoid(jnp.dot(xb, wr_ref[d], preferred_element_type=F32) + br_ref[d:d + 1])
        g = jax.nn.sigmoid(jnp.dot(xb, wi_ref[d], preferred_element_type=F32) + bi_ref[d:d + 1])
        log_a = (-LRU_C * sp_ref[d:d + 1]) * r
        a = jnp.exp(log_a)
        a_s[d] = a
        b_s[d] = jnp.sqrt(-jnp.tanh(log_a) * (a * a + 1.0)) * (g * xm)

    @pl.when(jnp.logical_and(is_first, is_ctx))
    def _():
        hc_s[...] = jnp.zeros_like(hc_s)

    @pl.when(jnp.logical_and(is_first, jnp.logical_not(is_ctx)))
    def _():
        hc_s[0:1] = st_s[pl.ds(2 * b, 1), :]
        hc_s[1:2] = st_s[pl.ds(2 * b + 1, 1), :]

    coeffs(0, lru_input(xf_ref[:, lru_cols], xfp_ref[:, lru_cols], xfn_ref[:, lru_cols], not_first, not_last))
    hf = hc_s[0:1]
    for t in range(ch):
        hf = a_s[0, t:t + 1, :] * hf + b_s[0, t:t + 1, :]
        hf_ref[t:t + 1, :] = hf
    hc_s[0:1] = hf

    coeffs(1, lru_input(lxb_ref[...], lxbp_ref[...], lxbn_ref[...], not_last, not_first))

    u = xf_ref[:, gate_cols] * xf_ref[:, value_cols]
    u_prev = xfp_ref[7:8, gate_cols] * xfp_ref[7:8, value_cols] * not_first
    u_next = xfn_ref[0:1, gate_cols] * xfn_ref[0:1, value_cols] * not_last
    conv = scw_ref[0:1] * back1(u, u_prev) + scw_ref[1:2] * u + scw_ref[2:3] * fwd1(u, u_next)
    zb_ref[...] = (sb_ref[...] * conv).astype(BF16)

    hb = hc_s[1:2]
    for t in range(ch - 1, -1, -1):
        hb = a_s[1, t:t + 1, :] * hb + b_s[1, t:t + 1, :]
        hb_ref[t:t + 1, :] = hb
    hc_s[1:2] = hb

    _attention_item(it, r0t, cls, q_ref, k_ref, v_ref, kc_ref, vc_ref,
                    (bias0_ref, bias1_ref, bias2_ref, bias3_ref), att_ref, band)

    @pl.when(is_ctx)
    def _():
        st_s[pl.ds(2 * b, 1), :] = hf
        st_s[pl.ds(2 * b + 1, 1), :] = hb


def _mixer_tables(n_batch, seq, n_ctx, ch):
    assert n_ctx == ch and ch == ATT_QROWS
    nc = seq // ch
    rows = seq // GRID_W
    kr = min(NA_WIN_ROWS, rows)
    ctx0 = n_batch * seq // ch
    fblk, bblk, first, last, seqb, r0t, cls = [], [], [], [], [], [], []
    for b in range(n_batch):
        fblk.append(ctx0 + b); bblk.append(ctx0 + b); first.append(1); last.append(1); seqb.append(b)
        r0t.extend([0] * ATT_ROWS); cls.extend([kr] * ATT_ROWS)
    for b in range(n_batch):
        for c in range(nc):
            fblk.append(b * nc + c); bblk.append(b * nc + nc - 1 - c)
            first.append(int(c == 0)); last.append(int(c == nc - 1)); seqb.append(b)
            for r in range(c * ATT_ROWS, (c + 1) * ATT_ROWS):
                r0 = min(max(r - kr // 2, 0), rows - kr)
                r0t.append(r0); cls.append(r - r0)
    return [np.asarray(a, np.int32) for a in (fblk, bblk, first, last, seqb, r0t, cls)]


def _token_mixers(q, k, v, bias, sb, rest, sc_w, lru_cw, lru_cb, sp, wr_bd, wi_bd, b_r, b_i, n_batch, seq, n_ctx):
    m = sb.shape[0]
    ch = SCAN_CHUNK
    width = BRANCH_WIDTH
    tables = _mixer_tables(n_batch, seq, n_ctx, ch)
    n_items = len(tables[0])
    halo_per_chunk = ch // SUBLANES
    last_halo = m // SUBLANES - 1
    band = min(NA_WIN_ROWS, seq // GRID_W) * GRID_W
    ctx_blk0 = n_batch * seq // n_ctx

    def cur(col, which, n=1):
        return pl.BlockSpec((ch, n * width), lambda i, f, bk, *_: ((f, bk)[which][i], col))

    def prev(col, which, n=1):
        return pl.BlockSpec((SUBLANES, n * width),
                            lambda i, f, bk, *_: (jnp.maximum((f, bk)[which][i] * halo_per_chunk - 1, 0), col))

    def nxt(col, which, n=1):
        return pl.BlockSpec((SUBLANES, n * width),
                            lambda i, f, bk, *_: (jnp.minimum(((f, bk)[which][i] + 1) * halo_per_chunk, last_halo), col))

    def full(shape):
        return pl.BlockSpec(shape, lambda i, *_: (0,) * len(shape))

    in_specs = [cur(0, 0), cur(0, 0, 3), prev(0, 0, 3), nxt(0, 0, 3),
                cur(2, 1), prev(2, 1), nxt(2, 1),
                full(sc_w.shape), full(lru_cw.shape), full((1, width)), full(sp.shape),
                full(wr_bd.shape), full(wi_bd.shape), full(b_r.shape), full(b_i.shape)]
    in_specs += [cur(0, 0),
                 pl.BlockSpec((seq, width), lambda i, f, bk, fi, la, sq, *_: (sq[i], 0)),
                 pl.BlockSpec((seq, width), lambda i, f, bk, fi, la, sq, *_: (sq[i], 0)),
                 pl.BlockSpec((n_ctx, width), lambda i, f, bk, fi, la, sq, *_: (ctx_blk0 + sq[i], 0)),
                 pl.BlockSpec((n_ctx, width), lambda i, f, bk, fi, la, sq, *_: (ctx_blk0 + sq[i], 0))]
    for row in range(ATT_ROWS):
        in_specs.append(pl.BlockSpec((1, NA_HEADS, GRID_W, band),
                                     lambda i, f, bk, fi, la, sq, r0, cl, row=row: (cl[i * ATT_ROWS + row], 0, 0, 0)))
    out_specs = [cur(0, 0), cur(0, 0), cur(0, 1), cur(0, 0)]
    grid_spec = pltpu.PrefetchScalarGridSpec(
        num_scalar_prefetch=7, grid=(n_items,), in_specs=in_specs, out_specs=out_specs,
        scratch_shapes=[pltpu.VMEM((2, ch, width), F32), pltpu.VMEM((2, ch, width), F32),
                        pltpu.VMEM((SUBLANES, width), F32), pltpu.VMEM((2 * n_batch, width), F32)])
    return pl.pallas_call(
        functools.partial(_mixer_kernel, ch=ch, n_ctx_items=n_batch, band=band),
        grid_spec=grid_spec,
        out_shape=[jax.ShapeDtypeStruct((m, width), BF16),
                   jax.ShapeDtypeStruct((m, width), F32),
                   jax.ShapeDtypeStruct((m, width), F32),
                   jax.ShapeDtypeStruct((m, width), BF16)],
        compiler_params=_params(1),
        name="token_mixers",
    )(*[jnp.asarray(t) for t in tables],
      sb, rest, rest, rest, rest, rest, rest,
      sc_w, lru_cw, lru_cb.reshape(1, width), sp, wr_bd, wi_bd, b_r, b_i,
      q, k, v, k, v, bias, bias, bias, bias)


def _attention_item(it, r0t, cls, q_ref, k_ref, v_ref, kc_ref, vc_ref, bias_refs, o_ref, band):
    nq = GRID_W
    lane = lax.broadcasted_iota(jnp.int32, (nq, LANES), 1)
    low = lane < NA_HEAD_DIM
    scale = NA_HEAD_DIM ** -0.5
    nt = (((1,), (1,)), ((), ()))
    starts = [pl.multiple_of(r0t[it * ATT_ROWS + row] * GRID_W, GRID_W) for row in range(ATT_ROWS)]
    for hp in range(NA_HEADS // 2):
        cols = slice(hp * LANES, (hp + 1) * LANES)
        stacked = []
        for row in range(ATT_ROWS):
            qp = q_ref[row * nq:(row + 1) * nq, cols].astype(F32) * scale
            stacked += [jnp.where(low, qp, 0.0), jnp.where(low, 0.0, qp)]
        qs_all = jnp.concatenate(stacked, axis=0).astype(BF16)
        s_ctx_all = lax.dot_general(qs_all, kc_ref[:, cols], nt, preferred_element_type=F32)
        e_ctx_rows, e_loc_rows, dens = [], [], []
        for row in range(ATT_ROWS):
            part = slice(row * 2 * nq, (row + 1) * 2 * nq)
            kb = k_ref[pl.ds(starts[row], band), cols]
            s_loc = lax.dot_general(qs_all[part], kb, nt, preferred_element_type=F32)
            bias_ref = bias_refs[row]
            s_loc = s_loc + jnp.concatenate([bias_ref[0, 2 * hp], bias_ref[0, 2 * hp + 1]], axis=0)
            s_ctx = s_ctx_all[part]
            mx = jnp.maximum(jnp.max(s_loc, axis=-1, keepdims=True), jnp.max(s_ctx, axis=-1, keepdims=True))
            e_loc = jnp.exp(s_loc - mx)
            e_ctx = jnp.exp(s_ctx - mx)
            dens.append(jnp.sum(e_loc, axis=-1, keepdims=True) + jnp.sum(e_ctx, axis=-1, keepdims=True))
            e_loc_rows.append(e_loc.astype(BF16))
            e_ctx_rows.append(e_ctx.astype(BF16))
        o_ctx_all = jnp.dot(jnp.concatenate(e_ctx_rows, axis=0), vc_ref[:, cols], preferred_element_type=F32)
        for row in range(ATT_ROWS):
            part = slice(row * 2 * nq, (row + 1) * 2 * nq)
            vb = v_ref[pl.ds(starts[row], band), cols]
            o = (jnp.dot(e_loc_rows[row], vb, preferred_element_type=F32) + o_ctx_all[part]) / dens[row]
            o_ref[row * nq:(row + 1) * nq, cols] = jnp.where(low, o[:nq], o[nq:]).astype(BF16)


def _attention_bias(rpb, seq):
    rows = seq // GRID_W
    kr = min(NA_WIN_ROWS, rows)
    kc = NA_WIN_COLS
    cq = np.arange(GRID_W)
    c0 = np.clip(cq - kc // 2, 0, GRID_W - kc)
    ck = np.arange(GRID_W)
    inside = (ck[None, :] >= c0[:, None]) & (ck[None, :] < c0[:, None] + kc)
    dc = np.clip(ck[None, :] - cq[:, None] + (NA_WIN_COLS - 1), 0, 2 * NA_WIN_COLS - 2)
    n_dr = 2 * NA_WIN_ROWS - 1
    n_dc = 2 * NA_WIN_COLS - 1
    pick = jnp.asarray((np.arange(n_dc)[:, None] == dc.reshape(1, -1)).astype(np.float32))
    picked = jnp.dot(rpb.reshape(-1, n_dc), pick, precision=lax.Precision.HIGHEST)
    picked = picked.reshape(NA_HEADS, 2 * NA_WIN_ROWS - 1, GRID_W, GRID_W)
    table = jnp.where(jnp.asarray(inside)[None, None], picked, NEG_BIG)
    table = table.transpose(0, 2, 1, 3).reshape(NA_HEADS, GRID_W, n_dr * GRID_W)
    classes = []
    for cl in range(kr):
        lo = (NA_WIN_ROWS - 1 - cl) * GRID_W
        classes.append(table[:, :, lo:lo + kr * GRID_W])
    classes.append(jnp.full((NA_HEADS, GRID_W, kr * GRID_W), NEG_BIG, F32))
    return jnp.stack(classes, axis=0)


def _post_kernel(ha_ref, hc_ref, att_ref, zb_ref, hf_ref, hb_ref, lg_ref,
                 sh1_ref, sc1_ref, g1_ref, sh2_ref, sc2_ref,
                 wgl0_ref, wgl1_ref, wgl2_ref, bgl0_ref, bgl1_ref, bgl2_ref,
                 wpa_ref, wpc_ref, wpl_ref, wo_ref, bo_ref, l1g_ref, l1b_ref,
                 rw_ref, rb_ref, tri_ref, upper_ref,
                 h1_ref, xs_ref, pos_ref, cnt_ref, u_s, logit_s, *, split, n_tiles):
    i = pl.program_id(0)

    @pl.when(i == 0)
    def _():
        u_s[...] = jnp.zeros_like(u_s)
        logit_s[...] = jnp.zeros_like(logit_s)

    prev_u = u_s[...]
    prev_logits = logit_s[...]

    h = jnp.where(jnp.minimum(i, n_tiles - 1) < split, ha_ref[...], hc_ref[...])
    u1 = (_layer_norm(h) * (1.0 + sc1_ref[0]) + sh1_ref[0]).astype(BF16)
    y_a = jnp.dot(att_ref[...], wpa_ref[...], preferred_element_type=F32)
    y_b = jnp.dot(zb_ref[...], wpc_ref[...], preferred_element_type=F32)
    prev_pos, prev_w = _route(prev_logits, tri_ref, upper_ref, pos_ref, cnt_ref)
    zc = jax.nn.gelu(lg_ref[...]) * (hf_ref[...] + hb_ref[...])
    y_c = jnp.dot(zc.astype(BF16), wpl_ref[...], preferred_element_type=F32)
    merged = (jax.nn.sigmoid(jnp.dot(u1, wgl0_ref[0], preferred_element_type=F32) + bgl0_ref[0]) * y_a
              + jax.nn.sigmoid(jnp.dot(u1, wgl1_ref[0], preferred_element_type=F32) + bgl1_ref[0]) * y_b
              + jax.nn.sigmoid(jnp.dot(u1, wgl2_ref[0], preferred_element_type=F32) + bgl2_ref[0]) * y_c)
    y = jnp.dot(merged.astype(BF16), wo_ref[...], preferred_element_type=F32) + bo_ref[...]
    h1 = _layer_norm(DEEPNORM_ALPHA * h + g1_ref[0] * y) * l1g_ref[...] + l1b_ref[...]
    h1_ref[...] = h1
    u2 = _layer_norm(h1) * (1.0 + sc2_ref[0]) + sh2_ref[0]
    _sort_rows(prev_u, prev_pos, prev_w, xs_ref)

    u_hi = u2.astype(BF16)
    u_lo = (u2 - u_hi.astype(F32)).astype(BF16)
    by_hi = jnp.dot(u_hi, rw_ref[...], preferred_element_type=F32)
    logit_s[...] = (by_hi[:, :LANES] + by_hi[:, LANES:]
                    + jnp.dot(u_lo, rw_ref[:, :LANES], preferred_element_type=F32) + rb_ref[...])
    u_s[...] = u_hi


def _route(logits, tri_ref, upper_ref, pos_ref, cnt_ref):
    tm = logits.shape[0]
    lane = lax.broadcasted_iota(jnp.int32, (tm, LANES), 1)
    lane_f = lane.astype(F32)
    work = logits
    tops, idxs, hots = [], [], []
    for _ in range(TOP_K):
        mx = jnp.max(work, axis=-1, keepdims=True)
        idx = jnp.min(jnp.where(work == mx, lane_f, float(LANES)), axis=-1, keepdims=True)
        hot = lane_f == idx
        work = jnp.where(hot, -3e38, work)
        tops.append(mx); idxs.append(idx); hots.append(hot)
    exps = [jnp.exp(t - tops[0]) for t in tops]
    den = exps[0] + exps[1] + exps[2] + exps[3]
    hot_all = jnp.zeros((tm, LANES), F32)
    for hot in hots:
        hot_all = hot_all + hot.astype(F32)
    cnt = jnp.sum(hot_all, axis=0, keepdims=True)
    cnt_pad = jnp.floor((cnt + (SUBLANES - 1.0)) * (1.0 / SUBLANES)) * SUBLANES
    off = jnp.dot(jnp.broadcast_to(cnt_pad, (SUBLANES, LANES)), upper_ref[...],
                  precision=lax.Precision.HIGHEST, preferred_element_type=F32)[0:1]
    slot = off + jnp.dot(tri_ref[...], hot_all.astype(BF16), preferred_element_type=F32)
    pos4 = jnp.zeros((tm, LANES), F32)
    w_tile = jnp.zeros((tm, LANES), F32)
    for kk in range(TOP_K):
        pos_k = jnp.sum(jnp.where(hots[kk], slot, 0.0), axis=-1, keepdims=True)
        pos4 = jnp.where(lane == kk, pos_k, pos4)
        p = exps[kk] / den
        p_hi = p.astype(BF16).astype(F32)
        p_mid = (p - p_hi).astype(BF16).astype(F32)
        p_lo = p - p_hi - p_mid
        w_tile = jnp.where(hots[kk], p_hi, w_tile)
        w_tile = jnp.where(lane_f == idxs[kk] + float(N_EXPERTS), p_mid, w_tile)
        w_tile = jnp.where(lane_f == idxs[kk] + float(2 * N_EXPERTS), p_lo, w_tile)
    pos_ref[...] = pos4
    sub = lax.broadcasted_iota(jnp.int32, (SUBLANES, LANES), 0)
    cnt_ref[...] = jnp.where(sub == 0, cnt, jnp.where(sub == 1, off, 0.0))
    return pos4, w_tile


def _sort_rows(u_hi, pos4, w_tile, xs_ref):
    tm = pos4.shape[0]
    n_sorted = xs_ref.shape[0]
    pos_t = pos4.T
    r_iota = lax.broadcasted_iota(jnp.int32, (n_sorted, tm), 0).astype(F32)
    hit = r_iota == pos_t[0:1, :]
    for kk in range(1, TOP_K):
        hit = jnp.logical_or(hit, r_iota == pos_t[kk:kk + 1, :])
    perm = jnp.where(hit, 1.0, 0.0).astype(BF16)
    feats = jnp.concatenate([u_hi, w_tile.astype(BF16)], axis=1)
    xs_ref[...] = jnp.dot(perm, feats, preferred_element_type=F32)


def _post_mixer(h_parts, att, zb, hf, hb, rest, mod3, mod_base, layer, w_in_bf, b_in, wpa, wpc, wpl, wo, b_o, l1g,
                l1b, rw_pad, rb_pad, n_rows, n_lat, seq):
    tm = TM_POST
    width = BRANCH_WIDTH
    ha, hc = h_parts[0], h_parts[-1]
    split = min(ha.shape[0], n_rows) // tm
    gate_col0 = N_EARLY // D_MODEL
    tiles_per_seq = seq // tm
    n_lat_tiles = n_lat // tm
    n_groups_lat = n_lat // seq
    tri = jnp.asarray(np.tril(np.ones((tm, tm), np.float32), -1), BF16)
    upper = jnp.asarray(np.triu(np.ones((LANES, LANES), np.float32), 1))
    n_tiles = n_rows // tm

    def dense(i):
        return jnp.minimum(i, n_tiles - 1)

    def routed(i):
        return jnp.maximum(i - 1, 0)

    def group(i):
        return jnp.where(dense(i) < n_lat_tiles, dense(i) // tiles_per_seq, n_groups_lat)

    def rows(wd, col=0):
        return pl.BlockSpec((tm, wd), lambda i: (dense(i), col))

    def full(shape):
        return pl.BlockSpec(shape, lambda i: (0,) * len(shape))

    def mod(which):
        return pl.BlockSpec((1, 1, D_MODEL), lambda i: (mod_base + group(i) * 6 + which, 0, 0))

    in_specs = [pl.BlockSpec((tm, D_MODEL), lambda i: (jnp.minimum(dense(i), split - 1), 0)),
                pl.BlockSpec((tm, D_MODEL), lambda i: (jnp.maximum(dense(i) - split, 0), 0)),
                rows(width), rows(width), rows(width), rows(width), rows(width, 3),
                mod(0), mod(1), mod(2), mod(3), mod(4)]
    in_specs += [pl.BlockSpec((1, D_MODEL, D_MODEL), lambda i, c=c: (layer, 0, gate_col0 + c)) for c in range(3)]
    in_specs += [pl.BlockSpec((1, 1, D_MODEL), lambda i, c=c: (layer, 0, gate_col0 + c)) for c in range(3)]
    in_specs += [full(wpa.shape), full(wpc.shape), full(wpl.shape), full(wo.shape), full((1, D_MODEL)),
                full((1, D_MODEL)), full((1, D_MODEL)), full(rw_pad.shape), full(rb_pad.shape), full(tri.shape),
                full(upper.shape)]
    out_specs = [rows(D_MODEL), pl.BlockSpec((TOK_BLOCK, XS_WIDTH), lambda i: (routed(i), 0)),
                 pl.BlockSpec((tm, LANES), lambda i: (routed(i), 0)),
                 pl.BlockSpec((SUBLANES, LANES), lambda i: (routed(i), 0))]
    return pl.pallas_call(
        functools.partial(_post_kernel, split=split, n_tiles=n_tiles),
        grid=(n_tiles + 1,),
        in_specs=in_specs, out_specs=out_specs,
        out_shape=[jax.ShapeDtypeStruct((n_rows, D_MODEL), F32),
                   jax.ShapeDtypeStruct((n_tiles * TOK_BLOCK, XS_WIDTH), F32),
                   jax.ShapeDtypeStruct((n_rows, LANES), F32),
                   jax.ShapeDtypeStruct((n_tiles * SUBLANES, LANES), F32)],
        scratch_shapes=[pltpu.VMEM((tm, D_MODEL), BF16), pltpu.VMEM((tm, LANES), F32)],
        compiler_params=_params(1),
        name="post_mixer",
    )(ha, hc, att, zb, hf, hb, rest, mod3, mod3, mod3, mod3, mod3,
      w_in_bf, w_in_bf, w_in_bf, b_in.reshape(DEPTH, 1, -1), b_in.reshape(DEPTH, 1, -1), b_in.reshape(DEPTH, 1, -1),
      wpa, wpc, wpl, wo, b_o.reshape(1, D_MODEL), l1g.reshape(1, D_MODEL), l1b.reshape(1, D_MODEL),
      rw_pad, rb_pad, tri, upper)


def _expert_plan(cnt_out, n_tok_tiles, n_rows):
    tm = TM_EXPERT
    co = cnt_out.reshape(n_tok_tiles, SUBLANES, LANES)
    cnt = (co[:, 0, :N_EXPERTS].astype(jnp.int32) + SUBLANES - 1) // SUBLANES * SUBLANES
    off = co[:, 1, :N_EXPERTS].astype(jnp.int32)
    cum_end = jnp.cumsum(cnt, axis=0)
    cum = cum_end - cnt
    total = cum_end[-1]
    n_et = (total + tm - 1) // tm
    et_end = jnp.cumsum(n_et)
    n_act = et_end[-1:].astype(jnp.int32)
    n_tiles = -(-n_tok_tiles * TOK_BLOCK // tm) + N_EXPERTS
    j = jnp.arange(n_tiles, dtype=jnp.int32)
    tile_e = jnp.minimum(jnp.sum((et_end[None, :] <= j[:, None]).astype(jnp.int32), axis=1), N_EXPERTS - 1)
    pick_e = (tile_e[:, None] == jnp.arange(N_EXPERTS, dtype=jnp.int32)[None, :]).astype(F32)

    def per_tile(table):
        return jnp.dot(pick_e, table.astype(F32), precision=lax.Precision.HIGHEST).astype(jnp.int32)

    row0 = (j - per_tile(et_end - n_et)) * tm
    n_rows_tile = jnp.clip(per_tile(total) - row0, 0, tm)
    cum_e = per_tile(cum.T)
    cum_end_e = per_tile(cum_end.T)
    delta_e = per_tile((jnp.arange(n_tok_tiles, dtype=jnp.int32)[:, None] * TOK_BLOCK + off - cum).T)
    q = row0[:, None] + SUBLANES * jnp.arange(tm // SUBLANES, dtype=jnp.int32)[None, :]
    inside = jnp.logical_and(cum_e.T[:, :, None] <= q[None], q[None] < cum_end_e.T[:, :, None])
    src = q + jnp.sum(jnp.where(inside, delta_e.T[:, :, None], 0), axis=0)
    used = off[:, N_EXPERTS - 1] + cnt[:, N_EXPERTS - 1]
    first = jnp.concatenate([jnp.ones((1,), jnp.int32), (tile_e[1:] != tile_e[:-1]).astype(jnp.int32)])
    group = jnp.cumsum(first) - 1
    after = per_tile(et_end)
    next_e = jnp.where(after < n_act[0],
                       jnp.minimum(jnp.sum((et_end[None, :] <= after[:, None]).astype(jnp.int32), axis=1),
                                   N_EXPERTS - 1), -1)
    return (tile_e.astype(jnp.int32), n_rows_tile.astype(jnp.int32), n_act, src.reshape(-1).astype(jnp.int32),
            used.astype(jnp.int32), first, group.astype(jnp.int32), next_e.astype(jnp.int32), n_tiles)


def _expert_kernel(tile_e, n_rows_t, n_act, src_t, used_t, first_t, group_t, next_t,
                   xs_hbm, wgu_hbm, bgu_ref, wdn_hbm, bdn_ref, ys_hbm,
                   xin, yout, wgu_f, wdn_f, wgu_s, wdn_s, zeros, sem_in, sem_out, sem_zero, sem_wgu, sem_wdn,
                   *, tm, tok_block, n_tok_tiles, layer):
    j = pl.program_id(0)
    na = n_act[0]
    chunks = tm // SUBLANES

    def weight_copies(e, wslot):
        copies = []
        for hbm, buf, sem in ((wgu_hbm, wgu_f, sem_wgu), (wdn_hbm, wdn_f, sem_wdn)):
            rows = buf.shape[1] // WEIGHT_DMA_PIECES
            for piece in range(WEIGHT_DMA_PIECES):
                at = pl.ds(piece * rows, rows)
                copies.append(pltpu.make_async_copy(hbm.at[layer, e, at], buf.at[wslot, at], sem.at[wslot]))
        return copies

    def gather(src, dst, size, slot):
        return pltpu.make_async_copy(xs_hbm.at[pl.ds(src, size)], xin.at[slot, pl.ds(dst, size)], sem_in.at[slot])

    def scatter(src, dst, size, slot):
        return pltpu.make_async_copy(yout.at[slot, pl.ds(dst, size)], ys_hbm.at[pl.ds(src, size)], sem_out.at[slot])

    def start_chunks(jj, slot, copy):
        def one(c, priority):
            src = pl.multiple_of(src_t[jj * chunks + c], SUBLANES)
            copy(src, pl.multiple_of(c * SUBLANES, SUBLANES), SUBLANES, slot).start(priority=priority)

        def body(c, carry):
            one(c, 0)
            return carry

        def body_unrolled(g, carry):
            for u in range(SUBLANES):
                one(g * SUBLANES + u, u % 2)
            return carry

        n = lax.shift_right_logical(n_rows_t[jj], 3)

        @pl.when(n == chunks)
        def _():
            lax.fori_loop(0, chunks // SUBLANES, body_unrolled, 0)

        @pl.when(n != chunks)
        def _():
            lax.fori_loop(0, n, body, 0)

    def wait_rows(jj, slot, copy):
        n = n_rows_t[jj]
        size = tm
        while size >= SUBLANES:
            @pl.when((n & size) != 0)
            def _(size=size):
                copy(0, 0, size, slot).wait()
            size //= 2

    slot = j % 2
    in_slot = j % GATHER_DEPTH

    @pl.when(j == 0)
    def _():
        xin[...] = jnp.zeros_like(xin)
        start_chunks(0, 0, gather)

        @pl.when(na > 1)
        def _():
            start_chunks(1, 1, gather)
        zeros[...] = jnp.zeros_like(zeros)

        def clear_tail(i, copy_op):
            used = used_t[i]
            tail = tok_block - used
            size = ZERO_ROWS
            sizes = []
            while size >= SUBLANES:
                sizes.append(size)
                size //= 2
            for size in sizes:
                @pl.when((tail & size) != 0)
                def _(size=size):
                    at = pl.multiple_of(i * tok_block + used + (tail & ~(2 * size - 1)), SUBLANES)
                    copy_op(pltpu.make_async_copy(zeros.at[pl.ds(0, size)], ys_hbm.at[pl.ds(at, size)], sem_zero))

        def start_clear(i, c):
            clear_tail(i, lambda cp: cp.start())
            return c

        def wait_clear(i, c):
            clear_tail(i, lambda cp: cp.wait())
            return c

        lax.fori_loop(0, n_tok_tiles, start_clear, 0)
        lax.fori_loop(0, n_tok_tiles, wait_clear, 0)

    @pl.when(j + GATHER_DEPTH - 1 < na)
    def _():
        start_chunks(j + GATHER_DEPTH - 1, (j + GATHER_DEPTH - 1) % GATHER_DEPTH, gather)

    @pl.when(j < na)
    def _():
        e = tile_e[j]

        @pl.when(first_t[j] == 1)
        def _():
            wslot = group_t[j] % 2

            @pl.when(j == 0)
            def _():
                for i, cp in enumerate(weight_copies(e, wslot)):
                    cp.start(priority=i % 2)

            for cp in weight_copies(e, wslot):
                cp.wait()
            wgu_s[...] = wgu_f[wslot].astype(BF16)
            wdn_s[...] = wdn_f[wslot].astype(BF16)

            @pl.when(next_t[j] >= 0)
            def _():
                for i, cp in enumerate(weight_copies(next_t[j], 1 - wslot)):
                    cp.start(priority=i % 2)

        wait_rows(j, in_slot, gather)

        def compute(rows):
            x = xin[in_slot, pl.ds(0, rows)]
            lane = lax.broadcasted_iota(jnp.int32, (rows, LANES), 1)
            p = jnp.sum(jnp.where(lane % N_EXPERTS == e, x[:, D_MODEL:], 0.0), axis=-1, keepdims=True)
            gu = (jnp.dot(x[:, :D_MODEL].astype(BF16), wgu_s[...], preferred_element_type=F32)
                  + bgu_ref[0, pl.ds(e, 1), :])
            f = gu.shape[1] // 2
            gate = jnp.minimum(gu[:, :f], SWIGLU_LIMIT)
            up = jnp.clip(gu[:, f:], -SWIGLU_LIMIT, SWIGLU_LIMIT)
            hid = (up + 1.0) * gate * jax.nn.sigmoid(SWIGLU_ALPHA * gate)
            y = (jnp.dot(hid.astype(BF16), wdn_s[...], preferred_element_type=F32)
                 + bdn_ref[0, pl.ds(e, 1), :])
            yout[slot, pl.ds(0, rows)] = y * p

        quarter = tm // 4
        for part in range(1, 5):
            @pl.when(jnp.logical_and(n_rows_t[j] > (part - 1) * quarter, n_rows_t[j] <= part * quarter))
            def _(part=part):
                compute(part * quarter)

        start_chunks(j, slot, scatter)

        @pl.when(j >= 1)
        def _():
            wait_rows(j - 1, 1 - slot, scatter)

        @pl.when(j == na - 1)
        def _():
            wait_rows(j, slot, scatter)


def _experts(xs, plan, layer, w_gu, b_gu, w_dn, b_dn):
    tile_e, n_rows_tile, n_act, src, used, first, group, next_e, n_tiles = plan
    tm = TM_EXPERT
    f2 = w_gu.shape[-1]

    in_specs = [pl.BlockSpec(memory_space=pl.ANY),
                pl.BlockSpec(memory_space=pl.ANY),
                pl.BlockSpec((1, N_EXPERTS, f2), lambda j, *_: (layer, 0, 0)),
                pl.BlockSpec(memory_space=pl.ANY),
                pl.BlockSpec((1, N_EXPERTS, D_MODEL), lambda j, *_: (layer, 0, 0))]
    grid_spec = pltpu.PrefetchScalarGridSpec(
        num_scalar_prefetch=8, grid=(n_tiles,), in_specs=in_specs,
        out_specs=pl.BlockSpec(memory_space=pl.ANY),
        scratch_shapes=[pltpu.VMEM((GATHER_DEPTH, tm, XS_WIDTH), F32), pltpu.VMEM((2, tm, D_MODEL), F32),
                        pltpu.VMEM((2, D_MODEL, f2), F32), pltpu.VMEM((2, f2 // 2, D_MODEL), F32),
                        pltpu.VMEM((D_MODEL, f2), BF16), pltpu.VMEM((f2 // 2, D_MODEL), BF16),
                        pltpu.VMEM((ZERO_ROWS, D_MODEL), F32),
                        pltpu.SemaphoreType.DMA((GATHER_DEPTH,)), pltpu.SemaphoreType.DMA((2,)),
                        pltpu.SemaphoreType.DMA(()),
                        pltpu.SemaphoreType.DMA((2,)), pltpu.SemaphoreType.DMA((2,))])
    return pl.pallas_call(
        functools.partial(_expert_kernel, tm=tm, tok_block=TOK_BLOCK, n_tok_tiles=xs.shape[0] // TOK_BLOCK,
                          layer=layer),
        grid_spec=grid_spec,
        out_shape=jax.ShapeDtypeStruct((xs.shape[0], D_MODEL), F32),
        compiler_params=_params(1),
        name="moe_experts",
    )(tile_e, n_rows_tile, n_act, src, used, first, group, next_e,
      xs, w_gu, b_gu, w_dn, b_dn)


def _combine_kernel(ys_ref, pos_ref, h1_ref, g2_ref, l2g_ref, l2b_ref, o_ref):
    tm = pos_ref.shape[0]
    n_sorted = ys_ref.shape[0]
    pos = pos_ref[...]
    col = lax.broadcasted_iota(jnp.int32, (tm, tm), 1).astype(F32)
    y2 = None
    for r0 in range(0, n_sorted, tm):
        ys = ys_ref[r0:r0 + tm]
        sel_b = jnp.zeros((tm, tm), F32)
        for kk in range(TOP_K):
            sel_b = sel_b + (col == pos[:, kk:kk + 1] - float(r0)).astype(F32)
        sel_b = sel_b.astype(BF16)
        hi = ys.astype(BF16)
        rest = ys - hi.astype(F32)
        mid = rest.astype(BF16)
        lo = (rest - mid.astype(F32)).astype(BF16)
        part = (jnp.dot(sel_b, hi, preferred_element_type=F32) + jnp.dot(sel_b, mid, preferred_element_type=F32)
                + jnp.dot(sel_b, lo, preferred_element_type=F32))
        y2 = part if y2 is None else y2 + part
    o_ref[...] = _layer_norm(DEEPNORM_ALPHA * h1_ref[...] + g2_ref[0] * y2) * l2g_ref[...] + l2b_ref[...]


def _combine(ys, pos4, h1, mod3, mod_base, l2g, l2b, n_lat, seq):
    n_rows = h1.shape[0]
    tm = TM_POST
    tiles_per_seq = seq // tm
    n_lat_tiles = n_lat // tm
    n_groups_lat = n_lat // seq

    def group(i):
        return jnp.where(i < n_lat_tiles, i // tiles_per_seq, n_groups_lat)

    in_specs = [pl.BlockSpec((TOK_BLOCK, D_MODEL), lambda i: (i, 0)),
                pl.BlockSpec((tm, LANES), lambda i: (i, 0)),
                pl.BlockSpec((tm, D_MODEL), lambda i: (i, 0)),
                pl.BlockSpec((1, 1, D_MODEL), lambda i: (mod_base + group(i) * 6 + 5, 0, 0)),
                pl.BlockSpec((1, D_MODEL), lambda i: (0, 0)),
                pl.BlockSpec((1, D_MODEL), lambda i: (0, 0))]
    return pl.pallas_call(
        _combine_kernel,
        grid=(n_rows // tm,),
        in_specs=in_specs,
        out_specs=pl.BlockSpec((tm, D_MODEL), lambda i: (i, 0)),
        out_shape=jax.ShapeDtypeStruct((n_rows, D_MODEL), F32),
        compiler_params=_params(1),
        name="moe_combine",
    )(ys, pos4, h1, mod3, l2g.reshape(1, D_MODEL), l2b.reshape(1, D_MODEL))


def _block_diag(w):
    two, n, d, e = w.shape
    eye = jnp.eye(n, dtype=w.dtype)
    return (w[:, :, :, None, :] * eye[None, :, None, :, None]).reshape(two, n * d, n * e)


def kernel(x, c, ctx, c_ctx, w_mod, b_mod, w_in, b_in, na_rpb, w_proj_attn, w_proj_conv, w_proj_lru, sc_conv_w, lru_conv_w, lru_conv_b, lru_lambda, lru_w_r, lru_b_r, lru_w_i, lru_b_i, w_o, b_o, ln1_g, ln1_b, router_w, router_b, exp_w_gu, exp_b_gu, exp_w_dn, exp_b_dn, ln2_g, ln2_b):
    n_batch, seq, d = x.shape
    n_ctx = ctx.shape[1]
    n_lat = n_batch * seq
    n_all = n_lat + n_batch * n_ctx
    assert d == D_MODEL and n_batch + 1 <= SUBLANES

    cc = jnp.concatenate([c, c_ctx[None], jnp.zeros((SUBLANES - n_batch - 1, d), F32)], axis=0)
    mod = _modulation(cc, w_mod, b_mod)
    groups = n_batch + 1
    mod3 = mod.reshape(DEPTH, SUBLANES, 6, d)[:, :groups].reshape(DEPTH * groups * 6, 1, d)

    cos_t, sin_t = _make_rope(seq, TM_INPROJ)
    h = (x.reshape(n_lat, d), ctx.reshape(n_batch * n_ctx, d))
    w_in_bf = w_in.astype(BF16)

    for layer in range(DEPTH):
        last = layer == DEPTH - 1
        mod_base = layer * groups * 6
        q, k, v, sb, rest = _input_projection(h, mod3, mod_base, layer, w_in_bf, b_in, N_EARLY,
                                              cos_t, sin_t, n_lat, seq)
        sp = jax.nn.softplus(-lru_lambda[layer])
        zb, hf, hb, att = _token_mixers(
            q, k, v, _attention_bias(na_rpb[layer], seq), sb, rest, sc_conv_w[layer], lru_conv_w[layer],
            lru_conv_b[layer], sp, _block_diag(lru_w_r[layer]).astype(BF16), _block_diag(lru_w_i[layer]).astype(BF16),
            lru_b_r[layer], lru_b_i[layer], n_batch, seq, n_ctx)
        n_rows = n_lat if last else n_all
        rw_full = jnp.pad(router_w[layer], ((0, 0), (0, LANES - N_EXPERTS)))
        rw_hi = rw_full.astype(BF16)
        rw_pad = jnp.concatenate([rw_hi, (rw_full - rw_hi.astype(F32)).astype(BF16)], axis=1)
        rb_pad = jnp.concatenate([router_b[layer], jnp.full((LANES - N_EXPERTS,), NEG_BIG, F32)]).reshape(1, LANES)
        h1, xs, pos4, cnt_out = _post_mixer(
            h, att, zb, hf, hb, rest, mod3, mod_base, layer, w_in_bf, b_in,
            w_proj_attn[layer].astype(BF16), w_proj_conv[layer].astype(BF16), w_proj_lru[layer].astype(BF16),
            w_o[layer].astype(BF16), b_o[layer], ln1_g[layer], ln1_b[layer], rw_pad, rb_pad, n_rows, n_lat, seq)
        plan = _expert_plan(cnt_out, n_rows // TM_POST, n_rows)
        ys = _experts(xs, plan, layer, exp_w_gu, exp_b_gu, exp_w_dn, exp_b_dn)
        h = (_combine(ys, pos4, h1, mod3, mod_base, ln2_g[layer], ln2_b[layer], n_lat, seq),)
    return h[0].reshape(n_batch, seq, d)
```

```python
import functools

import numpy as np
import jax
import jax.numpy as jnp
from jax import lax
from jax.experimental import pallas as pl
from jax.experimental.pallas import tpu as pltpu

D_MODEL = 1024
DEPTH = 2
GRID_W = 64
NA_HEADS = 8
NA_HEAD_DIM = 64
NA_WIN_ROWS = 8
NA_WIN_COLS = 16
ROPE_BASE = 10000.0
BRANCH_WIDTH = 512
LRU_C = 8.0
N_EARLY = 8 * BRANCH_WIDTH
N_EXPERTS = 32
TOP_K = 4
SWIGLU_LIMIT = 7.0
SWIGLU_ALPHA = 1.702
LN_EPS = 1e-5
DEEPNORM_ALPHA = (2 * DEPTH) ** 0.25
NEG_BIG = -1e30

LANES = 128
SUBLANES = 8
VMEM_LIMIT_BYTES = 56 * 1024 * 1024

TM_INPROJ = 512
SCAN_CHUNK = 256
ATT_ROWS = 4
ATT_QROWS = ATT_ROWS * GRID_W
TM_POST = 256
TM_EXPERT = 512
XS_WIDTH = D_MODEL + LANES
TOK_BLOCK = TM_POST * TOP_K + N_EXPERTS * SUBLANES
ZERO_ROWS = TOK_BLOCK - TM_POST * TOP_K

F32 = jnp.float32
BF16 = jnp.bfloat16


def _params(n_axes):
    return pltpu.CompilerParams(dimension_semantics=("arbitrary",) * n_axes,
                                vmem_limit_bytes=VMEM_LIMIT_BYTES)


def _layer_norm(x):
    mu = jnp.mean(x, axis=-1, keepdims=True)
    xc = x - mu
    var = jnp.mean(xc * xc, axis=-1, keepdims=True)
    return xc * lax.rsqrt(var + LN_EPS)


def _mod_kernel(c_ref, w_ref, b_ref, o_ref):
    c = c_ref[...]
    s = (c * jax.nn.sigmoid(c)).astype(BF16)
    o_ref[0] = jnp.dot(s, w_ref[0].astype(BF16), preferred_element_type=F32) + b_ref[0]


def _modulation(cc, w_mod, b_mod):
    n_out = w_mod.shape[-1]
    return pl.pallas_call(
        _mod_kernel,
        grid=(DEPTH, n_out // D_MODEL),
        in_specs=[pl.BlockSpec((SUBLANES, D_MODEL), lambda l, j: (0, 0)),
                  pl.BlockSpec((1, D_MODEL, D_MODEL), lambda l, j: (l, 0, j)),
                  pl.BlockSpec((1, 1, D_MODEL), lambda l, j: (l, 0, j))],
        out_specs=pl.BlockSpec((1, SUBLANES, D_MODEL), lambda l, j: (l, 0, j)),
        out_shape=jax.ShapeDtypeStruct((DEPTH, SUBLANES, n_out), F32),
        compiler_params=_params(2),
        name="modulation",
    )(cc, w_mod, b_mod.reshape(DEPTH, 1, n_out))


def _rope_half(x, cos, sin_signed):
    m = NA_HEAD_DIM // 4
    lane = lax.broadcasted_iota(jnp.int32, (x.shape[0], LANES), 1)
    first = (lane % (2 * m)) < m
    outs = []
    for cidx in range(x.shape[1] // LANES):
        xc = x[:, cidx * LANES:(cidx + 1) * LANES]
        partner = jnp.where(first, pltpu.roll(xc, LANES - m, 1), pltpu.roll(xc, m, 1))
        outs.append(xc * cos + partner * sin_signed)
    return jnp.concatenate(outs, axis=1)


def _inproj_kernel(ha_ref, hb_ref, sh_ref, sc_ref, w_ref, b_ref, cos_ref, sin_ref,
                   q_ref, k_ref, v_ref, sb_ref, rest_ref, *, split):
    h = jnp.where(pl.program_id(0) < split, ha_ref[...], hb_ref[...])
    xn = (_layer_norm(h) * (1.0 + sc_ref[0]) + sh_ref[0]).astype(BF16)
    half = BRANCH_WIDTH

    def columns(lo, width):
        return jnp.dot(xn, w_ref[0, :, lo:lo + width], preferred_element_type=F32) + b_ref[0, :, lo:lo + width]

    cos = cos_ref[...]
    sin = sin_ref[...]
    q_ref[...] = _rope_half(columns(0, half), cos, sin).astype(BF16)
    k_ref[...] = _rope_half(columns(half, half), cos, sin).astype(BF16)
    v_ref[...] = columns(2 * half, half).astype(BF16)
    sb_ref[...] = columns(3 * half, half)
    n_rest = rest_ref.shape[1]
    for lo in range(0, n_rest, 2 * half):
        rest_ref[:, lo:lo + 2 * half] = columns(4 * half + lo, 2 * half)


def _input_projection(h_parts, mod3, mod_base, layer, w_in_bf, b_in, n_cols, cos_t, sin_t, n_lat, seq):
    tm = TM_INPROJ
    ha, hb = h_parts[0], h_parts[-1]
    split = ha.shape[0] // tm
    m = ha.shape[0] + (hb.shape[0] if len(h_parts) == 2 else 0)
    n_lat_tiles = n_lat // tm
    tiles_per_seq = seq // tm
    n_groups_lat = n_lat // seq

    def group(i):
        return jnp.where(i < n_lat_tiles, i // tiles_per_seq, n_groups_lat)

    def rope_blk(i):
        return jnp.where(i < n_lat_tiles, i % tiles_per_seq, tiles_per_seq)

    half = BRANCH_WIDTH
    return pl.pallas_call(
        functools.partial(_inproj_kernel, split=split),
        grid=(m // tm,),
        in_specs=[pl.BlockSpec((tm, D_MODEL), lambda i: (jnp.minimum(i, split - 1), 0)),
                  pl.BlockSpec((tm, D_MODEL), lambda i: (jnp.maximum(i - split, 0), 0)),
                  pl.BlockSpec((1, 1, D_MODEL), lambda i: (mod_base + group(i) * 6 + 0, 0, 0)),
                  pl.BlockSpec((1, 1, D_MODEL), lambda i: (mod_base + group(i) * 6 + 1, 0, 0)),
                  pl.BlockSpec((1, D_MODEL, n_cols), lambda i: (layer, 0, 0)),
                  pl.BlockSpec((1, 1, n_cols), lambda i: (layer, 0, 0)),
                  pl.BlockSpec((tm, LANES), lambda i: (rope_blk(i), 0)),
                  pl.BlockSpec((tm, LANES), lambda i: (rope_blk(i), 0))],
        out_specs=[pl.BlockSpec((tm, half), lambda i: (i, 0)),
                   pl.BlockSpec((tm, half), lambda i: (i, 0)),
                   pl.BlockSpec((tm, half), lambda i: (i, 0)),
                   pl.BlockSpec((tm, half), lambda i: (i, 0)),
                   pl.BlockSpec((tm, n_cols - 4 * half), lambda i: (i, 0))],
        out_shape=[jax.ShapeDtypeStruct((m, half), BF16),
                   jax.ShapeDtypeStruct((m, half), BF16),
                   jax.ShapeDtypeStruct((m, half), BF16),
                   jax.ShapeDtypeStruct((m, half), F32),
                   jax.ShapeDtypeStruct((m, n_cols - 4 * half), F32)],
        compiler_params=_params(1),
        name="input_projection",
    )(ha, hb, mod3, mod3, w_in_bf, b_in.reshape(DEPTH, 1, -1), cos_t, sin_t)


def _make_rope(seq, tm):
    rows = seq // GRID_W
    d = np.arange(LANES) % NA_HEAD_DIM
    m = NA_HEAD_DIM // 4
    inv_freq = (ROPE_BASE ** (-jnp.arange(m, dtype=F32) / m))[d % m]
    n_pos = max(rows, GRID_W)
    ang = jnp.arange(n_pos, dtype=F32)[:, None] * inv_freq[None, :]
    by_row = jnp.asarray(d < 2 * m)[None, None, :]

    def expand(tab):
        full = jnp.where(by_row, tab[:rows, None, :], tab[None, :GRID_W, :])
        return full.reshape(seq, LANES)

    cos = expand(jnp.cos(ang))
    sin = expand(jnp.sin(ang))
    sin_signed = jnp.where(jnp.asarray((d % (2 * m)) < m)[None, :], -sin, sin)
    cos = jnp.concatenate([cos, jnp.ones((tm, LANES), F32)], axis=0)
    sin_signed = jnp.concatenate([sin_signed, jnp.zeros((tm, LANES), F32)], axis=0)
    return cos, sin_signed


def _mixer_kernel(fblk, bblk, first, last, seqb, r0t, cls,
                  sb_ref, xf_ref, xfp_ref, xfn_ref, lxb_ref, lxbp_ref, lxbn_ref,
                  scw_ref, cw_ref, cb_ref, sp_ref, wr_ref, wi_ref, br_ref, bi_ref,
                  q_ref, k_ref, v_ref, kc_ref, vc_ref, bias0_ref, bias1_ref, bias2_ref, bias3_ref,
                  zb_ref, hf_ref, hb_ref, att_ref,
                  a_s, b_s, hc_s, st_s, *, ch, n_ctx_items, band):
    it = pl.program_id(0)
    is_first = first[it] == 1
    is_last = last[it] == 1
    is_ctx = it < n_ctx_items
    b = seqb[it]
    width = BRANCH_WIDTH
    row = lax.broadcasted_iota(jnp.int32, (ch, width), 0)
    not_first = jnp.where(is_first, 0.0, 1.0).astype(F32)
    not_last = jnp.where(is_last, 0.0, 1.0).astype(F32)

    def back1(u, prev_row):
        return jnp.where(row == 0, prev_row, pltpu.roll(u, 1, 0))

    def back2(u, prev2, prev1):
        return jnp.where(row == 0, prev2, jnp.where(row == 1, prev1, pltpu.roll(u, 2, 0)))

    def fwd1(u, next_row):
        return jnp.where(row == ch - 1, next_row, pltpu.roll(u, ch - 1, 0))

    gate_cols, value_cols, lru_cols = (slice(s * width, (s + 1) * width) for s in range(3))

    def lru_input(x, p, n, prev_ok, next_ok):
        p = p * prev_ok
        n = n * next_ok
        return (cw_ref[0:1] * back2(x, p[6:7], p[7:8]) + cw_ref[1:2] * back1(x, p[7:8])
                + cw_ref[2:3] * x + cw_ref[3:4] * fwd1(x, n[0:1]) + cb_ref[...])

    def coeffs(d, xm):
        xb = xm.astype(BF16)
        r = jax.nn.sigmoid(jnp.dot(xb, wr_ref[d], preferred_element_type=F32) + br_ref[d:d + 1])
        g = jax.nn.sigmoid(jnp.dot(xb, wi_ref[d], preferred_element_type=F32) + bi_ref[d:d + 1])
        log_a = (-LRU_C * sp_ref[d:d + 1]) * r
        a = jnp.exp(log_a)
        a_s[d] = a
        b_s[d] = jnp.sqrt(-jnp.tanh(log_a) * (a * a + 1.0)) * (g * xm)

    @pl.when(jnp.logical_and(is_first, is_ctx))
    def _():
        hc_s[...] = jnp.zeros_like(hc_s)

    @pl.when(jnp.logical_and(is_first, jnp.logical_not(is_ctx)))
    def _():
        hc_s[0:1] = st_s[pl.ds(2 * b, 1), :]
        hc_s[1:2] = st_s[pl.ds(2 * b + 1, 1), :]

    coeffs(0, lru_input(xf_ref[:, lru_cols], xfp_ref[:, lru_cols], xfn_ref[:, lru_cols], not_first, not_last))
    hf = hc_s[0:1]
    for t in range(ch):
        hf = a_s[0, t:t + 1, :] * hf + b_s[0, t:t + 1, :]
        hf_ref[t:t + 1, :] = hf
    hc_s[0:1] = hf

    coeffs(1, lru_input(lxb_ref[...], lxbp_ref[...], lxbn_ref[...], not_last, not_first))

    u = xf_ref[:, gate_cols] * xf_ref[:, value_cols]
    u_prev = xfp_ref[7:8, gate_cols] * xfp_ref[7:8, value_cols] * not_first
    u_next = xfn_ref[0:1, gate_cols] * xfn_ref[0:1, value_cols] * not_last
    conv = scw_ref[0:1] * back1(u, u_prev) + scw_ref[1:2] * u + scw_ref[2:3] * fwd1(u, u_next)
    zb_ref[...] = (sb_ref[...] * conv).astype(BF16)

    hb = hc_s[1:2]
    for t in range(ch - 1, -1, -1):
        hb = a_s[1, t:t + 1, :] * hb + b_s[1, t:t + 1, :]
        hb_ref[t:t + 1, :] = hb
    hc_s[1:2] = hb

    _attention_item(it, r0t, cls, q_ref, k_ref, v_ref, kc_ref, vc_ref,
                    (bias0_ref, bias1_ref, bias2_ref, bias3_ref), att_ref, band)

    @pl.when(is_ctx)
    def _():
        st_s[pl.ds(2 * b, 1), :] = hf
        st_s[pl.ds(2 * b + 1, 1), :] = hb


def _mixer_tables(n_batch, seq, n_ctx, ch):
    assert n_ctx == ch and ch == ATT_QROWS
    nc = seq // ch
    rows = seq // GRID_W
    kr = min(NA_WIN_ROWS, rows)
    ctx0 = n_batch * seq // ch
    fblk, bblk, first, last, seqb, r0t, cls = [], [], [], [], [], [], []
    for b in range(n_batch):
        fblk.append(ctx0 + b); bblk.append(ctx0 + b); first.append(1); last.append(1); seqb.append(b)
        r0t.extend([0] * ATT_ROWS); cls.extend([kr] * ATT_ROWS)
    for b in range(n_batch):
        for c in range(nc):
            fblk.append(b * nc + c); bblk.append(b * nc + nc - 1 - c)
            first.append(int(c == 0)); last.append(int(c == nc - 1)); seqb.append(b)
            for r in range(c * ATT_ROWS, (c + 1) * ATT_ROWS):
                r0 = min(max(r - kr // 2, 0), rows - kr)
                r0t.append(r0); cls.append(r - r0)
    return [np.asarray(a, np.int32) for a in (fblk, bblk, first, last, seqb, r0t, cls)]


def _token_mixers(q, k, v, bias, sb, rest, sc_w, lru_cw, lru_cb, sp, wr_bd, wi_bd, b_r, b_i, n_batch, seq, n_ctx):
    m = sb.shape[0]
    ch = SCAN_CHUNK
    width = BRANCH_WIDTH
    tables = _mixer_tables(n_batch, seq, n_ctx, ch)
    n_items = len(tables[0])
    halo_per_chunk = ch // SUBLANES
    last_halo = m // SUBLANES - 1
    band = min(NA_WIN_ROWS, seq // GRID_W) * GRID_W
    ctx_blk0 = n_batch * seq // n_ctx

    def cur(col, which, n=1):
        return pl.BlockSpec((ch, n * width), lambda i, f, bk, *_: ((f, bk)[which][i], col))

    def prev(col, which, n=1):
        return pl.BlockSpec((SUBLANES, n * width),
                            lambda i, f, bk, *_: (jnp.maximum((f, bk)[which][i] * halo_per_chunk - 1, 0), col))

    def nxt(col, which, n=1):
        return pl.BlockSpec((SUBLANES, n * width),
                            lambda i, f, bk, *_: (jnp.minimum(((f, bk)[which][i] + 1) * halo_per_chunk, last_halo), col))

    def full(shape):
        return pl.BlockSpec(shape, lambda i, *_: (0,) * len(shape))

    in_specs = [cur(0, 0), cur(0, 0, 3), prev(0, 0, 3), nxt(0, 0, 3),
                cur(2, 1), prev(2, 1), nxt(2, 1),
                full(sc_w.shape), full(lru_cw.shape), full((1, width)), full(sp.shape),
                full(wr_bd.shape), full(wi_bd.shape), full(b_r.shape), full(b_i.shape)]
    in_specs += [cur(0, 0),
                 pl.BlockSpec((seq, width), lambda i, f, bk, fi, la, sq, *_: (sq[i], 0)),
                 pl.BlockSpec((seq, width), lambda i, f, bk, fi, la, sq, *_: (sq[i], 0)),
                 pl.BlockSpec((n_ctx, width), lambda i, f, bk, fi, la, sq, *_: (ctx_blk0 + sq[i], 0)),
                 pl.BlockSpec((n_ctx, width), lambda i, f, bk, fi, la, sq, *_: (ctx_blk0 + sq[i], 0))]
    for row in range(ATT_ROWS):
        in_specs.append(pl.BlockSpec((1, NA_HEADS, GRID_W, band),
                                     lambda i, f, bk, fi, la, sq, r0, cl, row=row: (cl[i * ATT_ROWS + row], 0, 0, 0)))
    out_specs = [cur(0, 0), cur(0, 0), cur(0, 1), cur(0, 0)]
    grid_spec = pltpu.PrefetchScalarGridSpec(
        num_scalar_prefetch=7, grid=(n_items,), in_specs=in_specs, out_specs=out_specs,
        scratch_shapes=[pltpu.VMEM((2, ch, width), F32), pltpu.VMEM((2, ch, width), F32),
                        pltpu.VMEM((SUBLANES, width), F32), pltpu.VMEM((2 * n_batch, width), F32)])
    return pl.pallas_call(
        functools.partial(_mixer_kernel, ch=ch, n_ctx_items=n_batch, band=band),
        grid_spec=grid_spec,
        out_shape=[jax.ShapeDtypeStruct((m, width), BF16),
                   jax.ShapeDtypeStruct((m, width), F32),
                   jax.ShapeDtypeStruct((m, width), F32),
                   jax.ShapeDtypeStruct((m, width), BF16)],
        compiler_params=_params(1),
        name="token_mixers",
    )(*[jnp.asarray(t) for t in tables],
      sb, rest, rest, rest, rest, rest, rest,
      sc_w, lru_cw, lru_cb.reshape(1, width), sp, wr_bd, wi_bd, b_r, b_i,
      q, k, v, k, v, bias, bias, bias, bias)


def _attention_item(it, r0t, cls, q_ref, k_ref, v_ref, kc_ref, vc_ref, bias_refs, o_ref, band):
    nq = GRID_W
    lane = lax.broadcasted_iota(jnp.int32, (nq, LANES), 1)
    low = lane < NA_HEAD_DIM
    scale = NA_HEAD_DIM ** -0.5
    nt = (((1,), (1,)), ((), ()))
    starts = [pl.multiple_of(r0t[it * ATT_ROWS + row] * GRID_W, GRID_W) for row in range(ATT_ROWS)]
    for hp in range(NA_HEADS // 2):
        cols = slice(hp * LANES, (hp + 1) * LANES)
        stacked = []
        for row in range(ATT_ROWS):
            qp = q_ref[row * nq:(row + 1) * nq, cols].astype(F32) * scale
            stacked += [jnp.where(low, qp, 0.0), jnp.where(low, 0.0, qp)]
        qs_all = jnp.concatenate(stacked, axis=0).astype(BF16)
        s_ctx_all = lax.dot_general(qs_all, kc_ref[:, cols], nt, preferred_element_type=F32)
        e_ctx_rows, e_loc_rows, dens = [], [], []
        for row in range(ATT_ROWS):
            part = slice(row * 2 * nq, (row + 1) * 2 * nq)
            kb = k_ref[pl.ds(starts[row], band), cols]
            s_loc = lax.dot_general(qs_all[part], kb, nt, preferred_element_type=F32)
            bias_ref = bias_refs[row]
            s_loc = s_loc + jnp.concatenate([bias_ref[0, 2 * hp], bias_ref[0, 2 * hp + 1]], axis=0)
            s_ctx = s_ctx_all[part]
            mx = jnp.maximum(jnp.max(s_loc, axis=-1, keepdims=True), jnp.max(s_ctx, axis=-1, keepdims=True))
            e_loc = jnp.exp(s_loc - mx)
            e_ctx = jnp.exp(s_ctx - mx)
            dens.append(jnp.sum(e_loc, axis=-1, keepdims=True) + jnp.sum(e_ctx, axis=-1, keepdims=True))
            e_loc_rows.append(e_loc.astype(BF16))
            e_ctx_rows.append(e_ctx.astype(BF16))
        o_ctx_all = jnp.dot(jnp.concatenate(e_ctx_rows, axis=0), vc_ref[:, cols], preferred_element_type=F32)
        for row in range(ATT_ROWS):
            part = slice(row * 2 * nq, (row + 1) * 2 * nq)
            vb = v_ref[pl.ds(starts[row], band), cols]
            o = (jnp.dot(e_loc_rows[row], vb, preferred_element_type=F32) + o_ctx_all[part]) / dens[row]
            o_ref[row * nq:(row + 1) * nq, cols] = jnp.where(low, o[:nq], o[nq:]).astype(BF16)


def _attention_bias(rpb, seq):
    rows = seq // GRID_W
    kr = min(NA_WIN_ROWS, rows)
    kc = NA_WIN_COLS
    cq = np.arange(GRID_W)
    c0 = np.clip(cq - kc // 2, 0, GRID_W - kc)
    ck = np.arange(GRID_W)
    inside = (ck[None, :] >= c0[:, None]) & (ck[None, :] < c0[:, None] + kc)
    dc = np.clip(ck[None, :] - cq[:, None] + (NA_WIN_COLS - 1), 0, 2 * NA_WIN_COLS - 2)
    n_dr = 2 * NA_WIN_ROWS - 1
    n_dc = 2 * NA_WIN_COLS - 1
    pick = jnp.asarray((np.arange(n_dc)[:, None] == dc.reshape(1, -1)).astype(np.float32))
    picked = jnp.dot(rpb.reshape(-1, n_dc), pick, precision=lax.Precision.HIGHEST)
    picked = picked.reshape(NA_HEADS, 2 * NA_WIN_ROWS - 1, GRID_W, GRID_W)
    table = jnp.where(jnp.asarray(inside)[None, None], picked, NEG_BIG)
    table = table.transpose(0, 2, 1, 3).reshape(NA_HEADS, GRID_W, n_dr * GRID_W)
    classes = []
    for cl in range(kr):
        lo = (NA_WIN_ROWS - 1 - cl) * GRID_W
        classes.append(table[:, :, lo:lo + kr * GRID_W])
    classes.append(jnp.full((NA_HEADS, GRID_W, kr * GRID_W), NEG_BIG, F32))
    return jnp.stack(classes, axis=0)


def _post_kernel(ha_ref, hc_ref, att_ref, zb_ref, hf_ref, hb_ref, lg_ref,
                 sh1_ref, sc1_ref, g1_ref, sh2_ref, sc2_ref,
                 wgl0_ref, wgl1_ref, wgl2_ref, bgl0_ref, bgl1_ref, bgl2_ref,
                 wpa_ref, wpc_ref, wpl_ref, wo_ref, bo_ref, l1g_ref, l1b_ref,
                 rw_ref, rb_ref, tri_ref, upper_ref,
                 h1_ref, xs_ref, pos_ref, cnt_ref, u_s, logit_s, *, split, n_tiles):
    i = pl.program_id(0)

    @pl.when(i == 0)
    def _():
        u_s[...] = jnp.zeros_like(u_s)
        logit_s[...] = jnp.zeros_like(logit_s)

    prev_u = u_s[...]
    prev_logits = logit_s[...]

    h = jnp.where(jnp.minimum(i, n_tiles - 1) < split, ha_ref[...], hc_ref[...])
    u1 = (_layer_norm(h) * (1.0 + sc1_ref[0]) + sh1_ref[0]).astype(BF16)
    y_a = jnp.dot(att_ref[...], wpa_ref[...], preferred_element_type=F32)
    y_b = jnp.dot(zb_ref[...], wpc_ref[...], preferred_element_type=F32)
    prev_pos, prev_w = _route(prev_logits, tri_ref, upper_ref, pos_ref, cnt_ref)
    zc = jax.nn.gelu(lg_ref[...]) * (hf_ref[...] + hb_ref[...])
    y_c = jnp.dot(zc.astype(BF16), wpl_ref[...], preferred_element_type=F32)
    merged = (jax.nn.sigmoid(jnp.dot(u1, wgl0_ref[0], preferred_element_type=F32) + bgl0_ref[0]) * y_a
              + jax.nn.sigmoid(jnp.dot(u1, wgl1_ref[0], preferred_element_type=F32) + bgl1_ref[0]) * y_b
              + jax.nn.sigmoid(jnp.dot(u1, wgl2_ref[0], preferred_element_type=F32) + bgl2_ref[0]) * y_c)
    y = jnp.dot(merged.astype(BF16), wo_ref[...], preferred_element_type=F32) + bo_ref[...]
    h1 = _layer_norm(DEEPNORM_ALPHA * h + g1_ref[0] * y) * l1g_ref[...] + l1b_ref[...]
    h1_ref[...] = h1
    u2 = _layer_norm(h1) * (1.0 + sc2_ref[0]) + sh2_ref[0]
    _sort_rows(prev_u, prev_pos, prev_w, xs_ref)

    u_hi = u2.astype(BF16)
    u_lo = (u2 - u_hi.astype(F32)).astype(BF16)
    by_hi = jnp.dot(u_hi, rw_ref[...], preferred_element_type=F32)
    logit_s[...] = (by_hi[:, :LANES] + by_hi[:, LANES:]
                    + jnp.dot(u_lo, rw_ref[:, :LANES], preferred_element_type=F32) + rb_ref[...])
    u_s[...] = u_hi


def _route(logits, tri_ref, upper_ref, pos_ref, cnt_ref):
    tm = logits.shape[0]
    lane = lax.broadcasted_iota(jnp.int32, (tm, LANES), 1)
    lane_f = lane.astype(F32)
    work = logits
    tops, idxs, hots = [], [], []
    for _ in range(TOP_K):
        mx = jnp.max(work, axis=-1, keepdims=True)
        idx = jnp.min(jnp.where(work == mx, lane_f, float(LANES)), axis=-1, keepdims=True)
        hot = lane_f == idx
        work = jnp.where(hot, -3e38, work)
        tops.append(mx); idxs.append(idx); hots.append(hot)
    exps = [jnp.exp(t - tops[0]) for t in tops]
    den = exps[0] + exps[1] + exps[2] + exps[3]
    hot_all = jnp.zeros((tm, LANES), F32)
    for hot in hots:
        hot_all = hot_all + hot.astype(F32)
    cnt = jnp.sum(hot_all, axis=0, keepdims=True)
    cnt_pad = jnp.floor((cnt + (SUBLANES - 1.0)) * (1.0 / SUBLANES)) * SUBLANES
    off = jnp.dot(jnp.broadcast_to(cnt_pad, (SUBLANES, LANES)), upper_ref[...],
                  precision=lax.Precision.HIGHEST, preferred_element_type=F32)[0:1]
    slot = off + jnp.dot(tri_ref[...], hot_all.astype(BF16), preferred_element_type=F32)
    pos4 = jnp.zeros((tm, LANES), F32)
    w_tile = jnp.zeros((tm, LANES), F32)
    for kk in range(TOP_K):
        pos_k = jnp.sum(jnp.where(hots[kk], slot, 0.0), axis=-1, keepdims=True)
        pos4 = jnp.where(lane == kk, pos_k, pos4)
        p = exps[kk] / den
        p_hi = p.astype(BF16).astype(F32)
        p_mid = (p - p_hi).astype(BF16).astype(F32)
        p_lo = p - p_hi - p_mid
        w_tile = jnp.where(hots[kk], p_hi, w_tile)
        w_tile = jnp.where(lane_f == idxs[kk] + float(N_EXPERTS), p_mid, w_tile)
        w_tile = jnp.where(lane_f == idxs[kk] + float(2 * N_EXPERTS), p_lo, w_tile)
    pos_ref[...] = pos4
    sub = lax.broadcasted_iota(jnp.int32, (SUBLANES, LANES), 0)
    cnt_ref[...] = jnp.where(sub == 0, cnt, jnp.where(sub == 1, off, 0.0))
    return pos4, w_tile


def _sort_rows(u_hi, pos4, w_tile, xs_ref):
    tm = pos4.shape[0]
    n_sorted = xs_ref.shape[0]
    pos_t = pos4.T
    r_iota = lax.broadcasted_iota(jnp.int32, (n_sorted, tm), 0).astype(F32)
    hit = r_iota == pos_t[0:1, :]
    for kk in range(1, TOP_K):
        hit = jnp.logical_or(hit, r_iota == pos_t[kk:kk + 1, :])
    perm = jnp.where(hit, 1.0, 0.0).astype(BF16)
    feats = jnp.concatenate([u_hi, w_tile.astype(BF16)], axis=1)
    xs_ref[...] = jnp.dot(perm, feats, preferred_element_type=F32)


def _post_mixer(h_parts, att, zb, hf, hb, rest, mod3, mod_base, layer, w_in_bf, b_in, wpa, wpc, wpl, wo, b_o, l1g,
                l1b, rw_pad, rb_pad, n_rows, n_lat, seq):
    tm = TM_POST
    width = BRANCH_WIDTH
    ha, hc = h_parts[0], h_parts[-1]
    split = min(ha.shape[0], n_rows) // tm
    gate_col0 = N_EARLY // D_MODEL
    tiles_per_seq = seq // tm
    n_lat_tiles = n_lat // tm
    n_groups_lat = n_lat // seq
    tri = jnp.asarray(np.tril(np.ones((tm, tm), np.float32), -1), BF16)
    upper = jnp.asarray(np.triu(np.ones((LANES, LANES), np.float32), 1))
    n_tiles = n_rows // tm

    def dense(i):
        return jnp.minimum(i, n_tiles - 1)

    def routed(i):
        return jnp.maximum(i - 1, 0)

    def group(i):
        return jnp.where(dense(i) < n_lat_tiles, dense(i) // tiles_per_seq, n_groups_lat)

    def rows(wd, col=0):
        return pl.BlockSpec((tm, wd), lambda i: (dense(i), col))

    def full(shape):
        return pl.BlockSpec(shape, lambda i: (0,) * len(shape))

    def mod(which):
        return pl.BlockSpec((1, 1, D_MODEL), lambda i: (mod_base + group(i) * 6 + which, 0, 0))

    in_specs = [pl.BlockSpec((tm, D_MODEL), lambda i: (jnp.minimum(dense(i), split - 1), 0)),
                pl.BlockSpec((tm, D_MODEL), lambda i: (jnp.maximum(dense(i) - split, 0), 0)),
                rows(width), rows(width), rows(width), rows(width), rows(width, 3),
                mod(0), mod(1), mod(2), mod(3), mod(4)]
    in_specs += [pl.BlockSpec((1, D_MODEL, D_MODEL), lambda i, c=c: (layer, 0, gate_col0 + c)) for c in range(3)]
    in_specs += [pl.BlockSpec((1, 1, D_MODEL), lambda i, c=c: (layer, 0, gate_col0 + c)) for c in range(3)]
    in_specs += [full(wpa.shape), full(wpc.shape), full(wpl.shape), full(wo.shape), full((1, D_MODEL)),
                full((1, D_MODEL)), full((1, D_MODEL)), full(rw_pad.shape), full(rb_pad.shape), full(tri.shape),
                full(upper.shape)]
    out_specs = [rows(D_MODEL), pl.BlockSpec((TOK_BLOCK, XS_WIDTH), lambda i: (routed(i), 0)),
                 pl.BlockSpec((tm, LANES), lambda i: (routed(i), 0)),
                 pl.BlockSpec((SUBLANES, LANES), lambda i: (routed(i), 0))]
    return pl.pallas_call(
        functools.partial(_post_kernel, split=split, n_tiles=n_tiles),
        grid=(n_tiles + 1,),
        in_specs=in_specs, out_specs=out_specs,
        out_shape=[jax.ShapeDtypeStruct((n_rows, D_MODEL), F32),
                   jax.ShapeDtypeStruct((n_tiles * TOK_BLOCK, XS_WIDTH), F32),
                   jax.ShapeDtypeStruct((n_rows, LANES), F32),
                   jax.ShapeDtypeStruct((n_tiles * SUBLANES, LANES), F32)],
        scratch_shapes=[pltpu.VMEM((tm, D_MODEL), BF16), pltpu.VMEM((tm, LANES), F32)],
        compiler_params=_params(1),
        name="post_mixer",
    )(ha, hc, att, zb, hf, hb, rest, mod3, mod3, mod3, mod3, mod3,
      w_in_bf, w_in_bf, w_in_bf, b_in.reshape(DEPTH, 1, -1), b_in.reshape(DEPTH, 1, -1), b_in.reshape(DEPTH, 1, -1),
      wpa, wpc, wpl, wo, b_o.reshape(1, D_MODEL), l1g.reshape(1, D_MODEL), l1b.reshape(1, D_MODEL),
      rw_pad, rb_pad, tri, upper)


def _expert_plan(cnt_out, n_tok_tiles, n_rows):
    tm = TM_EXPERT
    co = cnt_out.reshape(n_tok_tiles, SUBLANES, LANES)
    cnt = (co[:, 0, :N_EXPERTS].astype(jnp.int32) + SUBLANES - 1) // SUBLANES * SUBLANES
    off = co[:, 1, :N_EXPERTS].astype(jnp.int32)
    cum_end = jnp.cumsum(cnt, axis=0)
    cum = cum_end - cnt
    total = cum_end[-1]
    n_et = (total + tm - 1) // tm
    et_end = jnp.cumsum(n_et)
    n_act = et_end[-1:].astype(jnp.int32)
    n_tiles = -(-n_tok_tiles * TOK_BLOCK // tm) + N_EXPERTS
    j = jnp.arange(n_tiles, dtype=jnp.int32)
    tile_e = jnp.minimum(jnp.sum((et_end[None, :] <= j[:, None]).astype(jnp.int32), axis=1), N_EXPERTS - 1)
    pick_e = (tile_e[:, None] == jnp.arange(N_EXPERTS, dtype=jnp.int32)[None, :]).astype(F32)

    def per_tile(table):
        return jnp.dot(pick_e, table.astype(F32), precision=lax.Precision.HIGHEST).astype(jnp.int32)

    row0 = (j - per_tile(et_end - n_et)) * tm
    n_rows_tile = jnp.clip(per_tile(total) - row0, 0, tm)
    cum_e = per_tile(cum.T)
    cum_end_e = per_tile(cum_end.T)
    delta_e = per_tile((jnp.arange(n_tok_tiles, dtype=jnp.int32)[:, None] * TOK_BLOCK + off - cum).T)
    q = row0[:, None] + SUBLANES * jnp.arange(tm // SUBLANES, dtype=jnp.int32)[None, :]
    inside = jnp.logical_and(cum_e.T[:, :, None] <= q[None], q[None] < cum_end_e.T[:, :, None])
    src = q + jnp.sum(jnp.where(inside, delta_e.T[:, :, None], 0), axis=0)
    used = off[:, N_EXPERTS - 1] + cnt[:, N_EXPERTS - 1]
    first = jnp.concatenate([jnp.ones((1,), jnp.int32), (tile_e[1:] != tile_e[:-1]).astype(jnp.int32)])
    group = jnp.cumsum(first) - 1
    after = per_tile(et_end)
    next_e = jnp.where(after < n_act[0],
                       jnp.minimum(jnp.sum((et_end[None, :] <= after[:, None]).astype(jnp.int32), axis=1),
                                   N_EXPERTS - 1), -1)
    return (tile_e.astype(jnp.int32), n_rows_tile.astype(jnp.int32), n_act, src.reshape(-1).astype(jnp.int32),
            used.astype(jnp.int32), first, group.astype(jnp.int32), next_e.astype(jnp.int32), n_tiles)


def _expert_kernel(tile_e, n_rows_t, n_act, src_t, used_t, first_t, group_t, next_t,
                   xs_hbm, wgu_hbm, bgu_ref, wdn_hbm, bdn_ref, ys_hbm,
                   xin, yout, wgu_f, wdn_f, wgu_s, wdn_s, zeros, sem_in, sem_out, sem_zero, sem_wgu, sem_wdn,
                   *, tm, tok_block, n_tok_tiles, layer):
    j = pl.program_id(0)
    na = n_act[0]
    chunks = tm // SUBLANES

    def weight_copies(e, wslot):
        return (pltpu.make_async_copy(wgu_hbm.at[layer, e], wgu_f.at[wslot], sem_wgu.at[wslot]),
                pltpu.make_async_copy(wdn_hbm.at[layer, e], wdn_f.at[wslot], sem_wdn.at[wslot]))

    def gather(src, dst, size, slot):
        return pltpu.make_async_copy(xs_hbm.at[pl.ds(src, size)], xin.at[slot, pl.ds(dst, size)], sem_in.at[slot])

    def scatter(src, dst, size, slot):
        return pltpu.make_async_copy(yout.at[slot, pl.ds(dst, size)], ys_hbm.at[pl.ds(src, size)], sem_out.at[slot])

    def start_chunks(jj, slot, copy):
        def one(c, priority):
            src = pl.multiple_of(src_t[jj * chunks + c], SUBLANES)
            copy(src, pl.multiple_of(c * SUBLANES, SUBLANES), SUBLANES, slot).start(priority=priority)

        def body(c, carry):
            one(c, 0)
            return carry

        def body_unrolled(g, carry):
            for u in range(SUBLANES):
                one(g * SUBLANES + u, u % 2)
            return carry

        n = lax.shift_right_logical(n_rows_t[jj], 3)

        @pl.when(n == chunks)
        def _():
            lax.fori_loop(0, chunks // SUBLANES, body_unrolled, 0)

        @pl.when(n != chunks)
        def _():
            lax.fori_loop(0, n, body, 0)

    def wait_rows(jj, slot, copy):
        n = n_rows_t[jj]
        size = tm
        while size >= SUBLANES:
            @pl.when((n & size) != 0)
            def _(size=size):
                copy(0, 0, size, slot).wait()
            size //= 2

    slot = j % 2

    @pl.when(j == 0)
    def _():
        xin[...] = jnp.zeros_like(xin)
        start_chunks(0, 0, gather)
        zeros[...] = jnp.zeros_like(zeros)

        def clear_tail(i, copy_op):
            used = used_t[i]
            tail = tok_block - used
            size = ZERO_ROWS
            sizes = []
            while size >= SUBLANES:
                sizes.append(size)
                size //= 2
            for size in sizes:
                @pl.when((tail & size) != 0)
                def _(size=size):
                    at = pl.multiple_of(i * tok_block + used + (tail & ~(2 * size - 1)), SUBLANES)
                    copy_op(pltpu.make_async_copy(zeros.at[pl.ds(0, size)], ys_hbm.at[pl.ds(at, size)], sem_zero))

        def start_clear(i, c):
            clear_tail(i, lambda cp: cp.start())
            return c

        def wait_clear(i, c):
            clear_tail(i, lambda cp: cp.wait())
            return c

        lax.fori_loop(0, n_tok_tiles, start_clear, 0)
        lax.fori_loop(0, n_tok_tiles, wait_clear, 0)

    @pl.when(j + 1 < na)
    def _():
        start_chunks(j + 1, 1 - slot, gather)

    @pl.when(j < na)
    def _():
        e = tile_e[j]

        @pl.when(first_t[j] == 1)
        def _():
            wslot = group_t[j] % 2

            @pl.when(j == 0)
            def _():
                for cp in weight_copies(e, wslot):
                    cp.start()

            for cp in weight_copies(e, wslot):
                cp.wait()
            wgu_s[...] = wgu_f[wslot].astype(BF16)
            wdn_s[...] = wdn_f[wslot].astype(BF16)

            @pl.when(next_t[j] >= 0)
            def _():
                for cp in weight_copies(next_t[j], 1 - wslot):
                    cp.start(priority=1)

        wait_rows(j, slot, gather)

        def compute(rows):
            x = xin[slot, pl.ds(0, rows)]
            lane = lax.broadcasted_iota(jnp.int32, (rows, LANES), 1)
            p = jnp.sum(jnp.where(lane % N_EXPERTS == e, x[:, D_MODEL:], 0.0), axis=-1, keepdims=True)
            gu = (jnp.dot(x[:, :D_MODEL].astype(BF16), wgu_s[...], preferred_element_type=F32)
                  + bgu_ref[0, pl.ds(e, 1), :])
            f = gu.shape[1] // 2
            gate = jnp.minimum(gu[:, :f], SWIGLU_LIMIT)
            up = jnp.clip(gu[:, f:], -SWIGLU_LIMIT, SWIGLU_LIMIT)
            hid = (up + 1.0) * gate * jax.nn.sigmoid(SWIGLU_ALPHA * gate)
            y = (jnp.dot(hid.astype(BF16), wdn_s[...], preferred_element_type=F32)
                 + bdn_ref[0, pl.ds(e, 1), :])
            yout[slot, pl.ds(0, rows)] = y * p

        quarter = tm // 4
        for part in range(1, 5):
            @pl.when(jnp.logical_and(n_rows_t[j] > (part - 1) * quarter, n_rows_t[j] <= part * quarter))
            def _(part=part):
                compute(part * quarter)

        start_chunks(j, slot, scatter)

        @pl.when(j >= 1)
        def _():
            wait_rows(j - 1, 1 - slot, scatter)

        @pl.when(j == na - 1)
        def _():
            wait_rows(j, slot, scatter)


def _experts(xs, plan, layer, w_gu, b_gu, w_dn, b_dn):
    tile_e, n_rows_tile, n_act, src, used, first, group, next_e, n_tiles = plan
    tm = TM_EXPERT
    f2 = w_gu.shape[-1]

    in_specs = [pl.BlockSpec(memory_space=pl.ANY),
                pl.BlockSpec(memory_space=pl.ANY),
                pl.BlockSpec((1, N_EXPERTS, f2), lambda j, *_: (layer, 0, 0)),
                pl.BlockSpec(memory_space=pl.ANY),
                pl.BlockSpec((1, N_EXPERTS, D_MODEL), lambda j, *_: (layer, 0, 0))]
    grid_spec = pltpu.PrefetchScalarGridSpec(
        num_scalar_prefetch=8, grid=(n_tiles,), in_specs=in_specs,
        out_specs=pl.BlockSpec(memory_space=pl.ANY),
        scratch_shapes=[pltpu.VMEM((2, tm, XS_WIDTH), F32), pltpu.VMEM((2, tm, D_MODEL), F32),
                        pltpu.VMEM((2, D_MODEL, f2), F32), pltpu.VMEM((2, f2 // 2, D_MODEL), F32),
                        pltpu.VMEM((D_MODEL, f2), BF16), pltpu.VMEM((f2 // 2, D_MODEL), BF16),
                        pltpu.VMEM((ZERO_ROWS, D_MODEL), F32),
                        pltpu.SemaphoreType.DMA((2,)), pltpu.SemaphoreType.DMA((2,)), pltpu.SemaphoreType.DMA(()),
                        pltpu.SemaphoreType.DMA((2,)), pltpu.SemaphoreType.DMA((2,))])
    return pl.pallas_call(
        functools.partial(_expert_kernel, tm=tm, tok_block=TOK_BLOCK, n_tok_tiles=xs.shape[0] // TOK_BLOCK,
                          layer=layer),
        grid_spec=grid_spec,
        out_shape=jax.ShapeDtypeStruct((xs.shape[0], D_MODEL), F32),
        compiler_params=_params(1),
        name="moe_experts",
    )(tile_e, n_rows_tile, n_act, src, used, first, group, next_e,
      xs, w_gu, b_gu, w_dn, b_dn)


def _combine_kernel(ys_ref, pos_ref, h1_ref, g2_ref, l2g_ref, l2b_ref, o_ref):
    tm = pos_ref.shape[0]
    n_sorted = ys_ref.shape[0]
    pos = pos_ref[...]
    col = lax.broadcasted_iota(jnp.int32, (tm, tm), 1).astype(F32)
    y2 = None
    for r0 in range(0, n_sorted, tm):
        ys = ys_ref[r0:r0 + tm]
        sel_b = jnp.zeros((tm, tm), F32)
        for kk in range(TOP_K):
            sel_b = sel_b + (col == pos[:, kk:kk + 1] - float(r0)).astype(F32)
        sel_b = sel_b.astype(BF16)
        hi = ys.astype(BF16)
        rest = ys - hi.astype(F32)
        mid = rest.astype(BF16)
        lo = (rest - mid.astype(F32)).astype(BF16)
        part = (jnp.dot(sel_b, hi, preferred_element_type=F32) + jnp.dot(sel_b, mid, preferred_element_type=F32)
                + jnp.dot(sel_b, lo, preferred_element_type=F32))
        y2 = part if y2 is None else y2 + part
    o_ref[...] = _layer_norm(DEEPNORM_ALPHA * h1_ref[...] + g2_ref[0] * y2) * l2g_ref[...] + l2b_ref[...]


def _combine(ys, pos4, h1, mod3, mod_base, l2g, l2b, n_lat, seq):
    n_rows = h1.shape[0]
    tm = TM_POST
    tiles_per_seq = seq // tm
    n_lat_tiles = n_lat // tm
    n_groups_lat = n_lat // seq

    def group(i):
        return jnp.where(i < n_lat_tiles, i // tiles_per_seq, n_groups_lat)

    in_specs = [pl.BlockSpec((TOK_BLOCK, D_MODEL), lambda i: (i, 0)),
                pl.BlockSpec((tm, LANES), lambda i: (i, 0)),
                pl.BlockSpec((tm, D_MODEL), lambda i: (i, 0)),
                pl.BlockSpec((1, 1, D_MODEL), lambda i: (mod_base + group(i) * 6 + 5, 0, 0)),
                pl.BlockSpec((1, D_MODEL), lambda i: (0, 0)),
                pl.BlockSpec((1, D_MODEL), lambda i: (0, 0))]
    return pl.pallas_call(
        _combine_kernel,
        grid=(n_rows // tm,),
        in_specs=in_specs,
        out_specs=pl.BlockSpec((tm, D_MODEL), lambda i: (i, 0)),
        out_shape=jax.ShapeDtypeStruct((n_rows, D_MODEL), F32),
        compiler_params=_params(1),
        name="moe_combine",
    )(ys, pos4, h1, mod3, l2g.reshape(1, D_MODEL), l2b.reshape(1, D_MODEL))


def _block_diag(w):
    two, n, d, e = w.shape
    eye = jnp.eye(n, dtype=w.dtype)
    return (w[:, :, :, None, :] * eye[None, :, None, :, None]).reshape(two, n * d, n * e)


def kernel(x, c, ctx, c_ctx, w_mod, b_mod, w_in, b_in, na_rpb, w_proj_attn, w_proj_conv, w_proj_lru, sc_conv_w, lru_conv_w, lru_conv_b, lru_lambda, lru_w_r, lru_b_r, lru_w_i, lru_b_i, w_o, b_o, ln1_g, ln1_b, router_w, router_b, exp_w_gu, exp_b_gu, exp_w_dn, exp_b_dn, ln2_g, ln2_b):
    n_batch, seq, d = x.shape
    n_ctx = ctx.shape[1]
    n_lat = n_batch * seq
    n_all = n_lat + n_batch * n_ctx
    assert d == D_MODEL and n_batch + 1 <= SUBLANES

    cc = jnp.concatenate([c, c_ctx[None], jnp.zeros((SUBLANES - n_batch - 1, d), F32)], axis=0)
    mod = _modulation(cc, w_mod, b_mod)
    groups = n_batch + 1
    mod3 = mod.reshape(DEPTH, SUBLANES, 6, d)[:, :groups].reshape(DEPTH * groups * 6, 1, d)

    cos_t, sin_t = _make_rope(seq, TM_INPROJ)
    h = (x.reshape(n_lat, d), ctx.reshape(n_batch * n_ctx, d))
    w_in_bf = w_in.astype(BF16)

    for layer in range(DEPTH):
        last = layer == DEPTH - 1
        mod_base = layer * groups * 6
        q, k, v, sb, rest = _input_projection(h, mod3, mod_base, layer, w_in_bf, b_in, N_EARLY,
                                              cos_t, sin_t, n_lat, seq)
        sp = jax.nn.softplus(-lru_lambda[layer])
        zb, hf, hb, att = _token_mixers(
            q, k, v, _attention_bias(na_rpb[layer], seq), sb, rest, sc_conv_w[layer], lru_conv_w[layer],
            lru_conv_b[layer], sp, _block_diag(lru_w_r[layer]).astype(BF16), _block_diag(lru_w_i[layer]).astype(BF16),
            lru_b_r[layer], lru_b_i[layer], n_batch, seq, n_ctx)
        n_rows = n_lat if last else n_all
        rw_full = jnp.pad(router_w[layer], ((0, 0), (0, LANES - N_EXPERTS)))
        rw_hi = rw_full.astype(BF16)
        rw_pad = jnp.concatenate([rw_hi, (rw_full - rw_hi.astype(F32)).astype(BF16)], axis=1)
        rb_pad = jnp.concatenate([router_b[layer], jnp.full((LANES - N_EXPERTS,), NEG_BIG, F32)]).reshape(1, LANES)
        h1, xs, pos4, cnt_out = _post_mixer(
            h, att, zb, hf, hb, rest, mod3, mod_base, layer, w_in_bf, b_in,
            w_proj_attn[layer].astype(BF16), w_proj_conv[layer].astype(BF16), w_proj_lru[layer].astype(BF16),
            w_o[layer].astype(BF16), b_o[layer], ln1_g[layer], ln1_b[layer], rw_pad, rb_pad, n_rows, n_lat, seq)
        plan = _expert_plan(cnt_out, n_rows // TM_POST, n_rows)
        ys = _experts(xs, plan, layer, exp_w_gu, exp_b_gu, exp_w_dn, exp_b_dn)
        h = (_combine(ys, pos4, h1, mod3, mod_base, ln2_g[layer], ln2_b[layer], n_lat, seq),)
    return h[0].reshape(n_batch, seq, d)
```
